```python
import jax
import jax.numpy as jnp
from jax import lax
import numpy as np

D_MODEL = 1024
BATCH = 2
SEQ = 16384
DEPTH = 2

CTX_LEN = 256
GRID_W = 64
N_MIXERS = 4
MIX_WIDTH = D_MODEL
GROUP_WIDTH = MIX_WIDTH // N_MIXERS
GROUP_HEADS = 4
HEAD_DIM = GROUP_WIDTH // GROUP_HEADS
NA_ROWS = 8
NA_COLS = 16
ML_CHUNK = 64
ML_CONV = 5
MLA_Q_RANK = 256
MLA_KV_RANK = 128
MLA_NOPE = 64
MLA_ROPE = 32
MLA_V = 64
SWA_KV_HEADS = 2
SWA_WINDOW = 128
ATTN_BLOCK = 128
PEER_HEADS = 8
PEER_NKEYS = 128
PEER_EXPERTS = PEER_NKEYS * PEER_NKEYS
PEER_DKEY = 128
PEER_TOPK = 16
PEER_BLOCK = 128
ROPE_BASE = 10000.0
EPS = 1e-6
IN_SIZES = (GROUP_WIDTH, GROUP_WIDTH, GROUP_WIDTH,
            2 * GROUP_WIDTH, GROUP_WIDTH, GROUP_WIDTH, 4 * GROUP_HEADS,
            MLA_Q_RANK, MLA_KV_RANK, MLA_ROPE,
            GROUP_WIDTH, SWA_KV_HEADS * HEAD_DIM, SWA_KV_HEADS * HEAD_DIM)
IN_WIDTH = sum(IN_SIZES)
F32 = jnp.float32

kernel_name = 'hybrid_na_mlstm_mla_swa_peer_dit'


def rmsnorm(x, g):
    xf = x.astype(F32)
    y = xf * lax.rsqrt(jnp.mean(xf * xf, axis=-1, keepdims=True) + EPS) * g.astype(F32)
    return y.astype(x.dtype)


def heads(a, h):
    return a.reshape(a.shape[:-1] + (h, a.shape[-1] // h))


def split_cols(p):
    return jnp.split(p, np.cumsum(IN_SIZES)[:-1].tolist(), axis=-1)


def axial_angles(T, rot_dim):
    t = jnp.arange(T)
    row = (t // GRID_W).astype(F32)
    col = (t % GRID_W).astype(F32)
    half = rot_dim // 2
    inv = 1.0 / (ROPE_BASE ** (jnp.arange(0, half, 2, dtype=F32) / half))
    return row[:, None] * inv, col[:, None] * inv


def rope_1d(x, ang):
    cos = jnp.cos(ang)[None, :, None, :]
    sin = jnp.sin(ang)[None, :, None, :]
    x1, x2 = jnp.split(x.astype(F32), 2, axis=-1)
    return jnp.concatenate([x1 * cos - x2 * sin, x1 * sin + x2 * cos], axis=-1)


def rope_2d(x, angs):
    xr, xc = jnp.split(x, 2, axis=-1)
    return jnp.concatenate([rope_1d(xr, angs[0]), rope_1d(xc, angs[1])], axis=-1).astype(x.dtype)


def ctx_attn(q, k, v, scale, sink=None):
    rep = q.shape[2] // k.shape[2]
    k = jnp.repeat(k, rep, axis=2)
    v = jnp.repeat(v, rep, axis=2)
    s = jnp.einsum('bqhd,bkhd->bhqk', q, k).astype(F32) * scale
    nk = s.shape[-1]
    if sink is not None:
        s = jnp.concatenate([s, jnp.broadcast_to(sink.astype(F32)[None, :, None, None], s.shape[:-1] + (1,))], axis=-1)
    p = jax.nn.softmax(s, axis=-1)[..., :nk].astype(v.dtype)
    out = jnp.einsum('bhqk,bkhd->bqhd', p, v)
    return out.reshape(out.shape[:2] + (-1,))


def neighbourhood_attention(q, k, v, kc, vc, rpb):
    B, T, H, d = q.shape
    rows = T // GRID_W
    nr = min(NA_ROWS, rows)
    scale = d ** -0.5
    qg = jnp.swapaxes(q.reshape(B, rows, GRID_W, H, d), 0, 1)
    kg = k.reshape(B, rows, GRID_W, H, d)
    vg = v.reshape(B, rows, GRID_W, H, d)
    col_start = np.clip(np.arange(GRID_W) - NA_COLS // 2, 0, GRID_W - NA_COLS)
    col_idx = col_start[:, None] + np.arange(NA_COLS)[None, :]
    dc = col_idx - np.arange(GRID_W)[:, None] + (NA_COLS - 1)
    rpb_c = rpb.astype(F32)[:, :, dc]
    n_loc = nr * NA_COLS

    def row_block(args):
        qr, r = args
        rs = jnp.clip(r - nr // 2, 0, rows - nr)
        kr = lax.dynamic_slice_in_dim(kg, rs, nr, axis=1)[:, :, col_idx]
        vr = lax.dynamic_slice_in_dim(vg, rs, nr, axis=1)[:, :, col_idx]
        dr = rs + jnp.arange(nr) - r + (NA_ROWS - 1)
        bias = jnp.take(rpb_c, dr, axis=1)
        s_loc = jnp.einsum('bqhd,brqchd->bhqrc', qr, kr).astype(F32) * scale + jnp.transpose(bias, (0, 2, 1, 3))[None]
        s_ctx = jnp.einsum('bqhd,bkhd->bhqk', qr, kc).astype(F32) * scale
        p = jax.nn.softmax(jnp.concatenate([s_loc.reshape(B, H, GRID_W, n_loc), s_ctx], axis=-1), axis=-1).astype(v.dtype)
        p_loc = p[..., :n_loc].reshape(B, H, GRID_W, nr, NA_COLS)
        return (jnp.einsum('bhqrc,brqchd->bqhd', p_loc, vr)
                + jnp.einsum('bhqk,bkhd->bqhd', p[..., n_loc:], vc))

    out = lax.map(row_block, (qg, jnp.arange(rows)))
    return jnp.swapaxes(out, 0, 1).reshape(B, T, H * d)


def short_conv(a, w):
    T = a.shape[1]
    pad = w.shape[0] // 2
    ap = jnp.pad(a, ((0, 0), (pad, pad), (0, 0)))
    out = ap[:, :T] * w[0]
    for j in range(1, w.shape[0]):
        out = out + ap[:, j:j + T] * w[j]
    return out


def mlstm_scan(q, k, v, ig, lf, state):
    B, T, H, d = q.shape
    nc = T // ML_CHUNK

    def chunks(a):
        a = a.astype(F32).reshape((B, nc, ML_CHUNK) + a.shape[2:])
        return jnp.swapaxes(jnp.swapaxes(a, 0, 1), 2, 3)

    seen = jnp.tril(jnp.ones((ML_CHUNK, ML_CHUNK), dtype=bool))

    def step(carry, inp):
        C, n, m = carry
        qt, kt, vt, it, ft = inp
        b = jnp.cumsum(ft, axis=-1)
        d_log = jnp.where(seen, b[..., :, None] - b[..., None, :] + it[..., None, :], -jnp.inf)
        inter = b + m[..., None]
        m_t = jnp.maximum(inter, jnp.max(d_log, axis=-1))
        w = jnp.exp(d_log - m_t[..., None])
        a = jnp.exp(inter - m_t)
        s = jnp.einsum('bhtk,bhsk->bhts', qt, kt) * w
        num = jnp.einsum('bhts,bhsv->bhtv', s, vt) + a[..., None] * jnp.einsum('bhtk,bhkv->bhtv', qt, C)
        den = jnp.sum(s, axis=-1) + a * jnp.einsum('bhtk,bhk->bht', qt, n)
        h = num / jnp.maximum(jnp.abs(den), jnp.exp(-m_t))[..., None]
        g = b[..., -1:] - b + it
        m_new = jnp.maximum(b[..., -1] + m, jnp.max(g, axis=-1))
        wk = jnp.exp(g - m_new[..., None])
        decay = jnp.exp(b[..., -1] + m - m_new)
        C = decay[..., None, None] * C + jnp.einsum('bhs,bhsk,bhsv->bhkv', wk, kt, vt)
        n = decay[..., None] * n + jnp.einsum('bhs,bhsk->bhk', wk, kt)
        return (C, n, m_new), h

    state, h = lax.scan(step, state, (chunks(q), chunks(k), chunks(v), chunks(ig), chunks(lf)))
    h = jnp.swapaxes(jnp.swapaxes(h, 2, 3), 0, 1).reshape(B, T, H, d)
    return h.astype(v.dtype), state


def mlstm_prep(qk, v, gates, conv_w, gate_b):
    qk = jax.nn.silu(short_conv(qk, conv_w))
    q, k = jnp.split(qk, 2, axis=-1)
    g = (gates + gate_b).astype(F32)
    i_f, f_f, i_b, f_b = jnp.split(g, 4, axis=-1)
    return (heads(q, GROUP_HEADS) * HEAD_DIM ** -0.5, heads(k, GROUP_HEADS), heads(v, GROUP_HEADS),
            (i_f, jax.nn.log_sigmoid(f_f), i_b, jax.nn.log_sigmoid(f_b)))


def mlstm_mixer(lat, ctx, conv_w, gate_b):
    ql, kl, vl, gl = mlstm_prep(lat[0], lat[1], lat[2], conv_w, gate_b)
    qc, kc, vc, gc = mlstm_prep(ctx[0], ctx[1], ctx[2], conv_w, gate_b)
    B, _, H, d = ql.shape
    st0 = (jnp.zeros((B, H, d, d), F32), jnp.zeros((B, H, d), F32), jnp.zeros((B, H), F32))

    def rev(a):
        return a[:, ::-1]

    hc_f, st_f = mlstm_scan(qc, kc, vc, gc[0], gc[1], st0)
    hl_f, _ = mlstm_scan(ql, kl, vl, gl[0], gl[1], st_f)
    hc_b, st_b = mlstm_scan(rev(qc), rev(kc), rev(vc), rev(gc[2]), rev(gc[3]), st0)
    hl_b, _ = mlstm_scan(rev(ql), rev(kl), rev(vl), rev(gl[2]), rev(gl[3]), st_b)
    return hl_f + rev(hl_b), hc_f + rev(hc_b)


def mla_project(cq, ckv, kr, q_norm, w_uq, kv_norm, w_ukv, angs):
    q = heads(rmsnorm(cq, q_norm) @ w_uq, GROUP_HEADS)
    kv = heads(rmsnorm(ckv, kv_norm) @ w_ukv, GROUP_HEADS)
    q_nope, q_rope = q[..., :MLA_NOPE], q[..., MLA_NOPE:]
    k_nope, v = kv[..., :MLA_NOPE], kv[..., MLA_NOPE:]
    k_rope = kr[:, :, None, :]
    if angs is not None:
        q_rope = rope_2d(q_rope, angs)
        k_rope = rope_2d(k_rope, angs)
    k_rope = jnp.broadcast_to(k_rope, k_nope.shape[:-1] + (MLA_ROPE,))
    return (jnp.concatenate([q_nope, q_rope], axis=-1), jnp.concatenate([k_nope, k_rope], axis=-1), v)


def block_dense_attention(q, k_all, v_all, scale):
    B, T, H, dq = q.shape
    nb = T // ATTN_BLOCK
    qb = jnp.swapaxes(q.reshape(B, nb, ATTN_BLOCK, H, dq), 0, 1)

    def one(qblk):
        s = jnp.einsum('bqhd,bkhd->bhqk', qblk, k_all).astype(F32) * scale
        p = jax.nn.softmax(s, axis=-1).astype(v_all.dtype)
        return jnp.einsum('bhqk,bkhd->bqhd', p, v_all)

    out = lax.map(one, qb)
    return jnp.swapaxes(out, 0, 1).reshape(B, T, -1)


def window_attention(q, k, v, kc, vc, sink):
    B, T, H, d = q.shape
    KVH = k.shape[2]
    G = H // KVH
    nb = T // ATTN_BLOCK
    span = ATTN_BLOCK + 2 * SWA_WINDOW
    n_ctx = kc.shape[1]
    scale = d ** -0.5
    padw = ((0, 0), (SWA_WINDOW, SWA_WINDOW), (0, 0), (0, 0))
    kp = jnp.pad(k, padw)
    vp = jnp.pad(v, padw)
    start = jnp.arange(nb) * ATTN_BLOCK
    idx = start[:, None] + jnp.arange(span)[None, :]
    kb = kp[:, idx]
    vb = vp[:, idx]
    key_pos = idx - SWA_WINDOW
    q_pos = start[:, None] + jnp.arange(ATTN_BLOCK)[None, :]
    mask = ((jnp.abs(q_pos[:, :, None] - key_pos[:, None, :]) <= SWA_WINDOW)
            & (key_pos >= 0)[:, None, :] & (key_pos < T)[:, None, :])
    qb = q.reshape(B, nb, ATTN_BLOCK, KVH, G, d)
    s_loc = jnp.einsum('bnqhgd,bnkhd->bnhgqk', qb, kb).astype(F32) * scale
    s_loc = jnp.where(mask[None, :, None, None], s_loc, -jnp.inf)
    s_ctx = jnp.einsum('bnqhgd,bkhd->bnhgqk', qb, kc).astype(F32) * scale
    s_sink = jnp.broadcast_to(sink.astype(F32).reshape(1, 1, KVH, G, 1, 1), s_loc.shape[:-1] + (1,))
    p = jax.nn.softmax(jnp.concatenate([s_loc, s_ctx, s_sink], axis=-1), axis=-1).astype(v.dtype)
    out = (jnp.einsum('bnhgqk,bnkhd->bnqhgd', p[..., :span], vb)
           + jnp.einsum('bnhgqk,bkhd->bnqhgd', p[..., span:span + n_ctx], vc))
    return out.reshape(B, T, H * d)


def peer_ffn(h, wq, sub_keys, u, v):
    N, D = h.shape
    hb = h.reshape(N // PEER_BLOCK, PEER_BLOCK, D)

    def block(xb):
        qry = (xb @ wq).reshape(PEER_BLOCK, PEER_HEADS, 2, PEER_DKEY)
        s = jnp.einsum('thpk,hpnk->thpn', qry, sub_keys).astype(F32)
        sv, si = lax.top_k(s, PEER_TOPK)
        cand_s = (sv[:, :, 0, :, None] + sv[:, :, 1, None, :]).reshape(PEER_BLOCK, PEER_HEADS, PEER_TOPK * PEER_TOPK)
        cand_i = (si[:, :, 0, :, None] * PEER_NKEYS + si[:, :, 1, None, :]).reshape(PEER_BLOCK, PEER_HEADS, PEER_TOPK * PEER_TOPK)
        fs, fpos = lax.top_k(cand_s, PEER_TOPK)
        eidx = jnp.take_along_axis(cand_i, fpos, axis=-1)
        gate = jax.nn.softmax(fs, axis=-1)
        act = jax.nn.gelu(jnp.einsum('td,thkd->thk', xb, u[eidx]).astype(F32), approximate=False)
        w = (gate * act).astype(xb.dtype)
        return jnp.einsum('thk,thkd->td', w, v[eidx])

    return lax.map(block, hb).reshape(N, D)


def hybrid_layer(x, xc, c, c_ctx, need_ctx, angs_mla, angs_swa,
                 norm1_g, norm2_g, w_ada, b_ada, w_in, na_rpb, ml_conv, ml_gate_b,
                 mla_q_norm, mla_w_uq, mla_kv_norm, mla_w_ukv, swa_sink, w_out,
                 peer_wq, peer_keys, peer_u, peer_v):
    B, T, D = x.shape
    H = GROUP_HEADS
    sh1, sc1, g1, sh2, sc2, g2 = jnp.split((jax.nn.silu(c) @ w_ada + b_ada)[:, None, :], 6, axis=-1)
    sh1c, sc1c, g1c, sh2c, sc2c, g2c = jnp.split(jax.nn.silu(c_ctx) @ w_ada + b_ada, 6, axis=-1)
    h = rmsnorm(x, norm1_g) * (1.0 + sc1) + sh1
    hc = rmsnorm(xc, norm1_g) * (1.0 + sc1c) + sh1c
    (na_q, na_k, na_v, ml_qk, ml_v, ml_o, ml_g,
     mla_cq, mla_ckv, mla_kr, sw_q, sw_k, sw_v) = split_cols(h @ w_in)
    (na_qc, na_kc, na_vc, ml_qkc, ml_vc, ml_oc, ml_gc,
     mla_cqc, mla_ckvc, mla_krc, sw_qc, sw_kc, sw_vc) = split_cols(hc @ w_in)
    attn_scale = HEAD_DIM ** -0.5
    mla_scale = (MLA_NOPE + MLA_ROPE) ** -0.5
    kc_a, vc_a = heads(na_kc, H), heads(na_vc, H)
    y_a = neighbourhood_attention(heads(na_q, H), heads(na_k, H), heads(na_v, H), kc_a, vc_a, na_rpb)
    h_lat, h_ctx = mlstm_mixer((ml_qk, ml_v, ml_g), (ml_qkc, ml_vc, ml_gc), ml_conv, ml_gate_b)
    y_b = h_lat.reshape(B, T, GROUP_WIDTH) * jax.nn.sigmoid(ml_o)
    q_m, k_m, v_m = mla_project(mla_cq, mla_ckv, mla_kr, mla_q_norm, mla_w_uq, mla_kv_norm, mla_w_ukv, angs_mla)
    qc_m, kc_m, vc_m = mla_project(mla_cqc, mla_ckvc, mla_krc, mla_q_norm, mla_w_uq, mla_kv_norm, mla_w_ukv, None)
    y_c = block_dense_attention(q_m, jnp.concatenate([kc_m, k_m], axis=1), jnp.concatenate([vc_m, v_m], axis=1), mla_scale)
    kc_d, vc_d = heads(sw_kc, SWA_KV_HEADS), heads(sw_vc, SWA_KV_HEADS)
    y_d = window_attention(rope_2d(heads(sw_q, H), angs_swa), rope_2d(heads(sw_k, SWA_KV_HEADS), angs_swa),
                           heads(sw_v, SWA_KV_HEADS), kc_d, vc_d, swa_sink)
    x = x + g1 * (jnp.concatenate([y_a, y_b, y_c, y_d], axis=-1) @ w_out)
    h2 = rmsnorm(x, norm2_g) * (1.0 + sc2) + sh2
    if need_ctx:
        Tc = xc.shape[1]
        y_ctx = jnp.concatenate([
            ctx_attn(heads(na_qc, H), kc_a, vc_a, attn_scale),
            h_ctx.reshape(B, Tc, GROUP_WIDTH) * jax.nn.sigmoid(ml_oc),
            ctx_attn(qc_m, kc_m, vc_m, mla_scale),
            ctx_attn(heads(sw_qc, H), kc_d, vc_d, attn_scale, swa_sink)], axis=-1)
        xc = xc + g1c * (y_ctx @ w_out)
        h2c = rmsnorm(xc, norm2_g) * (1.0 + sc2c) + sh2c
        f = peer_ffn(jnp.concatenate([h2.reshape(B * T, D), h2c.reshape(B * Tc, D)], axis=0),
                     peer_wq, peer_keys, peer_u, peer_v)
        x = x + g2 * f[:B * T].reshape(B, T, D)
        xc = xc + g2c * f[B * T:].reshape(B, Tc, D)
        return x, xc
    x = x + g2 * peer_ffn(h2.reshape(B * T, D), peer_wq, peer_keys, peer_u, peer_v).reshape(B, T, D)
    return x, None


def setup_inputs(seed: int = 0) -> dict:
    key = jax.random.key(seed)
    ks = iter(jax.random.split(key, 32))

    def nrm(shape, std):
        return jax.random.normal(next(ks), shape, F32) * std

    L, D, H = DEPTH, D_MODEL, GROUP_HEADS
    f_base = jnp.linspace(3.0, 6.0, H, dtype=F32)
    zero_h = jnp.zeros((H,), F32)
    gate_base = jnp.stack([zero_h, f_base, zero_h, f_base])
    return {
        'x': nrm((BATCH, SEQ, D), 1.0),
        'c': nrm((BATCH, D), 1.0),
        'ctx': nrm((BATCH, CTX_LEN, D), 1.0),
        'c_ctx': nrm((D,), 1.0),
        'norm1_g': 1.0 + nrm((L, D), 0.02),
        'norm2_g': 1.0 + nrm((L, D), 0.02),
        'w_ada': nrm((L, D, 6 * D), 0.5 * D ** -0.5),
        'b_ada': nrm((L, 6 * D), 0.02),
        'w_in': nrm((L, D, IN_WIDTH), D ** -0.5),
        'na_rpb': nrm((L, H, 2 * NA_ROWS - 1, 2 * NA_COLS - 1), 0.5),
        'ml_conv': nrm((L, ML_CONV, 2 * GROUP_WIDTH), ML_CONV ** -0.5),
        'ml_gate_b': (gate_base[None] + nrm((L, 4, H), 0.1)).reshape(L, 4 * H),
        'mla_q_norm': 1.0 + nrm((L, MLA_Q_RANK), 0.02),
        'mla_w_uq': nrm((L, MLA_Q_RANK, H * (MLA_NOPE + MLA_ROPE)), MLA_Q_RANK ** -0.5),
        'mla_kv_norm': 1.0 + nrm((L, MLA_KV_RANK), 0.02),
        'mla_w_ukv': nrm((L, MLA_KV_RANK, H * (MLA_NOPE + MLA_V)), MLA_KV_RANK ** -0.5),
        'swa_sink': nrm((L, H), 0.5),
        'w_out': nrm((L, MIX_WIDTH, D), MIX_WIDTH ** -0.5),
        'peer_wq': nrm((L, D, PEER_HEADS * 2 * PEER_DKEY), D ** -0.5),
        'peer_keys': nrm((L, PEER_HEADS, 2, PEER_NKEYS, PEER_DKEY), PEER_DKEY ** -0.5),
        'peer_u': nrm((L, PEER_EXPERTS, D), D ** -0.5),
        'peer_v': nrm((L, PEER_EXPERTS, D), 0.5),
        'final_norm_g': 1.0 + nrm((D,), 0.02),
    }


def reference(x, c, ctx, c_ctx, norm1_g, norm2_g, w_ada, b_ada, w_in, na_rpb, ml_conv, ml_gate_b,
              mla_q_norm, mla_w_uq, mla_kv_norm, mla_w_ukv, swa_sink, w_out,
              peer_wq, peer_keys, peer_u, peer_v, final_norm_g):
    T = x.shape[1]
    angs_mla = axial_angles(T, MLA_ROPE)
    angs_swa = axial_angles(T, HEAD_DIM)
    xc = ctx
    for l in range(DEPTH):
        x, xc = hybrid_layer(x, xc, c, c_ctx, l < DEPTH - 1, angs_mla, angs_swa,
                             norm1_g[l], norm2_g[l], w_ada[l], b_ada[l], w_in[l], na_rpb[l],
                             ml_conv[l], ml_gate_b[l], mla_q_norm[l], mla_w_uq[l], mla_kv_norm[l],
                             mla_w_ukv[l], swa_sink[l], w_out[l], peer_wq[l], peer_keys[l],
                             peer_u[l], peer_v[l])
    return rmsnorm(x, final_norm_g)
```

```python
import functools

import jax
import jax.numpy as jnp
from jax import lax
import numpy as np
from jax.experimental import pallas as pl
from jax.experimental.pallas import tpu as pltpu

D_MODEL = 1024
BATCH = 2
SEQ = 16384
DEPTH = 2

CTX_LEN = 256
GRID_W = 64
N_MIXERS = 4
MIX_WIDTH = D_MODEL
GROUP_WIDTH = MIX_WIDTH // N_MIXERS
GROUP_HEADS = 4
HEAD_DIM = GROUP_WIDTH // GROUP_HEADS
NA_ROWS = 8
NA_COLS = 16
ML_CHUNK = 64
ML_CONV = 5
MLA_Q_RANK = 256
MLA_KV_RANK = 128
MLA_NOPE = 64
MLA_ROPE = 32
MLA_V = 64
SWA_KV_HEADS = 2
SWA_WINDOW = 128
ATTN_BLOCK = 128
PEER_HEADS = 8
PEER_NKEYS = 128
PEER_EXPERTS = PEER_NKEYS * PEER_NKEYS
PEER_DKEY = 128
PEER_TOPK = 16
PEER_BLOCK = 128
ROPE_BASE = 10000.0
EPS = 1e-6
IN_SIZES = (GROUP_WIDTH, GROUP_WIDTH, GROUP_WIDTH,
            2 * GROUP_WIDTH, GROUP_WIDTH, GROUP_WIDTH, 4 * GROUP_HEADS,
            MLA_Q_RANK, MLA_KV_RANK, MLA_ROPE,
            GROUP_WIDTH, SWA_KV_HEADS * HEAD_DIM, SWA_KV_HEADS * HEAD_DIM)
IN_WIDTH = sum(IN_SIZES)
F32 = jnp.float32


def rmsnorm(x, g):
    xf = x.astype(F32)
    y = xf * lax.rsqrt(jnp.mean(xf * xf, axis=-1, keepdims=True) + EPS) * g.astype(F32)
    return y.astype(x.dtype)


def heads(a, h):
    return a.reshape(a.shape[:-1] + (h, a.shape[-1] // h))


def split_cols(p):
    return jnp.split(p, np.cumsum(IN_SIZES)[:-1].tolist(), axis=-1)


def axial_angles(T, rot_dim):
    t = jnp.arange(T)
    row = (t // GRID_W).astype(F32)
    col = (t % GRID_W).astype(F32)
    half = rot_dim // 2
    inv = 1.0 / (ROPE_BASE ** (jnp.arange(0, half, 2, dtype=F32) / half))
    return row[:, None] * inv, col[:, None] * inv


def rope_1d(x, ang):
    cos = jnp.cos(ang)[None, :, None, :]
    sin = jnp.sin(ang)[None, :, None, :]
    x1, x2 = jnp.split(x.astype(F32), 2, axis=-1)
    return jnp.concatenate([x1 * cos - x2 * sin, x1 * sin + x2 * cos], axis=-1)


def rope_2d(x, angs):
    xr, xc = jnp.split(x, 2, axis=-1)
    return jnp.concatenate([rope_1d(xr, angs[0]), rope_1d(xc, angs[1])], axis=-1).astype(x.dtype)


def ctx_attn(q, k, v, scale, sink=None):
    rep = q.shape[2] // k.shape[2]
    k = jnp.repeat(k, rep, axis=2)
    v = jnp.repeat(v, rep, axis=2)
    s = jnp.einsum('bqhd,bkhd->bhqk', q, k).astype(F32) * scale
    nk = s.shape[-1]
    if sink is not None:
        s = jnp.concatenate([s, jnp.broadcast_to(sink.astype(F32)[None, :, None, None], s.shape[:-1] + (1,))], axis=-1)
    p = jax.nn.softmax(s, axis=-1)[..., :nk].astype(v.dtype)
    out = jnp.einsum('bhqk,bkhd->bqhd', p, v)
    return out.reshape(out.shape[:2] + (-1,))


def neighbourhood_attention(q, k, v, kc, vc, rpb):
    B, T, H, d = q.shape
    rows = T // GRID_W
    nr = min(NA_ROWS, rows)
    scale = d ** -0.5
    qg = jnp.swapaxes(q.reshape(B, rows, GRID_W, H, d), 0, 1)
    kg = k.reshape(B, rows, GRID_W, H, d)
    vg = v.reshape(B, rows, GRID_W, H, d)
    col_start = np.clip(np.arange(GRID_W) - NA_COLS // 2, 0, GRID_W - NA_COLS)
    col_idx = col_start[:, None] + np.arange(NA_COLS)[None, :]
    dc = col_idx - np.arange(GRID_W)[:, None] + (NA_COLS - 1)
    rpb_c = rpb.astype(F32)[:, :, dc]
    n_loc = nr * NA_COLS

    def row_block(args):
        qr, r = args
        rs = jnp.clip(r - nr // 2, 0, rows - nr)
        kr = lax.dynamic_slice_in_dim(kg, rs, nr, axis=1)[:, :, col_idx]
        vr = lax.dynamic_slice_in_dim(vg, rs, nr, axis=1)[:, :, col_idx]
        dr = rs + jnp.arange(nr) - r + (NA_ROWS - 1)
        bias = jnp.take(rpb_c, dr, axis=1)
        s_loc = jnp.einsum('bqhd,brqchd->bhqrc', qr, kr).astype(F32) * scale + jnp.transpose(bias, (0, 2, 1, 3))[None]
        s_ctx = jnp.einsum('bqhd,bkhd->bhqk', qr, kc).astype(F32) * scale
        p = jax.nn.softmax(jnp.concatenate([s_loc.reshape(B, H, GRID_W, n_loc), s_ctx], axis=-1), axis=-1).astype(v.dtype)
        p_loc = p[..., :n_loc].reshape(B, H, GRID_W, nr, NA_COLS)
        return (jnp.einsum('bhqrc,brqchd->bqhd', p_loc, vr)
                + jnp.einsum('bhqk,bkhd->bqhd', p[..., n_loc:], vc))

    out = lax.map(row_block, (qg, jnp.arange(rows)))
    return jnp.swapaxes(out, 0, 1).reshape(B, T, H * d)


def short_conv(a, w):
    T = a.shape[1]
    pad = w.shape[0] // 2
    ap = jnp.pad(a, ((0, 0), (pad, pad), (0, 0)))
    out = ap[:, :T] * w[0]
    for j in range(1, w.shape[0]):
        out = out + ap[:, j:j + T] * w[j]
    return out


def mlstm_scan(q, k, v, ig, lf, state):
    B, T, H, d = q.shape
    nc = T // ML_CHUNK

    def chunks(a):
        a = a.astype(F32).reshape((B, nc, ML_CHUNK) + a.shape[2:])
        return jnp.swapaxes(jnp.swapaxes(a, 0, 1), 2, 3)

    seen = jnp.tril(jnp.ones((ML_CHUNK, ML_CHUNK), dtype=bool))

    def step(carry, inp):
        C, n, m = carry
        qt, kt, vt, it, ft = inp
        b = jnp.cumsum(ft, axis=-1)
        d_log = jnp.where(seen, b[..., :, None] - b[..., None, :] + it[..., None, :], -jnp.inf)
        inter = b + m[..., None]
        m_t = jnp.maximum(inter, jnp.max(d_log, axis=-1))
        w = jnp.exp(d_log - m_t[..., None])
        a = jnp.exp(inter - m_t)
        s = jnp.einsum('bhtk,bhsk->bhts', qt, kt) * w
        num = jnp.einsum('bhts,bhsv->bhtv', s, vt) + a[..., None] * jnp.einsum('bhtk,bhkv->bhtv', qt, C)
        den = jnp.sum(s, axis=-1) + a * jnp.einsum('bhtk,bhk->bht', qt, n)
        h = num / jnp.maximum(jnp.abs(den), jnp.exp(-m_t))[..., None]
        g = b[..., -1:] - b + it
        m_new = jnp.maximum(b[..., -1] + m, jnp.max(g, axis=-1))
        wk = jnp.exp(g - m_new[..., None])
        decay = jnp.exp(b[..., -1] + m - m_new)
        C = decay[..., None, None] * C + jnp.einsum('bhs,bhsk,bhsv->bhkv', wk, kt, vt)
        n = decay[..., None] * n + jnp.einsum('bhs,bhsk->bhk', wk, kt)
        return (C, n, m_new), h

    state, h = lax.scan(step, state, (chunks(q), chunks(k), chunks(v), chunks(ig), chunks(lf)))
    h = jnp.swapaxes(jnp.swapaxes(h, 2, 3), 0, 1).reshape(B, T, H, d)
    return h.astype(v.dtype), state


def mlstm_prep(qk, v, gates, conv_w, gate_b):
    qk = jax.nn.silu(short_conv(qk, conv_w))
    q, k = jnp.split(qk, 2, axis=-1)
    g = (gates + gate_b).astype(F32)
    i_f, f_f, i_b, f_b = jnp.split(g, 4, axis=-1)
    return (heads(q, GROUP_HEADS) * HEAD_DIM ** -0.5, heads(k, GROUP_HEADS), heads(v, GROUP_HEADS),
            (i_f, jax.nn.log_sigmoid(f_f), i_b, jax.nn.log_sigmoid(f_b)))


def mlstm_mixer(lat, ctx, conv_w, gate_b):
    ql, kl, vl, gl = mlstm_prep(lat[0], lat[1], lat[2], conv_w, gate_b)
    qc, kc, vc, gc = mlstm_prep(ctx[0], ctx[1], ctx[2], conv_w, gate_b)
    B, _, H, d = ql.shape
    st0 = (jnp.zeros((B, H, d, d), F32), jnp.zeros((B, H, d), F32), jnp.zeros((B, H), F32))

    def rev(a):
        return a[:, ::-1]

    hc_f, st_f = mlstm_scan(qc, kc, vc, gc[0], gc[1], st0)
    hl_f, _ = mlstm_scan(ql, kl, vl, gl[0], gl[1], st_f)
    hc_b, st_b = mlstm_scan(rev(qc), rev(kc), rev(vc), rev(gc[2]), rev(gc[3]), st0)
    hl_b, _ = mlstm_scan(rev(ql), rev(kl), rev(vl), rev(gl[2]), rev(gl[3]), st_b)
    return hl_f + rev(hl_b), hc_f + rev(hc_b)


def mla_project(cq, ckv, kr, q_norm, w_uq, kv_norm, w_ukv, angs):
    q = heads(rmsnorm(cq, q_norm) @ w_uq, GROUP_HEADS)
    kv = heads(rmsnorm(ckv, kv_norm) @ w_ukv, GROUP_HEADS)
    q_nope, q_rope = q[..., :MLA_NOPE], q[..., MLA_NOPE:]
    k_nope, v = kv[..., :MLA_NOPE], kv[..., MLA_NOPE:]
    k_rope = kr[:, :, None, :]
    if angs is not None:
        q_rope = rope_2d(q_rope, angs)
        k_rope = rope_2d(k_rope, angs)
    k_rope = jnp.broadcast_to(k_rope, k_nope.shape[:-1] + (MLA_ROPE,))
    return (jnp.concatenate([q_nope, q_rope], axis=-1), jnp.concatenate([k_nope, k_rope], axis=-1), v)


def block_dense_attention(q, k_all, v_all, scale):
    B, T, H, dq = q.shape
    nb = T // ATTN_BLOCK
    qb = jnp.swapaxes(q.reshape(B, nb, ATTN_BLOCK, H, dq), 0, 1)

    def one(qblk):
        s = jnp.einsum('bqhd,bkhd->bhqk', qblk, k_all).astype(F32) * scale
        p = jax.nn.softmax(s, axis=-1).astype(v_all.dtype)
        return jnp.einsum('bhqk,bkhd->bqhd', p, v_all)

    out = lax.map(one, qb)
    return jnp.swapaxes(out, 0, 1).reshape(B, T, -1)


def window_attention(q, k, v, kc, vc, sink):
    B, T, H, d = q.shape
    KVH = k.shape[2]
    G = H // KVH
    nb = T // ATTN_BLOCK
    span = ATTN_BLOCK + 2 * SWA_WINDOW
    n_ctx = kc.shape[1]
    scale = d ** -0.5
    padw = ((0, 0), (SWA_WINDOW, SWA_WINDOW), (0, 0), (0, 0))
    kp = jnp.pad(k, padw)
    vp = jnp.pad(v, padw)
    start = jnp.arange(nb) * ATTN_BLOCK
    idx = start[:, None] + jnp.arange(span)[None, :]
    kb = kp[:, idx]
    vb = vp[:, idx]
    key_pos = idx - SWA_WINDOW
    q_pos = start[:, None] + jnp.arange(ATTN_BLOCK)[None, :]
    mask = ((jnp.abs(q_pos[:, :, None] - key_pos[:, None, :]) <= SWA_WINDOW)
            & (key_pos >= 0)[:, None, :] & (key_pos < T)[:, None, :])
    qb = q.reshape(B, nb, ATTN_BLOCK, KVH, G, d)
    s_loc = jnp.einsum('bnqhgd,bnkhd->bnhgqk', qb, kb).astype(F32) * scale
    s_loc = jnp.where(mask[None, :, None, None], s_loc, -jnp.inf)
    s_ctx = jnp.einsum('bnqhgd,bkhd->bnhgqk', qb, kc).astype(F32) * scale
    s_sink = jnp.broadcast_to(sink.astype(F32).reshape(1, 1, KVH, G, 1, 1), s_loc.shape[:-1] + (1,))
    p = jax.nn.softmax(jnp.concatenate([s_loc, s_ctx, s_sink], axis=-1), axis=-1).astype(v.dtype)
    out = (jnp.einsum('bnhgqk,bnkhd->bnqhgd', p[..., :span], vb)
           + jnp.einsum('bnhgqk,bkhd->bnqhgd', p[..., span:span + n_ctx], vc))
    return out.reshape(B, T, H * d)


def peer_ffn(h, wq, sub_keys, u, v):
    N, D = h.shape
    hb = h.reshape(N // PEER_BLOCK, PEER_BLOCK, D)

    def block(xb):
        qry = (xb @ wq).reshape(PEER_BLOCK, PEER_HEADS, 2, PEER_DKEY)
        s = jnp.einsum('thpk,hpnk->thpn', qry, sub_keys).astype(F32)
        sv, si = lax.top_k(s, PEER_TOPK)
        cand_s = (sv[:, :, 0, :, None] + sv[:, :, 1, None, :]).reshape(PEER_BLOCK, PEER_HEADS, PEER_TOPK * PEER_TOPK)
        cand_i = (si[:, :, 0, :, None] * PEER_NKEYS + si[:, :, 1, None, :]).reshape(PEER_BLOCK, PEER_HEADS, PEER_TOPK * PEER_TOPK)
        fs, fpos = lax.top_k(cand_s, PEER_TOPK)
        eidx = jnp.take_along_axis(cand_i, fpos, axis=-1)
        gate = jax.nn.softmax(fs, axis=-1)
        act = jax.nn.gelu(jnp.einsum('td,thkd->thk', xb, u[eidx]).astype(F32), approximate=False)
        w = (gate * act).astype(xb.dtype)
        return jnp.einsum('thk,thkd->td', w, v[eidx])

    return lax.map(block, hb).reshape(N, D)


def hybrid_layer(x, xc, c, c_ctx, need_ctx, angs_mla, angs_swa,
                 norm1_g, norm2_g, w_ada, b_ada, w_in, na_rpb, ml_conv, ml_gate_b,
                 mla_q_norm, mla_w_uq, mla_kv_norm, mla_w_ukv, swa_sink, w_out,
                 peer_wq, peer_keys, peer_u, peer_v):
    B, T, D = x.shape
    H = GROUP_HEADS
    sh1, sc1, g1, sh2, sc2, g2 = jnp.split((jax.nn.silu(c) @ w_ada + b_ada)[:, None, :], 6, axis=-1)
    sh1c, sc1c, g1c, sh2c, sc2c, g2c = jnp.split(jax.nn.silu(c_ctx) @ w_ada + b_ada, 6, axis=-1)
    h = rmsnorm(x, norm1_g) * (1.0 + sc1) + sh1
    hc = rmsnorm(xc, norm1_g) * (1.0 + sc1c) + sh1c
    (na_q, na_k, na_v, ml_qk, ml_v, ml_o, ml_g,
     mla_cq, mla_ckv, mla_kr, sw_q, sw_k, sw_v) = split_cols(h @ w_in)
    (na_qc, na_kc, na_vc, ml_qkc, ml_vc, ml_oc, ml_gc,
     mla_cqc, mla_ckvc, mla_krc, sw_qc, sw_kc, sw_vc) = split_cols(hc @ w_in)
    attn_scale = HEAD_DIM ** -0.5
    mla_scale = (MLA_NOPE + MLA_ROPE) ** -0.5
    kc_a, vc_a = heads(na_kc, H), heads(na_vc, H)
    y_a = neighbourhood_attention(heads(na_q, H), heads(na_k, H), heads(na_v, H), kc_a, vc_a, na_rpb)
    h_lat, h_ctx = mlstm_mixer((ml_qk, ml_v, ml_g), (ml_qkc, ml_vc, ml_gc), ml_conv, ml_gate_b)
    y_b = h_lat.reshape(B, T, GROUP_WIDTH) * jax.nn.sigmoid(ml_o)
    q_m, k_m, v_m = mla_project(mla_cq, mla_ckv, mla_kr, mla_q_norm, mla_w_uq, mla_kv_norm, mla_w_ukv, angs_mla)
    qc_m, kc_m, vc_m = mla_project(mla_cqc, mla_ckvc, mla_krc, mla_q_norm, mla_w_uq, mla_kv_norm, mla_w_ukv, None)
    y_c = block_dense_attention(q_m, jnp.concatenate([kc_m, k_m], axis=1), jnp.concatenate([vc_m, v_m], axis=1), mla_scale)
    kc_d, vc_d = heads(sw_kc, SWA_KV_HEADS), heads(sw_vc, SWA_KV_HEADS)
    y_d = window_attention(rope_2d(heads(sw_q, H), angs_swa), rope_2d(heads(sw_k, SWA_KV_HEADS), angs_swa),
                           heads(sw_v, SWA_KV_HEADS), kc_d, vc_d, swa_sink)
    x = x + g1 * (jnp.concatenate([y_a, y_b, y_c, y_d], axis=-1) @ w_out)
    h2 = rmsnorm(x, norm2_g) * (1.0 + sc2) + sh2
    if need_ctx:
        Tc = xc.shape[1]
        y_ctx = jnp.concatenate([
            ctx_attn(heads(na_qc, H), kc_a, vc_a, attn_scale),
            h_ctx.reshape(B, Tc, GROUP_WIDTH) * jax.nn.sigmoid(ml_oc),
            ctx_attn(qc_m, kc_m, vc_m, mla_scale),
            ctx_attn(heads(sw_qc, H), kc_d, vc_d, attn_scale, swa_sink)], axis=-1)
        xc = xc + g1c * (y_ctx @ w_out)
        h2c = rmsnorm(xc, norm2_g) * (1.0 + sc2c) + sh2c
        f = peer_ffn(jnp.concatenate([h2.reshape(B * T, D), h2c.reshape(B * Tc, D)], axis=0),
                     peer_wq, peer_keys, peer_u, peer_v)
        x = x + g2 * f[:B * T].reshape(B, T, D)
        xc = xc + g2c * f[B * T:].reshape(B, Tc, D)
        return x, xc
    x = x + g2 * peer_ffn(h2.reshape(B * T, D), peer_wq, peer_keys, peer_u, peer_v).reshape(B, T, D)
    return x, None


def _final_rmsnorm_kernel(x_ref, g_ref, o_ref):
    x = x_ref[...]
    o_ref[...] = x * lax.rsqrt(jnp.mean(x * x, axis=-1, keepdims=True) + EPS) * g_ref[...]


def final_rmsnorm(x, g):
    B, T, D = x.shape
    rows = 1024
    xf = x.reshape(B * T, D)
    out = pl.pallas_call(
        _final_rmsnorm_kernel,
        grid=(B * T // rows,),
        in_specs=[pl.BlockSpec((rows, D), lambda i: (i, 0)), pl.BlockSpec((1, D), lambda i: (0, 0))],
        out_specs=pl.BlockSpec((rows, D), lambda i: (i, 0)),
        out_shape=jax.ShapeDtypeStruct((B * T, D), x.dtype),
    )(xf, g.reshape(1, D))
    return out.reshape(B, T, D)


def kernel(x, c, ctx, c_ctx, norm1_g, norm2_g, w_ada, b_ada, w_in, na_rpb, ml_conv, ml_gate_b,
           mla_q_norm, mla_w_uq, mla_kv_norm, mla_w_ukv, swa_sink, w_out,
           peer_wq, peer_keys, peer_u, peer_v, final_norm_g):
    T = x.shape[1]
    angs_mla = axial_angles(T, MLA_ROPE)
    angs_swa = axial_angles(T, HEAD_DIM)
    xc = ctx
    for l in range(DEPTH):
        x, xc = hybrid_layer(x, xc, c, c_ctx, l < DEPTH - 1, angs_mla, angs_swa,
                             norm1_g[l], norm2_g[l], w_ada[l], b_ada[l], w_in[l], na_rpb[l],
                             ml_conv[l], ml_gate_b[l], mla_q_norm[l], mla_w_uq[l], mla_kv_norm[l],
                             mla_w_ukv[l], swa_sink[l], w_out[l], peer_wq[l], peer_keys[l],
                             peer_u[l], peer_v[l])
    return final_rmsnorm(x, final_norm_g)
```

```python
import functools

import jax
import jax.numpy as jnp
from jax import lax
import numpy as np
from jax.experimental import pallas as pl
from jax.experimental.pallas import tpu as pltpu

D_MODEL = 1024
BATCH = 2
SEQ = 16384
DEPTH = 2

CTX_LEN = 256
GRID_W = 64
N_MIXERS = 4
MIX_WIDTH = D_MODEL
GROUP_WIDTH = MIX_WIDTH // N_MIXERS
GROUP_HEADS = 4
HEAD_DIM = GROUP_WIDTH // GROUP_HEADS
NA_ROWS = 8
NA_COLS = 16
ML_CHUNK = 64
ML_CONV = 5
MLA_Q_RANK = 256
MLA_KV_RANK = 128
MLA_NOPE = 64
MLA_ROPE = 32
MLA_V = 64
SWA_KV_HEADS = 2
SWA_WINDOW = 128
ATTN_BLOCK = 128
PEER_HEADS = 8
PEER_NKEYS = 128
PEER_EXPERTS = PEER_NKEYS * PEER_NKEYS
PEER_DKEY = 128
PEER_TOPK = 16
PEER_BLOCK = 128
ROPE_BASE = 10000.0
EPS = 1e-6
IN_SIZES = (GROUP_WIDTH, GROUP_WIDTH, GROUP_WIDTH,
            2 * GROUP_WIDTH, GROUP_WIDTH, GROUP_WIDTH, 4 * GROUP_HEADS,
            MLA_Q_RANK, MLA_KV_RANK, MLA_ROPE,
            GROUP_WIDTH, SWA_KV_HEADS * HEAD_DIM, SWA_KV_HEADS * HEAD_DIM)
IN_WIDTH = sum(IN_SIZES)
F32 = jnp.float32


def rmsnorm(x, g):
    xf = x.astype(F32)
    y = xf * lax.rsqrt(jnp.mean(xf * xf, axis=-1, keepdims=True) + EPS) * g.astype(F32)
    return y.astype(x.dtype)


def heads(a, h):
    return a.reshape(a.shape[:-1] + (h, a.shape[-1] // h))


def split_cols(p):
    return jnp.split(p, np.cumsum(IN_SIZES)[:-1].tolist(), axis=-1)


def axial_angles(T, rot_dim):
    t = jnp.arange(T)
    row = (t // GRID_W).astype(F32)
    col = (t % GRID_W).astype(F32)
    half = rot_dim // 2
    inv = 1.0 / (ROPE_BASE ** (jnp.arange(0, half, 2, dtype=F32) / half))
    return row[:, None] * inv, col[:, None] * inv


def rope_1d(x, ang):
    cos = jnp.cos(ang)[None, :, None, :]
    sin = jnp.sin(ang)[None, :, None, :]
    x1, x2 = jnp.split(x.astype(F32), 2, axis=-1)
    return jnp.concatenate([x1 * cos - x2 * sin, x1 * sin + x2 * cos], axis=-1)


def rope_2d(x, angs):
    xr, xc = jnp.split(x, 2, axis=-1)
    return jnp.concatenate([rope_1d(xr, angs[0]), rope_1d(xc, angs[1])], axis=-1).astype(x.dtype)


def ctx_attn(q, k, v, scale, sink=None):
    rep = q.shape[2] // k.shape[2]
    k = jnp.repeat(k, rep, axis=2)
    v = jnp.repeat(v, rep, axis=2)
    s = jnp.einsum('bqhd,bkhd->bhqk', q, k).astype(F32) * scale
    nk = s.shape[-1]
    if sink is not None:
        s = jnp.concatenate([s, jnp.broadcast_to(sink.astype(F32)[None, :, None, None], s.shape[:-1] + (1,))], axis=-1)
    p = jax.nn.softmax(s, axis=-1)[..., :nk].astype(v.dtype)
    out = jnp.einsum('bhqk,bkhd->bqhd', p, v)
    return out.reshape(out.shape[:2] + (-1,))


def neighbourhood_attention(q, k, v, kc, vc, rpb):
    B, T, H, d = q.shape
    rows = T // GRID_W
    nr = min(NA_ROWS, rows)
    scale = d ** -0.5
    qg = jnp.swapaxes(q.reshape(B, rows, GRID_W, H, d), 0, 1)
    kg = k.reshape(B, rows, GRID_W, H, d)
    vg = v.reshape(B, rows, GRID_W, H, d)
    col_start = np.clip(np.arange(GRID_W) - NA_COLS // 2, 0, GRID_W - NA_COLS)
    col_idx = col_start[:, None] + np.arange(NA_COLS)[None, :]
    dc = col_idx - np.arange(GRID_W)[:, None] + (NA_COLS - 1)
    rpb_c = rpb.astype(F32)[:, :, dc]
    n_loc = nr * NA_COLS

    def row_block(args):
        qr, r = args
        rs = jnp.clip(r - nr // 2, 0, rows - nr)
        kr = lax.dynamic_slice_in_dim(kg, rs, nr, axis=1)[:, :, col_idx]
        vr = lax.dynamic_slice_in_dim(vg, rs, nr, axis=1)[:, :, col_idx]
        dr = rs + jnp.arange(nr) - r + (NA_ROWS - 1)
        bias = jnp.take(rpb_c, dr, axis=1)
        s_loc = jnp.einsum('bqhd,brqchd->bhqrc', qr, kr).astype(F32) * scale + jnp.transpose(bias, (0, 2, 1, 3))[None]
        s_ctx = jnp.einsum('bqhd,bkhd->bhqk', qr, kc).astype(F32) * scale
        p = jax.nn.softmax(jnp.concatenate([s_loc.reshape(B, H, GRID_W, n_loc), s_ctx], axis=-1), axis=-1).astype(v.dtype)
        p_loc = p[..., :n_loc].reshape(B, H, GRID_W, nr, NA_COLS)
        return (jnp.einsum('bhqrc,brqchd->bqhd', p_loc, vr)
                + jnp.einsum('bhqk,bkhd->bqhd', p[..., n_loc:], vc))

    out = lax.map(row_block, (qg, jnp.arange(rows)))
    return jnp.swapaxes(out, 0, 1).reshape(B, T, H * d)


def short_conv(a, w):
    T = a.shape[1]
    pad = w.shape[0] // 2
    ap = jnp.pad(a, ((0, 0), (pad, pad), (0, 0)))
    out = ap[:, :T] * w[0]
    for j in range(1, w.shape[0]):
        out = out + ap[:, j:j + T] * w[j]
    return out


def mlstm_scan(q, k, v, ig, lf, state):
    B, T, H, d = q.shape
    nc = T // ML_CHUNK

    def chunks(a):
        a = a.astype(F32).reshape((B, nc, ML_CHUNK) + a.shape[2:])
        return jnp.swapaxes(jnp.swapaxes(a, 0, 1), 2, 3)

    seen = jnp.tril(jnp.ones((ML_CHUNK, ML_CHUNK), dtype=bool))

    def step(carry, inp):
        C, n, m = carry
        qt, kt, vt, it, ft = inp
        b = jnp.cumsum(ft, axis=-1)
        d_log = jnp.where(seen, b[..., :, None] - b[..., None, :] + it[..., None, :], -jnp.inf)
        inter = b + m[..., None]
        m_t = jnp.maximum(inter, jnp.max(d_log, axis=-1))
        w = jnp.exp(d_log - m_t[..., None])
        a = jnp.exp(inter - m_t)
        s = jnp.einsum('bhtk,bhsk->bhts', qt, kt) * w
        num = jnp.einsum('bhts,bhsv->bhtv', s, vt) + a[..., None] * jnp.einsum('bhtk,bhkv->bhtv', qt, C)
        den = jnp.sum(s, axis=-1) + a * jnp.einsum('bhtk,bhk->bht', qt, n)
        h = num / jnp.maximum(jnp.abs(den), jnp.exp(-m_t))[..., None]
        g = b[..., -1:] - b + it
        m_new = jnp.maximum(b[..., -1] + m, jnp.max(g, axis=-1))
        wk = jnp.exp(g - m_new[..., None])
        decay = jnp.exp(b[..., -1] + m - m_new)
        C = decay[..., None, None] * C + jnp.einsum('bhs,bhsk,bhsv->bhkv', wk, kt, vt)
        n = decay[..., None] * n + jnp.einsum('bhs,bhsk->bhk', wk, kt)
        return (C, n, m_new), h

    state, h = lax.scan(step, state, (chunks(q), chunks(k), chunks(v), chunks(ig), chunks(lf)))
    h = jnp.swapaxes(jnp.swapaxes(h, 2, 3), 0, 1).reshape(B, T, H, d)
    return h.astype(v.dtype), state


def mlstm_prep(qk, v, gates, conv_w, gate_b):
    qk = jax.nn.silu(short_conv(qk, conv_w))
    q, k = jnp.split(qk, 2, axis=-1)
    g = (gates + gate_b).astype(F32)
    i_f, f_f, i_b, f_b = jnp.split(g, 4, axis=-1)
    return (heads(q, GROUP_HEADS) * HEAD_DIM ** -0.5, heads(k, GROUP_HEADS), heads(v, GROUP_HEADS),
            (i_f, jax.nn.log_sigmoid(f_f), i_b, jax.nn.log_sigmoid(f_b)))


def mlstm_mixer(lat, ctx, conv_w, gate_b):
    ql, kl, vl, gl = mlstm_prep(lat[0], lat[1], lat[2], conv_w, gate_b)
    qc, kc, vc, gc = mlstm_prep(ctx[0], ctx[1], ctx[2], conv_w, gate_b)
    B, _, H, d = ql.shape
    st0 = (jnp.zeros((B, H, d, d), F32), jnp.zeros((B, H, d), F32), jnp.zeros((B, H), F32))

    def rev(a):
        return a[:, ::-1]

    hc_f, st_f = mlstm_scan(qc, kc, vc, gc[0], gc[1], st0)
    hl_f, _ = mlstm_scan(ql, kl, vl, gl[0], gl[1], st_f)
    hc_b, st_b = mlstm_scan(rev(qc), rev(kc), rev(vc), rev(gc[2]), rev(gc[3]), st0)
    hl_b, _ = mlstm_scan(rev(ql), rev(kl), rev(vl), rev(gl[2]), rev(gl[3]), st_b)
    return hl_f + rev(hl_b), hc_f + rev(hc_b)


def mla_project(cq, ckv, kr, q_norm, w_uq, kv_norm, w_ukv, angs):
    q = heads(rmsnorm(cq, q_norm) @ w_uq, GROUP_HEADS)
    kv = heads(rmsnorm(ckv, kv_norm) @ w_ukv, GROUP_HEADS)
    q_nope, q_rope = q[..., :MLA_NOPE], q[..., MLA_NOPE:]
    k_nope, v = kv[..., :MLA_NOPE], kv[..., MLA_NOPE:]
    k_rope = kr[:, :, None, :]
    if angs is not None:
        q_rope = rope_2d(q_rope, angs)
        k_rope = rope_2d(k_rope, angs)
    k_rope = jnp.broadcast_to(k_rope, k_nope.shape[:-1] + (MLA_ROPE,))
    return (jnp.concatenate([q_nope, q_rope], axis=-1), jnp.concatenate([k_nope, k_rope], axis=-1), v)


def block_dense_attention(q, k_all, v_all, scale):
    B, T, H, dq = q.shape
    nb = T // ATTN_BLOCK
    qb = jnp.swapaxes(q.reshape(B, nb, ATTN_BLOCK, H, dq), 0, 1)

    def one(qblk):
        s = jnp.einsum('bqhd,bkhd->bhqk', qblk, k_all).astype(F32) * scale
        p = jax.nn.softmax(s, axis=-1).astype(v_all.dtype)
        return jnp.einsum('bhqk,bkhd->bqhd', p, v_all)

    out = lax.map(one, qb)
    return jnp.swapaxes(out, 0, 1).reshape(B, T, -1)


def window_attention(q, k, v, kc, vc, sink):
    B, T, H, d = q.shape
    KVH = k.shape[2]
    G = H // KVH
    nb = T // ATTN_BLOCK
    span = ATTN_BLOCK + 2 * SWA_WINDOW
    n_ctx = kc.shape[1]
    scale = d ** -0.5
    padw = ((0, 0), (SWA_WINDOW, SWA_WINDOW), (0, 0), (0, 0))
    kp = jnp.pad(k, padw)
    vp = jnp.pad(v, padw)
    start = jnp.arange(nb) * ATTN_BLOCK
    idx = start[:, None] + jnp.arange(span)[None, :]
    kb = kp[:, idx]
    vb = vp[:, idx]
    key_pos = idx - SWA_WINDOW
    q_pos = start[:, None] + jnp.arange(ATTN_BLOCK)[None, :]
    mask = ((jnp.abs(q_pos[:, :, None] - key_pos[:, None, :]) <= SWA_WINDOW)
            & (key_pos >= 0)[:, None, :] & (key_pos < T)[:, None, :])
    qb = q.reshape(B, nb, ATTN_BLOCK, KVH, G, d)
    s_loc = jnp.einsum('bnqhgd,bnkhd->bnhgqk', qb, kb).astype(F32) * scale
    s_loc = jnp.where(mask[None, :, None, None], s_loc, -jnp.inf)
    s_ctx = jnp.einsum('bnqhgd,bkhd->bnhgqk', qb, kc).astype(F32) * scale
    s_sink = jnp.broadcast_to(sink.astype(F32).reshape(1, 1, KVH, G, 1, 1), s_loc.shape[:-1] + (1,))
    p = jax.nn.softmax(jnp.concatenate([s_loc, s_ctx, s_sink], axis=-1), axis=-1).astype(v.dtype)
    out = (jnp.einsum('bnhgqk,bnkhd->bnqhgd', p[..., :span], vb)
           + jnp.einsum('bnhgqk,bkhd->bnqhgd', p[..., span:span + n_ctx], vc))
    return out.reshape(B, T, H * d)


BF16 = jnp.bfloat16
LANES = 128
SUBLANES = 8
ROW_SEGS = D_MODEL // LANES
ROW_WORDS = ROW_SEGS // 2
PEER_PICKS = PEER_HEADS * PEER_TOPK
PEER_TOPK_TOKENS = 256
PEER_GATHER_TOKENS = 128
VMEM_LIMIT_BYTES = 56 * 1024 * 1024


def _split_bf16(x, parts):
    out = []
    for _ in range(parts):
        p = x.astype(BF16)
        out.append(p)
        x = x - p.astype(F32)
    return out


def _topk_rows(s, k):
    n = s.shape[0]
    iota = lax.broadcasted_iota(jnp.int32, s.shape, 0)
    vals, idxs = [], []
    for _ in range(k):
        m = jnp.max(s, axis=0, keepdims=True)
        i = jnp.min(jnp.where(s == m, iota, n), axis=0, keepdims=True)
        vals.append(m)
        idxs.append(i)
        s = jnp.where(iota == i, -jnp.inf, s)
    return jnp.concatenate(vals, axis=0), jnp.concatenate(idxs, axis=0)


def _peer_topk_kernel(x_ref, wq_ref, keys_ref, eidx_ref, gate_ref):
    xb = x_ref[...].astype(BF16)
    q = jnp.dot(xb, wq_ref[...], preferred_element_type=F32)
    nt = (((1,), (1,)), ((), ()))
    sv, si = [], []
    for p in range(2):
        qp = q[:, p * PEER_DKEY:(p + 1) * PEER_DKEY].astype(BF16)
        s = lax.dot_general(keys_ref[0, p], qp, nt, preferred_element_type=F32)
        v_, i_ = _topk_rows(s, PEER_TOPK)
        sv.append(v_)
        si.append(i_)
    cs, ce = [], []
    half = PEER_TOPK // 2
    for a in range(half):
        nb = PEER_TOPK if a == 0 else half
        cs.append(sv[0][a:a + 1] + sv[1][:nb])
        ce.append(si[0][a:a + 1] * PEER_NKEYS + si[1][:nb])
    cs.append(sv[0][half:] + sv[1][0:1])
    ce.append(si[0][half:] * PEER_NKEYS + si[1][0:1])
    cand_s = jnp.concatenate(cs, axis=0)
    cand_e = jnp.concatenate(ce, axis=0)
    fs, fpos = _topk_rows(cand_s, PEER_TOPK)
    iota = lax.broadcasted_iota(jnp.int32, cand_e.shape, 0)
    eidx = [jnp.max(jnp.where(iota == fpos[j:j + 1], cand_e, -1), axis=0, keepdims=True)
            for j in range(PEER_TOPK)]
    ex = jnp.exp(fs - fs[0:1])
    eidx_ref[0] = jnp.concatenate(eidx, axis=0)
    gate_ref[0] = ex / jnp.sum(ex, axis=0, keepdims=True)


def peer_topk(h, wq, sub_keys):
    N, D = h.shape
    T = PEER_TOPK_TOKENS
    wqb = wq.astype(BF16)
    kb = sub_keys.astype(BF16)
    eidx, gate = pl.pallas_call(
        _peer_topk_kernel,
        grid=(N // T, PEER_HEADS),
        in_specs=[pl.BlockSpec((T, D), lambda i, h_: (i, 0)),
                  pl.BlockSpec((D, 2 * PEER_DKEY), lambda i, h_: (0, h_)),
                  pl.BlockSpec((1, 2, PEER_NKEYS, PEER_DKEY), lambda i, h_: (h_, 0, 0, 0))],
        out_specs=[pl.BlockSpec((1, PEER_TOPK, T), lambda i, h_: (h_, 0, i)),
                   pl.BlockSpec((1, PEER_TOPK, T), lambda i, h_: (h_, 0, i))],
        out_shape=[jax.ShapeDtypeStruct((PEER_HEADS, PEER_TOPK, N), jnp.int32),
                   jax.ShapeDtypeStruct((PEER_HEADS, PEER_TOPK, N), F32)],
        compiler_params=pltpu.CompilerParams(vmem_limit_bytes=VMEM_LIMIT_BYTES),
    )(h, wqb, kb)
    return eidx.reshape(PEER_PICKS, N), gate.reshape(PEER_PICKS, N)


def pack_expert_table(tab):
    E = tab.shape[0]
    t = tab.astype(BF16).reshape(E, ROW_WORDS, 2, LANES)
    t = jnp.swapaxes(t, -1, -2)
    return lax.bitcast_convert_type(t, jnp.uint32).reshape(E * ROW_WORDS, LANES)


def _stage_rows(idx_ref, tab_ref, stage_ref, t):
    for k in range(PEER_PICKS):
        off = pl.multiple_of(idx_ref[k, t] * ROW_WORDS, ROW_WORDS)
        stage_ref[k * ROW_WORDS:(k + 1) * ROW_WORDS, :] = tab_ref[pl.ds(off, ROW_WORDS), :]
    return pltpu.bitcast(stage_ref[...], BF16)


def _peer_act_kernel(idx_ref, x_ref, gate_ref, tab_ref, seg_mask_ref, group_ref, w_ref,
                     stage_ref, rows_ref):
    T = x_ref.shape[0]
    nt = (((1,), (1,)), ((), ()))

    def token(t, carry):
        sb = _stage_rows(idx_ref, tab_ref, stage_ref, t)
        xs = jnp.concatenate(_split_bf16(x_ref[t], 2), axis=0)
        r = lax.dot_general(xs, sb, nt, preferred_element_type=F32)
        r = r * seg_mask_ref[...]
        rows_ref[t] = r[:SUBLANES] + r[SUBLANES:]
        return carry

    lax.fori_loop(0, T, token, 0)
    rows = rows_ref[...].reshape(T * SUBLANES, PEER_PICKS * ROW_SEGS)
    part = jnp.zeros((T * SUBLANES, PEER_PICKS), F32)
    for piece in _split_bf16(rows, 3):
        part = part + jnp.dot(piece, group_ref[...], preferred_element_type=F32)
    act = jnp.sum(part.reshape(T, SUBLANES, PEER_PICKS), axis=1)
    w_ref[...] = gate_ref[...] * (0.5 * act * (1.0 + lax.erf(act * (2.0 ** -0.5))))


def _peer_out_kernel(idx_ref, w_ref, tab_ref, expand_ref, seg_mask_ref, f_ref, stage_ref):
    T = w_ref.shape[0]

    def token(t, carry):
        sb = _stage_rows(idx_ref, tab_ref, stage_ref, t)
        w8 = w_ref[pl.ds(pl.multiple_of((t // SUBLANES) * SUBLANES, SUBLANES), SUBLANES), :]
        row = lax.broadcasted_iota(jnp.int32, w8.shape, 0) == t % SUBLANES
        wt = jnp.sum(jnp.where(row, w8, 0.0), axis=0, keepdims=True)
        lhs = jnp.concatenate([jnp.broadcast_to(p, (SUBLANES, PEER_PICKS)) for p in _split_bf16(wt, 2)],
                              axis=0)
        wrep = jnp.dot(lhs, expand_ref[...], preferred_element_type=F32)
        wsel = (wrep * seg_mask_ref[...]).astype(BF16)
        o = jnp.dot(wsel, sb, preferred_element_type=F32)
        f_ref[t] = o[:SUBLANES] + o[SUBLANES:]
        return carry

    lax.fori_loop(0, T, token, 0)


def _peer_constants():
    cols = np.arange(PEER_PICKS * ROW_SEGS)
    seg_mask = (cols[None, :] % ROW_SEGS == np.arange(2 * SUBLANES)[:, None] % SUBLANES)
    group = (cols[:, None] // ROW_SEGS == np.arange(PEER_PICKS)[None, :])
    return (jnp.asarray(seg_mask, F32), jnp.asarray(group, BF16), jnp.asarray(group.T, BF16))


def peer_ffn(h, wq, sub_keys, u, v):
    N, D = h.shape
    T = PEER_GATHER_TOKENS
    eidx, gate = peer_topk(h, wq, sub_keys)
    seg_mask, group, expand = _peer_constants()
    x3 = h.reshape(N, ROW_SEGS, LANES)
    rows = u.shape[0] * ROW_WORDS
    idx_spec = pl.BlockSpec((PEER_PICKS, T), lambda i: (0, i), memory_space=pltpu.SMEM)
    tab_spec = pl.BlockSpec((rows, LANES), lambda i: (0, 0), pipeline_mode=pl.Buffered(1))
    const = lambda shape: pl.BlockSpec(shape, lambda i: (0, 0))
    params = pltpu.CompilerParams(vmem_limit_bytes=VMEM_LIMIT_BYTES)
    w = pl.pallas_call(
        _peer_act_kernel,
        grid=(N // T,),
        in_specs=[idx_spec,
                  pl.BlockSpec((T, ROW_SEGS, LANES), lambda i: (i, 0, 0)),
                  pl.BlockSpec((T, PEER_PICKS), lambda i: (i, 0)),
                  tab_spec, const(seg_mask.shape), const(group.shape)],
        out_specs=pl.BlockSpec((T, PEER_PICKS), lambda i: (i, 0)),
        out_shape=jax.ShapeDtypeStruct((N, PEER_PICKS), F32),
        scratch_shapes=[pltpu.VMEM((PEER_PICKS * ROW_WORDS, LANES), jnp.uint32),
                        pltpu.VMEM((T, SUBLANES, PEER_PICKS * ROW_SEGS), F32)],
        compiler_params=params,
    )(eidx, x3, gate.T, pack_expert_table(u), seg_mask, group)
    f = pl.pallas_call(
        _peer_out_kernel,
        grid=(N // T,),
        in_specs=[idx_spec,
                  pl.BlockSpec((T, PEER_PICKS), lambda i: (i, 0)),
                  tab_spec, const(expand.shape), const(seg_mask.shape)],
        out_specs=pl.BlockSpec((T, ROW_SEGS, LANES), lambda i: (i, 0, 0)),
        out_shape=jax.ShapeDtypeStruct((N, ROW_SEGS, LANES), F32),
        scratch_shapes=[pltpu.VMEM((PEER_PICKS * ROW_WORDS, LANES), jnp.uint32)],
        compiler_params=params,
    )(eidx, w, pack_expert_table(v), expand, seg_mask)
    return f.reshape(N, D)


def hybrid_layer(x, xc, c, c_ctx, need_ctx, angs_mla, angs_swa,
                 norm1_g, norm2_g, w_ada, b_ada, w_in, na_rpb, ml_conv, ml_gate_b,
                 mla_q_norm, mla_w_uq, mla_kv_norm, mla_w_ukv, swa_sink, w_out,
                 peer_wq, peer_keys, peer_u, peer_v):
    B, T, D = x.shape
    H = GROUP_HEADS
    sh1, sc1, g1, sh2, sc2, g2 = jnp.split((jax.nn.silu(c) @ w_ada + b_ada)[:, None, :], 6, axis=-1)
    sh1c, sc1c, g1c, sh2c, sc2c, g2c = jnp.split(jax.nn.silu(c_ctx) @ w_ada + b_ada, 6, axis=-1)
    h = rmsnorm(x, norm1_g) * (1.0 + sc1) + sh1
    hc = rmsnorm(xc, norm1_g) * (1.0 + sc1c) + sh1c
    (na_q, na_k, na_v, ml_qk, ml_v, ml_o, ml_g,
     mla_cq, mla_ckv, mla_kr, sw_q, sw_k, sw_v) = split_cols(h @ w_in)
    (na_qc, na_kc, na_vc, ml_qkc, ml_vc, ml_oc, ml_gc,
     mla_cqc, mla_ckvc, mla_krc, sw_qc, sw_kc, sw_vc) = split_cols(hc @ w_in)
    attn_scale = HEAD_DIM ** -0.5
    mla_scale = (MLA_NOPE + MLA_ROPE) ** -0.5
    kc_a, vc_a = heads(na_kc, H), heads(na_vc, H)
    y_a = neighbourhood_attention(heads(na_q, H), heads(na_k, H), heads(na_v, H), kc_a, vc_a, na_rpb)
    h_lat, h_ctx = mlstm_mixer((ml_qk, ml_v, ml_g), (ml_qkc, ml_vc, ml_gc), ml_conv, ml_gate_b)
    y_b = h_lat.reshape(B, T, GROUP_WIDTH) * jax.nn.sigmoid(ml_o)
    q_m, k_m, v_m = mla_project(mla_cq, mla_ckv, mla_kr, mla_q_norm, mla_w_uq, mla_kv_norm, mla_w_ukv, angs_mla)
    qc_m, kc_m, vc_m = mla_project(mla_cqc, mla_ckvc, mla_krc, mla_q_norm, mla_w_uq, mla_kv_norm, mla_w_ukv, None)
    y_c = block_dense_attention(q_m, jnp.concatenate([kc_m, k_m], axis=1), jnp.concatenate([vc_m, v_m], axis=1), mla_scale)
    kc_d, vc_d = heads(sw_kc, SWA_KV_HEADS), heads(sw_vc, SWA_KV_HEADS)
    y_d = window_attention(rope_2d(heads(sw_q, H), angs_swa), rope_2d(heads(sw_k, SWA_KV_HEADS), angs_swa),
                           heads(sw_v, SWA_KV_HEADS), kc_d, vc_d, swa_sink)
    x = x + g1 * (jnp.concatenate([y_a, y_b, y_c, y_d], axis=-1) @ w_out)
    h2 = rmsnorm(x, norm2_g) * (1.0 + sc2) + sh2
    if need_ctx:
        Tc = xc.shape[1]
        y_ctx = jnp.concatenate([
            ctx_attn(heads(na_qc, H), kc_a, vc_a, attn_scale),
            h_ctx.reshape(B, Tc, GROUP_WIDTH) * jax.nn.sigmoid(ml_oc),
            ctx_attn(qc_m, kc_m, vc_m, mla_scale),
            ctx_attn(heads(sw_qc, H), kc_d, vc_d, attn_scale, swa_sink)], axis=-1)
        xc = xc + g1c * (y_ctx @ w_out)
        h2c = rmsnorm(xc, norm2_g) * (1.0 + sc2c) + sh2c
        f = peer_ffn(jnp.concatenate([h2.reshape(B * T, D), h2c.reshape(B * Tc, D)], axis=0),
                     peer_wq, peer_keys, peer_u, peer_v)
        x = x + g2 * f[:B * T].reshape(B, T, D)
        xc = xc + g2c * f[B * T:].reshape(B, Tc, D)
        return x, xc
    x = x + g2 * peer_ffn(h2.reshape(B * T, D), peer_wq, peer_keys, peer_u, peer_v).reshape(B, T, D)
    return x, None


def _final_rmsnorm_kernel(x_ref, g_ref, o_ref):
    x = x_ref[...]
    o_ref[...] = x * lax.rsqrt(jnp.mean(x * x, axis=-1, keepdims=True) + EPS) * g_ref[...]


def final_rmsnorm(x, g):
    B, T, D = x.shape
    rows = 1024
    xf = x.reshape(B * T, D)
    out = pl.pallas_call(
        _final_rmsnorm_kernel,
        grid=(B * T // rows,),
        in_specs=[pl.BlockSpec((rows, D), lambda i: (i, 0)), pl.BlockSpec((1, D), lambda i: (0, 0))],
        out_specs=pl.BlockSpec((rows, D), lambda i: (i, 0)),
        out_shape=jax.ShapeDtypeStruct((B * T, D), x.dtype),
    )(xf, g.reshape(1, D))
    return out.reshape(B, T, D)


def kernel(x, c, ctx, c_ctx, norm1_g, norm2_g, w_ada, b_ada, w_in, na_rpb, ml_conv, ml_gate_b,
           mla_q_norm, mla_w_uq, mla_kv_norm, mla_w_ukv, swa_sink, w_out,
           peer_wq, peer_keys, peer_u, peer_v, final_norm_g):
    T = x.shape[1]
    angs_mla = axial_angles(T, MLA_ROPE)
    angs_swa = axial_angles(T, HEAD_DIM)
    xc = ctx
    for l in range(DEPTH):
        x, xc = hybrid_layer(x, xc, c, c_ctx, l < DEPTH - 1, angs_mla, angs_swa,
                             norm1_g[l], norm2_g[l], w_ada[l], b_ada[l], w_in[l], na_rpb[l],
                             ml_conv[l], ml_gate_b[l], mla_q_norm[l], mla_w_uq[l], mla_kv_norm[l],
                             mla_w_ukv[l], swa_sink[l], w_out[l], peer_wq[l], peer_keys[l],
                             peer_u[l], peer_v[l])
    return final_rmsnorm(x, final_norm_g)
```

```python
import functools

import jax
import jax.numpy as jnp
from jax import lax
import numpy as np
from jax.experimental import pallas as pl
from jax.experimental.pallas import tpu as pltpu

D_MODEL = 1024
BATCH = 2
SEQ = 16384
DEPTH = 2

CTX_LEN = 256
GRID_W = 64
N_MIXERS = 4
MIX_WIDTH = D_MODEL
GROUP_WIDTH = MIX_WIDTH // N_MIXERS
GROUP_HEADS = 4
HEAD_DIM = GROUP_WIDTH // GROUP_HEADS
NA_ROWS = 8
NA_COLS = 16
ML_CHUNK = 64
ML_CONV = 5
MLA_Q_RANK = 256
MLA_KV_RANK = 128
MLA_NOPE = 64
MLA_ROPE = 32
MLA_V = 64
SWA_KV_HEADS = 2
SWA_WINDOW = 128
ATTN_BLOCK = 128
PEER_HEADS = 8
PEER_NKEYS = 128
PEER_EXPERTS = PEER_NKEYS * PEER_NKEYS
PEER_DKEY = 128
PEER_TOPK = 16
PEER_BLOCK = 128
ROPE_BASE = 10000.0
EPS = 1e-6
IN_SIZES = (GROUP_WIDTH, GROUP_WIDTH, GROUP_WIDTH,
            2 * GROUP_WIDTH, GROUP_WIDTH, GROUP_WIDTH, 4 * GROUP_HEADS,
            MLA_Q_RANK, MLA_KV_RANK, MLA_ROPE,
            GROUP_WIDTH, SWA_KV_HEADS * HEAD_DIM, SWA_KV_HEADS * HEAD_DIM)
IN_WIDTH = sum(IN_SIZES)
F32 = jnp.float32


def rmsnorm(x, g):
    xf = x.astype(F32)
    y = xf * lax.rsqrt(jnp.mean(xf * xf, axis=-1, keepdims=True) + EPS) * g.astype(F32)
    return y.astype(x.dtype)


def heads(a, h):
    return a.reshape(a.shape[:-1] + (h, a.shape[-1] // h))


def split_cols(p):
    return jnp.split(p, np.cumsum(IN_SIZES)[:-1].tolist(), axis=-1)


def axial_angles(T, rot_dim):
    t = jnp.arange(T)
    row = (t // GRID_W).astype(F32)
    col = (t % GRID_W).astype(F32)
    half = rot_dim // 2
    inv = 1.0 / (ROPE_BASE ** (jnp.arange(0, half, 2, dtype=F32) / half))
    return row[:, None] * inv, col[:, None] * inv


def rope_1d(x, ang):
    cos = jnp.cos(ang)[None, :, None, :]
    sin = jnp.sin(ang)[None, :, None, :]
    x1, x2 = jnp.split(x.astype(F32), 2, axis=-1)
    return jnp.concatenate([x1 * cos - x2 * sin, x1 * sin + x2 * cos], axis=-1)


def rope_2d(x, angs):
    xr, xc = jnp.split(x, 2, axis=-1)
    return jnp.concatenate([rope_1d(xr, angs[0]), rope_1d(xc, angs[1])], axis=-1).astype(x.dtype)


def ctx_attn(q, k, v, scale, sink=None):
    rep = q.shape[2] // k.shape[2]
    k = jnp.repeat(k, rep, axis=2)
    v = jnp.repeat(v, rep, axis=2)
    s = jnp.einsum('bqhd,bkhd->bhqk', q, k).astype(F32) * scale
    nk = s.shape[-1]
    if sink is not None:
        s = jnp.concatenate([s, jnp.broadcast_to(sink.astype(F32)[None, :, None, None], s.shape[:-1] + (1,))], axis=-1)
    p = jax.nn.softmax(s, axis=-1)[..., :nk].astype(v.dtype)
    out = jnp.einsum('bhqk,bkhd->bqhd', p, v)
    return out.reshape(out.shape[:2] + (-1,))


NT_DIMS = (((1,), (1,)), ((), ()))
NA_SPAN = NA_ROWS * GRID_W


def _head_mask(width):
    rows = lax.broadcasted_iota(jnp.int32, (GROUP_HEADS * width, GROUP_WIDTH), 0) // width
    cols = lax.broadcasted_iota(jnp.int32, (GROUP_HEADS * width, GROUP_WIDTH), 1) // HEAD_DIM
    return (rows == cols).astype(F32)


def _na_kernel(q_ref, k_ref, v_ref, kc_ref, vc_ref, bias_ref, o_ref):
    r = pl.program_id(1)
    rows = pl.num_programs(1)
    rs = jnp.clip(r - NA_ROWS // 2, 0, rows - NA_ROWS)
    start = pl.multiple_of(rs * GRID_W, GRID_W)
    kw = k_ref[0, pl.ds(start, NA_SPAN), :]
    vw = v_ref[0, pl.ds(start, NA_SPAN), :]
    hm = _head_mask(GRID_W)
    q = q_ref[0] * (HEAD_DIM ** -0.5)
    q4 = (jnp.concatenate([q] * GROUP_HEADS, axis=0) * hm).astype(BF16)
    s_loc = lax.dot_general(q4, kw, NT_DIMS, preferred_element_type=F32) + bias_ref[rs - r + NA_ROWS - 1]
    s_ctx = lax.dot_general(q4, kc_ref[0], NT_DIMS, preferred_element_type=F32)
    m = jnp.maximum(jnp.max(s_loc, axis=-1, keepdims=True), jnp.max(s_ctx, axis=-1, keepdims=True))
    p_loc = jnp.exp(s_loc - m)
    p_ctx = jnp.exp(s_ctx - m)
    l = jnp.sum(p_loc, axis=-1, keepdims=True) + jnp.sum(p_ctx, axis=-1, keepdims=True)
    o = (jnp.dot(p_loc.astype(BF16), vw, preferred_element_type=F32)
         + jnp.dot(p_ctx.astype(BF16), vc_ref[0], preferred_element_type=F32)) * (hm / l)
    o_ref[0] = sum(o[h * GRID_W:(h + 1) * GRID_W] for h in range(GROUP_HEADS))


def _na_bias_table(rpb):
    c = np.arange(GRID_W)
    col_start = np.clip(c - NA_COLS // 2, 0, GRID_W - NA_COLS)
    valid = (c[None, :] >= col_start[:, None]) & (c[None, :] < col_start[:, None] + NA_COLS)
    dc = np.clip(c[None, :] - c[:, None] + NA_COLS - 1, 0, 2 * NA_COLS - 2)
    dr = np.arange(NA_ROWS)[:, None] + np.arange(NA_ROWS)[None, :]
    t = rpb.astype(F32)[:, dr][..., dc]
    t = jnp.where(valid[None, None, None], t, -jnp.inf)
    return jnp.transpose(t, (1, 0, 3, 2, 4)).reshape(NA_ROWS, GROUP_HEADS * GRID_W, NA_SPAN)


def neighbourhood_attention(q, k, v, kc, vc, rpb):
    B, T, C = q.shape
    rows = T // GRID_W
    n_ctx = kc.shape[1]
    bias = _na_bias_table(rpb)
    full = lambda n: pl.BlockSpec((1, n, C), lambda b, r: (b, 0, 0))
    return pl.pallas_call(
        _na_kernel,
        grid=(B, rows),
        in_specs=[pl.BlockSpec((1, GRID_W, C), lambda b, r: (b, r, 0)),
                  full(T), full(T), full(n_ctx), full(n_ctx),
                  pl.BlockSpec(bias.shape, lambda b, r: (0, 0, 0))],
        out_specs=pl.BlockSpec((1, GRID_W, C), lambda b, r: (b, r, 0)),
        out_shape=jax.ShapeDtypeStruct((B, T, C), F32),
        compiler_params=pltpu.CompilerParams(vmem_limit_bytes=VMEM_LIMIT_BYTES),
    )(q, k.astype(BF16), v.astype(BF16), kc.astype(BF16), vc.astype(BF16), bias)


def short_conv(a, w):
    T = a.shape[1]
    pad = w.shape[0] // 2
    ap = jnp.pad(a, ((0, 0), (pad, pad), (0, 0)))
    out = ap[:, :T] * w[0]
    for j in range(1, w.shape[0]):
        out = out + ap[:, j:j + T] * w[j]
    return out


def mlstm_scan(q, k, v, ig, lf, state):
    B, T, H, d = q.shape
    nc = T // ML_CHUNK

    def chunks(a):
        a = a.astype(F32).reshape((B, nc, ML_CHUNK) + a.shape[2:])
        return jnp.swapaxes(jnp.swapaxes(a, 0, 1), 2, 3)

    seen = jnp.tril(jnp.ones((ML_CHUNK, ML_CHUNK), dtype=bool))

    def step(carry, inp):
        C, n, m = carry
        qt, kt, vt, it, ft = inp
        b = jnp.cumsum(ft, axis=-1)
        d_log = jnp.where(seen, b[..., :, None] - b[..., None, :] + it[..., None, :], -jnp.inf)
        inter = b + m[..., None]
        m_t = jnp.maximum(inter, jnp.max(d_log, axis=-1))
        w = jnp.exp(d_log - m_t[..., None])
        a = jnp.exp(inter - m_t)
        s = jnp.einsum('bhtk,bhsk->bhts', qt, kt) * w
        num = jnp.einsum('bhts,bhsv->bhtv', s, vt) + a[..., None] * jnp.einsum('bhtk,bhkv->bhtv', qt, C)
        den = jnp.sum(s, axis=-1) + a * jnp.einsum('bhtk,bhk->bht', qt, n)
        h = num / jnp.maximum(jnp.abs(den), jnp.exp(-m_t))[..., None]
        g = b[..., -1:] - b + it
        m_new = jnp.maximum(b[..., -1] + m, jnp.max(g, axis=-1))
        wk = jnp.exp(g - m_new[..., None])
        decay = jnp.exp(b[..., -1] + m - m_new)
        C = decay[..., None, None] * C + jnp.einsum('bhs,bhsk,bhsv->bhkv', wk, kt, vt)
        n = decay[..., None] * n + jnp.einsum('bhs,bhsk->bhk', wk, kt)
        return (C, n, m_new), h

    state, h = lax.scan(step, state, (chunks(q), chunks(k), chunks(v), chunks(ig), chunks(lf)))
    h = jnp.swapaxes(jnp.swapaxes(h, 2, 3), 0, 1).reshape(B, T, H, d)
    return h.astype(v.dtype), state


def mlstm_prep(qk, v, gates, conv_w, gate_b):
    qk = jax.nn.silu(short_conv(qk, conv_w))
    q, k = jnp.split(qk, 2, axis=-1)
    g = (gates + gate_b).astype(F32)
    i_f, f_f, i_b, f_b = jnp.split(g, 4, axis=-1)
    return (heads(q, GROUP_HEADS) * HEAD_DIM ** -0.5, heads(k, GROUP_HEADS), heads(v, GROUP_HEADS),
            (i_f, jax.nn.log_sigmoid(f_f), i_b, jax.nn.log_sigmoid(f_b)))


def mlstm_mixer(lat, ctx, conv_w, gate_b):
    ql, kl, vl, gl = mlstm_prep(lat[0], lat[1], lat[2], conv_w, gate_b)
    qc, kc, vc, gc = mlstm_prep(ctx[0], ctx[1], ctx[2], conv_w, gate_b)
    B, _, H, d = ql.shape
    st0 = (jnp.zeros((B, H, d, d), F32), jnp.zeros((B, H, d), F32), jnp.zeros((B, H), F32))

    def rev(a):
        return a[:, ::-1]

    hc_f, st_f = mlstm_scan(qc, kc, vc, gc[0], gc[1], st0)
    hl_f, _ = mlstm_scan(ql, kl, vl, gl[0], gl[1], st_f)
    hc_b, st_b = mlstm_scan(rev(qc), rev(kc), rev(vc), rev(gc[2]), rev(gc[3]), st0)
    hl_b, _ = mlstm_scan(rev(ql), rev(kl), rev(vl), rev(gl[2]), rev(gl[3]), st_b)
    return hl_f + rev(hl_b), hc_f + rev(hc_b)


def mla_project(cq, ckv, kr, q_norm, w_uq, kv_norm, w_ukv, angs):
    q = heads(rmsnorm(cq, q_norm) @ w_uq, GROUP_HEADS)
    kv = heads(rmsnorm(ckv, kv_norm) @ w_ukv, GROUP_HEADS)
    q_nope, q_rope = q[..., :MLA_NOPE], q[..., MLA_NOPE:]
    k_nope, v = kv[..., :MLA_NOPE], kv[..., MLA_NOPE:]
    k_rope = kr[:, :, None, :]
    if angs is not None:
        q_rope = rope_2d(q_rope, angs)
        k_rope = rope_2d(k_rope, angs)
    k_rope = jnp.broadcast_to(k_rope, k_nope.shape[:-1] + (MLA_ROPE,))
    return (jnp.concatenate([q_nope, q_rope], axis=-1), jnp.concatenate([k_nope, k_rope], axis=-1), v)


DENSE_Q_TILE = 1024
DENSE_K_TILE_MAX = 1664


def _dense_attn_kernel(q_ref, k_ref, v_ref, o_ref, m_ref, l_ref, acc_ref, *, scale):
    j = pl.program_id(3)

    @pl.when(j == 0)
    def _():
        m_ref[...] = jnp.full(m_ref.shape, -jnp.inf, F32)
        l_ref[...] = jnp.zeros(l_ref.shape, F32)
        acc_ref[...] = jnp.zeros(acc_ref.shape, F32)

    s = lax.dot_general(q_ref[0, 0], k_ref[0, 0], NT_DIMS, preferred_element_type=F32) * scale
    m_prev = m_ref[...]
    m_new = jnp.maximum(m_prev, jnp.max(s, axis=-1, keepdims=True))
    alpha = jnp.exp(m_prev - m_new)
    p = jnp.exp(s - m_new)
    l_ref[...] = alpha * l_ref[...] + jnp.sum(p, axis=-1, keepdims=True)
    acc_ref[...] = alpha * acc_ref[...] + jnp.dot(p.astype(BF16), v_ref[0, 0], preferred_element_type=F32)
    m_ref[...] = m_new

    @pl.when(j == pl.num_programs(3) - 1)
    def _():
        o_ref[0, 0] = acc_ref[...] / l_ref[...]


def block_dense_attention(q, k_all, v_all, scale):
    B, T, H, dq = q.shape
    NK, dv = k_all.shape[1], v_all.shape[-1]
    tq = min(DENSE_Q_TILE, T)
    tk = max(t for t in range(LANES, DENSE_K_TILE_MAX + 1, LANES) if NK % t == 0)
    hm = lambda a: jnp.swapaxes(a, 1, 2).astype(BF16)
    out = pl.pallas_call(
        functools.partial(_dense_attn_kernel, scale=scale),
        grid=(B, H, T // tq, NK // tk),
        in_specs=[pl.BlockSpec((1, 1, tq, dq), lambda b, h, i, j: (b, h, i, 0)),
                  pl.BlockSpec((1, 1, tk, dq), lambda b, h, i, j: (b, h, j, 0)),
                  pl.BlockSpec((1, 1, tk, dv), lambda b, h, i, j: (b, h, j, 0))],
        out_specs=pl.BlockSpec((1, 1, tq, dv), lambda b, h, i, j: (b, h, i, 0)),
        out_shape=jax.ShapeDtypeStruct((B, H, T, dv), F32),
        scratch_shapes=[pltpu.VMEM((tq, 1), F32), pltpu.VMEM((tq, 1), F32), pltpu.VMEM((tq, dv), F32)],
        compiler_params=pltpu.CompilerParams(vmem_limit_bytes=VMEM_LIMIT_BYTES),
    )(hm(q), hm(k_all), hm(v_all))
    return jnp.swapaxes(out, 1, 2).reshape(B, T, H * dv)


def window_attention(q, k, v, kc, vc, sink):
    B, T, H, d = q.shape
    KVH = k.shape[2]
    G = H // KVH
    nb = T // ATTN_BLOCK
    span = ATTN_BLOCK + 2 * SWA_WINDOW
    n_ctx = kc.shape[1]
    scale = d ** -0.5
    padw = ((0, 0), (SWA_WINDOW, SWA_WINDOW), (0, 0), (0, 0))
    kp = jnp.pad(k, padw)
    vp = jnp.pad(v, padw)
    start = jnp.arange(nb) * ATTN_BLOCK
    idx = start[:, None] + jnp.arange(span)[None, :]
    kb = kp[:, idx]
    vb = vp[:, idx]
    key_pos = idx - SWA_WINDOW
    q_pos = start[:, None] + jnp.arange(ATTN_BLOCK)[None, :]
    mask = ((jnp.abs(q_pos[:, :, None] - key_pos[:, None, :]) <= SWA_WINDOW)
            & (key_pos >= 0)[:, None, :] & (key_pos < T)[:, None, :])
    qb = q.reshape(B, nb, ATTN_BLOCK, KVH, G, d)
    s_loc = jnp.einsum('bnqhgd,bnkhd->bnhgqk', qb, kb).astype(F32) * scale
    s_loc = jnp.where(mask[None, :, None, None], s_loc, -jnp.inf)
    s_ctx = jnp.einsum('bnqhgd,bkhd->bnhgqk', qb, kc).astype(F32) * scale
    s_sink = jnp.broadcast_to(sink.astype(F32).reshape(1, 1, KVH, G, 1, 1), s_loc.shape[:-1] + (1,))
    p = jax.nn.softmax(jnp.concatenate([s_loc, s_ctx, s_sink], axis=-1), axis=-1).astype(v.dtype)
    out = (jnp.einsum('bnhgqk,bnkhd->bnqhgd', p[..., :span], vb)
           + jnp.einsum('bnhgqk,bkhd->bnqhgd', p[..., span:span + n_ctx], vc))
    return out.reshape(B, T, H * d)


BF16 = jnp.bfloat16
LANES = 128
SUBLANES = 8
ROW_SEGS = D_MODEL // LANES
ROW_WORDS = ROW_SEGS // 2
PEER_PICKS = PEER_HEADS * PEER_TOPK
PEER_TOPK_TOKENS = 256
PEER_GATHER_TOKENS = 128
VMEM_LIMIT_BYTES = 56 * 1024 * 1024


def _split_bf16(x, parts):
    out = []
    for _ in range(parts):
        p = x.astype(BF16)
        out.append(p)
        x = x - p.astype(F32)
    return out


def _topk_rows(s, k):
    n = s.shape[0]
    iota = lax.broadcasted_iota(jnp.int32, s.shape, 0)
    vals, idxs = [], []
    for _ in range(k):
        m = jnp.max(s, axis=0, keepdims=True)
        i = jnp.min(jnp.where(s == m, iota, n), axis=0, keepdims=True)
        vals.append(m)
        idxs.append(i)
        s = jnp.where(iota == i, -jnp.inf, s)
    return jnp.concatenate(vals, axis=0), jnp.concatenate(idxs, axis=0)


def _peer_topk_kernel(x_ref, wq_ref, keys_ref, eidx_ref, gate_ref):
    xb = x_ref[...].astype(BF16)
    q = jnp.dot(xb, wq_ref[...], preferred_element_type=F32)
    nt = (((1,), (1,)), ((), ()))
    sv, si = [], []
    for p in range(2):
        qp = q[:, p * PEER_DKEY:(p + 1) * PEER_DKEY].astype(BF16)
        s = lax.dot_general(keys_ref[0, p], qp, nt, preferred_element_type=F32)
        v_, i_ = _topk_rows(s, PEER_TOPK)
        sv.append(v_)
        si.append(i_)
    cs, ce = [], []
    half = PEER_TOPK // 2
    for a in range(half):
        nb = PEER_TOPK if a == 0 else half
        cs.append(sv[0][a:a + 1] + sv[1][:nb])
        ce.append(si[0][a:a + 1] * PEER_NKEYS + si[1][:nb])
    cs.append(sv[0][half:] + sv[1][0:1])
    ce.append(si[0][half:] * PEER_NKEYS + si[1][0:1])
    cand_s = jnp.concatenate(cs, axis=0)
    cand_e = jnp.concatenate(ce, axis=0)
    fs, fpos = _topk_rows(cand_s, PEER_TOPK)
    iota = lax.broadcasted_iota(jnp.int32, cand_e.shape, 0)
    eidx = [jnp.max(jnp.where(iota == fpos[j:j + 1], cand_e, -1), axis=0, keepdims=True)
            for j in range(PEER_TOPK)]
    ex = jnp.exp(fs - fs[0:1])
    eidx_ref[0] = jnp.concatenate(eidx, axis=0)
    gate_ref[0] = ex / jnp.sum(ex, axis=0, keepdims=True)


def peer_topk(h, wq, sub_keys):
    N, D = h.shape
    T = PEER_TOPK_TOKENS
    wqb = wq.astype(BF16)
    kb = sub_keys.astype(BF16)
    eidx, gate = pl.pallas_call(
        _peer_topk_kernel,
        grid=(N // T, PEER_HEADS),
        in_specs=[pl.BlockSpec((T, D), lambda i, h_: (i, 0)),
                  pl.BlockSpec((D, 2 * PEER_DKEY), lambda i, h_: (0, h_)),
                  pl.BlockSpec((1, 2, PEER_NKEYS, PEER_DKEY), lambda i, h_: (h_, 0, 0, 0))],
        out_specs=[pl.BlockSpec((1, PEER_TOPK, T), lambda i, h_: (h_, 0, i)),
                   pl.BlockSpec((1, PEER_TOPK, T), lambda i, h_: (h_, 0, i))],
        out_shape=[jax.ShapeDtypeStruct((PEER_HEADS, PEER_TOPK, N), jnp.int32),
                   jax.ShapeDtypeStruct((PEER_HEADS, PEER_TOPK, N), F32)],
        compiler_params=pltpu.CompilerParams(vmem_limit_bytes=VMEM_LIMIT_BYTES),
    )(h, wqb, kb)
    return eidx.reshape(PEER_PICKS, N), gate.reshape(PEER_PICKS, N)


def pack_expert_table(tab):
    E = tab.shape[0]
    t = tab.astype(BF16).reshape(E, ROW_WORDS, 2, LANES)
    t = jnp.swapaxes(t, -1, -2)
    return lax.bitcast_convert_type(t, jnp.uint32).reshape(E * ROW_WORDS, LANES)


def _stage_rows(idx_ref, tab_ref, stage_ref, t):
    for k in range(PEER_PICKS):
        off = pl.multiple_of(idx_ref[k, t] * ROW_WORDS, ROW_WORDS)
        stage_ref[k * ROW_WORDS:(k + 1) * ROW_WORDS, :] = tab_ref[pl.ds(off, ROW_WORDS), :]
    return pltpu.bitcast(stage_ref[...], BF16)


def _peer_act_kernel(idx_ref, x_ref, gate_ref, tab_ref, seg_mask_ref, group_ref, w_ref,
                     stage_ref, rows_ref):
    T = x_ref.shape[0]
    nt = (((1,), (1,)), ((), ()))

    def token(t, carry):
        sb = _stage_rows(idx_ref, tab_ref, stage_ref, t)
        xs = jnp.concatenate(_split_bf16(x_ref[t], 2), axis=0)
        r = lax.dot_general(xs, sb, nt, preferred_element_type=F32)
        r = r * seg_mask_ref[...]
        rows_ref[t] = r[:SUBLANES] + r[SUBLANES:]
        return carry

    lax.fori_loop(0, T, token, 0)
    rows = rows_ref[...].reshape(T * SUBLANES, PEER_PICKS * ROW_SEGS)
    part = jnp.zeros((T * SUBLANES, PEER_PICKS), F32)
    for piece in _split_bf16(rows, 3):
        part = part + jnp.dot(piece, group_ref[...], preferred_element_type=F32)
    act = jnp.sum(part.reshape(T, SUBLANES, PEER_PICKS), axis=1)
    w_ref[...] = gate_ref[...] * (0.5 * act * (1.0 + lax.erf(act * (2.0 ** -0.5))))


def _peer_out_kernel(idx_ref, w_ref, tab_ref, expand_ref, seg_mask_ref, f_ref, stage_ref):
    T = w_ref.shape[0]

    def token(t, carry):
        sb = _stage_rows(idx_ref, tab_ref, stage_ref, t)
        w8 = w_ref[pl.ds(pl.multiple_of((t // SUBLANES) * SUBLANES, SUBLANES), SUBLANES), :]
        row = lax.broadcasted_iota(jnp.int32, w8.shape, 0) == t % SUBLANES
        wt = jnp.sum(jnp.where(row, w8, 0.0), axis=0, keepdims=True)
        lhs = jnp.concatenate([jnp.broadcast_to(p, (SUBLANES, PEER_PICKS)) for p in _split_bf16(wt, 2)],
                              axis=0)
        wrep = jnp.dot(lhs, expand_ref[...], preferred_element_type=F32)
        wsel = (wrep * seg_mask_ref[...]).astype(BF16)
        o = jnp.dot(wsel, sb, preferred_element_type=F32)
        f_ref[t] = o[:SUBLANES] + o[SUBLANES:]
        return carry

    lax.fori_loop(0, T, token, 0)


def _peer_constants():
    cols = np.arange(PEER_PICKS * ROW_SEGS)
    seg_mask = (cols[None, :] % ROW_SEGS == np.arange(2 * SUBLANES)[:, None] % SUBLANES)
    group = (cols[:, None] // ROW_SEGS == np.arange(PEER_PICKS)[None, :])
    return (jnp.asarray(seg_mask, F32), jnp.asarray(group, BF16), jnp.asarray(group.T, BF16))


def peer_ffn(h, wq, sub_keys, u, v):
    N, D = h.shape
    T = PEER_GATHER_TOKENS
    eidx, gate = peer_topk(h, wq, sub_keys)
    seg_mask, group, expand = _peer_constants()
    x3 = h.reshape(N, ROW_SEGS, LANES)
    rows = u.shape[0] * ROW_WORDS
    idx_spec = pl.BlockSpec((PEER_PICKS, T), lambda i: (0, i), memory_space=pltpu.SMEM)
    tab_spec = pl.BlockSpec((rows, LANES), lambda i: (0, 0), pipeline_mode=pl.Buffered(1))
    const = lambda shape: pl.BlockSpec(shape, lambda i: (0, 0))
    params = pltpu.CompilerParams(vmem_limit_bytes=VMEM_LIMIT_BYTES)
    w = pl.pallas_call(
        _peer_act_kernel,
        grid=(N // T,),
        in_specs=[idx_spec,
                  pl.BlockSpec((T, ROW_SEGS, LANES), lambda i: (i, 0, 0)),
                  pl.BlockSpec((T, PEER_PICKS), lambda i: (i, 0)),
                  tab_spec, const(seg_mask.shape), const(group.shape)],
        out_specs=pl.BlockSpec((T, PEER_PICKS), lambda i: (i, 0)),
        out_shape=jax.ShapeDtypeStruct((N, PEER_PICKS), F32),
        scratch_shapes=[pltpu.VMEM((PEER_PICKS * ROW_WORDS, LANES), jnp.uint32),
                        pltpu.VMEM((T, SUBLANES, PEER_PICKS * ROW_SEGS), F32)],
        compiler_params=params,
    )(eidx, x3, gate.T, pack_expert_table(u), seg_mask, group)
    f = pl.pallas_call(
        _peer_out_kernel,
        grid=(N // T,),
        in_specs=[idx_spec,
                  pl.BlockSpec((T, PEER_PICKS), lambda i: (i, 0)),
                  tab_spec, const(expand.shape), const(seg_mask.shape)],
        out_specs=pl.BlockSpec((T, ROW_SEGS, LANES), lambda i: (i, 0, 0)),
        out_shape=jax.ShapeDtypeStruct((N, ROW_SEGS, LANES), F32),
        scratch_shapes=[pltpu.VMEM((PEER_PICKS * ROW_WORDS, LANES), jnp.uint32)],
        compiler_params=params,
    )(eidx, w, pack_expert_table(v), expand, seg_mask)
    return f.reshape(N, D)


def hybrid_layer(x, xc, c, c_ctx, need_ctx, angs_mla, angs_swa,
                 norm1_g, norm2_g, w_ada, b_ada, w_in, na_rpb, ml_conv, ml_gate_b,
                 mla_q_norm, mla_w_uq, mla_kv_norm, mla_w_ukv, swa_sink, w_out,
                 peer_wq, peer_keys, peer_u, peer_v):
    B, T, D = x.shape
    H = GROUP_HEADS
    sh1, sc1, g1, sh2, sc2, g2 = jnp.split((jax.nn.silu(c) @ w_ada + b_ada)[:, None, :], 6, axis=-1)
    sh1c, sc1c, g1c, sh2c, sc2c, g2c = jnp.split(jax.nn.silu(c_ctx) @ w_ada + b_ada, 6, axis=-1)
    h = rmsnorm(x, norm1_g) * (1.0 + sc1) + sh1
    hc = rmsnorm(xc, norm1_g) * (1.0 + sc1c) + sh1c
    (na_q, na_k, na_v, ml_qk, ml_v, ml_o, ml_g,
     mla_cq, mla_ckv, mla_kr, sw_q, sw_k, sw_v) = split_cols(h @ w_in)
    (na_qc, na_kc, na_vc, ml_qkc, ml_vc, ml_oc, ml_gc,
     mla_cqc, mla_ckvc, mla_krc, sw_qc, sw_kc, sw_vc) = split_cols(hc @ w_in)
    attn_scale = HEAD_DIM ** -0.5
    mla_scale = (MLA_NOPE + MLA_ROPE) ** -0.5
    kc_a, vc_a = heads(na_kc, H), heads(na_vc, H)
    y_a = neighbourhood_attention(na_q, na_k, na_v, na_kc, na_vc, na_rpb)
    h_lat, h_ctx = mlstm_mixer((ml_qk, ml_v, ml_g), (ml_qkc, ml_vc, ml_gc), ml_conv, ml_gate_b)
    y_b = h_lat.reshape(B, T, GROUP_WIDTH) * jax.nn.sigmoid(ml_o)
    q_m, k_m, v_m = mla_project(mla_cq, mla_ckv, mla_kr, mla_q_norm, mla_w_uq, mla_kv_norm, mla_w_ukv, angs_mla)
    qc_m, kc_m, vc_m = mla_project(mla_cqc, mla_ckvc, mla_krc, mla_q_norm, mla_w_uq, mla_kv_norm, mla_w_ukv, None)
    y_c = block_dense_attention(q_m, jnp.concatenate([kc_m, k_m], axis=1), jnp.concatenate([vc_m, v_m], axis=1), mla_scale)
    kc_d, vc_d = heads(sw_kc, SWA_KV_HEADS), heads(sw_vc, SWA_KV_HEADS)
    y_d = window_attention(rope_2d(heads(sw_q, H), angs_swa), rope_2d(heads(sw_k, SWA_KV_HEADS), angs_swa),
                           heads(sw_v, SWA_KV_HEADS), kc_d, vc_d, swa_sink)
    x = x + g1 * (jnp.concatenate([y_a, y_b, y_c, y_d], axis=-1) @ w_out)
    h2 = rmsnorm(x, norm2_g) * (1.0 + sc2) + sh2
    if need_ctx:
        Tc = xc.shape[1]
        y_ctx = jnp.concatenate([
            ctx_attn(heads(na_qc, H), kc_a, vc_a, attn_scale),
            h_ctx.reshape(B, Tc, GROUP_WIDTH) * jax.nn.sigmoid(ml_oc),
            ctx_attn(qc_m, kc_m, vc_m, mla_scale),
            ctx_attn(heads(sw_qc, H), kc_d, vc_d, attn_scale, swa_sink)], axis=-1)
        xc = xc + g1c * (y_ctx @ w_out)
        h2c = rmsnorm(xc, norm2_g) * (1.0 + sc2c) + sh2c
        f = peer_ffn(jnp.concatenate([h2.reshape(B * T, D), h2c.reshape(B * Tc, D)], axis=0),
                     peer_wq, peer_keys, peer_u, peer_v)
        x = x + g2 * f[:B * T].reshape(B, T, D)
        xc = xc + g2c * f[B * T:].reshape(B, Tc, D)
        return x, xc
    x = x + g2 * peer_ffn(h2.reshape(B * T, D), peer_wq, peer_keys, peer_u, peer_v).reshape(B, T, D)
    return x, None


def _final_rmsnorm_kernel(x_ref, g_ref, o_ref):
    x = x_ref[...]
    o_ref[...] = x * lax.rsqrt(jnp.mean(x * x, axis=-1, keepdims=True) + EPS) * g_ref[...]


def final_rmsnorm(x, g):
    B, T, D = x.shape
    rows = 1024
    xf = x.reshape(B * T, D)
    out = pl.pallas_call(
        _final_rmsnorm_kernel,
        grid=(B * T // rows,),
        in_specs=[pl.BlockSpec((rows, D), lambda i: (i, 0)), pl.BlockSpec((1, D), lambda i: (0, 0))],
        out_specs=pl.BlockSpec((rows, D), lambda i: (i, 0)),
        out_shape=jax.ShapeDtypeStruct((B * T, D), x.dtype),
    )(xf, g.reshape(1, D))
    return out.reshape(B, T, D)


def kernel(x, c, ctx, c_ctx, norm1_g, norm2_g, w_ada, b_ada, w_in, na_rpb, ml_conv, ml_gate_b,
           mla_q_norm, mla_w_uq, mla_kv_norm, mla_w_ukv, swa_sink, w_out,
           peer_wq, peer_keys, peer_u, peer_v, final_norm_g):
    T = x.shape[1]
    angs_mla = axial_angles(T, MLA_ROPE)
    angs_swa = axial_angles(T, HEAD_DIM)
    xc = ctx
    for l in range(DEPTH):
        x, xc = hybrid_layer(x, xc, c, c_ctx, l < DEPTH - 1, angs_mla, angs_swa,
                             norm1_g[l], norm2_g[l], w_ada[l], b_ada[l], w_in[l], na_rpb[l],
                             ml_conv[l], ml_gate_b[l], mla_q_norm[l], mla_w_uq[l], mla_kv_norm[l],
                             mla_w_ukv[l], swa_sink[l], w_out[l], peer_wq[l], peer_keys[l],
                             peer_u[l], peer_v[l])
    return final_rmsnorm(x, final_norm_g)
```

```python
import functools

import jax
import jax.numpy as jnp
from jax import lax
import numpy as np
from jax.experimental import pallas as pl
from jax.experimental.pallas import tpu as pltpu

D_MODEL = 1024
BATCH = 2
SEQ = 16384
DEPTH = 2

CTX_LEN = 256
GRID_W = 64
N_MIXERS = 4
MIX_WIDTH = D_MODEL
GROUP_WIDTH = MIX_WIDTH // N_MIXERS
GROUP_HEADS = 4
HEAD_DIM = GROUP_WIDTH // GROUP_HEADS
NA_ROWS = 8
NA_COLS = 16
ML_CHUNK = 64
ML_CONV = 5
MLA_Q_RANK = 256
MLA_KV_RANK = 128
MLA_NOPE = 64
MLA_ROPE = 32
MLA_V = 64
SWA_KV_HEADS = 2
SWA_WINDOW = 128
ATTN_BLOCK = 128
PEER_HEADS = 8
PEER_NKEYS = 128
PEER_EXPERTS = PEER_NKEYS * PEER_NKEYS
PEER_DKEY = 128
PEER_TOPK = 16
PEER_BLOCK = 128
ROPE_BASE = 10000.0
EPS = 1e-6
IN_SIZES = (GROUP_WIDTH, GROUP_WIDTH, GROUP_WIDTH,
            2 * GROUP_WIDTH, GROUP_WIDTH, GROUP_WIDTH, 4 * GROUP_HEADS,
            MLA_Q_RANK, MLA_KV_RANK, MLA_ROPE,
            GROUP_WIDTH, SWA_KV_HEADS * HEAD_DIM, SWA_KV_HEADS * HEAD_DIM)
IN_WIDTH = sum(IN_SIZES)
F32 = jnp.float32


def rmsnorm(x, g):
    xf = x.astype(F32)
    y = xf * lax.rsqrt(jnp.mean(xf * xf, axis=-1, keepdims=True) + EPS) * g.astype(F32)
    return y.astype(x.dtype)


def heads(a, h):
    return a.reshape(a.shape[:-1] + (h, a.shape[-1] // h))


def split_cols(p):
    return jnp.split(p, np.cumsum(IN_SIZES)[:-1].tolist(), axis=-1)


def axial_angles(T, rot_dim):
    t = jnp.arange(T)
    row = (t // GRID_W).astype(F32)
    col = (t % GRID_W).astype(F32)
    half = rot_dim // 2
    inv = 1.0 / (ROPE_BASE ** (jnp.arange(0, half, 2, dtype=F32) / half))
    return row[:, None] * inv, col[:, None] * inv


def rope_1d(x, ang):
    cos = jnp.cos(ang)[None, :, None, :]
    sin = jnp.sin(ang)[None, :, None, :]
    x1, x2 = jnp.split(x.astype(F32), 2, axis=-1)
    return jnp.concatenate([x1 * cos - x2 * sin, x1 * sin + x2 * cos], axis=-1)


def rope_2d(x, angs):
    xr, xc = jnp.split(x, 2, axis=-1)
    return jnp.concatenate([rope_1d(xr, angs[0]), rope_1d(xc, angs[1])], axis=-1).astype(x.dtype)


def ctx_attn(q, k, v, scale, sink=None):
    rep = q.shape[2] // k.shape[2]
    k = jnp.repeat(k, rep, axis=2)
    v = jnp.repeat(v, rep, axis=2)
    s = jnp.einsum('bqhd,bkhd->bhqk', q, k).astype(F32) * scale
    nk = s.shape[-1]
    if sink is not None:
        s = jnp.concatenate([s, jnp.broadcast_to(sink.astype(F32)[None, :, None, None], s.shape[:-1] + (1,))], axis=-1)
    p = jax.nn.softmax(s, axis=-1)[..., :nk].astype(v.dtype)
    out = jnp.einsum('bhqk,bkhd->bqhd', p, v)
    return out.reshape(out.shape[:2] + (-1,))


NT_DIMS = (((1,), (1,)), ((), ()))
NA_SPAN = NA_ROWS * GRID_W


def _head_mask(width):
    rows = lax.broadcasted_iota(jnp.int32, (GROUP_HEADS * width, GROUP_WIDTH), 0) // width
    cols = lax.broadcasted_iota(jnp.int32, (GROUP_HEADS * width, GROUP_WIDTH), 1) // HEAD_DIM
    return (rows == cols).astype(F32)


def _na_kernel(q_ref, k_ref, v_ref, kc_ref, vc_ref, bias_ref, o_ref):
    r = pl.program_id(1)
    rows = pl.num_programs(1)
    rs = jnp.clip(r - NA_ROWS // 2, 0, rows - NA_ROWS)
    start = pl.multiple_of(rs * GRID_W, GRID_W)
    kw = k_ref[0, pl.ds(start, NA_SPAN), :]
    vw = v_ref[0, pl.ds(start, NA_SPAN), :]
    hm = _head_mask(GRID_W)
    q = q_ref[0] * (HEAD_DIM ** -0.5)
    q4 = (jnp.concatenate([q] * GROUP_HEADS, axis=0) * hm).astype(BF16)
    s_loc = lax.dot_general(q4, kw, NT_DIMS, preferred_element_type=F32) + bias_ref[rs - r + NA_ROWS - 1]
    s_ctx = lax.dot_general(q4, kc_ref[0], NT_DIMS, preferred_element_type=F32)
    m = jnp.maximum(jnp.max(s_loc, axis=-1, keepdims=True), jnp.max(s_ctx, axis=-1, keepdims=True))
    p_loc = jnp.exp(s_loc - m)
    p_ctx = jnp.exp(s_ctx - m)
    l = jnp.sum(p_loc, axis=-1, keepdims=True) + jnp.sum(p_ctx, axis=-1, keepdims=True)
    o = (jnp.dot(p_loc.astype(BF16), vw, preferred_element_type=F32)
         + jnp.dot(p_ctx.astype(BF16), vc_ref[0], preferred_element_type=F32)) * (hm / l)
    o_ref[0] = sum(o[h * GRID_W:(h + 1) * GRID_W] for h in range(GROUP_HEADS))


def _na_bias_table(rpb):
    c = np.arange(GRID_W)
    col_start = np.clip(c - NA_COLS // 2, 0, GRID_W - NA_COLS)
    valid = (c[None, :] >= col_start[:, None]) & (c[None, :] < col_start[:, None] + NA_COLS)
    dc = np.clip(c[None, :] - c[:, None] + NA_COLS - 1, 0, 2 * NA_COLS - 2)
    dr = np.arange(NA_ROWS)[:, None] + np.arange(NA_ROWS)[None, :]
    t = rpb.astype(F32)[:, dr][..., dc]
    t = jnp.where(valid[None, None, None], t, -jnp.inf)
    return jnp.transpose(t, (1, 0, 3, 2, 4)).reshape(NA_ROWS, GROUP_HEADS * GRID_W, NA_SPAN)


def neighbourhood_attention(q, k, v, kc, vc, rpb):
    B, T, C = q.shape
    rows = T // GRID_W
    n_ctx = kc.shape[1]
    bias = _na_bias_table(rpb)
    full = lambda n: pl.BlockSpec((1, n, C), lambda b, r: (b, 0, 0))
    return pl.pallas_call(
        _na_kernel,
        grid=(B, rows),
        in_specs=[pl.BlockSpec((1, GRID_W, C), lambda b, r: (b, r, 0)),
                  full(T), full(T), full(n_ctx), full(n_ctx),
                  pl.BlockSpec(bias.shape, lambda b, r: (0, 0, 0))],
        out_specs=pl.BlockSpec((1, GRID_W, C), lambda b, r: (b, r, 0)),
        out_shape=jax.ShapeDtypeStruct((B, T, C), F32),
        compiler_params=pltpu.CompilerParams(vmem_limit_bytes=VMEM_LIMIT_BYTES),
    )(q, k.astype(BF16), v.astype(BF16), kc.astype(BF16), vc.astype(BF16), bias)


def short_conv(a, w):
    T = a.shape[1]
    pad = w.shape[0] // 2
    ap = jnp.pad(a, ((0, 0), (pad, pad), (0, 0)))
    out = ap[:, :T] * w[0]
    for j in range(1, w.shape[0]):
        out = out + ap[:, j:j + T] * w[j]
    return out


def mlstm_scan(q, k, v, ig, lf, state):
    B, T, H, d = q.shape
    nc = T // ML_CHUNK

    def chunks(a):
        a = a.astype(F32).reshape((B, nc, ML_CHUNK) + a.shape[2:])
        return jnp.swapaxes(jnp.swapaxes(a, 0, 1), 2, 3)

    seen = jnp.tril(jnp.ones((ML_CHUNK, ML_CHUNK), dtype=bool))

    def step(carry, inp):
        C, n, m = carry
        qt, kt, vt, it, ft = inp
        b = jnp.cumsum(ft, axis=-1)
        d_log = jnp.where(seen, b[..., :, None] - b[..., None, :] + it[..., None, :], -jnp.inf)
        inter = b + m[..., None]
        m_t = jnp.maximum(inter, jnp.max(d_log, axis=-1))
        w = jnp.exp(d_log - m_t[..., None])
        a = jnp.exp(inter - m_t)
        s = jnp.einsum('bhtk,bhsk->bhts', qt, kt) * w
        num = jnp.einsum('bhts,bhsv->bhtv', s, vt) + a[..., None] * jnp.einsum('bhtk,bhkv->bhtv', qt, C)
        den = jnp.sum(s, axis=-1) + a * jnp.einsum('bhtk,bhk->bht', qt, n)
        h = num / jnp.maximum(jnp.abs(den), jnp.exp(-m_t))[..., None]
        g = b[..., -1:] - b + it
        m_new = jnp.maximum(b[..., -1] + m, jnp.max(g, axis=-1))
        wk = jnp.exp(g - m_new[..., None])
        decay = jnp.exp(b[..., -1] + m - m_new)
        C = decay[..., None, None] * C + jnp.einsum('bhs,bhsk,bhsv->bhkv', wk, kt, vt)
        n = decay[..., None] * n + jnp.einsum('bhs,bhsk->bhk', wk, kt)
        return (C, n, m_new), h

    state, h = lax.scan(step, state, (chunks(q), chunks(k), chunks(v), chunks(ig), chunks(lf)))
    h = jnp.swapaxes(jnp.swapaxes(h, 2, 3), 0, 1).reshape(B, T, H, d)
    return h.astype(v.dtype), state


def mlstm_prep(qk, v, gates, conv_w, gate_b):
    qk = jax.nn.silu(short_conv(qk, conv_w))
    q, k = jnp.split(qk, 2, axis=-1)
    g = (gates + gate_b).astype(F32)
    i_f, f_f, i_b, f_b = jnp.split(g, 4, axis=-1)
    return (heads(q, GROUP_HEADS) * HEAD_DIM ** -0.5, heads(k, GROUP_HEADS), heads(v, GROUP_HEADS),
            (i_f, jax.nn.log_sigmoid(f_f), i_b, jax.nn.log_sigmoid(f_b)))


def mlstm_mixer(lat, ctx, conv_w, gate_b):
    ql, kl, vl, gl = mlstm_prep(lat[0], lat[1], lat[2], conv_w, gate_b)
    qc, kc, vc, gc = mlstm_prep(ctx[0], ctx[1], ctx[2], conv_w, gate_b)
    B, _, H, d = ql.shape
    st0 = (jnp.zeros((B, H, d, d), F32), jnp.zeros((B, H, d), F32), jnp.zeros((B, H), F32))

    def rev(a):
        return a[:, ::-1]

    hc_f, st_f = mlstm_scan(qc, kc, vc, gc[0], gc[1], st0)
    hl_f, _ = mlstm_scan(ql, kl, vl, gl[0], gl[1], st_f)
    hc_b, st_b = mlstm_scan(rev(qc), rev(kc), rev(vc), rev(gc[2]), rev(gc[3]), st0)
    hl_b, _ = mlstm_scan(rev(ql), rev(kl), rev(vl), rev(gl[2]), rev(gl[3]), st_b)
    return hl_f + rev(hl_b), hc_f + rev(hc_b)


def mla_project(cq, ckv, kr, q_norm, w_uq, kv_norm, w_ukv, angs):
    q = heads(rmsnorm(cq, q_norm) @ w_uq, GROUP_HEADS)
    kv = heads(rmsnorm(ckv, kv_norm) @ w_ukv, GROUP_HEADS)
    q_nope, q_rope = q[..., :MLA_NOPE], q[..., MLA_NOPE:]
    k_nope, v = kv[..., :MLA_NOPE], kv[..., MLA_NOPE:]
    k_rope = kr[:, :, None, :]
    if angs is not None:
        q_rope = rope_2d(q_rope, angs)
        k_rope = rope_2d(k_rope, angs)
    k_rope = jnp.broadcast_to(k_rope, k_nope.shape[:-1] + (MLA_ROPE,))
    return (jnp.concatenate([q_nope, q_rope], axis=-1), jnp.concatenate([k_nope, k_rope], axis=-1), v)


LOG2_E = 1.4426950408889634
DENSE_Q_TILE = 1024
DENSE_Q_SUB = 256
DENSE_Q_UNROLL = 4
DENSE_K_TILE_MAX = 1664


def _dense_attn_kernel(q_ref, k_ref, v_ref, o_ref, m_ref, l_ref, acc_ref, *, scale):
    j = pl.program_id(3)

    @pl.when(j == 0)
    def _():
        m_ref[...] = jnp.full(m_ref.shape, -jnp.inf, F32)
        l_ref[...] = jnp.zeros(l_ref.shape, F32)
        acc_ref[...] = jnp.zeros(acc_ref.shape, F32)

    def rows(i, carry):
        for u in range(DENSE_Q_UNROLL):
            r = pl.ds(pl.multiple_of((i * DENSE_Q_UNROLL + u) * DENSE_Q_SUB, DENSE_Q_SUB), DENSE_Q_SUB)
            s = lax.dot_general(q_ref[0, 0, r, :], k_ref[0, 0], NT_DIMS,
                                preferred_element_type=F32) * (scale * LOG2_E)
            m_prev = m_ref[r, :]
            m_new = jnp.maximum(m_prev, jnp.max(s, axis=-1, keepdims=True))
            alpha = jnp.exp2(m_prev - m_new)
            p = jnp.exp2(s - m_new)
            l_ref[r, :] = alpha * l_ref[r, :] + jnp.sum(p, axis=-1, keepdims=True)
            acc_ref[r, :] = alpha * acc_ref[r, :] + jnp.dot(p.astype(BF16), v_ref[0, 0],
                                                            preferred_element_type=F32)
            m_ref[r, :] = m_new
        return carry

    lax.fori_loop(0, q_ref.shape[2] // (DENSE_Q_SUB * DENSE_Q_UNROLL), rows, 0)

    @pl.when(j == pl.num_programs(3) - 1)
    def _():
        o_ref[0, 0] = acc_ref[...] / l_ref[...]


def block_dense_attention(q, k_all, v_all, scale):
    B, T, H, dq = q.shape
    NK, dv = k_all.shape[1], v_all.shape[-1]
    tq = min(DENSE_Q_TILE, T)
    tk = max(t for t in range(LANES, DENSE_K_TILE_MAX + 1, LANES) if NK % t == 0)
    hm = lambda a: jnp.swapaxes(a, 1, 2).astype(BF16)
    out = pl.pallas_call(
        functools.partial(_dense_attn_kernel, scale=scale),
        grid=(B, H, T // tq, NK // tk),
        in_specs=[pl.BlockSpec((1, 1, tq, dq), lambda b, h, i, j: (b, h, i, 0)),
                  pl.BlockSpec((1, 1, tk, dq), lambda b, h, i, j: (b, h, j, 0)),
                  pl.BlockSpec((1, 1, tk, dv), lambda b, h, i, j: (b, h, j, 0))],
        out_specs=pl.BlockSpec((1, 1, tq, dv), lambda b, h, i, j: (b, h, i, 0)),
        out_shape=jax.ShapeDtypeStruct((B, H, T, dv), F32),
        scratch_shapes=[pltpu.VMEM((tq, 1), F32), pltpu.VMEM((tq, 1), F32), pltpu.VMEM((tq, dv), F32)],
        compiler_params=pltpu.CompilerParams(vmem_limit_bytes=VMEM_LIMIT_BYTES),
    )(hm(q), hm(k_all), hm(v_all))
    return jnp.swapaxes(out, 1, 2).reshape(B, T, H * dv)


def window_attention(q, k, v, kc, vc, sink):
    B, T, H, d = q.shape
    KVH = k.shape[2]
    G = H // KVH
    nb = T // ATTN_BLOCK
    span = ATTN_BLOCK + 2 * SWA_WINDOW
    n_ctx = kc.shape[1]
    scale = d ** -0.5
    padw = ((0, 0), (SWA_WINDOW, SWA_WINDOW), (0, 0), (0, 0))
    kp = jnp.pad(k, padw)
    vp = jnp.pad(v, padw)
    start = jnp.arange(nb) * ATTN_BLOCK
    idx = start[:, None] + jnp.arange(span)[None, :]
    kb = kp[:, idx]
    vb = vp[:, idx]
    key_pos = idx - SWA_WINDOW
    q_pos = start[:, None] + jnp.arange(ATTN_BLOCK)[None, :]
    mask = ((jnp.abs(q_pos[:, :, None] - key_pos[:, None, :]) <= SWA_WINDOW)
            & (key_pos >= 0)[:, None, :] & (key_pos < T)[:, None, :])
    qb = q.reshape(B, nb, ATTN_BLOCK, KVH, G, d)
    s_loc = jnp.einsum('bnqhgd,bnkhd->bnhgqk', qb, kb).astype(F32) * scale
    s_loc = jnp.where(mask[None, :, None, None], s_loc, -jnp.inf)
    s_ctx = jnp.einsum('bnqhgd,bkhd->bnhgqk', qb, kc).astype(F32) * scale
    s_sink = jnp.broadcast_to(sink.astype(F32).reshape(1, 1, KVH, G, 1, 1), s_loc.shape[:-1] + (1,))
    p = jax.nn.softmax(jnp.concatenate([s_loc, s_ctx, s_sink], axis=-1), axis=-1).astype(v.dtype)
    out = (jnp.einsum('bnhgqk,bnkhd->bnqhgd', p[..., :span], vb)
           + jnp.einsum('bnhgqk,bkhd->bnqhgd', p[..., span:span + n_ctx], vc))
    return out.reshape(B, T, H * d)


BF16 = jnp.bfloat16
LANES = 128
SUBLANES = 8
ROW_SEGS = D_MODEL // LANES
ROW_WORDS = ROW_SEGS // 2
PEER_PICKS = PEER_HEADS * PEER_TOPK
PEER_TOPK_TOKENS = 256
PEER_GATHER_TOKENS = 128
PEER_ACT_UNROLL = 4
VMEM_LIMIT_BYTES = 56 * 1024 * 1024


def _split_bf16(x, parts):
    out = []
    for _ in range(parts):
        p = x.astype(BF16)
        out.append(p)
        x = x - p.astype(F32)
    return out


def _topk_rows(s, k):
    n = s.shape[0]
    iota = lax.broadcasted_iota(jnp.int32, s.shape, 0)
    vals, idxs = [], []
    for _ in range(k):
        m = jnp.max(s, axis=0, keepdims=True)
        i = jnp.min(jnp.where(s == m, iota, n), axis=0, keepdims=True)
        vals.append(m)
        idxs.append(i)
        s = jnp.where(iota == i, -jnp.inf, s)
    return jnp.concatenate(vals, axis=0), jnp.concatenate(idxs, axis=0)


def _peer_topk_kernel(x_ref, wq_ref, keys_ref, eidx_ref, gate_ref):
    xb = x_ref[...].astype(BF16)
    q = jnp.dot(xb, wq_ref[...], preferred_element_type=F32)
    nt = (((1,), (1,)), ((), ()))
    sv, si = [], []
    for p in range(2):
        qp = q[:, p * PEER_DKEY:(p + 1) * PEER_DKEY].astype(BF16)
        s = lax.dot_general(keys_ref[0, p], qp, nt, preferred_element_type=F32)
        v_, i_ = _topk_rows(s, PEER_TOPK)
        sv.append(v_)
        si.append(i_)
    cs, ce = [], []
    half = PEER_TOPK // 2
    for a in range(half):
        nb = PEER_TOPK if a == 0 else half
        cs.append(sv[0][a:a + 1] + sv[1][:nb])
        ce.append(si[0][a:a + 1] * PEER_NKEYS + si[1][:nb])
    cs.append(sv[0][half:] + sv[1][0:1])
    ce.append(si[0][half:] * PEER_NKEYS + si[1][0:1])
    cand_s = jnp.concatenate(cs, axis=0)
    cand_e = jnp.concatenate(ce, axis=0)
    fs, fpos = _topk_rows(cand_s, PEER_TOPK)
    iota = lax.broadcasted_iota(jnp.int32, cand_e.shape, 0)
    eidx = [jnp.max(jnp.where(iota == fpos[j:j + 1], cand_e, -1), axis=0, keepdims=True)
            for j in range(PEER_TOPK)]
    ex = jnp.exp(fs - fs[0:1])
    eidx_ref[0] = jnp.concatenate(eidx, axis=0)
    gate_ref[0] = ex / jnp.sum(ex, axis=0, keepdims=True)


def peer_topk(h, wq, sub_keys):
    N, D = h.shape
    T = PEER_TOPK_TOKENS
    wqb = wq.astype(BF16)
    kb = sub_keys.astype(BF16)
    eidx, gate = pl.pallas_call(
        _peer_topk_kernel,
        grid=(N // T, PEER_HEADS),
        in_specs=[pl.BlockSpec((T, D), lambda i, h_: (i, 0)),
                  pl.BlockSpec((D, 2 * PEER_DKEY), lambda i, h_: (0, h_)),
                  pl.BlockSpec((1, 2, PEER_NKEYS, PEER_DKEY), lambda i, h_: (h_, 0, 0, 0))],
        out_specs=[pl.BlockSpec((1, PEER_TOPK, T), lambda i, h_: (h_, 0, i)),
                   pl.BlockSpec((1, PEER_TOPK, T), lambda i, h_: (h_, 0, i))],
        out_shape=[jax.ShapeDtypeStruct((PEER_HEADS, PEER_TOPK, N), jnp.int32),
                   jax.ShapeDtypeStruct((PEER_HEADS, PEER_TOPK, N), F32)],
        compiler_params=pltpu.CompilerParams(vmem_limit_bytes=VMEM_LIMIT_BYTES),
    )(h, wqb, kb)
    return eidx.reshape(PEER_PICKS, N), gate.reshape(PEER_PICKS, N)


def pack_expert_table(tab):
    E = tab.shape[0]
    t = tab.astype(BF16).reshape(E, ROW_WORDS, 2, LANES)
    t = jnp.swapaxes(t, -1, -2)
    return lax.bitcast_convert_type(t, jnp.uint32).reshape(E * ROW_WORDS, LANES)


def _stage_rows(idx_ref, tab_ref, stage_ref, t):
    for k in range(PEER_PICKS):
        off = pl.multiple_of(idx_ref[t, k], ROW_WORDS)
        stage_ref[k * ROW_WORDS:(k + 1) * ROW_WORDS, :] = tab_ref[pl.ds(off, ROW_WORDS), :]
    return pltpu.bitcast(stage_ref[...], BF16)


def _peer_act_kernel(idx_ref, x_ref, gate_ref, tab_ref, seg_mask_ref, group_ref, w_ref,
                     stage_ref, rows_ref):
    T = x_ref.shape[0]
    U = stage_ref.shape[0]

    def tokens(g, carry):
        for j in range(U):
            t = g * U + j
            sb = _stage_rows(idx_ref, tab_ref, stage_ref.at[j], t)
            xs = jnp.concatenate(_split_bf16(x_ref[t], 2), axis=0)
            r = lax.dot_general(xs, sb, NT_DIMS, preferred_element_type=F32)
            r = r * seg_mask_ref[...]
            rows_ref[t] = r[:SUBLANES] + r[SUBLANES:]
        return carry

    lax.fori_loop(0, T // U, tokens, 0)
    rows = rows_ref[...].reshape(T * SUBLANES, PEER_PICKS * ROW_SEGS)
    part = jnp.zeros((T * SUBLANES, PEER_PICKS), F32)
    for piece in _split_bf16(rows, 3):
        part = part + jnp.dot(piece, group_ref[...], preferred_element_type=F32)
    act = jnp.sum(part.reshape(T, SUBLANES, PEER_PICKS), axis=1)
    w_ref[...] = gate_ref[...] * (0.5 * act * (1.0 + lax.erf(act * (2.0 ** -0.5))))


def _peer_out_kernel(idx_ref, w_ref, tab_ref, expand_ref, seg_mask_ref, f_ref, stage_ref):
    T = w_ref.shape[0]
    U = stage_ref.shape[0]

    def tokens(g, carry):
        w8 = w_ref[pl.ds(pl.multiple_of(g * U, U), U), :]
        for j in range(U):
            t = g * U + j
            sb = _stage_rows(idx_ref, tab_ref, stage_ref.at[j], t)
            lhs = jnp.concatenate([jnp.broadcast_to(p, (SUBLANES, PEER_PICKS))
                                   for p in _split_bf16(w8[j:j + 1], 2)], axis=0)
            wrep = jnp.dot(lhs, expand_ref[...], preferred_element_type=F32)
            wsel = (wrep * seg_mask_ref[...]).astype(BF16)
            o = jnp.dot(wsel, sb, preferred_element_type=F32)
            f_ref[t] = o[:SUBLANES] + o[SUBLANES:]
        return carry

    lax.fori_loop(0, T // U, tokens, 0)


def _peer_constants():
    cols = np.arange(PEER_PICKS * ROW_SEGS)
    seg_mask = (cols[None, :] % ROW_SEGS == np.arange(2 * SUBLANES)[:, None] % SUBLANES)
    group = (cols[:, None] // ROW_SEGS == np.arange(PEER_PICKS)[None, :])
    return (jnp.asarray(seg_mask, F32), jnp.asarray(group, BF16), jnp.asarray(group.T, BF16))


def peer_ffn(h, wq, sub_keys, u, v):
    N, D = h.shape
    T = PEER_GATHER_TOKENS
    eidx, gate = peer_topk(h, wq, sub_keys)
    seg_mask, group, expand = _peer_constants()
    x3 = h.reshape(N, ROW_SEGS, LANES)
    rows = u.shape[0] * ROW_WORDS
    offs = eidx.T * ROW_WORDS
    idx_spec = pl.BlockSpec((T, PEER_PICKS), lambda i: (i, 0), memory_space=pltpu.SMEM)
    tab_spec = pl.BlockSpec((rows, LANES), lambda i: (0, 0), pipeline_mode=pl.Buffered(1))
    const = lambda shape: pl.BlockSpec(shape, lambda i: (0, 0))
    params = pltpu.CompilerParams(vmem_limit_bytes=VMEM_LIMIT_BYTES)
    w = pl.pallas_call(
        _peer_act_kernel,
        grid=(N // T,),
        in_specs=[idx_spec,
                  pl.BlockSpec((T, ROW_SEGS, LANES), lambda i: (i, 0, 0)),
                  pl.BlockSpec((T, PEER_PICKS), lambda i: (i, 0)),
                  tab_spec, const(seg_mask.shape), const(group.shape)],
        out_specs=pl.BlockSpec((T, PEER_PICKS), lambda i: (i, 0)),
        out_shape=jax.ShapeDtypeStruct((N, PEER_PICKS), F32),
        scratch_shapes=[pltpu.VMEM((PEER_ACT_UNROLL, PEER_PICKS * ROW_WORDS, LANES), jnp.uint32),
                        pltpu.VMEM((T, SUBLANES, PEER_PICKS * ROW_SEGS), F32)],
        compiler_params=params,
    )(offs, x3, gate.T, pack_expert_table(u), seg_mask, group)
    f = pl.pallas_call(
        _peer_out_kernel,
        grid=(N // T,),
        in_specs=[idx_spec,
                  pl.BlockSpec((T, PEER_PICKS), lambda i: (i, 0)),
                  tab_spec, const(expand.shape), const(seg_mask.shape)],
        out_specs=pl.BlockSpec((T, ROW_SEGS, LANES), lambda i: (i, 0, 0)),
        out_shape=jax.ShapeDtypeStruct((N, ROW_SEGS, LANES), F32),
        scratch_shapes=[pltpu.VMEM((SUBLANES, PEER_PICKS * ROW_WORDS, LANES), jnp.uint32)],
        compiler_params=params,
    )(offs, w, pack_expert_table(v), expand, seg_mask)
    return f.reshape(N, D)


def hybrid_layer(x, xc, c, c_ctx, need_ctx, angs_mla, angs_swa,
                 norm1_g, norm2_g, w_ada, b_ada, w_in, na_rpb, ml_conv, ml_gate_b,
                 mla_q_norm, mla_w_uq, mla_kv_norm, mla_w_ukv, swa_sink, w_out,
                 peer_wq, peer_keys, peer_u, peer_v):
    B, T, D = x.shape
    H = GROUP_HEADS
    sh1, sc1, g1, sh2, sc2, g2 = jnp.split((jax.nn.silu(c) @ w_ada + b_ada)[:, None, :], 6, axis=-1)
    sh1c, sc1c, g1c, sh2c, sc2c, g2c = jnp.split(jax.nn.silu(c_ctx) @ w_ada + b_ada, 6, axis=-1)
    h = rmsnorm(x, norm1_g) * (1.0 + sc1) + sh1
    hc = rmsnorm(xc, norm1_g) * (1.0 + sc1c) + sh1c
    (na_q, na_k, na_v, ml_qk, ml_v, ml_o, ml_g,
     mla_cq, mla_ckv, mla_kr, sw_q, sw_k, sw_v) = split_cols(h @ w_in)
    (na_qc, na_kc, na_vc, ml_qkc, ml_vc, ml_oc, ml_gc,
     mla_cqc, mla_ckvc, mla_krc, sw_qc, sw_kc, sw_vc) = split_cols(hc @ w_in)
    attn_scale = HEAD_DIM ** -0.5
    mla_scale = (MLA_NOPE + MLA_ROPE) ** -0.5
    kc_a, vc_a = heads(na_kc, H), heads(na_vc, H)
    y_a = neighbourhood_attention(na_q, na_k, na_v, na_kc, na_vc, na_rpb)
    h_lat, h_ctx = mlstm_mixer((ml_qk, ml_v, ml_g), (ml_qkc, ml_vc, ml_gc), ml_conv, ml_gate_b)
    y_b = h_lat.reshape(B, T, GROUP_WIDTH) * jax.nn.sigmoid(ml_o)
    q_m, k_m, v_m = mla_project(mla_cq, mla_ckv, mla_kr, mla_q_norm, mla_w_uq, mla_kv_norm, mla_w_ukv, angs_mla)
    qc_m, kc_m, vc_m = mla_project(mla_cqc, mla_ckvc, mla_krc, mla_q_norm, mla_w_uq, mla_kv_norm, mla_w_ukv, None)
    y_c = block_dense_attention(q_m, jnp.concatenate([kc_m, k_m], axis=1), jnp.concatenate([vc_m, v_m], axis=1), mla_scale)
    kc_d, vc_d = heads(sw_kc, SWA_KV_HEADS), heads(sw_vc, SWA_KV_HEADS)
    y_d = window_attention(rope_2d(heads(sw_q, H), angs_swa), rope_2d(heads(sw_k, SWA_KV_HEADS), angs_swa),
                           heads(sw_v, SWA_KV_HEADS), kc_d, vc_d, swa_sink)
    x = x + g1 * (jnp.concatenate([y_a, y_b, y_c, y_d], axis=-1) @ w_out)
    h2 = rmsnorm(x, norm2_g) * (1.0 + sc2) + sh2
    if need_ctx:
        Tc = xc.shape[1]
        y_ctx = jnp.concatenate([
            ctx_attn(heads(na_qc, H), kc_a, vc_a, attn_scale),
            h_ctx.reshape(B, Tc, GROUP_WIDTH) * jax.nn.sigmoid(ml_oc),
            ctx_attn(qc_m, kc_m, vc_m, mla_scale),
            ctx_attn(heads(sw_qc, H), kc_d, vc_d, attn_scale, swa_sink)], axis=-1)
        xc = xc + g1c * (y_ctx @ w_out)
        h2c = rmsnorm(xc, norm2_g) * (1.0 + sc2c) + sh2c
        f = peer_ffn(jnp.concatenate([h2.reshape(B * T, D), h2c.reshape(B * Tc, D)], axis=0),
                     peer_wq, peer_keys, peer_u, peer_v)
        x = x + g2 * f[:B * T].reshape(B, T, D)
        xc = xc + g2c * f[B * T:].reshape(B, Tc, D)
        return x, xc
    x = x + g2 * peer_ffn(h2.reshape(B * T, D), peer_wq, peer_keys, peer_u, peer_v).reshape(B, T, D)
    return x, None


def _final_rmsnorm_kernel(x_ref, g_ref, o_ref):
    x = x_ref[...]
    o_ref[...] = x * lax.rsqrt(jnp.mean(x * x, axis=-1, keepdims=True) + EPS) * g_ref[...]


def final_rmsnorm(x, g):
    B, T, D = x.shape
    rows = 1024
    xf = x.reshape(B * T, D)
    out = pl.pallas_call(
        _final_rmsnorm_kernel,
        grid=(B * T // rows,),
        in_specs=[pl.BlockSpec((rows, D), lambda i: (i, 0)), pl.BlockSpec((1, D), lambda i: (0, 0))],
        out_specs=pl.BlockSpec((rows, D), lambda i: (i, 0)),
        out_shape=jax.ShapeDtypeStruct((B * T, D), x.dtype),
    )(xf, g.reshape(1, D))
    return out.reshape(B, T, D)


def kernel(x, c, ctx, c_ctx, norm1_g, norm2_g, w_ada, b_ada, w_in, na_rpb, ml_conv, ml_gate_b,
           mla_q_norm, mla_w_uq, mla_kv_norm, mla_w_ukv, swa_sink, w_out,
           peer_wq, peer_keys, peer_u, peer_v, final_norm_g):
    T = x.shape[1]
    angs_mla = axial_angles(T, MLA_ROPE)
    angs_swa = axial_angles(T, HEAD_DIM)
    xc = ctx
    for l in range(DEPTH):
        x, xc = hybrid_layer(x, xc, c, c_ctx, l < DEPTH - 1, angs_mla, angs_swa,
                             norm1_g[l], norm2_g[l], w_ada[l], b_ada[l], w_in[l], na_rpb[l],
                             ml_conv[l], ml_gate_b[l], mla_q_norm[l], mla_w_uq[l], mla_kv_norm[l],
                             mla_w_ukv[l], swa_sink[l], w_out[l], peer_wq[l], peer_keys[l],
                             peer_u[l], peer_v[l])
    return final_rmsnorm(x, final_norm_g)
```

```python
import functools

import jax
import jax.numpy as jnp
from jax import lax
import numpy as np
from jax.experimental import pallas as pl
from jax.experimental.pallas import tpu as pltpu

D_MODEL = 1024
BATCH = 2
SEQ = 16384
DEPTH = 2

CTX_LEN = 256
GRID_W = 64
N_MIXERS = 4
MIX_WIDTH = D_MODEL
GROUP_WIDTH = MIX_WIDTH // N_MIXERS
GROUP_HEADS = 4
HEAD_DIM = GROUP_WIDTH // GROUP_HEADS
NA_ROWS = 8
NA_COLS = 16
ML_CHUNK = 64
ML_CONV = 5
MLA_Q_RANK = 256
MLA_KV_RANK = 128
MLA_NOPE = 64
MLA_ROPE = 32
MLA_V = 64
SWA_KV_HEADS = 2
SWA_WINDOW = 128
ATTN_BLOCK = 128
PEER_HEADS = 8
PEER_NKEYS = 128
PEER_EXPERTS = PEER_NKEYS * PEER_NKEYS
PEER_DKEY = 128
PEER_TOPK = 16
PEER_BLOCK = 128
ROPE_BASE = 10000.0
EPS = 1e-6
IN_SIZES = (GROUP_WIDTH, GROUP_WIDTH, GROUP_WIDTH,
            2 * GROUP_WIDTH, GROUP_WIDTH, GROUP_WIDTH, 4 * GROUP_HEADS,
            MLA_Q_RANK, MLA_KV_RANK, MLA_ROPE,
            GROUP_WIDTH, SWA_KV_HEADS * HEAD_DIM, SWA_KV_HEADS * HEAD_DIM)
IN_WIDTH = sum(IN_SIZES)
F32 = jnp.float32


def rmsnorm(x, g):
    xf = x.astype(F32)
    y = xf * lax.rsqrt(jnp.mean(xf * xf, axis=-1, keepdims=True) + EPS) * g.astype(F32)
    return y.astype(x.dtype)


def heads(a, h):
    return a.reshape(a.shape[:-1] + (h, a.shape[-1] // h))


def split_cols(p):
    return jnp.split(p, np.cumsum(IN_SIZES)[:-1].tolist(), axis=-1)


def axial_angles(T, rot_dim):
    t = jnp.arange(T)
    row = (t // GRID_W).astype(F32)
    col = (t % GRID_W).astype(F32)
    half = rot_dim // 2
    inv = 1.0 / (ROPE_BASE ** (jnp.arange(0, half, 2, dtype=F32) / half))
    return row[:, None] * inv, col[:, None] * inv


def rope_1d(x, ang):
    cos = jnp.cos(ang)[None, :, None, :]
    sin = jnp.sin(ang)[None, :, None, :]
    x1, x2 = jnp.split(x.astype(F32), 2, axis=-1)
    return jnp.concatenate([x1 * cos - x2 * sin, x1 * sin + x2 * cos], axis=-1)


def rope_2d(x, angs):
    xr, xc = jnp.split(x, 2, axis=-1)
    return jnp.concatenate([rope_1d(xr, angs[0]), rope_1d(xc, angs[1])], axis=-1).astype(x.dtype)


def ctx_attn(q, k, v, scale, sink=None):
    rep = q.shape[2] // k.shape[2]
    k = jnp.repeat(k, rep, axis=2)
    v = jnp.repeat(v, rep, axis=2)
    s = jnp.einsum('bqhd,bkhd->bhqk', q, k).astype(F32) * scale
    nk = s.shape[-1]
    if sink is not None:
        s = jnp.concatenate([s, jnp.broadcast_to(sink.astype(F32)[None, :, None, None], s.shape[:-1] + (1,))], axis=-1)
    p = jax.nn.softmax(s, axis=-1)[..., :nk].astype(v.dtype)
    out = jnp.einsum('bhqk,bkhd->bqhd', p, v)
    return out.reshape(out.shape[:2] + (-1,))


NT_DIMS = (((1,), (1,)), ((), ()))
NA_SPAN = NA_ROWS * GRID_W


def _head_mask(width):
    rows = lax.broadcasted_iota(jnp.int32, (GROUP_HEADS * width, GROUP_WIDTH), 0) // width
    cols = lax.broadcasted_iota(jnp.int32, (GROUP_HEADS * width, GROUP_WIDTH), 1) // HEAD_DIM
    return (rows == cols).astype(F32)


def _na_kernel(q_ref, k_ref, v_ref, kc_ref, vc_ref, bias_ref, o_ref):
    r = pl.program_id(1)
    rows = pl.num_programs(1)
    rs = jnp.clip(r - NA_ROWS // 2, 0, rows - NA_ROWS)
    start = pl.multiple_of(rs * GRID_W, GRID_W)
    kw = k_ref[0, pl.ds(start, NA_SPAN), :]
    vw = v_ref[0, pl.ds(start, NA_SPAN), :]
    hm = _head_mask(GRID_W)
    q = q_ref[0] * (HEAD_DIM ** -0.5)
    q4 = (jnp.concatenate([q] * GROUP_HEADS, axis=0) * hm).astype(BF16)
    s_loc = lax.dot_general(q4, kw, NT_DIMS, preferred_element_type=F32) + bias_ref[rs - r + NA_ROWS - 1]
    s_ctx = lax.dot_general(q4, kc_ref[0], NT_DIMS, preferred_element_type=F32)
    m = jnp.maximum(jnp.max(s_loc, axis=-1, keepdims=True), jnp.max(s_ctx, axis=-1, keepdims=True))
    p_loc = jnp.exp(s_loc - m)
    p_ctx = jnp.exp(s_ctx - m)
    l = jnp.sum(p_loc, axis=-1, keepdims=True) + jnp.sum(p_ctx, axis=-1, keepdims=True)
    o = (jnp.dot(p_loc.astype(BF16), vw, preferred_element_type=F32)
         + jnp.dot(p_ctx.astype(BF16), vc_ref[0], preferred_element_type=F32)) * (hm / l)
    o_ref[0] = sum(o[h * GRID_W:(h + 1) * GRID_W] for h in range(GROUP_HEADS))


def _na_bias_table(rpb):
    c = np.arange(GRID_W)
    col_start = np.clip(c - NA_COLS // 2, 0, GRID_W - NA_COLS)
    valid = (c[None, :] >= col_start[:, None]) & (c[None, :] < col_start[:, None] + NA_COLS)
    dc = np.clip(c[None, :] - c[:, None] + NA_COLS - 1, 0, 2 * NA_COLS - 2)
    dr = np.arange(NA_ROWS)[:, None] + np.arange(NA_ROWS)[None, :]
    t = rpb.astype(F32)[:, dr][..., dc]
    t = jnp.where(valid[None, None, None], t, -jnp.inf)
    return jnp.transpose(t, (1, 0, 3, 2, 4)).reshape(NA_ROWS, GROUP_HEADS * GRID_W, NA_SPAN)


def neighbourhood_attention(q, k, v, kc, vc, rpb):
    B, T, C = q.shape
    rows = T // GRID_W
    n_ctx = kc.shape[1]
    bias = _na_bias_table(rpb)
    full = lambda n: pl.BlockSpec((1, n, C), lambda b, r: (b, 0, 0))
    return pl.pallas_call(
        _na_kernel,
        grid=(B, rows),
        in_specs=[pl.BlockSpec((1, GRID_W, C), lambda b, r: (b, r, 0)),
                  full(T), full(T), full(n_ctx), full(n_ctx),
                  pl.BlockSpec(bias.shape, lambda b, r: (0, 0, 0))],
        out_specs=pl.BlockSpec((1, GRID_W, C), lambda b, r: (b, r, 0)),
        out_shape=jax.ShapeDtypeStruct((B, T, C), F32),
        compiler_params=pltpu.CompilerParams(vmem_limit_bytes=VMEM_LIMIT_BYTES),
    )(q, k.astype(BF16), v.astype(BF16), kc.astype(BF16), vc.astype(BF16), bias)


def short_conv(a, w):
    T = a.shape[1]
    pad = w.shape[0] // 2
    ap = jnp.pad(a, ((0, 0), (pad, pad), (0, 0)))
    out = ap[:, :T] * w[0]
    for j in range(1, w.shape[0]):
        out = out + ap[:, j:j + T] * w[j]
    return out


ML_CHUNKS_PER_STEP = 10


def _bmm(a, b, contract):
    return lax.dot_general(a.astype(BF16), b.astype(BF16), (contract, ((0,), (0,))),
                           preferred_element_type=F32)


def _mlstm_kernel(q_ref, k_ref, v_ref, i_ref, b_ref, h_ref, c_ref, n_ref, m_ref):
    L = ML_CHUNK

    @pl.when(pl.program_id(0) == 0)
    def _():
        c_ref[...] = jnp.zeros(c_ref.shape, F32)
        n_ref[...] = jnp.zeros(n_ref.shape, F32)
        m_ref[...] = jnp.zeros(m_ref.shape, F32)

    row = lax.broadcasted_iota(jnp.int32, (1, L, L), 1)
    col = lax.broadcasted_iota(jnp.int32, (1, L, L), 2)
    seen, eye = row >= col, row == col

    def as_col(r):
        return jnp.sum(jnp.where(eye, r, 0.0), axis=2, keepdims=True)

    C, nrow, m = c_ref[...], n_ref[...], m_ref[...]
    for c in range(ML_CHUNKS_PER_STEP):
        rows = slice(c * L, (c + 1) * L)
        qt, kt, vt = q_ref[:, rows, :], k_ref[:, rows, :], v_ref[:, rows, :]
        irow, brow = i_ref[:, 0, c:c + 1, :], b_ref[:, 0, c:c + 1, :]
        blast = brow[:, :, L - 1:L]
        rrow = brow - irow
        bcol = as_col(brow)
        d_log = jnp.where(seen, bcol - rrow, -jnp.inf)
        inter = bcol + m
        m_t = jnp.maximum(inter, jnp.max(d_log, axis=2, keepdims=True))
        w = jnp.exp(d_log - m_t)
        a = jnp.exp(inter - m_t)
        s = _bmm(qt, kt, ((2,), (2,))) * w
        num = _bmm(s, vt, ((2,), (1,))) + a * _bmm(qt, C, ((2,), (1,)))
        den = jnp.sum(s, axis=2, keepdims=True) + a * jnp.sum(qt * nrow, axis=2, keepdims=True)
        h_ref[:, rows, :] = num / jnp.maximum(jnp.abs(den), jnp.exp(-m_t))
        g = blast - rrow
        m_new = jnp.maximum(blast + m, jnp.max(g, axis=2, keepdims=True))
        kw = kt * as_col(jnp.exp(g - m_new))
        decay = jnp.exp(blast + m - m_new)
        C = decay * C + _bmm(jnp.swapaxes(kw, 1, 2), vt, ((2,), (1,)))
        nrow = decay * nrow + jnp.sum(kw, axis=1, keepdims=True)
        m = m_new
    c_ref[...], n_ref[...], m_ref[...] = C, nrow, m


def mlstm_scan(q, k, v, ig, lf):
    S, T, H, d = q.shape
    CB, L = ML_CHUNKS_PER_STEP, ML_CHUNK
    N, steps = S * H, T // (CB * L)
    hm = lambda a: jnp.swapaxes(a, 1, 2).reshape(N, T, d)
    gates = lambda a: jnp.swapaxes(a, 1, 2).reshape(N, steps, CB, L)
    b = jnp.cumsum(lf.reshape(S, T // L, L, H), axis=2).reshape(S, T, H)
    seq = pl.BlockSpec((N, CB * L, d), lambda j: (0, j, 0))
    gate = pl.BlockSpec((N, 1, CB, L), lambda j: (0, j, 0, 0))
    out = pl.pallas_call(
        _mlstm_kernel,
        grid=(steps,),
        in_specs=[seq, seq, seq, gate, gate],
        out_specs=seq,
        out_shape=jax.ShapeDtypeStruct((N, T, d), F32),
        scratch_shapes=[pltpu.VMEM((N, d, d), F32), pltpu.VMEM((N, 1, d), F32), pltpu.VMEM((N, 1, 1), F32)],
        compiler_params=pltpu.CompilerParams(vmem_limit_bytes=VMEM_LIMIT_BYTES),
    )(hm(q), hm(k), hm(v), gates(ig), gates(b))
    return jnp.swapaxes(out.reshape(S, H, T, d), 1, 2)


def mlstm_prep(qk, v, gates, conv_w, gate_b):
    qk = jax.nn.silu(short_conv(qk, conv_w))
    q, k = jnp.split(qk, 2, axis=-1)
    g = (gates + gate_b).astype(F32)
    i_f, f_f, i_b, f_b = jnp.split(g, 4, axis=-1)
    return (heads(q, GROUP_HEADS) * HEAD_DIM ** -0.5, heads(k, GROUP_HEADS), heads(v, GROUP_HEADS),
            (i_f, jax.nn.log_sigmoid(f_f), i_b, jax.nn.log_sigmoid(f_b)))


def mlstm_mixer(lat, ctx, conv_w, gate_b):
    ql, kl, vl, gl = mlstm_prep(lat[0], lat[1], lat[2], conv_w, gate_b)
    qc, kc, vc, gc = mlstm_prep(ctx[0], ctx[1], ctx[2], conv_w, gate_b)
    B, Tc = qc.shape[:2]

    def rev(a):
        return a[:, ::-1]

    def streams(c_, l_, cb, lb):
        return jnp.concatenate([jnp.concatenate([c_, l_], axis=1),
                                jnp.concatenate([rev(cb), rev(lb)], axis=1)], axis=0)

    h = mlstm_scan(streams(qc, ql, qc, ql), streams(kc, kl, kc, kl), streams(vc, vl, vc, vl),
                   streams(gc[0], gl[0], gc[2], gl[2]), streams(gc[1], gl[1], gc[3], gl[3]))
    hf, hb = h[:B], h[B:]
    return hf[:, Tc:] + rev(hb[:, Tc:]), hf[:, :Tc] + rev(hb[:, :Tc])


def mla_project(cq, ckv, kr, q_norm, w_uq, kv_norm, w_ukv, angs):
    q = heads(rmsnorm(cq, q_norm) @ w_uq, GROUP_HEADS)
    kv = heads(rmsnorm(ckv, kv_norm) @ w_ukv, GROUP_HEADS)
    q_nope, q_rope = q[..., :MLA_NOPE], q[..., MLA_NOPE:]
    k_nope, v = kv[..., :MLA_NOPE], kv[..., MLA_NOPE:]
    k_rope = kr[:, :, None, :]
    if angs is not None:
        q_rope = rope_2d(q_rope, angs)
        k_rope = rope_2d(k_rope, angs)
    k_rope = jnp.broadcast_to(k_rope, k_nope.shape[:-1] + (MLA_ROPE,))
    return (jnp.concatenate([q_nope, q_rope], axis=-1), jnp.concatenate([k_nope, k_rope], axis=-1), v)


LOG2_E = 1.4426950408889634
DENSE_Q_TILE = 1024
DENSE_Q_SUB = 256
DENSE_Q_UNROLL = 4
DENSE_K_TILE_MAX = 1664


def _dense_attn_kernel(q_ref, k_ref, v_ref, o_ref, m_ref, l_ref, acc_ref, *, scale):
    j = pl.program_id(3)

    @pl.when(j == 0)
    def _():
        m_ref[...] = jnp.full(m_ref.shape, -jnp.inf, F32)
        l_ref[...] = jnp.zeros(l_ref.shape, F32)
        acc_ref[...] = jnp.zeros(acc_ref.shape, F32)

    def rows(i, carry):
        for u in range(DENSE_Q_UNROLL):
            r = pl.ds(pl.multiple_of((i * DENSE_Q_UNROLL + u) * DENSE_Q_SUB, DENSE_Q_SUB), DENSE_Q_SUB)
            s = lax.dot_general(q_ref[0, 0, r, :], k_ref[0, 0], NT_DIMS,
                                preferred_element_type=F32) * (scale * LOG2_E)
            m_prev = m_ref[r, :]
            m_new = jnp.maximum(m_prev, jnp.max(s, axis=-1, keepdims=True))
            alpha = jnp.exp2(m_prev - m_new)
            p = jnp.exp2(s - m_new)
            l_ref[r, :] = alpha * l_ref[r, :] + jnp.sum(p, axis=-1, keepdims=True)
            acc_ref[r, :] = alpha * acc_ref[r, :] + jnp.dot(p.astype(BF16), v_ref[0, 0],
                                                            preferred_element_type=F32)
            m_ref[r, :] = m_new
        return carry

    lax.fori_loop(0, q_ref.shape[2] // (DENSE_Q_SUB * DENSE_Q_UNROLL), rows, 0)

    @pl.when(j == pl.num_programs(3) - 1)
    def _():
        o_ref[0, 0] = acc_ref[...] / l_ref[...]


def block_dense_attention(q, k_all, v_all, scale):
    B, T, H, dq = q.shape
    NK, dv = k_all.shape[1], v_all.shape[-1]
    tq = min(DENSE_Q_TILE, T)
    tk = max(t for t in range(LANES, DENSE_K_TILE_MAX + 1, LANES) if NK % t == 0)
    hm = lambda a: jnp.swapaxes(a, 1, 2).astype(BF16)
    out = pl.pallas_call(
        functools.partial(_dense_attn_kernel, scale=scale),
        grid=(B, H, T // tq, NK // tk),
        in_specs=[pl.BlockSpec((1, 1, tq, dq), lambda b, h, i, j: (b, h, i, 0)),
                  pl.BlockSpec((1, 1, tk, dq), lambda b, h, i, j: (b, h, j, 0)),
                  pl.BlockSpec((1, 1, tk, dv), lambda b, h, i, j: (b, h, j, 0))],
        out_specs=pl.BlockSpec((1, 1, tq, dv), lambda b, h, i, j: (b, h, i, 0)),
        out_shape=jax.ShapeDtypeStruct((B, H, T, dv), F32),
        scratch_shapes=[pltpu.VMEM((tq, 1), F32), pltpu.VMEM((tq, 1), F32), pltpu.VMEM((tq, dv), F32)],
        compiler_params=pltpu.CompilerParams(vmem_limit_bytes=VMEM_LIMIT_BYTES),
    )(hm(q), hm(k_all), hm(v_all))
    return jnp.swapaxes(out, 1, 2).reshape(B, T, H * dv)


def window_attention(q, k, v, kc, vc, sink):
    B, T, H, d = q.shape
    KVH = k.shape[2]
    G = H // KVH
    nb = T // ATTN_BLOCK
    span = ATTN_BLOCK + 2 * SWA_WINDOW
    n_ctx = kc.shape[1]
    scale = d ** -0.5
    padw = ((0, 0), (SWA_WINDOW, SWA_WINDOW), (0, 0), (0, 0))
    kp = jnp.pad(k, padw)
    vp = jnp.pad(v, padw)
    start = jnp.arange(nb) * ATTN_BLOCK
    idx = start[:, None] + jnp.arange(span)[None, :]
    kb = kp[:, idx]
    vb = vp[:, idx]
    key_pos = idx - SWA_WINDOW
    q_pos = start[:, None] + jnp.arange(ATTN_BLOCK)[None, :]
    mask = ((jnp.abs(q_pos[:, :, None] - key_pos[:, None, :]) <= SWA_WINDOW)
            & (key_pos >= 0)[:, None, :] & (key_pos < T)[:, None, :])
    qb = q.reshape(B, nb, ATTN_BLOCK, KVH, G, d)
    s_loc = jnp.einsum('bnqhgd,bnkhd->bnhgqk', qb, kb).astype(F32) * scale
    s_loc = jnp.where(mask[None, :, None, None], s_loc, -jnp.inf)
    s_ctx = jnp.einsum('bnqhgd,bkhd->bnhgqk', qb, kc).astype(F32) * scale
    s_sink = jnp.broadcast_to(sink.astype(F32).reshape(1, 1, KVH, G, 1, 1), s_loc.shape[:-1] + (1,))
    p = jax.nn.softmax(jnp.concatenate([s_loc, s_ctx, s_sink], axis=-1), axis=-1).astype(v.dtype)
    out = (jnp.einsum('bnhgqk,bnkhd->bnqhgd', p[..., :span], vb)
           + jnp.einsum('bnhgqk,bkhd->bnqhgd', p[..., span:span + n_ctx], vc))
    return out.reshape(B, T, H * d)


BF16 = jnp.bfloat16
LANES = 128
SUBLANES = 8
ROW_SEGS = D_MODEL // LANES
ROW_WORDS = ROW_SEGS // 2
PEER_PICKS = PEER_HEADS * PEER_TOPK
PEER_TOPK_TOKENS = 256
PEER_GATHER_TOKENS = 128
PEER_ACT_UNROLL = 4
VMEM_LIMIT_BYTES = 56 * 1024 * 1024


def _split_bf16(x, parts):
    out = []
    for _ in range(parts):
        p = x.astype(BF16)
        out.append(p)
        x = x - p.astype(F32)
    return out


def _topk_rows(s, k):
    n = s.shape[0]
    iota = lax.broadcasted_iota(jnp.int32, s.shape, 0)
    vals, idxs = [], []
    for _ in range(k):
        m = jnp.max(s, axis=0, keepdims=True)
        i = jnp.min(jnp.where(s == m, iota, n), axis=0, keepdims=True)
        vals.append(m)
        idxs.append(i)
        s = jnp.where(iota == i, -jnp.inf, s)
    return jnp.concatenate(vals, axis=0), jnp.concatenate(idxs, axis=0)


def _peer_topk_kernel(x_ref, wq_ref, keys_ref, eidx_ref, gate_ref):
    xb = x_ref[...].astype(BF16)
    q = jnp.dot(xb, wq_ref[...], preferred_element_type=F32)
    nt = (((1,), (1,)), ((), ()))
    sv, si = [], []
    for p in range(2):
        qp = q[:, p * PEER_DKEY:(p + 1) * PEER_DKEY].astype(BF16)
        s = lax.dot_general(keys_ref[0, p], qp, nt, preferred_element_type=F32)
        v_, i_ = _topk_rows(s, PEER_TOPK)
        sv.append(v_)
        si.append(i_)
    cs, ce = [], []
    half = PEER_TOPK // 2
    for a in range(half):
        nb = PEER_TOPK if a == 0 else half
        cs.append(sv[0][a:a + 1] + sv[1][:nb])
        ce.append(si[0][a:a + 1] * PEER_NKEYS + si[1][:nb])
    cs.append(sv[0][half:] + sv[1][0:1])
    ce.append(si[0][half:] * PEER_NKEYS + si[1][0:1])
    cand_s = jnp.concatenate(cs, axis=0)
    cand_e = jnp.concatenate(ce, axis=0)
    fs, fpos = _topk_rows(cand_s, PEER_TOPK)
    iota = lax.broadcasted_iota(jnp.int32, cand_e.shape, 0)
    eidx = [jnp.max(jnp.where(iota == fpos[j:j + 1], cand_e, -1), axis=0, keepdims=True)
            for j in range(PEER_TOPK)]
    ex = jnp.exp(fs - fs[0:1])
    eidx_ref[0] = jnp.concatenate(eidx, axis=0)
    gate_ref[0] = ex / jnp.sum(ex, axis=0, keepdims=True)


def peer_topk(h, wq, sub_keys):
    N, D = h.shape
    T = PEER_TOPK_TOKENS
    wqb = wq.astype(BF16)
    kb = sub_keys.astype(BF16)
    eidx, gate = pl.pallas_call(
        _peer_topk_kernel,
        grid=(N // T, PEER_HEADS),
        in_specs=[pl.BlockSpec((T, D), lambda i, h_: (i, 0)),
                  pl.BlockSpec((D, 2 * PEER_DKEY), lambda i, h_: (0, h_)),
                  pl.BlockSpec((1, 2, PEER_NKEYS, PEER_DKEY), lambda i, h_: (h_, 0, 0, 0))],
        out_specs=[pl.BlockSpec((1, PEER_TOPK, T), lambda i, h_: (h_, 0, i)),
                   pl.BlockSpec((1, PEER_TOPK, T), lambda i, h_: (h_, 0, i))],
        out_shape=[jax.ShapeDtypeStruct((PEER_HEADS, PEER_TOPK, N), jnp.int32),
                   jax.ShapeDtypeStruct((PEER_HEADS, PEER_TOPK, N), F32)],
        compiler_params=pltpu.CompilerParams(vmem_limit_bytes=VMEM_LIMIT_BYTES),
    )(h, wqb, kb)
    return eidx.reshape(PEER_PICKS, N), gate.reshape(PEER_PICKS, N)


def pack_expert_table(tab):
    E = tab.shape[0]
    t = tab.astype(BF16).reshape(E, ROW_WORDS, 2, LANES)
    t = jnp.swapaxes(t, -1, -2)
    return lax.bitcast_convert_type(t, jnp.uint32).reshape(E * ROW_WORDS, LANES)


def _stage_rows(idx_ref, tab_ref, stage_ref, t):
    for k in range(PEER_PICKS):
        off = pl.multiple_of(idx_ref[t, k], ROW_WORDS)
        stage_ref[k * ROW_WORDS:(k + 1) * ROW_WORDS, :] = tab_ref[pl.ds(off, ROW_WORDS), :]
    return pltpu.bitcast(stage_ref[...], BF16)


def _peer_act_kernel(idx_ref, x_ref, gate_ref, tab_ref, seg_mask_ref, group_ref, w_ref,
                     stage_ref, rows_ref):
    T = x_ref.shape[0]
    U = stage_ref.shape[0]

    def tokens(g, carry):
        for j in range(U):
            t = g * U + j
            sb = _stage_rows(idx_ref, tab_ref, stage_ref.at[j], t)
            xs = jnp.concatenate(_split_bf16(x_ref[t], 2), axis=0)
            r = lax.dot_general(xs, sb, NT_DIMS, preferred_element_type=F32)
            r = r * seg_mask_ref[...]
            rows_ref[t] = r[:SUBLANES] + r[SUBLANES:]
        return carry

    lax.fori_loop(0, T // U, tokens, 0)
    rows = rows_ref[...].reshape(T * SUBLANES, PEER_PICKS * ROW_SEGS)
    part = jnp.zeros((T * SUBLANES, PEER_PICKS), F32)
    for piece in _split_bf16(rows, 3):
        part = part + jnp.dot(piece, group_ref[...], preferred_element_type=F32)
    act = jnp.sum(part.reshape(T, SUBLANES, PEER_PICKS), axis=1)
    w_ref[...] = gate_ref[...] * (0.5 * act * (1.0 + lax.erf(act * (2.0 ** -0.5))))


def _peer_out_kernel(idx_ref, w_ref, tab_ref, expand_ref, seg_mask_ref, f_ref, stage_ref):
    T = w_ref.shape[0]
    U = stage_ref.shape[0]

    def tokens(g, carry):
        w8 = w_ref[pl.ds(pl.multiple_of(g * U, U), U), :]
        for j in range(U):
            t = g * U + j
            sb = _stage_rows(idx_ref, tab_ref, stage_ref.at[j], t)
            lhs = jnp.concatenate([jnp.broadcast_to(p, (SUBLANES, PEER_PICKS))
                                   for p in _split_bf16(w8[j:j + 1], 2)], axis=0)
            wrep = jnp.dot(lhs, expand_ref[...], preferred_element_type=F32)
            wsel = (wrep * seg_mask_ref[...]).astype(BF16)
            o = jnp.dot(wsel, sb, preferred_element_type=F32)
            f_ref[t] = o[:SUBLANES] + o[SUBLANES:]
        return carry

    lax.fori_loop(0, T // U, tokens, 0)


def _peer_constants():
    cols = np.arange(PEER_PICKS * ROW_SEGS)
    seg_mask = (cols[None, :] % ROW_SEGS == np.arange(2 * SUBLANES)[:, None] % SUBLANES)
    group = (cols[:, None] // ROW_SEGS == np.arange(PEER_PICKS)[None, :])
    return (jnp.asarray(seg_mask, F32), jnp.asarray(group, BF16), jnp.asarray(group.T, BF16))


def peer_ffn(h, wq, sub_keys, u, v):
    N, D = h.shape
    T = PEER_GATHER_TOKENS
    eidx, gate = peer_topk(h, wq, sub_keys)
    seg_mask, group, expand = _peer_constants()
    x3 = h.reshape(N, ROW_SEGS, LANES)
    rows = u.shape[0] * ROW_WORDS
    offs = eidx.T * ROW_WORDS
    idx_spec = pl.BlockSpec((T, PEER_PICKS), lambda i: (i, 0), memory_space=pltpu.SMEM)
    tab_spec = pl.BlockSpec((rows, LANES), lambda i: (0, 0), pipeline_mode=pl.Buffered(1))
    const = lambda shape: pl.BlockSpec(shape, lambda i: (0, 0))
    params = pltpu.CompilerParams(vmem_limit_bytes=VMEM_LIMIT_BYTES)
    w = pl.pallas_call(
        _peer_act_kernel,
        grid=(N // T,),
        in_specs=[idx_spec,
                  pl.BlockSpec((T, ROW_SEGS, LANES), lambda i: (i, 0, 0)),
                  pl.BlockSpec((T, PEER_PICKS), lambda i: (i, 0)),
                  tab_spec, const(seg_mask.shape), const(group.shape)],
        out_specs=pl.BlockSpec((T, PEER_PICKS), lambda i: (i, 0)),
        out_shape=jax.ShapeDtypeStruct((N, PEER_PICKS), F32),
        scratch_shapes=[pltpu.VMEM((PEER_ACT_UNROLL, PEER_PICKS * ROW_WORDS, LANES), jnp.uint32),
                        pltpu.VMEM((T, SUBLANES, PEER_PICKS * ROW_SEGS), F32)],
        compiler_params=params,
    )(offs, x3, gate.T, pack_expert_table(u), seg_mask, group)
    f = pl.pallas_call(
        _peer_out_kernel,
        grid=(N // T,),
        in_specs=[idx_spec,
                  pl.BlockSpec((T, PEER_PICKS), lambda i: (i, 0)),
                  tab_spec, const(expand.shape), const(seg_mask.shape)],
        out_specs=pl.BlockSpec((T, ROW_SEGS, LANES), lambda i: (i, 0, 0)),
        out_shape=jax.ShapeDtypeStruct((N, ROW_SEGS, LANES), F32),
        scratch_shapes=[pltpu.VMEM((SUBLANES, PEER_PICKS * ROW_WORDS, LANES), jnp.uint32)],
        compiler_params=params,
    )(offs, w, pack_expert_table(v), expand, seg_mask)
    return f.reshape(N, D)


def hybrid_layer(x, xc, c, c_ctx, need_ctx, angs_mla, angs_swa,
                 norm1_g, norm2_g, w_ada, b_ada, w_in, na_rpb, ml_conv, ml_gate_b,
                 mla_q_norm, mla_w_uq, mla_kv_norm, mla_w_ukv, swa_sink, w_out,
                 peer_wq, peer_keys, peer_u, peer_v):
    B, T, D = x.shape
    H = GROUP_HEADS
    sh1, sc1, g1, sh2, sc2, g2 = jnp.split((jax.nn.silu(c) @ w_ada + b_ada)[:, None, :], 6, axis=-1)
    sh1c, sc1c, g1c, sh2c, sc2c, g2c = jnp.split(jax.nn.silu(c_ctx) @ w_ada + b_ada, 6, axis=-1)
    h = rmsnorm(x, norm1_g) * (1.0 + sc1) + sh1
    hc = rmsnorm(xc, norm1_g) * (1.0 + sc1c) + sh1c
    (na_q, na_k, na_v, ml_qk, ml_v, ml_o, ml_g,
     mla_cq, mla_ckv, mla_kr, sw_q, sw_k, sw_v) = split_cols(h @ w_in)
    (na_qc, na_kc, na_vc, ml_qkc, ml_vc, ml_oc, ml_gc,
     mla_cqc, mla_ckvc, mla_krc, sw_qc, sw_kc, sw_vc) = split_cols(hc @ w_in)
    attn_scale = HEAD_DIM ** -0.5
    mla_scale = (MLA_NOPE + MLA_ROPE) ** -0.5
    kc_a, vc_a = heads(na_kc, H), heads(na_vc, H)
    y_a = neighbourhood_attention(na_q, na_k, na_v, na_kc, na_vc, na_rpb)
    h_lat, h_ctx = mlstm_mixer((ml_qk, ml_v, ml_g), (ml_qkc, ml_vc, ml_gc), ml_conv, ml_gate_b)
    y_b = h_lat.reshape(B, T, GROUP_WIDTH) * jax.nn.sigmoid(ml_o)
    q_m, k_m, v_m = mla_project(mla_cq, mla_ckv, mla_kr, mla_q_norm, mla_w_uq, mla_kv_norm, mla_w_ukv, angs_mla)
    qc_m, kc_m, vc_m = mla_project(mla_cqc, mla_ckvc, mla_krc, mla_q_norm, mla_w_uq, mla_kv_norm, mla_w_ukv, None)
    y_c = block_dense_attention(q_m, jnp.concatenate([kc_m, k_m], axis=1), jnp.concatenate([vc_m, v_m], axis=1), mla_scale)
    kc_d, vc_d = heads(sw_kc, SWA_KV_HEADS), heads(sw_vc, SWA_KV_HEADS)
    y_d = window_attention(rope_2d(heads(sw_q, H), angs_swa), rope_2d(heads(sw_k, SWA_KV_HEADS), angs_swa),
                           heads(sw_v, SWA_KV_HEADS), kc_d, vc_d, swa_sink)
    x = x + g1 * (jnp.concatenate([y_a, y_b, y_c, y_d], axis=-1) @ w_out)
    h2 = rmsnorm(x, norm2_g) * (1.0 + sc2) + sh2
    if need_ctx:
        Tc = xc.shape[1]
        y_ctx = jnp.concatenate([
            ctx_attn(heads(na_qc, H), kc_a, vc_a, attn_scale),
            h_ctx.reshape(B, Tc, GROUP_WIDTH) * jax.nn.sigmoid(ml_oc),
            ctx_attn(qc_m, kc_m, vc_m, mla_scale),
            ctx_attn(heads(sw_qc, H), kc_d, vc_d, attn_scale, swa_sink)], axis=-1)
        xc = xc + g1c * (y_ctx @ w_out)
        h2c = rmsnorm(xc, norm2_g) * (1.0 + sc2c) + sh2c
        f = peer_ffn(jnp.concatenate([h2.reshape(B * T, D), h2c.reshape(B * Tc, D)], axis=0),
                     peer_wq, peer_keys, peer_u, peer_v)
        x = x + g2 * f[:B * T].reshape(B, T, D)
        xc = xc + g2c * f[B * T:].reshape(B, Tc, D)
        return x, xc
    x = x + g2 * peer_ffn(h2.reshape(B * T, D), peer_wq, peer_keys, peer_u, peer_v).reshape(B, T, D)
    return x, None


def _final_rmsnorm_kernel(x_ref, g_ref, o_ref):
    x = x_ref[...]
    o_ref[...] = x * lax.rsqrt(jnp.mean(x * x, axis=-1, keepdims=True) + EPS) * g_ref[...]


def final_rmsnorm(x, g):
    B, T, D = x.shape
    rows = 1024
    xf = x.reshape(B * T, D)
    out = pl.pallas_call(
        _final_rmsnorm_kernel,
        grid=(B * T // rows,),
        in_specs=[pl.BlockSpec((rows, D), lambda i: (i, 0)), pl.BlockSpec((1, D), lambda i: (0, 0))],
        out_specs=pl.BlockSpec((rows, D), lambda i: (i, 0)),
        out_shape=jax.ShapeDtypeStruct((B * T, D), x.dtype),
    )(xf, g.reshape(1, D))
    return out.reshape(B, T, D)


def kernel(x, c, ctx, c_ctx, norm1_g, norm2_g, w_ada, b_ada, w_in, na_rpb, ml_conv, ml_gate_b,
           mla_q_norm, mla_w_uq, mla_kv_norm, mla_w_ukv, swa_sink, w_out,
           peer_wq, peer_keys, peer_u, peer_v, final_norm_g):
    T = x.shape[1]
    angs_mla = axial_angles(T, MLA_ROPE)
    angs_swa = axial_angles(T, HEAD_DIM)
    xc = ctx
    for l in range(DEPTH):
        x, xc = hybrid_layer(x, xc, c, c_ctx, l < DEPTH - 1, angs_mla, angs_swa,
                             norm1_g[l], norm2_g[l], w_ada[l], b_ada[l], w_in[l], na_rpb[l],
                             ml_conv[l], ml_gate_b[l], mla_q_norm[l], mla_w_uq[l], mla_kv_norm[l],
                             mla_w_ukv[l], swa_sink[l], w_out[l], peer_wq[l], peer_keys[l],
                             peer_u[l], peer_v[l])
    return final_rmsnorm(x, final_norm_g)
```

```python
import functools

import jax
import jax.numpy as jnp
from jax import lax
import numpy as np
from jax.experimental import pallas as pl
from jax.experimental.pallas import tpu as pltpu

D_MODEL = 1024
BATCH = 2
SEQ = 16384
DEPTH = 2

CTX_LEN = 256
GRID_W = 64
N_MIXERS = 4
MIX_WIDTH = D_MODEL
GROUP_WIDTH = MIX_WIDTH // N_MIXERS
GROUP_HEADS = 4
HEAD_DIM = GROUP_WIDTH // GROUP_HEADS
NA_ROWS = 8
NA_COLS = 16
ML_CHUNK = 64
ML_CONV = 5
MLA_Q_RANK = 256
MLA_KV_RANK = 128
MLA_NOPE = 64
MLA_ROPE = 32
MLA_V = 64
SWA_KV_HEADS = 2
SWA_WINDOW = 128
ATTN_BLOCK = 128
PEER_HEADS = 8
PEER_NKEYS = 128
PEER_EXPERTS = PEER_NKEYS * PEER_NKEYS
PEER_DKEY = 128
PEER_TOPK = 16
PEER_BLOCK = 128
ROPE_BASE = 10000.0
EPS = 1e-6
IN_SIZES = (GROUP_WIDTH, GROUP_WIDTH, GROUP_WIDTH,
            2 * GROUP_WIDTH, GROUP_WIDTH, GROUP_WIDTH, 4 * GROUP_HEADS,
            MLA_Q_RANK, MLA_KV_RANK, MLA_ROPE,
            GROUP_WIDTH, SWA_KV_HEADS * HEAD_DIM, SWA_KV_HEADS * HEAD_DIM)
IN_WIDTH = sum(IN_SIZES)
F32 = jnp.float32


def rmsnorm(x, g):
    xf = x.astype(F32)
    y = xf * lax.rsqrt(jnp.mean(xf * xf, axis=-1, keepdims=True) + EPS) * g.astype(F32)
    return y.astype(x.dtype)


def heads(a, h):
    return a.reshape(a.shape[:-1] + (h, a.shape[-1] // h))


def split_cols(p):
    return jnp.split(p, np.cumsum(IN_SIZES)[:-1].tolist(), axis=-1)


def axial_angles(T, rot_dim):
    t = jnp.arange(T)
    row = (t // GRID_W).astype(F32)
    col = (t % GRID_W).astype(F32)
    half = rot_dim // 2
    inv = 1.0 / (ROPE_BASE ** (jnp.arange(0, half, 2, dtype=F32) / half))
    return row[:, None] * inv, col[:, None] * inv


def rope_1d(x, ang):
    cos = jnp.cos(ang)[None, :, None, :]
    sin = jnp.sin(ang)[None, :, None, :]
    x1, x2 = jnp.split(x.astype(F32), 2, axis=-1)
    return jnp.concatenate([x1 * cos - x2 * sin, x1 * sin + x2 * cos], axis=-1)


def rope_2d(x, angs):
    xr, xc = jnp.split(x, 2, axis=-1)
    return jnp.concatenate([rope_1d(xr, angs[0]), rope_1d(xc, angs[1])], axis=-1).astype(x.dtype)


def ctx_attn(q, k, v, scale, sink=None):
    rep = q.shape[2] // k.shape[2]
    k = jnp.repeat(k, rep, axis=2)
    v = jnp.repeat(v, rep, axis=2)
    s = jnp.einsum('bqhd,bkhd->bhqk', q, k).astype(F32) * scale
    nk = s.shape[-1]
    if sink is not None:
        s = jnp.concatenate([s, jnp.broadcast_to(sink.astype(F32)[None, :, None, None], s.shape[:-1] + (1,))], axis=-1)
    p = jax.nn.softmax(s, axis=-1)[..., :nk].astype(v.dtype)
    out = jnp.einsum('bhqk,bkhd->bqhd', p, v)
    return out.reshape(out.shape[:2] + (-1,))


NT_DIMS = (((1,), (1,)), ((), ()))
NA_SPAN = NA_ROWS * GRID_W


def _head_mask(width):
    rows = lax.broadcasted_iota(jnp.int32, (GROUP_HEADS * width, GROUP_WIDTH), 0) // width
    cols = lax.broadcasted_iota(jnp.int32, (GROUP_HEADS * width, GROUP_WIDTH), 1) // HEAD_DIM
    return (rows == cols).astype(F32)


def _na_kernel(q_ref, k_ref, v_ref, kc_ref, vc_ref, bias_ref, o_ref):
    r = pl.program_id(1)
    rows = pl.num_programs(1)
    rs = jnp.clip(r - NA_ROWS // 2, 0, rows - NA_ROWS)
    start = pl.multiple_of(rs * GRID_W, GRID_W)
    kw = k_ref[0, pl.ds(start, NA_SPAN), :]
    vw = v_ref[0, pl.ds(start, NA_SPAN), :]
    hm = _head_mask(GRID_W)
    q = q_ref[0] * (HEAD_DIM ** -0.5)
    q4 = (jnp.concatenate([q] * GROUP_HEADS, axis=0) * hm).astype(BF16)
    s_loc = lax.dot_general(q4, kw, NT_DIMS, preferred_element_type=F32) + bias_ref[rs - r + NA_ROWS - 1]
    s_ctx = lax.dot_general(q4, kc_ref[0], NT_DIMS, preferred_element_type=F32)
    m = jnp.maximum(jnp.max(s_loc, axis=-1, keepdims=True), jnp.max(s_ctx, axis=-1, keepdims=True))
    p_loc = jnp.exp(s_loc - m)
    p_ctx = jnp.exp(s_ctx - m)
    l = jnp.sum(p_loc, axis=-1, keepdims=True) + jnp.sum(p_ctx, axis=-1, keepdims=True)
    o = (jnp.dot(p_loc.astype(BF16), vw, preferred_element_type=F32)
         + jnp.dot(p_ctx.astype(BF16), vc_ref[0], preferred_element_type=F32)) * (hm / l)
    o_ref[0] = sum(o[h * GRID_W:(h + 1) * GRID_W] for h in range(GROUP_HEADS))


def _na_bias_table(rpb):
    c = np.arange(GRID_W)
    col_start = np.clip(c - NA_COLS // 2, 0, GRID_W - NA_COLS)
    valid = (c[None, :] >= col_start[:, None]) & (c[None, :] < col_start[:, None] + NA_COLS)
    dc = np.clip(c[None, :] - c[:, None] + NA_COLS - 1, 0, 2 * NA_COLS - 2)
    dr = np.arange(NA_ROWS)[:, None] + np.arange(NA_ROWS)[None, :]
    t = rpb.astype(F32)[:, dr][..., dc]
    t = jnp.where(valid[None, None, None], t, -jnp.inf)
    return jnp.transpose(t, (1, 0, 3, 2, 4)).reshape(NA_ROWS, GROUP_HEADS * GRID_W, NA_SPAN)


def neighbourhood_attention(q, k, v, kc, vc, rpb):
    B, T, C = q.shape
    rows = T // GRID_W
    n_ctx = kc.shape[1]
    bias = _na_bias_table(rpb)
    full = lambda n: pl.BlockSpec((1, n, C), lambda b, r: (b, 0, 0))
    return pl.pallas_call(
        _na_kernel,
        grid=(B, rows),
        in_specs=[pl.BlockSpec((1, GRID_W, C), lambda b, r: (b, r, 0)),
                  full(T), full(T), full(n_ctx), full(n_ctx),
                  pl.BlockSpec(bias.shape, lambda b, r: (0, 0, 0))],
        out_specs=pl.BlockSpec((1, GRID_W, C), lambda b, r: (b, r, 0)),
        out_shape=jax.ShapeDtypeStruct((B, T, C), F32),
        compiler_params=pltpu.CompilerParams(vmem_limit_bytes=VMEM_LIMIT_BYTES),
    )(q, k.astype(BF16), v.astype(BF16), kc.astype(BF16), vc.astype(BF16), bias)


def short_conv(a, w):
    T = a.shape[1]
    pad = w.shape[0] // 2
    ap = jnp.pad(a, ((0, 0), (pad, pad), (0, 0)))
    out = ap[:, :T] * w[0]
    for j in range(1, w.shape[0]):
        out = out + ap[:, j:j + T] * w[j]
    return out


ML_CHUNKS_PER_STEP = CTX_LEN // ML_CHUNK


def _bmm(a, b, contract):
    return lax.dot_general(a.astype(BF16), b.astype(BF16), (contract, ((0,), (0,))),
                           preferred_element_type=F32)


def _mlstm_chunk(qt, kt, vt, irow, brow, state, backward):
    L = ML_CHUNK
    C, nrow, m = state
    row = lax.broadcasted_iota(jnp.int32, (1, L, L), 1)
    col = lax.broadcasted_iota(jnp.int32, (1, L, L), 2)
    seen = (row <= col) if backward else (row >= col)
    eye = row == col

    def as_col(r):
        return jnp.sum(jnp.where(eye, r, 0.0), axis=2, keepdims=True)

    blast = brow[:, :, 0:1] if backward else brow[:, :, L - 1:L]
    rrow = brow - irow
    bcol = as_col(brow)
    d_log = jnp.where(seen, bcol - rrow, -jnp.inf)
    inter = bcol + m
    m_t = jnp.maximum(inter, jnp.max(d_log, axis=2, keepdims=True))
    w = jnp.exp(d_log - m_t)
    a = jnp.exp(inter - m_t)
    s = _bmm(qt, kt, ((2,), (2,))) * w
    num = _bmm(s, vt, ((2,), (1,))) + a * _bmm(qt, C, ((2,), (1,)))
    den = jnp.sum(s, axis=2, keepdims=True) + a * jnp.sum(qt * nrow, axis=2, keepdims=True)
    h = num / jnp.maximum(jnp.abs(den), jnp.exp(-m_t))
    g = blast - rrow
    m_new = jnp.maximum(blast + m, jnp.max(g, axis=2, keepdims=True))
    kw = kt * as_col(jnp.exp(g - m_new))
    decay = jnp.exp(blast + m - m_new)
    C = decay * C + _bmm(jnp.swapaxes(kw, 1, 2), vt, ((2,), (1,)))
    nrow = decay * nrow + jnp.sum(kw, axis=1, keepdims=True)
    return h, (C, nrow, m_new)


def _mlstm_kernel(qf_ref, kf_ref, vf_ref, if_ref, bf_ref, qb_ref, kb_ref, vb_ref, ib_ref, bb_ref,
                  hf_ref, hb_ref, c_ref, n_ref, m_ref):
    N, L = qf_ref.shape[0], ML_CHUNK

    @pl.when(pl.program_id(0) == 0)
    def _():
        c_ref[...] = jnp.zeros(c_ref.shape, F32)
        n_ref[...] = jnp.zeros(n_ref.shape, F32)
        m_ref[...] = jnp.zeros(m_ref.shape, F32)

    fwd = (c_ref[:N], n_ref[:N], m_ref[:N])
    bwd = (c_ref[N:], n_ref[N:], m_ref[N:])
    for c in range(ML_CHUNKS_PER_STEP):
        rows = slice(c * L, (c + 1) * L)
        h, fwd = _mlstm_chunk(qf_ref[:, rows, :], kf_ref[:, rows, :], vf_ref[:, rows, :],
                              if_ref[:, 0, c:c + 1, :], bf_ref[:, 0, c:c + 1, :], fwd, False)
        hf_ref[:, rows, :] = h
        cb = ML_CHUNKS_PER_STEP - 1 - c
        rows = slice(cb * L, (cb + 1) * L)
        h, bwd = _mlstm_chunk(qb_ref[:, rows, :], kb_ref[:, rows, :], vb_ref[:, rows, :],
                              ib_ref[:, 0, cb:cb + 1, :], bb_ref[:, 0, cb:cb + 1, :], bwd, True)
        hb_ref[:, rows, :] = h
    for i, ref in enumerate((c_ref, n_ref, m_ref)):
        ref[:N] = fwd[i]
        ref[N:] = bwd[i]


def mlstm_scan(q, k, v, gates_f, gates_b, n_ctx):
    B, T, H, d = q.shape
    CB, L = ML_CHUNKS_PER_STEP, ML_CHUNK
    assert n_ctx == CB * L and T % (CB * L) == 0
    N, steps = B * H, T // (CB * L)
    hm = lambda a: jnp.swapaxes(a, 1, 2).reshape(N, T, d)
    gates = lambda a: jnp.swapaxes(a, 1, 2).reshape(N, steps, CB, L)
    chunked = lambda a: a.reshape(B, T // L, L, H)
    b_f = jnp.cumsum(chunked(gates_f[1]), axis=2).reshape(B, T, H)
    b_b = lax.cumsum(chunked(gates_b[1]), axis=2, reverse=True).reshape(B, T, H)
    back = lambda j: jnp.where(j == 0, 0, steps - j)
    seq_f = pl.BlockSpec((N, CB * L, d), lambda j: (0, j, 0))
    seq_b = pl.BlockSpec((N, CB * L, d), lambda j: (0, back(j), 0))
    gate_f = pl.BlockSpec((N, 1, CB, L), lambda j: (0, j, 0, 0))
    gate_b = pl.BlockSpec((N, 1, CB, L), lambda j: (0, back(j), 0, 0))
    qh, kh, vh = hm(q), hm(k), hm(v)
    hf, hb = pl.pallas_call(
        _mlstm_kernel,
        grid=(steps,),
        in_specs=[seq_f, seq_f, seq_f, gate_f, gate_f, seq_b, seq_b, seq_b, gate_b, gate_b],
        out_specs=[seq_f, seq_b],
        out_shape=[jax.ShapeDtypeStruct((N, T, d), F32)] * 2,
        scratch_shapes=[pltpu.VMEM((2 * N, d, d), F32), pltpu.VMEM((2 * N, 1, d), F32),
                        pltpu.VMEM((2 * N, 1, 1), F32)],
        compiler_params=pltpu.CompilerParams(vmem_limit_bytes=VMEM_LIMIT_BYTES),
    )(qh, kh, vh, gates(gates_f[0]), gates(b_f), qh, kh, vh, gates(gates_b[0]), gates(b_b))
    return jnp.swapaxes((hf + hb).reshape(B, H, T, d), 1, 2)


def mlstm_prep(qk, v, gates, conv_w, gate_b):
    qk = jax.nn.silu(short_conv(qk, conv_w))
    q, k = jnp.split(qk, 2, axis=-1)
    g = (gates + gate_b).astype(F32)
    i_f, f_f, i_b, f_b = jnp.split(g, 4, axis=-1)
    return (heads(q, GROUP_HEADS) * HEAD_DIM ** -0.5, heads(k, GROUP_HEADS), heads(v, GROUP_HEADS),
            (i_f, jax.nn.log_sigmoid(f_f), i_b, jax.nn.log_sigmoid(f_b)))


def mlstm_mixer(lat, ctx, conv_w, gate_b):
    ql, kl, vl, gl = mlstm_prep(lat[0], lat[1], lat[2], conv_w, gate_b)
    qc, kc, vc, gc = mlstm_prep(ctx[0], ctx[1], ctx[2], conv_w, gate_b)
    Tc = qc.shape[1]
    cat = lambda c_, l_: jnp.concatenate([c_, l_], axis=1)
    h = mlstm_scan(cat(qc, ql), cat(kc, kl), cat(vc, vl),
                   (cat(gc[0], gl[0]), cat(gc[1], gl[1])), (cat(gc[2], gl[2]), cat(gc[3], gl[3])), Tc)
    return h[:, Tc:], h[:, :Tc]


def mla_project(cq, ckv, kr, q_norm, w_uq, kv_norm, w_ukv, angs):
    q = heads(rmsnorm(cq, q_norm) @ w_uq, GROUP_HEADS)
    kv = heads(rmsnorm(ckv, kv_norm) @ w_ukv, GROUP_HEADS)
    q_nope, q_rope = q[..., :MLA_NOPE], q[..., MLA_NOPE:]
    k_nope, v = kv[..., :MLA_NOPE], kv[..., MLA_NOPE:]
    k_rope = kr[:, :, None, :]
    if angs is not None:
        q_rope = rope_2d(q_rope, angs)
        k_rope = rope_2d(k_rope, angs)
    k_rope = jnp.broadcast_to(k_rope, k_nope.shape[:-1] + (MLA_ROPE,))
    return (jnp.concatenate([q_nope, q_rope], axis=-1), jnp.concatenate([k_nope, k_rope], axis=-1), v)


LOG2_E = 1.4426950408889634
DENSE_Q_TILE = 1024
DENSE_Q_SUB = 256
DENSE_Q_UNROLL = 4
DENSE_K_TILE_MAX = 1664


def _dense_attn_kernel(q_ref, k_ref, v_ref, o_ref, m_ref, l_ref, acc_ref, *, scale):
    j = pl.program_id(3)

    @pl.when(j == 0)
    def _():
        m_ref[...] = jnp.full(m_ref.shape, -jnp.inf, F32)
        l_ref[...] = jnp.zeros(l_ref.shape, F32)
        acc_ref[...] = jnp.zeros(acc_ref.shape, F32)

    def rows(i, carry):
        for u in range(DENSE_Q_UNROLL):
            r = pl.ds(pl.multiple_of((i * DENSE_Q_UNROLL + u) * DENSE_Q_SUB, DENSE_Q_SUB), DENSE_Q_SUB)
            s = lax.dot_general(q_ref[0, 0, r, :], k_ref[0, 0], NT_DIMS,
                                preferred_element_type=F32) * (scale * LOG2_E)
            m_prev = m_ref[r, :]
            m_new = jnp.maximum(m_prev, jnp.max(s, axis=-1, keepdims=True))
            alpha = jnp.exp2(m_prev - m_new)
            p = jnp.exp2(s - m_new)
            l_ref[r, :] = alpha * l_ref[r, :] + jnp.sum(p, axis=-1, keepdims=True)
            acc_ref[r, :] = alpha * acc_ref[r, :] + jnp.dot(p.astype(BF16), v_ref[0, 0],
                                                            preferred_element_type=F32)
            m_ref[r, :] = m_new
        return carry

    lax.fori_loop(0, q_ref.shape[2] // (DENSE_Q_SUB * DENSE_Q_UNROLL), rows, 0)

    @pl.when(j == pl.num_programs(3) - 1)
    def _():
        o_ref[0, 0] = acc_ref[...] / l_ref[...]


def block_dense_attention(q, k_all, v_all, scale):
    B, T, H, dq = q.shape
    NK, dv = k_all.shape[1], v_all.shape[-1]
    tq = min(DENSE_Q_TILE, T)
    tk = max(t for t in range(LANES, DENSE_K_TILE_MAX + 1, LANES) if NK % t == 0)
    hm = lambda a: jnp.swapaxes(a, 1, 2).astype(BF16)
    out = pl.pallas_call(
        functools.partial(_dense_attn_kernel, scale=scale),
        grid=(B, H, T // tq, NK // tk),
        in_specs=[pl.BlockSpec((1, 1, tq, dq), lambda b, h, i, j: (b, h, i, 0)),
                  pl.BlockSpec((1, 1, tk, dq), lambda b, h, i, j: (b, h, j, 0)),
                  pl.BlockSpec((1, 1, tk, dv), lambda b, h, i, j: (b, h, j, 0))],
        out_specs=pl.BlockSpec((1, 1, tq, dv), lambda b, h, i, j: (b, h, i, 0)),
        out_shape=jax.ShapeDtypeStruct((B, H, T, dv), F32),
        scratch_shapes=[pltpu.VMEM((tq, 1), F32), pltpu.VMEM((tq, 1), F32), pltpu.VMEM((tq, dv), F32)],
        compiler_params=pltpu.CompilerParams(vmem_limit_bytes=VMEM_LIMIT_BYTES),
    )(hm(q), hm(k_all), hm(v_all))
    return jnp.swapaxes(out, 1, 2).reshape(B, T, H * dv)


SWA_SPAN = ATTN_BLOCK + 2 * SWA_WINDOW


def _swa_kernel(q_ref, k_ref, v_ref, kc_ref, vc_ref, sink_ref, o_ref):
    n = pl.program_id(1)
    T = k_ref.shape[1]
    start = pl.multiple_of(jnp.clip(n * ATTN_BLOCK - SWA_WINDOW, 0, T - SWA_SPAN), ATTN_BLOCK)
    kw = k_ref[0, pl.ds(start, SWA_SPAN), :]
    vw = v_ref[0, pl.ds(start, SWA_SPAN), :]
    hm = _head_mask(ATTN_BLOCK)
    q = q_ref[0] * (HEAD_DIM ** -0.5)
    q4 = (jnp.concatenate([q] * GROUP_HEADS, axis=0) * hm).astype(BF16)
    rows = GROUP_HEADS * ATTN_BLOCK
    q_pos = n * ATTN_BLOCK + lax.broadcasted_iota(jnp.int32, (rows, SWA_SPAN), 0) % ATTN_BLOCK
    k_pos = start + lax.broadcasted_iota(jnp.int32, (rows, SWA_SPAN), 1)
    s_loc = lax.dot_general(q4, kw, NT_DIMS, preferred_element_type=F32)
    s_loc = jnp.where(jnp.abs(q_pos - k_pos) <= SWA_WINDOW, s_loc, -jnp.inf)
    s_ctx = lax.dot_general(q4, kc_ref[0], NT_DIMS, preferred_element_type=F32)
    sink = sink_ref[...]
    m = jnp.maximum(jnp.maximum(jnp.max(s_loc, axis=-1, keepdims=True),
                                jnp.max(s_ctx, axis=-1, keepdims=True)), sink)
    p_loc = jnp.exp(s_loc - m)
    p_ctx = jnp.exp(s_ctx - m)
    l = jnp.sum(p_loc, axis=-1, keepdims=True) + jnp.sum(p_ctx, axis=-1, keepdims=True) + jnp.exp(sink - m)
    o = (jnp.dot(p_loc.astype(BF16), vw, preferred_element_type=F32)
         + jnp.dot(p_ctx.astype(BF16), vc_ref[0], preferred_element_type=F32)) * (hm / l)
    o_ref[0] = sum(o[h * ATTN_BLOCK:(h + 1) * ATTN_BLOCK] for h in range(GROUP_HEADS))


def window_attention(q, k, v, kc, vc, sink):
    B, T, H, d = q.shape
    G = H // k.shape[2]
    n_ctx = kc.shape[1]
    C = H * d
    rep = lambda a: jnp.repeat(a, G, axis=2).reshape(a.shape[0], a.shape[1], C).astype(BF16)
    sink_rows = jnp.repeat(sink.astype(F32), ATTN_BLOCK).reshape(H * ATTN_BLOCK, 1)
    full = lambda n: pl.BlockSpec((1, n, C), lambda b, i: (b, 0, 0))
    return pl.pallas_call(
        _swa_kernel,
        grid=(B, T // ATTN_BLOCK),
        in_specs=[pl.BlockSpec((1, ATTN_BLOCK, C), lambda b, i: (b, i, 0)),
                  full(T), full(T), full(n_ctx), full(n_ctx),
                  pl.BlockSpec(sink_rows.shape, lambda b, i: (0, 0))],
        out_specs=pl.BlockSpec((1, ATTN_BLOCK, C), lambda b, i: (b, i, 0)),
        out_shape=jax.ShapeDtypeStruct((B, T, C), F32),
        compiler_params=pltpu.CompilerParams(vmem_limit_bytes=VMEM_LIMIT_BYTES),
    )(q.reshape(B, T, C), rep(k), rep(v), rep(kc), rep(vc), sink_rows)


BF16 = jnp.bfloat16
LANES = 128
SUBLANES = 8
ROW_SEGS = D_MODEL // LANES
ROW_WORDS = ROW_SEGS // 2
PEER_PICKS = PEER_HEADS * PEER_TOPK
PEER_TOPK_TOKENS = 256
PEER_GATHER_TOKENS = 128
PEER_ACT_UNROLL = 4
VMEM_LIMIT_BYTES = 56 * 1024 * 1024


def _split_bf16(x, parts):
    out = []
    for _ in range(parts):
        p = x.astype(BF16)
        out.append(p)
        x = x - p.astype(F32)
    return out


def _topk_rows(s, k):
    n = s.shape[0]
    iota = lax.broadcasted_iota(jnp.int32, s.shape, 0)
    vals, idxs = [], []
    for _ in range(k):
        m = jnp.max(s, axis=0, keepdims=True)
        i = jnp.min(jnp.where(s == m, iota, n), axis=0, keepdims=True)
        vals.append(m)
        idxs.append(i)
        s = jnp.where(iota == i, -jnp.inf, s)
    return jnp.concatenate(vals, axis=0), jnp.concatenate(idxs, axis=0)


def _peer_topk_kernel(x_ref, wq_ref, keys_ref, eidx_ref, gate_ref):
    xb = x_ref[...].astype(BF16)
    q = jnp.dot(xb, wq_ref[...], preferred_element_type=F32)
    nt = (((1,), (1,)), ((), ()))
    sv, si = [], []
    for p in range(2):
        qp = q[:, p * PEER_DKEY:(p + 1) * PEER_DKEY].astype(BF16)
        s = lax.dot_general(keys_ref[0, p], qp, nt, preferred_element_type=F32)
        v_, i_ = _topk_rows(s, PEER_TOPK)
        sv.append(v_)
        si.append(i_)
    cs, ce = [], []
    half = PEER_TOPK // 2
    for a in range(half):
        nb = PEER_TOPK if a == 0 else half
        cs.append(sv[0][a:a + 1] + sv[1][:nb])
        ce.append(si[0][a:a + 1] * PEER_NKEYS + si[1][:nb])
    cs.append(sv[0][half:] + sv[1][0:1])
    ce.append(si[0][half:] * PEER_NKEYS + si[1][0:1])
    cand_s = jnp.concatenate(cs, axis=0)
    cand_e = jnp.concatenate(ce, axis=0)
    fs, fpos = _topk_rows(cand_s, PEER_TOPK)
    iota = lax.broadcasted_iota(jnp.int32, cand_e.shape, 0)
    eidx = [jnp.max(jnp.where(iota == fpos[j:j + 1], cand_e, -1), axis=0, keepdims=True)
            for j in range(PEER_TOPK)]
    ex = jnp.exp(fs - fs[0:1])
    eidx_ref[0] = jnp.concatenate(eidx, axis=0)
    gate_ref[0] = ex / jnp.sum(ex, axis=0, keepdims=True)


def peer_topk(h, wq, sub_keys):
    N, D = h.shape
    T = PEER_TOPK_TOKENS
    wqb = wq.astype(BF16)
    kb = sub_keys.astype(BF16)
    eidx, gate = pl.pallas_call(
        _peer_topk_kernel,
        grid=(N // T, PEER_HEADS),
        in_specs=[pl.BlockSpec((T, D), lambda i, h_: (i, 0)),
                  pl.BlockSpec((D, 2 * PEER_DKEY), lambda i, h_: (0, h_)),
                  pl.BlockSpec((1, 2, PEER_NKEYS, PEER_DKEY), lambda i, h_: (h_, 0, 0, 0))],
        out_specs=[pl.BlockSpec((1, PEER_TOPK, T), lambda i, h_: (h_, 0, i)),
                   pl.BlockSpec((1, PEER_TOPK, T), lambda i, h_: (h_, 0, i))],
        out_shape=[jax.ShapeDtypeStruct((PEER_HEADS, PEER_TOPK, N), jnp.int32),
                   jax.ShapeDtypeStruct((PEER_HEADS, PEER_TOPK, N), F32)],
        compiler_params=pltpu.CompilerParams(vmem_limit_bytes=VMEM_LIMIT_BYTES),
    )(h, wqb, kb)
    return eidx.reshape(PEER_PICKS, N), gate.reshape(PEER_PICKS, N)


def pack_expert_table(tab):
    E = tab.shape[0]
    t = tab.astype(BF16).reshape(E, ROW_WORDS, 2, LANES)
    t = jnp.swapaxes(t, -1, -2)
    return lax.bitcast_convert_type(t, jnp.uint32).reshape(E * ROW_WORDS, LANES)


def _stage_rows(idx_ref, tab_ref, stage_ref, t):
    for k in range(PEER_PICKS):
        off = pl.multiple_of(idx_ref[t, k], ROW_WORDS)
        stage_ref[k * ROW_WORDS:(k + 1) * ROW_WORDS, :] = tab_ref[pl.ds(off, ROW_WORDS), :]
    return pltpu.bitcast(stage_ref[...], BF16)


def _peer_act_kernel(idx_ref, x_ref, gate_ref, tab_ref, seg_mask_ref, group_ref, w_ref,
                     stage_ref, rows_ref):
    T = x_ref.shape[0]
    U = stage_ref.shape[0]

    def tokens(g, carry):
        for j in range(U):
            t = g * U + j
            sb = _stage_rows(idx_ref, tab_ref, stage_ref.at[j], t)
            xs = jnp.concatenate(_split_bf16(x_ref[t], 2), axis=0)
            r = lax.dot_general(xs, sb, NT_DIMS, preferred_element_type=F32)
            r = r * seg_mask_ref[...]
            rows_ref[t] = r[:SUBLANES] + r[SUBLANES:]
        return carry

    lax.fori_loop(0, T // U, tokens, 0)
    rows = rows_ref[...].reshape(T * SUBLANES, PEER_PICKS * ROW_SEGS)
    part = jnp.zeros((T * SUBLANES, PEER_PICKS), F32)
    for piece in _split_bf16(rows, 3):
        part = part + jnp.dot(piece, group_ref[...], preferred_element_type=F32)
    act = jnp.sum(part.reshape(T, SUBLANES, PEER_PICKS), axis=1)
    w_ref[...] = gate_ref[...] * (0.5 * act * (1.0 + lax.erf(act * (2.0 ** -0.5))))


def _peer_out_kernel(idx_ref, w_ref, tab_ref, expand_ref, seg_mask_ref, f_ref, stage_ref):
    T = w_ref.shape[0]
    U = stage_ref.shape[0]

    def tokens(g, carry):
        w8 = w_ref[pl.ds(pl.multiple_of(g * U, U), U), :]
        for j in range(U):
            t = g * U + j
            sb = _stage_rows(idx_ref, tab_ref, stage_ref.at[j], t)
            lhs = jnp.concatenate([jnp.broadcast_to(p, (SUBLANES, PEER_PICKS))
                                   for p in _split_bf16(w8[j:j + 1], 2)], axis=0)
            wrep = jnp.dot(lhs, expand_ref[...], preferred_element_type=F32)
            wsel = (wrep * seg_mask_ref[...]).astype(BF16)
            o = jnp.dot(wsel, sb, preferred_element_type=F32)
            f_ref[t] = o[:SUBLANES] + o[SUBLANES:]
        return carry

    lax.fori_loop(0, T // U, tokens, 0)


def _peer_constants():
    cols = np.arange(PEER_PICKS * ROW_SEGS)
    seg_mask = (cols[None, :] % ROW_SEGS == np.arange(2 * SUBLANES)[:, None] % SUBLANES)
    group = (cols[:, None] // ROW_SEGS == np.arange(PEER_PICKS)[None, :])
    return (jnp.asarray(seg_mask, F32), jnp.asarray(group, BF16), jnp.asarray(group.T, BF16))


def peer_ffn(h, wq, sub_keys, u, v):
    N, D = h.shape
    T = PEER_GATHER_TOKENS
    eidx, gate = peer_topk(h, wq, sub_keys)
    seg_mask, group, expand = _peer_constants()
    x3 = h.reshape(N, ROW_SEGS, LANES)
    rows = u.shape[0] * ROW_WORDS
    offs = eidx.T * ROW_WORDS
    idx_spec = pl.BlockSpec((T, PEER_PICKS), lambda i: (i, 0), memory_space=pltpu.SMEM)
    tab_spec = pl.BlockSpec((rows, LANES), lambda i: (0, 0), pipeline_mode=pl.Buffered(1))
    const = lambda shape: pl.BlockSpec(shape, lambda i: (0, 0))
    params = pltpu.CompilerParams(vmem_limit_bytes=VMEM_LIMIT_BYTES)
    w = pl.pallas_call(
        _peer_act_kernel,
        grid=(N // T,),
        in_specs=[idx_spec,
                  pl.BlockSpec((T, ROW_SEGS, LANES), lambda i: (i, 0, 0)),
                  pl.BlockSpec((T, PEER_PICKS), lambda i: (i, 0)),
                  tab_spec, const(seg_mask.shape), const(group.shape)],
        out_specs=pl.BlockSpec((T, PEER_PICKS), lambda i: (i, 0)),
        out_shape=jax.ShapeDtypeStruct((N, PEER_PICKS), F32),
        scratch_shapes=[pltpu.VMEM((PEER_ACT_UNROLL, PEER_PICKS * ROW_WORDS, LANES), jnp.uint32),
                        pltpu.VMEM((T, SUBLANES, PEER_PICKS * ROW_SEGS), F32)],
        compiler_params=params,
    )(offs, x3, gate.T, pack_expert_table(u), seg_mask, group)
    f = pl.pallas_call(
        _peer_out_kernel,
        grid=(N // T,),
        in_specs=[idx_spec,
                  pl.BlockSpec((T, PEER_PICKS), lambda i: (i, 0)),
                  tab_spec, const(expand.shape), const(seg_mask.shape)],
        out_specs=pl.BlockSpec((T, ROW_SEGS, LANES), lambda i: (i, 0, 0)),
        out_shape=jax.ShapeDtypeStruct((N, ROW_SEGS, LANES), F32),
        scratch_shapes=[pltpu.VMEM((SUBLANES, PEER_PICKS * ROW_WORDS, LANES), jnp.uint32)],
        compiler_params=params,
    )(offs, w, pack_expert_table(v), expand, seg_mask)
    return f.reshape(N, D)


def hybrid_layer(x, xc, c, c_ctx, need_ctx, angs_mla, angs_swa,
                 norm1_g, norm2_g, w_ada, b_ada, w_in, na_rpb, ml_conv, ml_gate_b,
                 mla_q_norm, mla_w_uq, mla_kv_norm, mla_w_ukv, swa_sink, w_out,
                 peer_wq, peer_keys, peer_u, peer_v):
    B, T, D = x.shape
    H = GROUP_HEADS
    sh1, sc1, g1, sh2, sc2, g2 = jnp.split((jax.nn.silu(c) @ w_ada + b_ada)[:, None, :], 6, axis=-1)
    sh1c, sc1c, g1c, sh2c, sc2c, g2c = jnp.split(jax.nn.silu(c_ctx) @ w_ada + b_ada, 6, axis=-1)
    h = rmsnorm(x, norm1_g) * (1.0 + sc1) + sh1
    hc = rmsnorm(xc, norm1_g) * (1.0 + sc1c) + sh1c
    (na_q, na_k, na_v, ml_qk, ml_v, ml_o, ml_g,
     mla_cq, mla_ckv, mla_kr, sw_q, sw_k, sw_v) = split_cols(h @ w_in)
    (na_qc, na_kc, na_vc, ml_qkc, ml_vc, ml_oc, ml_gc,
     mla_cqc, mla_ckvc, mla_krc, sw_qc, sw_kc, sw_vc) = split_cols(hc @ w_in)
    attn_scale = HEAD_DIM ** -0.5
    mla_scale = (MLA_NOPE + MLA_ROPE) ** -0.5
    kc_a, vc_a = heads(na_kc, H), heads(na_vc, H)
    y_a = neighbourhood_attention(na_q, na_k, na_v, na_kc, na_vc, na_rpb)
    h_lat, h_ctx = mlstm_mixer((ml_qk, ml_v, ml_g), (ml_qkc, ml_vc, ml_gc), ml_conv, ml_gate_b)
    y_b = h_lat.reshape(B, T, GROUP_WIDTH) * jax.nn.sigmoid(ml_o)
    q_m, k_m, v_m = mla_project(mla_cq, mla_ckv, mla_kr, mla_q_norm, mla_w_uq, mla_kv_norm, mla_w_ukv, angs_mla)
    qc_m, kc_m, vc_m = mla_project(mla_cqc, mla_ckvc, mla_krc, mla_q_norm, mla_w_uq, mla_kv_norm, mla_w_ukv, None)
    y_c = block_dense_attention(q_m, jnp.concatenate([kc_m, k_m], axis=1), jnp.concatenate([vc_m, v_m], axis=1), mla_scale)
    kc_d, vc_d = heads(sw_kc, SWA_KV_HEADS), heads(sw_vc, SWA_KV_HEADS)
    y_d = window_attention(rope_2d(heads(sw_q, H), angs_swa), rope_2d(heads(sw_k, SWA_KV_HEADS), angs_swa),
                           heads(sw_v, SWA_KV_HEADS), kc_d, vc_d, swa_sink)
    x = x + g1 * (jnp.concatenate([y_a, y_b, y_c, y_d], axis=-1) @ w_out)
    h2 = rmsnorm(x, norm2_g) * (1.0 + sc2) + sh2
    if need_ctx:
        Tc = xc.shape[1]
        y_ctx = jnp.concatenate([
            ctx_attn(heads(na_qc, H), kc_a, vc_a, attn_scale),
            h_ctx.reshape(B, Tc, GROUP_WIDTH) * jax.nn.sigmoid(ml_oc),
            ctx_attn(qc_m, kc_m, vc_m, mla_scale),
            ctx_attn(heads(sw_qc, H), kc_d, vc_d, attn_scale, swa_sink)], axis=-1)
        xc = xc + g1c * (y_ctx @ w_out)
        h2c = rmsnorm(xc, norm2_g) * (1.0 + sc2c) + sh2c
        f = peer_ffn(jnp.concatenate([h2.reshape(B * T, D), h2c.reshape(B * Tc, D)], axis=0),
                     peer_wq, peer_keys, peer_u, peer_v)
        x = x + g2 * f[:B * T].reshape(B, T, D)
        xc = xc + g2c * f[B * T:].reshape(B, Tc, D)
        return x, xc
    x = x + g2 * peer_ffn(h2.reshape(B * T, D), peer_wq, peer_keys, peer_u, peer_v).reshape(B, T, D)
    return x, None


def _final_rmsnorm_kernel(x_ref, g_ref, o_ref):
    x = x_ref[...]
    o_ref[...] = x * lax.rsqrt(jnp.mean(x * x, axis=-1, keepdims=True) + EPS) * g_ref[...]


def final_rmsnorm(x, g):
    B, T, D = x.shape
    rows = 1024
    xf = x.reshape(B * T, D)
    out = pl.pallas_call(
        _final_rmsnorm_kernel,
        grid=(B * T // rows,),
        in_specs=[pl.BlockSpec((rows, D), lambda i: (i, 0)), pl.BlockSpec((1, D), lambda i: (0, 0))],
        out_specs=pl.BlockSpec((rows, D), lambda i: (i, 0)),
        out_shape=jax.ShapeDtypeStruct((B * T, D), x.dtype),
    )(xf, g.reshape(1, D))
    return out.reshape(B, T, D)


def kernel(x, c, ctx, c_ctx, norm1_g, norm2_g, w_ada, b_ada, w_in, na_rpb, ml_conv, ml_gate_b,
           mla_q_norm, mla_w_uq, mla_kv_norm, mla_w_ukv, swa_sink, w_out,
           peer_wq, peer_keys, peer_u, peer_v, final_norm_g):
    T = x.shape[1]
    angs_mla = axial_angles(T, MLA_ROPE)
    angs_swa = axial_angles(T, HEAD_DIM)
    xc = ctx
    for l in range(DEPTH):
        x, xc = hybrid_layer(x, xc, c, c_ctx, l < DEPTH - 1, angs_mla, angs_swa,
                             norm1_g[l], norm2_g[l], w_ada[l], b_ada[l], w_in[l], na_rpb[l],
                             ml_conv[l], ml_gate_b[l], mla_q_norm[l], mla_w_uq[l], mla_kv_norm[l],
                             mla_w_ukv[l], swa_sink[l], w_out[l], peer_wq[l], peer_keys[l],
                             peer_u[l], peer_v[l])
    return final_rmsnorm(x, final_norm_g)
```

```python
import functools

import jax
import jax.numpy as jnp
from jax import lax
import numpy as np
from jax.experimental import pallas as pl
from jax.experimental.pallas import tpu as pltpu

D_MODEL = 1024
BATCH = 2
SEQ = 16384
DEPTH = 2

CTX_LEN = 256
GRID_W = 64
N_MIXERS = 4
MIX_WIDTH = D_MODEL
GROUP_WIDTH = MIX_WIDTH // N_MIXERS
GROUP_HEADS = 4
HEAD_DIM = GROUP_WIDTH // GROUP_HEADS
NA_ROWS = 8
NA_COLS = 16
ML_CHUNK = 64
ML_CONV = 5
MLA_Q_RANK = 256
MLA_KV_RANK = 128
MLA_NOPE = 64
MLA_ROPE = 32
MLA_V = 64
SWA_KV_HEADS = 2
SWA_WINDOW = 128
ATTN_BLOCK = 128
PEER_HEADS = 8
PEER_NKEYS = 128
PEER_EXPERTS = PEER_NKEYS * PEER_NKEYS
PEER_DKEY = 128
PEER_TOPK = 16
PEER_BLOCK = 128
ROPE_BASE = 10000.0
EPS = 1e-6
IN_SIZES = (GROUP_WIDTH, GROUP_WIDTH, GROUP_WIDTH,
            2 * GROUP_WIDTH, GROUP_WIDTH, GROUP_WIDTH, 4 * GROUP_HEADS,
            MLA_Q_RANK, MLA_KV_RANK, MLA_ROPE,
            GROUP_WIDTH, SWA_KV_HEADS * HEAD_DIM, SWA_KV_HEADS * HEAD_DIM)
IN_WIDTH = sum(IN_SIZES)
F32 = jnp.float32


def rmsnorm(x, g):
    xf = x.astype(F32)
    y = xf * lax.rsqrt(jnp.mean(xf * xf, axis=-1, keepdims=True) + EPS) * g.astype(F32)
    return y.astype(x.dtype)


def heads(a, h):
    return a.reshape(a.shape[:-1] + (h, a.shape[-1] // h))


def split_cols(p):
    return jnp.split(p, np.cumsum(IN_SIZES)[:-1].tolist(), axis=-1)


def axial_angles(T, rot_dim):
    t = jnp.arange(T)
    row = (t // GRID_W).astype(F32)
    col = (t % GRID_W).astype(F32)
    half = rot_dim // 2
    inv = 1.0 / (ROPE_BASE ** (jnp.arange(0, half, 2, dtype=F32) / half))
    return row[:, None] * inv, col[:, None] * inv


def rope_1d(x, ang):
    cos = jnp.cos(ang)[None, :, None, :]
    sin = jnp.sin(ang)[None, :, None, :]
    x1, x2 = jnp.split(x.astype(F32), 2, axis=-1)
    return jnp.concatenate([x1 * cos - x2 * sin, x1 * sin + x2 * cos], axis=-1)


def rope_2d(x, angs):
    xr, xc = jnp.split(x, 2, axis=-1)
    return jnp.concatenate([rope_1d(xr, angs[0]), rope_1d(xc, angs[1])], axis=-1).astype(x.dtype)


def ctx_attn(q, k, v, scale, sink=None):
    rep = q.shape[2] // k.shape[2]
    k = jnp.repeat(k, rep, axis=2)
    v = jnp.repeat(v, rep, axis=2)
    s = jnp.einsum('bqhd,bkhd->bhqk', q, k).astype(F32) * scale
    nk = s.shape[-1]
    if sink is not None:
        s = jnp.concatenate([s, jnp.broadcast_to(sink.astype(F32)[None, :, None, None], s.shape[:-1] + (1,))], axis=-1)
    p = jax.nn.softmax(s, axis=-1)[..., :nk].astype(v.dtype)
    out = jnp.einsum('bhqk,bkhd->bqhd', p, v)
    return out.reshape(out.shape[:2] + (-1,))


NT_DIMS = (((1,), (1,)), ((), ()))
NA_SPAN = NA_ROWS * GRID_W


def _head_mask(width):
    rows = lax.broadcasted_iota(jnp.int32, (GROUP_HEADS * width, GROUP_WIDTH), 0) // width
    cols = lax.broadcasted_iota(jnp.int32, (GROUP_HEADS * width, GROUP_WIDTH), 1) // HEAD_DIM
    return (rows == cols).astype(F32)


def _na_kernel(q_ref, k_ref, v_ref, kc_ref, vc_ref, bias_ref, o_ref):
    r = pl.program_id(1)
    rows = pl.num_programs(1)
    rs = jnp.clip(r - NA_ROWS // 2, 0, rows - NA_ROWS)
    start = pl.multiple_of(rs * GRID_W, GRID_W)
    kw = k_ref[0, pl.ds(start, NA_SPAN), :]
    vw = v_ref[0, pl.ds(start, NA_SPAN), :]
    hm = _head_mask(GRID_W)
    q = q_ref[0] * (HEAD_DIM ** -0.5)
    q4 = (jnp.concatenate([q] * GROUP_HEADS, axis=0) * hm).astype(BF16)
    s_loc = lax.dot_general(q4, kw, NT_DIMS, preferred_element_type=F32) + bias_ref[rs - r + NA_ROWS - 1]
    s_ctx = lax.dot_general(q4, kc_ref[0], NT_DIMS, preferred_element_type=F32)
    m = jnp.maximum(jnp.max(s_loc, axis=-1, keepdims=True), jnp.max(s_ctx, axis=-1, keepdims=True))
    p_loc = jnp.exp(s_loc - m)
    p_ctx = jnp.exp(s_ctx - m)
    l = jnp.sum(p_loc, axis=-1, keepdims=True) + jnp.sum(p_ctx, axis=-1, keepdims=True)
    o = (jnp.dot(p_loc.astype(BF16), vw, preferred_element_type=F32)
         + jnp.dot(p_ctx.astype(BF16), vc_ref[0], preferred_element_type=F32)) * (hm / l)
    o_ref[0] = sum(o[h * GRID_W:(h + 1) * GRID_W] for h in range(GROUP_HEADS))


def _na_bias_table(rpb):
    c = np.arange(GRID_W)
    col_start = np.clip(c - NA_COLS // 2, 0, GRID_W - NA_COLS)
    valid = (c[None, :] >= col_start[:, None]) & (c[None, :] < col_start[:, None] + NA_COLS)
    dc = np.clip(c[None, :] - c[:, None] + NA_COLS - 1, 0, 2 * NA_COLS - 2)
    dr = np.arange(NA_ROWS)[:, None] + np.arange(NA_ROWS)[None, :]
    t = rpb.astype(F32)[:, dr][..., dc]
    t = jnp.where(valid[None, None, None], t, -jnp.inf)
    return jnp.transpose(t, (1, 0, 3, 2, 4)).reshape(NA_ROWS, GROUP_HEADS * GRID_W, NA_SPAN)


def neighbourhood_attention(q, k, v, kc, vc, rpb):
    B, T, C = q.shape
    rows = T // GRID_W
    n_ctx = kc.shape[1]
    bias = _na_bias_table(rpb)
    full = lambda n: pl.BlockSpec((1, n, C), lambda b, r: (b, 0, 0))
    return pl.pallas_call(
        _na_kernel,
        grid=(B, rows),
        in_specs=[pl.BlockSpec((1, GRID_W, C), lambda b, r: (b, r, 0)),
                  full(T), full(T), full(n_ctx), full(n_ctx),
                  pl.BlockSpec(bias.shape, lambda b, r: (0, 0, 0))],
        out_specs=pl.BlockSpec((1, GRID_W, C), lambda b, r: (b, r, 0)),
        out_shape=jax.ShapeDtypeStruct((B, T, C), F32),
        compiler_params=pltpu.CompilerParams(vmem_limit_bytes=VMEM_LIMIT_BYTES),
    )(q, k.astype(BF16), v.astype(BF16), kc.astype(BF16), vc.astype(BF16), bias)


def short_conv(a, w):
    T = a.shape[1]
    pad = w.shape[0] // 2
    ap = jnp.pad(a, ((0, 0), (pad, pad), (0, 0)))
    out = ap[:, :T] * w[0]
    for j in range(1, w.shape[0]):
        out = out + ap[:, j:j + T] * w[j]
    return out


ML_CHUNKS_PER_STEP = CTX_LEN // ML_CHUNK


def _bmm(a, b, contract):
    return lax.dot_general(a.astype(BF16), b.astype(BF16), (contract, ((0,), (0,))),
                           preferred_element_type=F32)


def _mlstm_chunk(qt, kt, vt, irow, brow, state, backward):
    L = ML_CHUNK
    C, nrow, m = state
    row = lax.broadcasted_iota(jnp.int32, (1, L, L), 1)
    col = lax.broadcasted_iota(jnp.int32, (1, L, L), 2)
    seen = (row <= col) if backward else (row >= col)
    eye = row == col

    def as_col(r):
        return jnp.sum(jnp.where(eye, r, 0.0), axis=2, keepdims=True)

    blast = brow[:, :, 0:1] if backward else brow[:, :, L - 1:L]
    rrow = brow - irow
    bcol = as_col(brow)
    d_log = jnp.where(seen, bcol - rrow, -jnp.inf)
    inter = bcol + m
    m_t = jnp.maximum(inter, jnp.max(d_log, axis=2, keepdims=True))
    w = jnp.exp(d_log - m_t)
    a = jnp.exp(inter - m_t)
    s = _bmm(qt, kt, ((2,), (2,))) * w
    num = _bmm(s, vt, ((2,), (1,))) + a * _bmm(qt, C, ((2,), (1,)))
    den = jnp.sum(s, axis=2, keepdims=True) + a * jnp.sum(qt * nrow, axis=2, keepdims=True)
    h = num / jnp.maximum(jnp.abs(den), jnp.exp(-m_t))
    g = blast - rrow
    m_new = jnp.maximum(blast + m, jnp.max(g, axis=2, keepdims=True))
    kw = kt * as_col(jnp.exp(g - m_new))
    decay = jnp.exp(blast + m - m_new)
    C = decay * C + _bmm(jnp.swapaxes(kw, 1, 2), vt, ((2,), (1,)))
    nrow = decay * nrow + jnp.sum(kw, axis=1, keepdims=True)
    return h, (C, nrow, m_new)


def _mlstm_kernel(qf_ref, kf_ref, vf_ref, if_ref, bf_ref, qb_ref, kb_ref, vb_ref, ib_ref, bb_ref,
                  hf_ref, hb_ref, c_ref, n_ref, m_ref):
    N, L = qf_ref.shape[0], ML_CHUNK

    @pl.when(pl.program_id(0) == 0)
    def _():
        c_ref[...] = jnp.zeros(c_ref.shape, F32)
        n_ref[...] = jnp.zeros(n_ref.shape, F32)
        m_ref[...] = jnp.zeros(m_ref.shape, F32)

    fwd = (c_ref[:N], n_ref[:N], m_ref[:N])
    bwd = (c_ref[N:], n_ref[N:], m_ref[N:])
    for c in range(ML_CHUNKS_PER_STEP):
        rows = slice(c * L, (c + 1) * L)
        h, fwd = _mlstm_chunk(qf_ref[:, rows, :], kf_ref[:, rows, :], vf_ref[:, rows, :],
                              if_ref[:, 0, c:c + 1, :], bf_ref[:, 0, c:c + 1, :], fwd, False)
        hf_ref[:, rows, :] = h
        cb = ML_CHUNKS_PER_STEP - 1 - c
        rows = slice(cb * L, (cb + 1) * L)
        h, bwd = _mlstm_chunk(qb_ref[:, rows, :], kb_ref[:, rows, :], vb_ref[:, rows, :],
                              ib_ref[:, 0, cb:cb + 1, :], bb_ref[:, 0, cb:cb + 1, :], bwd, True)
        hb_ref[:, rows, :] = h
    for i, ref in enumerate((c_ref, n_ref, m_ref)):
        ref[:N] = fwd[i]
        ref[N:] = bwd[i]


def mlstm_scan(q, k, v, gates_f, gates_b, n_ctx):
    B, T, H, d = q.shape
    CB, L = ML_CHUNKS_PER_STEP, ML_CHUNK
    assert n_ctx == CB * L and T % (CB * L) == 0
    N, steps = B * H, T // (CB * L)
    hm = lambda a: jnp.swapaxes(a, 1, 2).reshape(N, T, d)
    gates = lambda a: jnp.swapaxes(a, 1, 2).reshape(N, steps, CB, L)
    chunked = lambda a: a.reshape(B, T // L, L, H)
    b_f = jnp.cumsum(chunked(gates_f[1]), axis=2).reshape(B, T, H)
    b_b = lax.cumsum(chunked(gates_b[1]), axis=2, reverse=True).reshape(B, T, H)
    back = lambda j: jnp.where(j == 0, 0, steps - j)
    seq_f = pl.BlockSpec((N, CB * L, d), lambda j: (0, j, 0))
    seq_b = pl.BlockSpec((N, CB * L, d), lambda j: (0, back(j), 0))
    gate_f = pl.BlockSpec((N, 1, CB, L), lambda j: (0, j, 0, 0))
    gate_b = pl.BlockSpec((N, 1, CB, L), lambda j: (0, back(j), 0, 0))
    qh, kh, vh = hm(q), hm(k), hm(v)
    hf, hb = pl.pallas_call(
        _mlstm_kernel,
        grid=(steps,),
        in_specs=[seq_f, seq_f, seq_f, gate_f, gate_f, seq_b, seq_b, seq_b, gate_b, gate_b],
        out_specs=[seq_f, seq_b],
        out_shape=[jax.ShapeDtypeStruct((N, T, d), F32)] * 2,
        scratch_shapes=[pltpu.VMEM((2 * N, d, d), F32), pltpu.VMEM((2 * N, 1, d), F32),
                        pltpu.VMEM((2 * N, 1, 1), F32)],
        compiler_params=pltpu.CompilerParams(vmem_limit_bytes=VMEM_LIMIT_BYTES),
    )(qh, kh, vh, gates(gates_f[0]), gates(b_f), qh, kh, vh, gates(gates_b[0]), gates(b_b))
    return jnp.swapaxes((hf + hb).reshape(B, H, T, d), 1, 2)


def mlstm_prep(qk, v, gates, conv_w, gate_b):
    qk = jax.nn.silu(short_conv(qk, conv_w))
    q, k = jnp.split(qk, 2, axis=-1)
    g = (gates + gate_b).astype(F32)
    i_f, f_f, i_b, f_b = jnp.split(g, 4, axis=-1)
    return (heads(q, GROUP_HEADS) * HEAD_DIM ** -0.5, heads(k, GROUP_HEADS), heads(v, GROUP_HEADS),
            (i_f, jax.nn.log_sigmoid(f_f), i_b, jax.nn.log_sigmoid(f_b)))


def mlstm_mixer(lat, ctx, conv_w, gate_b):
    ql, kl, vl, gl = mlstm_prep(lat[0], lat[1], lat[2], conv_w, gate_b)
    qc, kc, vc, gc = mlstm_prep(ctx[0], ctx[1], ctx[2], conv_w, gate_b)
    Tc = qc.shape[1]
    cat = lambda c_, l_: jnp.concatenate([c_, l_], axis=1)
    h = mlstm_scan(cat(qc, ql), cat(kc, kl), cat(vc, vl),
                   (cat(gc[0], gl[0]), cat(gc[1], gl[1])), (cat(gc[2], gl[2]), cat(gc[3], gl[3])), Tc)
    return h[:, Tc:], h[:, :Tc]


def mla_project(cq, ckv, kr, q_norm, w_uq, kv_norm, w_ukv, angs):
    q = heads(rmsnorm(cq, q_norm) @ w_uq, GROUP_HEADS)
    kv = heads(rmsnorm(ckv, kv_norm) @ w_ukv, GROUP_HEADS)
    q_nope, q_rope = q[..., :MLA_NOPE], q[..., MLA_NOPE:]
    k_nope, v = kv[..., :MLA_NOPE], kv[..., MLA_NOPE:]
    k_rope = kr[:, :, None, :]
    if angs is not None:
        q_rope = rope_2d(q_rope, angs)
        k_rope = rope_2d(k_rope, angs)
    k_rope = jnp.broadcast_to(k_rope, k_nope.shape[:-1] + (MLA_ROPE,))
    return (jnp.concatenate([q_nope, q_rope], axis=-1), jnp.concatenate([k_nope, k_rope], axis=-1), v)


LOG2_E = 1.4426950408889634
DENSE_Q_TILE = 1024
DENSE_Q_SUB = 256
DENSE_Q_UNROLL = 4
DENSE_K_TILE_MAX = 1664


def _dense_attn_kernel(q_ref, k_ref, v_ref, o_ref, m_ref, l_ref, acc_ref, *, scale):
    j = pl.program_id(3)

    @pl.when(j == 0)
    def _():
        m_ref[...] = jnp.full(m_ref.shape, -jnp.inf, F32)
        l_ref[...] = jnp.zeros(l_ref.shape, F32)
        acc_ref[...] = jnp.zeros(acc_ref.shape, F32)

    def rows(i, carry):
        for u in range(DENSE_Q_UNROLL):
            r = pl.ds(pl.multiple_of((i * DENSE_Q_UNROLL + u) * DENSE_Q_SUB, DENSE_Q_SUB), DENSE_Q_SUB)
            s = lax.dot_general(q_ref[0, 0, r, :], k_ref[0, 0], NT_DIMS,
                                preferred_element_type=F32) * (scale * LOG2_E)
            m_prev = m_ref[r, :]
            m_new = jnp.maximum(m_prev, jnp.max(s, axis=-1, keepdims=True))
            alpha = jnp.exp2(m_prev - m_new)
            p = jnp.exp2(s - m_new)
            l_ref[r, :] = alpha * l_ref[r, :] + jnp.sum(p, axis=-1, keepdims=True)
            acc_ref[r, :] = alpha * acc_ref[r, :] + jnp.dot(p.astype(BF16), v_ref[0, 0],
                                                            preferred_element_type=F32)
            m_ref[r, :] = m_new
        return carry

    lax.fori_loop(0, q_ref.shape[2] // (DENSE_Q_SUB * DENSE_Q_UNROLL), rows, 0)

    @pl.when(j == pl.num_programs(3) - 1)
    def _():
        o_ref[0, 0] = acc_ref[...] / l_ref[...]


def block_dense_attention(q, k_all, v_all, scale):
    B, T, H, dq = q.shape
    NK, dv = k_all.shape[1], v_all.shape[-1]
    tq = min(DENSE_Q_TILE, T)
    tk = max(t for t in range(LANES, DENSE_K_TILE_MAX + 1, LANES) if NK % t == 0)
    hm = lambda a: jnp.swapaxes(a, 1, 2).astype(BF16)
    out = pl.pallas_call(
        functools.partial(_dense_attn_kernel, scale=scale),
        grid=(B, H, T // tq, NK // tk),
        in_specs=[pl.BlockSpec((1, 1, tq, dq), lambda b, h, i, j: (b, h, i, 0)),
                  pl.BlockSpec((1, 1, tk, dq), lambda b, h, i, j: (b, h, j, 0)),
                  pl.BlockSpec((1, 1, tk, dv), lambda b, h, i, j: (b, h, j, 0))],
        out_specs=pl.BlockSpec((1, 1, tq, dv), lambda b, h, i, j: (b, h, i, 0)),
        out_shape=jax.ShapeDtypeStruct((B, H, T, dv), F32),
        scratch_shapes=[pltpu.VMEM((tq, 1), F32), pltpu.VMEM((tq, 1), F32), pltpu.VMEM((tq, dv), F32)],
        compiler_params=pltpu.CompilerParams(vmem_limit_bytes=VMEM_LIMIT_BYTES),
    )(hm(q), hm(k_all), hm(v_all))
    return jnp.swapaxes(out, 1, 2).reshape(B, T, H * dv)


SWA_SPAN = ATTN_BLOCK + 2 * SWA_WINDOW


def _swa_kernel(q_ref, k_ref, v_ref, kc_ref, vc_ref, sink_ref, o_ref):
    n = pl.program_id(1)
    T = k_ref.shape[1]
    start = pl.multiple_of(jnp.clip(n * ATTN_BLOCK - SWA_WINDOW, 0, T - SWA_SPAN), ATTN_BLOCK)
    kw = k_ref[0, pl.ds(start, SWA_SPAN), :]
    vw = v_ref[0, pl.ds(start, SWA_SPAN), :]
    hm = _head_mask(ATTN_BLOCK)
    q = q_ref[0] * (HEAD_DIM ** -0.5)
    q4 = (jnp.concatenate([q] * GROUP_HEADS, axis=0) * hm).astype(BF16)
    rows = GROUP_HEADS * ATTN_BLOCK
    q_pos = n * ATTN_BLOCK + lax.broadcasted_iota(jnp.int32, (rows, SWA_SPAN), 0) % ATTN_BLOCK
    k_pos = start + lax.broadcasted_iota(jnp.int32, (rows, SWA_SPAN), 1)
    s_loc = lax.dot_general(q4, kw, NT_DIMS, preferred_element_type=F32)
    s_loc = jnp.where(jnp.abs(q_pos - k_pos) <= SWA_WINDOW, s_loc, -jnp.inf)
    s_ctx = lax.dot_general(q4, kc_ref[0], NT_DIMS, preferred_element_type=F32)
    sink = sink_ref[...]
    m = jnp.maximum(jnp.maximum(jnp.max(s_loc, axis=-1, keepdims=True),
                                jnp.max(s_ctx, axis=-1, keepdims=True)), sink)
    p_loc = jnp.exp(s_loc - m)
    p_ctx = jnp.exp(s_ctx - m)
    l = jnp.sum(p_loc, axis=-1, keepdims=True) + jnp.sum(p_ctx, axis=-1, keepdims=True) + jnp.exp(sink - m)
    o = (jnp.dot(p_loc.astype(BF16), vw, preferred_element_type=F32)
         + jnp.dot(p_ctx.astype(BF16), vc_ref[0], preferred_element_type=F32)) * (hm / l)
    o_ref[0] = sum(o[h * ATTN_BLOCK:(h + 1) * ATTN_BLOCK] for h in range(GROUP_HEADS))


def window_attention(q, k, v, kc, vc, sink):
    B, T, H, d = q.shape
    G = H // k.shape[2]
    n_ctx = kc.shape[1]
    C = H * d
    rep = lambda a: jnp.repeat(a, G, axis=2).reshape(a.shape[0], a.shape[1], C).astype(BF16)
    sink_rows = jnp.repeat(sink.astype(F32), ATTN_BLOCK).reshape(H * ATTN_BLOCK, 1)
    full = lambda n: pl.BlockSpec((1, n, C), lambda b, i: (b, 0, 0))
    return pl.pallas_call(
        _swa_kernel,
        grid=(B, T // ATTN_BLOCK),
        in_specs=[pl.BlockSpec((1, ATTN_BLOCK, C), lambda b, i: (b, i, 0)),
                  full(T), full(T), full(n_ctx), full(n_ctx),
                  pl.BlockSpec(sink_rows.shape, lambda b, i: (0, 0))],
        out_specs=pl.BlockSpec((1, ATTN_BLOCK, C), lambda b, i: (b, i, 0)),
        out_shape=jax.ShapeDtypeStruct((B, T, C), F32),
        compiler_params=pltpu.CompilerParams(vmem_limit_bytes=VMEM_LIMIT_BYTES),
    )(q.reshape(B, T, C), rep(k), rep(v), rep(kc), rep(vc), sink_rows)


BF16 = jnp.bfloat16
LANES = 128
SUBLANES = 8
ROW_SEGS = D_MODEL // LANES
ROW_WORDS = ROW_SEGS // 2
PEER_PICKS = PEER_HEADS * PEER_TOPK
PEER_TOPK_TOKENS = 256
PEER_GATHER_TOKENS = 128
PEER_ACT_UNROLL = 4
VMEM_LIMIT_BYTES = 56 * 1024 * 1024


def _split_bf16(x, parts):
    out = []
    for _ in range(parts):
        p = x.astype(BF16)
        out.append(p)
        x = x - p.astype(F32)
    return out


def _topk_rows(s, k):
    n = s.shape[0]
    iota = lax.broadcasted_iota(jnp.int32, s.shape, 0)
    vals, idxs = [], []
    for _ in range(k):
        m = jnp.max(s, axis=0, keepdims=True)
        i = jnp.min(jnp.where(s == m, iota, n), axis=0, keepdims=True)
        vals.append(m)
        idxs.append(i)
        s = jnp.where(iota == i, -jnp.inf, s)
    return jnp.concatenate(vals, axis=0), jnp.concatenate(idxs, axis=0)


def _peer_topk_kernel(x_ref, wq_ref, keys_ref, eidx_ref, gate_ref):
    xb = x_ref[...].astype(BF16)
    q = jnp.dot(xb, wq_ref[...], preferred_element_type=F32)
    nt = (((1,), (1,)), ((), ()))
    sv, si = [], []
    for p in range(2):
        qp = q[:, p * PEER_DKEY:(p + 1) * PEER_DKEY].astype(BF16)
        s = lax.dot_general(keys_ref[0, p], qp, nt, preferred_element_type=F32)
        v_, i_ = _topk_rows(s, PEER_TOPK)
        sv.append(v_)
        si.append(i_)
    cs, ce = [], []
    half = PEER_TOPK // 2
    for a in range(half):
        nb = PEER_TOPK if a == 0 else half
        cs.append(sv[0][a:a + 1] + sv[1][:nb])
        ce.append(si[0][a:a + 1] * PEER_NKEYS + si[1][:nb])
    cs.append(sv[0][half:] + sv[1][0:1])
    ce.append(si[0][half:] * PEER_NKEYS + si[1][0:1])
    cand_s = jnp.concatenate(cs, axis=0)
    cand_e = jnp.concatenate(ce, axis=0)
    fs, fpos = _topk_rows(cand_s, PEER_TOPK)
    iota = lax.broadcasted_iota(jnp.int32, cand_e.shape, 0)
    eidx = [jnp.max(jnp.where(iota == fpos[j:j + 1], cand_e, -1), axis=0, keepdims=True)
            for j in range(PEER_TOPK)]
    ex = jnp.exp(fs - fs[0:1])
    eidx_ref[0] = jnp.concatenate(eidx, axis=0)
    gate_ref[0] = ex / jnp.sum(ex, axis=0, keepdims=True)


def peer_topk(h, wq, sub_keys):
    N, D = h.shape
    T = PEER_TOPK_TOKENS
    wqb = wq.astype(BF16)
    kb = sub_keys.astype(BF16)
    eidx, gate = pl.pallas_call(
        _peer_topk_kernel,
        grid=(N // T, PEER_HEADS),
        in_specs=[pl.BlockSpec((T, D), lambda i, h_: (i, 0)),
                  pl.BlockSpec((D, 2 * PEER_DKEY), lambda i, h_: (0, h_)),
                  pl.BlockSpec((1, 2, PEER_NKEYS, PEER_DKEY), lambda i, h_: (h_, 0, 0, 0))],
        out_specs=[pl.BlockSpec((1, PEER_TOPK, T), lambda i, h_: (h_, 0, i)),
                   pl.BlockSpec((1, PEER_TOPK, T), lambda i, h_: (h_, 0, i))],
        out_shape=[jax.ShapeDtypeStruct((PEER_HEADS, PEER_TOPK, N), jnp.int32),
                   jax.ShapeDtypeStruct((PEER_HEADS, PEER_TOPK, N), F32)],
        compiler_params=pltpu.CompilerParams(vmem_limit_bytes=VMEM_LIMIT_BYTES),
    )(h, wqb, kb)
    return eidx.reshape(PEER_PICKS, N), gate.reshape(PEER_PICKS, N)


def pack_expert_table(tab):
    E = tab.shape[0]
    t = tab.astype(BF16).reshape(E, ROW_WORDS, 2, LANES)
    t = jnp.swapaxes(t, -1, -2)
    return lax.bitcast_convert_type(t, jnp.uint32).reshape(E * ROW_WORDS, LANES)


def _stage_rows(idx_ref, tab_ref, stage_ref, t):
    for k in range(PEER_PICKS):
        off = pl.multiple_of(idx_ref[t, k], ROW_WORDS)
        stage_ref[k * ROW_WORDS:(k + 1) * ROW_WORDS, :] = tab_ref[pl.ds(off, ROW_WORDS), :]
    return pltpu.bitcast(stage_ref[...], BF16)


def _peer_act_kernel(idx_ref, x_ref, gate_ref, tab_ref, seg_mask_ref, group_ref, w_ref,
                     stage_ref, rows_ref):
    T = x_ref.shape[0]
    U = stage_ref.shape[0]

    def tokens(g, carry):
        for j in range(U):
            t = g * U + j
            sb = _stage_rows(idx_ref, tab_ref, stage_ref.at[j], t)
            xs = jnp.concatenate(_split_bf16(x_ref[t], 2), axis=0)
            r = lax.dot_general(xs, sb, NT_DIMS, preferred_element_type=F32)
            r = r * seg_mask_ref[...]
            rows_ref[t] = r[:SUBLANES] + r[SUBLANES:]
        return carry

    lax.fori_loop(0, T // U, tokens, 0)
    rows = rows_ref[...].reshape(T * SUBLANES, PEER_PICKS * ROW_SEGS)
    part = jnp.zeros((T * SUBLANES, PEER_PICKS), F32)
    for piece in _split_bf16(rows, 3):
        part = part + jnp.dot(piece, group_ref[...], preferred_element_type=F32)
    act = jnp.sum(part.reshape(T, SUBLANES, PEER_PICKS), axis=1)
    w_ref[...] = gate_ref[...] * (0.5 * act * (1.0 + lax.erf(act * (2.0 ** -0.5))))


def _peer_out_kernel(idx_ref, w_ref, x_ref, g_ref, tab_ref, expand_ref, seg_mask_ref, f_ref, stage_ref):
    T = w_ref.shape[0]
    U = stage_ref.shape[0]

    def tokens(g, carry):
        w8 = w_ref[pl.ds(pl.multiple_of(g * U, U), U), :]
        for j in range(U):
            t = g * U + j
            sb = _stage_rows(idx_ref, tab_ref, stage_ref.at[j], t)
            lhs = jnp.concatenate([jnp.broadcast_to(p, (SUBLANES, PEER_PICKS))
                                   for p in _split_bf16(w8[j:j + 1], 2)], axis=0)
            wrep = jnp.dot(lhs, expand_ref[...], preferred_element_type=F32)
            wsel = (wrep * seg_mask_ref[...]).astype(BF16)
            o = jnp.dot(wsel, sb, preferred_element_type=F32)
            f_ref[t] = x_ref[t] + g_ref[0] * (o[:SUBLANES] + o[SUBLANES:])
        return carry

    lax.fori_loop(0, T // U, tokens, 0)


def _peer_constants():
    cols = np.arange(PEER_PICKS * ROW_SEGS)
    seg_mask = (cols[None, :] % ROW_SEGS == np.arange(2 * SUBLANES)[:, None] % SUBLANES)
    group = (cols[:, None] // ROW_SEGS == np.arange(PEER_PICKS)[None, :])
    return (jnp.asarray(seg_mask, F32), jnp.asarray(group, BF16), jnp.asarray(group.T, BF16))


def peer_ffn(h, x, gate2, group_tokens, wq, sub_keys, u_packed, v_packed):
    N, D = h.shape
    T = PEER_GATHER_TOKENS
    eidx, gate = peer_topk(h, wq, sub_keys)
    seg_mask, group, expand = _peer_constants()
    rows3 = lambda a: a.reshape(a.shape[0], ROW_SEGS, LANES)
    offs = eidx.T * ROW_WORDS
    idx_spec = pl.BlockSpec((T, PEER_PICKS), lambda i: (i, 0), memory_space=pltpu.SMEM)
    tab_spec = pl.BlockSpec(u_packed.shape, lambda i: (0, 0), pipeline_mode=pl.Buffered(1))
    tok_spec = pl.BlockSpec((T, ROW_SEGS, LANES), lambda i: (i, 0, 0))
    const = lambda shape: pl.BlockSpec(shape, lambda i: (0, 0))
    params = pltpu.CompilerParams(vmem_limit_bytes=VMEM_LIMIT_BYTES)
    w = pl.pallas_call(
        _peer_act_kernel,
        grid=(N // T,),
        in_specs=[idx_spec, tok_spec,
                  pl.BlockSpec((T, PEER_PICKS), lambda i: (i, 0)),
                  tab_spec, const(seg_mask.shape), const(group.shape)],
        out_specs=pl.BlockSpec((T, PEER_PICKS), lambda i: (i, 0)),
        out_shape=jax.ShapeDtypeStruct((N, PEER_PICKS), F32),
        scratch_shapes=[pltpu.VMEM((PEER_ACT_UNROLL, PEER_PICKS * ROW_WORDS, LANES), jnp.uint32),
                        pltpu.VMEM((T, SUBLANES, PEER_PICKS * ROW_SEGS), F32)],
        compiler_params=params,
    )(offs, rows3(h), gate.T, u_packed, seg_mask, group)
    out = pl.pallas_call(
        _peer_out_kernel,
        grid=(N // T,),
        in_specs=[idx_spec,
                  pl.BlockSpec((T, PEER_PICKS), lambda i: (i, 0)),
                  tok_spec,
                  pl.BlockSpec((1, ROW_SEGS, LANES), lambda i: (i // (group_tokens // T), 0, 0)),
                  tab_spec, const(expand.shape), const(seg_mask.shape)],
        out_specs=tok_spec,
        out_shape=jax.ShapeDtypeStruct((N, ROW_SEGS, LANES), F32),
        scratch_shapes=[pltpu.VMEM((SUBLANES, PEER_PICKS * ROW_WORDS, LANES), jnp.uint32)],
        compiler_params=params,
    )(offs, w, rows3(x), rows3(gate2), v_packed, expand, seg_mask)
    return out.reshape(N, D)


PROJ_TOKENS = 512
MOD_ROWS = SUBLANES
IN_ALIGNED = tuple(i for i, s_ in enumerate(IN_SIZES) if s_ % LANES == 0)
IN_SMALL = tuple(i for i, s_ in enumerate(IN_SIZES) if s_ % LANES)


def _rms_modulate(x, gain, scale1p, shift):
    r = lax.rsqrt(jnp.mean(x * x, axis=-1, keepdims=True) + EPS)
    return (x * r * gain) * scale1p + shift


def _in_proj_kernel(x_ref, mod_ref, w_ref, *out_refs):
    mod = mod_ref[0]
    h = _rms_modulate(x_ref[...], mod[0:1], mod[1:2], mod[2:3])
    y = jnp.dot(h.astype(BF16), w_ref[...], preferred_element_type=F32)
    off = 0
    for o_ref in out_refs:
        o_ref[...] = y[:, off:off + o_ref.shape[1]].astype(o_ref.dtype)
        off += o_ref.shape[1]


def in_projection(x, mod, w_in, group_tokens):
    N, D = x.shape
    T = min(PROJ_TOKENS, group_tokens)
    starts = np.cumsum((0,) + IN_SIZES)
    cols = np.concatenate([np.arange(starts[i], starts[i + 1]) for i in IN_ALIGNED + IN_SMALL])
    small = sum(IN_SIZES[i] for i in IN_SMALL)
    wp = jnp.pad(w_in[:, cols], ((0, 0), (0, -small % LANES))).astype(BF16)
    widths = [IN_SIZES[i] for i in IN_ALIGNED] + [small + (-small % LANES)]
    outs = pl.pallas_call(
        _in_proj_kernel,
        grid=(N // T,),
        in_specs=[pl.BlockSpec((T, D), lambda i: (i, 0)),
                  pl.BlockSpec((1, MOD_ROWS, D), lambda i: (i // (group_tokens // T), 0, 0)),
                  pl.BlockSpec(wp.shape, lambda i: (0, 0))],
        out_specs=[pl.BlockSpec((T, w_), lambda i: (i, 0)) for w_ in widths],
        out_shape=[jax.ShapeDtypeStruct((N, w_), F32) for w_ in widths],
        compiler_params=pltpu.CompilerParams(vmem_limit_bytes=VMEM_LIMIT_BYTES),
    )(x, mod, wp)
    groups = dict(zip(IN_ALIGNED, outs[:-1]))
    off = 0
    for i in IN_SMALL:
        groups[i] = outs[-1][:, off:off + IN_SIZES[i]]
        off += IN_SIZES[i]
    return [groups[i] for i in range(len(IN_SIZES))]


def _out_proj_kernel(ya_ref, hl_ref, mo_ref, yc_ref, yd_ref, x_ref, mod_ref, w_ref, xo_ref, h2_ref):
    yb = hl_ref[...] * jax.nn.sigmoid(mo_ref[...])
    y = jnp.concatenate([ya_ref[...], yb, yc_ref[...], yd_ref[...]], axis=-1).astype(BF16)
    mod = mod_ref[0]
    xn = x_ref[...] + mod[0:1] * jnp.dot(y, w_ref[...], preferred_element_type=F32)
    xo_ref[...] = xn
    h2_ref[...] = _rms_modulate(xn, mod[1:2], mod[2:3], mod[3:4])


def out_projection(ya, hl, mo, yc, yd, x, mod, w_out, group_tokens):
    N, D = x.shape
    T = min(PROJ_TOKENS, group_tokens)
    part = pl.BlockSpec((T, GROUP_WIDTH), lambda i: (i, 0))
    tok = pl.BlockSpec((T, D), lambda i: (i, 0))
    return pl.pallas_call(
        _out_proj_kernel,
        grid=(N // T,),
        in_specs=[part, part, part, part, part, tok,
                  pl.BlockSpec((1, MOD_ROWS, D), lambda i: (i // (group_tokens // T), 0, 0)),
                  pl.BlockSpec(w_out.shape, lambda i: (0, 0))],
        out_specs=[tok, tok],
        out_shape=[jax.ShapeDtypeStruct((N, D), F32)] * 2,
        compiler_params=pltpu.CompilerParams(vmem_limit_bytes=VMEM_LIMIT_BYTES),
    )(ya, hl, mo, yc, yd, x, mod, w_out.astype(BF16))


def _mod_rows(*rows):
    m = jnp.stack([jnp.broadcast_to(r, rows[-1].shape) for r in rows], axis=1)
    return jnp.pad(m, ((0, 0), (0, MOD_ROWS - len(rows)), (0, 0)))


def hybrid_layer(x, xc, c, c_ctx, need_ctx, angs_mla, angs_swa,
                 norm1_g, norm2_g, w_ada, b_ada, w_in, na_rpb, ml_conv, ml_gate_b,
                 mla_q_norm, mla_w_uq, mla_kv_norm, mla_w_ukv, swa_sink, w_out,
                 peer_wq, peer_keys, peer_u, peer_v):
    B, T, D = x.shape
    Tc = xc.shape[1]
    H = GROUP_HEADS
    flat = lambda a: a.reshape(-1, a.shape[-1])
    sh1, sc1, g1, sh2, sc2, g2 = jnp.split(jax.nn.silu(c) @ w_ada + b_ada, 6, axis=-1)
    sh1c, sc1c, g1c, sh2c, sc2c, g2c = jnp.split((jax.nn.silu(c_ctx) @ w_ada + b_ada)[None], 6, axis=-1)
    lat = in_projection(flat(x), _mod_rows(norm1_g, 1.0 + sc1, sh1), w_in, T)
    cx = in_projection(flat(xc), _mod_rows(norm1_g, 1.0 + sc1c, sh1c), w_in, B * Tc)
    (na_q, na_k, na_v, ml_qk, ml_v, ml_o, ml_g,
     mla_cq, mla_ckv, mla_kr, sw_q, sw_k, sw_v) = [a.reshape(B, T, -1) for a in lat]
    (na_qc, na_kc, na_vc, ml_qkc, ml_vc, ml_oc, ml_gc,
     mla_cqc, mla_ckvc, mla_krc, sw_qc, sw_kc, sw_vc) = [a.reshape(B, Tc, -1) for a in cx]
    attn_scale = HEAD_DIM ** -0.5
    mla_scale = (MLA_NOPE + MLA_ROPE) ** -0.5
    kc_a, vc_a = heads(na_kc, H), heads(na_vc, H)
    y_a = neighbourhood_attention(na_q, na_k, na_v, na_kc, na_vc, na_rpb)
    h_lat, h_ctx = mlstm_mixer((ml_qk, ml_v, ml_g), (ml_qkc, ml_vc, ml_gc), ml_conv, ml_gate_b)
    q_m, k_m, v_m = mla_project(mla_cq, mla_ckv, mla_kr, mla_q_norm, mla_w_uq, mla_kv_norm, mla_w_ukv, angs_mla)
    qc_m, kc_m, vc_m = mla_project(mla_cqc, mla_ckvc, mla_krc, mla_q_norm, mla_w_uq, mla_kv_norm, mla_w_ukv, None)
    y_c = block_dense_attention(q_m, jnp.concatenate([kc_m, k_m], axis=1), jnp.concatenate([vc_m, v_m], axis=1), mla_scale)
    kc_d, vc_d = heads(sw_kc, SWA_KV_HEADS), heads(sw_vc, SWA_KV_HEADS)
    y_d = window_attention(rope_2d(heads(sw_q, H), angs_swa), rope_2d(heads(sw_k, SWA_KV_HEADS), angs_swa),
                           heads(sw_v, SWA_KV_HEADS), kc_d, vc_d, swa_sink)
    x2, h2 = out_projection(flat(y_a), h_lat.reshape(B * T, GROUP_WIDTH), flat(ml_o), flat(y_c), flat(y_d),
                            flat(x), _mod_rows(g1, norm2_g, 1.0 + sc2, sh2), w_out, T)
    u_packed, v_packed = pack_expert_table(peer_u), pack_expert_table(peer_v)
    x = peer_ffn(h2, x2, g2, T, peer_wq, peer_keys, u_packed, v_packed).reshape(B, T, D)
    if not need_ctx:
        return x, None
    xc2, h2c = out_projection(flat(ctx_attn(heads(na_qc, H), kc_a, vc_a, attn_scale)),
                              h_ctx.reshape(B * Tc, GROUP_WIDTH), flat(ml_oc),
                              flat(ctx_attn(qc_m, kc_m, vc_m, mla_scale)),
                              flat(ctx_attn(heads(sw_qc, H), kc_d, vc_d, attn_scale, swa_sink)),
                              flat(xc), _mod_rows(g1c, norm2_g, 1.0 + sc2c, sh2c), w_out, B * Tc)
    xc = peer_ffn(h2c, xc2, g2c, B * Tc, peer_wq, peer_keys, u_packed, v_packed).reshape(B, Tc, D)
    return x, xc


def _final_rmsnorm_kernel(x_ref, g_ref, o_ref):
    x = x_ref[...]
    o_ref[...] = x * lax.rsqrt(jnp.mean(x * x, axis=-1, keepdims=True) + EPS) * g_ref[...]


def final_rmsnorm(x, g):
    B, T, D = x.shape
    rows = 1024
    xf = x.reshape(B * T, D)
    out = pl.pallas_call(
        _final_rmsnorm_kernel,
        grid=(B * T // rows,),
        in_specs=[pl.BlockSpec((rows, D), lambda i: (i, 0)), pl.BlockSpec((1, D), lambda i: (0, 0))],
        out_specs=pl.BlockSpec((rows, D), lambda i: (i, 0)),
        out_shape=jax.ShapeDtypeStruct((B * T, D), x.dtype),
    )(xf, g.reshape(1, D))
    return out.reshape(B, T, D)


def kernel(x, c, ctx, c_ctx, norm1_g, norm2_g, w_ada, b_ada, w_in, na_rpb, ml_conv, ml_gate_b,
           mla_q_norm, mla_w_uq, mla_kv_norm, mla_w_ukv, swa_sink, w_out,
           peer_wq, peer_keys, peer_u, peer_v, final_norm_g):
    T = x.shape[1]
    angs_mla = axial_angles(T, MLA_ROPE)
    angs_swa = axial_angles(T, HEAD_DIM)
    xc = ctx
    for l in range(DEPTH):
        x, xc = hybrid_layer(x, xc, c, c_ctx, l < DEPTH - 1, angs_mla, angs_swa,
                             norm1_g[l], norm2_g[l], w_ada[l], b_ada[l], w_in[l], na_rpb[l],
                             ml_conv[l], ml_gate_b[l], mla_q_norm[l], mla_w_uq[l], mla_kv_norm[l],
                             mla_w_ukv[l], swa_sink[l], w_out[l], peer_wq[l], peer_keys[l],
                             peer_u[l], peer_v[l])
    return final_rmsnorm(x, final_norm_g)
```

```python
import functools

import jax
import jax.numpy as jnp
from jax import lax
import numpy as np
from jax.experimental import pallas as pl
from jax.experimental.pallas import tpu as pltpu

D_MODEL = 1024
BATCH = 2
SEQ = 16384
DEPTH = 2

CTX_LEN = 256
GRID_W = 64
N_MIXERS = 4
MIX_WIDTH = D_MODEL
GROUP_WIDTH = MIX_WIDTH // N_MIXERS
GROUP_HEADS = 4
HEAD_DIM = GROUP_WIDTH // GROUP_HEADS
NA_ROWS = 8
NA_COLS = 16
ML_CHUNK = 64
ML_CONV = 5
MLA_Q_RANK = 256
MLA_KV_RANK = 128
MLA_NOPE = 64
MLA_ROPE = 32
MLA_V = 64
SWA_KV_HEADS = 2
SWA_WINDOW = 128
ATTN_BLOCK = 128
PEER_HEADS = 8
PEER_NKEYS = 128
PEER_EXPERTS = PEER_NKEYS * PEER_NKEYS
PEER_DKEY = 128
PEER_TOPK = 16
PEER_BLOCK = 128
ROPE_BASE = 10000.0
EPS = 1e-6
IN_SIZES = (GROUP_WIDTH, GROUP_WIDTH, GROUP_WIDTH,
            2 * GROUP_WIDTH, GROUP_WIDTH, GROUP_WIDTH, 4 * GROUP_HEADS,
            MLA_Q_RANK, MLA_KV_RANK, MLA_ROPE,
            GROUP_WIDTH, SWA_KV_HEADS * HEAD_DIM, SWA_KV_HEADS * HEAD_DIM)
IN_WIDTH = sum(IN_SIZES)
F32 = jnp.float32


def rmsnorm(x, g):
    xf = x.astype(F32)
    y = xf * lax.rsqrt(jnp.mean(xf * xf, axis=-1, keepdims=True) + EPS) * g.astype(F32)
    return y.astype(x.dtype)


def heads(a, h):
    return a.reshape(a.shape[:-1] + (h, a.shape[-1] // h))


def split_cols(p):
    return jnp.split(p, np.cumsum(IN_SIZES)[:-1].tolist(), axis=-1)


def axial_angles(T, rot_dim):
    t = jnp.arange(T)
    row = (t // GRID_W).astype(F32)
    col = (t % GRID_W).astype(F32)
    half = rot_dim // 2
    inv = 1.0 / (ROPE_BASE ** (jnp.arange(0, half, 2, dtype=F32) / half))
    return row[:, None] * inv, col[:, None] * inv


def rope_1d(x, ang):
    cos = jnp.cos(ang)[None, :, None, :]
    sin = jnp.sin(ang)[None, :, None, :]
    x1, x2 = jnp.split(x.astype(F32), 2, axis=-1)
    return jnp.concatenate([x1 * cos - x2 * sin, x1 * sin + x2 * cos], axis=-1)


def rope_2d(x, angs):
    xr, xc = jnp.split(x, 2, axis=-1)
    return jnp.concatenate([rope_1d(xr, angs[0]), rope_1d(xc, angs[1])], axis=-1).astype(x.dtype)


def ctx_attn(q, k, v, scale, sink=None):
    rep = q.shape[2] // k.shape[2]
    k = jnp.repeat(k, rep, axis=2)
    v = jnp.repeat(v, rep, axis=2)
    s = jnp.einsum('bqhd,bkhd->bhqk', q, k).astype(F32) * scale
    nk = s.shape[-1]
    if sink is not None:
        s = jnp.concatenate([s, jnp.broadcast_to(sink.astype(F32)[None, :, None, None], s.shape[:-1] + (1,))], axis=-1)
    p = jax.nn.softmax(s, axis=-1)[..., :nk].astype(v.dtype)
    out = jnp.einsum('bhqk,bkhd->bqhd', p, v)
    return out.reshape(out.shape[:2] + (-1,))


NT_DIMS = (((1,), (1,)), ((), ()))
NA_SPAN = NA_ROWS * GRID_W


def _head_mask(width):
    rows = lax.broadcasted_iota(jnp.int32, (GROUP_HEADS * width, GROUP_WIDTH), 0) // width
    cols = lax.broadcasted_iota(jnp.int32, (GROUP_HEADS * width, GROUP_WIDTH), 1) // HEAD_DIM
    return (rows == cols).astype(F32)


def _na_kernel(q_ref, k_ref, v_ref, kc_ref, vc_ref, bias_ref, o_ref):
    r = pl.program_id(1)
    rows = pl.num_programs(1)
    rs = jnp.clip(r - NA_ROWS // 2, 0, rows - NA_ROWS)
    start = pl.multiple_of(rs * GRID_W, GRID_W)
    kw = k_ref[0, pl.ds(start, NA_SPAN), :]
    vw = v_ref[0, pl.ds(start, NA_SPAN), :]
    hm = _head_mask(GRID_W)
    q = q_ref[0] * (HEAD_DIM ** -0.5)
    q4 = (jnp.concatenate([q] * GROUP_HEADS, axis=0) * hm).astype(BF16)
    s_loc = lax.dot_general(q4, kw, NT_DIMS, preferred_element_type=F32) + bias_ref[rs - r + NA_ROWS - 1]
    s_ctx = lax.dot_general(q4, kc_ref[0], NT_DIMS, preferred_element_type=F32)
    m = jnp.maximum(jnp.max(s_loc, axis=-1, keepdims=True), jnp.max(s_ctx, axis=-1, keepdims=True))
    p_loc = jnp.exp(s_loc - m)
    p_ctx = jnp.exp(s_ctx - m)
    l = jnp.sum(p_loc, axis=-1, keepdims=True) + jnp.sum(p_ctx, axis=-1, keepdims=True)
    o = (jnp.dot(p_loc.astype(BF16), vw, preferred_element_type=F32)
         + jnp.dot(p_ctx.astype(BF16), vc_ref[0], preferred_element_type=F32)) * (hm / l)
    o_ref[0] = sum(o[h * GRID_W:(h + 1) * GRID_W] for h in range(GROUP_HEADS))


def _na_bias_table(rpb):
    c = np.arange(GRID_W)
    col_start = np.clip(c - NA_COLS // 2, 0, GRID_W - NA_COLS)
    valid = (c[None, :] >= col_start[:, None]) & (c[None, :] < col_start[:, None] + NA_COLS)
    dc = np.clip(c[None, :] - c[:, None] + NA_COLS - 1, 0, 2 * NA_COLS - 2)
    dr = np.arange(NA_ROWS)[:, None] + np.arange(NA_ROWS)[None, :]
    t = rpb.astype(F32)[:, dr][..., dc]
    t = jnp.where(valid[None, None, None], t, -jnp.inf)
    return jnp.transpose(t, (1, 0, 3, 2, 4)).reshape(NA_ROWS, GROUP_HEADS * GRID_W, NA_SPAN)


def neighbourhood_attention(q, k, v, kc, vc, rpb):
    B, T, C = q.shape
    rows = T // GRID_W
    n_ctx = kc.shape[1]
    bias = _na_bias_table(rpb)
    full = lambda n: pl.BlockSpec((1, n, C), lambda b, r: (b, 0, 0))
    return pl.pallas_call(
        _na_kernel,
        grid=(B, rows),
        in_specs=[pl.BlockSpec((1, GRID_W, C), lambda b, r: (b, r, 0)),
                  full(T), full(T), full(n_ctx), full(n_ctx),
                  pl.BlockSpec(bias.shape, lambda b, r: (0, 0, 0))],
        out_specs=pl.BlockSpec((1, GRID_W, C), lambda b, r: (b, r, 0)),
        out_shape=jax.ShapeDtypeStruct((B, T, C), F32),
        compiler_params=pltpu.CompilerParams(vmem_limit_bytes=VMEM_LIMIT_BYTES),
    )(q, k.astype(BF16), v.astype(BF16), kc.astype(BF16), vc.astype(BF16), bias)


def short_conv(a, w):
    T = a.shape[1]
    pad = w.shape[0] // 2
    ap = jnp.pad(a, ((0, 0), (pad, pad), (0, 0)))
    out = ap[:, :T] * w[0]
    for j in range(1, w.shape[0]):
        out = out + ap[:, j:j + T] * w[j]
    return out


ML_CHUNKS_PER_STEP = CTX_LEN // ML_CHUNK


def _bmm(a, b, contract):
    return lax.dot_general(a.astype(BF16), b.astype(BF16), (contract, ((0,), (0,))),
                           preferred_element_type=F32)


def _mlstm_chunk(qt, kt, vt, irow, brow, state, backward):
    L = ML_CHUNK
    C, nrow, m = state
    row = lax.broadcasted_iota(jnp.int32, (1, L, L), 1)
    col = lax.broadcasted_iota(jnp.int32, (1, L, L), 2)
    seen = (row <= col) if backward else (row >= col)
    eye = row == col

    def as_col(r):
        return jnp.sum(jnp.where(eye, r, 0.0), axis=2, keepdims=True)

    blast = brow[:, :, 0:1] if backward else brow[:, :, L - 1:L]
    rrow = brow - irow
    bcol = as_col(brow)
    d_log = jnp.where(seen, bcol - rrow, -jnp.inf)
    inter = bcol + m
    m_t = jnp.maximum(inter, jnp.max(d_log, axis=2, keepdims=True))
    w = jnp.exp(d_log - m_t)
    a = jnp.exp(inter - m_t)
    s = _bmm(qt, kt, ((2,), (2,))) * w
    num = _bmm(s, vt, ((2,), (1,))) + a * _bmm(qt, C, ((2,), (1,)))
    den = jnp.sum(s, axis=2, keepdims=True) + a * jnp.sum(qt * nrow, axis=2, keepdims=True)
    h = num / jnp.maximum(jnp.abs(den), jnp.exp(-m_t))
    g = blast - rrow
    m_new = jnp.maximum(blast + m, jnp.max(g, axis=2, keepdims=True))
    kw = kt * as_col(jnp.exp(g - m_new))
    decay = jnp.exp(blast + m - m_new)
    C = decay * C + _bmm(jnp.swapaxes(kw, 1, 2), vt, ((2,), (1,)))
    nrow = decay * nrow + jnp.sum(kw, axis=1, keepdims=True)
    return h, (C, nrow, m_new)


def _mlstm_kernel(qf_ref, kf_ref, vf_ref, if_ref, bf_ref, qb_ref, kb_ref, vb_ref, ib_ref, bb_ref,
                  hf_ref, hb_ref, c_ref, n_ref, m_ref):
    N, L = qf_ref.shape[0], ML_CHUNK

    @pl.when(pl.program_id(0) == 0)
    def _():
        c_ref[...] = jnp.zeros(c_ref.shape, F32)
        n_ref[...] = jnp.zeros(n_ref.shape, F32)
        m_ref[...] = jnp.zeros(m_ref.shape, F32)

    fwd = (c_ref[:N], n_ref[:N], m_ref[:N])
    bwd = (c_ref[N:], n_ref[N:], m_ref[N:])
    for c in range(ML_CHUNKS_PER_STEP):
        rows = slice(c * L, (c + 1) * L)
        h, fwd = _mlstm_chunk(qf_ref[:, rows, :], kf_ref[:, rows, :], vf_ref[:, rows, :],
                              if_ref[:, 0, c:c + 1, :], bf_ref[:, 0, c:c + 1, :], fwd, False)
        hf_ref[:, rows, :] = h
        cb = ML_CHUNKS_PER_STEP - 1 - c
        rows = slice(cb * L, (cb + 1) * L)
        h, bwd = _mlstm_chunk(qb_ref[:, rows, :], kb_ref[:, rows, :], vb_ref[:, rows, :],
                              ib_ref[:, 0, cb:cb + 1, :], bb_ref[:, 0, cb:cb + 1, :], bwd, True)
        hb_ref[:, rows, :] = h
    for i, ref in enumerate((c_ref, n_ref, m_ref)):
        ref[:N] = fwd[i]
        ref[N:] = bwd[i]


def mlstm_scan(q, k, v, gates_f, gates_b, n_ctx):
    B, T, H, d = q.shape
    CB, L = ML_CHUNKS_PER_STEP, ML_CHUNK
    assert n_ctx == CB * L and T % (CB * L) == 0
    N, steps = B * H, T // (CB * L)
    hm = lambda a: jnp.swapaxes(a, 1, 2).reshape(N, T, d)
    gates = lambda a: jnp.swapaxes(a, 1, 2).reshape(N, steps, CB, L)
    chunked = lambda a: a.reshape(B, T // L, L, H)
    b_f = jnp.cumsum(chunked(gates_f[1]), axis=2).reshape(B, T, H)
    b_b = lax.cumsum(chunked(gates_b[1]), axis=2, reverse=True).reshape(B, T, H)
    back = lambda j: jnp.where(j == 0, 0, steps - j)
    seq_f = pl.BlockSpec((N, CB * L, d), lambda j: (0, j, 0))
    seq_b = pl.BlockSpec((N, CB * L, d), lambda j: (0, back(j), 0))
    gate_f = pl.BlockSpec((N, 1, CB, L), lambda j: (0, j, 0, 0))
    gate_b = pl.BlockSpec((N, 1, CB, L), lambda j: (0, back(j), 0, 0))
    qh, kh, vh = hm(q), hm(k), hm(v)
    hf, hb = pl.pallas_call(
        _mlstm_kernel,
        grid=(steps,),
        in_specs=[seq_f, seq_f, seq_f, gate_f, gate_f, seq_b, seq_b, seq_b, gate_b, gate_b],
        out_specs=[seq_f, seq_b],
        out_shape=[jax.ShapeDtypeStruct((N, T, d), F32)] * 2,
        scratch_shapes=[pltpu.VMEM((2 * N, d, d), F32), pltpu.VMEM((2 * N, 1, d), F32),
                        pltpu.VMEM((2 * N, 1, 1), F32)],
        compiler_params=pltpu.CompilerParams(vmem_limit_bytes=VMEM_LIMIT_BYTES),
    )(qh, kh, vh, gates(gates_f[0]), gates(b_f), qh, kh, vh, gates(gates_b[0]), gates(b_b))
    return jnp.swapaxes((hf + hb).reshape(B, H, T, d), 1, 2)


def mlstm_prep(qk, v, gates, conv_w, gate_b):
    qk = jax.nn.silu(short_conv(qk, conv_w))
    q, k = jnp.split(qk, 2, axis=-1)
    g = (gates + gate_b).astype(F32)
    i_f, f_f, i_b, f_b = jnp.split(g, 4, axis=-1)
    return (heads(q, GROUP_HEADS) * HEAD_DIM ** -0.5, heads(k, GROUP_HEADS), heads(v, GROUP_HEADS),
            (i_f, jax.nn.log_sigmoid(f_f), i_b, jax.nn.log_sigmoid(f_b)))


def mlstm_mixer(lat, ctx, conv_w, gate_b):
    ql, kl, vl, gl = mlstm_prep(lat[0], lat[1], lat[2], conv_w, gate_b)
    qc, kc, vc, gc = mlstm_prep(ctx[0], ctx[1], ctx[2], conv_w, gate_b)
    Tc = qc.shape[1]
    cat = lambda c_, l_: jnp.concatenate([c_, l_], axis=1)
    h = mlstm_scan(cat(qc, ql), cat(kc, kl), cat(vc, vl),
                   (cat(gc[0], gl[0]), cat(gc[1], gl[1])), (cat(gc[2], gl[2]), cat(gc[3], gl[3])), Tc)
    return h[:, Tc:], h[:, :Tc]


def mla_project(cq, ckv, kr, q_norm, w_uq, kv_norm, w_ukv, angs):
    q = heads(rmsnorm(cq, q_norm) @ w_uq, GROUP_HEADS)
    kv = heads(rmsnorm(ckv, kv_norm) @ w_ukv, GROUP_HEADS)
    q_nope, q_rope = q[..., :MLA_NOPE], q[..., MLA_NOPE:]
    k_nope, v = kv[..., :MLA_NOPE], kv[..., MLA_NOPE:]
    k_rope = kr[:, :, None, :]
    if angs is not None:
        q_rope = rope_2d(q_rope, angs)
        k_rope = rope_2d(k_rope, angs)
    k_rope = jnp.broadcast_to(k_rope, k_nope.shape[:-1] + (MLA_ROPE,))
    return (jnp.concatenate([q_nope, q_rope], axis=-1), jnp.concatenate([k_nope, k_rope], axis=-1), v)


LOG2_E = 1.4426950408889634
DENSE_Q_TILE = 1024
DENSE_Q_SUB = 256
DENSE_Q_UNROLL = 4
DENSE_K_TILE_MAX = 8320


def _dense_attn_kernel(q_ref, k_ref, v_ref, o_ref, m_ref, l_ref, acc_ref, *, scale):
    j = pl.program_id(3)

    @pl.when(j == 0)
    def _():
        m_ref[...] = jnp.full(m_ref.shape, -jnp.inf, F32)
        l_ref[...] = jnp.zeros(l_ref.shape, F32)
        acc_ref[...] = jnp.zeros(acc_ref.shape, F32)

    def rows(i, carry):
        for u in range(DENSE_Q_UNROLL):
            r = pl.ds(pl.multiple_of((i * DENSE_Q_UNROLL + u) * DENSE_Q_SUB, DENSE_Q_SUB), DENSE_Q_SUB)
            s = lax.dot_general(q_ref[0, 0, r, :], k_ref[0, 0], NT_DIMS,
                                preferred_element_type=F32) * (scale * LOG2_E)
            m_prev = m_ref[r, :]
            m_new = jnp.maximum(m_prev, jnp.max(s, axis=-1, keepdims=True))
            alpha = jnp.exp2(m_prev - m_new)
            p = jnp.exp2(s - m_new)
            l_ref[r, :] = alpha * l_ref[r, :] + jnp.sum(p, axis=-1, keepdims=True)
            acc_ref[r, :] = alpha * acc_ref[r, :] + jnp.dot(p.astype(BF16), v_ref[0, 0],
                                                            preferred_element_type=F32)
            m_ref[r, :] = m_new
        return carry

    lax.fori_loop(0, q_ref.shape[2] // (DENSE_Q_SUB * DENSE_Q_UNROLL), rows, 0)

    @pl.when(j == pl.num_programs(3) - 1)
    def _():
        o_ref[0, 0] = acc_ref[...] / l_ref[...]


def block_dense_attention(q, k_all, v_all, scale):
    B, T, H, dq = q.shape
    NK, dv = k_all.shape[1], v_all.shape[-1]
    tq = min(DENSE_Q_TILE, T)
    tk = max(t for t in range(LANES, DENSE_K_TILE_MAX + 1, LANES) if NK % t == 0)
    hm = lambda a: jnp.swapaxes(a, 1, 2).astype(BF16)
    out = pl.pallas_call(
        functools.partial(_dense_attn_kernel, scale=scale),
        grid=(B, H, T // tq, NK // tk),
        in_specs=[pl.BlockSpec((1, 1, tq, dq), lambda b, h, i, j: (b, h, i, 0)),
                  pl.BlockSpec((1, 1, tk, dq), lambda b, h, i, j: (b, h, j, 0)),
                  pl.BlockSpec((1, 1, tk, dv), lambda b, h, i, j: (b, h, j, 0))],
        out_specs=pl.BlockSpec((1, 1, tq, dv), lambda b, h, i, j: (b, h, i, 0)),
        out_shape=jax.ShapeDtypeStruct((B, H, T, dv), F32),
        scratch_shapes=[pltpu.VMEM((tq, 1), F32), pltpu.VMEM((tq, 1), F32), pltpu.VMEM((tq, dv), F32)],
        compiler_params=pltpu.CompilerParams(vmem_limit_bytes=VMEM_LIMIT_BYTES),
    )(hm(q), hm(k_all), hm(v_all))
    return jnp.swapaxes(out, 1, 2).reshape(B, T, H * dv)


SWA_SPAN = ATTN_BLOCK + 2 * SWA_WINDOW


def _swa_kernel(q_ref, k_ref, v_ref, kc_ref, vc_ref, sink_ref, o_ref):
    n = pl.program_id(1)
    T = k_ref.shape[1]
    start = pl.multiple_of(jnp.clip(n * ATTN_BLOCK - SWA_WINDOW, 0, T - SWA_SPAN), ATTN_BLOCK)
    kw = k_ref[0, pl.ds(start, SWA_SPAN), :]
    vw = v_ref[0, pl.ds(start, SWA_SPAN), :]
    hm = _head_mask(ATTN_BLOCK)
    q = q_ref[0] * (HEAD_DIM ** -0.5)
    q4 = (jnp.concatenate([q] * GROUP_HEADS, axis=0) * hm).astype(BF16)
    rows = GROUP_HEADS * ATTN_BLOCK
    q_pos = n * ATTN_BLOCK + lax.broadcasted_iota(jnp.int32, (rows, SWA_SPAN), 0) % ATTN_BLOCK
    k_pos = start + lax.broadcasted_iota(jnp.int32, (rows, SWA_SPAN), 1)
    s_loc = lax.dot_general(q4, kw, NT_DIMS, preferred_element_type=F32)
    s_loc = jnp.where(jnp.abs(q_pos - k_pos) <= SWA_WINDOW, s_loc, -jnp.inf)
    s_ctx = lax.dot_general(q4, kc_ref[0], NT_DIMS, preferred_element_type=F32)
    sink = sink_ref[...]
    m = jnp.maximum(jnp.maximum(jnp.max(s_loc, axis=-1, keepdims=True),
                                jnp.max(s_ctx, axis=-1, keepdims=True)), sink)
    p_loc = jnp.exp(s_loc - m)
    p_ctx = jnp.exp(s_ctx - m)
    l = jnp.sum(p_loc, axis=-1, keepdims=True) + jnp.sum(p_ctx, axis=-1, keepdims=True) + jnp.exp(sink - m)
    o = (jnp.dot(p_loc.astype(BF16), vw, preferred_element_type=F32)
         + jnp.dot(p_ctx.astype(BF16), vc_ref[0], preferred_element_type=F32)) * (hm / l)
    o_ref[0] = sum(o[h * ATTN_BLOCK:(h + 1) * ATTN_BLOCK] for h in range(GROUP_HEADS))


def window_attention(q, k, v, kc, vc, sink):
    B, T, H, d = q.shape
    G = H // k.shape[2]
    n_ctx = kc.shape[1]
    C = H * d
    rep = lambda a: jnp.repeat(a, G, axis=2).reshape(a.shape[0], a.shape[1], C).astype(BF16)
    sink_rows = jnp.repeat(sink.astype(F32), ATTN_BLOCK).reshape(H * ATTN_BLOCK, 1)
    full = lambda n: pl.BlockSpec((1, n, C), lambda b, i: (b, 0, 0))
    return pl.pallas_call(
        _swa_kernel,
        grid=(B, T // ATTN_BLOCK),
        in_specs=[pl.BlockSpec((1, ATTN_BLOCK, C), lambda b, i: (b, i, 0)),
                  full(T), full(T), full(n_ctx), full(n_ctx),
                  pl.BlockSpec(sink_rows.shape, lambda b, i: (0, 0))],
        out_specs=pl.BlockSpec((1, ATTN_BLOCK, C), lambda b, i: (b, i, 0)),
        out_shape=jax.ShapeDtypeStruct((B, T, C), F32),
        compiler_params=pltpu.CompilerParams(vmem_limit_bytes=VMEM_LIMIT_BYTES),
    )(q.reshape(B, T, C), rep(k), rep(v), rep(kc), rep(vc), sink_rows)


BF16 = jnp.bfloat16
LANES = 128
SUBLANES = 8
ROW_SEGS = D_MODEL // LANES
ROW_WORDS = ROW_SEGS // 2
PEER_PICKS = PEER_HEADS * PEER_TOPK
PEER_TOPK_TOKENS = 256
PEER_GATHER_TOKENS = 128
PEER_ACT_UNROLL = 8
VMEM_LIMIT_BYTES = 56 * 1024 * 1024


def _split_bf16(x, parts):
    out = []
    for _ in range(parts):
        p = x.astype(BF16)
        out.append(p)
        x = x - p.astype(F32)
    return out


def _topk_rows(s, k):
    n = s.shape[0]
    iota = lax.broadcasted_iota(jnp.int32, s.shape, 0)
    vals, idxs = [], []
    for _ in range(k):
        m = jnp.max(s, axis=0, keepdims=True)
        i = jnp.min(jnp.where(s == m, iota, n), axis=0, keepdims=True)
        vals.append(m)
        idxs.append(i)
        s = jnp.where(iota == i, -jnp.inf, s)
    return jnp.concatenate(vals, axis=0), jnp.concatenate(idxs, axis=0)


def _peer_topk_kernel(x_ref, wq_ref, keys_ref, eidx_ref, gate_ref):
    xb = x_ref[...].astype(BF16)
    q = jnp.dot(xb, wq_ref[...], preferred_element_type=F32)
    nt = (((1,), (1,)), ((), ()))
    sv, si = [], []
    for p in range(2):
        qp = q[:, p * PEER_DKEY:(p + 1) * PEER_DKEY].astype(BF16)
        s = lax.dot_general(keys_ref[0, p], qp, nt, preferred_element_type=F32)
        v_, i_ = _topk_rows(s, PEER_TOPK)
        sv.append(v_)
        si.append(i_)
    cs, ce = [], []
    half = PEER_TOPK // 2
    for a in range(half):
        nb = PEER_TOPK if a == 0 else half
        cs.append(sv[0][a:a + 1] + sv[1][:nb])
        ce.append(si[0][a:a + 1] * PEER_NKEYS + si[1][:nb])
    cs.append(sv[0][half:] + sv[1][0:1])
    ce.append(si[0][half:] * PEER_NKEYS + si[1][0:1])
    cand_s = jnp.concatenate(cs, axis=0)
    cand_e = jnp.concatenate(ce, axis=0)
    fs, fpos = _topk_rows(cand_s, PEER_TOPK)
    iota = lax.broadcasted_iota(jnp.int32, cand_e.shape, 0)
    eidx = [jnp.max(jnp.where(iota == fpos[j:j + 1], cand_e, -1), axis=0, keepdims=True)
            for j in range(PEER_TOPK)]
    ex = jnp.exp(fs - fs[0:1])
    eidx_ref[0] = jnp.concatenate(eidx, axis=0)
    gate_ref[0] = ex / jnp.sum(ex, axis=0, keepdims=True)


def peer_topk(h, wq, sub_keys):
    N, D = h.shape
    T = PEER_TOPK_TOKENS
    wqb = wq.astype(BF16)
    kb = sub_keys.astype(BF16)
    eidx, gate = pl.pallas_call(
        _peer_topk_kernel,
        grid=(N // T, PEER_HEADS),
        in_specs=[pl.BlockSpec((T, D), lambda i, h_: (i, 0)),
                  pl.BlockSpec((D, 2 * PEER_DKEY), lambda i, h_: (0, h_)),
                  pl.BlockSpec((1, 2, PEER_NKEYS, PEER_DKEY), lambda i, h_: (h_, 0, 0, 0))],
        out_specs=[pl.BlockSpec((1, PEER_TOPK, T), lambda i, h_: (h_, 0, i)),
                   pl.BlockSpec((1, PEER_TOPK, T), lambda i, h_: (h_, 0, i))],
        out_shape=[jax.ShapeDtypeStruct((PEER_HEADS, PEER_TOPK, N), jnp.int32),
                   jax.ShapeDtypeStruct((PEER_HEADS, PEER_TOPK, N), F32)],
        compiler_params=pltpu.CompilerParams(vmem_limit_bytes=VMEM_LIMIT_BYTES),
    )(h, wqb, kb)
    return eidx.reshape(PEER_PICKS, N), gate.reshape(PEER_PICKS, N)


def pack_expert_table(tab):
    E = tab.shape[0]
    t = tab.astype(BF16).reshape(E, ROW_WORDS, 2, LANES)
    t = jnp.swapaxes(t, -1, -2)
    return lax.bitcast_convert_type(t, jnp.uint32).reshape(E * ROW_WORDS, LANES)


def _stage_rows(idx_ref, tab_ref, stage_ref, t):
    for k in range(PEER_PICKS):
        off = pl.multiple_of(idx_ref[t, k], ROW_WORDS)
        stage_ref[k * ROW_WORDS:(k + 1) * ROW_WORDS, :] = tab_ref[pl.ds(off, ROW_WORDS), :]
    return pltpu.bitcast(stage_ref[...], BF16)


def _peer_act_kernel(idx_ref, x_ref, gate_ref, tab_ref, seg_mask_ref, group_ref, w_ref,
                     stage_ref, rows_ref):
    T = x_ref.shape[0]
    U = stage_ref.shape[0]

    def tokens(g, carry):
        for j in range(U):
            t = g * U + j
            sb = _stage_rows(idx_ref, tab_ref, stage_ref.at[j], t)
            xs = jnp.concatenate(_split_bf16(x_ref[t], 2), axis=0)
            r = lax.dot_general(xs, sb, NT_DIMS, preferred_element_type=F32)
            r = r * seg_mask_ref[...]
            rows_ref[t] = r[:SUBLANES] + r[SUBLANES:]
        return carry

    lax.fori_loop(0, T // U, tokens, 0)
    rows = rows_ref[...].reshape(T * SUBLANES, PEER_PICKS * ROW_SEGS)
    part = jnp.zeros((T * SUBLANES, PEER_PICKS), F32)
    for piece in _split_bf16(rows, 3):
        part = part + jnp.dot(piece, group_ref[...], preferred_element_type=F32)
    act = jnp.sum(part.reshape(T, SUBLANES, PEER_PICKS), axis=1)
    w_ref[...] = gate_ref[...] * (0.5 * act * (1.0 + lax.erf(act * (2.0 ** -0.5))))


def _peer_out_kernel(idx_ref, w_ref, x_ref, g_ref, tab_ref, expand_ref, seg_mask_ref, f_ref, stage_ref):
    T = w_ref.shape[0]
    U = stage_ref.shape[0]

    def tokens(g, carry):
        w8 = w_ref[pl.ds(pl.multiple_of(g * U, U), U), :]
        hi, lo = _split_bf16(w8, 2)
        lhs = jnp.concatenate([jnp.broadcast_to(p[j:j + 1], (SUBLANES, PEER_PICKS))
                               for j in range(U) for p in (hi, lo)], axis=0)
        wrep = jnp.dot(lhs, expand_ref[...], preferred_element_type=F32)
        for j in range(U):
            t = g * U + j
            sb = _stage_rows(idx_ref, tab_ref, stage_ref.at[j], t)
            wsel = (wrep[j * 2 * SUBLANES:(j + 1) * 2 * SUBLANES] * seg_mask_ref[...]).astype(BF16)
            o = jnp.dot(wsel, sb, preferred_element_type=F32)
            f_ref[t] = x_ref[t] + g_ref[0] * (o[:SUBLANES] + o[SUBLANES:])
        return carry

    lax.fori_loop(0, T // U, tokens, 0)


def _peer_constants():
    cols = np.arange(PEER_PICKS * ROW_SEGS)
    seg_mask = (cols[None, :] % ROW_SEGS == np.arange(2 * SUBLANES)[:, None] % SUBLANES)
    group = (cols[:, None] // ROW_SEGS == np.arange(PEER_PICKS)[None, :])
    return (jnp.asarray(seg_mask, F32), jnp.asarray(group, BF16), jnp.asarray(group.T, BF16))


def peer_ffn(h, x, gate2, group_tokens, wq, sub_keys, u_packed, v_packed):
    N, D = h.shape
    T = PEER_GATHER_TOKENS
    eidx, gate = peer_topk(h, wq, sub_keys)
    seg_mask, group, expand = _peer_constants()
    rows3 = lambda a: a.reshape(a.shape[0], ROW_SEGS, LANES)
    offs = eidx.T * ROW_WORDS
    idx_spec = pl.BlockSpec((T, PEER_PICKS), lambda i: (i, 0), memory_space=pltpu.SMEM)
    tab_spec = pl.BlockSpec(u_packed.shape, lambda i: (0, 0), pipeline_mode=pl.Buffered(1))
    tok_spec = pl.BlockSpec((T, ROW_SEGS, LANES), lambda i: (i, 0, 0))
    const = lambda shape: pl.BlockSpec(shape, lambda i: (0, 0))
    params = pltpu.CompilerParams(vmem_limit_bytes=VMEM_LIMIT_BYTES)
    w = pl.pallas_call(
        _peer_act_kernel,
        grid=(N // T,),
        in_specs=[idx_spec, tok_spec,
                  pl.BlockSpec((T, PEER_PICKS), lambda i: (i, 0)),
                  tab_spec, const(seg_mask.shape), const(group.shape)],
        out_specs=pl.BlockSpec((T, PEER_PICKS), lambda i: (i, 0)),
        out_shape=jax.ShapeDtypeStruct((N, PEER_PICKS), F32),
        scratch_shapes=[pltpu.VMEM((PEER_ACT_UNROLL, PEER_PICKS * ROW_WORDS, LANES), jnp.uint32),
                        pltpu.VMEM((T, SUBLANES, PEER_PICKS * ROW_SEGS), F32)],
        compiler_params=params,
    )(offs, rows3(h), gate.T, u_packed, seg_mask, group)
    out = pl.pallas_call(
        _peer_out_kernel,
        grid=(N // T,),
        in_specs=[idx_spec,
                  pl.BlockSpec((T, PEER_PICKS), lambda i: (i, 0)),
                  tok_spec,
                  pl.BlockSpec((1, ROW_SEGS, LANES), lambda i: (i // (group_tokens // T), 0, 0)),
                  tab_spec, const(expand.shape), const(seg_mask.shape)],
        out_specs=tok_spec,
        out_shape=jax.ShapeDtypeStruct((N, ROW_SEGS, LANES), F32),
        scratch_shapes=[pltpu.VMEM((SUBLANES, PEER_PICKS * ROW_WORDS, LANES), jnp.uint32)],
        compiler_params=params,
    )(offs, w, rows3(x), rows3(gate2), v_packed, expand, seg_mask)
    return out.reshape(N, D)


PROJ_TOKENS = 512
MOD_ROWS = SUBLANES
IN_ALIGNED = tuple(i for i, s_ in enumerate(IN_SIZES) if s_ % LANES == 0)
IN_SMALL = tuple(i for i, s_ in enumerate(IN_SIZES) if s_ % LANES)


def _rms_modulate(x, gain, scale1p, shift):
    r = lax.rsqrt(jnp.mean(x * x, axis=-1, keepdims=True) + EPS)
    return (x * r * gain) * scale1p + shift


def _in_proj_kernel(x_ref, mod_ref, w_ref, *out_refs):
    mod = mod_ref[0]
    h = _rms_modulate(x_ref[...], mod[0:1], mod[1:2], mod[2:3])
    y = jnp.dot(h.astype(BF16), w_ref[...], preferred_element_type=F32)
    off = 0
    for o_ref in out_refs:
        o_ref[...] = y[:, off:off + o_ref.shape[1]].astype(o_ref.dtype)
        off += o_ref.shape[1]


def in_projection(x, mod, w_in, group_tokens):
    N, D = x.shape
    T = min(PROJ_TOKENS, group_tokens)
    starts = np.cumsum((0,) + IN_SIZES)
    cols = np.concatenate([np.arange(starts[i], starts[i + 1]) for i in IN_ALIGNED + IN_SMALL])
    small = sum(IN_SIZES[i] for i in IN_SMALL)
    wp = jnp.pad(w_in[:, cols], ((0, 0), (0, -small % LANES))).astype(BF16)
    widths = [IN_SIZES[i] for i in IN_ALIGNED] + [small + (-small % LANES)]
    outs = pl.pallas_call(
        _in_proj_kernel,
        grid=(N // T,),
        in_specs=[pl.BlockSpec((T, D), lambda i: (i, 0)),
                  pl.BlockSpec((1, MOD_ROWS, D), lambda i: (i // (group_tokens // T), 0, 0)),
                  pl.BlockSpec(wp.shape, lambda i: (0, 0))],
        out_specs=[pl.BlockSpec((T, w_), lambda i: (i, 0)) for w_ in widths],
        out_shape=[jax.ShapeDtypeStruct((N, w_), F32) for w_ in widths],
        compiler_params=pltpu.CompilerParams(vmem_limit_bytes=VMEM_LIMIT_BYTES),
    )(x, mod, wp)
    groups = dict(zip(IN_ALIGNED, outs[:-1]))
    off = 0
    for i in IN_SMALL:
        groups[i] = outs[-1][:, off:off + IN_SIZES[i]]
        off += IN_SIZES[i]
    return [groups[i] for i in range(len(IN_SIZES))]


def _out_proj_kernel(ya_ref, hl_ref, mo_ref, yc_ref, yd_ref, x_ref, mod_ref, w_ref, xo_ref, h2_ref):
    yb = hl_ref[...] * jax.nn.sigmoid(mo_ref[...])
    y = jnp.concatenate([ya_ref[...], yb, yc_ref[...], yd_ref[...]], axis=-1).astype(BF16)
    mod = mod_ref[0]
    xn = x_ref[...] + mod[0:1] * jnp.dot(y, w_ref[...], preferred_element_type=F32)
    xo_ref[...] = xn
    h2_ref[...] = _rms_modulate(xn, mod[1:2], mod[2:3], mod[3:4])


def out_projection(ya, hl, mo, yc, yd, x, mod, w_out, group_tokens):
    N, D = x.shape
    T = min(PROJ_TOKENS, group_tokens)
    part = pl.BlockSpec((T, GROUP_WIDTH), lambda i: (i, 0))
    tok = pl.BlockSpec((T, D), lambda i: (i, 0))
    return pl.pallas_call(
        _out_proj_kernel,
        grid=(N // T,),
        in_specs=[part, part, part, part, part, tok,
                  pl.BlockSpec((1, MOD_ROWS, D), lambda i: (i // (group_tokens // T), 0, 0)),
                  pl.BlockSpec(w_out.shape, lambda i: (0, 0))],
        out_specs=[tok, tok],
        out_shape=[jax.ShapeDtypeStruct((N, D), F32)] * 2,
        compiler_params=pltpu.CompilerParams(vmem_limit_bytes=VMEM_LIMIT_BYTES),
    )(ya, hl, mo, yc, yd, x, mod, w_out.astype(BF16))


def _mod_rows(*rows):
    m = jnp.stack([jnp.broadcast_to(r, rows[-1].shape) for r in rows], axis=1)
    return jnp.pad(m, ((0, 0), (0, MOD_ROWS - len(rows)), (0, 0)))


def hybrid_layer(x, xc, c, c_ctx, need_ctx, angs_mla, angs_swa,
                 norm1_g, norm2_g, w_ada, b_ada, w_in, na_rpb, ml_conv, ml_gate_b,
                 mla_q_norm, mla_w_uq, mla_kv_norm, mla_w_ukv, swa_sink, w_out,
                 peer_wq, peer_keys, peer_u, peer_v):
    B, T, D = x.shape
    Tc = xc.shape[1]
    H = GROUP_HEADS
    flat = lambda a: a.reshape(-1, a.shape[-1])
    sh1, sc1, g1, sh2, sc2, g2 = jnp.split(jax.nn.silu(c) @ w_ada + b_ada, 6, axis=-1)
    sh1c, sc1c, g1c, sh2c, sc2c, g2c = jnp.split((jax.nn.silu(c_ctx) @ w_ada + b_ada)[None], 6, axis=-1)
    lat = in_projection(flat(x), _mod_rows(norm1_g, 1.0 + sc1, sh1), w_in, T)
    cx = in_projection(flat(xc), _mod_rows(norm1_g, 1.0 + sc1c, sh1c), w_in, B * Tc)
    (na_q, na_k, na_v, ml_qk, ml_v, ml_o, ml_g,
     mla_cq, mla_ckv, mla_kr, sw_q, sw_k, sw_v) = [a.reshape(B, T, -1) for a in lat]
    (na_qc, na_kc, na_vc, ml_qkc, ml_vc, ml_oc, ml_gc,
     mla_cqc, mla_ckvc, mla_krc, sw_qc, sw_kc, sw_vc) = [a.reshape(B, Tc, -1) for a in cx]
    attn_scale = HEAD_DIM ** -0.5
    mla_scale = (MLA_NOPE + MLA_ROPE) ** -0.5
    kc_a, vc_a = heads(na_kc, H), heads(na_vc, H)
    y_a = neighbourhood_attention(na_q, na_k, na_v, na_kc, na_vc, na_rpb)
    h_lat, h_ctx = mlstm_mixer((ml_qk, ml_v, ml_g), (ml_qkc, ml_vc, ml_gc), ml_conv, ml_gate_b)
    q_m, k_m, v_m = mla_project(mla_cq, mla_ckv, mla_kr, mla_q_norm, mla_w_uq, mla_kv_norm, mla_w_ukv, angs_mla)
    qc_m, kc_m, vc_m = mla_project(mla_cqc, mla_ckvc, mla_krc, mla_q_norm, mla_w_uq, mla_kv_norm, mla_w_ukv, None)
    y_c = block_dense_attention(q_m, jnp.concatenate([kc_m, k_m], axis=1), jnp.concatenate([vc_m, v_m], axis=1), mla_scale)
    kc_d, vc_d = heads(sw_kc, SWA_KV_HEADS), heads(sw_vc, SWA_KV_HEADS)
    y_d = window_attention(rope_2d(heads(sw_q, H), angs_swa), rope_2d(heads(sw_k, SWA_KV_HEADS), angs_swa),
                           heads(sw_v, SWA_KV_HEADS), kc_d, vc_d, swa_sink)
    x2, h2 = out_projection(flat(y_a), h_lat.reshape(B * T, GROUP_WIDTH), flat(ml_o), flat(y_c), flat(y_d),
                            flat(x), _mod_rows(g1, norm2_g, 1.0 + sc2, sh2), w_out, T)
    u_packed, v_packed = pack_expert_table(peer_u), pack_expert_table(peer_v)
    x = peer_ffn(h2, x2, g2, T, peer_wq, peer_keys, u_packed, v_packed).reshape(B, T, D)
    if not need_ctx:
        return x, None
    xc2, h2c = out_projection(flat(ctx_attn(heads(na_qc, H), kc_a, vc_a, attn_scale)),
                              h_ctx.reshape(B * Tc, GROUP_WIDTH), flat(ml_oc),
                              flat(ctx_attn(qc_m, kc_m, vc_m, mla_scale)),
                              flat(ctx_attn(heads(sw_qc, H), kc_d, vc_d, attn_scale, swa_sink)),
                              flat(xc), _mod_rows(g1c, norm2_g, 1.0 + sc2c, sh2c), w_out, B * Tc)
    xc = peer_ffn(h2c, xc2, g2c, B * Tc, peer_wq, peer_keys, u_packed, v_packed).reshape(B, Tc, D)
    return x, xc


def _final_rmsnorm_kernel(x_ref, g_ref, o_ref):
    x = x_ref[...]
    o_ref[...] = x * lax.rsqrt(jnp.mean(x * x, axis=-1, keepdims=True) + EPS) * g_ref[...]


def final_rmsnorm(x, g):
    B, T, D = x.shape
    rows = 1024
    xf = x.reshape(B * T, D)
    out = pl.pallas_call(
        _final_rmsnorm_kernel,
        grid=(B * T // rows,),
        in_specs=[pl.BlockSpec((rows, D), lambda i: (i, 0)), pl.BlockSpec((1, D), lambda i: (0, 0))],
        out_specs=pl.BlockSpec((rows, D), lambda i: (i, 0)),
        out_shape=jax.ShapeDtypeStruct((B * T, D), x.dtype),
    )(xf, g.reshape(1, D))
    return out.reshape(B, T, D)


def kernel(x, c, ctx, c_ctx, norm1_g, norm2_g, w_ada, b_ada, w_in, na_rpb, ml_conv, ml_gate_b,
           mla_q_norm, mla_w_uq, mla_kv_norm, mla_w_ukv, swa_sink, w_out,
           peer_wq, peer_keys, peer_u, peer_v, final_norm_g):
    T = x.shape[1]
    angs_mla = axial_angles(T, MLA_ROPE)
    angs_swa = axial_angles(T, HEAD_DIM)
    xc = ctx
    for l in range(DEPTH):
        x, xc = hybrid_layer(x, xc, c, c_ctx, l < DEPTH - 1, angs_mla, angs_swa,
                             norm1_g[l], norm2_g[l], w_ada[l], b_ada[l], w_in[l], na_rpb[l],
                             ml_conv[l], ml_gate_b[l], mla_q_norm[l], mla_w_uq[l], mla_kv_norm[l],
                             mla_w_ukv[l], swa_sink[l], w_out[l], peer_wq[l], peer_keys[l],
                             peer_u[l], peer_v[l])
    return final_rmsnorm(x, final_norm_g)
```

```python
import functools

import jax
import jax.numpy as jnp
from jax import lax
import numpy as np
from jax.experimental import pallas as pl
from jax.experimental.pallas import tpu as pltpu

D_MODEL = 1024
BATCH = 2
SEQ = 16384
DEPTH = 2

CTX_LEN = 256
GRID_W = 64
N_MIXERS = 4
MIX_WIDTH = D_MODEL
GROUP_WIDTH = MIX_WIDTH // N_MIXERS
GROUP_HEADS = 4
HEAD_DIM = GROUP_WIDTH // GROUP_HEADS
NA_ROWS = 8
NA_COLS = 16
ML_CHUNK = 64
ML_CONV = 5
MLA_Q_RANK = 256
MLA_KV_RANK = 128
MLA_NOPE = 64
MLA_ROPE = 32
MLA_V = 64
SWA_KV_HEADS = 2
SWA_WINDOW = 128
ATTN_BLOCK = 128
PEER_HEADS = 8
PEER_NKEYS = 128
PEER_EXPERTS = PEER_NKEYS * PEER_NKEYS
PEER_DKEY = 128
PEER_TOPK = 16
PEER_BLOCK = 128
ROPE_BASE = 10000.0
EPS = 1e-6
IN_SIZES = (GROUP_WIDTH, GROUP_WIDTH, GROUP_WIDTH,
            2 * GROUP_WIDTH, GROUP_WIDTH, GROUP_WIDTH, 4 * GROUP_HEADS,
            MLA_Q_RANK, MLA_KV_RANK, MLA_ROPE,
            GROUP_WIDTH, SWA_KV_HEADS * HEAD_DIM, SWA_KV_HEADS * HEAD_DIM)
IN_WIDTH = sum(IN_SIZES)
F32 = jnp.float32


def rmsnorm(x, g):
    xf = x.astype(F32)
    y = xf * lax.rsqrt(jnp.mean(xf * xf, axis=-1, keepdims=True) + EPS) * g.astype(F32)
    return y.astype(x.dtype)


def heads(a, h):
    return a.reshape(a.shape[:-1] + (h, a.shape[-1] // h))


def split_cols(p):
    return jnp.split(p, np.cumsum(IN_SIZES)[:-1].tolist(), axis=-1)


def axial_angles(T, rot_dim):
    t = jnp.arange(T)
    row = (t // GRID_W).astype(F32)
    col = (t % GRID_W).astype(F32)
    half = rot_dim // 2
    inv = 1.0 / (ROPE_BASE ** (jnp.arange(0, half, 2, dtype=F32) / half))
    return row[:, None] * inv, col[:, None] * inv


def rope_1d(x, ang):
    cos = jnp.cos(ang)[None, :, None, :]
    sin = jnp.sin(ang)[None, :, None, :]
    x1, x2 = jnp.split(x.astype(F32), 2, axis=-1)
    return jnp.concatenate([x1 * cos - x2 * sin, x1 * sin + x2 * cos], axis=-1)


def rope_2d(x, angs):
    xr, xc = jnp.split(x, 2, axis=-1)
    return jnp.concatenate([rope_1d(xr, angs[0]), rope_1d(xc, angs[1])], axis=-1).astype(x.dtype)


def ctx_attn(q, k, v, scale, sink=None):
    rep = q.shape[2] // k.shape[2]
    k = jnp.repeat(k, rep, axis=2)
    v = jnp.repeat(v, rep, axis=2)
    s = jnp.einsum('bqhd,bkhd->bhqk', q, k).astype(F32) * scale
    nk = s.shape[-1]
    if sink is not None:
        s = jnp.concatenate([s, jnp.broadcast_to(sink.astype(F32)[None, :, None, None], s.shape[:-1] + (1,))], axis=-1)
    p = jax.nn.softmax(s, axis=-1)[..., :nk].astype(v.dtype)
    out = jnp.einsum('bhqk,bkhd->bqhd', p, v)
    return out.reshape(out.shape[:2] + (-1,))


NT_DIMS = (((1,), (1,)), ((), ()))
NA_SPAN = NA_ROWS * GRID_W


def _head_mask(width):
    rows = lax.broadcasted_iota(jnp.int32, (GROUP_HEADS * width, GROUP_WIDTH), 0) // width
    cols = lax.broadcasted_iota(jnp.int32, (GROUP_HEADS * width, GROUP_WIDTH), 1) // HEAD_DIM
    return (rows == cols).astype(F32)


def _na_kernel(q_ref, k_ref, v_ref, kc_ref, vc_ref, bias_ref, o_ref):
    r = pl.program_id(1)
    rows = pl.num_programs(1)
    rs = jnp.clip(r - NA_ROWS // 2, 0, rows - NA_ROWS)
    start = pl.multiple_of(rs * GRID_W, GRID_W)
    kw = k_ref[0, pl.ds(start, NA_SPAN), :]
    vw = v_ref[0, pl.ds(start, NA_SPAN), :]
    hm = _head_mask(GRID_W)
    q = q_ref[0] * (HEAD_DIM ** -0.5)
    q4 = (jnp.concatenate([q] * GROUP_HEADS, axis=0) * hm).astype(BF16)
    s_loc = lax.dot_general(q4, kw, NT_DIMS, preferred_element_type=F32) + bias_ref[rs - r + NA_ROWS - 1]
    s_ctx = lax.dot_general(q4, kc_ref[0], NT_DIMS, preferred_element_type=F32)
    m = jnp.maximum(jnp.max(s_loc, axis=-1, keepdims=True), jnp.max(s_ctx, axis=-1, keepdims=True))
    p_loc = jnp.exp(s_loc - m)
    p_ctx = jnp.exp(s_ctx - m)
    l = jnp.sum(p_loc, axis=-1, keepdims=True) + jnp.sum(p_ctx, axis=-1, keepdims=True)
    o = (jnp.dot(p_loc.astype(BF16), vw, preferred_element_type=F32)
         + jnp.dot(p_ctx.astype(BF16), vc_ref[0], preferred_element_type=F32)) * (hm / l)
    o_ref[0] = sum(o[h * GRID_W:(h + 1) * GRID_W] for h in range(GROUP_HEADS))


def _na_bias_table(rpb):
    c = np.arange(GRID_W)
    col_start = np.clip(c - NA_COLS // 2, 0, GRID_W - NA_COLS)
    valid = (c[None, :] >= col_start[:, None]) & (c[None, :] < col_start[:, None] + NA_COLS)
    dc = np.clip(c[None, :] - c[:, None] + NA_COLS - 1, 0, 2 * NA_COLS - 2)
    dr = np.arange(NA_ROWS)[:, None] + np.arange(NA_ROWS)[None, :]
    t = rpb.astype(F32)[:, dr][..., dc]
    t = jnp.where(valid[None, None, None], t, -jnp.inf)
    return jnp.transpose(t, (1, 0, 3, 2, 4)).reshape(NA_ROWS, GROUP_HEADS * GRID_W, NA_SPAN)


def neighbourhood_attention(q, k, v, kc, vc, rpb):
    B, T, C = q.shape
    rows = T // GRID_W
    n_ctx = kc.shape[1]
    bias = _na_bias_table(rpb)
    full = lambda n: pl.BlockSpec((1, n, C), lambda b, r: (b, 0, 0))
    return pl.pallas_call(
        _na_kernel,
        grid=(B, rows),
        in_specs=[pl.BlockSpec((1, GRID_W, C), lambda b, r: (b, r, 0)),
                  full(T), full(T), full(n_ctx), full(n_ctx),
                  pl.BlockSpec(bias.shape, lambda b, r: (0, 0, 0))],
        out_specs=pl.BlockSpec((1, GRID_W, C), lambda b, r: (b, r, 0)),
        out_shape=jax.ShapeDtypeStruct((B, T, C), F32),
        compiler_params=pltpu.CompilerParams(vmem_limit_bytes=VMEM_LIMIT_BYTES),
    )(q, k.astype(BF16), v.astype(BF16), kc.astype(BF16), vc.astype(BF16), bias)


def short_conv(a, w):
    T = a.shape[1]
    pad = w.shape[0] // 2
    ap = jnp.pad(a, ((0, 0), (pad, pad), (0, 0)))
    out = ap[:, :T] * w[0]
    for j in range(1, w.shape[0]):
        out = out + ap[:, j:j + T] * w[j]
    return out


ML_CHUNKS_PER_STEP = CTX_LEN // ML_CHUNK


def _bmm(a, b, contract):
    return lax.dot_general(a.astype(BF16), b.astype(BF16), (contract, ((0,), (0,))),
                           preferred_element_type=F32)


def _mlstm_chunk(qt, kt, vt, irow, brow, state, backward):
    L = ML_CHUNK
    C, nrow, m = state
    row = lax.broadcasted_iota(jnp.int32, (1, L, L), 1)
    col = lax.broadcasted_iota(jnp.int32, (1, L, L), 2)
    seen = (row <= col) if backward else (row >= col)
    eye = row == col

    def as_col(r):
        return jnp.sum(jnp.where(eye, r, 0.0), axis=2, keepdims=True)

    blast = brow[:, :, 0:1] if backward else brow[:, :, L - 1:L]
    rrow = brow - irow
    bcol = as_col(brow)
    d_log = jnp.where(seen, bcol - rrow, -jnp.inf)
    inter = bcol + m
    m_t = jnp.maximum(inter, jnp.max(d_log, axis=2, keepdims=True))
    w = jnp.exp(d_log - m_t)
    a = jnp.exp(inter - m_t)
    s = _bmm(qt, kt, ((2,), (2,))) * w
    num = _bmm(s, vt, ((2,), (1,))) + a * _bmm(qt, C, ((2,), (1,)))
    den = jnp.sum(s, axis=2, keepdims=True) + a * jnp.sum(qt * nrow, axis=2, keepdims=True)
    h = num / jnp.maximum(jnp.abs(den), jnp.exp(-m_t))
    g = blast - rrow
    m_new = jnp.maximum(blast + m, jnp.max(g, axis=2, keepdims=True))
    kw = kt * as_col(jnp.exp(g - m_new))
    decay = jnp.exp(blast + m - m_new)
    C = decay * C + _bmm(jnp.swapaxes(kw, 1, 2), vt, ((2,), (1,)))
    nrow = decay * nrow + jnp.sum(kw, axis=1, keepdims=True)
    return h, (C, nrow, m_new)


def _mlstm_kernel(qf_ref, kf_ref, vf_ref, if_ref, bf_ref, qb_ref, kb_ref, vb_ref, ib_ref, bb_ref,
                  hf_ref, hb_ref, c_ref, n_ref, m_ref):
    N, L = qf_ref.shape[0], ML_CHUNK

    @pl.when(pl.program_id(0) == 0)
    def _():
        c_ref[...] = jnp.zeros(c_ref.shape, F32)
        n_ref[...] = jnp.zeros(n_ref.shape, F32)
        m_ref[...] = jnp.zeros(m_ref.shape, F32)

    fwd = (c_ref[:N], n_ref[:N], m_ref[:N])
    bwd = (c_ref[N:], n_ref[N:], m_ref[N:])
    for c in range(ML_CHUNKS_PER_STEP):
        rows = slice(c * L, (c + 1) * L)
        h, fwd = _mlstm_chunk(qf_ref[:, rows, :], kf_ref[:, rows, :], vf_ref[:, rows, :],
                              if_ref[:, 0, c:c + 1, :], bf_ref[:, 0, c:c + 1, :], fwd, False)
        hf_ref[:, rows, :] = h
        cb = ML_CHUNKS_PER_STEP - 1 - c
        rows = slice(cb * L, (cb + 1) * L)
        h, bwd = _mlstm_chunk(qb_ref[:, rows, :], kb_ref[:, rows, :], vb_ref[:, rows, :],
                              ib_ref[:, 0, cb:cb + 1, :], bb_ref[:, 0, cb:cb + 1, :], bwd, True)
        hb_ref[:, rows, :] = h
    for i, ref in enumerate((c_ref, n_ref, m_ref)):
        ref[:N] = fwd[i]
        ref[N:] = bwd[i]


def mlstm_scan(q, k, v, gates_f, gates_b, n_ctx):
    B, T, H, d = q.shape
    CB, L = ML_CHUNKS_PER_STEP, ML_CHUNK
    assert n_ctx == CB * L and T % (CB * L) == 0
    N, steps = B * H, T // (CB * L)
    hm = lambda a: jnp.swapaxes(a, 1, 2).reshape(N, T, d)
    gates = lambda a: jnp.swapaxes(a, 1, 2).reshape(N, steps, CB, L)
    chunked = lambda a: a.reshape(B, T // L, L, H)
    b_f = jnp.cumsum(chunked(gates_f[1]), axis=2).reshape(B, T, H)
    b_b = lax.cumsum(chunked(gates_b[1]), axis=2, reverse=True).reshape(B, T, H)
    back = lambda j: jnp.where(j == 0, 0, steps - j)
    seq_f = pl.BlockSpec((N, CB * L, d), lambda j: (0, j, 0))
    seq_b = pl.BlockSpec((N, CB * L, d), lambda j: (0, back(j), 0))
    gate_f = pl.BlockSpec((N, 1, CB, L), lambda j: (0, j, 0, 0))
    gate_b = pl.BlockSpec((N, 1, CB, L), lambda j: (0, back(j), 0, 0))
    qh, kh, vh = hm(q), hm(k), hm(v)
    hf, hb = pl.pallas_call(
        _mlstm_kernel,
        grid=(steps,),
        in_specs=[seq_f, seq_f, seq_f, gate_f, gate_f, seq_b, seq_b, seq_b, gate_b, gate_b],
        out_specs=[seq_f, seq_b],
        out_shape=[jax.ShapeDtypeStruct((N, T, d), F32)] * 2,
        scratch_shapes=[pltpu.VMEM((2 * N, d, d), F32), pltpu.VMEM((2 * N, 1, d), F32),
                        pltpu.VMEM((2 * N, 1, 1), F32)],
        compiler_params=pltpu.CompilerParams(vmem_limit_bytes=VMEM_LIMIT_BYTES),
    )(qh, kh, vh, gates(gates_f[0]), gates(b_f), qh, kh, vh, gates(gates_b[0]), gates(b_b))
    return jnp.swapaxes((hf + hb).reshape(B, H, T, d), 1, 2)


def mlstm_prep(qk, v, gates, conv_w, gate_b):
    qk = jax.nn.silu(short_conv(qk, conv_w))
    q, k = jnp.split(qk, 2, axis=-1)
    g = (gates + gate_b).astype(F32)
    i_f, f_f, i_b, f_b = jnp.split(g, 4, axis=-1)
    return (heads(q, GROUP_HEADS) * HEAD_DIM ** -0.5, heads(k, GROUP_HEADS), heads(v, GROUP_HEADS),
            (i_f, jax.nn.log_sigmoid(f_f), i_b, jax.nn.log_sigmoid(f_b)))


def mlstm_mixer(lat, ctx, conv_w, gate_b):
    ql, kl, vl, gl = mlstm_prep(lat[0], lat[1], lat[2], conv_w, gate_b)
    qc, kc, vc, gc = mlstm_prep(ctx[0], ctx[1], ctx[2], conv_w, gate_b)
    Tc = qc.shape[1]
    cat = lambda c_, l_: jnp.concatenate([c_, l_], axis=1)
    h = mlstm_scan(cat(qc, ql), cat(kc, kl), cat(vc, vl),
                   (cat(gc[0], gl[0]), cat(gc[1], gl[1])), (cat(gc[2], gl[2]), cat(gc[3], gl[3])), Tc)
    return h[:, Tc:], h[:, :Tc]


def mla_project(cq, ckv, kr, q_norm, w_uq, kv_norm, w_ukv, angs):
    q = heads(rmsnorm(cq, q_norm) @ w_uq, GROUP_HEADS)
    kv = heads(rmsnorm(ckv, kv_norm) @ w_ukv, GROUP_HEADS)
    q_nope, q_rope = q[..., :MLA_NOPE], q[..., MLA_NOPE:]
    k_nope, v = kv[..., :MLA_NOPE], kv[..., MLA_NOPE:]
    k_rope = kr[:, :, None, :]
    if angs is not None:
        q_rope = rope_2d(q_rope, angs)
        k_rope = rope_2d(k_rope, angs)
    k_rope = jnp.broadcast_to(k_rope, k_nope.shape[:-1] + (MLA_ROPE,))
    return (jnp.concatenate([q_nope, q_rope], axis=-1), jnp.concatenate([k_nope, k_rope], axis=-1), v)


LOG2_E = 1.4426950408889634
DENSE_Q_TILE = 1024
DENSE_Q_SUB = 256
DENSE_Q_UNROLL = 4
DENSE_K_TILE_MAX = 8320


def _dense_attn_kernel(q_ref, k_ref, v_ref, o_ref, m_ref, l_ref, acc_ref, *, scale, dv):
    h, j = pl.program_id(2), pl.program_id(3)

    @pl.when(j == 0)
    def _():
        m_ref[...] = jnp.full(m_ref.shape, -jnp.inf, F32)
        l_ref[...] = jnp.zeros(l_ref.shape, F32)
        acc_ref[...] = jnp.zeros(acc_ref.shape, F32)

    def rows(i, carry):
        for u in range(DENSE_Q_UNROLL):
            r = pl.ds(pl.multiple_of((i * DENSE_Q_UNROLL + u) * DENSE_Q_SUB, DENSE_Q_SUB), DENSE_Q_SUB)
            s = lax.dot_general(q_ref[0, r, :], k_ref[0], NT_DIMS,
                                preferred_element_type=F32) * (scale * LOG2_E)
            m_prev = m_ref[r, :]
            m_new = jnp.maximum(m_prev, jnp.max(s, axis=-1, keepdims=True))
            alpha = jnp.exp2(m_prev - m_new)
            p = jnp.exp2(s - m_new)
            l_ref[r, :] = alpha * l_ref[r, :] + jnp.sum(p, axis=-1, keepdims=True)
            acc_ref[r, :] = alpha * acc_ref[r, :] + jnp.dot(p.astype(BF16), v_ref[0],
                                                            preferred_element_type=F32)
            m_ref[r, :] = m_new
        return carry

    lax.fori_loop(0, q_ref.shape[1] // (DENSE_Q_SUB * DENSE_Q_UNROLL), rows, 0)

    for hh in range(o_ref.shape[2] // dv):
        @pl.when((j == pl.num_programs(3) - 1) & (h == hh))
        def _():
            o_ref[0, :, hh * dv:(hh + 1) * dv] = (acc_ref[...] / l_ref[...])[:, :dv]


def block_dense_attention(q, k_all, v_all, scale):
    B, T, H, dq = q.shape
    NK, dv = k_all.shape[1], v_all.shape[-1]
    tq = min(DENSE_Q_TILE, T)
    tk = max(t for t in range(LANES, DENSE_K_TILE_MAX + 1, LANES) if NK % t == 0)

    def lanes(a):
        a = jnp.pad(a.astype(BF16), ((0, 0), (0, 0), (0, 0), (0, LANES - a.shape[-1])))
        return a.reshape(a.shape[0], a.shape[1], H * LANES)

    return pl.pallas_call(
        functools.partial(_dense_attn_kernel, scale=scale, dv=dv),
        grid=(B, T // tq, H, NK // tk),
        in_specs=[pl.BlockSpec((1, tq, LANES), lambda b, i, h, j: (b, i, h)),
                  pl.BlockSpec((1, tk, LANES), lambda b, i, h, j: (b, j, h)),
                  pl.BlockSpec((1, tk, LANES), lambda b, i, h, j: (b, j, h))],
        out_specs=pl.BlockSpec((1, tq, H * dv), lambda b, i, h, j: (b, i, 0)),
        out_shape=jax.ShapeDtypeStruct((B, T, H * dv), F32),
        scratch_shapes=[pltpu.VMEM((tq, 1), F32), pltpu.VMEM((tq, 1), F32), pltpu.VMEM((tq, LANES), F32)],
        compiler_params=pltpu.CompilerParams(vmem_limit_bytes=VMEM_LIMIT_BYTES),
    )(lanes(q), lanes(k_all), lanes(v_all))


SWA_SPAN = ATTN_BLOCK + 2 * SWA_WINDOW


def _swa_kernel(q_ref, k_ref, v_ref, kc_ref, vc_ref, sink_ref, o_ref):
    n = pl.program_id(1)
    T = k_ref.shape[1]
    start = pl.multiple_of(jnp.clip(n * ATTN_BLOCK - SWA_WINDOW, 0, T - SWA_SPAN), ATTN_BLOCK)
    kw = k_ref[0, pl.ds(start, SWA_SPAN), :]
    vw = v_ref[0, pl.ds(start, SWA_SPAN), :]
    hm = _head_mask(ATTN_BLOCK)
    q = q_ref[0] * (HEAD_DIM ** -0.5)
    q4 = (jnp.concatenate([q] * GROUP_HEADS, axis=0) * hm).astype(BF16)
    rows = GROUP_HEADS * ATTN_BLOCK
    q_pos = n * ATTN_BLOCK + lax.broadcasted_iota(jnp.int32, (rows, SWA_SPAN), 0) % ATTN_BLOCK
    k_pos = start + lax.broadcasted_iota(jnp.int32, (rows, SWA_SPAN), 1)
    s_loc = lax.dot_general(q4, kw, NT_DIMS, preferred_element_type=F32)
    s_loc = jnp.where(jnp.abs(q_pos - k_pos) <= SWA_WINDOW, s_loc, -jnp.inf)
    s_ctx = lax.dot_general(q4, kc_ref[0], NT_DIMS, preferred_element_type=F32)
    sink = sink_ref[...]
    m = jnp.maximum(jnp.maximum(jnp.max(s_loc, axis=-1, keepdims=True),
                                jnp.max(s_ctx, axis=-1, keepdims=True)), sink)
    p_loc = jnp.exp(s_loc - m)
    p_ctx = jnp.exp(s_ctx - m)
    l = jnp.sum(p_loc, axis=-1, keepdims=True) + jnp.sum(p_ctx, axis=-1, keepdims=True) + jnp.exp(sink - m)
    o = (jnp.dot(p_loc.astype(BF16), vw, preferred_element_type=F32)
         + jnp.dot(p_ctx.astype(BF16), vc_ref[0], preferred_element_type=F32)) * (hm / l)
    o_ref[0] = sum(o[h * ATTN_BLOCK:(h + 1) * ATTN_BLOCK] for h in range(GROUP_HEADS))


def window_attention(q, k, v, kc, vc, sink):
    B, T, H, d = q.shape
    G = H // k.shape[2]
    n_ctx = kc.shape[1]
    C = H * d
    rep = lambda a: jnp.repeat(a, G, axis=2).reshape(a.shape[0], a.shape[1], C).astype(BF16)
    sink_rows = jnp.repeat(sink.astype(F32), ATTN_BLOCK).reshape(H * ATTN_BLOCK, 1)
    full = lambda n: pl.BlockSpec((1, n, C), lambda b, i: (b, 0, 0))
    return pl.pallas_call(
        _swa_kernel,
        grid=(B, T // ATTN_BLOCK),
        in_specs=[pl.BlockSpec((1, ATTN_BLOCK, C), lambda b, i: (b, i, 0)),
                  full(T), full(T), full(n_ctx), full(n_ctx),
                  pl.BlockSpec(sink_rows.shape, lambda b, i: (0, 0))],
        out_specs=pl.BlockSpec((1, ATTN_BLOCK, C), lambda b, i: (b, i, 0)),
        out_shape=jax.ShapeDtypeStruct((B, T, C), F32),
        compiler_params=pltpu.CompilerParams(vmem_limit_bytes=VMEM_LIMIT_BYTES),
    )(q.reshape(B, T, C), rep(k), rep(v), rep(kc), rep(vc), sink_rows)


BF16 = jnp.bfloat16
LANES = 128
SUBLANES = 8
ROW_SEGS = D_MODEL // LANES
ROW_WORDS = ROW_SEGS // 2
PEER_PICKS = PEER_HEADS * PEER_TOPK
PEER_TOPK_TOKENS = 256
PEER_GATHER_TOKENS = 128
PEER_ACT_UNROLL = 8
VMEM_LIMIT_BYTES = 56 * 1024 * 1024


def _split_bf16(x, parts):
    out = []
    for _ in range(parts):
        p = x.astype(BF16)
        out.append(p)
        x = x - p.astype(F32)
    return out


def _topk_rows(s, k):
    n = s.shape[0]
    iota = lax.broadcasted_iota(jnp.int32, s.shape, 0)
    vals, idxs = [], []
    for _ in range(k):
        m = jnp.max(s, axis=0, keepdims=True)
        i = jnp.min(jnp.where(s == m, iota, n), axis=0, keepdims=True)
        vals.append(m)
        idxs.append(i)
        s = jnp.where(iota == i, -jnp.inf, s)
    return jnp.concatenate(vals, axis=0), jnp.concatenate(idxs, axis=0)


def _peer_topk_kernel(x_ref, wq_ref, keys_ref, eidx_ref, gate_ref):
    xb = x_ref[...].astype(BF16)
    q = jnp.dot(xb, wq_ref[...], preferred_element_type=F32)
    nt = (((1,), (1,)), ((), ()))
    sv, si = [], []
    for p in range(2):
        qp = q[:, p * PEER_DKEY:(p + 1) * PEER_DKEY].astype(BF16)
        s = lax.dot_general(keys_ref[0, p], qp, nt, preferred_element_type=F32)
        v_, i_ = _topk_rows(s, PEER_TOPK)
        sv.append(v_)
        si.append(i_)
    cs, ce = [], []
    half = PEER_TOPK // 2
    for a in range(half):
        nb = PEER_TOPK if a == 0 else half
        cs.append(sv[0][a:a + 1] + sv[1][:nb])
        ce.append(si[0][a:a + 1] * PEER_NKEYS + si[1][:nb])
    cs.append(sv[0][half:] + sv[1][0:1])
    ce.append(si[0][half:] * PEER_NKEYS + si[1][0:1])
    cand_s = jnp.concatenate(cs, axis=0)
    cand_e = jnp.concatenate(ce, axis=0)
    fs, fpos = _topk_rows(cand_s, PEER_TOPK)
    iota = lax.broadcasted_iota(jnp.int32, cand_e.shape, 0)
    eidx = [jnp.max(jnp.where(iota == fpos[j:j + 1], cand_e, -1), axis=0, keepdims=True)
            for j in range(PEER_TOPK)]
    ex = jnp.exp(fs - fs[0:1])
    eidx_ref[0] = jnp.concatenate(eidx, axis=0)
    gate_ref[0] = ex / jnp.sum(ex, axis=0, keepdims=True)


def peer_topk(h, wq, sub_keys):
    N, D = h.shape
    T = PEER_TOPK_TOKENS
    wqb = wq.astype(BF16)
    kb = sub_keys.astype(BF16)
    eidx, gate = pl.pallas_call(
        _peer_topk_kernel,
        grid=(N // T, PEER_HEADS),
        in_specs=[pl.BlockSpec((T, D), lambda i, h_: (i, 0)),
                  pl.BlockSpec((D, 2 * PEER_DKEY), lambda i, h_: (0, h_)),
                  pl.BlockSpec((1, 2, PEER_NKEYS, PEER_DKEY), lambda i, h_: (h_, 0, 0, 0))],
        out_specs=[pl.BlockSpec((1, PEER_TOPK, T), lambda i, h_: (h_, 0, i)),
                   pl.BlockSpec((1, PEER_TOPK, T), lambda i, h_: (h_, 0, i))],
        out_shape=[jax.ShapeDtypeStruct((PEER_HEADS, PEER_TOPK, N), jnp.int32),
                   jax.ShapeDtypeStruct((PEER_HEADS, PEER_TOPK, N), F32)],
        compiler_params=pltpu.CompilerParams(vmem_limit_bytes=VMEM_LIMIT_BYTES),
    )(h, wqb, kb)
    return eidx.reshape(PEER_PICKS, N), gate.reshape(PEER_PICKS, N)


def pack_expert_table(tab):
    E = tab.shape[0]
    t = tab.astype(BF16).reshape(E, ROW_WORDS, 2, LANES)
    t = jnp.swapaxes(t, -1, -2)
    return lax.bitcast_convert_type(t, jnp.uint32).reshape(E * ROW_WORDS, LANES)


def _stage_rows(idx_ref, tab_ref, stage_ref, t):
    for k in range(PEER_PICKS):
        off = pl.multiple_of(idx_ref[t, k], ROW_WORDS)
        stage_ref[k * ROW_WORDS:(k + 1) * ROW_WORDS, :] = tab_ref[pl.ds(off, ROW_WORDS), :]
    return pltpu.bitcast(stage_ref[...], BF16)


def _peer_act_kernel(idx_ref, x_ref, gate_ref, tab_ref, seg_mask_ref, group_ref, w_ref,
                     stage_ref, rows_ref):
    T = x_ref.shape[0]
    U = stage_ref.shape[0]

    def tokens(g, carry):
        for j in range(U):
            t = g * U + j
            sb = _stage_rows(idx_ref, tab_ref, stage_ref.at[j], t)
            xs = jnp.concatenate(_split_bf16(x_ref[t], 2), axis=0)
            r = lax.dot_general(xs, sb, NT_DIMS, preferred_element_type=F32)
            r = r * seg_mask_ref[...]
            rows_ref[t] = r[:SUBLANES] + r[SUBLANES:]
        return carry

    lax.fori_loop(0, T // U, tokens, 0)
    rows = rows_ref[...].reshape(T * SUBLANES, PEER_PICKS * ROW_SEGS)
    part = jnp.zeros((T * SUBLANES, PEER_PICKS), F32)
    for piece in _split_bf16(rows, 3):
        part = part + jnp.dot(piece, group_ref[...], preferred_element_type=F32)
    act = jnp.sum(part.reshape(T, SUBLANES, PEER_PICKS), axis=1)
    w_ref[...] = gate_ref[...] * (0.5 * act * (1.0 + lax.erf(act * (2.0 ** -0.5))))


def _peer_out_kernel(idx_ref, w_ref, x_ref, g_ref, tab_ref, expand_ref, seg_mask_ref, f_ref, stage_ref):
    T = w_ref.shape[0]
    U = stage_ref.shape[0]

    def tokens(g, carry):
        w8 = w_ref[pl.ds(pl.multiple_of(g * U, U), U), :]
        hi, lo = _split_bf16(w8, 2)
        lhs = jnp.concatenate([jnp.broadcast_to(p[j:j + 1], (SUBLANES, PEER_PICKS))
                               for j in range(U) for p in (hi, lo)], axis=0)
        wrep = jnp.dot(lhs, expand_ref[...], preferred_element_type=F32)
        for j in range(U):
            t = g * U + j
            sb = _stage_rows(idx_ref, tab_ref, stage_ref.at[j], t)
            wsel = (wrep[j * 2 * SUBLANES:(j + 1) * 2 * SUBLANES] * seg_mask_ref[...]).astype(BF16)
            o = jnp.dot(wsel, sb, preferred_element_type=F32)
            f_ref[t] = x_ref[t] + g_ref[0] * (o[:SUBLANES] + o[SUBLANES:])
        return carry

    lax.fori_loop(0, T // U, tokens, 0)


def _peer_constants():
    cols = np.arange(PEER_PICKS * ROW_SEGS)
    seg_mask = (cols[None, :] % ROW_SEGS == np.arange(2 * SUBLANES)[:, None] % SUBLANES)
    group = (cols[:, None] // ROW_SEGS == np.arange(PEER_PICKS)[None, :])
    return (jnp.asarray(seg_mask, F32), jnp.asarray(group, BF16), jnp.asarray(group.T, BF16))


def peer_ffn(h, x, gate2, group_tokens, wq, sub_keys, u_packed, v_packed):
    N, D = h.shape
    T = PEER_GATHER_TOKENS
    eidx, gate = peer_topk(h, wq, sub_keys)
    seg_mask, group, expand = _peer_constants()
    rows3 = lambda a: a.reshape(a.shape[0], ROW_SEGS, LANES)
    offs = eidx.T * ROW_WORDS
    idx_spec = pl.BlockSpec((T, PEER_PICKS), lambda i: (i, 0), memory_space=pltpu.SMEM)
    tab_spec = pl.BlockSpec(u_packed.shape, lambda i: (0, 0), pipeline_mode=pl.Buffered(1))
    tok_spec = pl.BlockSpec((T, ROW_SEGS, LANES), lambda i: (i, 0, 0))
    const = lambda shape: pl.BlockSpec(shape, lambda i: (0, 0))
    params = pltpu.CompilerParams(vmem_limit_bytes=VMEM_LIMIT_BYTES)
    w = pl.pallas_call(
        _peer_act_kernel,
        grid=(N // T,),
        in_specs=[idx_spec, tok_spec,
                  pl.BlockSpec((T, PEER_PICKS), lambda i: (i, 0)),
                  tab_spec, const(seg_mask.shape), const(group.shape)],
        out_specs=pl.BlockSpec((T, PEER_PICKS), lambda i: (i, 0)),
        out_shape=jax.ShapeDtypeStruct((N, PEER_PICKS), F32),
        scratch_shapes=[pltpu.VMEM((PEER_ACT_UNROLL, PEER_PICKS * ROW_WORDS, LANES), jnp.uint32),
                        pltpu.VMEM((T, SUBLANES, PEER_PICKS * ROW_SEGS), F32)],
        compiler_params=params,
    )(offs, rows3(h), gate.T, u_packed, seg_mask, group)
    out = pl.pallas_call(
        _peer_out_kernel,
        grid=(N // T,),
        in_specs=[idx_spec,
                  pl.BlockSpec((T, PEER_PICKS), lambda i: (i, 0)),
                  tok_spec,
                  pl.BlockSpec((1, ROW_SEGS, LANES), lambda i: (i // (group_tokens // T), 0, 0)),
                  tab_spec, const(expand.shape), const(seg_mask.shape)],
        out_specs=tok_spec,
        out_shape=jax.ShapeDtypeStruct((N, ROW_SEGS, LANES), F32),
        scratch_shapes=[pltpu.VMEM((SUBLANES, PEER_PICKS * ROW_WORDS, LANES), jnp.uint32)],
        compiler_params=params,
    )(offs, w, rows3(x), rows3(gate2), v_packed, expand, seg_mask)
    return out.reshape(N, D)


PROJ_TOKENS = 512
MOD_ROWS = SUBLANES
IN_ALIGNED = tuple(i for i, s_ in enumerate(IN_SIZES) if s_ % LANES == 0)
IN_SMALL = tuple(i for i, s_ in enumerate(IN_SIZES) if s_ % LANES)


def _rms_modulate(x, gain, scale1p, shift):
    r = lax.rsqrt(jnp.mean(x * x, axis=-1, keepdims=True) + EPS)
    return (x * r * gain) * scale1p + shift


def _in_proj_kernel(x_ref, mod_ref, w_ref, *out_refs):
    mod = mod_ref[0]
    h = _rms_modulate(x_ref[...], mod[0:1], mod[1:2], mod[2:3])
    y = jnp.dot(h.astype(BF16), w_ref[...], preferred_element_type=F32)
    off = 0
    for o_ref in out_refs:
        o_ref[...] = y[:, off:off + o_ref.shape[1]].astype(o_ref.dtype)
        off += o_ref.shape[1]


def in_projection(x, mod, w_in, group_tokens):
    N, D = x.shape
    T = min(PROJ_TOKENS, group_tokens)
    starts = np.cumsum((0,) + IN_SIZES)
    cols = np.concatenate([np.arange(starts[i], starts[i + 1]) for i in IN_ALIGNED + IN_SMALL])
    small = sum(IN_SIZES[i] for i in IN_SMALL)
    wp = jnp.pad(w_in[:, cols], ((0, 0), (0, -small % LANES))).astype(BF16)
    widths = [IN_SIZES[i] for i in IN_ALIGNED] + [small + (-small % LANES)]
    outs = pl.pallas_call(
        _in_proj_kernel,
        grid=(N // T,),
        in_specs=[pl.BlockSpec((T, D), lambda i: (i, 0)),
                  pl.BlockSpec((1, MOD_ROWS, D), lambda i: (i // (group_tokens // T), 0, 0)),
                  pl.BlockSpec(wp.shape, lambda i: (0, 0))],
        out_specs=[pl.BlockSpec((T, w_), lambda i: (i, 0)) for w_ in widths],
        out_shape=[jax.ShapeDtypeStruct((N, w_), F32) for w_ in widths],
        compiler_params=pltpu.CompilerParams(vmem_limit_bytes=VMEM_LIMIT_BYTES),
    )(x, mod, wp)
    groups = dict(zip(IN_ALIGNED, outs[:-1]))
    off = 0
    for i in IN_SMALL:
        groups[i] = outs[-1][:, off:off + IN_SIZES[i]]
        off += IN_SIZES[i]
    return [groups[i] for i in range(len(IN_SIZES))]


def _out_proj_kernel(ya_ref, hl_ref, mo_ref, yc_ref, yd_ref, x_ref, mod_ref, w_ref, xo_ref, h2_ref):
    yb = hl_ref[...] * jax.nn.sigmoid(mo_ref[...])
    y = jnp.concatenate([ya_ref[...], yb, yc_ref[...], yd_ref[...]], axis=-1).astype(BF16)
    mod = mod_ref[0]
    xn = x_ref[...] + mod[0:1] * jnp.dot(y, w_ref[...], preferred_element_type=F32)
    xo_ref[...] = xn
    h2_ref[...] = _rms_modulate(xn, mod[1:2], mod[2:3], mod[3:4])


def out_projection(ya, hl, mo, yc, yd, x, mod, w_out, group_tokens):
    N, D = x.shape
    T = min(PROJ_TOKENS, group_tokens)
    part = pl.BlockSpec((T, GROUP_WIDTH), lambda i: (i, 0))
    tok = pl.BlockSpec((T, D), lambda i: (i, 0))
    return pl.pallas_call(
        _out_proj_kernel,
        grid=(N // T,),
        in_specs=[part, part, part, part, part, tok,
                  pl.BlockSpec((1, MOD_ROWS, D), lambda i: (i // (group_tokens // T), 0, 0)),
                  pl.BlockSpec(w_out.shape, lambda i: (0, 0))],
        out_specs=[tok, tok],
        out_shape=[jax.ShapeDtypeStruct((N, D), F32)] * 2,
        compiler_params=pltpu.CompilerParams(vmem_limit_bytes=VMEM_LIMIT_BYTES),
    )(ya, hl, mo, yc, yd, x, mod, w_out.astype(BF16))


def _mod_rows(*rows):
    m = jnp.stack([jnp.broadcast_to(r, rows[-1].shape) for r in rows], axis=1)
    return jnp.pad(m, ((0, 0), (0, MOD_ROWS - len(rows)), (0, 0)))


def hybrid_layer(x, xc, c, c_ctx, need_ctx, angs_mla, angs_swa,
                 norm1_g, norm2_g, w_ada, b_ada, w_in, na_rpb, ml_conv, ml_gate_b,
                 mla_q_norm, mla_w_uq, mla_kv_norm, mla_w_ukv, swa_sink, w_out,
                 peer_wq, peer_keys, peer_u, peer_v):
    B, T, D = x.shape
    Tc = xc.shape[1]
    H = GROUP_HEADS
    flat = lambda a: a.reshape(-1, a.shape[-1])
    sh1, sc1, g1, sh2, sc2, g2 = jnp.split(jax.nn.silu(c) @ w_ada + b_ada, 6, axis=-1)
    sh1c, sc1c, g1c, sh2c, sc2c, g2c = jnp.split((jax.nn.silu(c_ctx) @ w_ada + b_ada)[None], 6, axis=-1)
    lat = in_projection(flat(x), _mod_rows(norm1_g, 1.0 + sc1, sh1), w_in, T)
    cx = in_projection(flat(xc), _mod_rows(norm1_g, 1.0 + sc1c, sh1c), w_in, B * Tc)
    (na_q, na_k, na_v, ml_qk, ml_v, ml_o, ml_g,
     mla_cq, mla_ckv, mla_kr, sw_q, sw_k, sw_v) = [a.reshape(B, T, -1) for a in lat]
    (na_qc, na_kc, na_vc, ml_qkc, ml_vc, ml_oc, ml_gc,
     mla_cqc, mla_ckvc, mla_krc, sw_qc, sw_kc, sw_vc) = [a.reshape(B, Tc, -1) for a in cx]
    attn_scale = HEAD_DIM ** -0.5
    mla_scale = (MLA_NOPE + MLA_ROPE) ** -0.5
    kc_a, vc_a = heads(na_kc, H), heads(na_vc, H)
    y_a = neighbourhood_attention(na_q, na_k, na_v, na_kc, na_vc, na_rpb)
    h_lat, h_ctx = mlstm_mixer((ml_qk, ml_v, ml_g), (ml_qkc, ml_vc, ml_gc), ml_conv, ml_gate_b)
    q_m, k_m, v_m = mla_project(mla_cq, mla_ckv, mla_kr, mla_q_norm, mla_w_uq, mla_kv_norm, mla_w_ukv, angs_mla)
    qc_m, kc_m, vc_m = mla_project(mla_cqc, mla_ckvc, mla_krc, mla_q_norm, mla_w_uq, mla_kv_norm, mla_w_ukv, None)
    y_c = block_dense_attention(q_m, jnp.concatenate([kc_m, k_m], axis=1), jnp.concatenate([vc_m, v_m], axis=1), mla_scale)
    kc_d, vc_d = heads(sw_kc, SWA_KV_HEADS), heads(sw_vc, SWA_KV_HEADS)
    y_d = window_attention(rope_2d(heads(sw_q, H), angs_swa), rope_2d(heads(sw_k, SWA_KV_HEADS), angs_swa),
                           heads(sw_v, SWA_KV_HEADS), kc_d, vc_d, swa_sink)
    x2, h2 = out_projection(flat(y_a), h_lat.reshape(B * T, GROUP_WIDTH), flat(ml_o), flat(y_c), flat(y_d),
                            flat(x), _mod_rows(g1, norm2_g, 1.0 + sc2, sh2), w_out, T)
    u_packed, v_packed = pack_expert_table(peer_u), pack_expert_table(peer_v)
    x = peer_ffn(h2, x2, g2, T, peer_wq, peer_keys, u_packed, v_packed).reshape(B, T, D)
    if not need_ctx:
        return x, None
    xc2, h2c = out_projection(flat(ctx_attn(heads(na_qc, H), kc_a, vc_a, attn_scale)),
                              h_ctx.reshape(B * Tc, GROUP_WIDTH), flat(ml_oc),
                              flat(ctx_attn(qc_m, kc_m, vc_m, mla_scale)),
                              flat(ctx_attn(heads(sw_qc, H), kc_d, vc_d, attn_scale, swa_sink)),
                              flat(xc), _mod_rows(g1c, norm2_g, 1.0 + sc2c, sh2c), w_out, B * Tc)
    xc = peer_ffn(h2c, xc2, g2c, B * Tc, peer_wq, peer_keys, u_packed, v_packed).reshape(B, Tc, D)
    return x, xc


def _final_rmsnorm_kernel(x_ref, g_ref, o_ref):
    x = x_ref[...]
    o_ref[...] = x * lax.rsqrt(jnp.mean(x * x, axis=-1, keepdims=True) + EPS) * g_ref[...]


def final_rmsnorm(x, g):
    B, T, D = x.shape
    rows = 1024
    xf = x.reshape(B * T, D)
    out = pl.pallas_call(
        _final_rmsnorm_kernel,
        grid=(B * T // rows,),
        in_specs=[pl.BlockSpec((rows, D), lambda i: (i, 0)), pl.BlockSpec((1, D), lambda i: (0, 0))],
        out_specs=pl.BlockSpec((rows, D), lambda i: (i, 0)),
        out_shape=jax.ShapeDtypeStruct((B * T, D), x.dtype),
    )(xf, g.reshape(1, D))
    return out.reshape(B, T, D)


def kernel(x, c, ctx, c_ctx, norm1_g, norm2_g, w_ada, b_ada, w_in, na_rpb, ml_conv, ml_gate_b,
           mla_q_norm, mla_w_uq, mla_kv_norm, mla_w_ukv, swa_sink, w_out,
           peer_wq, peer_keys, peer_u, peer_v, final_norm_g):
    T = x.shape[1]
    angs_mla = axial_angles(T, MLA_ROPE)
    angs_swa = axial_angles(T, HEAD_DIM)
    xc = ctx
    for l in range(DEPTH):
        x, xc = hybrid_layer(x, xc, c, c_ctx, l < DEPTH - 1, angs_mla, angs_swa,
                             norm1_g[l], norm2_g[l], w_ada[l], b_ada[l], w_in[l], na_rpb[l],
                             ml_conv[l], ml_gate_b[l], mla_q_norm[l], mla_w_uq[l], mla_kv_norm[l],
                             mla_w_ukv[l], swa_sink[l], w_out[l], peer_wq[l], peer_keys[l],
                             peer_u[l], peer_v[l])
    return final_rmsnorm(x, final_norm_g)
```

```python
import functools

import jax
import jax.numpy as jnp
from jax import lax
import numpy as np
from jax.experimental import pallas as pl
from jax.experimental.pallas import tpu as pltpu

D_MODEL = 1024
BATCH = 2
SEQ = 16384
DEPTH = 2

CTX_LEN = 256
GRID_W = 64
N_MIXERS = 4
MIX_WIDTH = D_MODEL
GROUP_WIDTH = MIX_WIDTH // N_MIXERS
GROUP_HEADS = 4
HEAD_DIM = GROUP_WIDTH // GROUP_HEADS
NA_ROWS = 8
NA_COLS = 16
ML_CHUNK = 64
ML_CONV = 5
MLA_Q_RANK = 256
MLA_KV_RANK = 128
MLA_NOPE = 64
MLA_ROPE = 32
MLA_V = 64
SWA_KV_HEADS = 2
SWA_WINDOW = 128
ATTN_BLOCK = 128
PEER_HEADS = 8
PEER_NKEYS = 128
PEER_EXPERTS = PEER_NKEYS * PEER_NKEYS
PEER_DKEY = 128
PEER_TOPK = 16
PEER_BLOCK = 128
ROPE_BASE = 10000.0
EPS = 1e-6
IN_SIZES = (GROUP_WIDTH, GROUP_WIDTH, GROUP_WIDTH,
            2 * GROUP_WIDTH, GROUP_WIDTH, GROUP_WIDTH, 4 * GROUP_HEADS,
            MLA_Q_RANK, MLA_KV_RANK, MLA_ROPE,
            GROUP_WIDTH, SWA_KV_HEADS * HEAD_DIM, SWA_KV_HEADS * HEAD_DIM)
IN_WIDTH = sum(IN_SIZES)
F32 = jnp.float32


def rmsnorm(x, g):
    xf = x.astype(F32)
    y = xf * lax.rsqrt(jnp.mean(xf * xf, axis=-1, keepdims=True) + EPS) * g.astype(F32)
    return y.astype(x.dtype)


def heads(a, h):
    return a.reshape(a.shape[:-1] + (h, a.shape[-1] // h))


def split_cols(p):
    return jnp.split(p, np.cumsum(IN_SIZES)[:-1].tolist(), axis=-1)


def axial_angles(T, rot_dim):
    t = jnp.arange(T)
    row = (t // GRID_W).astype(F32)
    col = (t % GRID_W).astype(F32)
    half = rot_dim // 2
    inv = 1.0 / (ROPE_BASE ** (jnp.arange(0, half, 2, dtype=F32) / half))
    return row[:, None] * inv, col[:, None] * inv


def rope_1d(x, ang):
    cos = jnp.cos(ang)[None, :, None, :]
    sin = jnp.sin(ang)[None, :, None, :]
    x1, x2 = jnp.split(x.astype(F32), 2, axis=-1)
    return jnp.concatenate([x1 * cos - x2 * sin, x1 * sin + x2 * cos], axis=-1)


def rope_2d(x, angs):
    xr, xc = jnp.split(x, 2, axis=-1)
    return jnp.concatenate([rope_1d(xr, angs[0]), rope_1d(xc, angs[1])], axis=-1).astype(x.dtype)


def ctx_attn(q, k, v, scale, sink=None):
    rep = q.shape[2] // k.shape[2]
    k = jnp.repeat(k, rep, axis=2)
    v = jnp.repeat(v, rep, axis=2)
    s = jnp.einsum('bqhd,bkhd->bhqk', q, k).astype(F32) * scale
    nk = s.shape[-1]
    if sink is not None:
        s = jnp.concatenate([s, jnp.broadcast_to(sink.astype(F32)[None, :, None, None], s.shape[:-1] + (1,))], axis=-1)
    p = jax.nn.softmax(s, axis=-1)[..., :nk].astype(v.dtype)
    out = jnp.einsum('bhqk,bkhd->bqhd', p, v)
    return out.reshape(out.shape[:2] + (-1,))


NT_DIMS = (((1,), (1,)), ((), ()))
NA_SPAN = NA_ROWS * GRID_W


def _head_mask(width):
    rows = lax.broadcasted_iota(jnp.int32, (GROUP_HEADS * width, GROUP_WIDTH), 0) // width
    cols = lax.broadcasted_iota(jnp.int32, (GROUP_HEADS * width, GROUP_WIDTH), 1) // HEAD_DIM
    return (rows == cols).astype(F32)


def _na_kernel(q_ref, k_ref, v_ref, kc_ref, vc_ref, bias_ref, o_ref):
    r = pl.program_id(1)
    rows = pl.num_programs(1)
    rs = jnp.clip(r - NA_ROWS // 2, 0, rows - NA_ROWS)
    start = pl.multiple_of(rs * GRID_W, GRID_W)
    kw = k_ref[0, pl.ds(start, NA_SPAN), :]
    vw = v_ref[0, pl.ds(start, NA_SPAN), :]
    hm = _head_mask(GRID_W)
    q = q_ref[0] * (HEAD_DIM ** -0.5)
    q4 = (jnp.concatenate([q] * GROUP_HEADS, axis=0) * hm).astype(BF16)
    s_loc = lax.dot_general(q4, kw, NT_DIMS, preferred_element_type=F32) + bias_ref[rs - r + NA_ROWS - 1]
    s_ctx = lax.dot_general(q4, kc_ref[0], NT_DIMS, preferred_element_type=F32)
    m = jnp.maximum(jnp.max(s_loc, axis=-1, keepdims=True), jnp.max(s_ctx, axis=-1, keepdims=True))
    p_loc = jnp.exp(s_loc - m)
    p_ctx = jnp.exp(s_ctx - m)
    l = jnp.sum(p_loc, axis=-1, keepdims=True) + jnp.sum(p_ctx, axis=-1, keepdims=True)
    o = (jnp.dot(p_loc.astype(BF16), vw, preferred_element_type=F32)
         + jnp.dot(p_ctx.astype(BF16), vc_ref[0], preferred_element_type=F32)) * (hm / l)
    o_ref[0] = sum(o[h * GRID_W:(h + 1) * GRID_W] for h in range(GROUP_HEADS))


def _na_bias_table(rpb):
    c = np.arange(GRID_W)
    col_start = np.clip(c - NA_COLS // 2, 0, GRID_W - NA_COLS)
    valid = (c[None, :] >= col_start[:, None]) & (c[None, :] < col_start[:, None] + NA_COLS)
    dc = np.clip(c[None, :] - c[:, None] + NA_COLS - 1, 0, 2 * NA_COLS - 2)
    dr = np.arange(NA_ROWS)[:, None] + np.arange(NA_ROWS)[None, :]
    t = rpb.astype(F32)[:, dr][..., dc]
    t = jnp.where(valid[None, None, None], t, -jnp.inf)
    return jnp.transpose(t, (1, 0, 3, 2, 4)).reshape(NA_ROWS, GROUP_HEADS * GRID_W, NA_SPAN)


def neighbourhood_attention(q, k, v, kc, vc, rpb):
    B, T, C = q.shape
    rows = T // GRID_W
    n_ctx = kc.shape[1]
    bias = _na_bias_table(rpb)
    full = lambda n: pl.BlockSpec((1, n, C), lambda b, r: (b, 0, 0))
    return pl.pallas_call(
        _na_kernel,
        grid=(B, rows),
        in_specs=[pl.BlockSpec((1, GRID_W, C), lambda b, r: (b, r, 0)),
                  full(T), full(T), full(n_ctx), full(n_ctx),
                  pl.BlockSpec(bias.shape, lambda b, r: (0, 0, 0))],
        out_specs=pl.BlockSpec((1, GRID_W, C), lambda b, r: (b, r, 0)),
        out_shape=jax.ShapeDtypeStruct((B, T, C), F32),
        compiler_params=pltpu.CompilerParams(vmem_limit_bytes=VMEM_LIMIT_BYTES),
    )(q, k.astype(BF16), v.astype(BF16), kc.astype(BF16), vc.astype(BF16), bias)


def short_conv(a, w):
    T = a.shape[1]
    pad = w.shape[0] // 2
    ap = jnp.pad(a, ((0, 0), (pad, pad), (0, 0)))
    out = ap[:, :T] * w[0]
    for j in range(1, w.shape[0]):
        out = out + ap[:, j:j + T] * w[j]
    return out


ML_CHUNKS_PER_STEP = CTX_LEN // ML_CHUNK


def _bmm(a, b, contract):
    return lax.dot_general(a.astype(BF16), b.astype(BF16), (contract, ((0,), (0,))),
                           preferred_element_type=F32)


def _mlstm_chunk(qt, kt, vt, irow, brow, state, backward):
    L = ML_CHUNK
    C, nrow, m = state
    row = lax.broadcasted_iota(jnp.int32, (1, L, L), 1)
    col = lax.broadcasted_iota(jnp.int32, (1, L, L), 2)
    seen = (row <= col) if backward else (row >= col)
    eye = row == col

    def as_col(r):
        return jnp.sum(jnp.where(eye, r, 0.0), axis=2, keepdims=True)

    blast = brow[:, :, 0:1] if backward else brow[:, :, L - 1:L]
    rrow = brow - irow
    bcol = as_col(brow)
    d_log = jnp.where(seen, bcol - rrow, -jnp.inf)
    inter = bcol + m
    m_t = jnp.maximum(inter, jnp.max(d_log, axis=2, keepdims=True))
    w = jnp.exp(d_log - m_t)
    a = jnp.exp(inter - m_t)
    s = _bmm(qt, kt, ((2,), (2,))) * w
    num = _bmm(s, vt, ((2,), (1,))) + a * _bmm(qt, C, ((2,), (1,)))
    den = jnp.sum(s, axis=2, keepdims=True) + a * jnp.sum(qt * nrow, axis=2, keepdims=True)
    h = num / jnp.maximum(jnp.abs(den), jnp.exp(-m_t))
    g = blast - rrow
    m_new = jnp.maximum(blast + m, jnp.max(g, axis=2, keepdims=True))
    kw = kt * as_col(jnp.exp(g - m_new))
    decay = jnp.exp(blast + m - m_new)
    C = decay * C + _bmm(jnp.swapaxes(kw, 1, 2), vt, ((2,), (1,)))
    nrow = decay * nrow + jnp.sum(kw, axis=1, keepdims=True)
    return h, (C, nrow, m_new)


def _mlstm_kernel(qf_ref, kf_ref, vf_ref, if_ref, bf_ref, qb_ref, kb_ref, vb_ref, ib_ref, bb_ref,
                  hf_ref, hb_ref, c_ref, n_ref, m_ref):
    N, L = qf_ref.shape[0], ML_CHUNK

    @pl.when(pl.program_id(0) == 0)
    def _():
        c_ref[...] = jnp.zeros(c_ref.shape, F32)
        n_ref[...] = jnp.zeros(n_ref.shape, F32)
        m_ref[...] = jnp.zeros(m_ref.shape, F32)

    fwd = (c_ref[:N], n_ref[:N], m_ref[:N])
    bwd = (c_ref[N:], n_ref[N:], m_ref[N:])
    for c in range(ML_CHUNKS_PER_STEP):
        rows = slice(c * L, (c + 1) * L)
        h, fwd = _mlstm_chunk(qf_ref[:, rows, :], kf_ref[:, rows, :], vf_ref[:, rows, :],
                              if_ref[:, 0, c:c + 1, :], bf_ref[:, 0, c:c + 1, :], fwd, False)
        hf_ref[:, rows, :] = h
        cb = ML_CHUNKS_PER_STEP - 1 - c
        rows = slice(cb * L, (cb + 1) * L)
        h, bwd = _mlstm_chunk(qb_ref[:, rows, :], kb_ref[:, rows, :], vb_ref[:, rows, :],
                              ib_ref[:, 0, cb:cb + 1, :], bb_ref[:, 0, cb:cb + 1, :], bwd, True)
        hb_ref[:, rows, :] = h
    for i, ref in enumerate((c_ref, n_ref, m_ref)):
        ref[:N] = fwd[i]
        ref[N:] = bwd[i]


def mlstm_scan(q, k, v, gates_f, gates_b, n_ctx):
    B, T, H, d = q.shape
    CB, L = ML_CHUNKS_PER_STEP, ML_CHUNK
    assert n_ctx == CB * L and T % (CB * L) == 0
    N, steps = B * H, T // (CB * L)
    hm = lambda a: jnp.swapaxes(a, 1, 2).reshape(N, T, d)
    gates = lambda a: jnp.swapaxes(a, 1, 2).reshape(N, steps, CB, L)
    chunked = lambda a: a.reshape(B, T // L, L, H)
    b_f = jnp.cumsum(chunked(gates_f[1]), axis=2).reshape(B, T, H)
    b_b = lax.cumsum(chunked(gates_b[1]), axis=2, reverse=True).reshape(B, T, H)
    back = lambda j: jnp.where(j == 0, 0, steps - j)
    seq_f = pl.BlockSpec((N, CB * L, d), lambda j: (0, j, 0))
    seq_b = pl.BlockSpec((N, CB * L, d), lambda j: (0, back(j), 0))
    gate_f = pl.BlockSpec((N, 1, CB, L), lambda j: (0, j, 0, 0))
    gate_b = pl.BlockSpec((N, 1, CB, L), lambda j: (0, back(j), 0, 0))
    qh, kh, vh = hm(q), hm(k), hm(v)
    hf, hb = pl.pallas_call(
        _mlstm_kernel,
        grid=(steps,),
        in_specs=[seq_f, seq_f, seq_f, gate_f, gate_f, seq_b, seq_b, seq_b, gate_b, gate_b],
        out_specs=[seq_f, seq_b],
        out_shape=[jax.ShapeDtypeStruct((N, T, d), F32)] * 2,
        scratch_shapes=[pltpu.VMEM((2 * N, d, d), F32), pltpu.VMEM((2 * N, 1, d), F32),
                        pltpu.VMEM((2 * N, 1, 1), F32)],
        compiler_params=pltpu.CompilerParams(vmem_limit_bytes=VMEM_LIMIT_BYTES),
    )(qh, kh, vh, gates(gates_f[0]), gates(b_f), qh, kh, vh, gates(gates_b[0]), gates(b_b))
    return jnp.swapaxes((hf + hb).reshape(B, H, T, d), 1, 2)


def mlstm_prep(qk, v, gates, conv_w, gate_b):
    qk = jax.nn.silu(short_conv(qk, conv_w))
    q, k = jnp.split(qk, 2, axis=-1)
    g = (gates + gate_b).astype(F32)
    i_f, f_f, i_b, f_b = jnp.split(g, 4, axis=-1)
    return (heads(q, GROUP_HEADS) * HEAD_DIM ** -0.5, heads(k, GROUP_HEADS), heads(v, GROUP_HEADS),
            (i_f, jax.nn.log_sigmoid(f_f), i_b, jax.nn.log_sigmoid(f_b)))


def mlstm_mixer(lat, ctx, conv_w, gate_b):
    ql, kl, vl, gl = mlstm_prep(lat[0], lat[1], lat[2], conv_w, gate_b)
    qc, kc, vc, gc = mlstm_prep(ctx[0], ctx[1], ctx[2], conv_w, gate_b)
    Tc = qc.shape[1]
    cat = lambda c_, l_: jnp.concatenate([c_, l_], axis=1)
    h = mlstm_scan(cat(qc, ql), cat(kc, kl), cat(vc, vl),
                   (cat(gc[0], gl[0]), cat(gc[1], gl[1])), (cat(gc[2], gl[2]), cat(gc[3], gl[3])), Tc)
    return h[:, Tc:], h[:, :Tc]


def mla_project(cq, ckv, kr, q_norm, w_uq, kv_norm, w_ukv, angs):
    q = heads(rmsnorm(cq, q_norm) @ w_uq, GROUP_HEADS)
    kv = heads(rmsnorm(ckv, kv_norm) @ w_ukv, GROUP_HEADS)
    q_nope, q_rope = q[..., :MLA_NOPE], q[..., MLA_NOPE:]
    k_nope, v = kv[..., :MLA_NOPE], kv[..., MLA_NOPE:]
    k_rope = kr[:, :, None, :]
    if angs is not None:
        q_rope = rope_2d(q_rope, angs)
        k_rope = rope_2d(k_rope, angs)
    k_rope = jnp.broadcast_to(k_rope, k_nope.shape[:-1] + (MLA_ROPE,))
    return (jnp.concatenate([q_nope, q_rope], axis=-1), jnp.concatenate([k_nope, k_rope], axis=-1), v)


LOG2_E = 1.4426950408889634
DENSE_Q_TILE = 1024
DENSE_Q_SUB = 256
DENSE_Q_UNROLL = 4
DENSE_K_TILE_MAX = 8320


def _dense_attn_kernel(q_ref, k_ref, v_ref, o_ref, m_ref, l_ref, acc_ref, *, scale, dv):
    h, j = pl.program_id(2), pl.program_id(3)

    @pl.when(j == 0)
    def _():
        m_ref[...] = jnp.full(m_ref.shape, -jnp.inf, F32)
        l_ref[...] = jnp.zeros(l_ref.shape, F32)
        acc_ref[...] = jnp.zeros(acc_ref.shape, F32)

    def rows(i, carry):
        for u in range(DENSE_Q_UNROLL):
            r = pl.ds(pl.multiple_of((i * DENSE_Q_UNROLL + u) * DENSE_Q_SUB, DENSE_Q_SUB), DENSE_Q_SUB)
            s = lax.dot_general(q_ref[0, r, :], k_ref[0], NT_DIMS,
                                preferred_element_type=F32) * (scale * LOG2_E)
            m_prev = m_ref[r, :]
            m_new = jnp.maximum(m_prev, jnp.max(s, axis=-1, keepdims=True))
            alpha = jnp.exp2(m_prev - m_new)
            p = jnp.exp2(s - m_new)
            l_ref[r, :] = alpha * l_ref[r, :] + jnp.sum(p, axis=-1, keepdims=True)
            acc_ref[r, :] = alpha * acc_ref[r, :] + jnp.dot(p.astype(BF16), v_ref[0],
                                                            preferred_element_type=F32)
            m_ref[r, :] = m_new
        return carry

    lax.fori_loop(0, q_ref.shape[1] // (DENSE_Q_SUB * DENSE_Q_UNROLL), rows, 0)

    for hh in range(o_ref.shape[2] // dv):
        @pl.when((j == pl.num_programs(3) - 1) & (h == hh))
        def _():
            o_ref[0, :, hh * dv:(hh + 1) * dv] = (acc_ref[...] / l_ref[...])[:, :dv]


def block_dense_attention(q, k_all, v_all, scale):
    B, T, H, dq = q.shape
    NK, dv = k_all.shape[1], v_all.shape[-1]
    tq = min(DENSE_Q_TILE, T)
    tk = max(t for t in range(LANES, DENSE_K_TILE_MAX + 1, LANES) if NK % t == 0)

    def lanes(a):
        a = jnp.pad(a.astype(BF16), ((0, 0), (0, 0), (0, 0), (0, LANES - a.shape[-1])))
        return a.reshape(a.shape[0], a.shape[1], H * LANES)

    return pl.pallas_call(
        functools.partial(_dense_attn_kernel, scale=scale, dv=dv),
        grid=(B, T // tq, H, NK // tk),
        in_specs=[pl.BlockSpec((1, tq, LANES), lambda b, i, h, j: (b, i, h)),
                  pl.BlockSpec((1, tk, LANES), lambda b, i, h, j: (b, j, h)),
                  pl.BlockSpec((1, tk, LANES), lambda b, i, h, j: (b, j, h))],
        out_specs=pl.BlockSpec((1, tq, H * dv), lambda b, i, h, j: (b, i, 0)),
        out_shape=jax.ShapeDtypeStruct((B, T, H * dv), F32),
        scratch_shapes=[pltpu.VMEM((tq, 1), F32), pltpu.VMEM((tq, 1), F32), pltpu.VMEM((tq, LANES), F32)],
        compiler_params=pltpu.CompilerParams(vmem_limit_bytes=VMEM_LIMIT_BYTES),
    )(lanes(q), lanes(k_all), lanes(v_all))


SWA_SPAN = ATTN_BLOCK + 2 * SWA_WINDOW


def _swa_kernel(q_ref, k_ref, v_ref, kc_ref, vc_ref, sink_ref, o_ref):
    n = pl.program_id(1)
    T = k_ref.shape[1]
    start = pl.multiple_of(jnp.clip(n * ATTN_BLOCK - SWA_WINDOW, 0, T - SWA_SPAN), ATTN_BLOCK)
    kw = k_ref[0, pl.ds(start, SWA_SPAN), :]
    vw = v_ref[0, pl.ds(start, SWA_SPAN), :]
    hm = _head_mask(ATTN_BLOCK)
    q = q_ref[0] * (HEAD_DIM ** -0.5)
    q4 = (jnp.concatenate([q] * GROUP_HEADS, axis=0) * hm).astype(BF16)
    rows = GROUP_HEADS * ATTN_BLOCK
    q_pos = n * ATTN_BLOCK + lax.broadcasted_iota(jnp.int32, (rows, SWA_SPAN), 0) % ATTN_BLOCK
    k_pos = start + lax.broadcasted_iota(jnp.int32, (rows, SWA_SPAN), 1)
    s_loc = lax.dot_general(q4, kw, NT_DIMS, preferred_element_type=F32)
    s_loc = jnp.where(jnp.abs(q_pos - k_pos) <= SWA_WINDOW, s_loc, -jnp.inf)
    s_ctx = lax.dot_general(q4, kc_ref[0], NT_DIMS, preferred_element_type=F32)
    sink = sink_ref[...]
    m = jnp.maximum(jnp.maximum(jnp.max(s_loc, axis=-1, keepdims=True),
                                jnp.max(s_ctx, axis=-1, keepdims=True)), sink)
    p_loc = jnp.exp(s_loc - m)
    p_ctx = jnp.exp(s_ctx - m)
    l = jnp.sum(p_loc, axis=-1, keepdims=True) + jnp.sum(p_ctx, axis=-1, keepdims=True) + jnp.exp(sink - m)
    o = (jnp.dot(p_loc.astype(BF16), vw, preferred_element_type=F32)
         + jnp.dot(p_ctx.astype(BF16), vc_ref[0], preferred_element_type=F32)) * (hm / l)
    o_ref[0] = sum(o[h * ATTN_BLOCK:(h + 1) * ATTN_BLOCK] for h in range(GROUP_HEADS))


def window_attention(q, k, v, kc, vc, sink):
    B, T, H, d = q.shape
    G = H // k.shape[2]
    n_ctx = kc.shape[1]
    C = H * d
    rep = lambda a: jnp.repeat(a, G, axis=2).reshape(a.shape[0], a.shape[1], C).astype(BF16)
    sink_rows = jnp.repeat(sink.astype(F32), ATTN_BLOCK).reshape(H * ATTN_BLOCK, 1)
    full = lambda n: pl.BlockSpec((1, n, C), lambda b, i: (b, 0, 0))
    return pl.pallas_call(
        _swa_kernel,
        grid=(B, T // ATTN_BLOCK),
        in_specs=[pl.BlockSpec((1, ATTN_BLOCK, C), lambda b, i: (b, i, 0)),
                  full(T), full(T), full(n_ctx), full(n_ctx),
                  pl.BlockSpec(sink_rows.shape, lambda b, i: (0, 0))],
        out_specs=pl.BlockSpec((1, ATTN_BLOCK, C), lambda b, i: (b, i, 0)),
        out_shape=jax.ShapeDtypeStruct((B, T, C), F32),
        compiler_params=pltpu.CompilerParams(vmem_limit_bytes=VMEM_LIMIT_BYTES),
    )(q.reshape(B, T, C), rep(k), rep(v), rep(kc), rep(vc), sink_rows)


BF16 = jnp.bfloat16
LANES = 128
SUBLANES = 8
ROW_SEGS = D_MODEL // LANES
ROW_WORDS = ROW_SEGS // 2
PEER_PICKS = PEER_HEADS * PEER_TOPK
PEER_TOPK_TOKENS = 256
PEER_GATHER_TOKENS = 128
PEER_ACT_UNROLL = SUBLANES
VMEM_LIMIT_BYTES = 56 * 1024 * 1024


def _split_bf16(x, parts):
    out = []
    for _ in range(parts):
        p = x.astype(BF16)
        out.append(p)
        x = x - p.astype(F32)
    return out


def _topk_rows(s, k):
    n = s.shape[0]
    iota = lax.broadcasted_iota(jnp.int32, s.shape, 0)
    vals, idxs = [], []
    for _ in range(k):
        m = jnp.max(s, axis=0, keepdims=True)
        i = jnp.min(jnp.where(s == m, iota, n), axis=0, keepdims=True)
        vals.append(m)
        idxs.append(i)
        s = jnp.where(iota == i, -jnp.inf, s)
    return jnp.concatenate(vals, axis=0), jnp.concatenate(idxs, axis=0)


def _peer_topk_kernel(x_ref, wq_ref, keys_ref, eidx_ref, gate_ref):
    xb = x_ref[...].astype(BF16)
    q = jnp.dot(xb, wq_ref[...], preferred_element_type=F32)
    nt = (((1,), (1,)), ((), ()))
    sv, si = [], []
    for p in range(2):
        qp = q[:, p * PEER_DKEY:(p + 1) * PEER_DKEY].astype(BF16)
        s = lax.dot_general(keys_ref[0, p], qp, nt, preferred_element_type=F32)
        v_, i_ = _topk_rows(s, PEER_TOPK)
        sv.append(v_)
        si.append(i_)
    cs, ce = [], []
    half = PEER_TOPK // 2
    for a in range(half):
        nb = PEER_TOPK if a == 0 else half
        cs.append(sv[0][a:a + 1] + sv[1][:nb])
        ce.append(si[0][a:a + 1] * PEER_NKEYS + si[1][:nb])
    cs.append(sv[0][half:] + sv[1][0:1])
    ce.append(si[0][half:] * PEER_NKEYS + si[1][0:1])
    cand_s = jnp.concatenate(cs, axis=0)
    cand_e = jnp.concatenate(ce, axis=0)
    fs, fpos = _topk_rows(cand_s, PEER_TOPK)
    iota = lax.broadcasted_iota(jnp.int32, cand_e.shape, 0)
    eidx = [jnp.max(jnp.where(iota == fpos[j:j + 1], cand_e, -1), axis=0, keepdims=True)
            for j in range(PEER_TOPK)]
    ex = jnp.exp(fs - fs[0:1])
    eidx_ref[0] = jnp.concatenate(eidx, axis=0)
    gate_ref[0] = ex / jnp.sum(ex, axis=0, keepdims=True)


def peer_topk(h, wq, sub_keys):
    N, D = h.shape
    T = PEER_TOPK_TOKENS
    wqb = wq.astype(BF16)
    kb = sub_keys.astype(BF16)
    eidx, gate = pl.pallas_call(
        _peer_topk_kernel,
        grid=(N // T, PEER_HEADS),
        in_specs=[pl.BlockSpec((T, D), lambda i, h_: (i, 0)),
                  pl.BlockSpec((D, 2 * PEER_DKEY), lambda i, h_: (0, h_)),
                  pl.BlockSpec((1, 2, PEER_NKEYS, PEER_DKEY), lambda i, h_: (h_, 0, 0, 0))],
        out_specs=[pl.BlockSpec((1, PEER_TOPK, T), lambda i, h_: (h_, 0, i)),
                   pl.BlockSpec((1, PEER_TOPK, T), lambda i, h_: (h_, 0, i))],
        out_shape=[jax.ShapeDtypeStruct((PEER_HEADS, PEER_TOPK, N), jnp.int32),
                   jax.ShapeDtypeStruct((PEER_HEADS, PEER_TOPK, N), F32)],
        compiler_params=pltpu.CompilerParams(vmem_limit_bytes=VMEM_LIMIT_BYTES),
    )(h, wqb, kb)
    return eidx.reshape(PEER_PICKS, N), gate.reshape(PEER_PICKS, N)


def pack_expert_table(tab):
    E = tab.shape[0]
    t = tab.astype(BF16).reshape(E, ROW_WORDS, 2, LANES)
    t = jnp.swapaxes(t, -1, -2)
    return lax.bitcast_convert_type(t, jnp.uint32).reshape(E * ROW_WORDS, LANES)


def _stage_rows(idx_ref, tab_ref, stage_ref, t):
    for k in range(PEER_PICKS):
        off = pl.multiple_of(idx_ref[t, k], ROW_WORDS)
        stage_ref[k * ROW_WORDS:(k + 1) * ROW_WORDS, :] = tab_ref[pl.ds(off, ROW_WORDS), :]
    return pltpu.bitcast(stage_ref[...], BF16)


def _peer_act_kernel(idx_ref, x_ref, gate_ref, tab_ref, seg_mask_ref, group_ref, w_ref,
                     stage_ref, rows_ref):
    T = x_ref.shape[0]
    U = stage_ref.shape[0]

    sub = lax.broadcasted_iota(jnp.int32, (SUBLANES, PEER_PICKS * ROW_SEGS), 0)

    def tokens(g, carry):
        tile = jnp.zeros((SUBLANES, PEER_PICKS * ROW_SEGS), F32)
        for j in range(U):
            t = g * U + j
            sb = _stage_rows(idx_ref, tab_ref, stage_ref.at[j], t)
            xs = jnp.concatenate(_split_bf16(x_ref[t], 2), axis=0)
            r = lax.dot_general(xs, sb, NT_DIMS, preferred_element_type=F32)
            r = jnp.sum(r * seg_mask_ref[...], axis=0, keepdims=True)
            tile = jnp.where(sub == j, r, tile)
        rows_ref[g] = tile
        return carry

    lax.fori_loop(0, T // U, tokens, 0)
    rows = rows_ref[...].reshape(T, PEER_PICKS * ROW_SEGS)
    act = jnp.zeros((T, PEER_PICKS), F32)
    for piece in _split_bf16(rows, 3):
        act = act + jnp.dot(piece, group_ref[...], preferred_element_type=F32)
    w_ref[...] = gate_ref[...] * (0.5 * act * (1.0 + lax.erf(act * (2.0 ** -0.5))))


def _peer_out_kernel(idx_ref, w_ref, x_ref, g_ref, tab_ref, expand_ref, seg_mask_ref, f_ref, stage_ref):
    T = w_ref.shape[0]
    U = stage_ref.shape[0]

    def tokens(g, carry):
        w8 = w_ref[pl.ds(pl.multiple_of(g * U, U), U), :]
        hi, lo = _split_bf16(w8, 2)
        lhs = jnp.concatenate([jnp.broadcast_to(p[j:j + 1], (SUBLANES, PEER_PICKS))
                               for j in range(U) for p in (hi, lo)], axis=0)
        wrep = jnp.dot(lhs, expand_ref[...], preferred_element_type=F32)
        for j in range(U):
            t = g * U + j
            sb = _stage_rows(idx_ref, tab_ref, stage_ref.at[j], t)
            wsel = (wrep[j * 2 * SUBLANES:(j + 1) * 2 * SUBLANES] * seg_mask_ref[...]).astype(BF16)
            o = jnp.dot(wsel, sb, preferred_element_type=F32)
            f_ref[t] = x_ref[t] + g_ref[0] * (o[:SUBLANES] + o[SUBLANES:])
        return carry

    lax.fori_loop(0, T // U, tokens, 0)


def _peer_constants():
    cols = np.arange(PEER_PICKS * ROW_SEGS)
    seg_mask = (cols[None, :] % ROW_SEGS == np.arange(2 * SUBLANES)[:, None] % SUBLANES)
    group = (cols[:, None] // ROW_SEGS == np.arange(PEER_PICKS)[None, :])
    return (jnp.asarray(seg_mask, F32), jnp.asarray(group, BF16), jnp.asarray(group.T, BF16))


def peer_ffn(h, x, gate2, group_tokens, wq, sub_keys, u_packed, v_packed):
    N, D = h.shape
    T = PEER_GATHER_TOKENS
    eidx, gate = peer_topk(h, wq, sub_keys)
    seg_mask, group, expand = _peer_constants()
    rows3 = lambda a: a.reshape(a.shape[0], ROW_SEGS, LANES)
    offs = eidx.T * ROW_WORDS
    idx_spec = pl.BlockSpec((T, PEER_PICKS), lambda i: (i, 0), memory_space=pltpu.SMEM)
    tab_spec = pl.BlockSpec(u_packed.shape, lambda i: (0, 0), pipeline_mode=pl.Buffered(1))
    tok_spec = pl.BlockSpec((T, ROW_SEGS, LANES), lambda i: (i, 0, 0))
    const = lambda shape: pl.BlockSpec(shape, lambda i: (0, 0))
    params = pltpu.CompilerParams(vmem_limit_bytes=VMEM_LIMIT_BYTES)
    w = pl.pallas_call(
        _peer_act_kernel,
        grid=(N // T,),
        in_specs=[idx_spec, tok_spec,
                  pl.BlockSpec((T, PEER_PICKS), lambda i: (i, 0)),
                  tab_spec, const(seg_mask.shape), const(group.shape)],
        out_specs=pl.BlockSpec((T, PEER_PICKS), lambda i: (i, 0)),
        out_shape=jax.ShapeDtypeStruct((N, PEER_PICKS), F32),
        scratch_shapes=[pltpu.VMEM((PEER_ACT_UNROLL, PEER_PICKS * ROW_WORDS, LANES), jnp.uint32),
                        pltpu.VMEM((T // SUBLANES, SUBLANES, PEER_PICKS * ROW_SEGS), F32)],
        compiler_params=params,
    )(offs, rows3(h), gate.T, u_packed, seg_mask, group)
    out = pl.pallas_call(
        _peer_out_kernel,
        grid=(N // T,),
        in_specs=[idx_spec,
                  pl.BlockSpec((T, PEER_PICKS), lambda i: (i, 0)),
                  tok_spec,
                  pl.BlockSpec((1, ROW_SEGS, LANES), lambda i: (i // (group_tokens // T), 0, 0)),
                  tab_spec, const(expand.shape), const(seg_mask.shape)],
        out_specs=tok_spec,
        out_shape=jax.ShapeDtypeStruct((N, ROW_SEGS, LANES), F32),
        scratch_shapes=[pltpu.VMEM((SUBLANES, PEER_PICKS * ROW_WORDS, LANES), jnp.uint32)],
        compiler_params=params,
    )(offs, w, rows3(x), rows3(gate2), v_packed, expand, seg_mask)
    return out.reshape(N, D)


PROJ_TOKENS = 512
MOD_ROWS = SUBLANES
IN_ALIGNED = tuple(i for i, s_ in enumerate(IN_SIZES) if s_ % LANES == 0)
IN_SMALL = tuple(i for i, s_ in enumerate(IN_SIZES) if s_ % LANES)


def _rms_modulate(x, gain, scale1p, shift):
    r = lax.rsqrt(jnp.mean(x * x, axis=-1, keepdims=True) + EPS)
    return (x * r * gain) * scale1p + shift


def _in_proj_kernel(x_ref, mod_ref, w_ref, *out_refs):
    mod = mod_ref[0]
    h = _rms_modulate(x_ref[...], mod[0:1], mod[1:2], mod[2:3])
    y = jnp.dot(h.astype(BF16), w_ref[...], preferred_element_type=F32)
    off = 0
    for o_ref in out_refs:
        o_ref[...] = y[:, off:off + o_ref.shape[1]].astype(o_ref.dtype)
        off += o_ref.shape[1]


def in_projection(x, mod, w_in, group_tokens):
    N, D = x.shape
    T = min(PROJ_TOKENS, group_tokens)
    starts = np.cumsum((0,) + IN_SIZES)
    cols = np.concatenate([np.arange(starts[i], starts[i + 1]) for i in IN_ALIGNED + IN_SMALL])
    small = sum(IN_SIZES[i] for i in IN_SMALL)
    wp = jnp.pad(w_in[:, cols], ((0, 0), (0, -small % LANES))).astype(BF16)
    widths = [IN_SIZES[i] for i in IN_ALIGNED] + [small + (-small % LANES)]
    outs = pl.pallas_call(
        _in_proj_kernel,
        grid=(N // T,),
        in_specs=[pl.BlockSpec((T, D), lambda i: (i, 0)),
                  pl.BlockSpec((1, MOD_ROWS, D), lambda i: (i // (group_tokens // T), 0, 0)),
                  pl.BlockSpec(wp.shape, lambda i: (0, 0))],
        out_specs=[pl.BlockSpec((T, w_), lambda i: (i, 0)) for w_ in widths],
        out_shape=[jax.ShapeDtypeStruct((N, w_), F32) for w_ in widths],
        compiler_params=pltpu.CompilerParams(vmem_limit_bytes=VMEM_LIMIT_BYTES),
    )(x, mod, wp)
    groups = dict(zip(IN_ALIGNED, outs[:-1]))
    off = 0
    for i in IN_SMALL:
        groups[i] = outs[-1][:, off:off + IN_SIZES[i]]
        off += IN_SIZES[i]
    return [groups[i] for i in range(len(IN_SIZES))]


def _out_proj_kernel(ya_ref, hl_ref, mo_ref, yc_ref, yd_ref, x_ref, mod_ref, w_ref, xo_ref, h2_ref):
    yb = hl_ref[...] * jax.nn.sigmoid(mo_ref[...])
    y = jnp.concatenate([ya_ref[...], yb, yc_ref[...], yd_ref[...]], axis=-1).astype(BF16)
    mod = mod_ref[0]
    xn = x_ref[...] + mod[0:1] * jnp.dot(y, w_ref[...], preferred_element_type=F32)
    xo_ref[...] = xn
    h2_ref[...] = _rms_modulate(xn, mod[1:2], mod[2:3], mod[3:4])


def out_projection(ya, hl, mo, yc, yd, x, mod, w_out, group_tokens):
    N, D = x.shape
    T = min(PROJ_TOKENS, group_tokens)
    part = pl.BlockSpec((T, GROUP_WIDTH), lambda i: (i, 0))
    tok = pl.BlockSpec((T, D), lambda i: (i, 0))
    return pl.pallas_call(
        _out_proj_kernel,
        grid=(N // T,),
        in_specs=[part, part, part, part, part, tok,
                  pl.BlockSpec((1, MOD_ROWS, D), lambda i: (i // (group_tokens // T), 0, 0)),
                  pl.BlockSpec(w_out.shape, lambda i: (0, 0))],
        out_specs=[tok, tok],
        out_shape=[jax.ShapeDtypeStruct((N, D), F32)] * 2,
        compiler_params=pltpu.CompilerParams(vmem_limit_bytes=VMEM_LIMIT_BYTES),
    )(ya, hl, mo, yc, yd, x, mod, w_out.astype(BF16))


def _mod_rows(*rows):
    m = jnp.stack([jnp.broadcast_to(r, rows[-1].shape) for r in rows], axis=1)
    return jnp.pad(m, ((0, 0), (0, MOD_ROWS - len(rows)), (0, 0)))


def hybrid_layer(x, xc, c, c_ctx, need_ctx, angs_mla, angs_swa,
                 norm1_g, norm2_g, w_ada, b_ada, w_in, na_rpb, ml_conv, ml_gate_b,
                 mla_q_norm, mla_w_uq, mla_kv_norm, mla_w_ukv, swa_sink, w_out,
                 peer_wq, peer_keys, peer_u, peer_v):
    B, T, D = x.shape
    Tc = xc.shape[1]
    H = GROUP_HEADS
    flat = lambda a: a.reshape(-1, a.shape[-1])
    sh1, sc1, g1, sh2, sc2, g2 = jnp.split(jax.nn.silu(c) @ w_ada + b_ada, 6, axis=-1)
    sh1c, sc1c, g1c, sh2c, sc2c, g2c = jnp.split((jax.nn.silu(c_ctx) @ w_ada + b_ada)[None], 6, axis=-1)
    lat = in_projection(flat(x), _mod_rows(norm1_g, 1.0 + sc1, sh1), w_in, T)
    cx = in_projection(flat(xc), _mod_rows(norm1_g, 1.0 + sc1c, sh1c), w_in, B * Tc)
    (na_q, na_k, na_v, ml_qk, ml_v, ml_o, ml_g,
     mla_cq, mla_ckv, mla_kr, sw_q, sw_k, sw_v) = [a.reshape(B, T, -1) for a in lat]
    (na_qc, na_kc, na_vc, ml_qkc, ml_vc, ml_oc, ml_gc,
     mla_cqc, mla_ckvc, mla_krc, sw_qc, sw_kc, sw_vc) = [a.reshape(B, Tc, -1) for a in cx]
    attn_scale = HEAD_DIM ** -0.5
    mla_scale = (MLA_NOPE + MLA_ROPE) ** -0.5
    kc_a, vc_a = heads(na_kc, H), heads(na_vc, H)
    y_a = neighbourhood_attention(na_q, na_k, na_v, na_kc, na_vc, na_rpb)
    h_lat, h_ctx = mlstm_mixer((ml_qk, ml_v, ml_g), (ml_qkc, ml_vc, ml_gc), ml_conv, ml_gate_b)
    q_m, k_m, v_m = mla_project(mla_cq, mla_ckv, mla_kr, mla_q_norm, mla_w_uq, mla_kv_norm, mla_w_ukv, angs_mla)
    qc_m, kc_m, vc_m = mla_project(mla_cqc, mla_ckvc, mla_krc, mla_q_norm, mla_w_uq, mla_kv_norm, mla_w_ukv, None)
    y_c = block_dense_attention(q_m, jnp.concatenate([kc_m, k_m], axis=1), jnp.concatenate([vc_m, v_m], axis=1), mla_scale)
    kc_d, vc_d = heads(sw_kc, SWA_KV_HEADS), heads(sw_vc, SWA_KV_HEADS)
    y_d = window_attention(rope_2d(heads(sw_q, H), angs_swa), rope_2d(heads(sw_k, SWA_KV_HEADS), angs_swa),
                           heads(sw_v, SWA_KV_HEADS), kc_d, vc_d, swa_sink)
    x2, h2 = out_projection(flat(y_a), h_lat.reshape(B * T, GROUP_WIDTH), flat(ml_o), flat(y_c), flat(y_d),
                            flat(x), _mod_rows(g1, norm2_g, 1.0 + sc2, sh2), w_out, T)
    u_packed, v_packed = pack_expert_table(peer_u), pack_expert_table(peer_v)
    x = peer_ffn(h2, x2, g2, T, peer_wq, peer_keys, u_packed, v_packed).reshape(B, T, D)
    if not need_ctx:
        return x, None
    xc2, h2c = out_projection(flat(ctx_attn(heads(na_qc, H), kc_a, vc_a, attn_scale)),
                              h_ctx.reshape(B * Tc, GROUP_WIDTH), flat(ml_oc),
                              flat(ctx_attn(qc_m, kc_m, vc_m, mla_scale)),
                              flat(ctx_attn(heads(sw_qc, H), kc_d, vc_d, attn_scale, swa_sink)),
                              flat(xc), _mod_rows(g1c, norm2_g, 1.0 + sc2c, sh2c), w_out, B * Tc)
    xc = peer_ffn(h2c, xc2, g2c, B * Tc, peer_wq, peer_keys, u_packed, v_packed).reshape(B, Tc, D)
    return x, xc


def _final_rmsnorm_kernel(x_ref, g_ref, o_ref):
    x = x_ref[...]
    o_ref[...] = x * lax.rsqrt(jnp.mean(x * x, axis=-1, keepdims=True) + EPS) * g_ref[...]


def final_rmsnorm(x, g):
    B, T, D = x.shape
    rows = 1024
    xf = x.reshape(B * T, D)
    out = pl.pallas_call(
        _final_rmsnorm_kernel,
        grid=(B * T // rows,),
        in_specs=[pl.BlockSpec((rows, D), lambda i: (i, 0)), pl.BlockSpec((1, D), lambda i: (0, 0))],
        out_specs=pl.BlockSpec((rows, D), lambda i: (i, 0)),
        out_shape=jax.ShapeDtypeStruct((B * T, D), x.dtype),
    )(xf, g.reshape(1, D))
    return out.reshape(B, T, D)


def kernel(x, c, ctx, c_ctx, norm1_g, norm2_g, w_ada, b_ada, w_in, na_rpb, ml_conv, ml_gate_b,
           mla_q_norm, mla_w_uq, mla_kv_norm, mla_w_ukv, swa_sink, w_out,
           peer_wq, peer_keys, peer_u, peer_v, final_norm_g):
    T = x.shape[1]
    angs_mla = axial_angles(T, MLA_ROPE)
    angs_swa = axial_angles(T, HEAD_DIM)
    xc = ctx
    for l in range(DEPTH):
        x, xc = hybrid_layer(x, xc, c, c_ctx, l < DEPTH - 1, angs_mla, angs_swa,
                             norm1_g[l], norm2_g[l], w_ada[l], b_ada[l], w_in[l], na_rpb[l],
                             ml_conv[l], ml_gate_b[l], mla_q_norm[l], mla_w_uq[l], mla_kv_norm[l],
                             mla_w_ukv[l], swa_sink[l], w_out[l], peer_wq[l], peer_keys[l],
                             peer_u[l], peer_v[l])
    return final_rmsnorm(x, final_norm_g)
```

```python
import functools

import jax
import jax.numpy as jnp
from jax import lax
import numpy as np
from jax.experimental import pallas as pl
from jax.experimental.pallas import tpu as pltpu

D_MODEL = 1024
BATCH = 2
SEQ = 16384
DEPTH = 2

CTX_LEN = 256
GRID_W = 64
N_MIXERS = 4
MIX_WIDTH = D_MODEL
GROUP_WIDTH = MIX_WIDTH // N_MIXERS
GROUP_HEADS = 4
HEAD_DIM = GROUP_WIDTH // GROUP_HEADS
NA_ROWS = 8
NA_COLS = 16
ML_CHUNK = 64
ML_CONV = 5
MLA_Q_RANK = 256
MLA_KV_RANK = 128
MLA_NOPE = 64
MLA_ROPE = 32
MLA_V = 64
SWA_KV_HEADS = 2
SWA_WINDOW = 128
ATTN_BLOCK = 128
PEER_HEADS = 8
PEER_NKEYS = 128
PEER_EXPERTS = PEER_NKEYS * PEER_NKEYS
PEER_DKEY = 128
PEER_TOPK = 16
PEER_BLOCK = 128
ROPE_BASE = 10000.0
EPS = 1e-6
IN_SIZES = (GROUP_WIDTH, GROUP_WIDTH, GROUP_WIDTH,
            2 * GROUP_WIDTH, GROUP_WIDTH, GROUP_WIDTH, 4 * GROUP_HEADS,
            MLA_Q_RANK, MLA_KV_RANK, MLA_ROPE,
            GROUP_WIDTH, SWA_KV_HEADS * HEAD_DIM, SWA_KV_HEADS * HEAD_DIM)
IN_WIDTH = sum(IN_SIZES)
F32 = jnp.float32


def rmsnorm(x, g):
    xf = x.astype(F32)
    y = xf * lax.rsqrt(jnp.mean(xf * xf, axis=-1, keepdims=True) + EPS) * g.astype(F32)
    return y.astype(x.dtype)


def heads(a, h):
    return a.reshape(a.shape[:-1] + (h, a.shape[-1] // h))


def axial_angles(T, rot_dim):
    t = jnp.arange(T)
    row = (t // GRID_W).astype(F32)
    col = (t % GRID_W).astype(F32)
    half = rot_dim // 2
    inv = 1.0 / (ROPE_BASE ** (jnp.arange(0, half, 2, dtype=F32) / half))
    return row[:, None] * inv, col[:, None] * inv


def rope_1d(x, ang):
    cos = jnp.cos(ang)[None, :, None, :]
    sin = jnp.sin(ang)[None, :, None, :]
    x1, x2 = jnp.split(x.astype(F32), 2, axis=-1)
    return jnp.concatenate([x1 * cos - x2 * sin, x1 * sin + x2 * cos], axis=-1)


def rope_2d(x, angs):
    xr, xc = jnp.split(x, 2, axis=-1)
    return jnp.concatenate([rope_1d(xr, angs[0]), rope_1d(xc, angs[1])], axis=-1).astype(x.dtype)


def ctx_attn(q, k, v, scale, sink=None):
    rep = q.shape[2] // k.shape[2]
    k = jnp.repeat(k, rep, axis=2)
    v = jnp.repeat(v, rep, axis=2)
    s = jnp.einsum('bqhd,bkhd->bhqk', q, k).astype(F32) * scale
    nk = s.shape[-1]
    if sink is not None:
        s = jnp.concatenate([s, jnp.broadcast_to(sink.astype(F32)[None, :, None, None], s.shape[:-1] + (1,))], axis=-1)
    p = jax.nn.softmax(s, axis=-1)[..., :nk].astype(v.dtype)
    out = jnp.einsum('bhqk,bkhd->bqhd', p, v)
    return out.reshape(out.shape[:2] + (-1,))


NT_DIMS = (((1,), (1,)), ((), ()))
NA_SPAN = NA_ROWS * GRID_W


def _head_mask(width):
    rows = lax.broadcasted_iota(jnp.int32, (GROUP_HEADS * width, GROUP_WIDTH), 0) // width
    cols = lax.broadcasted_iota(jnp.int32, (GROUP_HEADS * width, GROUP_WIDTH), 1) // HEAD_DIM
    return (rows == cols).astype(F32)


def _na_kernel(q_ref, k_ref, v_ref, kc_ref, vc_ref, bias_ref, o_ref):
    r = pl.program_id(1)
    rows = pl.num_programs(1)
    rs = jnp.clip(r - NA_ROWS // 2, 0, rows - NA_ROWS)
    start = pl.multiple_of(rs * GRID_W, GRID_W)
    kw = k_ref[0, pl.ds(start, NA_SPAN), :]
    vw = v_ref[0, pl.ds(start, NA_SPAN), :]
    hm = _head_mask(GRID_W)
    q = q_ref[0] * (HEAD_DIM ** -0.5)
    q4 = (jnp.concatenate([q] * GROUP_HEADS, axis=0) * hm).astype(BF16)
    s_loc = lax.dot_general(q4, kw, NT_DIMS, preferred_element_type=F32) + bias_ref[rs - r + NA_ROWS - 1]
    s_ctx = lax.dot_general(q4, kc_ref[0], NT_DIMS, preferred_element_type=F32)
    m = jnp.maximum(jnp.max(s_loc, axis=-1, keepdims=True), jnp.max(s_ctx, axis=-1, keepdims=True))
    p_loc = jnp.exp(s_loc - m)
    p_ctx = jnp.exp(s_ctx - m)
    l = jnp.sum(p_loc, axis=-1, keepdims=True) + jnp.sum(p_ctx, axis=-1, keepdims=True)
    o = (jnp.dot(p_loc.astype(BF16), vw, preferred_element_type=F32)
         + jnp.dot(p_ctx.astype(BF16), vc_ref[0], preferred_element_type=F32)) * (hm / l)
    o_ref[0] = sum(o[h * GRID_W:(h + 1) * GRID_W] for h in range(GROUP_HEADS))


def _na_bias_table(rpb):
    c = np.arange(GRID_W)
    col_start = np.clip(c - NA_COLS // 2, 0, GRID_W - NA_COLS)
    valid = (c[None, :] >= col_start[:, None]) & (c[None, :] < col_start[:, None] + NA_COLS)
    dc = np.clip(c[None, :] - c[:, None] + NA_COLS - 1, 0, 2 * NA_COLS - 2)
    dr = np.arange(NA_ROWS)[:, None] + np.arange(NA_ROWS)[None, :]
    t = rpb.astype(F32)[:, dr][..., dc]
    t = jnp.where(valid[None, None, None], t, -jnp.inf)
    return jnp.transpose(t, (1, 0, 3, 2, 4)).reshape(NA_ROWS, GROUP_HEADS * GRID_W, NA_SPAN)


def neighbourhood_attention(q, k, v, kc, vc, rpb):
    B, T, C = q.shape
    rows = T // GRID_W
    n_ctx = kc.shape[1]
    bias = _na_bias_table(rpb)
    full = lambda n: pl.BlockSpec((1, n, C), lambda b, r: (b, 0, 0))
    return pl.pallas_call(
        _na_kernel,
        grid=(B, rows),
        in_specs=[pl.BlockSpec((1, GRID_W, C), lambda b, r: (b, r, 0)),
                  full(T), full(T), full(n_ctx), full(n_ctx),
                  pl.BlockSpec(bias.shape, lambda b, r: (0, 0, 0))],
        out_specs=pl.BlockSpec((1, GRID_W, C), lambda b, r: (b, r, 0)),
        out_shape=jax.ShapeDtypeStruct((B, T, C), F32),
        compiler_params=pltpu.CompilerParams(vmem_limit_bytes=VMEM_LIMIT_BYTES),
    )(q, k.astype(BF16), v.astype(BF16), kc.astype(BF16), vc.astype(BF16), bias)


def short_conv(a, w):
    T = a.shape[1]
    pad = w.shape[0] // 2
    ap = jnp.pad(a, ((0, 0), (pad, pad), (0, 0)))
    out = ap[:, :T] * w[0]
    for j in range(1, w.shape[0]):
        out = out + ap[:, j:j + T] * w[j]
    return out


ML_CHUNKS_PER_STEP = CTX_LEN // ML_CHUNK


def _bmm(a, b, contract):
    return lax.dot_general(a.astype(BF16), b.astype(BF16), (contract, ((0,), (0,))),
                           preferred_element_type=F32)


def _mlstm_chunk(qt, kt, vt, irow, brow, state, backward):
    L = ML_CHUNK
    C, nrow, m = state
    row = lax.broadcasted_iota(jnp.int32, (1, L, L), 1)
    col = lax.broadcasted_iota(jnp.int32, (1, L, L), 2)
    seen = (row <= col) if backward else (row >= col)
    eye = row == col

    def as_col(r):
        return jnp.sum(jnp.where(eye, r, 0.0), axis=2, keepdims=True)

    blast = brow[:, :, 0:1] if backward else brow[:, :, L - 1:L]
    rrow = brow - irow
    bcol = as_col(brow)
    d_log = jnp.where(seen, bcol - rrow, -jnp.inf)
    inter = bcol + m
    m_t = jnp.maximum(inter, jnp.max(d_log, axis=2, keepdims=True))
    w = jnp.exp(d_log - m_t)
    a = jnp.exp(inter - m_t)
    s = _bmm(qt, kt, ((2,), (2,))) * w
    num = _bmm(s, vt, ((2,), (1,))) + a * _bmm(qt, C, ((2,), (1,)))
    den = jnp.sum(s, axis=2, keepdims=True) + a * jnp.sum(qt * nrow, axis=2, keepdims=True)
    h = num / jnp.maximum(jnp.abs(den), jnp.exp(-m_t))
    g = blast - rrow
    m_new = jnp.maximum(blast + m, jnp.max(g, axis=2, keepdims=True))
    kw = kt * as_col(jnp.exp(g - m_new))
    decay = jnp.exp(blast + m - m_new)
    C = decay * C + _bmm(jnp.swapaxes(kw, 1, 2), vt, ((2,), (1,)))
    nrow = decay * nrow + jnp.sum(kw, axis=1, keepdims=True)
    return h, (C, nrow, m_new)


def _mlstm_kernel(qf_ref, kf_ref, vf_ref, if_ref, bf_ref, qb_ref, kb_ref, vb_ref, ib_ref, bb_ref,
                  hf_ref, hb_ref, c_ref, n_ref, m_ref):
    N, L = qf_ref.shape[0], ML_CHUNK

    @pl.when(pl.program_id(0) == 0)
    def _():
        c_ref[...] = jnp.zeros(c_ref.shape, F32)
        n_ref[...] = jnp.zeros(n_ref.shape, F32)
        m_ref[...] = jnp.zeros(m_ref.shape, F32)

    fwd = (c_ref[:N], n_ref[:N], m_ref[:N])
    bwd = (c_ref[N:], n_ref[N:], m_ref[N:])
    for c in range(ML_CHUNKS_PER_STEP):
        rows = slice(c * L, (c + 1) * L)
        h, fwd = _mlstm_chunk(qf_ref[:, rows, :], kf_ref[:, rows, :], vf_ref[:, rows, :],
                              if_ref[:, 0, c:c + 1, :], bf_ref[:, 0, c:c + 1, :], fwd, False)
        hf_ref[:, rows, :] = h
        cb = ML_CHUNKS_PER_STEP - 1 - c
        rows = slice(cb * L, (cb + 1) * L)
        h, bwd = _mlstm_chunk(qb_ref[:, rows, :], kb_ref[:, rows, :], vb_ref[:, rows, :],
                              ib_ref[:, 0, cb:cb + 1, :], bb_ref[:, 0, cb:cb + 1, :], bwd, True)
        hb_ref[:, rows, :] = h
    for i, ref in enumerate((c_ref, n_ref, m_ref)):
        ref[:N] = fwd[i]
        ref[N:] = bwd[i]


def mlstm_scan(q, k, v, gates_f, gates_b, n_ctx):
    B, T, H, d = q.shape
    CB, L = ML_CHUNKS_PER_STEP, ML_CHUNK
    assert n_ctx == CB * L and T % (CB * L) == 0
    N, steps = B * H, T // (CB * L)
    hm = lambda a: jnp.swapaxes(a, 1, 2).reshape(N, T, d)
    gates = lambda a: jnp.swapaxes(a, 1, 2).reshape(N, steps, CB, L)
    chunked = lambda a: a.reshape(B, T // L, L, H)
    b_f = jnp.cumsum(chunked(gates_f[1]), axis=2).reshape(B, T, H)
    b_b = lax.cumsum(chunked(gates_b[1]), axis=2, reverse=True).reshape(B, T, H)
    back = lambda j: jnp.where(j == 0, 0, steps - j)
    seq_f = pl.BlockSpec((N, CB * L, d), lambda j: (0, j, 0))
    seq_b = pl.BlockSpec((N, CB * L, d), lambda j: (0, back(j), 0))
    gate_f = pl.BlockSpec((N, 1, CB, L), lambda j: (0, j, 0, 0))
    gate_b = pl.BlockSpec((N, 1, CB, L), lambda j: (0, back(j), 0, 0))
    qh, kh, vh = hm(q), hm(k), hm(v)
    hf, hb = pl.pallas_call(
        _mlstm_kernel,
        grid=(steps,),
        in_specs=[seq_f, seq_f, seq_f, gate_f, gate_f, seq_b, seq_b, seq_b, gate_b, gate_b],
        out_specs=[seq_f, seq_b],
        out_shape=[jax.ShapeDtypeStruct((N, T, d), F32)] * 2,
        scratch_shapes=[pltpu.VMEM((2 * N, d, d), F32), pltpu.VMEM((2 * N, 1, d), F32),
                        pltpu.VMEM((2 * N, 1, 1), F32)],
        compiler_params=pltpu.CompilerParams(vmem_limit_bytes=VMEM_LIMIT_BYTES),
    )(qh, kh, vh, gates(gates_f[0]), gates(b_f), qh, kh, vh, gates(gates_b[0]), gates(b_b))
    return jnp.swapaxes((hf + hb).reshape(B, H, T, d), 1, 2)


def mlstm_prep(qk, v, gates, conv_w, gate_b):
    qk = jax.nn.silu(short_conv(qk, conv_w))
    q, k = jnp.split(qk, 2, axis=-1)
    g = (gates + gate_b).astype(F32)
    i_f, f_f, i_b, f_b = jnp.split(g, 4, axis=-1)
    return (heads(q, GROUP_HEADS) * HEAD_DIM ** -0.5, heads(k, GROUP_HEADS), heads(v, GROUP_HEADS),
            (i_f, jax.nn.log_sigmoid(f_f), i_b, jax.nn.log_sigmoid(f_b)))


def mlstm_mixer(lat, ctx, conv_w, gate_b):
    ql, kl, vl, gl = mlstm_prep(lat[0], lat[1], lat[2], conv_w, gate_b)
    qc, kc, vc, gc = mlstm_prep(ctx[0], ctx[1], ctx[2], conv_w, gate_b)
    Tc = qc.shape[1]
    cat = lambda c_, l_: jnp.concatenate([c_, l_], axis=1)
    h = mlstm_scan(cat(qc, ql), cat(kc, kl), cat(vc, vl),
                   (cat(gc[0], gl[0]), cat(gc[1], gl[1])), (cat(gc[2], gl[2]), cat(gc[3], gl[3])), Tc)
    return h[:, Tc:], h[:, :Tc]


def mla_project(cq, ckv, kr, q_norm, w_uq, kv_norm, w_ukv, angs):
    q = heads(rmsnorm(cq, q_norm) @ w_uq, GROUP_HEADS)
    kv = heads(rmsnorm(ckv, kv_norm) @ w_ukv, GROUP_HEADS)
    q_nope, q_rope = q[..., :MLA_NOPE], q[..., MLA_NOPE:]
    k_nope, v = kv[..., :MLA_NOPE], kv[..., MLA_NOPE:]
    k_rope = kr[:, :, None, :]
    if angs is not None:
        q_rope = rope_2d(q_rope, angs)
        k_rope = rope_2d(k_rope, angs)
    k_rope = jnp.broadcast_to(k_rope, k_nope.shape[:-1] + (MLA_ROPE,))
    return (jnp.concatenate([q_nope, q_rope], axis=-1), jnp.concatenate([k_nope, k_rope], axis=-1), v)


LOG2_E = 1.4426950408889634
DENSE_Q_TILE = 1024
DENSE_Q_SUB = 256
DENSE_Q_UNROLL = 4
DENSE_K_TILE_MAX = 8320


def _dense_attn_kernel(q_ref, k_ref, v_ref, o_ref, m_ref, l_ref, acc_ref, *, scale, dv):
    h, j = pl.program_id(2), pl.program_id(3)

    @pl.when(j == 0)
    def _():
        m_ref[...] = jnp.full(m_ref.shape, -jnp.inf, F32)
        l_ref[...] = jnp.zeros(l_ref.shape, F32)
        acc_ref[...] = jnp.zeros(acc_ref.shape, F32)

    def rows(i, carry):
        for u in range(DENSE_Q_UNROLL):
            r = pl.ds(pl.multiple_of((i * DENSE_Q_UNROLL + u) * DENSE_Q_SUB, DENSE_Q_SUB), DENSE_Q_SUB)
            s = lax.dot_general(q_ref[0, r, :], k_ref[0], NT_DIMS,
                                preferred_element_type=F32) * (scale * LOG2_E)
            m_prev = m_ref[r, :]
            m_new = jnp.maximum(m_prev, jnp.max(s, axis=-1, keepdims=True))
            alpha = jnp.exp2(m_prev - m_new)
            p = jnp.exp2(s - m_new)
            l_ref[r, :] = alpha * l_ref[r, :] + jnp.sum(p, axis=-1, keepdims=True)
            acc_ref[r, :] = alpha * acc_ref[r, :] + jnp.dot(p.astype(BF16), v_ref[0],
                                                            preferred_element_type=F32)
            m_ref[r, :] = m_new
        return carry

    lax.fori_loop(0, q_ref.shape[1] // (DENSE_Q_SUB * DENSE_Q_UNROLL), rows, 0)

    for hh in range(o_ref.shape[2] // dv):
        @pl.when((j == pl.num_programs(3) - 1) & (h == hh))
        def _():
            o_ref[0, :, hh * dv:(hh + 1) * dv] = (acc_ref[...] / l_ref[...])[:, :dv]


def block_dense_attention(q, k_all, v_all, scale):
    B, T, H, dq = q.shape
    NK, dv = k_all.shape[1], v_all.shape[-1]
    tq = min(DENSE_Q_TILE, T)
    tk = max(t for t in range(LANES, DENSE_K_TILE_MAX + 1, LANES) if NK % t == 0)

    def lanes(a):
        a = jnp.pad(a.astype(BF16), ((0, 0), (0, 0), (0, 0), (0, LANES - a.shape[-1])))
        return a.reshape(a.shape[0], a.shape[1], H * LANES)

    return pl.pallas_call(
        functools.partial(_dense_attn_kernel, scale=scale, dv=dv),
        grid=(B, T // tq, H, NK // tk),
        in_specs=[pl.BlockSpec((1, tq, LANES), lambda b, i, h, j: (b, i, h)),
                  pl.BlockSpec((1, tk, LANES), lambda b, i, h, j: (b, j, h)),
                  pl.BlockSpec((1, tk, LANES), lambda b, i, h, j: (b, j, h))],
        out_specs=pl.BlockSpec((1, tq, H * dv), lambda b, i, h, j: (b, i, 0)),
        out_shape=jax.ShapeDtypeStruct((B, T, H * dv), F32),
        scratch_shapes=[pltpu.VMEM((tq, 1), F32), pltpu.VMEM((tq, 1), F32), pltpu.VMEM((tq, LANES), F32)],
        compiler_params=pltpu.CompilerParams(vmem_limit_bytes=VMEM_LIMIT_BYTES),
    )(lanes(q), lanes(k_all), lanes(v_all))


SWA_SPAN = ATTN_BLOCK + 2 * SWA_WINDOW


def _swa_kernel(q_ref, k_ref, v_ref, kc_ref, vc_ref, sink_ref, o_ref):
    n = pl.program_id(1)
    T = k_ref.shape[1]
    start = pl.multiple_of(jnp.clip(n * ATTN_BLOCK - SWA_WINDOW, 0, T - SWA_SPAN), ATTN_BLOCK)
    kw = k_ref[0, pl.ds(start, SWA_SPAN), :]
    vw = v_ref[0, pl.ds(start, SWA_SPAN), :]
    hm = _head_mask(ATTN_BLOCK)
    q = q_ref[0] * (HEAD_DIM ** -0.5)
    q4 = (jnp.concatenate([q] * GROUP_HEADS, axis=0) * hm).astype(BF16)
    rows = GROUP_HEADS * ATTN_BLOCK
    q_pos = n * ATTN_BLOCK + lax.broadcasted_iota(jnp.int32, (rows, SWA_SPAN), 0) % ATTN_BLOCK
    k_pos = start + lax.broadcasted_iota(jnp.int32, (rows, SWA_SPAN), 1)
    s_loc = lax.dot_general(q4, kw, NT_DIMS, preferred_element_type=F32)
    s_loc = jnp.where(jnp.abs(q_pos - k_pos) <= SWA_WINDOW, s_loc, -jnp.inf)
    s_ctx = lax.dot_general(q4, kc_ref[0], NT_DIMS, preferred_element_type=F32)
    sink = sink_ref[...]
    m = jnp.maximum(jnp.maximum(jnp.max(s_loc, axis=-1, keepdims=True),
                                jnp.max(s_ctx, axis=-1, keepdims=True)), sink)
    p_loc = jnp.exp(s_loc - m)
    p_ctx = jnp.exp(s_ctx - m)
    l = jnp.sum(p_loc, axis=-1, keepdims=True) + jnp.sum(p_ctx, axis=-1, keepdims=True) + jnp.exp(sink - m)
    o = (jnp.dot(p_loc.astype(BF16), vw, preferred_element_type=F32)
         + jnp.dot(p_ctx.astype(BF16), vc_ref[0], preferred_element_type=F32)) * (hm / l)
    o_ref[0] = sum(o[h * ATTN_BLOCK:(h + 1) * ATTN_BLOCK] for h in range(GROUP_HEADS))


def window_attention(q, k, v, kc, vc, sink):
    B, T, H, d = q.shape
    G = H // k.shape[2]
    n_ctx = kc.shape[1]
    C = H * d
    rep = lambda a: jnp.repeat(a, G, axis=2).reshape(a.shape[0], a.shape[1], C).astype(BF16)
    sink_rows = jnp.repeat(sink.astype(F32), ATTN_BLOCK).reshape(H * ATTN_BLOCK, 1)
    full = lambda n: pl.BlockSpec((1, n, C), lambda b, i: (b, 0, 0))
    return pl.pallas_call(
        _swa_kernel,
        grid=(B, T // ATTN_BLOCK),
        in_specs=[pl.BlockSpec((1, ATTN_BLOCK, C), lambda b, i: (b, i, 0)),
                  full(T), full(T), full(n_ctx), full(n_ctx),
                  pl.BlockSpec(sink_rows.shape, lambda b, i: (0, 0))],
        out_specs=pl.BlockSpec((1, ATTN_BLOCK, C), lambda b, i: (b, i, 0)),
        out_shape=jax.ShapeDtypeStruct((B, T, C), F32),
        compiler_params=pltpu.CompilerParams(vmem_limit_bytes=VMEM_LIMIT_BYTES),
    )(q.reshape(B, T, C), rep(k), rep(v), rep(kc), rep(vc), sink_rows)


BF16 = jnp.bfloat16
LANES = 128
SUBLANES = 8
ROW_SEGS = D_MODEL // LANES
ROW_WORDS = ROW_SEGS // 2
PEER_PICKS = PEER_HEADS * PEER_TOPK
PEER_TOPK_TOKENS = 512
PEER_GATHER_TOKENS = 128
PEER_ACT_UNROLL = SUBLANES
VMEM_LIMIT_BYTES = 56 * 1024 * 1024


def _split_bf16(x, parts):
    out = []
    for _ in range(parts):
        p = x.astype(BF16)
        out.append(p)
        x = x - p.astype(F32)
    return out


def _topk_rows(s, k):
    n = s.shape[0]
    iota = lax.broadcasted_iota(jnp.int32, s.shape, 0)
    vals, idxs = [], []
    for _ in range(k):
        m = jnp.max(s, axis=0, keepdims=True)
        i = jnp.min(jnp.where(s == m, iota, n), axis=0, keepdims=True)
        vals.append(m)
        idxs.append(i)
        s = jnp.where(iota == i, -jnp.inf, s)
    return jnp.concatenate(vals, axis=0), jnp.concatenate(idxs, axis=0)


def _peer_topk_kernel(x_ref, wq_ref, keys_ref, eidx_ref, gate_ref):
    xb = x_ref[...].astype(BF16)
    q = jnp.dot(xb, wq_ref[...], preferred_element_type=F32)
    nt = (((1,), (1,)), ((), ()))
    sv, si = [], []
    for p in range(2):
        qp = q[:, p * PEER_DKEY:(p + 1) * PEER_DKEY].astype(BF16)
        s = lax.dot_general(keys_ref[0, p], qp, nt, preferred_element_type=F32)
        v_, i_ = _topk_rows(s, PEER_TOPK)
        sv.append(v_)
        si.append(i_)
    cs, ce = [], []
    half = PEER_TOPK // 2
    for a in range(half):
        nb = PEER_TOPK if a == 0 else half
        cs.append(sv[0][a:a + 1] + sv[1][:nb])
        ce.append(si[0][a:a + 1] * PEER_NKEYS + si[1][:nb])
    cs.append(sv[0][half:] + sv[1][0:1])
    ce.append(si[0][half:] * PEER_NKEYS + si[1][0:1])
    cand_s = jnp.concatenate(cs, axis=0)
    cand_e = jnp.concatenate(ce, axis=0)
    fs, fpos = _topk_rows(cand_s, PEER_TOPK)
    iota = lax.broadcasted_iota(jnp.int32, cand_e.shape, 0)
    eidx = [jnp.max(jnp.where(iota == fpos[j:j + 1], cand_e, -1), axis=0, keepdims=True)
            for j in range(PEER_TOPK)]
    ex = jnp.exp(fs - fs[0:1])
    eidx_ref[0] = jnp.concatenate(eidx, axis=0)
    gate_ref[0] = ex / jnp.sum(ex, axis=0, keepdims=True)


def peer_topk(h, wq, sub_keys):
    N, D = h.shape
    T = PEER_TOPK_TOKENS
    wqb = wq.astype(BF16)
    kb = sub_keys.astype(BF16)
    eidx, gate = pl.pallas_call(
        _peer_topk_kernel,
        grid=(N // T, PEER_HEADS),
        in_specs=[pl.BlockSpec((T, D), lambda i, h_: (i, 0)),
                  pl.BlockSpec((D, 2 * PEER_DKEY), lambda i, h_: (0, h_)),
                  pl.BlockSpec((1, 2, PEER_NKEYS, PEER_DKEY), lambda i, h_: (h_, 0, 0, 0))],
        out_specs=[pl.BlockSpec((1, PEER_TOPK, T), lambda i, h_: (h_, 0, i)),
                   pl.BlockSpec((1, PEER_TOPK, T), lambda i, h_: (h_, 0, i))],
        out_shape=[jax.ShapeDtypeStruct((PEER_HEADS, PEER_TOPK, N), jnp.int32),
                   jax.ShapeDtypeStruct((PEER_HEADS, PEER_TOPK, N), F32)],
        compiler_params=pltpu.CompilerParams(vmem_limit_bytes=VMEM_LIMIT_BYTES),
    )(h, wqb, kb)
    return eidx.reshape(PEER_PICKS, N), gate.reshape(PEER_PICKS, N)


def pack_expert_table(tab):
    E = tab.shape[0]
    t = tab.astype(BF16).reshape(E, ROW_WORDS, 2, LANES)
    t = jnp.swapaxes(t, -1, -2)
    return lax.bitcast_convert_type(t, jnp.uint32).reshape(E * ROW_WORDS, LANES)


def _stage_rows(idx_ref, tab_ref, stage_ref, t):
    for k in range(PEER_PICKS):
        off = pl.multiple_of(idx_ref[t, k], ROW_WORDS)
        stage_ref[k * ROW_WORDS:(k + 1) * ROW_WORDS, :] = tab_ref[pl.ds(off, ROW_WORDS), :]
    return pltpu.bitcast(stage_ref[...], BF16)


def _peer_act_kernel(idx_ref, x_ref, gate_ref, tab_ref, seg_mask_ref, group_ref, w_ref,
                     stage_ref, rows_ref):
    T = x_ref.shape[0]
    U = stage_ref.shape[0]

    sub = lax.broadcasted_iota(jnp.int32, (SUBLANES, PEER_PICKS * ROW_SEGS), 0)

    def tokens(g, carry):
        tile = jnp.zeros((SUBLANES, PEER_PICKS * ROW_SEGS), F32)
        for j in range(U):
            t = g * U + j
            sb = _stage_rows(idx_ref, tab_ref, stage_ref.at[j], t)
            xs = jnp.concatenate(_split_bf16(x_ref[t], 2), axis=0)
            r = lax.dot_general(xs, sb, NT_DIMS, preferred_element_type=F32)
            r = jnp.sum(r * seg_mask_ref[...], axis=0, keepdims=True)
            tile = jnp.where(sub == j, r, tile)
        rows_ref[g] = tile
        return carry

    lax.fori_loop(0, T // U, tokens, 0)
    rows = rows_ref[...].reshape(T, PEER_PICKS * ROW_SEGS)
    act = jnp.zeros((T, PEER_PICKS), F32)
    for piece in _split_bf16(rows, 3):
        act = act + jnp.dot(piece, group_ref[...], preferred_element_type=F32)
    w_ref[...] = gate_ref[...] * (0.5 * act * (1.0 + lax.erf(act * (2.0 ** -0.5))))


def _peer_out_kernel(idx_ref, w_ref, x_ref, g_ref, tab_ref, expand_ref, seg_mask_ref, f_ref, stage_ref):
    T = w_ref.shape[0]
    U = stage_ref.shape[0]

    def tokens(g, carry):
        w8 = w_ref[pl.ds(pl.multiple_of(g * U, U), U), :]
        hi, lo = _split_bf16(w8, 2)
        lhs = jnp.concatenate([jnp.broadcast_to(p[j:j + 1], (SUBLANES, PEER_PICKS))
                               for j in range(U) for p in (hi, lo)], axis=0)
        wrep = jnp.dot(lhs, expand_ref[...], preferred_element_type=F32)
        for j in range(U):
            t = g * U + j
            sb = _stage_rows(idx_ref, tab_ref, stage_ref.at[j], t)
            wsel = (wrep[j * 2 * SUBLANES:(j + 1) * 2 * SUBLANES] * seg_mask_ref[...]).astype(BF16)
            o = jnp.dot(wsel, sb, preferred_element_type=F32)
            f_ref[t] = x_ref[t] + g_ref[0] * (o[:SUBLANES] + o[SUBLANES:])
        return carry

    lax.fori_loop(0, T // U, tokens, 0)


def _peer_constants():
    cols = np.arange(PEER_PICKS * ROW_SEGS)
    seg_mask = (cols[None, :] % ROW_SEGS == np.arange(2 * SUBLANES)[:, None] % SUBLANES)
    group = (cols[:, None] // ROW_SEGS == np.arange(PEER_PICKS)[None, :])
    return (jnp.asarray(seg_mask, F32), jnp.asarray(group, BF16), jnp.asarray(group.T, BF16))


def peer_ffn(h, x, gate2, group_tokens, wq, sub_keys, u_packed, v_packed):
    N, D = h.shape
    T = PEER_GATHER_TOKENS
    eidx, gate = peer_topk(h, wq, sub_keys)
    seg_mask, group, expand = _peer_constants()
    rows3 = lambda a: a.reshape(a.shape[0], ROW_SEGS, LANES)
    offs = eidx.T * ROW_WORDS
    idx_spec = pl.BlockSpec((T, PEER_PICKS), lambda i: (i, 0), memory_space=pltpu.SMEM)
    tab_spec = pl.BlockSpec(u_packed.shape, lambda i: (0, 0), pipeline_mode=pl.Buffered(1))
    tok_spec = pl.BlockSpec((T, ROW_SEGS, LANES), lambda i: (i, 0, 0))
    const = lambda shape: pl.BlockSpec(shape, lambda i: (0, 0))
    params = pltpu.CompilerParams(vmem_limit_bytes=VMEM_LIMIT_BYTES)
    w = pl.pallas_call(
        _peer_act_kernel,
        grid=(N // T,),
        in_specs=[idx_spec, tok_spec,
                  pl.BlockSpec((T, PEER_PICKS), lambda i: (i, 0)),
                  tab_spec, const(seg_mask.shape), const(group.shape)],
        out_specs=pl.BlockSpec((T, PEER_PICKS), lambda i: (i, 0)),
        out_shape=jax.ShapeDtypeStruct((N, PEER_PICKS), F32),
        scratch_shapes=[pltpu.VMEM((PEER_ACT_UNROLL, PEER_PICKS * ROW_WORDS, LANES), jnp.uint32),
                        pltpu.VMEM((T // SUBLANES, SUBLANES, PEER_PICKS * ROW_SEGS), F32)],
        compiler_params=params,
    )(offs, rows3(h), gate.T, u_packed, seg_mask, group)
    out = pl.pallas_call(
        _peer_out_kernel,
        grid=(N // T,),
        in_specs=[idx_spec,
                  pl.BlockSpec((T, PEER_PICKS), lambda i: (i, 0)),
                  tok_spec,
                  pl.BlockSpec((1, ROW_SEGS, LANES), lambda i: (i // (group_tokens // T), 0, 0)),
                  tab_spec, const(expand.shape), const(seg_mask.shape)],
        out_specs=tok_spec,
        out_shape=jax.ShapeDtypeStruct((N, ROW_SEGS, LANES), F32),
        scratch_shapes=[pltpu.VMEM((SUBLANES, PEER_PICKS * ROW_WORDS, LANES), jnp.uint32)],
        compiler_params=params,
    )(offs, w, rows3(x), rows3(gate2), v_packed, expand, seg_mask)
    return out.reshape(N, D)


PROJ_TOKENS = 512
MOD_ROWS = SUBLANES
IN_ALIGNED = tuple(i for i, s_ in enumerate(IN_SIZES) if s_ % LANES == 0)
IN_SMALL = tuple(i for i, s_ in enumerate(IN_SIZES) if s_ % LANES)


def _rms_modulate(x, gain, scale1p, shift):
    r = lax.rsqrt(jnp.mean(x * x, axis=-1, keepdims=True) + EPS)
    return (x * r * gain) * scale1p + shift


def _in_proj_kernel(x_ref, mod_ref, w_ref, *out_refs):
    mod = mod_ref[0]
    h = _rms_modulate(x_ref[...], mod[0:1], mod[1:2], mod[2:3])
    y = jnp.dot(h.astype(BF16), w_ref[...], preferred_element_type=F32)
    off = 0
    for o_ref in out_refs:
        o_ref[...] = y[:, off:off + o_ref.shape[1]].astype(o_ref.dtype)
        off += o_ref.shape[1]


def in_projection(x, mod, w_in, group_tokens):
    N, D = x.shape
    T = min(PROJ_TOKENS, group_tokens)
    starts = np.cumsum((0,) + IN_SIZES)
    cols = np.concatenate([np.arange(starts[i], starts[i + 1]) for i in IN_ALIGNED + IN_SMALL])
    small = sum(IN_SIZES[i] for i in IN_SMALL)
    wp = jnp.pad(w_in[:, cols], ((0, 0), (0, -small % LANES))).astype(BF16)
    widths = [IN_SIZES[i] for i in IN_ALIGNED] + [small + (-small % LANES)]
    outs = pl.pallas_call(
        _in_proj_kernel,
        grid=(N // T,),
        in_specs=[pl.BlockSpec((T, D), lambda i: (i, 0)),
                  pl.BlockSpec((1, MOD_ROWS, D), lambda i: (i // (group_tokens // T), 0, 0)),
                  pl.BlockSpec(wp.shape, lambda i: (0, 0))],
        out_specs=[pl.BlockSpec((T, w_), lambda i: (i, 0)) for w_ in widths],
        out_shape=[jax.ShapeDtypeStruct((N, w_), F32) for w_ in widths],
        compiler_params=pltpu.CompilerParams(vmem_limit_bytes=VMEM_LIMIT_BYTES),
    )(x, mod, wp)
    groups = dict(zip(IN_ALIGNED, outs[:-1]))
    off = 0
    for i in IN_SMALL:
        groups[i] = outs[-1][:, off:off + IN_SIZES[i]]
        off += IN_SIZES[i]
    return [groups[i] for i in range(len(IN_SIZES))]


def _out_proj_kernel(ya_ref, hl_ref, mo_ref, yc_ref, yd_ref, x_ref, mod_ref, w_ref, xo_ref, h2_ref):
    yb = hl_ref[...] * jax.nn.sigmoid(mo_ref[...])
    y = jnp.concatenate([ya_ref[...], yb, yc_ref[...], yd_ref[...]], axis=-1).astype(BF16)
    mod = mod_ref[0]
    xn = x_ref[...] + mod[0:1] * jnp.dot(y, w_ref[...], preferred_element_type=F32)
    xo_ref[...] = xn
    h2_ref[...] = _rms_modulate(xn, mod[1:2], mod[2:3], mod[3:4])


def out_projection(ya, hl, mo, yc, yd, x, mod, w_out, group_tokens):
    N, D = x.shape
    T = min(PROJ_TOKENS, group_tokens)
    part = pl.BlockSpec((T, GROUP_WIDTH), lambda i: (i, 0))
    tok = pl.BlockSpec((T, D), lambda i: (i, 0))
    return pl.pallas_call(
        _out_proj_kernel,
        grid=(N // T,),
        in_specs=[part, part, part, part, part, tok,
                  pl.BlockSpec((1, MOD_ROWS, D), lambda i: (i // (group_tokens // T), 0, 0)),
                  pl.BlockSpec(w_out.shape, lambda i: (0, 0))],
        out_specs=[tok, tok],
        out_shape=[jax.ShapeDtypeStruct((N, D), F32)] * 2,
        compiler_params=pltpu.CompilerParams(vmem_limit_bytes=VMEM_LIMIT_BYTES),
    )(ya, hl, mo, yc, yd, x, mod, w_out.astype(BF16))


def _mod_rows(*rows):
    m = jnp.stack([jnp.broadcast_to(r, rows[-1].shape) for r in rows], axis=1)
    return jnp.pad(m, ((0, 0), (0, MOD_ROWS - len(rows)), (0, 0)))


def hybrid_layer(x, xc, c, c_ctx, need_ctx, angs_mla, angs_swa,
                 norm1_g, norm2_g, w_ada, b_ada, w_in, na_rpb, ml_conv, ml_gate_b,
                 mla_q_norm, mla_w_uq, mla_kv_norm, mla_w_ukv, swa_sink, w_out,
                 peer_wq, peer_keys, peer_u, peer_v):
    B, T, D = x.shape
    Tc = xc.shape[1]
    H = GROUP_HEADS
    flat = lambda a: a.reshape(-1, a.shape[-1])
    sh1, sc1, g1, sh2, sc2, g2 = jnp.split(jax.nn.silu(c) @ w_ada + b_ada, 6, axis=-1)
    sh1c, sc1c, g1c, sh2c, sc2c, g2c = jnp.split((jax.nn.silu(c_ctx) @ w_ada + b_ada)[None], 6, axis=-1)
    lat = in_projection(flat(x), _mod_rows(norm1_g, 1.0 + sc1, sh1), w_in, T)
    cx = in_projection(flat(xc), _mod_rows(norm1_g, 1.0 + sc1c, sh1c), w_in, B * Tc)
    (na_q, na_k, na_v, ml_qk, ml_v, ml_o, ml_g,
     mla_cq, mla_ckv, mla_kr, sw_q, sw_k, sw_v) = [a.reshape(B, T, -1) for a in lat]
    (na_qc, na_kc, na_vc, ml_qkc, ml_vc, ml_oc, ml_gc,
     mla_cqc, mla_ckvc, mla_krc, sw_qc, sw_kc, sw_vc) = [a.reshape(B, Tc, -1) for a in cx]
    attn_scale = HEAD_DIM ** -0.5
    mla_scale = (MLA_NOPE + MLA_ROPE) ** -0.5
    kc_a, vc_a = heads(na_kc, H), heads(na_vc, H)
    y_a = neighbourhood_attention(na_q, na_k, na_v, na_kc, na_vc, na_rpb)
    h_lat, h_ctx = mlstm_mixer((ml_qk, ml_v, ml_g), (ml_qkc, ml_vc, ml_gc), ml_conv, ml_gate_b)
    q_m, k_m, v_m = mla_project(mla_cq, mla_ckv, mla_kr, mla_q_norm, mla_w_uq, mla_kv_norm, mla_w_ukv, angs_mla)
    qc_m, kc_m, vc_m = mla_project(mla_cqc, mla_ckvc, mla_krc, mla_q_norm, mla_w_uq, mla_kv_norm, mla_w_ukv, None)
    y_c = block_dense_attention(q_m, jnp.concatenate([kc_m, k_m], axis=1), jnp.concatenate([vc_m, v_m], axis=1), mla_scale)
    kc_d, vc_d = heads(sw_kc, SWA_KV_HEADS), heads(sw_vc, SWA_KV_HEADS)
    y_d = window_attention(rope_2d(heads(sw_q, H), angs_swa), rope_2d(heads(sw_k, SWA_KV_HEADS), angs_swa),
                           heads(sw_v, SWA_KV_HEADS), kc_d, vc_d, swa_sink)
    x2, h2 = out_projection(flat(y_a), h_lat.reshape(B * T, GROUP_WIDTH), flat(ml_o), flat(y_c), flat(y_d),
                            flat(x), _mod_rows(g1, norm2_g, 1.0 + sc2, sh2), w_out, T)
    u_packed, v_packed = pack_expert_table(peer_u), pack_expert_table(peer_v)
    x = peer_ffn(h2, x2, g2, T, peer_wq, peer_keys, u_packed, v_packed).reshape(B, T, D)
    if not need_ctx:
        return x, None
    xc2, h2c = out_projection(flat(ctx_attn(heads(na_qc, H), kc_a, vc_a, attn_scale)),
                              h_ctx.reshape(B * Tc, GROUP_WIDTH), flat(ml_oc),
                              flat(ctx_attn(qc_m, kc_m, vc_m, mla_scale)),
                              flat(ctx_attn(heads(sw_qc, H), kc_d, vc_d, attn_scale, swa_sink)),
                              flat(xc), _mod_rows(g1c, norm2_g, 1.0 + sc2c, sh2c), w_out, B * Tc)
    xc = peer_ffn(h2c, xc2, g2c, B * Tc, peer_wq, peer_keys, u_packed, v_packed).reshape(B, Tc, D)
    return x, xc


def _final_rmsnorm_kernel(x_ref, g_ref, o_ref):
    x = x_ref[...]
    o_ref[...] = x * lax.rsqrt(jnp.mean(x * x, axis=-1, keepdims=True) + EPS) * g_ref[...]


def final_rmsnorm(x, g):
    B, T, D = x.shape
    rows = 1024
    xf = x.reshape(B * T, D)
    out = pl.pallas_call(
        _final_rmsnorm_kernel,
        grid=(B * T // rows,),
        in_specs=[pl.BlockSpec((rows, D), lambda i: (i, 0)), pl.BlockSpec((1, D), lambda i: (0, 0))],
        out_specs=pl.BlockSpec((rows, D), lambda i: (i, 0)),
        out_shape=jax.ShapeDtypeStruct((B * T, D), x.dtype),
    )(xf, g.reshape(1, D))
    return out.reshape(B, T, D)


def kernel(x, c, ctx, c_ctx, norm1_g, norm2_g, w_ada, b_ada, w_in, na_rpb, ml_conv, ml_gate_b,
           mla_q_norm, mla_w_uq, mla_kv_norm, mla_w_ukv, swa_sink, w_out,
           peer_wq, peer_keys, peer_u, peer_v, final_norm_g):
    T = x.shape[1]
    angs_mla = axial_angles(T, MLA_ROPE)
    angs_swa = axial_angles(T, HEAD_DIM)
    xc = ctx
    for l in range(DEPTH):
        x, xc = hybrid_layer(x, xc, c, c_ctx, l < DEPTH - 1, angs_mla, angs_swa,
                             norm1_g[l], norm2_g[l], w_ada[l], b_ada[l], w_in[l], na_rpb[l],
                             ml_conv[l], ml_gate_b[l], mla_q_norm[l], mla_w_uq[l], mla_kv_norm[l],
                             mla_w_ukv[l], swa_sink[l], w_out[l], peer_wq[l], peer_keys[l],
                             peer_u[l], peer_v[l])
    return final_rmsnorm(x, final_norm_g)
```

```python
import functools

import jax
import jax.numpy as jnp
from jax import lax
import numpy as np
from jax.experimental import pallas as pl
from jax.experimental.pallas import tpu as pltpu

D_MODEL = 1024
BATCH = 2
SEQ = 16384
DEPTH = 2

CTX_LEN = 256
GRID_W = 64
N_MIXERS = 4
MIX_WIDTH = D_MODEL
GROUP_WIDTH = MIX_WIDTH // N_MIXERS
GROUP_HEADS = 4
HEAD_DIM = GROUP_WIDTH // GROUP_HEADS
NA_ROWS = 8
NA_COLS = 16
ML_CHUNK = 64
ML_CONV = 5
MLA_Q_RANK = 256
MLA_KV_RANK = 128
MLA_NOPE = 64
MLA_ROPE = 32
MLA_V = 64
SWA_KV_HEADS = 2
SWA_WINDOW = 128
ATTN_BLOCK = 128
PEER_HEADS = 8
PEER_NKEYS = 128
PEER_EXPERTS = PEER_NKEYS * PEER_NKEYS
PEER_DKEY = 128
PEER_TOPK = 16
PEER_BLOCK = 128
ROPE_BASE = 10000.0
EPS = 1e-6
IN_SIZES = (GROUP_WIDTH, GROUP_WIDTH, GROUP_WIDTH,
            2 * GROUP_WIDTH, GROUP_WIDTH, GROUP_WIDTH, 4 * GROUP_HEADS,
            MLA_Q_RANK, MLA_KV_RANK, MLA_ROPE,
            GROUP_WIDTH, SWA_KV_HEADS * HEAD_DIM, SWA_KV_HEADS * HEAD_DIM)
IN_WIDTH = sum(IN_SIZES)
F32 = jnp.float32


def heads(a, h):
    return a.reshape(a.shape[:-1] + (h, a.shape[-1] // h))


def axial_angles(T, rot_dim):
    t = jnp.arange(T)
    row = (t // GRID_W).astype(F32)
    col = (t % GRID_W).astype(F32)
    half = rot_dim // 2
    inv = 1.0 / (ROPE_BASE ** (jnp.arange(0, half, 2, dtype=F32) / half))
    return row[:, None] * inv, col[:, None] * inv


def rope_1d(x, ang):
    cos = jnp.cos(ang)[None, :, None, :]
    sin = jnp.sin(ang)[None, :, None, :]
    x1, x2 = jnp.split(x.astype(F32), 2, axis=-1)
    return jnp.concatenate([x1 * cos - x2 * sin, x1 * sin + x2 * cos], axis=-1)


def rope_2d(x, angs):
    xr, xc = jnp.split(x, 2, axis=-1)
    return jnp.concatenate([rope_1d(xr, angs[0]), rope_1d(xc, angs[1])], axis=-1).astype(x.dtype)


NT_DIMS = (((1,), (1,)), ((), ()))


def _ctx_attn_kernel(q_ref, k_ref, v_ref, sink_ref, o_ref, *, scale, use_sink):
    s = lax.dot_general(q_ref[0, 0].astype(BF16), k_ref[0, 0].astype(BF16), NT_DIMS,
                        preferred_element_type=F32) * scale
    m = jnp.max(s, axis=-1, keepdims=True)
    if use_sink:
        sink = sink_ref[pl.program_id(1)]
        m = jnp.maximum(m, sink)
    p = jnp.exp(s - m)
    l = jnp.sum(p, axis=-1, keepdims=True)
    if use_sink:
        l = l + jnp.exp(sink - m)
    o_ref[0, 0] = jnp.dot(p.astype(BF16), v_ref[0, 0].astype(BF16), preferred_element_type=F32) / l


def ctx_attn(q, k, v, scale, sink=None):
    B, Tc, H, _ = q.shape
    rep = H // k.shape[2]
    hm = lambda a: jnp.swapaxes(a, 1, 2)
    q, k, v = hm(q), hm(jnp.repeat(k, rep, axis=2)), hm(jnp.repeat(v, rep, axis=2))
    blk = lambda a: pl.BlockSpec((1, 1, Tc, a.shape[-1]), lambda b, h: (b, h, 0, 0))
    out = pl.pallas_call(
        functools.partial(_ctx_attn_kernel, scale=scale, use_sink=sink is not None),
        grid=(B, H),
        in_specs=[blk(q), blk(k), blk(v), pl.BlockSpec(memory_space=pltpu.SMEM)],
        out_specs=blk(v),
        out_shape=jax.ShapeDtypeStruct(v.shape, F32),
    )(q, k, v, jnp.zeros((H,), F32) if sink is None else sink.astype(F32))
    return jnp.swapaxes(out, 1, 2).reshape(B, Tc, -1)
NA_SPAN = NA_ROWS * GRID_W


def _head_mask(width):
    rows = lax.broadcasted_iota(jnp.int32, (GROUP_HEADS * width, GROUP_WIDTH), 0) // width
    cols = lax.broadcasted_iota(jnp.int32, (GROUP_HEADS * width, GROUP_WIDTH), 1) // HEAD_DIM
    return (rows == cols).astype(F32)


def _na_kernel(q_ref, k_ref, v_ref, kc_ref, vc_ref, bias_ref, o_ref):
    r = pl.program_id(1)
    rows = pl.num_programs(1)
    rs = jnp.clip(r - NA_ROWS // 2, 0, rows - NA_ROWS)
    start = pl.multiple_of(rs * GRID_W, GRID_W)
    kw = k_ref[0, pl.ds(start, NA_SPAN), :]
    vw = v_ref[0, pl.ds(start, NA_SPAN), :]
    hm = _head_mask(GRID_W)
    q = q_ref[0] * (HEAD_DIM ** -0.5)
    q4 = (jnp.concatenate([q] * GROUP_HEADS, axis=0) * hm).astype(BF16)
    s_loc = lax.dot_general(q4, kw, NT_DIMS, preferred_element_type=F32) + bias_ref[rs - r + NA_ROWS - 1]
    s_ctx = lax.dot_general(q4, kc_ref[0], NT_DIMS, preferred_element_type=F32)
    m = jnp.maximum(jnp.max(s_loc, axis=-1, keepdims=True), jnp.max(s_ctx, axis=-1, keepdims=True))
    p_loc = jnp.exp(s_loc - m)
    p_ctx = jnp.exp(s_ctx - m)
    l = jnp.sum(p_loc, axis=-1, keepdims=True) + jnp.sum(p_ctx, axis=-1, keepdims=True)
    o = (jnp.dot(p_loc.astype(BF16), vw, preferred_element_type=F32)
         + jnp.dot(p_ctx.astype(BF16), vc_ref[0], preferred_element_type=F32)) * (hm / l)
    o_ref[0] = sum(o[h * GRID_W:(h + 1) * GRID_W] for h in range(GROUP_HEADS))


def _na_bias_table(rpb):
    c = np.arange(GRID_W)
    col_start = np.clip(c - NA_COLS // 2, 0, GRID_W - NA_COLS)
    valid = (c[None, :] >= col_start[:, None]) & (c[None, :] < col_start[:, None] + NA_COLS)
    dc = np.clip(c[None, :] - c[:, None] + NA_COLS - 1, 0, 2 * NA_COLS - 2)
    dr = np.arange(NA_ROWS)[:, None] + np.arange(NA_ROWS)[None, :]
    t = rpb.astype(F32)[:, dr][..., dc]
    t = jnp.where(valid[None, None, None], t, -jnp.inf)
    return jnp.transpose(t, (1, 0, 3, 2, 4)).reshape(NA_ROWS, GROUP_HEADS * GRID_W, NA_SPAN)


def neighbourhood_attention(q, k, v, kc, vc, rpb):
    B, T, C = q.shape
    rows = T // GRID_W
    n_ctx = kc.shape[1]
    bias = _na_bias_table(rpb)
    full = lambda n: pl.BlockSpec((1, n, C), lambda b, r: (b, 0, 0))
    return pl.pallas_call(
        _na_kernel,
        grid=(B, rows),
        in_specs=[pl.BlockSpec((1, GRID_W, C), lambda b, r: (b, r, 0)),
                  full(T), full(T), full(n_ctx), full(n_ctx),
                  pl.BlockSpec(bias.shape, lambda b, r: (0, 0, 0))],
        out_specs=pl.BlockSpec((1, GRID_W, C), lambda b, r: (b, r, 0)),
        out_shape=jax.ShapeDtypeStruct((B, T, C), F32),
        compiler_params=pltpu.CompilerParams(vmem_limit_bytes=VMEM_LIMIT_BYTES),
    )(q, k.astype(BF16), v.astype(BF16), kc.astype(BF16), vc.astype(BF16), bias)


def short_conv(a, w):
    T = a.shape[1]
    pad = w.shape[0] // 2
    ap = jnp.pad(a, ((0, 0), (pad, pad), (0, 0)))
    out = ap[:, :T] * w[0]
    for j in range(1, w.shape[0]):
        out = out + ap[:, j:j + T] * w[j]
    return out


ML_CHUNKS_PER_STEP = CTX_LEN // ML_CHUNK


def _bmm(a, b, contract):
    return lax.dot_general(a.astype(BF16), b.astype(BF16), (contract, ((0,), (0,))),
                           preferred_element_type=F32)


def _mlstm_chunk(qt, kt, vt, irow, brow, state, backward):
    L = ML_CHUNK
    C, nrow, m = state
    row = lax.broadcasted_iota(jnp.int32, (1, L, L), 1)
    col = lax.broadcasted_iota(jnp.int32, (1, L, L), 2)
    seen = (row <= col) if backward else (row >= col)
    eye = row == col

    def as_col(r):
        return jnp.sum(jnp.where(eye, r, 0.0), axis=2, keepdims=True)

    blast = brow[:, :, 0:1] if backward else brow[:, :, L - 1:L]
    rrow = brow - irow
    bcol = as_col(brow)
    d_log = jnp.where(seen, bcol - rrow, -jnp.inf)
    inter = bcol + m
    m_t = jnp.maximum(inter, jnp.max(d_log, axis=2, keepdims=True))
    w = jnp.exp(d_log - m_t)
    a = jnp.exp(inter - m_t)
    s = _bmm(qt, kt, ((2,), (2,))) * w
    num = _bmm(s, vt, ((2,), (1,))) + a * _bmm(qt, C, ((2,), (1,)))
    den = jnp.sum(s, axis=2, keepdims=True) + a * jnp.sum(qt * nrow, axis=2, keepdims=True)
    h = num / jnp.maximum(jnp.abs(den), jnp.exp(-m_t))
    g = blast - rrow
    m_new = jnp.maximum(blast + m, jnp.max(g, axis=2, keepdims=True))
    kw = kt * as_col(jnp.exp(g - m_new))
    decay = jnp.exp(blast + m - m_new)
    C = decay * C + _bmm(jnp.swapaxes(kw, 1, 2), vt, ((2,), (1,)))
    nrow = decay * nrow + jnp.sum(kw, axis=1, keepdims=True)
    return h, (C, nrow, m_new)


def _mlstm_kernel(qf_ref, kf_ref, vf_ref, if_ref, bf_ref, qb_ref, kb_ref, vb_ref, ib_ref, bb_ref,
                  hf_ref, hb_ref, c_ref, n_ref, m_ref):
    N, L = qf_ref.shape[0], ML_CHUNK

    @pl.when(pl.program_id(0) == 0)
    def _():
        c_ref[...] = jnp.zeros(c_ref.shape, F32)
        n_ref[...] = jnp.zeros(n_ref.shape, F32)
        m_ref[...] = jnp.zeros(m_ref.shape, F32)

    fwd = (c_ref[:N], n_ref[:N], m_ref[:N])
    bwd = (c_ref[N:], n_ref[N:], m_ref[N:])
    for c in range(ML_CHUNKS_PER_STEP):
        rows = slice(c * L, (c + 1) * L)
        h, fwd = _mlstm_chunk(qf_ref[:, rows, :], kf_ref[:, rows, :], vf_ref[:, rows, :],
                              if_ref[:, 0, c:c + 1, :], bf_ref[:, 0, c:c + 1, :], fwd, False)
        hf_ref[:, rows, :] = h
        cb = ML_CHUNKS_PER_STEP - 1 - c
        rows = slice(cb * L, (cb + 1) * L)
        h, bwd = _mlstm_chunk(qb_ref[:, rows, :], kb_ref[:, rows, :], vb_ref[:, rows, :],
                              ib_ref[:, 0, cb:cb + 1, :], bb_ref[:, 0, cb:cb + 1, :], bwd, True)
        hb_ref[:, rows, :] = h
    for i, ref in enumerate((c_ref, n_ref, m_ref)):
        ref[:N] = fwd[i]
        ref[N:] = bwd[i]


def mlstm_scan(q, k, v, gates_f, gates_b, n_ctx):
    B, T, H, d = q.shape
    CB, L = ML_CHUNKS_PER_STEP, ML_CHUNK
    assert n_ctx == CB * L and T % (CB * L) == 0
    N, steps = B * H, T // (CB * L)
    hm = lambda a: jnp.swapaxes(a, 1, 2).reshape(N, T, d)
    gates = lambda a: jnp.swapaxes(a, 1, 2).reshape(N, steps, CB, L)
    chunked = lambda a: a.reshape(B, T // L, L, H)
    b_f = jnp.cumsum(chunked(gates_f[1]), axis=2).reshape(B, T, H)
    b_b = lax.cumsum(chunked(gates_b[1]), axis=2, reverse=True).reshape(B, T, H)
    back = lambda j: jnp.where(j == 0, 0, steps - j)
    seq_f = pl.BlockSpec((N, CB * L, d), lambda j: (0, j, 0))
    seq_b = pl.BlockSpec((N, CB * L, d), lambda j: (0, back(j), 0))
    gate_f = pl.BlockSpec((N, 1, CB, L), lambda j: (0, j, 0, 0))
    gate_b = pl.BlockSpec((N, 1, CB, L), lambda j: (0, back(j), 0, 0))
    qh, kh, vh = hm(q), hm(k), hm(v)
    hf, hb = pl.pallas_call(
        _mlstm_kernel,
        grid=(steps,),
        in_specs=[seq_f, seq_f, seq_f, gate_f, gate_f, seq_b, seq_b, seq_b, gate_b, gate_b],
        out_specs=[seq_f, seq_b],
        out_shape=[jax.ShapeDtypeStruct((N, T, d), F32)] * 2,
        scratch_shapes=[pltpu.VMEM((2 * N, d, d), F32), pltpu.VMEM((2 * N, 1, d), F32),
                        pltpu.VMEM((2 * N, 1, 1), F32)],
        compiler_params=pltpu.CompilerParams(vmem_limit_bytes=VMEM_LIMIT_BYTES),
    )(qh, kh, vh, gates(gates_f[0]), gates(b_f), qh, kh, vh, gates(gates_b[0]), gates(b_b))
    return jnp.swapaxes((hf + hb).reshape(B, H, T, d), 1, 2)


def mlstm_prep(qk, v, gates, conv_w, gate_b):
    qk = jax.nn.silu(short_conv(qk, conv_w))
    q, k = jnp.split(qk, 2, axis=-1)
    g = (gates + gate_b).astype(F32)
    i_f, f_f, i_b, f_b = jnp.split(g, 4, axis=-1)
    return (heads(q, GROUP_HEADS) * HEAD_DIM ** -0.5, heads(k, GROUP_HEADS), heads(v, GROUP_HEADS),
            (i_f, jax.nn.log_sigmoid(f_f), i_b, jax.nn.log_sigmoid(f_b)))


def mlstm_mixer(lat, ctx, conv_w, gate_b):
    ql, kl, vl, gl = mlstm_prep(lat[0], lat[1], lat[2], conv_w, gate_b)
    qc, kc, vc, gc = mlstm_prep(ctx[0], ctx[1], ctx[2], conv_w, gate_b)
    Tc = qc.shape[1]
    cat = lambda c_, l_: jnp.concatenate([c_, l_], axis=1)
    h = mlstm_scan(cat(qc, ql), cat(kc, kl), cat(vc, vl),
                   (cat(gc[0], gl[0]), cat(gc[1], gl[1])), (cat(gc[2], gl[2]), cat(gc[3], gl[3])), Tc)
    return h[:, Tc:], h[:, :Tc]


def _mla_up_kernel(cq_ref, ckv_ref, qn_ref, kvn_ref, wq_ref, wkv_ref, q_ref, kv_ref):
    def up(x, g, w_ref):
        y = x * lax.rsqrt(jnp.mean(x * x, axis=-1, keepdims=True) + EPS) * g
        return jnp.dot(y.astype(BF16), w_ref[...], preferred_element_type=F32)

    q_ref[...] = up(cq_ref[...], qn_ref[...], wq_ref)
    kv_ref[...] = up(ckv_ref[...], kvn_ref[...], wkv_ref)


def mla_up(cq, ckv, q_norm, w_uq, kv_norm, w_ukv):
    B, T = cq.shape[:2]
    N = B * T
    tm = min(PROJ_TOKENS, N)
    tok = lambda w_: pl.BlockSpec((tm, w_), lambda i: (i, 0))
    const = lambda a: pl.BlockSpec(a.shape, lambda i: (0, 0))
    args = (cq.reshape(N, -1), ckv.reshape(N, -1), q_norm[None], kv_norm[None],
            w_uq.astype(BF16), w_ukv.astype(BF16))
    q, kv = pl.pallas_call(
        _mla_up_kernel,
        grid=(N // tm,),
        in_specs=[tok(args[0].shape[1]), tok(args[1].shape[1])] + [const(a) for a in args[2:]],
        out_specs=[tok(w_uq.shape[1]), tok(w_ukv.shape[1])],
        out_shape=[jax.ShapeDtypeStruct((N, w_uq.shape[1]), F32), jax.ShapeDtypeStruct((N, w_ukv.shape[1]), F32)],
    )(*args)
    return q.reshape(B, T, -1), kv.reshape(B, T, -1)


def mla_project(cq, ckv, kr, q_norm, w_uq, kv_norm, w_ukv, angs):
    q, kv = mla_up(cq, ckv, q_norm, w_uq, kv_norm, w_ukv)
    q, kv = heads(q, GROUP_HEADS), heads(kv, GROUP_HEADS)
    q_nope, q_rope = q[..., :MLA_NOPE], q[..., MLA_NOPE:]
    k_nope, v = kv[..., :MLA_NOPE], kv[..., MLA_NOPE:]
    k_rope = kr[:, :, None, :]
    if angs is not None:
        q_rope = rope_2d(q_rope, angs)
        k_rope = rope_2d(k_rope, angs)
    k_rope = jnp.broadcast_to(k_rope, k_nope.shape[:-1] + (MLA_ROPE,))
    return (jnp.concatenate([q_nope, q_rope], axis=-1), jnp.concatenate([k_nope, k_rope], axis=-1), v)


LOG2_E = 1.4426950408889634
DENSE_Q_TILE = 1024
DENSE_Q_SUB = 256
DENSE_Q_UNROLL = 4
DENSE_K_TILE_MAX = 8320


def _dense_attn_kernel(q_ref, k_ref, v_ref, o_ref, m_ref, l_ref, acc_ref, *, scale, dv):
    h, j = pl.program_id(2), pl.program_id(3)

    @pl.when(j == 0)
    def _():
        m_ref[...] = jnp.full(m_ref.shape, -jnp.inf, F32)
        l_ref[...] = jnp.zeros(l_ref.shape, F32)
        acc_ref[...] = jnp.zeros(acc_ref.shape, F32)

    def rows(i, carry):
        for u in range(DENSE_Q_UNROLL):
            r = pl.ds(pl.multiple_of((i * DENSE_Q_UNROLL + u) * DENSE_Q_SUB, DENSE_Q_SUB), DENSE_Q_SUB)
            s = lax.dot_general(q_ref[0, r, :], k_ref[0], NT_DIMS,
                                preferred_element_type=F32) * (scale * LOG2_E)
            m_prev = m_ref[r, :]
            m_new = jnp.maximum(m_prev, jnp.max(s, axis=-1, keepdims=True))
            alpha = jnp.exp2(m_prev - m_new)
            p = jnp.exp2(s - m_new)
            l_ref[r, :] = alpha * l_ref[r, :] + jnp.sum(p, axis=-1, keepdims=True)
            acc_ref[r, :] = alpha * acc_ref[r, :] + jnp.dot(p.astype(BF16), v_ref[0],
                                                            preferred_element_type=F32)
            m_ref[r, :] = m_new
        return carry

    lax.fori_loop(0, q_ref.shape[1] // (DENSE_Q_SUB * DENSE_Q_UNROLL), rows, 0)

    for hh in range(o_ref.shape[2] // dv):
        @pl.when((j == pl.num_programs(3) - 1) & (h == hh))
        def _():
            o_ref[0, :, hh * dv:(hh + 1) * dv] = (acc_ref[...] / l_ref[...])[:, :dv]


def block_dense_attention(q, k_all, v_all, scale):
    B, T, H, dq = q.shape
    NK, dv = k_all.shape[1], v_all.shape[-1]
    tq = min(DENSE_Q_TILE, T)
    tk = max(t for t in range(LANES, DENSE_K_TILE_MAX + 1, LANES) if NK % t == 0)

    def lanes(a):
        a = jnp.pad(a.astype(BF16), ((0, 0), (0, 0), (0, 0), (0, LANES - a.shape[-1])))
        return a.reshape(a.shape[0], a.shape[1], H * LANES)

    return pl.pallas_call(
        functools.partial(_dense_attn_kernel, scale=scale, dv=dv),
        grid=(B, T // tq, H, NK // tk),
        in_specs=[pl.BlockSpec((1, tq, LANES), lambda b, i, h, j: (b, i, h)),
                  pl.BlockSpec((1, tk, LANES), lambda b, i, h, j: (b, j, h)),
                  pl.BlockSpec((1, tk, LANES), lambda b, i, h, j: (b, j, h))],
        out_specs=pl.BlockSpec((1, tq, H * dv), lambda b, i, h, j: (b, i, 0)),
        out_shape=jax.ShapeDtypeStruct((B, T, H * dv), F32),
        scratch_shapes=[pltpu.VMEM((tq, 1), F32), pltpu.VMEM((tq, 1), F32), pltpu.VMEM((tq, LANES), F32)],
        compiler_params=pltpu.CompilerParams(vmem_limit_bytes=VMEM_LIMIT_BYTES),
    )(lanes(q), lanes(k_all), lanes(v_all))


SWA_SPAN = ATTN_BLOCK + 2 * SWA_WINDOW


def _swa_kernel(q_ref, k_ref, v_ref, kc_ref, vc_ref, sink_ref, o_ref):
    n = pl.program_id(1)
    T = k_ref.shape[1]
    start = pl.multiple_of(jnp.clip(n * ATTN_BLOCK - SWA_WINDOW, 0, T - SWA_SPAN), ATTN_BLOCK)
    kw = k_ref[0, pl.ds(start, SWA_SPAN), :]
    vw = v_ref[0, pl.ds(start, SWA_SPAN), :]
    hm = _head_mask(ATTN_BLOCK)
    q = q_ref[0] * (HEAD_DIM ** -0.5)
    q4 = (jnp.concatenate([q] * GROUP_HEADS, axis=0) * hm).astype(BF16)
    rows = GROUP_HEADS * ATTN_BLOCK
    q_pos = n * ATTN_BLOCK + lax.broadcasted_iota(jnp.int32, (rows, SWA_SPAN), 0) % ATTN_BLOCK
    k_pos = start + lax.broadcasted_iota(jnp.int32, (rows, SWA_SPAN), 1)
    s_loc = lax.dot_general(q4, kw, NT_DIMS, preferred_element_type=F32)
    s_loc = jnp.where(jnp.abs(q_pos - k_pos) <= SWA_WINDOW, s_loc, -jnp.inf)
    s_ctx = lax.dot_general(q4, kc_ref[0], NT_DIMS, preferred_element_type=F32)
    sink = sink_ref[...]
    m = jnp.maximum(jnp.maximum(jnp.max(s_loc, axis=-1, keepdims=True),
                                jnp.max(s_ctx, axis=-1, keepdims=True)), sink)
    p_loc = jnp.exp(s_loc - m)
    p_ctx = jnp.exp(s_ctx - m)
    l = jnp.sum(p_loc, axis=-1, keepdims=True) + jnp.sum(p_ctx, axis=-1, keepdims=True) + jnp.exp(sink - m)
    o = (jnp.dot(p_loc.astype(BF16), vw, preferred_element_type=F32)
         + jnp.dot(p_ctx.astype(BF16), vc_ref[0], preferred_element_type=F32)) * (hm / l)
    o_ref[0] = sum(o[h * ATTN_BLOCK:(h + 1) * ATTN_BLOCK] for h in range(GROUP_HEADS))


def window_attention(q, k, v, kc, vc, sink):
    B, T, H, d = q.shape
    G = H // k.shape[2]
    n_ctx = kc.shape[1]
    C = H * d
    rep = lambda a: jnp.repeat(a, G, axis=2).reshape(a.shape[0], a.shape[1], C).astype(BF16)
    sink_rows = jnp.repeat(sink.astype(F32), ATTN_BLOCK).reshape(H * ATTN_BLOCK, 1)
    full = lambda n: pl.BlockSpec((1, n, C), lambda b, i: (b, 0, 0))
    return pl.pallas_call(
        _swa_kernel,
        grid=(B, T // ATTN_BLOCK),
        in_specs=[pl.BlockSpec((1, ATTN_BLOCK, C), lambda b, i: (b, i, 0)),
                  full(T), full(T), full(n_ctx), full(n_ctx),
                  pl.BlockSpec(sink_rows.shape, lambda b, i: (0, 0))],
        out_specs=pl.BlockSpec((1, ATTN_BLOCK, C), lambda b, i: (b, i, 0)),
        out_shape=jax.ShapeDtypeStruct((B, T, C), F32),
        compiler_params=pltpu.CompilerParams(vmem_limit_bytes=VMEM_LIMIT_BYTES),
    )(q.reshape(B, T, C), rep(k), rep(v), rep(kc), rep(vc), sink_rows)


BF16 = jnp.bfloat16
LANES = 128
SUBLANES = 8
ROW_SEGS = D_MODEL // LANES
ROW_WORDS = ROW_SEGS // 2
PEER_PICKS = PEER_HEADS * PEER_TOPK
PEER_TOPK_TOKENS = 512
PEER_GATHER_TOKENS = 128
PEER_ACT_UNROLL = SUBLANES
VMEM_LIMIT_BYTES = 56 * 1024 * 1024


def _split_bf16(x, parts):
    out = []
    for _ in range(parts):
        p = x.astype(BF16)
        out.append(p)
        x = x - p.astype(F32)
    return out


def _topk_rows(s, k):
    n = s.shape[0]
    iota = lax.broadcasted_iota(jnp.int32, s.shape, 0)
    vals, idxs = [], []
    for _ in range(k):
        m = jnp.max(s, axis=0, keepdims=True)
        i = jnp.min(jnp.where(s == m, iota, n), axis=0, keepdims=True)
        vals.append(m)
        idxs.append(i)
        s = jnp.where(iota == i, -jnp.inf, s)
    return jnp.concatenate(vals, axis=0), jnp.concatenate(idxs, axis=0)


def _peer_topk_kernel(x_ref, wq_ref, keys_ref, eidx_ref, gate_ref):
    xb = x_ref[...].astype(BF16)
    q = jnp.dot(xb, wq_ref[...], preferred_element_type=F32)
    nt = (((1,), (1,)), ((), ()))
    sv, si = [], []
    for p in range(2):
        qp = q[:, p * PEER_DKEY:(p + 1) * PEER_DKEY].astype(BF16)
        s = lax.dot_general(keys_ref[0, p], qp, nt, preferred_element_type=F32)
        v_, i_ = _topk_rows(s, PEER_TOPK)
        sv.append(v_)
        si.append(i_)
    cs, ce = [], []
    half = PEER_TOPK // 2
    for a in range(half):
        nb = PEER_TOPK if a == 0 else half
        cs.append(sv[0][a:a + 1] + sv[1][:nb])
        ce.append(si[0][a:a + 1] * PEER_NKEYS + si[1][:nb])
    cs.append(sv[0][half:] + sv[1][0:1])
    ce.append(si[0][half:] * PEER_NKEYS + si[1][0:1])
    cand_s = jnp.concatenate(cs, axis=0)
    cand_e = jnp.concatenate(ce, axis=0)
    fs, fpos = _topk_rows(cand_s, PEER_TOPK)
    iota = lax.broadcasted_iota(jnp.int32, cand_e.shape, 0)
    eidx = [jnp.max(jnp.where(iota == fpos[j:j + 1], cand_e, -1), axis=0, keepdims=True)
            for j in range(PEER_TOPK)]
    ex = jnp.exp(fs - fs[0:1])
    eidx_ref[0] = jnp.concatenate(eidx, axis=0)
    gate_ref[0] = ex / jnp.sum(ex, axis=0, keepdims=True)


def peer_topk(h, wq, sub_keys):
    N, D = h.shape
    T = PEER_TOPK_TOKENS
    wqb = wq.astype(BF16)
    kb = sub_keys.astype(BF16)
    eidx, gate = pl.pallas_call(
        _peer_topk_kernel,
        grid=(N // T, PEER_HEADS),
        in_specs=[pl.BlockSpec((T, D), lambda i, h_: (i, 0)),
                  pl.BlockSpec((D, 2 * PEER_DKEY), lambda i, h_: (0, h_)),
                  pl.BlockSpec((1, 2, PEER_NKEYS, PEER_DKEY), lambda i, h_: (h_, 0, 0, 0))],
        out_specs=[pl.BlockSpec((1, PEER_TOPK, T), lambda i, h_: (h_, 0, i)),
                   pl.BlockSpec((1, PEER_TOPK, T), lambda i, h_: (h_, 0, i))],
        out_shape=[jax.ShapeDtypeStruct((PEER_HEADS, PEER_TOPK, N), jnp.int32),
                   jax.ShapeDtypeStruct((PEER_HEADS, PEER_TOPK, N), F32)],
        compiler_params=pltpu.CompilerParams(vmem_limit_bytes=VMEM_LIMIT_BYTES),
    )(h, wqb, kb)
    return eidx.reshape(PEER_PICKS, N), gate.reshape(PEER_PICKS, N)


def pack_expert_table(tab):
    E = tab.shape[0]
    t = tab.astype(BF16).reshape(E, ROW_WORDS, 2, LANES)
    t = jnp.swapaxes(t, -1, -2)
    return lax.bitcast_convert_type(t, jnp.uint32).reshape(E * ROW_WORDS, LANES)


def _stage_rows(idx_ref, tab_ref, stage_ref, t):
    for k in range(PEER_PICKS):
        off = pl.multiple_of(idx_ref[t, k], ROW_WORDS)
        stage_ref[k * ROW_WORDS:(k + 1) * ROW_WORDS, :] = tab_ref[pl.ds(off, ROW_WORDS), :]
    return pltpu.bitcast(stage_ref[...], BF16)


def _peer_act_kernel(idx_ref, x_ref, gate_ref, tab_ref, seg_mask_ref, group_ref, w_ref,
                     stage_ref, rows_ref):
    T = x_ref.shape[0]
    U = stage_ref.shape[0]

    sub = lax.broadcasted_iota(jnp.int32, (SUBLANES, PEER_PICKS * ROW_SEGS), 0)

    def tokens(g, carry):
        tile = jnp.zeros((SUBLANES, PEER_PICKS * ROW_SEGS), F32)
        for j in range(U):
            t = g * U + j
            sb = _stage_rows(idx_ref, tab_ref, stage_ref.at[j], t)
            xs = jnp.concatenate(_split_bf16(x_ref[t], 2), axis=0)
            r = lax.dot_general(xs, sb, NT_DIMS, preferred_element_type=F32)
            r = jnp.sum(r * seg_mask_ref[...], axis=0, keepdims=True)
            tile = jnp.where(sub == j, r, tile)
        rows_ref[g] = tile
        return carry

    lax.fori_loop(0, T // U, tokens, 0)
    rows = rows_ref[...].reshape(T, PEER_PICKS * ROW_SEGS)
    act = jnp.zeros((T, PEER_PICKS), F32)
    for piece in _split_bf16(rows, 3):
        act = act + jnp.dot(piece, group_ref[...], preferred_element_type=F32)
    w_ref[...] = gate_ref[...] * (0.5 * act * (1.0 + lax.erf(act * (2.0 ** -0.5))))


def _peer_out_kernel(idx_ref, w_ref, x_ref, g_ref, tab_ref, expand_ref, seg_mask_ref, f_ref, stage_ref):
    T = w_ref.shape[0]
    U = stage_ref.shape[0]

    def tokens(g, carry):
        w8 = w_ref[pl.ds(pl.multiple_of(g * U, U), U), :]
        hi, lo = _split_bf16(w8, 2)
        lhs = jnp.concatenate([jnp.broadcast_to(p[j:j + 1], (SUBLANES, PEER_PICKS))
                               for j in range(U) for p in (hi, lo)], axis=0)
        wrep = jnp.dot(lhs, expand_ref[...], preferred_element_type=F32)
        for j in range(U):
            t = g * U + j
            sb = _stage_rows(idx_ref, tab_ref, stage_ref.at[j], t)
            wsel = (wrep[j * 2 * SUBLANES:(j + 1) * 2 * SUBLANES] * seg_mask_ref[...]).astype(BF16)
            o = jnp.dot(wsel, sb, preferred_element_type=F32)
            f_ref[t] = x_ref[t] + g_ref[0] * (o[:SUBLANES] + o[SUBLANES:])
        return carry

    lax.fori_loop(0, T // U, tokens, 0)


def _peer_constants():
    cols = np.arange(PEER_PICKS * ROW_SEGS)
    seg_mask = (cols[None, :] % ROW_SEGS == np.arange(2 * SUBLANES)[:, None] % SUBLANES)
    group = (cols[:, None] // ROW_SEGS == np.arange(PEER_PICKS)[None, :])
    return (jnp.asarray(seg_mask, F32), jnp.asarray(group, BF16), jnp.asarray(group.T, BF16))


def peer_ffn(h, x, gate2, group_tokens, wq, sub_keys, u_packed, v_packed):
    N, D = h.shape
    T = PEER_GATHER_TOKENS
    eidx, gate = peer_topk(h, wq, sub_keys)
    seg_mask, group, expand = _peer_constants()
    rows3 = lambda a: a.reshape(a.shape[0], ROW_SEGS, LANES)
    offs = eidx.T * ROW_WORDS
    idx_spec = pl.BlockSpec((T, PEER_PICKS), lambda i: (i, 0), memory_space=pltpu.SMEM)
    tab_spec = pl.BlockSpec(u_packed.shape, lambda i: (0, 0), pipeline_mode=pl.Buffered(1))
    tok_spec = pl.BlockSpec((T, ROW_SEGS, LANES), lambda i: (i, 0, 0))
    const = lambda shape: pl.BlockSpec(shape, lambda i: (0, 0))
    params = pltpu.CompilerParams(vmem_limit_bytes=VMEM_LIMIT_BYTES)
    w = pl.pallas_call(
        _peer_act_kernel,
        grid=(N // T,),
        in_specs=[idx_spec, tok_spec,
                  pl.BlockSpec((T, PEER_PICKS), lambda i: (i, 0)),
                  tab_spec, const(seg_mask.shape), const(group.shape)],
        out_specs=pl.BlockSpec((T, PEER_PICKS), lambda i: (i, 0)),
        out_shape=jax.ShapeDtypeStruct((N, PEER_PICKS), F32),
        scratch_shapes=[pltpu.VMEM((PEER_ACT_UNROLL, PEER_PICKS * ROW_WORDS, LANES), jnp.uint32),
                        pltpu.VMEM((T // SUBLANES, SUBLANES, PEER_PICKS * ROW_SEGS), F32)],
        compiler_params=params,
    )(offs, rows3(h), gate.T, u_packed, seg_mask, group)
    out = pl.pallas_call(
        _peer_out_kernel,
        grid=(N // T,),
        in_specs=[idx_spec,
                  pl.BlockSpec((T, PEER_PICKS), lambda i: (i, 0)),
                  tok_spec,
                  pl.BlockSpec((1, ROW_SEGS, LANES), lambda i: (i // (group_tokens // T), 0, 0)),
                  tab_spec, const(expand.shape), const(seg_mask.shape)],
        out_specs=tok_spec,
        out_shape=jax.ShapeDtypeStruct((N, ROW_SEGS, LANES), F32),
        scratch_shapes=[pltpu.VMEM((SUBLANES, PEER_PICKS * ROW_WORDS, LANES), jnp.uint32)],
        compiler_params=params,
    )(offs, w, rows3(x), rows3(gate2), v_packed, expand, seg_mask)
    return out.reshape(N, D)


PROJ_TOKENS = 512
MOD_ROWS = SUBLANES
IN_ALIGNED = tuple(i for i, s_ in enumerate(IN_SIZES) if s_ % LANES == 0)
IN_SMALL = tuple(i for i, s_ in enumerate(IN_SIZES) if s_ % LANES)


def _rms_modulate(x, gain, scale1p, shift):
    r = lax.rsqrt(jnp.mean(x * x, axis=-1, keepdims=True) + EPS)
    return (x * r * gain) * scale1p + shift


def _in_proj_kernel(x_ref, mod_ref, w_ref, *out_refs):
    mod = mod_ref[0]
    h = _rms_modulate(x_ref[...], mod[0:1], mod[1:2], mod[2:3])
    y = jnp.dot(h.astype(BF16), w_ref[...], preferred_element_type=F32)
    off = 0
    for o_ref in out_refs:
        o_ref[...] = y[:, off:off + o_ref.shape[1]].astype(o_ref.dtype)
        off += o_ref.shape[1]


def in_projection(x, mod, w_in, group_tokens):
    N, D = x.shape
    T = min(PROJ_TOKENS, group_tokens)
    starts = np.cumsum((0,) + IN_SIZES)
    cols = np.concatenate([np.arange(starts[i], starts[i + 1]) for i in IN_ALIGNED + IN_SMALL])
    small = sum(IN_SIZES[i] for i in IN_SMALL)
    wp = jnp.pad(w_in[:, cols], ((0, 0), (0, -small % LANES))).astype(BF16)
    widths = [IN_SIZES[i] for i in IN_ALIGNED] + [small + (-small % LANES)]
    outs = pl.pallas_call(
        _in_proj_kernel,
        grid=(N // T,),
        in_specs=[pl.BlockSpec((T, D), lambda i: (i, 0)),
                  pl.BlockSpec((1, MOD_ROWS, D), lambda i: (i // (group_tokens // T), 0, 0)),
                  pl.BlockSpec(wp.shape, lambda i: (0, 0))],
        out_specs=[pl.BlockSpec((T, w_), lambda i: (i, 0)) for w_ in widths],
        out_shape=[jax.ShapeDtypeStruct((N, w_), F32) for w_ in widths],
        compiler_params=pltpu.CompilerParams(vmem_limit_bytes=VMEM_LIMIT_BYTES),
    )(x, mod, wp)
    groups = dict(zip(IN_ALIGNED, outs[:-1]))
    off = 0
    for i in IN_SMALL:
        groups[i] = outs[-1][:, off:off + IN_SIZES[i]]
        off += IN_SIZES[i]
    return [groups[i] for i in range(len(IN_SIZES))]


def _out_proj_kernel(ya_ref, hl_ref, mo_ref, yc_ref, yd_ref, x_ref, mod_ref, w_ref, xo_ref, h2_ref):
    yb = hl_ref[...] * jax.nn.sigmoid(mo_ref[...])
    y = jnp.concatenate([ya_ref[...], yb, yc_ref[...], yd_ref[...]], axis=-1).astype(BF16)
    mod = mod_ref[0]
    xn = x_ref[...] + mod[0:1] * jnp.dot(y, w_ref[...], preferred_element_type=F32)
    xo_ref[...] = xn
    h2_ref[...] = _rms_modulate(xn, mod[1:2], mod[2:3], mod[3:4])


def out_projection(ya, hl, mo, yc, yd, x, mod, w_out, group_tokens):
    N, D = x.shape
    T = min(PROJ_TOKENS, group_tokens)
    part = pl.BlockSpec((T, GROUP_WIDTH), lambda i: (i, 0))
    tok = pl.BlockSpec((T, D), lambda i: (i, 0))
    return pl.pallas_call(
        _out_proj_kernel,
        grid=(N // T,),
        in_specs=[part, part, part, part, part, tok,
                  pl.BlockSpec((1, MOD_ROWS, D), lambda i: (i // (group_tokens // T), 0, 0)),
                  pl.BlockSpec(w_out.shape, lambda i: (0, 0))],
        out_specs=[tok, tok],
        out_shape=[jax.ShapeDtypeStruct((N, D), F32)] * 2,
        compiler_params=pltpu.CompilerParams(vmem_limit_bytes=VMEM_LIMIT_BYTES),
    )(ya, hl, mo, yc, yd, x, mod, w_out.astype(BF16))


def _mod_rows(*rows):
    m = jnp.stack([jnp.broadcast_to(r, rows[-1].shape) for r in rows], axis=1)
    return jnp.pad(m, ((0, 0), (0, MOD_ROWS - len(rows)), (0, 0)))


def hybrid_layer(x, xc, c, c_ctx, need_ctx, angs_mla, angs_swa,
                 norm1_g, norm2_g, w_ada, b_ada, w_in, na_rpb, ml_conv, ml_gate_b,
                 mla_q_norm, mla_w_uq, mla_kv_norm, mla_w_ukv, swa_sink, w_out,
                 peer_wq, peer_keys, peer_u, peer_v):
    B, T, D = x.shape
    Tc = xc.shape[1]
    H = GROUP_HEADS
    flat = lambda a: a.reshape(-1, a.shape[-1])
    sh1, sc1, g1, sh2, sc2, g2 = jnp.split(jax.nn.silu(c) @ w_ada + b_ada, 6, axis=-1)
    sh1c, sc1c, g1c, sh2c, sc2c, g2c = jnp.split((jax.nn.silu(c_ctx) @ w_ada + b_ada)[None], 6, axis=-1)
    lat = in_projection(flat(x), _mod_rows(norm1_g, 1.0 + sc1, sh1), w_in, T)
    cx = in_projection(flat(xc), _mod_rows(norm1_g, 1.0 + sc1c, sh1c), w_in, B * Tc)
    (na_q, na_k, na_v, ml_qk, ml_v, ml_o, ml_g,
     mla_cq, mla_ckv, mla_kr, sw_q, sw_k, sw_v) = [a.reshape(B, T, -1) for a in lat]
    (na_qc, na_kc, na_vc, ml_qkc, ml_vc, ml_oc, ml_gc,
     mla_cqc, mla_ckvc, mla_krc, sw_qc, sw_kc, sw_vc) = [a.reshape(B, Tc, -1) for a in cx]
    attn_scale = HEAD_DIM ** -0.5
    mla_scale = (MLA_NOPE + MLA_ROPE) ** -0.5
    kc_a, vc_a = heads(na_kc, H), heads(na_vc, H)
    y_a = neighbourhood_attention(na_q, na_k, na_v, na_kc, na_vc, na_rpb)
    h_lat, h_ctx = mlstm_mixer((ml_qk, ml_v, ml_g), (ml_qkc, ml_vc, ml_gc), ml_conv, ml_gate_b)
    q_m, k_m, v_m = mla_project(mla_cq, mla_ckv, mla_kr, mla_q_norm, mla_w_uq, mla_kv_norm, mla_w_ukv, angs_mla)
    qc_m, kc_m, vc_m = mla_project(mla_cqc, mla_ckvc, mla_krc, mla_q_norm, mla_w_uq, mla_kv_norm, mla_w_ukv, None)
    y_c = block_dense_attention(q_m, jnp.concatenate([kc_m, k_m], axis=1), jnp.concatenate([vc_m, v_m], axis=1), mla_scale)
    kc_d, vc_d = heads(sw_kc, SWA_KV_HEADS), heads(sw_vc, SWA_KV_HEADS)
    y_d = window_attention(rope_2d(heads(sw_q, H), angs_swa), rope_2d(heads(sw_k, SWA_KV_HEADS), angs_swa),
                           heads(sw_v, SWA_KV_HEADS), kc_d, vc_d, swa_sink)
    x2, h2 = out_projection(flat(y_a), h_lat.reshape(B * T, GROUP_WIDTH), flat(ml_o), flat(y_c), flat(y_d),
                            flat(x), _mod_rows(g1, norm2_g, 1.0 + sc2, sh2), w_out, T)
    u_packed, v_packed = pack_expert_table(peer_u), pack_expert_table(peer_v)
    x = peer_ffn(h2, x2, g2, T, peer_wq, peer_keys, u_packed, v_packed).reshape(B, T, D)
    if not need_ctx:
        return x, None
    xc2, h2c = out_projection(flat(ctx_attn(heads(na_qc, H), kc_a, vc_a, attn_scale)),
                              h_ctx.reshape(B * Tc, GROUP_WIDTH), flat(ml_oc),
                              flat(ctx_attn(qc_m, kc_m, vc_m, mla_scale)),
                              flat(ctx_attn(heads(sw_qc, H), kc_d, vc_d, attn_scale, swa_sink)),
                              flat(xc), _mod_rows(g1c, norm2_g, 1.0 + sc2c, sh2c), w_out, B * Tc)
    xc = peer_ffn(h2c, xc2, g2c, B * Tc, peer_wq, peer_keys, u_packed, v_packed).reshape(B, Tc, D)
    return x, xc


def _final_rmsnorm_kernel(x_ref, g_ref, o_ref):
    x = x_ref[...]
    o_ref[...] = x * lax.rsqrt(jnp.mean(x * x, axis=-1, keepdims=True) + EPS) * g_ref[...]


def final_rmsnorm(x, g):
    B, T, D = x.shape
    rows = 1024
    xf = x.reshape(B * T, D)
    out = pl.pallas_call(
        _final_rmsnorm_kernel,
        grid=(B * T // rows,),
        in_specs=[pl.BlockSpec((rows, D), lambda i: (i, 0)), pl.BlockSpec((1, D), lambda i: (0, 0))],
        out_specs=pl.BlockSpec((rows, D), lambda i: (i, 0)),
        out_shape=jax.ShapeDtypeStruct((B * T, D), x.dtype),
    )(xf, g.reshape(1, D))
    return out.reshape(B, T, D)


def kernel(x, c, ctx, c_ctx, norm1_g, norm2_g, w_ada, b_ada, w_in, na_rpb, ml_conv, ml_gate_b,
           mla_q_norm, mla_w_uq, mla_kv_norm, mla_w_ukv, swa_sink, w_out,
           peer_wq, peer_keys, peer_u, peer_v, final_norm_g):
    T = x.shape[1]
    angs_mla = axial_angles(T, MLA_ROPE)
    angs_swa = axial_angles(T, HEAD_DIM)
    xc = ctx
    for l in range(DEPTH):
        x, xc = hybrid_layer(x, xc, c, c_ctx, l < DEPTH - 1, angs_mla, angs_swa,
                             norm1_g[l], norm2_g[l], w_ada[l], b_ada[l], w_in[l], na_rpb[l],
                             ml_conv[l], ml_gate_b[l], mla_q_norm[l], mla_w_uq[l], mla_kv_norm[l],
                             mla_w_ukv[l], swa_sink[l], w_out[l], peer_wq[l], peer_keys[l],
                             peer_u[l], peer_v[l])
    return final_rmsnorm(x, final_norm_g)
```

```python
import functools

import jax
import jax.numpy as jnp
from jax import lax
import numpy as np
from jax.experimental import pallas as pl
from jax.experimental.pallas import tpu as pltpu

D_MODEL = 1024
BATCH = 2
SEQ = 16384
DEPTH = 2

CTX_LEN = 256
GRID_W = 64
N_MIXERS = 4
MIX_WIDTH = D_MODEL
GROUP_WIDTH = MIX_WIDTH // N_MIXERS
GROUP_HEADS = 4
HEAD_DIM = GROUP_WIDTH // GROUP_HEADS
NA_ROWS = 8
NA_COLS = 16
ML_CHUNK = 64
ML_CONV = 5
MLA_Q_RANK = 256
MLA_KV_RANK = 128
MLA_NOPE = 64
MLA_ROPE = 32
MLA_V = 64
SWA_KV_HEADS = 2
SWA_WINDOW = 128
ATTN_BLOCK = 128
PEER_HEADS = 8
PEER_NKEYS = 128
PEER_EXPERTS = PEER_NKEYS * PEER_NKEYS
PEER_DKEY = 128
PEER_TOPK = 16
PEER_BLOCK = 128
ROPE_BASE = 10000.0
EPS = 1e-6
IN_SIZES = (GROUP_WIDTH, GROUP_WIDTH, GROUP_WIDTH,
            2 * GROUP_WIDTH, GROUP_WIDTH, GROUP_WIDTH, 4 * GROUP_HEADS,
            MLA_Q_RANK, MLA_KV_RANK, MLA_ROPE,
            GROUP_WIDTH, SWA_KV_HEADS * HEAD_DIM, SWA_KV_HEADS * HEAD_DIM)
IN_WIDTH = sum(IN_SIZES)
F32 = jnp.float32


def heads(a, h):
    return a.reshape(a.shape[:-1] + (h, a.shape[-1] // h))


def axial_angles(T, rot_dim):
    t = jnp.arange(T)
    row = (t // GRID_W).astype(F32)
    col = (t % GRID_W).astype(F32)
    half = rot_dim // 2
    inv = 1.0 / (ROPE_BASE ** (jnp.arange(0, half, 2, dtype=F32) / half))
    return row[:, None] * inv, col[:, None] * inv


def rope_1d(x, ang):
    cos = jnp.cos(ang)[None, :, None, :]
    sin = jnp.sin(ang)[None, :, None, :]
    x1, x2 = jnp.split(x.astype(F32), 2, axis=-1)
    return jnp.concatenate([x1 * cos - x2 * sin, x1 * sin + x2 * cos], axis=-1)


def rope_2d(x, angs):
    xr, xc = jnp.split(x, 2, axis=-1)
    return jnp.concatenate([rope_1d(xr, angs[0]), rope_1d(xc, angs[1])], axis=-1).astype(x.dtype)


NT_DIMS = (((1,), (1,)), ((), ()))


def _ctx_attn_kernel(q_ref, k_ref, v_ref, sink_ref, o_ref, *, scale, use_sink):
    s = lax.dot_general(q_ref[0, 0].astype(BF16), k_ref[0, 0].astype(BF16), NT_DIMS,
                        preferred_element_type=F32) * scale
    m = jnp.max(s, axis=-1, keepdims=True)
    if use_sink:
        sink = sink_ref[pl.program_id(1)]
        m = jnp.maximum(m, sink)
    p = jnp.exp(s - m)
    l = jnp.sum(p, axis=-1, keepdims=True)
    if use_sink:
        l = l + jnp.exp(sink - m)
    o_ref[0, 0] = jnp.dot(p.astype(BF16), v_ref[0, 0].astype(BF16), preferred_element_type=F32) / l


def ctx_attn(q, k, v, scale, sink=None):
    B, Tc, H, _ = q.shape
    rep = H // k.shape[2]
    hm = lambda a: jnp.swapaxes(a, 1, 2)
    q, k, v = hm(q), hm(jnp.repeat(k, rep, axis=2)), hm(jnp.repeat(v, rep, axis=2))
    blk = lambda a: pl.BlockSpec((1, 1, Tc, a.shape[-1]), lambda b, h: (b, h, 0, 0))
    out = pl.pallas_call(
        functools.partial(_ctx_attn_kernel, scale=scale, use_sink=sink is not None),
        grid=(B, H),
        in_specs=[blk(q), blk(k), blk(v), pl.BlockSpec(memory_space=pltpu.SMEM)],
        out_specs=blk(v),
        out_shape=jax.ShapeDtypeStruct(v.shape, F32),
    )(q, k, v, jnp.zeros((H,), F32) if sink is None else sink.astype(F32))
    return jnp.swapaxes(out, 1, 2).reshape(B, Tc, -1)
NA_SPAN = NA_ROWS * GRID_W


def _head_mask(width):
    rows = lax.broadcasted_iota(jnp.int32, (GROUP_HEADS * width, GROUP_WIDTH), 0) // width
    cols = lax.broadcasted_iota(jnp.int32, (GROUP_HEADS * width, GROUP_WIDTH), 1) // HEAD_DIM
    return (rows == cols).astype(F32)


def _na_kernel(q_ref, k_ref, v_ref, kc_ref, vc_ref, bias_ref, o_ref):
    r = pl.program_id(1)
    rows = pl.num_programs(1)
    rs = jnp.clip(r - NA_ROWS // 2, 0, rows - NA_ROWS)
    start = pl.multiple_of(rs * GRID_W, GRID_W)
    kw = k_ref[0, pl.ds(start, NA_SPAN), :]
    vw = v_ref[0, pl.ds(start, NA_SPAN), :]
    hm = _head_mask(GRID_W)
    q = q_ref[0] * (HEAD_DIM ** -0.5)
    q4 = (jnp.concatenate([q] * GROUP_HEADS, axis=0) * hm).astype(BF16)
    s_loc = lax.dot_general(q4, kw, NT_DIMS, preferred_element_type=F32) + bias_ref[rs - r + NA_ROWS - 1]
    s_ctx = lax.dot_general(q4, kc_ref[0], NT_DIMS, preferred_element_type=F32)
    m = jnp.maximum(jnp.max(s_loc, axis=-1, keepdims=True), jnp.max(s_ctx, axis=-1, keepdims=True))
    p_loc = jnp.exp(s_loc - m)
    p_ctx = jnp.exp(s_ctx - m)
    l = jnp.sum(p_loc, axis=-1, keepdims=True) + jnp.sum(p_ctx, axis=-1, keepdims=True)
    o = (jnp.dot(p_loc.astype(BF16), vw, preferred_element_type=F32)
         + jnp.dot(p_ctx.astype(BF16), vc_ref[0], preferred_element_type=F32)) * (hm / l)
    o_ref[0] = sum(o[h * GRID_W:(h + 1) * GRID_W] for h in range(GROUP_HEADS))


def _na_bias_table(rpb):
    c = np.arange(GRID_W)
    col_start = np.clip(c - NA_COLS // 2, 0, GRID_W - NA_COLS)
    valid = (c[None, :] >= col_start[:, None]) & (c[None, :] < col_start[:, None] + NA_COLS)
    dc = np.clip(c[None, :] - c[:, None] + NA_COLS - 1, 0, 2 * NA_COLS - 2)
    dr = np.arange(NA_ROWS)[:, None] + np.arange(NA_ROWS)[None, :]
    t = rpb.astype(F32)[:, dr][..., dc]
    t = jnp.where(valid[None, None, None], t, -jnp.inf)
    return jnp.transpose(t, (1, 0, 3, 2, 4)).reshape(NA_ROWS, GROUP_HEADS * GRID_W, NA_SPAN)


def neighbourhood_attention(q, k, v, kc, vc, rpb):
    B, T, C = q.shape
    rows = T // GRID_W
    n_ctx = kc.shape[1]
    bias = _na_bias_table(rpb)
    full = lambda n: pl.BlockSpec((1, n, C), lambda b, r: (b, 0, 0))
    return pl.pallas_call(
        _na_kernel,
        grid=(B, rows),
        in_specs=[pl.BlockSpec((1, GRID_W, C), lambda b, r: (b, r, 0)),
                  full(T), full(T), full(n_ctx), full(n_ctx),
                  pl.BlockSpec(bias.shape, lambda b, r: (0, 0, 0))],
        out_specs=pl.BlockSpec((1, GRID_W, C), lambda b, r: (b, r, 0)),
        out_shape=jax.ShapeDtypeStruct((B, T, C), F32),
        compiler_params=pltpu.CompilerParams(vmem_limit_bytes=VMEM_LIMIT_BYTES),
    )(q, k.astype(BF16), v.astype(BF16), kc.astype(BF16), vc.astype(BF16), bias)


def short_conv(a, w):
    T = a.shape[1]
    pad = w.shape[0] // 2
    ap = jnp.pad(a, ((0, 0), (pad, pad), (0, 0)))
    out = ap[:, :T] * w[0]
    for j in range(1, w.shape[0]):
        out = out + ap[:, j:j + T] * w[j]
    return out


ML_CHUNKS_PER_STEP = CTX_LEN // ML_CHUNK


def _bmm(a, b, contract):
    return lax.dot_general(a.astype(BF16), b.astype(BF16), (contract, ((0,), (0,))),
                           preferred_element_type=F32)


def _mlstm_chunk(qt, kt, vt, irow, brow, state, backward):
    L = ML_CHUNK
    C, nrow, m = state
    row = lax.broadcasted_iota(jnp.int32, (1, L, L), 1)
    col = lax.broadcasted_iota(jnp.int32, (1, L, L), 2)
    seen = (row <= col) if backward else (row >= col)
    eye = row == col

    def as_col(r):
        return jnp.sum(jnp.where(eye, r, 0.0), axis=2, keepdims=True)

    blast = brow[:, :, 0:1] if backward else brow[:, :, L - 1:L]
    rrow = brow - irow
    bcol = as_col(brow)
    d_log = jnp.where(seen, bcol - rrow, -jnp.inf)
    inter = bcol + m
    m_t = jnp.maximum(inter, jnp.max(d_log, axis=2, keepdims=True))
    w = jnp.exp(d_log - m_t)
    a = jnp.exp(inter - m_t)
    s = _bmm(qt, kt, ((2,), (2,))) * w
    num = _bmm(s, vt, ((2,), (1,))) + a * _bmm(qt, C, ((2,), (1,)))
    den = jnp.sum(s, axis=2, keepdims=True) + a * jnp.sum(qt * nrow, axis=2, keepdims=True)
    h = num / jnp.maximum(jnp.abs(den), jnp.exp(-m_t))
    g = blast - rrow
    m_new = jnp.maximum(blast + m, jnp.max(g, axis=2, keepdims=True))
    kw = kt * as_col(jnp.exp(g - m_new))
    decay = jnp.exp(blast + m - m_new)
    C = decay * C + _bmm(jnp.swapaxes(kw, 1, 2), vt, ((2,), (1,)))
    nrow = decay * nrow + jnp.sum(kw, axis=1, keepdims=True)
    return h, (C, nrow, m_new)


def _mlstm_kernel(qf_ref, kf_ref, vf_ref, if_ref, bf_ref, qb_ref, kb_ref, vb_ref, ib_ref, bb_ref,
                  hf_ref, hb_ref, c_ref, n_ref, m_ref):
    N, L = qf_ref.shape[0], ML_CHUNK

    @pl.when(pl.program_id(0) == 0)
    def _():
        c_ref[...] = jnp.zeros(c_ref.shape, F32)
        n_ref[...] = jnp.zeros(n_ref.shape, F32)
        m_ref[...] = jnp.zeros(m_ref.shape, F32)

    fwd = (c_ref[:N], n_ref[:N], m_ref[:N])
    bwd = (c_ref[N:], n_ref[N:], m_ref[N:])
    for c in range(ML_CHUNKS_PER_STEP):
        rows = slice(c * L, (c + 1) * L)
        h, fwd = _mlstm_chunk(qf_ref[:, rows, :], kf_ref[:, rows, :], vf_ref[:, rows, :],
                              if_ref[:, 0, c:c + 1, :], bf_ref[:, 0, c:c + 1, :], fwd, False)
        hf_ref[:, rows, :] = h
        cb = ML_CHUNKS_PER_STEP - 1 - c
        rows = slice(cb * L, (cb + 1) * L)
        h, bwd = _mlstm_chunk(qb_ref[:, rows, :], kb_ref[:, rows, :], vb_ref[:, rows, :],
                              ib_ref[:, 0, cb:cb + 1, :], bb_ref[:, 0, cb:cb + 1, :], bwd, True)
        hb_ref[:, rows, :] = h
    for i, ref in enumerate((c_ref, n_ref, m_ref)):
        ref[:N] = fwd[i]
        ref[N:] = bwd[i]


def mlstm_scan(q, k, v, gates_f, gates_b, n_ctx):
    B, T, H, d = q.shape
    CB, L = ML_CHUNKS_PER_STEP, ML_CHUNK
    assert n_ctx == CB * L and T % (CB * L) == 0
    N, steps = B * H, T // (CB * L)
    hm = lambda a: jnp.swapaxes(a, 1, 2).reshape(N, T, d)
    gates = lambda a: jnp.swapaxes(a, 1, 2).reshape(N, steps, CB, L)
    chunked = lambda a: a.reshape(B, T // L, L, H)
    b_f = jnp.cumsum(chunked(gates_f[1]), axis=2).reshape(B, T, H)
    b_b = lax.cumsum(chunked(gates_b[1]), axis=2, reverse=True).reshape(B, T, H)
    back = lambda j: jnp.where(j == 0, 0, steps - j)
    seq_f = pl.BlockSpec((N, CB * L, d), lambda j: (0, j, 0))
    seq_b = pl.BlockSpec((N, CB * L, d), lambda j: (0, back(j), 0))
    gate_f = pl.BlockSpec((N, 1, CB, L), lambda j: (0, j, 0, 0))
    gate_b = pl.BlockSpec((N, 1, CB, L), lambda j: (0, back(j), 0, 0))
    qh, kh, vh = hm(q), hm(k), hm(v)
    hf, hb = pl.pallas_call(
        _mlstm_kernel,
        grid=(steps,),
        in_specs=[seq_f, seq_f, seq_f, gate_f, gate_f, seq_b, seq_b, seq_b, gate_b, gate_b],
        out_specs=[seq_f, seq_b],
        out_shape=[jax.ShapeDtypeStruct((N, T, d), F32)] * 2,
        scratch_shapes=[pltpu.VMEM((2 * N, d, d), F32), pltpu.VMEM((2 * N, 1, d), F32),
                        pltpu.VMEM((2 * N, 1, 1), F32)],
        compiler_params=pltpu.CompilerParams(vmem_limit_bytes=VMEM_LIMIT_BYTES),
    )(qh, kh, vh, gates(gates_f[0]), gates(b_f), qh, kh, vh, gates(gates_b[0]), gates(b_b))
    return jnp.swapaxes((hf + hb).reshape(B, H, T, d), 1, 2)


def mlstm_prep(qk, v, gates, conv_w, gate_b):
    qk = jax.nn.silu(short_conv(qk, conv_w))
    q, k = jnp.split(qk, 2, axis=-1)
    g = (gates + gate_b).astype(F32)
    i_f, f_f, i_b, f_b = jnp.split(g, 4, axis=-1)
    return (heads(q, GROUP_HEADS) * HEAD_DIM ** -0.5, heads(k, GROUP_HEADS), heads(v, GROUP_HEADS),
            (i_f, jax.nn.log_sigmoid(f_f), i_b, jax.nn.log_sigmoid(f_b)))


def mlstm_mixer(lat, ctx, conv_w, gate_b):
    ql, kl, vl, gl = mlstm_prep(lat[0], lat[1], lat[2], conv_w, gate_b)
    qc, kc, vc, gc = mlstm_prep(ctx[0], ctx[1], ctx[2], conv_w, gate_b)
    Tc = qc.shape[1]
    cat = lambda c_, l_: jnp.concatenate([c_, l_], axis=1)
    h = mlstm_scan(cat(qc, ql), cat(kc, kl), cat(vc, vl),
                   (cat(gc[0], gl[0]), cat(gc[1], gl[1])), (cat(gc[2], gl[2]), cat(gc[3], gl[3])), Tc)
    return h[:, Tc:], h[:, :Tc]


MLA_KR = 9
ROPE_PARTNER = np.concatenate([np.arange(q_, q_ + MLA_ROPE // 4) for q_ in
                               (MLA_ROPE // 4, 0, 3 * MLA_ROPE // 4, MLA_ROPE // 2)])


def mla_rope_tables(T):
    ang_r, ang_c = axial_angles(T, MLA_ROPE)
    cos = jnp.concatenate([jnp.cos(ang_r)] * 2 + [jnp.cos(ang_c)] * 2, axis=1)
    sin = jnp.concatenate([-jnp.sin(ang_r), jnp.sin(ang_r), -jnp.sin(ang_c), jnp.sin(ang_c)], axis=1)
    pad = LANES - MLA_NOPE - MLA_ROPE
    return (jnp.concatenate([jnp.ones((T, MLA_NOPE), F32), cos, jnp.zeros((T, pad), F32)], axis=1),
            jnp.concatenate([jnp.zeros((T, MLA_NOPE), F32), sin, jnp.zeros((T, pad), F32)], axis=1))


def _mla_qkv_kernel(cq_ref, ckv_ref, kr_ref, cos_ref, sin_ref, qn_ref, kvn_ref, wq_ref, wkv_ref,
                    q_ref, k_ref, v_ref):
    def up(x, g, w_ref):
        y = x * lax.rsqrt(jnp.mean(x * x, axis=-1, keepdims=True) + EPS) * g
        return jnp.dot(y.astype(BF16), w_ref[...], preferred_element_type=F32)

    W = GROUP_HEADS * LANES
    per_head = lambda a: jnp.concatenate([a] * GROUP_HEADS, axis=1)
    cos, sin = cos_ref[...], sin_ref[...]
    q2 = up(cq_ref[...], qn_ref[...], wq_ref)
    q_ref[...] = (q2[:, :W] * per_head(cos) + q2[:, W:] * per_head(sin)).astype(BF16)
    kv = up(ckv_ref[...], kvn_ref[...], wkv_ref)
    kr = kr_ref[...]
    k_rope = kr[:, :LANES] * cos + kr[:, LANES:] * sin
    k_ref[...] = (kv[:, :W] + per_head(k_rope)).astype(BF16)
    v_ref[...] = kv[:, W:].astype(BF16)


def mla_qkv(cq, ckv, kr2, q_norm, w_uq, kv_norm, w_ukv, cos, sin):
    N = cq.shape[0]
    P = cos.shape[0]
    tm = min(PROJ_TOKENS, P)
    H, dqk = GROUP_HEADS, MLA_NOPE + MLA_ROPE
    blocks = lambda w_, lo, n: jnp.pad(w_.reshape(w_.shape[0], H, -1)[:, :, lo:lo + n],
                                      ((0, 0), (0, 0), (0, LANES - n))).reshape(w_.shape[0], H * LANES)
    wq = w_uq.reshape(w_uq.shape[0], H, dqk)
    wq_partner = jnp.pad(wq[:, :, MLA_NOPE + ROPE_PARTNER], ((0, 0), (0, 0), (MLA_NOPE, LANES - dqk)))
    wq2 = jnp.concatenate([blocks(w_uq, 0, dqk), wq_partner.reshape(-1, H * LANES)], axis=1).astype(BF16)
    wkv2 = jnp.concatenate([blocks(w_ukv, 0, MLA_NOPE), blocks(w_ukv, MLA_NOPE, MLA_V)], axis=1).astype(BF16)
    tok = lambda w_: pl.BlockSpec((tm, w_), lambda i: (i, 0))
    pos = pl.BlockSpec((tm, LANES), lambda i: (i % (P // tm), 0))
    const = lambda a: pl.BlockSpec(a.shape, lambda i: (0, 0))
    args = (cq, ckv, kr2, cos, sin, q_norm[None], kv_norm[None], wq2, wkv2)
    out = jax.ShapeDtypeStruct((N, H * LANES), BF16)
    return pl.pallas_call(
        _mla_qkv_kernel,
        grid=(N // tm,),
        in_specs=[tok(cq.shape[1]), tok(ckv.shape[1]), tok(kr2.shape[1]), pos, pos] + [const(a) for a in args[5:]],
        out_specs=[tok(H * LANES)] * 3,
        out_shape=[out] * 3,
    )(*args)


LOG2_E = 1.4426950408889634
DENSE_Q_TILE = 1024
DENSE_Q_SUB = 256
DENSE_Q_UNROLL = 4
DENSE_K_TILE_MAX = 8320


def _dense_attn_kernel(q_ref, k_ref, v_ref, o_ref, m_ref, l_ref, acc_ref, *, scale, dv):
    h, j = pl.program_id(2), pl.program_id(3)

    @pl.when(j == 0)
    def _():
        m_ref[...] = jnp.full(m_ref.shape, -jnp.inf, F32)
        l_ref[...] = jnp.zeros(l_ref.shape, F32)
        acc_ref[...] = jnp.zeros(acc_ref.shape, F32)

    def rows(i, carry):
        for u in range(DENSE_Q_UNROLL):
            r = pl.ds(pl.multiple_of((i * DENSE_Q_UNROLL + u) * DENSE_Q_SUB, DENSE_Q_SUB), DENSE_Q_SUB)
            s = lax.dot_general(q_ref[0, r, :], k_ref[0], NT_DIMS,
                                preferred_element_type=F32) * (scale * LOG2_E)
            m_prev = m_ref[r, :]
            m_new = jnp.maximum(m_prev, jnp.max(s, axis=-1, keepdims=True))
            alpha = jnp.exp2(m_prev - m_new)
            p = jnp.exp2(s - m_new)
            l_ref[r, :] = alpha * l_ref[r, :] + jnp.sum(p, axis=-1, keepdims=True)
            acc_ref[r, :] = alpha * acc_ref[r, :] + jnp.dot(p.astype(BF16), v_ref[0],
                                                            preferred_element_type=F32)
            m_ref[r, :] = m_new
        return carry

    lax.fori_loop(0, q_ref.shape[1] // (DENSE_Q_SUB * DENSE_Q_UNROLL), rows, 0)

    for hh in range(o_ref.shape[2] // dv):
        @pl.when((j == pl.num_programs(3) - 1) & (h == hh))
        def _():
            o_ref[0, :, hh * dv:(hh + 1) * dv] = (acc_ref[...] / l_ref[...])[:, :dv]


def dense_attention(q, k_all, v_all, scale, dv):
    B, T, C = q.shape
    H, NK = C // LANES, k_all.shape[1]
    tq = min(DENSE_Q_TILE, T)
    tk = max(t for t in range(LANES, DENSE_K_TILE_MAX + 1, LANES) if NK % t == 0)
    return pl.pallas_call(
        functools.partial(_dense_attn_kernel, scale=scale, dv=dv),
        grid=(B, T // tq, H, NK // tk),
        in_specs=[pl.BlockSpec((1, tq, LANES), lambda b, i, h, j: (b, i, h)),
                  pl.BlockSpec((1, tk, LANES), lambda b, i, h, j: (b, j, h)),
                  pl.BlockSpec((1, tk, LANES), lambda b, i, h, j: (b, j, h))],
        out_specs=pl.BlockSpec((1, tq, H * dv), lambda b, i, h, j: (b, i, 0)),
        out_shape=jax.ShapeDtypeStruct((B, T, H * dv), F32),
        scratch_shapes=[pltpu.VMEM((tq, 1), F32), pltpu.VMEM((tq, 1), F32), pltpu.VMEM((tq, LANES), F32)],
        compiler_params=pltpu.CompilerParams(vmem_limit_bytes=VMEM_LIMIT_BYTES),
    )(q, k_all, v_all)


SWA_SPAN = ATTN_BLOCK + 2 * SWA_WINDOW


def _swa_kernel(q_ref, k_ref, v_ref, kc_ref, vc_ref, sink_ref, o_ref):
    n = pl.program_id(1)
    T = k_ref.shape[1]
    start = pl.multiple_of(jnp.clip(n * ATTN_BLOCK - SWA_WINDOW, 0, T - SWA_SPAN), ATTN_BLOCK)
    kw = k_ref[0, pl.ds(start, SWA_SPAN), :]
    vw = v_ref[0, pl.ds(start, SWA_SPAN), :]
    hm = _head_mask(ATTN_BLOCK)
    q = q_ref[0] * (HEAD_DIM ** -0.5)
    q4 = (jnp.concatenate([q] * GROUP_HEADS, axis=0) * hm).astype(BF16)
    rows = GROUP_HEADS * ATTN_BLOCK
    q_pos = n * ATTN_BLOCK + lax.broadcasted_iota(jnp.int32, (rows, SWA_SPAN), 0) % ATTN_BLOCK
    k_pos = start + lax.broadcasted_iota(jnp.int32, (rows, SWA_SPAN), 1)
    s_loc = lax.dot_general(q4, kw, NT_DIMS, preferred_element_type=F32)
    s_loc = jnp.where(jnp.abs(q_pos - k_pos) <= SWA_WINDOW, s_loc, -jnp.inf)
    s_ctx = lax.dot_general(q4, kc_ref[0], NT_DIMS, preferred_element_type=F32)
    sink = sink_ref[...]
    m = jnp.maximum(jnp.maximum(jnp.max(s_loc, axis=-1, keepdims=True),
                                jnp.max(s_ctx, axis=-1, keepdims=True)), sink)
    p_loc = jnp.exp(s_loc - m)
    p_ctx = jnp.exp(s_ctx - m)
    l = jnp.sum(p_loc, axis=-1, keepdims=True) + jnp.sum(p_ctx, axis=-1, keepdims=True) + jnp.exp(sink - m)
    o = (jnp.dot(p_loc.astype(BF16), vw, preferred_element_type=F32)
         + jnp.dot(p_ctx.astype(BF16), vc_ref[0], preferred_element_type=F32)) * (hm / l)
    o_ref[0] = sum(o[h * ATTN_BLOCK:(h + 1) * ATTN_BLOCK] for h in range(GROUP_HEADS))


def window_attention(q, k, v, kc, vc, sink):
    B, T, H, d = q.shape
    G = H // k.shape[2]
    n_ctx = kc.shape[1]
    C = H * d
    rep = lambda a: jnp.repeat(a, G, axis=2).reshape(a.shape[0], a.shape[1], C).astype(BF16)
    sink_rows = jnp.repeat(sink.astype(F32), ATTN_BLOCK).reshape(H * ATTN_BLOCK, 1)
    full = lambda n: pl.BlockSpec((1, n, C), lambda b, i: (b, 0, 0))
    return pl.pallas_call(
        _swa_kernel,
        grid=(B, T // ATTN_BLOCK),
        in_specs=[pl.BlockSpec((1, ATTN_BLOCK, C), lambda b, i: (b, i, 0)),
                  full(T), full(T), full(n_ctx), full(n_ctx),
                  pl.BlockSpec(sink_rows.shape, lambda b, i: (0, 0))],
        out_specs=pl.BlockSpec((1, ATTN_BLOCK, C), lambda b, i: (b, i, 0)),
        out_shape=jax.ShapeDtypeStruct((B, T, C), F32),
        compiler_params=pltpu.CompilerParams(vmem_limit_bytes=VMEM_LIMIT_BYTES),
    )(q.reshape(B, T, C), rep(k), rep(v), rep(kc), rep(vc), sink_rows)


BF16 = jnp.bfloat16
LANES = 128
SUBLANES = 8
ROW_SEGS = D_MODEL // LANES
ROW_WORDS = ROW_SEGS // 2
PEER_PICKS = PEER_HEADS * PEER_TOPK
PEER_TOPK_TOKENS = 512
PEER_GATHER_TOKENS = 128
PEER_ACT_UNROLL = SUBLANES
VMEM_LIMIT_BYTES = 56 * 1024 * 1024


def _split_bf16(x, parts):
    out = []
    for _ in range(parts):
        p = x.astype(BF16)
        out.append(p)
        x = x - p.astype(F32)
    return out


def _topk_rows(s, k):
    n = s.shape[0]
    iota = lax.broadcasted_iota(jnp.int32, s.shape, 0)
    vals, idxs = [], []
    for _ in range(k):
        m = jnp.max(s, axis=0, keepdims=True)
        i = jnp.min(jnp.where(s == m, iota, n), axis=0, keepdims=True)
        vals.append(m)
        idxs.append(i)
        s = jnp.where(iota == i, -jnp.inf, s)
    return jnp.concatenate(vals, axis=0), jnp.concatenate(idxs, axis=0)


def _peer_topk_kernel(x_ref, wq_ref, keys_ref, eidx_ref, gate_ref):
    xb = x_ref[...].astype(BF16)
    q = jnp.dot(xb, wq_ref[...], preferred_element_type=F32)
    nt = (((1,), (1,)), ((), ()))
    sv, si = [], []
    for p in range(2):
        qp = q[:, p * PEER_DKEY:(p + 1) * PEER_DKEY].astype(BF16)
        s = lax.dot_general(keys_ref[0, p], qp, nt, preferred_element_type=F32)
        v_, i_ = _topk_rows(s, PEER_TOPK)
        sv.append(v_)
        si.append(i_)
    cs, ce = [], []
    half = PEER_TOPK // 2
    for a in range(half):
        nb = PEER_TOPK if a == 0 else half
        cs.append(sv[0][a:a + 1] + sv[1][:nb])
        ce.append(si[0][a:a + 1] * PEER_NKEYS + si[1][:nb])
    cs.append(sv[0][half:] + sv[1][0:1])
    ce.append(si[0][half:] * PEER_NKEYS + si[1][0:1])
    cand_s = jnp.concatenate(cs, axis=0)
    cand_e = jnp.concatenate(ce, axis=0)
    fs, fpos = _topk_rows(cand_s, PEER_TOPK)
    iota = lax.broadcasted_iota(jnp.int32, cand_e.shape, 0)
    eidx = [jnp.max(jnp.where(iota == fpos[j:j + 1], cand_e, -1), axis=0, keepdims=True)
            for j in range(PEER_TOPK)]
    ex = jnp.exp(fs - fs[0:1])
    eidx_ref[0] = jnp.concatenate(eidx, axis=0)
    gate_ref[0] = ex / jnp.sum(ex, axis=0, keepdims=True)


def peer_topk(h, wq, sub_keys):
    N, D = h.shape
    T = PEER_TOPK_TOKENS
    wqb = wq.astype(BF16)
    kb = sub_keys.astype(BF16)
    eidx, gate = pl.pallas_call(
        _peer_topk_kernel,
        grid=(N // T, PEER_HEADS),
        in_specs=[pl.BlockSpec((T, D), lambda i, h_: (i, 0)),
                  pl.BlockSpec((D, 2 * PEER_DKEY), lambda i, h_: (0, h_)),
                  pl.BlockSpec((1, 2, PEER_NKEYS, PEER_DKEY), lambda i, h_: (h_, 0, 0, 0))],
        out_specs=[pl.BlockSpec((1, PEER_TOPK, T), lambda i, h_: (h_, 0, i)),
                   pl.BlockSpec((1, PEER_TOPK, T), lambda i, h_: (h_, 0, i))],
        out_shape=[jax.ShapeDtypeStruct((PEER_HEADS, PEER_TOPK, N), jnp.int32),
                   jax.ShapeDtypeStruct((PEER_HEADS, PEER_TOPK, N), F32)],
        compiler_params=pltpu.CompilerParams(vmem_limit_bytes=VMEM_LIMIT_BYTES),
    )(h, wqb, kb)
    return eidx.reshape(PEER_PICKS, N), gate.reshape(PEER_PICKS, N)


def pack_expert_table(tab):
    E = tab.shape[0]
    t = tab.astype(BF16).reshape(E, ROW_WORDS, 2, LANES)
    t = jnp.swapaxes(t, -1, -2)
    return lax.bitcast_convert_type(t, jnp.uint32).reshape(E * ROW_WORDS, LANES)


def _stage_rows(idx_ref, tab_ref, stage_ref, t):
    for k in range(PEER_PICKS):
        off = pl.multiple_of(idx_ref[t, k], ROW_WORDS)
        stage_ref[k * ROW_WORDS:(k + 1) * ROW_WORDS, :] = tab_ref[pl.ds(off, ROW_WORDS), :]
    return pltpu.bitcast(stage_ref[...], BF16)


def _peer_act_kernel(idx_ref, x_ref, gate_ref, tab_ref, seg_mask_ref, group_ref, w_ref,
                     stage_ref, rows_ref):
    T = x_ref.shape[0]
    U = stage_ref.shape[0]

    sub = lax.broadcasted_iota(jnp.int32, (SUBLANES, PEER_PICKS * ROW_SEGS), 0)

    def tokens(g, carry):
        tile = jnp.zeros((SUBLANES, PEER_PICKS * ROW_SEGS), F32)
        for j in range(U):
            t = g * U + j
            sb = _stage_rows(idx_ref, tab_ref, stage_ref.at[j], t)
            xs = jnp.concatenate(_split_bf16(x_ref[t], 2), axis=0)
            r = lax.dot_general(xs, sb, NT_DIMS, preferred_element_type=F32)
            r = jnp.sum(r * seg_mask_ref[...], axis=0, keepdims=True)
            tile = jnp.where(sub == j, r, tile)
        rows_ref[g] = tile
        return carry

    lax.fori_loop(0, T // U, tokens, 0)
    rows = rows_ref[...].reshape(T, PEER_PICKS * ROW_SEGS)
    act = jnp.zeros((T, PEER_PICKS), F32)
    for piece in _split_bf16(rows, 3):
        act = act + jnp.dot(piece, group_ref[...], preferred_element_type=F32)
    w_ref[...] = gate_ref[...] * (0.5 * act * (1.0 + lax.erf(act * (2.0 ** -0.5))))


def _peer_out_kernel(idx_ref, w_ref, x_ref, g_ref, tab_ref, expand_ref, seg_mask_ref, f_ref, stage_ref):
    T = w_ref.shape[0]
    U = stage_ref.shape[0]

    def tokens(g, carry):
        w8 = w_ref[pl.ds(pl.multiple_of(g * U, U), U), :]
        hi, lo = _split_bf16(w8, 2)
        lhs = jnp.concatenate([jnp.broadcast_to(p[j:j + 1], (SUBLANES, PEER_PICKS))
                               for j in range(U) for p in (hi, lo)], axis=0)
        wrep = jnp.dot(lhs, expand_ref[...], preferred_element_type=F32)
        for j in range(U):
            t = g * U + j
            sb = _stage_rows(idx_ref, tab_ref, stage_ref.at[j], t)
            wsel = (wrep[j * 2 * SUBLANES:(j + 1) * 2 * SUBLANES] * seg_mask_ref[...]).astype(BF16)
            o = jnp.dot(wsel, sb, preferred_element_type=F32)
            f_ref[t] = x_ref[t] + g_ref[0] * (o[:SUBLANES] + o[SUBLANES:])
        return carry

    lax.fori_loop(0, T // U, tokens, 0)


def _peer_constants():
    cols = np.arange(PEER_PICKS * ROW_SEGS)
    seg_mask = (cols[None, :] % ROW_SEGS == np.arange(2 * SUBLANES)[:, None] % SUBLANES)
    group = (cols[:, None] // ROW_SEGS == np.arange(PEER_PICKS)[None, :])
    return (jnp.asarray(seg_mask, F32), jnp.asarray(group, BF16), jnp.asarray(group.T, BF16))


def peer_ffn(h, x, gate2, group_tokens, wq, sub_keys, u_packed, v_packed):
    N, D = h.shape
    T = PEER_GATHER_TOKENS
    eidx, gate = peer_topk(h, wq, sub_keys)
    seg_mask, group, expand = _peer_constants()
    rows3 = lambda a: a.reshape(a.shape[0], ROW_SEGS, LANES)
    offs = eidx.T * ROW_WORDS
    idx_spec = pl.BlockSpec((T, PEER_PICKS), lambda i: (i, 0), memory_space=pltpu.SMEM)
    tab_spec = pl.BlockSpec(u_packed.shape, lambda i: (0, 0), pipeline_mode=pl.Buffered(1))
    tok_spec = pl.BlockSpec((T, ROW_SEGS, LANES), lambda i: (i, 0, 0))
    const = lambda shape: pl.BlockSpec(shape, lambda i: (0, 0))
    params = pltpu.CompilerParams(vmem_limit_bytes=VMEM_LIMIT_BYTES)
    w = pl.pallas_call(
        _peer_act_kernel,
        grid=(N // T,),
        in_specs=[idx_spec, tok_spec,
                  pl.BlockSpec((T, PEER_PICKS), lambda i: (i, 0)),
                  tab_spec, const(seg_mask.shape), const(group.shape)],
        out_specs=pl.BlockSpec((T, PEER_PICKS), lambda i: (i, 0)),
        out_shape=jax.ShapeDtypeStruct((N, PEER_PICKS), F32),
        scratch_shapes=[pltpu.VMEM((PEER_ACT_UNROLL, PEER_PICKS * ROW_WORDS, LANES), jnp.uint32),
                        pltpu.VMEM((T // SUBLANES, SUBLANES, PEER_PICKS * ROW_SEGS), F32)],
        compiler_params=params,
    )(offs, rows3(h), gate.T, u_packed, seg_mask, group)
    out = pl.pallas_call(
        _peer_out_kernel,
        grid=(N // T,),
        in_specs=[idx_spec,
                  pl.BlockSpec((T, PEER_PICKS), lambda i: (i, 0)),
                  tok_spec,
                  pl.BlockSpec((1, ROW_SEGS, LANES), lambda i: (i // (group_tokens // T), 0, 0)),
                  tab_spec, const(expand.shape), const(seg_mask.shape)],
        out_specs=tok_spec,
        out_shape=jax.ShapeDtypeStruct((N, ROW_SEGS, LANES), F32),
        scratch_shapes=[pltpu.VMEM((SUBLANES, PEER_PICKS * ROW_WORDS, LANES), jnp.uint32)],
        compiler_params=params,
    )(offs, w, rows3(x), rows3(gate2), v_packed, expand, seg_mask)
    return out.reshape(N, D)


PROJ_TOKENS = 512
MOD_ROWS = SUBLANES
IN_ALIGNED = tuple(i for i, s_ in enumerate(IN_SIZES) if s_ % LANES == 0)
IN_SMALL = tuple(i for i, s_ in enumerate(IN_SIZES) if s_ % LANES)


def _rms_modulate(x, gain, scale1p, shift):
    r = lax.rsqrt(jnp.mean(x * x, axis=-1, keepdims=True) + EPS)
    return (x * r * gain) * scale1p + shift


def _in_proj_kernel(x_ref, mod_ref, w_ref, *out_refs):
    mod = mod_ref[0]
    h = _rms_modulate(x_ref[...], mod[0:1], mod[1:2], mod[2:3])
    y = jnp.dot(h.astype(BF16), w_ref[...], preferred_element_type=F32)
    off = 0
    for o_ref in out_refs:
        o_ref[...] = y[:, off:off + o_ref.shape[1]].astype(o_ref.dtype)
        off += o_ref.shape[1]


def in_projection(x, mod, w_in, group_tokens):
    N, D = x.shape
    T = min(PROJ_TOKENS, group_tokens)
    starts = np.cumsum((0,) + IN_SIZES)
    group_cols = lambda i: np.arange(starts[i], starts[i + 1])
    small = [i for i in IN_SMALL if i != MLA_KR]
    n_small = sum(IN_SIZES[i] for i in small)
    lane_pad = lambda w_, lo, hi: jnp.pad(w_, ((0, 0), (lo, hi)))
    w_kr = w_in[:, group_cols(MLA_KR)]
    wp = jnp.concatenate(
        [w_in[:, np.concatenate([group_cols(i) for i in IN_ALIGNED])],
         lane_pad(w_in[:, np.concatenate([group_cols(i) for i in small])], 0, -n_small % LANES),
         lane_pad(w_kr, MLA_NOPE, LANES - MLA_NOPE - MLA_ROPE),
         lane_pad(w_kr[:, ROPE_PARTNER], MLA_NOPE, LANES - MLA_NOPE - MLA_ROPE)], axis=1).astype(BF16)
    widths = [IN_SIZES[i] for i in IN_ALIGNED] + [n_small + (-n_small % LANES), 2 * LANES]
    outs = pl.pallas_call(
        _in_proj_kernel,
        grid=(N // T,),
        in_specs=[pl.BlockSpec((T, D), lambda i: (i, 0)),
                  pl.BlockSpec((1, MOD_ROWS, D), lambda i: (i // (group_tokens // T), 0, 0)),
                  pl.BlockSpec(wp.shape, lambda i: (0, 0))],
        out_specs=[pl.BlockSpec((T, w_), lambda i: (i, 0)) for w_ in widths],
        out_shape=[jax.ShapeDtypeStruct((N, w_), F32) for w_ in widths],
        compiler_params=pltpu.CompilerParams(vmem_limit_bytes=VMEM_LIMIT_BYTES),
    )(x, mod, wp)
    groups = dict(zip(IN_ALIGNED, outs[:-2]))
    off = 0
    for i in small:
        groups[i] = outs[-2][:, off:off + IN_SIZES[i]]
        off += IN_SIZES[i]
    groups[MLA_KR] = outs[-1]
    return [groups[i] for i in range(len(IN_SIZES))]


def _out_proj_kernel(ya_ref, hl_ref, mo_ref, yc_ref, yd_ref, x_ref, mod_ref, w_ref, xo_ref, h2_ref):
    yb = hl_ref[...] * jax.nn.sigmoid(mo_ref[...])
    y = jnp.concatenate([ya_ref[...], yb, yc_ref[...], yd_ref[...]], axis=-1).astype(BF16)
    mod = mod_ref[0]
    xn = x_ref[...] + mod[0:1] * jnp.dot(y, w_ref[...], preferred_element_type=F32)
    xo_ref[...] = xn
    h2_ref[...] = _rms_modulate(xn, mod[1:2], mod[2:3], mod[3:4])


def out_projection(ya, hl, mo, yc, yd, x, mod, w_out, group_tokens):
    N, D = x.shape
    T = min(PROJ_TOKENS, group_tokens)
    part = pl.BlockSpec((T, GROUP_WIDTH), lambda i: (i, 0))
    tok = pl.BlockSpec((T, D), lambda i: (i, 0))
    return pl.pallas_call(
        _out_proj_kernel,
        grid=(N // T,),
        in_specs=[part, part, part, part, part, tok,
                  pl.BlockSpec((1, MOD_ROWS, D), lambda i: (i // (group_tokens // T), 0, 0)),
                  pl.BlockSpec(w_out.shape, lambda i: (0, 0))],
        out_specs=[tok, tok],
        out_shape=[jax.ShapeDtypeStruct((N, D), F32)] * 2,
        compiler_params=pltpu.CompilerParams(vmem_limit_bytes=VMEM_LIMIT_BYTES),
    )(ya, hl, mo, yc, yd, x, mod, w_out.astype(BF16))


def _mod_rows(*rows):
    m = jnp.stack([jnp.broadcast_to(r, rows[-1].shape) for r in rows], axis=1)
    return jnp.pad(m, ((0, 0), (0, MOD_ROWS - len(rows)), (0, 0)))


def hybrid_layer(x, xc, c, c_ctx, need_ctx, rope_mla, angs_swa,
                 norm1_g, norm2_g, w_ada, b_ada, w_in, na_rpb, ml_conv, ml_gate_b,
                 mla_q_norm, mla_w_uq, mla_kv_norm, mla_w_ukv, swa_sink, w_out,
                 peer_wq, peer_keys, peer_u, peer_v):
    B, T, D = x.shape
    Tc = xc.shape[1]
    H = GROUP_HEADS
    flat = lambda a: a.reshape(-1, a.shape[-1])
    sh1, sc1, g1, sh2, sc2, g2 = jnp.split(jax.nn.silu(c) @ w_ada + b_ada, 6, axis=-1)
    sh1c, sc1c, g1c, sh2c, sc2c, g2c = jnp.split((jax.nn.silu(c_ctx) @ w_ada + b_ada)[None], 6, axis=-1)
    lat = in_projection(flat(x), _mod_rows(norm1_g, 1.0 + sc1, sh1), w_in, T)
    cx = in_projection(flat(xc), _mod_rows(norm1_g, 1.0 + sc1c, sh1c), w_in, B * Tc)
    (na_q, na_k, na_v, ml_qk, ml_v, ml_o, ml_g,
     mla_cq, mla_ckv, mla_kr, sw_q, sw_k, sw_v) = [a.reshape(B, T, -1) for a in lat]
    (na_qc, na_kc, na_vc, ml_qkc, ml_vc, ml_oc, ml_gc,
     mla_cqc, mla_ckvc, mla_krc, sw_qc, sw_kc, sw_vc) = [a.reshape(B, Tc, -1) for a in cx]
    attn_scale = HEAD_DIM ** -0.5
    mla_scale = (MLA_NOPE + MLA_ROPE) ** -0.5
    kc_a, vc_a = heads(na_kc, H), heads(na_vc, H)
    y_a = neighbourhood_attention(na_q, na_k, na_v, na_kc, na_vc, na_rpb)
    h_lat, h_ctx = mlstm_mixer((ml_qk, ml_v, ml_g), (ml_qkc, ml_vc, ml_gc), ml_conv, ml_gate_b)
    no_rope = (jnp.ones((Tc, LANES), F32), jnp.zeros((Tc, LANES), F32))
    q_m, k_m, v_m = [a.reshape(B, T, -1) for a in
                     mla_qkv(flat(mla_cq), flat(mla_ckv), flat(mla_kr), mla_q_norm, mla_w_uq, mla_kv_norm, mla_w_ukv,
                             *rope_mla)]
    qc_m, kc_m, vc_m = [a.reshape(B, Tc, -1) for a in
                        mla_qkv(flat(mla_cqc), flat(mla_ckvc), flat(mla_krc), mla_q_norm, mla_w_uq, mla_kv_norm,
                                mla_w_ukv, *no_rope)]
    y_c = dense_attention(q_m, jnp.concatenate([kc_m, k_m], axis=1), jnp.concatenate([vc_m, v_m], axis=1),
                          mla_scale, MLA_V)
    kc_d, vc_d = heads(sw_kc, SWA_KV_HEADS), heads(sw_vc, SWA_KV_HEADS)
    y_d = window_attention(rope_2d(heads(sw_q, H), angs_swa), rope_2d(heads(sw_k, SWA_KV_HEADS), angs_swa),
                           heads(sw_v, SWA_KV_HEADS), kc_d, vc_d, swa_sink)
    x2, h2 = out_projection(flat(y_a), h_lat.reshape(B * T, GROUP_WIDTH), flat(ml_o), flat(y_c), flat(y_d),
                            flat(x), _mod_rows(g1, norm2_g, 1.0 + sc2, sh2), w_out, T)
    u_packed, v_packed = pack_expert_table(peer_u), pack_expert_table(peer_v)
    x = peer_ffn(h2, x2, g2, T, peer_wq, peer_keys, u_packed, v_packed).reshape(B, T, D)
    if not need_ctx:
        return x, None
    xc2, h2c = out_projection(flat(ctx_attn(heads(na_qc, H), kc_a, vc_a, attn_scale)),
                              h_ctx.reshape(B * Tc, GROUP_WIDTH), flat(ml_oc),
                              ctx_attn(heads(qc_m, H), heads(kc_m, H), heads(vc_m, H), mla_scale)
                              .reshape(B, Tc, H, LANES)[..., :MLA_V].reshape(B * Tc, GROUP_WIDTH),
                              flat(ctx_attn(heads(sw_qc, H), kc_d, vc_d, attn_scale, swa_sink)),
                              flat(xc), _mod_rows(g1c, norm2_g, 1.0 + sc2c, sh2c), w_out, B * Tc)
    xc = peer_ffn(h2c, xc2, g2c, B * Tc, peer_wq, peer_keys, u_packed, v_packed).reshape(B, Tc, D)
    return x, xc


def _final_rmsnorm_kernel(x_ref, g_ref, o_ref):
    x = x_ref[...]
    o_ref[...] = x * lax.rsqrt(jnp.mean(x * x, axis=-1, keepdims=True) + EPS) * g_ref[...]


def final_rmsnorm(x, g):
    B, T, D = x.shape
    rows = 1024
    xf = x.reshape(B * T, D)
    out = pl.pallas_call(
        _final_rmsnorm_kernel,
        grid=(B * T // rows,),
        in_specs=[pl.BlockSpec((rows, D), lambda i: (i, 0)), pl.BlockSpec((1, D), lambda i: (0, 0))],
        out_specs=pl.BlockSpec((rows, D), lambda i: (i, 0)),
        out_shape=jax.ShapeDtypeStruct((B * T, D), x.dtype),
    )(xf, g.reshape(1, D))
    return out.reshape(B, T, D)


def kernel(x, c, ctx, c_ctx, norm1_g, norm2_g, w_ada, b_ada, w_in, na_rpb, ml_conv, ml_gate_b,
           mla_q_norm, mla_w_uq, mla_kv_norm, mla_w_ukv, swa_sink, w_out,
           peer_wq, peer_keys, peer_u, peer_v, final_norm_g):
    T = x.shape[1]
    rope_mla = mla_rope_tables(T)
    angs_swa = axial_angles(T, HEAD_DIM)
    xc = ctx
    for l in range(DEPTH):
        x, xc = hybrid_layer(x, xc, c, c_ctx, l < DEPTH - 1, rope_mla, angs_swa,
                             norm1_g[l], norm2_g[l], w_ada[l], b_ada[l], w_in[l], na_rpb[l],
                             ml_conv[l], ml_gate_b[l], mla_q_norm[l], mla_w_uq[l], mla_kv_norm[l],
                             mla_w_ukv[l], swa_sink[l], w_out[l], peer_wq[l], peer_keys[l],
                             peer_u[l], peer_v[l])
    return final_rmsnorm(x, final_norm_g)
```

```python
import functools

import jax
import jax.numpy as jnp
from jax import lax
import numpy as np
from jax.experimental import pallas as pl
from jax.experimental.pallas import tpu as pltpu

D_MODEL = 1024
DEPTH = 2

CTX_LEN = 256
GRID_W = 64
N_MIXERS = 4
MIX_WIDTH = D_MODEL
GROUP_WIDTH = MIX_WIDTH // N_MIXERS
GROUP_HEADS = 4
HEAD_DIM = GROUP_WIDTH // GROUP_HEADS
NA_ROWS = 8
NA_COLS = 16
ML_CHUNK = 64
MLA_Q_RANK = 256
MLA_KV_RANK = 128
MLA_NOPE = 64
MLA_ROPE = 32
MLA_V = 64
SWA_KV_HEADS = 2
SWA_WINDOW = 128
ATTN_BLOCK = 128
PEER_HEADS = 8
PEER_NKEYS = 128
PEER_DKEY = 128
PEER_TOPK = 16
ROPE_BASE = 10000.0
EPS = 1e-6
IN_SIZES = (GROUP_WIDTH, GROUP_WIDTH, GROUP_WIDTH,
            2 * GROUP_WIDTH, GROUP_WIDTH, GROUP_WIDTH, 4 * GROUP_HEADS,
            MLA_Q_RANK, MLA_KV_RANK, MLA_ROPE,
            GROUP_WIDTH, SWA_KV_HEADS * HEAD_DIM, SWA_KV_HEADS * HEAD_DIM)
F32 = jnp.float32


def heads(a, h):
    return a.reshape(a.shape[:-1] + (h, a.shape[-1] // h))


def axial_angles(T, rot_dim):
    t = jnp.arange(T)
    row = (t // GRID_W).astype(F32)
    col = (t % GRID_W).astype(F32)
    half = rot_dim // 2
    inv = 1.0 / (ROPE_BASE ** (jnp.arange(0, half, 2, dtype=F32) / half))
    return row[:, None] * inv, col[:, None] * inv


def rope_1d(x, ang):
    cos = jnp.cos(ang)[None, :, None, :]
    sin = jnp.sin(ang)[None, :, None, :]
    x1, x2 = jnp.split(x.astype(F32), 2, axis=-1)
    return jnp.concatenate([x1 * cos - x2 * sin, x1 * sin + x2 * cos], axis=-1)


def rope_2d(x, angs):
    xr, xc = jnp.split(x, 2, axis=-1)
    return jnp.concatenate([rope_1d(xr, angs[0]), rope_1d(xc, angs[1])], axis=-1).astype(x.dtype)


NT_DIMS = (((1,), (1,)), ((), ()))


def _ctx_attn_kernel(q_ref, k_ref, v_ref, sink_ref, o_ref, *, scale, use_sink):
    s = lax.dot_general(q_ref[0, 0].astype(BF16), k_ref[0, 0].astype(BF16), NT_DIMS,
                        preferred_element_type=F32) * scale
    m = jnp.max(s, axis=-1, keepdims=True)
    if use_sink:
        sink = sink_ref[pl.program_id(1)]
        m = jnp.maximum(m, sink)
    p = jnp.exp(s - m)
    l = jnp.sum(p, axis=-1, keepdims=True)
    if use_sink:
        l = l + jnp.exp(sink - m)
    o_ref[0, 0] = jnp.dot(p.astype(BF16), v_ref[0, 0].astype(BF16), preferred_element_type=F32) / l


def ctx_attn(q, k, v, scale, sink=None):
    B, Tc, H, _ = q.shape
    rep = H // k.shape[2]
    hm = lambda a: jnp.swapaxes(a, 1, 2)
    q, k, v = hm(q), hm(jnp.repeat(k, rep, axis=2)), hm(jnp.repeat(v, rep, axis=2))
    blk = lambda a: pl.BlockSpec((1, 1, Tc, a.shape[-1]), lambda b, h: (b, h, 0, 0))
    out = pl.pallas_call(
        functools.partial(_ctx_attn_kernel, scale=scale, use_sink=sink is not None),
        grid=(B, H),
        in_specs=[blk(q), blk(k), blk(v), pl.BlockSpec(memory_space=pltpu.SMEM)],
        out_specs=blk(v),
        out_shape=jax.ShapeDtypeStruct(v.shape, F32),
    )(q, k, v, jnp.zeros((H,), F32) if sink is None else sink.astype(F32))
    return jnp.swapaxes(out, 1, 2).reshape(B, Tc, -1)
NA_SPAN = NA_ROWS * GRID_W


def _head_mask(width):
    rows = lax.broadcasted_iota(jnp.int32, (GROUP_HEADS * width, GROUP_WIDTH), 0) // width
    cols = lax.broadcasted_iota(jnp.int32, (GROUP_HEADS * width, GROUP_WIDTH), 1) // HEAD_DIM
    return (rows == cols).astype(F32)


def _na_kernel(q_ref, k_ref, v_ref, kc_ref, vc_ref, bias_ref, o_ref):
    r = pl.program_id(1)
    rows = pl.num_programs(1)
    rs = jnp.clip(r - NA_ROWS // 2, 0, rows - NA_ROWS)
    start = pl.multiple_of(rs * GRID_W, GRID_W)
    kw = k_ref[0, pl.ds(start, NA_SPAN), :]
    vw = v_ref[0, pl.ds(start, NA_SPAN), :]
    hm = _head_mask(GRID_W)
    q = q_ref[0] * (HEAD_DIM ** -0.5)
    q4 = (jnp.concatenate([q] * GROUP_HEADS, axis=0) * hm).astype(BF16)
    s_loc = lax.dot_general(q4, kw, NT_DIMS, preferred_element_type=F32) + bias_ref[rs - r + NA_ROWS - 1]
    s_ctx = lax.dot_general(q4, kc_ref[0], NT_DIMS, preferred_element_type=F32)
    m = jnp.maximum(jnp.max(s_loc, axis=-1, keepdims=True), jnp.max(s_ctx, axis=-1, keepdims=True))
    p_loc = jnp.exp(s_loc - m)
    p_ctx = jnp.exp(s_ctx - m)
    l = jnp.sum(p_loc, axis=-1, keepdims=True) + jnp.sum(p_ctx, axis=-1, keepdims=True)
    o = (jnp.dot(p_loc.astype(BF16), vw, preferred_element_type=F32)
         + jnp.dot(p_ctx.astype(BF16), vc_ref[0], preferred_element_type=F32)) * (hm / l)
    o_ref[0] = sum(o[h * GRID_W:(h + 1) * GRID_W] for h in range(GROUP_HEADS))


def _na_bias_table(rpb):
    c = np.arange(GRID_W)
    col_start = np.clip(c - NA_COLS // 2, 0, GRID_W - NA_COLS)
    valid = (c[None, :] >= col_start[:, None]) & (c[None, :] < col_start[:, None] + NA_COLS)
    dc = np.clip(c[None, :] - c[:, None] + NA_COLS - 1, 0, 2 * NA_COLS - 2)
    dr = np.arange(NA_ROWS)[:, None] + np.arange(NA_ROWS)[None, :]
    t = rpb.astype(F32)[:, dr][..., dc]
    t = jnp.where(valid[None, None, None], t, -jnp.inf)
    return jnp.transpose(t, (1, 0, 3, 2, 4)).reshape(NA_ROWS, GROUP_HEADS * GRID_W, NA_SPAN)


def neighbourhood_attention(q, k, v, kc, vc, rpb):
    B, T, C = q.shape
    rows = T // GRID_W
    n_ctx = kc.shape[1]
    bias = _na_bias_table(rpb)
    full = lambda n: pl.BlockSpec((1, n, C), lambda b, r: (b, 0, 0))
    return pl.pallas_call(
        _na_kernel,
        grid=(B, rows),
        in_specs=[pl.BlockSpec((1, GRID_W, C), lambda b, r: (b, r, 0)),
                  full(T), full(T), full(n_ctx), full(n_ctx),
                  pl.BlockSpec(bias.shape, lambda b, r: (0, 0, 0))],
        out_specs=pl.BlockSpec((1, GRID_W, C), lambda b, r: (b, r, 0)),
        out_shape=jax.ShapeDtypeStruct((B, T, C), F32),
        compiler_params=pltpu.CompilerParams(vmem_limit_bytes=VMEM_LIMIT_BYTES),
    )(q, k.astype(BF16), v.astype(BF16), kc.astype(BF16), vc.astype(BF16), bias)


def short_conv(a, w):
    T = a.shape[1]
    pad = w.shape[0] // 2
    ap = jnp.pad(a, ((0, 0), (pad, pad), (0, 0)))
    out = ap[:, :T] * w[0]
    for j in range(1, w.shape[0]):
        out = out + ap[:, j:j + T] * w[j]
    return out


ML_CHUNKS_PER_STEP = CTX_LEN // ML_CHUNK


def _bmm(a, b, contract):
    return lax.dot_general(a.astype(BF16), b.astype(BF16), (contract, ((0,), (0,))),
                           preferred_element_type=F32)


def _mlstm_chunk(qt, kt, vt, irow, brow, state, backward):
    L = ML_CHUNK
    C, nrow, m = state
    row = lax.broadcasted_iota(jnp.int32, (1, L, L), 1)
    col = lax.broadcasted_iota(jnp.int32, (1, L, L), 2)
    seen = (row <= col) if backward else (row >= col)
    eye = row == col

    def as_col(r):
        return jnp.sum(jnp.where(eye, r, 0.0), axis=2, keepdims=True)

    blast = brow[:, :, 0:1] if backward else brow[:, :, L - 1:L]
    rrow = brow - irow
    bcol = as_col(brow)
    d_log = jnp.where(seen, bcol - rrow, -jnp.inf)
    inter = bcol + m
    m_t = jnp.maximum(inter, jnp.max(d_log, axis=2, keepdims=True))
    w = jnp.exp(d_log - m_t)
    a = jnp.exp(inter - m_t)
    s = _bmm(qt, kt, ((2,), (2,))) * w
    num = _bmm(s, vt, ((2,), (1,))) + a * _bmm(qt, C, ((2,), (1,)))
    den = jnp.sum(s, axis=2, keepdims=True) + a * jnp.sum(qt * nrow, axis=2, keepdims=True)
    h = num / jnp.maximum(jnp.abs(den), jnp.exp(-m_t))
    g = blast - rrow
    m_new = jnp.maximum(blast + m, jnp.max(g, axis=2, keepdims=True))
    kw = kt * as_col(jnp.exp(g - m_new))
    decay = jnp.exp(blast + m - m_new)
    C = decay * C + _bmm(jnp.swapaxes(kw, 1, 2), vt, ((2,), (1,)))
    nrow = decay * nrow + jnp.sum(kw, axis=1, keepdims=True)
    return h, (C, nrow, m_new)


def _mlstm_kernel(qf_ref, kf_ref, vf_ref, if_ref, bf_ref, qb_ref, kb_ref, vb_ref, ib_ref, bb_ref,
                  hf_ref, hb_ref, c_ref, n_ref, m_ref):
    N, L = qf_ref.shape[0], ML_CHUNK

    @pl.when(pl.program_id(0) == 0)
    def _():
        c_ref[...] = jnp.zeros(c_ref.shape, F32)
        n_ref[...] = jnp.zeros(n_ref.shape, F32)
        m_ref[...] = jnp.zeros(m_ref.shape, F32)

    fwd = (c_ref[:N], n_ref[:N], m_ref[:N])
    bwd = (c_ref[N:], n_ref[N:], m_ref[N:])
    for c in range(ML_CHUNKS_PER_STEP):
        rows = slice(c * L, (c + 1) * L)
        h, fwd = _mlstm_chunk(qf_ref[:, rows, :], kf_ref[:, rows, :], vf_ref[:, rows, :],
                              if_ref[:, 0, c:c + 1, :], bf_ref[:, 0, c:c + 1, :], fwd, False)
        hf_ref[:, rows, :] = h
        cb = ML_CHUNKS_PER_STEP - 1 - c
        rows = slice(cb * L, (cb + 1) * L)
        h, bwd = _mlstm_chunk(qb_ref[:, rows, :], kb_ref[:, rows, :], vb_ref[:, rows, :],
                              ib_ref[:, 0, cb:cb + 1, :], bb_ref[:, 0, cb:cb + 1, :], bwd, True)
        hb_ref[:, rows, :] = h
    for i, ref in enumerate((c_ref, n_ref, m_ref)):
        ref[:N] = fwd[i]
        ref[N:] = bwd[i]


def mlstm_scan(q, k, v, gates_f, gates_b, n_ctx):
    B, T, H, d = q.shape
    CB, L = ML_CHUNKS_PER_STEP, ML_CHUNK
    assert n_ctx == CB * L and T % (CB * L) == 0
    N, steps = B * H, T // (CB * L)
    hm = lambda a: jnp.swapaxes(a, 1, 2).reshape(N, T, d)
    gates = lambda a: jnp.swapaxes(a, 1, 2).reshape(N, steps, CB, L)
    chunked = lambda a: a.reshape(B, T // L, L, H)
    b_f = jnp.cumsum(chunked(gates_f[1]), axis=2).reshape(B, T, H)
    b_b = lax.cumsum(chunked(gates_b[1]), axis=2, reverse=True).reshape(B, T, H)
    back = lambda j: jnp.where(j == 0, 0, steps - j)
    seq_f = pl.BlockSpec((N, CB * L, d), lambda j: (0, j, 0))
    seq_b = pl.BlockSpec((N, CB * L, d), lambda j: (0, back(j), 0))
    gate_f = pl.BlockSpec((N, 1, CB, L), lambda j: (0, j, 0, 0))
    gate_b = pl.BlockSpec((N, 1, CB, L), lambda j: (0, back(j), 0, 0))
    qh, kh, vh = hm(q), hm(k), hm(v)
    hf, hb = pl.pallas_call(
        _mlstm_kernel,
        grid=(steps,),
        in_specs=[seq_f, seq_f, seq_f, gate_f, gate_f, seq_b, seq_b, seq_b, gate_b, gate_b],
        out_specs=[seq_f, seq_b],
        out_shape=[jax.ShapeDtypeStruct((N, T, d), F32)] * 2,
        scratch_shapes=[pltpu.VMEM((2 * N, d, d), F32), pltpu.VMEM((2 * N, 1, d), F32),
                        pltpu.VMEM((2 * N, 1, 1), F32)],
        compiler_params=pltpu.CompilerParams(vmem_limit_bytes=VMEM_LIMIT_BYTES),
    )(qh, kh, vh, gates(gates_f[0]), gates(b_f), qh, kh, vh, gates(gates_b[0]), gates(b_b))
    return jnp.swapaxes((hf + hb).reshape(B, H, T, d), 1, 2)


def mlstm_prep(qk, v, gates, conv_w, gate_b):
    qk = jax.nn.silu(short_conv(qk, conv_w))
    q, k = jnp.split(qk, 2, axis=-1)
    g = (gates + gate_b).astype(F32)
    i_f, f_f, i_b, f_b = jnp.split(g, 4, axis=-1)
    return (heads(q, GROUP_HEADS) * HEAD_DIM ** -0.5, heads(k, GROUP_HEADS), heads(v, GROUP_HEADS),
            (i_f, jax.nn.log_sigmoid(f_f), i_b, jax.nn.log_sigmoid(f_b)))


def mlstm_mixer(lat, ctx, conv_w, gate_b):
    ql, kl, vl, gl = mlstm_prep(lat[0], lat[1], lat[2], conv_w, gate_b)
    qc, kc, vc, gc = mlstm_prep(ctx[0], ctx[1], ctx[2], conv_w, gate_b)
    Tc = qc.shape[1]
    cat = lambda c_, l_: jnp.concatenate([c_, l_], axis=1)
    h = mlstm_scan(cat(qc, ql), cat(kc, kl), cat(vc, vl),
                   (cat(gc[0], gl[0]), cat(gc[1], gl[1])), (cat(gc[2], gl[2]), cat(gc[3], gl[3])), Tc)
    return h[:, Tc:], h[:, :Tc]


MLA_KR = 9
ROPE_PARTNER = np.concatenate([np.arange(q_, q_ + MLA_ROPE // 4) for q_ in
                               (MLA_ROPE // 4, 0, 3 * MLA_ROPE // 4, MLA_ROPE // 2)])


def mla_rope_tables(T):
    ang_r, ang_c = axial_angles(T, MLA_ROPE)
    cos = jnp.concatenate([jnp.cos(ang_r)] * 2 + [jnp.cos(ang_c)] * 2, axis=1)
    sin = jnp.concatenate([-jnp.sin(ang_r), jnp.sin(ang_r), -jnp.sin(ang_c), jnp.sin(ang_c)], axis=1)
    pad = LANES - MLA_NOPE - MLA_ROPE
    return (jnp.concatenate([jnp.ones((T, MLA_NOPE), F32), cos, jnp.zeros((T, pad), F32)], axis=1),
            jnp.concatenate([jnp.zeros((T, MLA_NOPE), F32), sin, jnp.zeros((T, pad), F32)], axis=1))


def _mla_qkv_kernel(cq_ref, ckv_ref, kr_ref, cos_ref, sin_ref, qn_ref, kvn_ref, wq_ref, wkv_ref,
                    q_ref, k_ref, v_ref):
    def up(x, g, w_ref):
        y = x * lax.rsqrt(jnp.mean(x * x, axis=-1, keepdims=True) + EPS) * g
        return jnp.dot(y.astype(BF16), w_ref[...], preferred_element_type=F32)

    W = GROUP_HEADS * LANES
    per_head = lambda a: jnp.concatenate([a] * GROUP_HEADS, axis=1)
    cos, sin = cos_ref[...], sin_ref[...]
    q2 = up(cq_ref[...], qn_ref[...], wq_ref)
    q_ref[...] = (q2[:, :W] * per_head(cos) + q2[:, W:] * per_head(sin)).astype(BF16)
    kv = up(ckv_ref[...], kvn_ref[...], wkv_ref)
    kr = kr_ref[...]
    k_rope = kr[:, :LANES] * cos + kr[:, LANES:] * sin
    k_ref[...] = (kv[:, :W] + per_head(k_rope)).astype(BF16)
    v_ref[...] = kv[:, W:].astype(BF16)


def mla_qkv(cq, ckv, kr2, q_norm, w_uq, kv_norm, w_ukv, cos, sin):
    N = cq.shape[0]
    P = cos.shape[0]
    tm = min(PROJ_TOKENS, P)
    H, dqk = GROUP_HEADS, MLA_NOPE + MLA_ROPE
    blocks = lambda w_, lo, n: jnp.pad(w_.reshape(w_.shape[0], H, -1)[:, :, lo:lo + n],
                                      ((0, 0), (0, 0), (0, LANES - n))).reshape(w_.shape[0], H * LANES)
    wq = w_uq.reshape(w_uq.shape[0], H, dqk)
    wq_partner = jnp.pad(wq[:, :, MLA_NOPE + ROPE_PARTNER], ((0, 0), (0, 0), (MLA_NOPE, LANES - dqk)))
    wq2 = jnp.concatenate([blocks(w_uq, 0, dqk), wq_partner.reshape(-1, H * LANES)], axis=1).astype(BF16)
    wkv2 = jnp.concatenate([blocks(w_ukv, 0, MLA_NOPE), blocks(w_ukv, MLA_NOPE, MLA_V)], axis=1).astype(BF16)
    tok = lambda w_: pl.BlockSpec((tm, w_), lambda i: (i, 0))
    pos = pl.BlockSpec((tm, LANES), lambda i: (i % (P // tm), 0))
    const = lambda a: pl.BlockSpec(a.shape, lambda i: (0, 0))
    args = (cq, ckv, kr2, cos, sin, q_norm[None], kv_norm[None], wq2, wkv2)
    out = jax.ShapeDtypeStruct((N, H * LANES), BF16)
    return pl.pallas_call(
        _mla_qkv_kernel,
        grid=(N // tm,),
        in_specs=[tok(cq.shape[1]), tok(ckv.shape[1]), tok(kr2.shape[1]), pos, pos] + [const(a) for a in args[5:]],
        out_specs=[tok(H * LANES)] * 3,
        out_shape=[out] * 3,
    )(*args)


LOG2_E = 1.4426950408889634
DENSE_Q_TILE = 1024
DENSE_Q_SUB = 256
DENSE_Q_UNROLL = 4
DENSE_K_TILE_MAX = 8320


def _dense_attn_kernel(q_ref, k_ref, v_ref, o_ref, m_ref, l_ref, acc_ref, *, scale, dv):
    h, j = pl.program_id(2), pl.program_id(3)

    @pl.when(j == 0)
    def _():
        m_ref[...] = jnp.full(m_ref.shape, -jnp.inf, F32)
        l_ref[...] = jnp.zeros(l_ref.shape, F32)
        acc_ref[...] = jnp.zeros(acc_ref.shape, F32)

    def rows(i, carry):
        for u in range(DENSE_Q_UNROLL):
            r = pl.ds(pl.multiple_of((i * DENSE_Q_UNROLL + u) * DENSE_Q_SUB, DENSE_Q_SUB), DENSE_Q_SUB)
            s = lax.dot_general(q_ref[0, r, :], k_ref[0], NT_DIMS,
                                preferred_element_type=F32) * (scale * LOG2_E)
            m_prev = m_ref[r, :]
            m_new = jnp.maximum(m_prev, jnp.max(s, axis=-1, keepdims=True))
            alpha = jnp.exp2(m_prev - m_new)
            p = jnp.exp2(s - m_new)
            l_ref[r, :] = alpha * l_ref[r, :] + jnp.sum(p, axis=-1, keepdims=True)
            acc_ref[r, :] = alpha * acc_ref[r, :] + jnp.dot(p.astype(BF16), v_ref[0],
                                                            preferred_element_type=F32)
            m_ref[r, :] = m_new
        return carry

    lax.fori_loop(0, q_ref.shape[1] // (DENSE_Q_SUB * DENSE_Q_UNROLL), rows, 0)

    for hh in range(o_ref.shape[2] // dv):
        @pl.when((j == pl.num_programs(3) - 1) & (h == hh))
        def _():
            o_ref[0, :, hh * dv:(hh + 1) * dv] = (acc_ref[...] / l_ref[...])[:, :dv]


def dense_attention(q, k_all, v_all, scale, dv):
    B, T, C = q.shape
    H, NK = C // LANES, k_all.shape[1]
    tq = min(DENSE_Q_TILE, T)
    tk = max(t for t in range(LANES, DENSE_K_TILE_MAX + 1, LANES) if NK % t == 0)
    return pl.pallas_call(
        functools.partial(_dense_attn_kernel, scale=scale, dv=dv),
        grid=(B, T // tq, H, NK // tk),
        in_specs=[pl.BlockSpec((1, tq, LANES), lambda b, i, h, j: (b, i, h)),
                  pl.BlockSpec((1, tk, LANES), lambda b, i, h, j: (b, j, h)),
                  pl.BlockSpec((1, tk, LANES), lambda b, i, h, j: (b, j, h))],
        out_specs=pl.BlockSpec((1, tq, H * dv), lambda b, i, h, j: (b, i, 0)),
        out_shape=jax.ShapeDtypeStruct((B, T, H * dv), F32),
        scratch_shapes=[pltpu.VMEM((tq, 1), F32), pltpu.VMEM((tq, 1), F32), pltpu.VMEM((tq, LANES), F32)],
        compiler_params=pltpu.CompilerParams(vmem_limit_bytes=VMEM_LIMIT_BYTES),
    )(q, k_all, v_all)


SWA_SPAN = ATTN_BLOCK + 2 * SWA_WINDOW


def _swa_kernel(q_ref, k_ref, v_ref, kc_ref, vc_ref, sink_ref, o_ref):
    n = pl.program_id(1)
    T = k_ref.shape[1]
    start = pl.multiple_of(jnp.clip(n * ATTN_BLOCK - SWA_WINDOW, 0, T - SWA_SPAN), ATTN_BLOCK)
    kw = k_ref[0, pl.ds(start, SWA_SPAN), :]
    vw = v_ref[0, pl.ds(start, SWA_SPAN), :]
    hm = _head_mask(ATTN_BLOCK)
    q = q_ref[0] * (HEAD_DIM ** -0.5)
    q4 = (jnp.concatenate([q] * GROUP_HEADS, axis=0) * hm).astype(BF16)
    rows = GROUP_HEADS * ATTN_BLOCK
    q_pos = n * ATTN_BLOCK + lax.broadcasted_iota(jnp.int32, (rows, SWA_SPAN), 0) % ATTN_BLOCK
    k_pos = start + lax.broadcasted_iota(jnp.int32, (rows, SWA_SPAN), 1)
    s_loc = lax.dot_general(q4, kw, NT_DIMS, preferred_element_type=F32)
    s_loc = jnp.where(jnp.abs(q_pos - k_pos) <= SWA_WINDOW, s_loc, -jnp.inf)
    s_ctx = lax.dot_general(q4, kc_ref[0], NT_DIMS, preferred_element_type=F32)
    sink = sink_ref[...]
    m = jnp.maximum(jnp.maximum(jnp.max(s_loc, axis=-1, keepdims=True),
                                jnp.max(s_ctx, axis=-1, keepdims=True)), sink)
    p_loc = jnp.exp(s_loc - m)
    p_ctx = jnp.exp(s_ctx - m)
    l = jnp.sum(p_loc, axis=-1, keepdims=True) + jnp.sum(p_ctx, axis=-1, keepdims=True) + jnp.exp(sink - m)
    o = (jnp.dot(p_loc.astype(BF16), vw, preferred_element_type=F32)
         + jnp.dot(p_ctx.astype(BF16), vc_ref[0], preferred_element_type=F32)) * (hm / l)
    o_ref[0] = sum(o[h * ATTN_BLOCK:(h + 1) * ATTN_BLOCK] for h in range(GROUP_HEADS))


def window_attention(q, k, v, kc, vc, sink):
    B, T, H, d = q.shape
    G = H // k.shape[2]
    n_ctx = kc.shape[1]
    C = H * d
    rep = lambda a: jnp.repeat(a, G, axis=2).reshape(a.shape[0], a.shape[1], C).astype(BF16)
    sink_rows = jnp.repeat(sink.astype(F32), ATTN_BLOCK).reshape(H * ATTN_BLOCK, 1)
    full = lambda n: pl.BlockSpec((1, n, C), lambda b, i: (b, 0, 0))
    return pl.pallas_call(
        _swa_kernel,
        grid=(B, T // ATTN_BLOCK),
        in_specs=[pl.BlockSpec((1, ATTN_BLOCK, C), lambda b, i: (b, i, 0)),
                  full(T), full(T), full(n_ctx), full(n_ctx),
                  pl.BlockSpec(sink_rows.shape, lambda b, i: (0, 0))],
        out_specs=pl.BlockSpec((1, ATTN_BLOCK, C), lambda b, i: (b, i, 0)),
        out_shape=jax.ShapeDtypeStruct((B, T, C), F32),
        compiler_params=pltpu.CompilerParams(vmem_limit_bytes=VMEM_LIMIT_BYTES),
    )(q.reshape(B, T, C), rep(k), rep(v), rep(kc), rep(vc), sink_rows)


BF16 = jnp.bfloat16
LANES = 128
SUBLANES = 8
ROW_SEGS = D_MODEL // LANES
ROW_WORDS = ROW_SEGS // 2
PEER_PICKS = PEER_HEADS * PEER_TOPK
PEER_TOPK_TOKENS = 512
PEER_GATHER_TOKENS = 128
PEER_ACT_UNROLL = SUBLANES
VMEM_LIMIT_BYTES = 56 * 1024 * 1024


def _split_bf16(x, parts):
    out = []
    for _ in range(parts):
        p = x.astype(BF16)
        out.append(p)
        x = x - p.astype(F32)
    return out


def _topk_rows(s, k):
    n = s.shape[0]
    iota = lax.broadcasted_iota(jnp.int32, s.shape, 0)
    vals, idxs = [], []
    for _ in range(k):
        m = jnp.max(s, axis=0, keepdims=True)
        i = jnp.min(jnp.where(s == m, iota, n), axis=0, keepdims=True)
        vals.append(m)
        idxs.append(i)
        s = jnp.where(iota == i, -jnp.inf, s)
    return jnp.concatenate(vals, axis=0), jnp.concatenate(idxs, axis=0)


def _peer_topk_kernel(x_ref, wq_ref, keys_ref, eidx_ref, gate_ref):
    xb = x_ref[...].astype(BF16)
    q = jnp.dot(xb, wq_ref[...], preferred_element_type=F32)
    nt = (((1,), (1,)), ((), ()))
    sv, si = [], []
    for p in range(2):
        qp = q[:, p * PEER_DKEY:(p + 1) * PEER_DKEY].astype(BF16)
        s = lax.dot_general(keys_ref[0, p], qp, nt, preferred_element_type=F32)
        v_, i_ = _topk_rows(s, PEER_TOPK)
        sv.append(v_)
        si.append(i_)
    cs, ce = [], []
    half = PEER_TOPK // 2
    for a in range(half):
        nb = PEER_TOPK if a == 0 else half
        cs.append(sv[0][a:a + 1] + sv[1][:nb])
        ce.append(si[0][a:a + 1] * PEER_NKEYS + si[1][:nb])
    cs.append(sv[0][half:] + sv[1][0:1])
    ce.append(si[0][half:] * PEER_NKEYS + si[1][0:1])
    cand_s = jnp.concatenate(cs, axis=0)
    cand_e = jnp.concatenate(ce, axis=0)
    fs, fpos = _topk_rows(cand_s, PEER_TOPK)
    iota = lax.broadcasted_iota(jnp.int32, cand_e.shape, 0)
    eidx = [jnp.max(jnp.where(iota == fpos[j:j + 1], cand_e, -1), axis=0, keepdims=True)
            for j in range(PEER_TOPK)]
    ex = jnp.exp(fs - fs[0:1])
    eidx_ref[0] = jnp.concatenate(eidx, axis=0)
    gate_ref[0] = ex / jnp.sum(ex, axis=0, keepdims=True)


def peer_topk(h, wq, sub_keys):
    N, D = h.shape
    T = PEER_TOPK_TOKENS
    wqb = wq.astype(BF16)
    kb = sub_keys.astype(BF16)
    eidx, gate = pl.pallas_call(
        _peer_topk_kernel,
        grid=(N // T, PEER_HEADS),
        in_specs=[pl.BlockSpec((T, D), lambda i, h_: (i, 0)),
                  pl.BlockSpec((D, 2 * PEER_DKEY), lambda i, h_: (0, h_)),
                  pl.BlockSpec((1, 2, PEER_NKEYS, PEER_DKEY), lambda i, h_: (h_, 0, 0, 0))],
        out_specs=[pl.BlockSpec((1, PEER_TOPK, T), lambda i, h_: (h_, 0, i)),
                   pl.BlockSpec((1, PEER_TOPK, T), lambda i, h_: (h_, 0, i))],
        out_shape=[jax.ShapeDtypeStruct((PEER_HEADS, PEER_TOPK, N), jnp.int32),
                   jax.ShapeDtypeStruct((PEER_HEADS, PEER_TOPK, N), F32)],
        compiler_params=pltpu.CompilerParams(vmem_limit_bytes=VMEM_LIMIT_BYTES),
    )(h, wqb, kb)
    return eidx.reshape(PEER_PICKS, N), gate.reshape(PEER_PICKS, N)


def pack_expert_table(tab):
    E = tab.shape[0]
    t = tab.astype(BF16).reshape(E, ROW_WORDS, 2, LANES)
    t = jnp.swapaxes(t, -1, -2)
    return lax.bitcast_convert_type(t, jnp.uint32).reshape(E * ROW_WORDS, LANES)


def _stage_rows(idx_ref, tab_ref, stage_ref, t):
    for k in range(PEER_PICKS):
        off = pl.multiple_of(idx_ref[t, k], ROW_WORDS)
        stage_ref[k * ROW_WORDS:(k + 1) * ROW_WORDS, :] = tab_ref[pl.ds(off, ROW_WORDS), :]
    return pltpu.bitcast(stage_ref[...], BF16)


def _peer_act_kernel(idx_ref, x_ref, gate_ref, tab_ref, seg_mask_ref, group_ref, w_ref,
                     stage_ref, rows_ref):
    T = x_ref.shape[0]
    U = stage_ref.shape[0]

    sub = lax.broadcasted_iota(jnp.int32, (SUBLANES, PEER_PICKS * ROW_SEGS), 0)

    def tokens(g, carry):
        tile = jnp.zeros((SUBLANES, PEER_PICKS * ROW_SEGS), F32)
        for j in range(U):
            t = g * U + j
            sb = _stage_rows(idx_ref, tab_ref, stage_ref.at[j], t)
            xs = jnp.concatenate(_split_bf16(x_ref[t], 2), axis=0)
            r = lax.dot_general(xs, sb, NT_DIMS, preferred_element_type=F32)
            r = jnp.sum(r * seg_mask_ref[...], axis=0, keepdims=True)
            tile = jnp.where(sub == j, r, tile)
        rows_ref[g] = tile
        return carry

    lax.fori_loop(0, T // U, tokens, 0)
    rows = rows_ref[...].reshape(T, PEER_PICKS * ROW_SEGS)
    act = jnp.zeros((T, PEER_PICKS), F32)
    for piece in _split_bf16(rows, 3):
        act = act + jnp.dot(piece, group_ref[...], preferred_element_type=F32)
    w_ref[...] = gate_ref[...] * (0.5 * act * (1.0 + lax.erf(act * (2.0 ** -0.5))))


def _peer_out_kernel(idx_ref, w_ref, x_ref, g_ref, tab_ref, expand_ref, seg_mask_ref, f_ref, stage_ref):
    T = w_ref.shape[0]
    U = stage_ref.shape[0]

    def tokens(g, carry):
        w8 = w_ref[pl.ds(pl.multiple_of(g * U, U), U), :]
        hi, lo = _split_bf16(w8, 2)
        lhs = jnp.concatenate([jnp.broadcast_to(p[j:j + 1], (SUBLANES, PEER_PICKS))
                               for j in range(U) for p in (hi, lo)], axis=0)
        wrep = jnp.dot(lhs, expand_ref[...], preferred_element_type=F32)
        for j in range(U):
            t = g * U + j
            sb = _stage_rows(idx_ref, tab_ref, stage_ref.at[j], t)
            wsel = (wrep[j * 2 * SUBLANES:(j + 1) * 2 * SUBLANES] * seg_mask_ref[...]).astype(BF16)
            o = jnp.dot(wsel, sb, preferred_element_type=F32)
            f_ref[t] = x_ref[t] + g_ref[0] * (o[:SUBLANES] + o[SUBLANES:])
        return carry

    lax.fori_loop(0, T // U, tokens, 0)


def _peer_constants():
    cols = np.arange(PEER_PICKS * ROW_SEGS)
    seg_mask = (cols[None, :] % ROW_SEGS == np.arange(2 * SUBLANES)[:, None] % SUBLANES)
    group = (cols[:, None] // ROW_SEGS == np.arange(PEER_PICKS)[None, :])
    return (jnp.asarray(seg_mask, F32), jnp.asarray(group, BF16), jnp.asarray(group.T, BF16))


def peer_ffn(h, x, gate2, group_tokens, wq, sub_keys, u_packed, v_packed):
    N, D = h.shape
    T = PEER_GATHER_TOKENS
    eidx, gate = peer_topk(h, wq, sub_keys)
    seg_mask, group, expand = _peer_constants()
    rows3 = lambda a: a.reshape(a.shape[0], ROW_SEGS, LANES)
    offs = eidx.T * ROW_WORDS
    idx_spec = pl.BlockSpec((T, PEER_PICKS), lambda i: (i, 0), memory_space=pltpu.SMEM)
    tab_spec = pl.BlockSpec(u_packed.shape, lambda i: (0, 0), pipeline_mode=pl.Buffered(1))
    tok_spec = pl.BlockSpec((T, ROW_SEGS, LANES), lambda i: (i, 0, 0))
    const = lambda shape: pl.BlockSpec(shape, lambda i: (0, 0))
    params = pltpu.CompilerParams(vmem_limit_bytes=VMEM_LIMIT_BYTES)
    w = pl.pallas_call(
        _peer_act_kernel,
        grid=(N // T,),
        in_specs=[idx_spec, tok_spec,
                  pl.BlockSpec((T, PEER_PICKS), lambda i: (i, 0)),
                  tab_spec, const(seg_mask.shape), const(group.shape)],
        out_specs=pl.BlockSpec((T, PEER_PICKS), lambda i: (i, 0)),
        out_shape=jax.ShapeDtypeStruct((N, PEER_PICKS), F32),
        scratch_shapes=[pltpu.VMEM((PEER_ACT_UNROLL, PEER_PICKS * ROW_WORDS, LANES), jnp.uint32),
                        pltpu.VMEM((T // SUBLANES, SUBLANES, PEER_PICKS * ROW_SEGS), F32)],
        compiler_params=params,
    )(offs, rows3(h), gate.T, u_packed, seg_mask, group)
    out = pl.pallas_call(
        _peer_out_kernel,
        grid=(N // T,),
        in_specs=[idx_spec,
                  pl.BlockSpec((T, PEER_PICKS), lambda i: (i, 0)),
                  tok_spec,
                  pl.BlockSpec((1, ROW_SEGS, LANES), lambda i: (i // (group_tokens // T), 0, 0)),
                  tab_spec, const(expand.shape), const(seg_mask.shape)],
        out_specs=tok_spec,
        out_shape=jax.ShapeDtypeStruct((N, ROW_SEGS, LANES), F32),
        scratch_shapes=[pltpu.VMEM((SUBLANES, PEER_PICKS * ROW_WORDS, LANES), jnp.uint32)],
        compiler_params=params,
    )(offs, w, rows3(x), rows3(gate2), v_packed, expand, seg_mask)
    return out.reshape(N, D)


PROJ_TOKENS = 512
MOD_ROWS = SUBLANES
IN_ALIGNED = tuple(i for i, s_ in enumerate(IN_SIZES) if s_ % LANES == 0)
IN_SMALL = tuple(i for i, s_ in enumerate(IN_SIZES) if s_ % LANES)


def _rms_modulate(x, gain, scale1p, shift):
    r = lax.rsqrt(jnp.mean(x * x, axis=-1, keepdims=True) + EPS)
    return (x * r * gain) * scale1p + shift


def _in_proj_kernel(x_ref, mod_ref, w_ref, *out_refs):
    mod = mod_ref[0]
    h = _rms_modulate(x_ref[...], mod[0:1], mod[1:2], mod[2:3])
    y = jnp.dot(h.astype(BF16), w_ref[...], preferred_element_type=F32)
    off = 0
    for o_ref in out_refs:
        o_ref[...] = y[:, off:off + o_ref.shape[1]].astype(o_ref.dtype)
        off += o_ref.shape[1]


def in_projection(x, mod, w_in, group_tokens):
    N, D = x.shape
    T = min(PROJ_TOKENS, group_tokens)
    starts = np.cumsum((0,) + IN_SIZES)
    group_cols = lambda i: np.arange(starts[i], starts[i + 1])
    small = [i for i in IN_SMALL if i != MLA_KR]
    n_small = sum(IN_SIZES[i] for i in small)
    lane_pad = lambda w_, lo, hi: jnp.pad(w_, ((0, 0), (lo, hi)))
    w_kr = w_in[:, group_cols(MLA_KR)]
    wp = jnp.concatenate(
        [w_in[:, np.concatenate([group_cols(i) for i in IN_ALIGNED])],
         lane_pad(w_in[:, np.concatenate([group_cols(i) for i in small])], 0, -n_small % LANES),
         lane_pad(w_kr, MLA_NOPE, LANES - MLA_NOPE - MLA_ROPE),
         lane_pad(w_kr[:, ROPE_PARTNER], MLA_NOPE, LANES - MLA_NOPE - MLA_ROPE)], axis=1).astype(BF16)
    widths = [IN_SIZES[i] for i in IN_ALIGNED] + [n_small + (-n_small % LANES), 2 * LANES]
    outs = pl.pallas_call(
        _in_proj_kernel,
        grid=(N // T,),
        in_specs=[pl.BlockSpec((T, D), lambda i: (i, 0)),
                  pl.BlockSpec((1, MOD_ROWS, D), lambda i: (i // (group_tokens // T), 0, 0)),
                  pl.BlockSpec(wp.shape, lambda i: (0, 0))],
        out_specs=[pl.BlockSpec((T, w_), lambda i: (i, 0)) for w_ in widths],
        out_shape=[jax.ShapeDtypeStruct((N, w_), F32) for w_ in widths],
        compiler_params=pltpu.CompilerParams(vmem_limit_bytes=VMEM_LIMIT_BYTES),
    )(x, mod, wp)
    groups = dict(zip(IN_ALIGNED, outs[:-2]))
    off = 0
    for i in small:
        groups[i] = outs[-2][:, off:off + IN_SIZES[i]]
        off += IN_SIZES[i]
    groups[MLA_KR] = outs[-1]
    return [groups[i] for i in range(len(IN_SIZES))]


def _out_proj_kernel(ya_ref, hl_ref, mo_ref, yc_ref, yd_ref, x_ref, mod_ref, w_ref, xo_ref, h2_ref):
    yb = hl_ref[...] * jax.nn.sigmoid(mo_ref[...])
    y = jnp.concatenate([ya_ref[...], yb, yc_ref[...], yd_ref[...]], axis=-1).astype(BF16)
    mod = mod_ref[0]
    xn = x_ref[...] + mod[0:1] * jnp.dot(y, w_ref[...], preferred_element_type=F32)
    xo_ref[...] = xn
    h2_ref[...] = _rms_modulate(xn, mod[1:2], mod[2:3], mod[3:4])


def out_projection(ya, hl, mo, yc, yd, x, mod, w_out, group_tokens):
    N, D = x.shape
    T = min(PROJ_TOKENS, group_tokens)
    part = pl.BlockSpec((T, GROUP_WIDTH), lambda i: (i, 0))
    tok = pl.BlockSpec((T, D), lambda i: (i, 0))
    return pl.pallas_call(
        _out_proj_kernel,
        grid=(N // T,),
        in_specs=[part, part, part, part, part, tok,
                  pl.BlockSpec((1, MOD_ROWS, D), lambda i: (i // (group_tokens // T), 0, 0)),
                  pl.BlockSpec(w_out.shape, lambda i: (0, 0))],
        out_specs=[tok, tok],
        out_shape=[jax.ShapeDtypeStruct((N, D), F32)] * 2,
        compiler_params=pltpu.CompilerParams(vmem_limit_bytes=VMEM_LIMIT_BYTES),
    )(ya, hl, mo, yc, yd, x, mod, w_out.astype(BF16))


def _adaln_kernel(c_ref, w_ref, b_ref, o_ref):
    c = c_ref[...]
    a = c * jax.nn.sigmoid(c)
    o_ref[...] = jnp.dot(a.astype(BF16), w_ref[...].astype(BF16), preferred_element_type=F32) + b_ref[...]


def adaln_linear(c, w_ada, b_ada):
    R, D = c.shape
    rows = -R % SUBLANES + R
    out = pl.pallas_call(
        _adaln_kernel,
        grid=(w_ada.shape[1] // D,),
        in_specs=[pl.BlockSpec((rows, D), lambda j: (0, 0)),
                  pl.BlockSpec((D, D), lambda j: (0, j)),
                  pl.BlockSpec((1, D), lambda j: (0, j))],
        out_specs=pl.BlockSpec((rows, D), lambda j: (0, j)),
        out_shape=jax.ShapeDtypeStruct((rows, w_ada.shape[1]), F32),
    )(jnp.pad(c, ((0, rows - R), (0, 0))), w_ada, b_ada[None])
    return out[:R]


def _mod_rows(*rows):
    m = jnp.stack([jnp.broadcast_to(r, rows[-1].shape) for r in rows], axis=1)
    return jnp.pad(m, ((0, 0), (0, MOD_ROWS - len(rows)), (0, 0)))


def hybrid_layer(x, xc, c, c_ctx, need_ctx, rope_mla, angs_swa,
                 norm1_g, norm2_g, w_ada, b_ada, w_in, na_rpb, ml_conv, ml_gate_b,
                 mla_q_norm, mla_w_uq, mla_kv_norm, mla_w_ukv, swa_sink, w_out,
                 peer_wq, peer_keys, peer_u, peer_v):
    B, T, D = x.shape
    Tc = xc.shape[1]
    H = GROUP_HEADS
    flat = lambda a: a.reshape(-1, a.shape[-1])
    ada = adaln_linear(jnp.concatenate([c, c_ctx[None]], axis=0), w_ada, b_ada)
    sh1, sc1, g1, sh2, sc2, g2 = jnp.split(ada[:B], 6, axis=-1)
    sh1c, sc1c, g1c, sh2c, sc2c, g2c = jnp.split(ada[B:], 6, axis=-1)
    lat = in_projection(flat(x), _mod_rows(norm1_g, 1.0 + sc1, sh1), w_in, T)
    cx = in_projection(flat(xc), _mod_rows(norm1_g, 1.0 + sc1c, sh1c), w_in, B * Tc)
    (na_q, na_k, na_v, ml_qk, ml_v, ml_o, ml_g,
     mla_cq, mla_ckv, mla_kr, sw_q, sw_k, sw_v) = [a.reshape(B, T, -1) for a in lat]
    (na_qc, na_kc, na_vc, ml_qkc, ml_vc, ml_oc, ml_gc,
     mla_cqc, mla_ckvc, mla_krc, sw_qc, sw_kc, sw_vc) = [a.reshape(B, Tc, -1) for a in cx]
    attn_scale = HEAD_DIM ** -0.5
    mla_scale = (MLA_NOPE + MLA_ROPE) ** -0.5
    kc_a, vc_a = heads(na_kc, H), heads(na_vc, H)
    y_a = neighbourhood_attention(na_q, na_k, na_v, na_kc, na_vc, na_rpb)
    h_lat, h_ctx = mlstm_mixer((ml_qk, ml_v, ml_g), (ml_qkc, ml_vc, ml_gc), ml_conv, ml_gate_b)
    no_rope = (jnp.ones((Tc, LANES), F32), jnp.zeros((Tc, LANES), F32))
    q_m, k_m, v_m = [a.reshape(B, T, -1) for a in
                     mla_qkv(flat(mla_cq), flat(mla_ckv), flat(mla_kr), mla_q_norm, mla_w_uq, mla_kv_norm, mla_w_ukv,
                             *rope_mla)]
    qc_m, kc_m, vc_m = [a.reshape(B, Tc, -1) for a in
                        mla_qkv(flat(mla_cqc), flat(mla_ckvc), flat(mla_krc), mla_q_norm, mla_w_uq, mla_kv_norm,
                                mla_w_ukv, *no_rope)]
    y_c = dense_attention(q_m, jnp.concatenate([kc_m, k_m], axis=1), jnp.concatenate([vc_m, v_m], axis=1),
                          mla_scale, MLA_V)
    kc_d, vc_d = heads(sw_kc, SWA_KV_HEADS), heads(sw_vc, SWA_KV_HEADS)
    y_d = window_attention(rope_2d(heads(sw_q, H), angs_swa), rope_2d(heads(sw_k, SWA_KV_HEADS), angs_swa),
                           heads(sw_v, SWA_KV_HEADS), kc_d, vc_d, swa_sink)
    x2, h2 = out_projection(flat(y_a), h_lat.reshape(B * T, GROUP_WIDTH), flat(ml_o), flat(y_c), flat(y_d),
                            flat(x), _mod_rows(g1, norm2_g, 1.0 + sc2, sh2), w_out, T)
    u_packed, v_packed = pack_expert_table(peer_u), pack_expert_table(peer_v)
    x = peer_ffn(h2, x2, g2, T, peer_wq, peer_keys, u_packed, v_packed).reshape(B, T, D)
    if not need_ctx:
        return x, None
    xc2, h2c = out_projection(flat(ctx_attn(heads(na_qc, H), kc_a, vc_a, attn_scale)),
                              h_ctx.reshape(B * Tc, GROUP_WIDTH), flat(ml_oc),
                              ctx_attn(heads(qc_m, H), heads(kc_m, H), heads(vc_m, H), mla_scale)
                              .reshape(B, Tc, H, LANES)[..., :MLA_V].reshape(B * Tc, GROUP_WIDTH),
                              flat(ctx_attn(heads(sw_qc, H), kc_d, vc_d, attn_scale, swa_sink)),
                              flat(xc), _mod_rows(g1c, norm2_g, 1.0 + sc2c, sh2c), w_out, B * Tc)
    xc = peer_ffn(h2c, xc2, g2c, B * Tc, peer_wq, peer_keys, u_packed, v_packed).reshape(B, Tc, D)
    return x, xc


def _final_rmsnorm_kernel(x_ref, g_ref, o_ref):
    x = x_ref[...]
    o_ref[...] = x * lax.rsqrt(jnp.mean(x * x, axis=-1, keepdims=True) + EPS) * g_ref[...]


def final_rmsnorm(x, g):
    B, T, D = x.shape
    rows = 1024
    xf = x.reshape(B * T, D)
    out = pl.pallas_call(
        _final_rmsnorm_kernel,
        grid=(B * T // rows,),
        in_specs=[pl.BlockSpec((rows, D), lambda i: (i, 0)), pl.BlockSpec((1, D), lambda i: (0, 0))],
        out_specs=pl.BlockSpec((rows, D), lambda i: (i, 0)),
        out_shape=jax.ShapeDtypeStruct((B * T, D), x.dtype),
    )(xf, g.reshape(1, D))
    return out.reshape(B, T, D)


def kernel(x, c, ctx, c_ctx, norm1_g, norm2_g, w_ada, b_ada, w_in, na_rpb, ml_conv, ml_gate_b,
           mla_q_norm, mla_w_uq, mla_kv_norm, mla_w_ukv, swa_sink, w_out,
           peer_wq, peer_keys, peer_u, peer_v, final_norm_g):
    T = x.shape[1]
    rope_mla = mla_rope_tables(T)
    angs_swa = axial_angles(T, HEAD_DIM)
    xc = ctx
    for l in range(DEPTH):
        x, xc = hybrid_layer(x, xc, c, c_ctx, l < DEPTH - 1, rope_mla, angs_swa,
                             norm1_g[l], norm2_g[l], w_ada[l], b_ada[l], w_in[l], na_rpb[l],
                             ml_conv[l], ml_gate_b[l], mla_q_norm[l], mla_w_uq[l], mla_kv_norm[l],
                             mla_w_ukv[l], swa_sink[l], w_out[l], peer_wq[l], peer_keys[l],
                             peer_u[l], peer_v[l])
    return final_rmsnorm(x, final_norm_g)
```

```python
import functools

import jax
import jax.numpy as jnp
from jax import lax
import numpy as np
from jax.experimental import pallas as pl
from jax.experimental.pallas import tpu as pltpu

D_MODEL = 1024
DEPTH = 2

CTX_LEN = 256
GRID_W = 64
N_MIXERS = 4
MIX_WIDTH = D_MODEL
GROUP_WIDTH = MIX_WIDTH // N_MIXERS
GROUP_HEADS = 4
HEAD_DIM = GROUP_WIDTH // GROUP_HEADS
NA_ROWS = 8
NA_COLS = 16
ML_CHUNK = 64
MLA_Q_RANK = 256
MLA_KV_RANK = 128
MLA_NOPE = 64
MLA_ROPE = 32
MLA_V = 64
SWA_KV_HEADS = 2
SWA_WINDOW = 128
ATTN_BLOCK = 128
PEER_HEADS = 8
PEER_NKEYS = 128
PEER_DKEY = 128
PEER_TOPK = 16
ROPE_BASE = 10000.0
EPS = 1e-6
IN_SIZES = (GROUP_WIDTH, GROUP_WIDTH, GROUP_WIDTH,
            2 * GROUP_WIDTH, GROUP_WIDTH, GROUP_WIDTH, 4 * GROUP_HEADS,
            MLA_Q_RANK, MLA_KV_RANK, MLA_ROPE,
            GROUP_WIDTH, SWA_KV_HEADS * HEAD_DIM, SWA_KV_HEADS * HEAD_DIM)
F32 = jnp.float32


def heads(a, h):
    return a.reshape(a.shape[:-1] + (h, a.shape[-1] // h))


def axial_angles(T, rot_dim):
    t = jnp.arange(T)
    row = (t // GRID_W).astype(F32)
    col = (t % GRID_W).astype(F32)
    half = rot_dim // 2
    inv = 1.0 / (ROPE_BASE ** (jnp.arange(0, half, 2, dtype=F32) / half))
    return row[:, None] * inv, col[:, None] * inv


def rope_1d(x, ang):
    cos = jnp.cos(ang)[None, :, None, :]
    sin = jnp.sin(ang)[None, :, None, :]
    x1, x2 = jnp.split(x.astype(F32), 2, axis=-1)
    return jnp.concatenate([x1 * cos - x2 * sin, x1 * sin + x2 * cos], axis=-1)


def rope_2d(x, angs):
    xr, xc = jnp.split(x, 2, axis=-1)
    return jnp.concatenate([rope_1d(xr, angs[0]), rope_1d(xc, angs[1])], axis=-1).astype(x.dtype)


NT_DIMS = (((1,), (1,)), ((), ()))


def _ctx_attn_kernel(q_ref, k_ref, v_ref, sink_ref, o_ref, *, scale, use_sink):
    s = lax.dot_general(q_ref[0, 0].astype(BF16), k_ref[0, 0].astype(BF16), NT_DIMS,
                        preferred_element_type=F32) * scale
    m = jnp.max(s, axis=-1, keepdims=True)
    if use_sink:
        sink = sink_ref[pl.program_id(1)]
        m = jnp.maximum(m, sink)
    p = jnp.exp(s - m)
    l = jnp.sum(p, axis=-1, keepdims=True)
    if use_sink:
        l = l + jnp.exp(sink - m)
    o_ref[0, 0] = jnp.dot(p.astype(BF16), v_ref[0, 0].astype(BF16), preferred_element_type=F32) / l


def ctx_attn(q, k, v, scale, sink=None):
    B, Tc, H, _ = q.shape
    rep = H // k.shape[2]
    hm = lambda a: jnp.swapaxes(a, 1, 2)
    q, k, v = hm(q), hm(jnp.repeat(k, rep, axis=2)), hm(jnp.repeat(v, rep, axis=2))
    blk = lambda a: pl.BlockSpec((1, 1, Tc, a.shape[-1]), lambda b, h: (b, h, 0, 0))
    out = pl.pallas_call(
        functools.partial(_ctx_attn_kernel, scale=scale, use_sink=sink is not None),
        grid=(B, H),
        in_specs=[blk(q), blk(k), blk(v), pl.BlockSpec(memory_space=pltpu.SMEM)],
        out_specs=blk(v),
        out_shape=jax.ShapeDtypeStruct(v.shape, F32),
    )(q, k, v, jnp.zeros((H,), F32) if sink is None else sink.astype(F32))
    return jnp.swapaxes(out, 1, 2).reshape(B, Tc, -1)
NA_SPAN = NA_ROWS * GRID_W


def _head_mask(width):
    rows = lax.broadcasted_iota(jnp.int32, (GROUP_HEADS * width, GROUP_WIDTH), 0) // width
    cols = lax.broadcasted_iota(jnp.int32, (GROUP_HEADS * width, GROUP_WIDTH), 1) // HEAD_DIM
    return (rows == cols).astype(F32)


def _na_kernel(q_ref, k_ref, v_ref, kc_ref, vc_ref, bias_ref, o_ref):
    r = pl.program_id(1)
    rows = pl.num_programs(1)
    rs = jnp.clip(r - NA_ROWS // 2, 0, rows - NA_ROWS)
    start = pl.multiple_of(rs * GRID_W, GRID_W)
    kw = k_ref[0, pl.ds(start, NA_SPAN), :]
    vw = v_ref[0, pl.ds(start, NA_SPAN), :]
    hm = _head_mask(GRID_W)
    q = q_ref[0] * (HEAD_DIM ** -0.5)
    q4 = (jnp.concatenate([q] * GROUP_HEADS, axis=0) * hm).astype(BF16)
    s_loc = lax.dot_general(q4, kw, NT_DIMS, preferred_element_type=F32) + bias_ref[rs - r + NA_ROWS - 1]
    s_ctx = lax.dot_general(q4, kc_ref[0], NT_DIMS, preferred_element_type=F32)
    m = jnp.maximum(jnp.max(s_loc, axis=-1, keepdims=True), jnp.max(s_ctx, axis=-1, keepdims=True))
    p_loc = jnp.exp(s_loc - m)
    p_ctx = jnp.exp(s_ctx - m)
    l = jnp.sum(p_loc, axis=-1, keepdims=True) + jnp.sum(p_ctx, axis=-1, keepdims=True)
    o = (jnp.dot(p_loc.astype(BF16), vw, preferred_element_type=F32)
         + jnp.dot(p_ctx.astype(BF16), vc_ref[0], preferred_element_type=F32)) * (hm / l)
    o_ref[0] = sum(o[h * GRID_W:(h + 1) * GRID_W] for h in range(GROUP_HEADS))


def _na_bias_table(rpb):
    c = np.arange(GRID_W)
    col_start = np.clip(c - NA_COLS // 2, 0, GRID_W - NA_COLS)
    valid = (c[None, :] >= col_start[:, None]) & (c[None, :] < col_start[:, None] + NA_COLS)
    dc = np.clip(c[None, :] - c[:, None] + NA_COLS - 1, 0, 2 * NA_COLS - 2)
    dr = np.arange(NA_ROWS)[:, None] + np.arange(NA_ROWS)[None, :]
    t = rpb.astype(F32)[:, dr][..., dc]
    t = jnp.where(valid[None, None, None], t, -jnp.inf)
    return jnp.transpose(t, (1, 0, 3, 2, 4)).reshape(NA_ROWS, GROUP_HEADS * GRID_W, NA_SPAN)


def neighbourhood_attention(q, k, v, kc, vc, rpb):
    B, T, C = q.shape
    rows = T // GRID_W
    n_ctx = kc.shape[1]
    bias = _na_bias_table(rpb)
    full = lambda n: pl.BlockSpec((1, n, C), lambda b, r: (b, 0, 0))
    return pl.pallas_call(
        _na_kernel,
        grid=(B, rows),
        in_specs=[pl.BlockSpec((1, GRID_W, C), lambda b, r: (b, r, 0)),
                  full(T), full(T), full(n_ctx), full(n_ctx),
                  pl.BlockSpec(bias.shape, lambda b, r: (0, 0, 0))],
        out_specs=pl.BlockSpec((1, GRID_W, C), lambda b, r: (b, r, 0)),
        out_shape=jax.ShapeDtypeStruct((B, T, C), F32),
        compiler_params=pltpu.CompilerParams(vmem_limit_bytes=VMEM_LIMIT_BYTES),
    )(q, k.astype(BF16), v.astype(BF16), kc.astype(BF16), vc.astype(BF16), bias)


def short_conv(a, w):
    T = a.shape[1]
    pad = w.shape[0] // 2
    ap = jnp.pad(a, ((0, 0), (pad, pad), (0, 0)))
    out = ap[:, :T] * w[0]
    for j in range(1, w.shape[0]):
        out = out + ap[:, j:j + T] * w[j]
    return out


ML_CHUNKS_PER_STEP = CTX_LEN // ML_CHUNK


def _bmm(a, b, contract):
    return lax.dot_general(a.astype(BF16), b.astype(BF16), (contract, ((0,), (0,))),
                           preferred_element_type=F32)


def _mlstm_chunk(qt, kt, vt, irow, brow, state, backward):
    L = ML_CHUNK
    C, nrow, m = state
    row = lax.broadcasted_iota(jnp.int32, (1, L, L), 1)
    col = lax.broadcasted_iota(jnp.int32, (1, L, L), 2)
    seen = (row <= col) if backward else (row >= col)
    eye = row == col

    def as_col(r):
        return jnp.sum(jnp.where(eye, r, 0.0), axis=2, keepdims=True)

    blast = brow[:, :, 0:1] if backward else brow[:, :, L - 1:L]
    rrow = brow - irow
    bcol = as_col(brow)
    d_log = jnp.where(seen, bcol - rrow, -jnp.inf)
    inter = bcol + m
    m_t = jnp.maximum(inter, jnp.max(d_log, axis=2, keepdims=True))
    w = jnp.exp(d_log - m_t)
    a = jnp.exp(inter - m_t)
    s = _bmm(qt, kt, ((2,), (2,))) * w
    num = _bmm(s, vt, ((2,), (1,))) + a * _bmm(qt, C, ((2,), (1,)))
    den = jnp.sum(s, axis=2, keepdims=True) + a * jnp.sum(qt * nrow, axis=2, keepdims=True)
    h = num / jnp.maximum(jnp.abs(den), jnp.exp(-m_t))
    g = blast - rrow
    m_new = jnp.maximum(blast + m, jnp.max(g, axis=2, keepdims=True))
    kw = kt * as_col(jnp.exp(g - m_new))
    decay = jnp.exp(blast + m - m_new)
    C = decay * C + _bmm(jnp.swapaxes(kw, 1, 2), vt, ((2,), (1,)))
    nrow = decay * nrow + jnp.sum(kw, axis=1, keepdims=True)
    return h, (C, nrow, m_new)


def _mlstm_kernel(qf_ref, kf_ref, vf_ref, if_ref, bf_ref, qb_ref, kb_ref, vb_ref, ib_ref, bb_ref,
                  hf_ref, hb_ref, c_ref, n_ref, m_ref):
    N, L = qf_ref.shape[0], ML_CHUNK

    @pl.when(pl.program_id(0) == 0)
    def _():
        c_ref[...] = jnp.zeros(c_ref.shape, F32)
        n_ref[...] = jnp.zeros(n_ref.shape, F32)
        m_ref[...] = jnp.zeros(m_ref.shape, F32)

    fwd = (c_ref[:N], n_ref[:N], m_ref[:N])
    bwd = (c_ref[N:], n_ref[N:], m_ref[N:])
    for c in range(ML_CHUNKS_PER_STEP):
        rows = slice(c * L, (c + 1) * L)
        h, fwd = _mlstm_chunk(qf_ref[:, rows, :], kf_ref[:, rows, :], vf_ref[:, rows, :],
                              if_ref[:, 0, c:c + 1, :], bf_ref[:, 0, c:c + 1, :], fwd, False)
        hf_ref[:, rows, :] = h
        cb = ML_CHUNKS_PER_STEP - 1 - c
        rows = slice(cb * L, (cb + 1) * L)
        h, bwd = _mlstm_chunk(qb_ref[:, rows, :], kb_ref[:, rows, :], vb_ref[:, rows, :],
                              ib_ref[:, 0, cb:cb + 1, :], bb_ref[:, 0, cb:cb + 1, :], bwd, True)
        hb_ref[:, rows, :] = h
    for i, ref in enumerate((c_ref, n_ref, m_ref)):
        ref[:N] = fwd[i]
        ref[N:] = bwd[i]


def mlstm_scan(q, k, v, gates_f, gates_b, n_ctx):
    B, T, H, d = q.shape
    CB, L = ML_CHUNKS_PER_STEP, ML_CHUNK
    assert n_ctx == CB * L and T % (CB * L) == 0
    N, steps = B * H, T // (CB * L)
    hm = lambda a: jnp.swapaxes(a, 1, 2).reshape(N, T, d)
    gates = lambda a: jnp.swapaxes(a, 1, 2).reshape(N, steps, CB, L)
    chunked = lambda a: a.reshape(B, T // L, L, H)
    b_f = jnp.cumsum(chunked(gates_f[1]), axis=2).reshape(B, T, H)
    b_b = lax.cumsum(chunked(gates_b[1]), axis=2, reverse=True).reshape(B, T, H)
    back = lambda j: jnp.where(j == 0, 0, steps - j)
    seq_f = pl.BlockSpec((N, CB * L, d), lambda j: (0, j, 0))
    seq_b = pl.BlockSpec((N, CB * L, d), lambda j: (0, back(j), 0))
    gate_f = pl.BlockSpec((N, 1, CB, L), lambda j: (0, j, 0, 0))
    gate_b = pl.BlockSpec((N, 1, CB, L), lambda j: (0, back(j), 0, 0))
    qh, kh, vh = hm(q), hm(k), hm(v)
    hf, hb = pl.pallas_call(
        _mlstm_kernel,
        grid=(steps,),
        in_specs=[seq_f, seq_f, seq_f, gate_f, gate_f, seq_b, seq_b, seq_b, gate_b, gate_b],
        out_specs=[seq_f, seq_b],
        out_shape=[jax.ShapeDtypeStruct((N, T, d), F32)] * 2,
        scratch_shapes=[pltpu.VMEM((2 * N, d, d), F32), pltpu.VMEM((2 * N, 1, d), F32),
                        pltpu.VMEM((2 * N, 1, 1), F32)],
        compiler_params=pltpu.CompilerParams(vmem_limit_bytes=VMEM_LIMIT_BYTES),
    )(qh, kh, vh, gates(gates_f[0]), gates(b_f), qh, kh, vh, gates(gates_b[0]), gates(b_b))
    return jnp.swapaxes((hf + hb).reshape(B, H, T, d), 1, 2)


def mlstm_prep(qk, v, gates, conv_w, gate_b):
    qk = jax.nn.silu(short_conv(qk, conv_w))
    q, k = jnp.split(qk, 2, axis=-1)
    g = (gates + gate_b).astype(F32)
    i_f, f_f, i_b, f_b = jnp.split(g, 4, axis=-1)
    return (heads(q, GROUP_HEADS) * HEAD_DIM ** -0.5, heads(k, GROUP_HEADS), heads(v, GROUP_HEADS),
            (i_f, jax.nn.log_sigmoid(f_f), i_b, jax.nn.log_sigmoid(f_b)))


def mlstm_mixer(lat, ctx, conv_w, gate_b):
    ql, kl, vl, gl = mlstm_prep(lat[0], lat[1], lat[2], conv_w, gate_b)
    qc, kc, vc, gc = mlstm_prep(ctx[0], ctx[1], ctx[2], conv_w, gate_b)
    Tc = qc.shape[1]
    cat = lambda c_, l_: jnp.concatenate([c_, l_], axis=1)
    h = mlstm_scan(cat(qc, ql), cat(kc, kl), cat(vc, vl),
                   (cat(gc[0], gl[0]), cat(gc[1], gl[1])), (cat(gc[2], gl[2]), cat(gc[3], gl[3])), Tc)
    return h[:, Tc:], h[:, :Tc]


MLA_KR = 9
ROPE_PARTNER = np.concatenate([np.arange(q_, q_ + MLA_ROPE // 4) for q_ in
                               (MLA_ROPE // 4, 0, 3 * MLA_ROPE // 4, MLA_ROPE // 2)])


def mla_rope_tables(T):
    ang_r, ang_c = axial_angles(T, MLA_ROPE)
    cos = jnp.concatenate([jnp.cos(ang_r)] * 2 + [jnp.cos(ang_c)] * 2, axis=1)
    sin = jnp.concatenate([-jnp.sin(ang_r), jnp.sin(ang_r), -jnp.sin(ang_c), jnp.sin(ang_c)], axis=1)
    pad = LANES - MLA_NOPE - MLA_ROPE
    return (jnp.concatenate([jnp.ones((T, MLA_NOPE), F32), cos, jnp.zeros((T, pad), F32)], axis=1),
            jnp.concatenate([jnp.zeros((T, MLA_NOPE), F32), sin, jnp.zeros((T, pad), F32)], axis=1))


def _mla_qkv_kernel(cq_ref, ckv_ref, kr_ref, cos_ref, sin_ref, qn_ref, kvn_ref, wq_ref, wkv_ref,
                    q_ref, k_ref, v_ref):
    def up(x, g, w_ref):
        y = x * lax.rsqrt(jnp.mean(x * x, axis=-1, keepdims=True) + EPS) * g
        return jnp.dot(y.astype(BF16), w_ref[...], preferred_element_type=F32)

    W = GROUP_HEADS * LANES
    per_head = lambda a: jnp.concatenate([a] * GROUP_HEADS, axis=1)
    cos, sin = cos_ref[...], sin_ref[...]
    q2 = up(cq_ref[...], qn_ref[...], wq_ref)
    q_ref[...] = (q2[:, :W] * per_head(cos) + q2[:, W:] * per_head(sin)).astype(BF16)
    kv = up(ckv_ref[...], kvn_ref[...], wkv_ref)
    kr = kr_ref[...]
    k_rope = kr[:, :LANES] * cos + kr[:, LANES:] * sin
    k_ref[...] = (kv[:, :W] + per_head(k_rope)).astype(BF16)
    v_ref[...] = kv[:, W:].astype(BF16)


def mla_qkv(cq, ckv, kr2, q_norm, w_uq, kv_norm, w_ukv, cos, sin):
    N = cq.shape[0]
    P = cos.shape[0]
    tm = min(PROJ_TOKENS, P)
    H, dqk = GROUP_HEADS, MLA_NOPE + MLA_ROPE
    blocks = lambda w_, lo, n: jnp.pad(w_.reshape(w_.shape[0], H, -1)[:, :, lo:lo + n],
                                      ((0, 0), (0, 0), (0, LANES - n))).reshape(w_.shape[0], H * LANES)
    wq = w_uq.reshape(w_uq.shape[0], H, dqk)
    wq_partner = jnp.pad(wq[:, :, MLA_NOPE + ROPE_PARTNER], ((0, 0), (0, 0), (MLA_NOPE, LANES - dqk)))
    wq2 = jnp.concatenate([blocks(w_uq, 0, dqk), wq_partner.reshape(-1, H * LANES)], axis=1).astype(BF16)
    wkv2 = jnp.concatenate([blocks(w_ukv, 0, MLA_NOPE), blocks(w_ukv, MLA_NOPE, MLA_V)], axis=1).astype(BF16)
    tok = lambda w_: pl.BlockSpec((tm, w_), lambda i: (i, 0))
    pos = pl.BlockSpec((tm, LANES), lambda i: (i % (P // tm), 0))
    const = lambda a: pl.BlockSpec(a.shape, lambda i: (0, 0))
    args = (cq, ckv, kr2, cos, sin, q_norm[None], kv_norm[None], wq2, wkv2)
    out = jax.ShapeDtypeStruct((N, H * LANES), BF16)
    return pl.pallas_call(
        _mla_qkv_kernel,
        grid=(N // tm,),
        in_specs=[tok(cq.shape[1]), tok(ckv.shape[1]), tok(kr2.shape[1]), pos, pos] + [const(a) for a in args[5:]],
        out_specs=[tok(H * LANES)] * 3,
        out_shape=[out] * 3,
    )(*args)


LOG2_E = 1.4426950408889634
DENSE_Q_TILE = 1024
DENSE_Q_SUB = 256
DENSE_Q_UNROLL = 4
DENSE_K_TILE_MAX = 8320


def _dense_attn_kernel(q_ref, k_ref, v_ref, o_ref, m_ref, l_ref, acc_ref, *, scale, dv):
    h, j = pl.program_id(2), pl.program_id(3)

    @pl.when(j == 0)
    def _():
        m_ref[...] = jnp.full(m_ref.shape, -jnp.inf, F32)
        l_ref[...] = jnp.zeros(l_ref.shape, F32)
        acc_ref[...] = jnp.zeros(acc_ref.shape, F32)

    def rows(i, carry):
        for u in range(DENSE_Q_UNROLL):
            r = pl.ds(pl.multiple_of((i * DENSE_Q_UNROLL + u) * DENSE_Q_SUB, DENSE_Q_SUB), DENSE_Q_SUB)
            s = lax.dot_general(q_ref[0, r, :], k_ref[0], NT_DIMS,
                                preferred_element_type=F32) * (scale * LOG2_E)
            m_prev = m_ref[r, :]
            m_new = jnp.maximum(m_prev, jnp.max(s, axis=-1, keepdims=True))
            alpha = jnp.exp2(m_prev - m_new)
            p = jnp.exp2(s - m_new)
            l_ref[r, :] = alpha * l_ref[r, :] + jnp.sum(p, axis=-1, keepdims=True)
            acc_ref[r, :] = alpha * acc_ref[r, :] + jnp.dot(p.astype(BF16), v_ref[0],
                                                            preferred_element_type=F32)
            m_ref[r, :] = m_new
        return carry

    lax.fori_loop(0, q_ref.shape[1] // (DENSE_Q_SUB * DENSE_Q_UNROLL), rows, 0)

    for hh in range(o_ref.shape[2] // dv):
        @pl.when((j == pl.num_programs(3) - 1) & (h == hh))
        def _():
            o_ref[0, :, hh * dv:(hh + 1) * dv] = (acc_ref[...] / l_ref[...])[:, :dv]


def dense_attention(q, k_all, v_all, scale, dv):
    B, T, C = q.shape
    H, NK = C // LANES, k_all.shape[1]
    tq = min(DENSE_Q_TILE, T)
    tk = max(t for t in range(LANES, DENSE_K_TILE_MAX + 1, LANES) if NK % t == 0)
    return pl.pallas_call(
        functools.partial(_dense_attn_kernel, scale=scale, dv=dv),
        grid=(B, T // tq, H, NK // tk),
        in_specs=[pl.BlockSpec((1, tq, LANES), lambda b, i, h, j: (b, i, h)),
                  pl.BlockSpec((1, tk, LANES), lambda b, i, h, j: (b, j, h)),
                  pl.BlockSpec((1, tk, LANES), lambda b, i, h, j: (b, j, h))],
        out_specs=pl.BlockSpec((1, tq, H * dv), lambda b, i, h, j: (b, i, 0)),
        out_shape=jax.ShapeDtypeStruct((B, T, H * dv), F32),
        scratch_shapes=[pltpu.VMEM((tq, 1), F32), pltpu.VMEM((tq, 1), F32), pltpu.VMEM((tq, LANES), F32)],
        compiler_params=pltpu.CompilerParams(vmem_limit_bytes=VMEM_LIMIT_BYTES),
    )(q, k_all, v_all)


SWA_SPAN = ATTN_BLOCK + 2 * SWA_WINDOW


def _swa_kernel(q_ref, k_ref, v_ref, kc_ref, vc_ref, sink_ref, o_ref):
    n = pl.program_id(1)
    T = k_ref.shape[1]
    start = pl.multiple_of(jnp.clip(n * ATTN_BLOCK - SWA_WINDOW, 0, T - SWA_SPAN), ATTN_BLOCK)
    kw = k_ref[0, pl.ds(start, SWA_SPAN), :]
    vw = v_ref[0, pl.ds(start, SWA_SPAN), :]
    hm = _head_mask(ATTN_BLOCK)
    q = q_ref[0] * (HEAD_DIM ** -0.5)
    q4 = (jnp.concatenate([q] * GROUP_HEADS, axis=0) * hm).astype(BF16)
    rows = GROUP_HEADS * ATTN_BLOCK
    q_pos = n * ATTN_BLOCK + lax.broadcasted_iota(jnp.int32, (rows, SWA_SPAN), 0) % ATTN_BLOCK
    k_pos = start + lax.broadcasted_iota(jnp.int32, (rows, SWA_SPAN), 1)
    s_loc = lax.dot_general(q4, kw, NT_DIMS, preferred_element_type=F32)
    s_loc = jnp.where(jnp.abs(q_pos - k_pos) <= SWA_WINDOW, s_loc, -jnp.inf)
    s_ctx = lax.dot_general(q4, kc_ref[0], NT_DIMS, preferred_element_type=F32)
    sink = sink_ref[...]
    m = jnp.maximum(jnp.maximum(jnp.max(s_loc, axis=-1, keepdims=True),
                                jnp.max(s_ctx, axis=-1, keepdims=True)), sink)
    p_loc = jnp.exp(s_loc - m)
    p_ctx = jnp.exp(s_ctx - m)
    l = jnp.sum(p_loc, axis=-1, keepdims=True) + jnp.sum(p_ctx, axis=-1, keepdims=True) + jnp.exp(sink - m)
    o = (jnp.dot(p_loc.astype(BF16), vw, preferred_element_type=F32)
         + jnp.dot(p_ctx.astype(BF16), vc_ref[0], preferred_element_type=F32)) * (hm / l)
    o_ref[0] = sum(o[h * ATTN_BLOCK:(h + 1) * ATTN_BLOCK] for h in range(GROUP_HEADS))


def window_attention(q, k, v, kc, vc, sink):
    B, T, H, d = q.shape
    G = H // k.shape[2]
    n_ctx = kc.shape[1]
    C = H * d
    rep = lambda a: jnp.repeat(a, G, axis=2).reshape(a.shape[0], a.shape[1], C).astype(BF16)
    sink_rows = jnp.repeat(sink.astype(F32), ATTN_BLOCK).reshape(H * ATTN_BLOCK, 1)
    full = lambda n: pl.BlockSpec((1, n, C), lambda b, i: (b, 0, 0))
    return pl.pallas_call(
        _swa_kernel,
        grid=(B, T // ATTN_BLOCK),
        in_specs=[pl.BlockSpec((1, ATTN_BLOCK, C), lambda b, i: (b, i, 0)),
                  full(T), full(T), full(n_ctx), full(n_ctx),
                  pl.BlockSpec(sink_rows.shape, lambda b, i: (0, 0))],
        out_specs=pl.BlockSpec((1, ATTN_BLOCK, C), lambda b, i: (b, i, 0)),
        out_shape=jax.ShapeDtypeStruct((B, T, C), F32),
        compiler_params=pltpu.CompilerParams(vmem_limit_bytes=VMEM_LIMIT_BYTES),
    )(q.reshape(B, T, C), rep(k), rep(v), rep(kc), rep(vc), sink_rows)


BF16 = jnp.bfloat16
LANES = 128
SUBLANES = 8
ROW_SEGS = D_MODEL // LANES
ROW_WORDS = ROW_SEGS // 2
PEER_PICKS = PEER_HEADS * PEER_TOPK
PEER_TOPK_TOKENS = 512
PEER_GATHER_TOKENS = 128
PEER_ACT_UNROLL = SUBLANES
VMEM_LIMIT_BYTES = 56 * 1024 * 1024


def _split_bf16(x, parts):
    out = []
    for _ in range(parts):
        p = x.astype(BF16)
        out.append(p)
        x = x - p.astype(F32)
    return out


def _topk_rows(s, k):
    n = s.shape[0]
    iota = lax.broadcasted_iota(jnp.int32, s.shape, 0)
    vals, idxs = [], []
    for _ in range(k):
        m = jnp.max(s, axis=0, keepdims=True)
        i = jnp.min(jnp.where(s == m, iota, n), axis=0, keepdims=True)
        vals.append(m)
        idxs.append(i)
        s = jnp.where(iota == i, -jnp.inf, s)
    return jnp.concatenate(vals, axis=0), jnp.concatenate(idxs, axis=0)


def _peer_topk_kernel(x_ref, wq_ref, keys_ref, eidx_ref, gate_ref):
    xb = x_ref[...].astype(BF16)
    q = jnp.dot(xb, wq_ref[...], preferred_element_type=F32)
    nt = (((1,), (1,)), ((), ()))
    sv, si = [], []
    for p in range(2):
        qp = q[:, p * PEER_DKEY:(p + 1) * PEER_DKEY].astype(BF16)
        s = lax.dot_general(keys_ref[0, p], qp, nt, preferred_element_type=F32)
        v_, i_ = _topk_rows(s, PEER_TOPK)
        sv.append(v_)
        si.append(i_)
    cs, ce = [], []
    half = PEER_TOPK // 2
    for a in range(half):
        nb = PEER_TOPK if a == 0 else half
        cs.append(sv[0][a:a + 1] + sv[1][:nb])
        ce.append(si[0][a:a + 1] * PEER_NKEYS + si[1][:nb])
    cs.append(sv[0][half:] + sv[1][0:1])
    ce.append(si[0][half:] * PEER_NKEYS + si[1][0:1])
    cand_s = jnp.concatenate(cs, axis=0)
    cand_e = jnp.concatenate(ce, axis=0)
    fs, fpos = _topk_rows(cand_s, PEER_TOPK)
    iota = lax.broadcasted_iota(jnp.int32, cand_e.shape, 0)
    eidx = [jnp.max(jnp.where(iota == fpos[j:j + 1], cand_e, -1), axis=0, keepdims=True)
            for j in range(PEER_TOPK)]
    ex = jnp.exp(fs - fs[0:1])
    eidx_ref[0] = jnp.concatenate(eidx, axis=0)
    gate_ref[0] = ex / jnp.sum(ex, axis=0, keepdims=True)


def peer_topk(h, wq, sub_keys):
    N, D = h.shape
    T = PEER_TOPK_TOKENS
    wqb = wq.astype(BF16)
    kb = sub_keys.astype(BF16)
    eidx, gate = pl.pallas_call(
        _peer_topk_kernel,
        grid=(N // T, PEER_HEADS),
        in_specs=[pl.BlockSpec((T, D), lambda i, h_: (i, 0)),
                  pl.BlockSpec((D, 2 * PEER_DKEY), lambda i, h_: (0, h_)),
                  pl.BlockSpec((1, 2, PEER_NKEYS, PEER_DKEY), lambda i, h_: (h_, 0, 0, 0))],
        out_specs=[pl.BlockSpec((1, PEER_TOPK, T), lambda i, h_: (h_, 0, i)),
                   pl.BlockSpec((1, PEER_TOPK, T), lambda i, h_: (h_, 0, i))],
        out_shape=[jax.ShapeDtypeStruct((PEER_HEADS, PEER_TOPK, N), jnp.int32),
                   jax.ShapeDtypeStruct((PEER_HEADS, PEER_TOPK, N), F32)],
        compiler_params=pltpu.CompilerParams(vmem_limit_bytes=VMEM_LIMIT_BYTES),
    )(h, wqb, kb)
    return eidx.reshape(PEER_PICKS, N), gate.reshape(PEER_PICKS, N)


def pack_expert_table(tab):
    E, D = tab.shape
    return pl.pallas_call(
        _pack_table_kernel,
        grid=(E // PACK_ROWS,),
        in_specs=[pl.BlockSpec((PACK_ROWS, D), lambda i: (i, 0))],
        out_specs=pl.BlockSpec((PACK_ROWS * ROW_WORDS, LANES), lambda i: (i, 0)),
        out_shape=jax.ShapeDtypeStruct((E * ROW_WORDS, LANES), jnp.uint32),
    )(tab)


PACK_ROWS = 512


def _pack_table_kernel(t_ref, o_ref):
    bits = lambda a: lax.bitcast_convert_type(a.astype(BF16).astype(F32), jnp.uint32)
    for s_ in range(ROW_WORDS):
        lo = bits(t_ref[:, (2 * s_) * LANES:(2 * s_ + 1) * LANES])
        hi = bits(t_ref[:, (2 * s_ + 1) * LANES:(2 * s_ + 2) * LANES])
        o_ref[pl.ds(s_, PACK_ROWS, stride=ROW_WORDS), :] = (hi & jnp.uint32(0xFFFF0000)) | (lo >> 16)


def _stage_rows(idx_ref, tab_ref, stage_ref, t):
    for k in range(PEER_PICKS):
        off = pl.multiple_of(idx_ref[t, k], ROW_WORDS)
        stage_ref[k * ROW_WORDS:(k + 1) * ROW_WORDS, :] = tab_ref[pl.ds(off, ROW_WORDS), :]
    return pltpu.bitcast(stage_ref[...], BF16)


def _peer_act_kernel(idx_ref, x_ref, gate_ref, tab_ref, seg_mask_ref, group_ref, w_ref,
                     stage_ref, rows_ref):
    T = x_ref.shape[0]
    U = stage_ref.shape[0]

    sub = lax.broadcasted_iota(jnp.int32, (SUBLANES, PEER_PICKS * ROW_SEGS), 0)

    def tokens(g, carry):
        tile = jnp.zeros((SUBLANES, PEER_PICKS * ROW_SEGS), F32)
        for j in range(U):
            t = g * U + j
            sb = _stage_rows(idx_ref, tab_ref, stage_ref.at[j], t)
            xs = jnp.concatenate(_split_bf16(x_ref[t], 2), axis=0)
            r = lax.dot_general(xs, sb, NT_DIMS, preferred_element_type=F32)
            r = jnp.sum(r * seg_mask_ref[...], axis=0, keepdims=True)
            tile = jnp.where(sub == j, r, tile)
        rows_ref[g] = tile
        return carry

    lax.fori_loop(0, T // U, tokens, 0)
    rows = rows_ref[...].reshape(T, PEER_PICKS * ROW_SEGS)
    act = jnp.zeros((T, PEER_PICKS), F32)
    for piece in _split_bf16(rows, 3):
        act = act + jnp.dot(piece, group_ref[...], preferred_element_type=F32)
    w_ref[...] = gate_ref[...] * (0.5 * act * (1.0 + lax.erf(act * (2.0 ** -0.5))))


def _peer_out_kernel(idx_ref, w_ref, x_ref, g_ref, tab_ref, expand_ref, seg_mask_ref, f_ref, stage_ref):
    T = w_ref.shape[0]
    U = stage_ref.shape[0]

    def tokens(g, carry):
        w8 = w_ref[pl.ds(pl.multiple_of(g * U, U), U), :]
        hi, lo = _split_bf16(w8, 2)
        lhs = jnp.concatenate([jnp.broadcast_to(p[j:j + 1], (SUBLANES, PEER_PICKS))
                               for j in range(U) for p in (hi, lo)], axis=0)
        wrep = jnp.dot(lhs, expand_ref[...], preferred_element_type=F32)
        for j in range(U):
            t = g * U + j
            sb = _stage_rows(idx_ref, tab_ref, stage_ref.at[j], t)
            wsel = (wrep[j * 2 * SUBLANES:(j + 1) * 2 * SUBLANES] * seg_mask_ref[...]).astype(BF16)
            o = jnp.dot(wsel, sb, preferred_element_type=F32)
            f_ref[t] = x_ref[t] + g_ref[0] * (o[:SUBLANES] + o[SUBLANES:])
        return carry

    lax.fori_loop(0, T // U, tokens, 0)


def _peer_constants():
    cols = np.arange(PEER_PICKS * ROW_SEGS)
    seg_mask = (cols[None, :] % ROW_SEGS == np.arange(2 * SUBLANES)[:, None] % SUBLANES)
    group = (cols[:, None] // ROW_SEGS == np.arange(PEER_PICKS)[None, :])
    return (jnp.asarray(seg_mask, F32), jnp.asarray(group, BF16), jnp.asarray(group.T, BF16))


def peer_ffn(h, x, gate2, group_tokens, wq, sub_keys, u_packed, v_packed):
    N, D = h.shape
    T = PEER_GATHER_TOKENS
    eidx, gate = peer_topk(h, wq, sub_keys)
    seg_mask, group, expand = _peer_constants()
    rows3 = lambda a: a.reshape(a.shape[0], ROW_SEGS, LANES)
    offs = eidx.T * ROW_WORDS
    idx_spec = pl.BlockSpec((T, PEER_PICKS), lambda i: (i, 0), memory_space=pltpu.SMEM)
    tab_spec = pl.BlockSpec(u_packed.shape, lambda i: (0, 0), pipeline_mode=pl.Buffered(1))
    tok_spec = pl.BlockSpec((T, ROW_SEGS, LANES), lambda i: (i, 0, 0))
    const = lambda shape: pl.BlockSpec(shape, lambda i: (0, 0))
    params = pltpu.CompilerParams(vmem_limit_bytes=VMEM_LIMIT_BYTES)
    w = pl.pallas_call(
        _peer_act_kernel,
        grid=(N // T,),
        in_specs=[idx_spec, tok_spec,
                  pl.BlockSpec((T, PEER_PICKS), lambda i: (i, 0)),
                  tab_spec, const(seg_mask.shape), const(group.shape)],
        out_specs=pl.BlockSpec((T, PEER_PICKS), lambda i: (i, 0)),
        out_shape=jax.ShapeDtypeStruct((N, PEER_PICKS), F32),
        scratch_shapes=[pltpu.VMEM((PEER_ACT_UNROLL, PEER_PICKS * ROW_WORDS, LANES), jnp.uint32),
                        pltpu.VMEM((T // SUBLANES, SUBLANES, PEER_PICKS * ROW_SEGS), F32)],
        compiler_params=params,
    )(offs, rows3(h), gate.T, u_packed, seg_mask, group)
    out = pl.pallas_call(
        _peer_out_kernel,
        grid=(N // T,),
        in_specs=[idx_spec,
                  pl.BlockSpec((T, PEER_PICKS), lambda i: (i, 0)),
                  tok_spec,
                  pl.BlockSpec((1, ROW_SEGS, LANES), lambda i: (i // (group_tokens // T), 0, 0)),
                  tab_spec, const(expand.shape), const(seg_mask.shape)],
        out_specs=tok_spec,
        out_shape=jax.ShapeDtypeStruct((N, ROW_SEGS, LANES), F32),
        scratch_shapes=[pltpu.VMEM((SUBLANES, PEER_PICKS * ROW_WORDS, LANES), jnp.uint32)],
        compiler_params=params,
    )(offs, w, rows3(x), rows3(gate2), v_packed, expand, seg_mask)
    return out.reshape(N, D)


PROJ_TOKENS = 512
MOD_ROWS = SUBLANES
IN_ALIGNED = tuple(i for i, s_ in enumerate(IN_SIZES) if s_ % LANES == 0)
IN_SMALL = tuple(i for i, s_ in enumerate(IN_SIZES) if s_ % LANES)


def _rms_modulate(x, gain, scale1p, shift):
    r = lax.rsqrt(jnp.mean(x * x, axis=-1, keepdims=True) + EPS)
    return (x * r * gain) * scale1p + shift


def _in_proj_kernel(x_ref, mod_ref, w_ref, *out_refs):
    mod = mod_ref[0]
    h = _rms_modulate(x_ref[...], mod[0:1], mod[1:2], mod[2:3])
    y = jnp.dot(h.astype(BF16), w_ref[...], preferred_element_type=F32)
    off = 0
    for o_ref in out_refs:
        o_ref[...] = y[:, off:off + o_ref.shape[1]].astype(o_ref.dtype)
        off += o_ref.shape[1]


def in_projection(x, mod, w_in, group_tokens):
    N, D = x.shape
    T = min(PROJ_TOKENS, group_tokens)
    starts = np.cumsum((0,) + IN_SIZES)
    group_cols = lambda i: np.arange(starts[i], starts[i + 1])
    small = [i for i in IN_SMALL if i != MLA_KR]
    n_small = sum(IN_SIZES[i] for i in small)
    lane_pad = lambda w_, lo, hi: jnp.pad(w_, ((0, 0), (lo, hi)))
    w_kr = w_in[:, group_cols(MLA_KR)]
    wp = jnp.concatenate(
        [w_in[:, np.concatenate([group_cols(i) for i in IN_ALIGNED])],
         lane_pad(w_in[:, np.concatenate([group_cols(i) for i in small])], 0, -n_small % LANES),
         lane_pad(w_kr, MLA_NOPE, LANES - MLA_NOPE - MLA_ROPE),
         lane_pad(w_kr[:, ROPE_PARTNER], MLA_NOPE, LANES - MLA_NOPE - MLA_ROPE)], axis=1).astype(BF16)
    widths = [IN_SIZES[i] for i in IN_ALIGNED] + [n_small + (-n_small % LANES), 2 * LANES]
    outs = pl.pallas_call(
        _in_proj_kernel,
        grid=(N // T,),
        in_specs=[pl.BlockSpec((T, D), lambda i: (i, 0)),
                  pl.BlockSpec((1, MOD_ROWS, D), lambda i: (i // (group_tokens // T), 0, 0)),
                  pl.BlockSpec(wp.shape, lambda i: (0, 0))],
        out_specs=[pl.BlockSpec((T, w_), lambda i: (i, 0)) for w_ in widths],
        out_shape=[jax.ShapeDtypeStruct((N, w_), F32) for w_ in widths],
        compiler_params=pltpu.CompilerParams(vmem_limit_bytes=VMEM_LIMIT_BYTES),
    )(x, mod, wp)
    groups = dict(zip(IN_ALIGNED, outs[:-2]))
    off = 0
    for i in small:
        groups[i] = outs[-2][:, off:off + IN_SIZES[i]]
        off += IN_SIZES[i]
    groups[MLA_KR] = outs[-1]
    return [groups[i] for i in range(len(IN_SIZES))]


def _out_proj_kernel(ya_ref, hl_ref, mo_ref, yc_ref, yd_ref, x_ref, mod_ref, w_ref, xo_ref, h2_ref):
    yb = hl_ref[...] * jax.nn.sigmoid(mo_ref[...])
    y = jnp.concatenate([ya_ref[...], yb, yc_ref[...], yd_ref[...]], axis=-1).astype(BF16)
    mod = mod_ref[0]
    xn = x_ref[...] + mod[0:1] * jnp.dot(y, w_ref[...], preferred_element_type=F32)
    xo_ref[...] = xn
    h2_ref[...] = _rms_modulate(xn, mod[1:2], mod[2:3], mod[3:4])


def out_projection(ya, hl, mo, yc, yd, x, mod, w_out, group_tokens):
    N, D = x.shape
    T = min(PROJ_TOKENS, group_tokens)
    part = pl.BlockSpec((T, GROUP_WIDTH), lambda i: (i, 0))
    tok = pl.BlockSpec((T, D), lambda i: (i, 0))
    return pl.pallas_call(
        _out_proj_kernel,
        grid=(N // T,),
        in_specs=[part, part, part, part, part, tok,
                  pl.BlockSpec((1, MOD_ROWS, D), lambda i: (i // (group_tokens // T), 0, 0)),
                  pl.BlockSpec(w_out.shape, lambda i: (0, 0))],
        out_specs=[tok, tok],
        out_shape=[jax.ShapeDtypeStruct((N, D), F32)] * 2,
        compiler_params=pltpu.CompilerParams(vmem_limit_bytes=VMEM_LIMIT_BYTES),
    )(ya, hl, mo, yc, yd, x, mod, w_out.astype(BF16))


def _adaln_kernel(c_ref, w_ref, b_ref, o_ref):
    c = c_ref[...]
    a = c * jax.nn.sigmoid(c)
    o_ref[...] = jnp.dot(a.astype(BF16), w_ref[...].astype(BF16), preferred_element_type=F32) + b_ref[...]


def adaln_linear(c, w_ada, b_ada):
    R, D = c.shape
    rows = -R % SUBLANES + R
    out = pl.pallas_call(
        _adaln_kernel,
        grid=(w_ada.shape[1] // D,),
        in_specs=[pl.BlockSpec((rows, D), lambda j: (0, 0)),
                  pl.BlockSpec((D, D), lambda j: (0, j)),
                  pl.BlockSpec((1, D), lambda j: (0, j))],
        out_specs=pl.BlockSpec((rows, D), lambda j: (0, j)),
        out_shape=jax.ShapeDtypeStruct((rows, w_ada.shape[1]), F32),
    )(jnp.pad(c, ((0, rows - R), (0, 0))), w_ada, b_ada[None])
    return out[:R]


def _mod_rows(*rows):
    m = jnp.stack([jnp.broadcast_to(r, rows[-1].shape) for r in rows], axis=1)
    return jnp.pad(m, ((0, 0), (0, MOD_ROWS - len(rows)), (0, 0)))


def hybrid_layer(x, xc, c, c_ctx, need_ctx, rope_mla, angs_swa,
                 norm1_g, norm2_g, w_ada, b_ada, w_in, na_rpb, ml_conv, ml_gate_b,
                 mla_q_norm, mla_w_uq, mla_kv_norm, mla_w_ukv, swa_sink, w_out,
                 peer_wq, peer_keys, peer_u, peer_v):
    B, T, D = x.shape
    Tc = xc.shape[1]
    H = GROUP_HEADS
    flat = lambda a: a.reshape(-1, a.shape[-1])
    ada = adaln_linear(jnp.concatenate([c, c_ctx[None]], axis=0), w_ada, b_ada)
    sh1, sc1, g1, sh2, sc2, g2 = jnp.split(ada[:B], 6, axis=-1)
    sh1c, sc1c, g1c, sh2c, sc2c, g2c = jnp.split(ada[B:], 6, axis=-1)
    lat = in_projection(flat(x), _mod_rows(norm1_g, 1.0 + sc1, sh1), w_in, T)
    cx = in_projection(flat(xc), _mod_rows(norm1_g, 1.0 + sc1c, sh1c), w_in, B * Tc)
    (na_q, na_k, na_v, ml_qk, ml_v, ml_o, ml_g,
     mla_cq, mla_ckv, mla_kr, sw_q, sw_k, sw_v) = [a.reshape(B, T, -1) for a in lat]
    (na_qc, na_kc, na_vc, ml_qkc, ml_vc, ml_oc, ml_gc,
     mla_cqc, mla_ckvc, mla_krc, sw_qc, sw_kc, sw_vc) = [a.reshape(B, Tc, -1) for a in cx]
    attn_scale = HEAD_DIM ** -0.5
    mla_scale = (MLA_NOPE + MLA_ROPE) ** -0.5
    kc_a, vc_a = heads(na_kc, H), heads(na_vc, H)
    y_a = neighbourhood_attention(na_q, na_k, na_v, na_kc, na_vc, na_rpb)
    h_lat, h_ctx = mlstm_mixer((ml_qk, ml_v, ml_g), (ml_qkc, ml_vc, ml_gc), ml_conv, ml_gate_b)
    no_rope = (jnp.ones((Tc, LANES), F32), jnp.zeros((Tc, LANES), F32))
    q_m, k_m, v_m = [a.reshape(B, T, -1) for a in
                     mla_qkv(flat(mla_cq), flat(mla_ckv), flat(mla_kr), mla_q_norm, mla_w_uq, mla_kv_norm, mla_w_ukv,
                             *rope_mla)]
    qc_m, kc_m, vc_m = [a.reshape(B, Tc, -1) for a in
                        mla_qkv(flat(mla_cqc), flat(mla_ckvc), flat(mla_krc), mla_q_norm, mla_w_uq, mla_kv_norm,
                                mla_w_ukv, *no_rope)]
    y_c = dense_attention(q_m, jnp.concatenate([kc_m, k_m], axis=1), jnp.concatenate([vc_m, v_m], axis=1),
                          mla_scale, MLA_V)
    kc_d, vc_d = heads(sw_kc, SWA_KV_HEADS), heads(sw_vc, SWA_KV_HEADS)
    y_d = window_attention(rope_2d(heads(sw_q, H), angs_swa), rope_2d(heads(sw_k, SWA_KV_HEADS), angs_swa),
                           heads(sw_v, SWA_KV_HEADS), kc_d, vc_d, swa_sink)
    x2, h2 = out_projection(flat(y_a), h_lat.reshape(B * T, GROUP_WIDTH), flat(ml_o), flat(y_c), flat(y_d),
                            flat(x), _mod_rows(g1, norm2_g, 1.0 + sc2, sh2), w_out, T)
    u_packed, v_packed = pack_expert_table(peer_u), pack_expert_table(peer_v)
    x = peer_ffn(h2, x2, g2, T, peer_wq, peer_keys, u_packed, v_packed).reshape(B, T, D)
    if not need_ctx:
        return x, None
    xc2, h2c = out_projection(flat(ctx_attn(heads(na_qc, H), kc_a, vc_a, attn_scale)),
                              h_ctx.reshape(B * Tc, GROUP_WIDTH), flat(ml_oc),
                              ctx_attn(heads(qc_m, H), heads(kc_m, H), heads(vc_m, H), mla_scale)
                              .reshape(B, Tc, H, LANES)[..., :MLA_V].reshape(B * Tc, GROUP_WIDTH),
                              flat(ctx_attn(heads(sw_qc, H), kc_d, vc_d, attn_scale, swa_sink)),
                              flat(xc), _mod_rows(g1c, norm2_g, 1.0 + sc2c, sh2c), w_out, B * Tc)
    xc = peer_ffn(h2c, xc2, g2c, B * Tc, peer_wq, peer_keys, u_packed, v_packed).reshape(B, Tc, D)
    return x, xc


def _final_rmsnorm_kernel(x_ref, g_ref, o_ref):
    x = x_ref[...]
    o_ref[...] = x * lax.rsqrt(jnp.mean(x * x, axis=-1, keepdims=True) + EPS) * g_ref[...]


def final_rmsnorm(x, g):
    B, T, D = x.shape
    rows = 1024
    xf = x.reshape(B * T, D)
    out = pl.pallas_call(
        _final_rmsnorm_kernel,
        grid=(B * T // rows,),
        in_specs=[pl.BlockSpec((rows, D), lambda i: (i, 0)), pl.BlockSpec((1, D), lambda i: (0, 0))],
        out_specs=pl.BlockSpec((rows, D), lambda i: (i, 0)),
        out_shape=jax.ShapeDtypeStruct((B * T, D), x.dtype),
    )(xf, g.reshape(1, D))
    return out.reshape(B, T, D)


def kernel(x, c, ctx, c_ctx, norm1_g, norm2_g, w_ada, b_ada, w_in, na_rpb, ml_conv, ml_gate_b,
           mla_q_norm, mla_w_uq, mla_kv_norm, mla_w_ukv, swa_sink, w_out,
           peer_wq, peer_keys, peer_u, peer_v, final_norm_g):
    T = x.shape[1]
    rope_mla = mla_rope_tables(T)
    angs_swa = axial_angles(T, HEAD_DIM)
    xc = ctx
    for l in range(DEPTH):
        x, xc = hybrid_layer(x, xc, c, c_ctx, l < DEPTH - 1, rope_mla, angs_swa,
                             norm1_g[l], norm2_g[l], w_ada[l], b_ada[l], w_in[l], na_rpb[l],
                             ml_conv[l], ml_gate_b[l], mla_q_norm[l], mla_w_uq[l], mla_kv_norm[l],
                             mla_w_ukv[l], swa_sink[l], w_out[l], peer_wq[l], peer_keys[l],
                             peer_u[l], peer_v[l])
    return final_rmsnorm(x, final_norm_g)
```

```python
import functools

import jax
import jax.numpy as jnp
from jax import lax
import numpy as np
from jax.experimental import pallas as pl
from jax.experimental.pallas import tpu as pltpu

D_MODEL = 1024
DEPTH = 2

CTX_LEN = 256
GRID_W = 64
N_MIXERS = 4
MIX_WIDTH = D_MODEL
GROUP_WIDTH = MIX_WIDTH // N_MIXERS
GROUP_HEADS = 4
HEAD_DIM = GROUP_WIDTH // GROUP_HEADS
NA_ROWS = 8
NA_COLS = 16
ML_CHUNK = 64
MLA_Q_RANK = 256
MLA_KV_RANK = 128
MLA_NOPE = 64
MLA_ROPE = 32
MLA_V = 64
SWA_KV_HEADS = 2
SWA_WINDOW = 128
ATTN_BLOCK = 128
PEER_HEADS = 8
PEER_NKEYS = 128
PEER_DKEY = 128
PEER_TOPK = 16
ROPE_BASE = 10000.0
EPS = 1e-6
IN_SIZES = (GROUP_WIDTH, GROUP_WIDTH, GROUP_WIDTH,
            2 * GROUP_WIDTH, GROUP_WIDTH, GROUP_WIDTH, 4 * GROUP_HEADS,
            MLA_Q_RANK, MLA_KV_RANK, MLA_ROPE,
            GROUP_WIDTH, SWA_KV_HEADS * HEAD_DIM, SWA_KV_HEADS * HEAD_DIM)
F32 = jnp.float32


def heads(a, h):
    return a.reshape(a.shape[:-1] + (h, a.shape[-1] // h))


def axial_angles(T, rot_dim):
    t = jnp.arange(T)
    row = (t // GRID_W).astype(F32)
    col = (t % GRID_W).astype(F32)
    half = rot_dim // 2
    inv = 1.0 / (ROPE_BASE ** (jnp.arange(0, half, 2, dtype=F32) / half))
    return row[:, None] * inv, col[:, None] * inv


def rope_1d(x, ang):
    cos = jnp.cos(ang)[None, :, None, :]
    sin = jnp.sin(ang)[None, :, None, :]
    x1, x2 = jnp.split(x.astype(F32), 2, axis=-1)
    return jnp.concatenate([x1 * cos - x2 * sin, x1 * sin + x2 * cos], axis=-1)


def rope_2d(x, angs):
    xr, xc = jnp.split(x, 2, axis=-1)
    return jnp.concatenate([rope_1d(xr, angs[0]), rope_1d(xc, angs[1])], axis=-1).astype(x.dtype)


NT_DIMS = (((1,), (1,)), ((), ()))


def _ctx_attn_kernel(q_ref, k_ref, v_ref, sink_ref, o_ref, *, scale, use_sink):
    s = lax.dot_general(q_ref[0, 0].astype(BF16), k_ref[0, 0].astype(BF16), NT_DIMS,
                        preferred_element_type=F32) * scale
    m = jnp.max(s, axis=-1, keepdims=True)
    if use_sink:
        sink = sink_ref[pl.program_id(1)]
        m = jnp.maximum(m, sink)
    p = jnp.exp(s - m)
    l = jnp.sum(p, axis=-1, keepdims=True)
    if use_sink:
        l = l + jnp.exp(sink - m)
    o_ref[0, 0] = jnp.dot(p.astype(BF16), v_ref[0, 0].astype(BF16), preferred_element_type=F32) / l


def ctx_attn(q, k, v, scale, sink=None):
    B, Tc, H, _ = q.shape
    rep = H // k.shape[2]
    hm = lambda a: jnp.swapaxes(a, 1, 2)
    q, k, v = hm(q), hm(jnp.repeat(k, rep, axis=2)), hm(jnp.repeat(v, rep, axis=2))
    blk = lambda a: pl.BlockSpec((1, 1, Tc, a.shape[-1]), lambda b, h: (b, h, 0, 0))
    out = pl.pallas_call(
        functools.partial(_ctx_attn_kernel, scale=scale, use_sink=sink is not None),
        grid=(B, H),
        in_specs=[blk(q), blk(k), blk(v), pl.BlockSpec(memory_space=pltpu.SMEM)],
        out_specs=blk(v),
        out_shape=jax.ShapeDtypeStruct(v.shape, F32),
    )(q, k, v, jnp.zeros((H,), F32) if sink is None else sink.astype(F32))
    return jnp.swapaxes(out, 1, 2).reshape(B, Tc, -1)
NA_SPAN = NA_ROWS * GRID_W


def _head_mask(width):
    rows = lax.broadcasted_iota(jnp.int32, (GROUP_HEADS * width, GROUP_WIDTH), 0) // width
    cols = lax.broadcasted_iota(jnp.int32, (GROUP_HEADS * width, GROUP_WIDTH), 1) // HEAD_DIM
    return (rows == cols).astype(F32)


def _na_kernel(q_ref, k_ref, v_ref, kc_ref, vc_ref, bias_ref, o_ref):
    r = pl.program_id(1)
    rows = pl.num_programs(1)
    rs = jnp.clip(r - NA_ROWS // 2, 0, rows - NA_ROWS)
    start = pl.multiple_of(rs * GRID_W, GRID_W)
    kw = k_ref[0, pl.ds(start, NA_SPAN), :]
    vw = v_ref[0, pl.ds(start, NA_SPAN), :]
    hm = _head_mask(GRID_W)
    q = q_ref[0] * (HEAD_DIM ** -0.5)
    q4 = (jnp.concatenate([q] * GROUP_HEADS, axis=0) * hm).astype(BF16)
    s_loc = lax.dot_general(q4, kw, NT_DIMS, preferred_element_type=F32) + bias_ref[rs - r + NA_ROWS - 1]
    s_ctx = lax.dot_general(q4, kc_ref[0], NT_DIMS, preferred_element_type=F32)
    m = jnp.maximum(jnp.max(s_loc, axis=-1, keepdims=True), jnp.max(s_ctx, axis=-1, keepdims=True))
    p_loc = jnp.exp(s_loc - m)
    p_ctx = jnp.exp(s_ctx - m)
    l = jnp.sum(p_loc, axis=-1, keepdims=True) + jnp.sum(p_ctx, axis=-1, keepdims=True)
    o = (jnp.dot(p_loc.astype(BF16), vw, preferred_element_type=F32)
         + jnp.dot(p_ctx.astype(BF16), vc_ref[0], preferred_element_type=F32)) * (hm / l)
    o_ref[0] = sum(o[h * GRID_W:(h + 1) * GRID_W] for h in range(GROUP_HEADS))


def _na_bias_table(rpb):
    c = np.arange(GRID_W)
    col_start = np.clip(c - NA_COLS // 2, 0, GRID_W - NA_COLS)
    valid = (c[None, :] >= col_start[:, None]) & (c[None, :] < col_start[:, None] + NA_COLS)
    dc = np.clip(c[None, :] - c[:, None] + NA_COLS - 1, 0, 2 * NA_COLS - 2)
    dr = np.arange(NA_ROWS)[:, None] + np.arange(NA_ROWS)[None, :]
    t = rpb.astype(F32)[:, dr][..., dc]
    t = jnp.where(valid[None, None, None], t, -jnp.inf)
    return jnp.transpose(t, (1, 0, 3, 2, 4)).reshape(NA_ROWS, GROUP_HEADS * GRID_W, NA_SPAN)


def neighbourhood_attention(q, k, v, kc, vc, rpb):
    B, T, C = q.shape
    rows = T // GRID_W
    n_ctx = kc.shape[1]
    bias = _na_bias_table(rpb)
    full = lambda n: pl.BlockSpec((1, n, C), lambda b, r: (b, 0, 0))
    return pl.pallas_call(
        _na_kernel,
        grid=(B, rows),
        in_specs=[pl.BlockSpec((1, GRID_W, C), lambda b, r: (b, r, 0)),
                  full(T), full(T), full(n_ctx), full(n_ctx),
                  pl.BlockSpec(bias.shape, lambda b, r: (0, 0, 0))],
        out_specs=pl.BlockSpec((1, GRID_W, C), lambda b, r: (b, r, 0)),
        out_shape=jax.ShapeDtypeStruct((B, T, C), F32),
        compiler_params=pltpu.CompilerParams(vmem_limit_bytes=VMEM_LIMIT_BYTES),
    )(q, k.astype(BF16), v.astype(BF16), kc.astype(BF16), vc.astype(BF16), bias)


def short_conv(a, w):
    T = a.shape[1]
    pad = w.shape[0] // 2
    ap = jnp.pad(a, ((0, 0), (pad, pad), (0, 0)))
    out = ap[:, :T] * w[0]
    for j in range(1, w.shape[0]):
        out = out + ap[:, j:j + T] * w[j]
    return out


ML_CHUNKS_PER_STEP = CTX_LEN // ML_CHUNK


def _bmm(a, b, contract):
    return lax.dot_general(a.astype(BF16), b.astype(BF16), (contract, ((0,), (0,))),
                           preferred_element_type=F32)


def _mlstm_chunk(qt, kt, vt, irow, brow, state, backward):
    L = ML_CHUNK
    C, nrow, m = state
    row = lax.broadcasted_iota(jnp.int32, (1, L, L), 1)
    col = lax.broadcasted_iota(jnp.int32, (1, L, L), 2)
    seen = (row <= col) if backward else (row >= col)
    eye = row == col

    def as_col(r):
        return jnp.sum(jnp.where(eye, r, 0.0), axis=2, keepdims=True)

    blast = brow[:, :, 0:1] if backward else brow[:, :, L - 1:L]
    rrow = brow - irow
    bcol = as_col(brow)
    d_log = jnp.where(seen, bcol - rrow, -jnp.inf)
    inter = bcol + m
    m_t = jnp.maximum(inter, jnp.max(d_log, axis=2, keepdims=True))
    w = jnp.exp(d_log - m_t)
    a = jnp.exp(inter - m_t)
    s = _bmm(qt, kt, ((2,), (2,))) * w
    num = _bmm(s, vt, ((2,), (1,))) + a * _bmm(qt, C, ((2,), (1,)))
    den = jnp.sum(s, axis=2, keepdims=True) + a * jnp.sum(qt * nrow, axis=2, keepdims=True)
    h = num / jnp.maximum(jnp.abs(den), jnp.exp(-m_t))
    g = blast - rrow
    m_new = jnp.maximum(blast + m, jnp.max(g, axis=2, keepdims=True))
    kw = kt * as_col(jnp.exp(g - m_new))
    decay = jnp.exp(blast + m - m_new)
    C = decay * C + _bmm(jnp.swapaxes(kw, 1, 2), vt, ((2,), (1,)))
    nrow = decay * nrow + jnp.sum(kw, axis=1, keepdims=True)
    return h, (C, nrow, m_new)


def _mlstm_kernel(qf_ref, kf_ref, vf_ref, if_ref, bf_ref, qb_ref, kb_ref, vb_ref, ib_ref, bb_ref,
                  hf_ref, hb_ref, c_ref, n_ref, m_ref):
    N, L = qf_ref.shape[0], ML_CHUNK

    @pl.when(pl.program_id(0) == 0)
    def _():
        c_ref[...] = jnp.zeros(c_ref.shape, F32)
        n_ref[...] = jnp.zeros(n_ref.shape, F32)
        m_ref[...] = jnp.zeros(m_ref.shape, F32)

    fwd = (c_ref[:N], n_ref[:N], m_ref[:N])
    bwd = (c_ref[N:], n_ref[N:], m_ref[N:])
    for c in range(ML_CHUNKS_PER_STEP):
        rows = slice(c * L, (c + 1) * L)
        h, fwd = _mlstm_chunk(qf_ref[:, rows, :], kf_ref[:, rows, :], vf_ref[:, rows, :],
                              if_ref[:, 0, c:c + 1, :], bf_ref[:, 0, c:c + 1, :], fwd, False)
        hf_ref[:, rows, :] = h
        cb = ML_CHUNKS_PER_STEP - 1 - c
        rows = slice(cb * L, (cb + 1) * L)
        h, bwd = _mlstm_chunk(qb_ref[:, rows, :], kb_ref[:, rows, :], vb_ref[:, rows, :],
                              ib_ref[:, 0, cb:cb + 1, :], bb_ref[:, 0, cb:cb + 1, :], bwd, True)
        hb_ref[:, rows, :] = h
    for i, ref in enumerate((c_ref, n_ref, m_ref)):
        ref[:N] = fwd[i]
        ref[N:] = bwd[i]


def mlstm_scan(q, k, v, gates_f, gates_b, n_ctx):
    B, T, H, d = q.shape
    CB, L = ML_CHUNKS_PER_STEP, ML_CHUNK
    assert n_ctx == CB * L and T % (CB * L) == 0
    N, steps = B * H, T // (CB * L)
    hm = lambda a: jnp.swapaxes(a, 1, 2).reshape(N, T, d)
    gates = lambda a: jnp.swapaxes(a, 1, 2).reshape(N, steps, CB, L)
    chunked = lambda a: a.reshape(B, T // L, L, H)
    b_f = jnp.cumsum(chunked(gates_f[1]), axis=2).reshape(B, T, H)
    b_b = lax.cumsum(chunked(gates_b[1]), axis=2, reverse=True).reshape(B, T, H)
    back = lambda j: jnp.where(j == 0, 0, steps - j)
    seq_f = pl.BlockSpec((N, CB * L, d), lambda j: (0, j, 0))
    seq_b = pl.BlockSpec((N, CB * L, d), lambda j: (0, back(j), 0))
    gate_f = pl.BlockSpec((N, 1, CB, L), lambda j: (0, j, 0, 0))
    gate_b = pl.BlockSpec((N, 1, CB, L), lambda j: (0, back(j), 0, 0))
    qh, kh, vh = hm(q), hm(k), hm(v)
    hf, hb = pl.pallas_call(
        _mlstm_kernel,
        grid=(steps,),
        in_specs=[seq_f, seq_f, seq_f, gate_f, gate_f, seq_b, seq_b, seq_b, gate_b, gate_b],
        out_specs=[seq_f, seq_b],
        out_shape=[jax.ShapeDtypeStruct((N, T, d), F32)] * 2,
        scratch_shapes=[pltpu.VMEM((2 * N, d, d), F32), pltpu.VMEM((2 * N, 1, d), F32),
                        pltpu.VMEM((2 * N, 1, 1), F32)],
        compiler_params=pltpu.CompilerParams(vmem_limit_bytes=VMEM_LIMIT_BYTES),
    )(qh, kh, vh, gates(gates_f[0]), gates(b_f), qh, kh, vh, gates(gates_b[0]), gates(b_b))
    return jnp.swapaxes((hf + hb).reshape(B, H, T, d), 1, 2)


def mlstm_prep(qk, v, gates, conv_w, gate_b):
    qk = jax.nn.silu(short_conv(qk, conv_w))
    q, k = jnp.split(qk, 2, axis=-1)
    g = (gates + gate_b).astype(F32)
    i_f, f_f, i_b, f_b = jnp.split(g, 4, axis=-1)
    return (heads(q, GROUP_HEADS) * HEAD_DIM ** -0.5, heads(k, GROUP_HEADS), heads(v, GROUP_HEADS),
            (i_f, jax.nn.log_sigmoid(f_f), i_b, jax.nn.log_sigmoid(f_b)))


def mlstm_mixer(lat, ctx, conv_w, gate_b):
    ql, kl, vl, gl = mlstm_prep(lat[0], lat[1], lat[2], conv_w, gate_b)
    qc, kc, vc, gc = mlstm_prep(ctx[0], ctx[1], ctx[2], conv_w, gate_b)
    Tc = qc.shape[1]
    cat = lambda c_, l_: jnp.concatenate([c_, l_], axis=1)
    h = mlstm_scan(cat(qc, ql), cat(kc, kl), cat(vc, vl),
                   (cat(gc[0], gl[0]), cat(gc[1], gl[1])), (cat(gc[2], gl[2]), cat(gc[3], gl[3])), Tc)
    return h[:, Tc:], h[:, :Tc]


MLA_KR = 9
ROPE_PARTNER = np.concatenate([np.arange(q_, q_ + MLA_ROPE // 4) for q_ in
                               (MLA_ROPE // 4, 0, 3 * MLA_ROPE // 4, MLA_ROPE // 2)])


def mla_rope_tables(T):
    ang_r, ang_c = axial_angles(T, MLA_ROPE)
    cos = jnp.concatenate([jnp.cos(ang_r)] * 2 + [jnp.cos(ang_c)] * 2, axis=1)
    sin = jnp.concatenate([-jnp.sin(ang_r), jnp.sin(ang_r), -jnp.sin(ang_c), jnp.sin(ang_c)], axis=1)
    pad = LANES - MLA_NOPE - MLA_ROPE
    return (jnp.concatenate([jnp.ones((T, MLA_NOPE), F32), cos, jnp.zeros((T, pad), F32)], axis=1),
            jnp.concatenate([jnp.zeros((T, MLA_NOPE), F32), sin, jnp.zeros((T, pad), F32)], axis=1))


def _mla_qkv_kernel(cq_ref, ckv_ref, kr_ref, cos_ref, sin_ref, qn_ref, kvn_ref, wq_ref, wkv_ref,
                    q_ref, k_ref, v_ref):
    def up(x, g, w_ref):
        y = x * lax.rsqrt(jnp.mean(x * x, axis=-1, keepdims=True) + EPS) * g
        return jnp.dot(y.astype(BF16), w_ref[...], preferred_element_type=F32)

    W = GROUP_HEADS * LANES
    per_head = lambda a: jnp.concatenate([a] * GROUP_HEADS, axis=1)
    cos, sin = cos_ref[...], sin_ref[...]
    q2 = up(cq_ref[...], qn_ref[...], wq_ref)
    q_ref[...] = (q2[:, :W] * per_head(cos) + q2[:, W:] * per_head(sin)).astype(BF16)
    kv = up(ckv_ref[...], kvn_ref[...], wkv_ref)
    kr = kr_ref[...]
    k_rope = kr[:, :LANES] * cos + kr[:, LANES:] * sin
    k_ref[...] = (kv[:, :W] + per_head(k_rope)).astype(BF16)
    v_ref[...] = kv[:, W:].astype(BF16)


def mla_qkv(cq, ckv, kr2, q_norm, w_uq, kv_norm, w_ukv, cos, sin):
    N = cq.shape[0]
    P = cos.shape[0]
    tm = min(PROJ_TOKENS, P)
    H, dqk = GROUP_HEADS, MLA_NOPE + MLA_ROPE
    blocks = lambda w_, lo, n: jnp.pad(w_.reshape(w_.shape[0], H, -1)[:, :, lo:lo + n],
                                      ((0, 0), (0, 0), (0, LANES - n))).reshape(w_.shape[0], H * LANES)
    wq = w_uq.reshape(w_uq.shape[0], H, dqk)
    wq_partner = jnp.pad(wq[:, :, MLA_NOPE + ROPE_PARTNER], ((0, 0), (0, 0), (MLA_NOPE, LANES - dqk)))
    wq2 = jnp.concatenate([blocks(w_uq, 0, dqk), wq_partner.reshape(-1, H * LANES)], axis=1).astype(BF16)
    wkv2 = jnp.concatenate([blocks(w_ukv, 0, MLA_NOPE), blocks(w_ukv, MLA_NOPE, MLA_V)], axis=1).astype(BF16)
    tok = lambda w_: pl.BlockSpec((tm, w_), lambda i: (i, 0))
    pos = pl.BlockSpec((tm, LANES), lambda i: (i % (P // tm), 0))
    const = lambda a: pl.BlockSpec(a.shape, lambda i: (0, 0))
    args = (cq, ckv, kr2, cos, sin, q_norm[None], kv_norm[None], wq2, wkv2)
    out = jax.ShapeDtypeStruct((N, H * LANES), BF16)
    return pl.pallas_call(
        _mla_qkv_kernel,
        grid=(N // tm,),
        in_specs=[tok(cq.shape[1]), tok(ckv.shape[1]), tok(kr2.shape[1]), pos, pos] + [const(a) for a in args[5:]],
        out_specs=[tok(H * LANES)] * 3,
        out_shape=[out] * 3,
    )(*args)


LOG2_E = 1.4426950408889634
DENSE_Q_TILE = 1024
DENSE_Q_SUB = 256
DENSE_Q_UNROLL = 4
DENSE_K_TILE_MAX = 8320


def _dense_attn_kernel(q_ref, k_ref, v_ref, o_ref, m_ref, l_ref, acc_ref, *, scale, dv):
    h, j = pl.program_id(2), pl.program_id(3)

    @pl.when(j == 0)
    def _():
        m_ref[...] = jnp.full(m_ref.shape, -jnp.inf, F32)
        l_ref[...] = jnp.zeros(l_ref.shape, F32)
        acc_ref[...] = jnp.zeros(acc_ref.shape, F32)

    def rows(i, carry):
        for u in range(DENSE_Q_UNROLL):
            r = pl.ds(pl.multiple_of((i * DENSE_Q_UNROLL + u) * DENSE_Q_SUB, DENSE_Q_SUB), DENSE_Q_SUB)
            s = lax.dot_general(q_ref[0, r, :], k_ref[0], NT_DIMS,
                                preferred_element_type=F32) * (scale * LOG2_E)
            m_prev = m_ref[r, :]
            m_new = jnp.maximum(m_prev, jnp.max(s, axis=-1, keepdims=True))
            alpha = jnp.exp2(m_prev - m_new)
            p = jnp.exp2(s - m_new)
            l_ref[r, :] = alpha * l_ref[r, :] + jnp.sum(p, axis=-1, keepdims=True)
            acc_ref[r, :] = alpha * acc_ref[r, :] + jnp.dot(p.astype(BF16), v_ref[0],
                                                            preferred_element_type=F32)
            m_ref[r, :] = m_new
        return carry

    lax.fori_loop(0, q_ref.shape[1] // (DENSE_Q_SUB * DENSE_Q_UNROLL), rows, 0)

    for hh in range(o_ref.shape[2] // dv):
        @pl.when((j == pl.num_programs(3) - 1) & (h == hh))
        def _():
            o_ref[0, :, hh * dv:(hh + 1) * dv] = (acc_ref[...] / l_ref[...])[:, :dv]


def dense_attention(q, k_all, v_all, scale, dv):
    B, T, C = q.shape
    H, NK = C // LANES, k_all.shape[1]
    tq = min(DENSE_Q_TILE, T)
    tk = max(t for t in range(LANES, DENSE_K_TILE_MAX + 1, LANES) if NK % t == 0)
    return pl.pallas_call(
        functools.partial(_dense_attn_kernel, scale=scale, dv=dv),
        grid=(B, T // tq, H, NK // tk),
        in_specs=[pl.BlockSpec((1, tq, LANES), lambda b, i, h, j: (b, i, h)),
                  pl.BlockSpec((1, tk, LANES), lambda b, i, h, j: (b, j, h)),
                  pl.BlockSpec((1, tk, LANES), lambda b, i, h, j: (b, j, h))],
        out_specs=pl.BlockSpec((1, tq, H * dv), lambda b, i, h, j: (b, i, 0)),
        out_shape=jax.ShapeDtypeStruct((B, T, H * dv), F32),
        scratch_shapes=[pltpu.VMEM((tq, 1), F32), pltpu.VMEM((tq, 1), F32), pltpu.VMEM((tq, LANES), F32)],
        compiler_params=pltpu.CompilerParams(vmem_limit_bytes=VMEM_LIMIT_BYTES),
    )(q, k_all, v_all)


SWA_SPAN = ATTN_BLOCK + 2 * SWA_WINDOW


def _swa_kernel(q_ref, k_ref, v_ref, kc_ref, vc_ref, sink_ref, o_ref):
    n = pl.program_id(1)
    T = k_ref.shape[1]
    start = pl.multiple_of(jnp.clip(n * ATTN_BLOCK - SWA_WINDOW, 0, T - SWA_SPAN), ATTN_BLOCK)
    kw = k_ref[0, pl.ds(start, SWA_SPAN), :]
    vw = v_ref[0, pl.ds(start, SWA_SPAN), :]
    hm = _head_mask(ATTN_BLOCK)
    q = q_ref[0] * (HEAD_DIM ** -0.5)
    q4 = (jnp.concatenate([q] * GROUP_HEADS, axis=0) * hm).astype(BF16)
    rows = GROUP_HEADS * ATTN_BLOCK
    q_pos = n * ATTN_BLOCK + lax.broadcasted_iota(jnp.int32, (rows, SWA_SPAN), 0) % ATTN_BLOCK
    k_pos = start + lax.broadcasted_iota(jnp.int32, (rows, SWA_SPAN), 1)
    s_loc = lax.dot_general(q4, kw, NT_DIMS, preferred_element_type=F32)
    s_loc = jnp.where(jnp.abs(q_pos - k_pos) <= SWA_WINDOW, s_loc, -jnp.inf)
    s_ctx = lax.dot_general(q4, kc_ref[0], NT_DIMS, preferred_element_type=F32)
    sink = sink_ref[...]
    m = jnp.maximum(jnp.maximum(jnp.max(s_loc, axis=-1, keepdims=True),
                                jnp.max(s_ctx, axis=-1, keepdims=True)), sink)
    p_loc = jnp.exp(s_loc - m)
    p_ctx = jnp.exp(s_ctx - m)
    l = jnp.sum(p_loc, axis=-1, keepdims=True) + jnp.sum(p_ctx, axis=-1, keepdims=True) + jnp.exp(sink - m)
    o = (jnp.dot(p_loc.astype(BF16), vw, preferred_element_type=F32)
         + jnp.dot(p_ctx.astype(BF16), vc_ref[0], preferred_element_type=F32)) * (hm / l)
    o_ref[0] = sum(o[h * ATTN_BLOCK:(h + 1) * ATTN_BLOCK] for h in range(GROUP_HEADS))


def window_attention(q, k, v, kc, vc, sink):
    B, T, H, d = q.shape
    G = H // k.shape[2]
    n_ctx = kc.shape[1]
    C = H * d
    rep = lambda a: jnp.repeat(a, G, axis=2).reshape(a.shape[0], a.shape[1], C).astype(BF16)
    sink_rows = jnp.repeat(sink.astype(F32), ATTN_BLOCK).reshape(H * ATTN_BLOCK, 1)
    full = lambda n: pl.BlockSpec((1, n, C), lambda b, i: (b, 0, 0))
    return pl.pallas_call(
        _swa_kernel,
        grid=(B, T // ATTN_BLOCK),
        in_specs=[pl.BlockSpec((1, ATTN_BLOCK, C), lambda b, i: (b, i, 0)),
                  full(T), full(T), full(n_ctx), full(n_ctx),
                  pl.BlockSpec(sink_rows.shape, lambda b, i: (0, 0))],
        out_specs=pl.BlockSpec((1, ATTN_BLOCK, C), lambda b, i: (b, i, 0)),
        out_shape=jax.ShapeDtypeStruct((B, T, C), F32),
        compiler_params=pltpu.CompilerParams(vmem_limit_bytes=VMEM_LIMIT_BYTES),
    )(q.reshape(B, T, C), rep(k), rep(v), rep(kc), rep(vc), sink_rows)


BF16 = jnp.bfloat16
LANES = 128
SUBLANES = 8
ROW_SEGS = D_MODEL // LANES
ROW_WORDS = ROW_SEGS // 2
PEER_PICKS = PEER_HEADS * PEER_TOPK
PEER_TOPK_TOKENS = 512
PEER_GATHER_TOKENS = 128
PEER_ACT_UNROLL = SUBLANES
VMEM_LIMIT_BYTES = 56 * 1024 * 1024


def _split_bf16(x, parts):
    out = []
    for _ in range(parts):
        p = x.astype(BF16)
        out.append(p)
        x = x - p.astype(F32)
    return out


def _topk_rows(s, k):
    n = s.shape[0]
    iota = lax.broadcasted_iota(jnp.int32, s.shape, 0)
    vals, idxs = [], []
    for _ in range(k):
        m = jnp.max(s, axis=0, keepdims=True)
        i = jnp.min(jnp.where(s == m, iota, n), axis=0, keepdims=True)
        vals.append(m)
        idxs.append(i)
        s = jnp.where(iota == i, -jnp.inf, s)
    return jnp.concatenate(vals, axis=0), jnp.concatenate(idxs, axis=0)


def _peer_topk_kernel(x_ref, wq_ref, keys_ref, eidx_ref, gate_ref):
    xb = x_ref[...].astype(BF16)
    q = jnp.dot(xb, wq_ref[...], preferred_element_type=F32)
    nt = (((1,), (1,)), ((), ()))
    sv, si = [], []
    for p in range(2):
        qp = q[:, p * PEER_DKEY:(p + 1) * PEER_DKEY].astype(BF16)
        s = lax.dot_general(keys_ref[0, p], qp, nt, preferred_element_type=F32)
        v_, i_ = _topk_rows(s, PEER_TOPK)
        sv.append(v_)
        si.append(i_)
    cs, ce = [], []
    half = PEER_TOPK // 2
    for a in range(half):
        nb = PEER_TOPK if a == 0 else half
        cs.append(sv[0][a:a + 1] + sv[1][:nb])
        ce.append(si[0][a:a + 1] * PEER_NKEYS + si[1][:nb])
    cs.append(sv[0][half:] + sv[1][0:1])
    ce.append(si[0][half:] * PEER_NKEYS + si[1][0:1])
    cand_s = jnp.concatenate(cs, axis=0)
    cand_e = jnp.concatenate(ce, axis=0)
    fs, fpos = _topk_rows(cand_s, PEER_TOPK)
    iota = lax.broadcasted_iota(jnp.int32, cand_e.shape, 0)
    eidx = [jnp.max(jnp.where(iota == fpos[j:j + 1], cand_e, -1), axis=0, keepdims=True)
            for j in range(PEER_TOPK)]
    ex = jnp.exp(fs - fs[0:1])
    eidx_ref[0] = jnp.concatenate(eidx, axis=0)
    gate_ref[0] = ex / jnp.sum(ex, axis=0, keepdims=True)


def peer_topk(h, wq, sub_keys):
    N, D = h.shape
    T = PEER_TOPK_TOKENS
    wqb = wq.astype(BF16)
    kb = sub_keys.astype(BF16)
    eidx, gate = pl.pallas_call(
        _peer_topk_kernel,
        grid=(N // T, PEER_HEADS),
        in_specs=[pl.BlockSpec((T, D), lambda i, h_: (i, 0)),
                  pl.BlockSpec((D, 2 * PEER_DKEY), lambda i, h_: (0, h_)),
                  pl.BlockSpec((1, 2, PEER_NKEYS, PEER_DKEY), lambda i, h_: (h_, 0, 0, 0))],
        out_specs=[pl.BlockSpec((1, PEER_TOPK, T), lambda i, h_: (h_, 0, i)),
                   pl.BlockSpec((1, PEER_TOPK, T), lambda i, h_: (h_, 0, i))],
        out_shape=[jax.ShapeDtypeStruct((PEER_HEADS, PEER_TOPK, N), jnp.int32),
                   jax.ShapeDtypeStruct((PEER_HEADS, PEER_TOPK, N), F32)],
        compiler_params=pltpu.CompilerParams(vmem_limit_bytes=VMEM_LIMIT_BYTES),
    )(h, wqb, kb)
    return eidx.reshape(PEER_PICKS, N), gate.reshape(PEER_PICKS, N)


def pack_expert_table(tab):
    E, D = tab.shape
    return pl.pallas_call(
        _pack_table_kernel,
        grid=(E // PACK_ROWS,),
        in_specs=[pl.BlockSpec((PACK_ROWS, D), lambda i: (i, 0))],
        out_specs=pl.BlockSpec((PACK_ROWS * ROW_WORDS, LANES), lambda i: (i, 0)),
        out_shape=jax.ShapeDtypeStruct((E * ROW_WORDS, LANES), jnp.uint32),
    )(tab)


PACK_ROWS = 512


def _pack_table_kernel(t_ref, o_ref):
    bits = lambda a: lax.bitcast_convert_type(a.astype(BF16).astype(F32), jnp.uint32)
    for s_ in range(ROW_WORDS):
        lo = bits(t_ref[:, (2 * s_) * LANES:(2 * s_ + 1) * LANES])
        hi = bits(t_ref[:, (2 * s_ + 1) * LANES:(2 * s_ + 2) * LANES])
        o_ref[pl.ds(s_, PACK_ROWS, stride=ROW_WORDS), :] = (hi & jnp.uint32(0xFFFF0000)) | (lo >> 16)


def _stage_rows(idx_ref, tab_ref, stage_ref, t):
    for k in range(PEER_PICKS):
        off = pl.multiple_of(idx_ref[t, k], ROW_WORDS)
        stage_ref[k * ROW_WORDS:(k + 1) * ROW_WORDS, :] = tab_ref[pl.ds(off, ROW_WORDS), :]
    return pltpu.bitcast(stage_ref[...], BF16)


def _peer_act_kernel(idx_ref, x_ref, gate_ref, tab_ref, seg_mask_ref, group_ref, w_ref,
                     stage_ref, rows_ref):
    T = x_ref.shape[0]
    U = stage_ref.shape[0]

    sub = lax.broadcasted_iota(jnp.int32, (SUBLANES, PEER_PICKS * ROW_SEGS), 0)

    def tokens(g, carry):
        tile = jnp.zeros((SUBLANES, PEER_PICKS * ROW_SEGS), F32)
        for j in range(U):
            t = g * U + j
            sb = _stage_rows(idx_ref, tab_ref, stage_ref.at[j], t)
            xs = jnp.concatenate(_split_bf16(x_ref[t], 2), axis=0)
            r = lax.dot_general(xs, sb, NT_DIMS, preferred_element_type=F32)
            r = jnp.sum(r * seg_mask_ref[...], axis=0, keepdims=True)
            tile = jnp.where(sub == j, r, tile)
        rows_ref[g] = tile
        return carry

    lax.fori_loop(0, T // U, tokens, 0)
    rows = rows_ref[...].reshape(T, PEER_PICKS * ROW_SEGS)
    act = jnp.zeros((T, PEER_PICKS), F32)
    for piece in _split_bf16(rows, 3):
        act = act + jnp.dot(piece, group_ref[...], preferred_element_type=F32)
    w_ref[...] = gate_ref[...] * (0.5 * act * (1.0 + lax.erf(act * (2.0 ** -0.5))))


def _peer_out_kernel(idx_ref, w_ref, x_ref, g_ref, tab_ref, expand_ref, seg_mask_ref, f_ref, stage_ref):
    T = w_ref.shape[0]
    U = stage_ref.shape[0]

    def tokens(g, carry):
        w8 = w_ref[pl.ds(pl.multiple_of(g * U, U), U), :]
        hi, lo = _split_bf16(w8, 2)
        lhs = jnp.concatenate([jnp.broadcast_to(p[j:j + 1], (SUBLANES, PEER_PICKS))
                               for j in range(U) for p in (hi, lo)], axis=0)
        wrep = jnp.dot(lhs, expand_ref[...], preferred_element_type=F32)
        for j in range(U):
            t = g * U + j
            sb = _stage_rows(idx_ref, tab_ref, stage_ref.at[j], t)
            wsel = (wrep[j * 2 * SUBLANES:(j + 1) * 2 * SUBLANES] * seg_mask_ref[...]).astype(BF16)
            o = jnp.dot(wsel, sb, preferred_element_type=F32)
            f_ref[t] = x_ref[t] + g_ref[0] * (o[:SUBLANES] + o[SUBLANES:])
        return carry

    lax.fori_loop(0, T // U, tokens, 0)


def _peer_constants():
    cols = np.arange(PEER_PICKS * ROW_SEGS)
    seg_mask = (cols[None, :] % ROW_SEGS == np.arange(2 * SUBLANES)[:, None] % SUBLANES)
    group = (cols[:, None] // ROW_SEGS == np.arange(PEER_PICKS)[None, :])
    return (jnp.asarray(seg_mask, F32), jnp.asarray(group, BF16), jnp.asarray(group.T, BF16))


def peer_ffn(h, x, gate2, group_tokens, wq, sub_keys, u_packed, v_packed):
    N, D = h.shape
    T = PEER_GATHER_TOKENS
    eidx, gate = peer_topk(h, wq, sub_keys)
    seg_mask, group, expand = _peer_constants()
    rows3 = lambda a: a.reshape(a.shape[0], ROW_SEGS, LANES)
    offs = eidx.T * ROW_WORDS
    idx_spec = pl.BlockSpec((T, PEER_PICKS), lambda i: (i, 0), memory_space=pltpu.SMEM)
    tab_spec = pl.BlockSpec(u_packed.shape, lambda i: (0, 0), pipeline_mode=pl.Buffered(1))
    tok_spec = pl.BlockSpec((T, ROW_SEGS, LANES), lambda i: (i, 0, 0))
    const = lambda shape: pl.BlockSpec(shape, lambda i: (0, 0))
    params = pltpu.CompilerParams(vmem_limit_bytes=VMEM_LIMIT_BYTES)
    w = pl.pallas_call(
        _peer_act_kernel,
        grid=(N // T,),
        in_specs=[idx_spec, tok_spec,
                  pl.BlockSpec((T, PEER_PICKS), lambda i: (i, 0)),
                  tab_spec, const(seg_mask.shape), const(group.shape)],
        out_specs=pl.BlockSpec((T, PEER_PICKS), lambda i: (i, 0)),
        out_shape=jax.ShapeDtypeStruct((N, PEER_PICKS), F32),
        scratch_shapes=[pltpu.VMEM((PEER_ACT_UNROLL, PEER_PICKS * ROW_WORDS, LANES), jnp.uint32),
                        pltpu.VMEM((T // SUBLANES, SUBLANES, PEER_PICKS * ROW_SEGS), F32)],
        compiler_params=params,
    )(offs, rows3(h), gate.T, u_packed, seg_mask, group)
    out = pl.pallas_call(
        _peer_out_kernel,
        grid=(N // T,),
        in_specs=[idx_spec,
                  pl.BlockSpec((T, PEER_PICKS), lambda i: (i, 0)),
                  tok_spec,
                  pl.BlockSpec((1, ROW_SEGS, LANES), lambda i: (i // (group_tokens // T), 0, 0)),
                  tab_spec, const(expand.shape), const(seg_mask.shape)],
        out_specs=tok_spec,
        out_shape=jax.ShapeDtypeStruct((N, ROW_SEGS, LANES), F32),
        scratch_shapes=[pltpu.VMEM((SUBLANES, PEER_PICKS * ROW_WORDS, LANES), jnp.uint32)],
        compiler_params=params,
    )(offs, w, rows3(x), rows3(gate2), v_packed, expand, seg_mask)
    return out.reshape(N, D)


PROJ_TOKENS = 512
MOD_ROWS = SUBLANES
IN_ALIGNED = tuple(i for i, s_ in enumerate(IN_SIZES) if s_ % LANES == 0)
IN_SMALL = tuple(i for i, s_ in enumerate(IN_SIZES) if s_ % LANES)
IN_MXU_ONLY = (1, 2, 4, 12)


def _rms_modulate(x, gain, scale1p, shift):
    r = lax.rsqrt(jnp.mean(x * x, axis=-1, keepdims=True) + EPS)
    return (x * r * gain) * scale1p + shift


def _in_proj_kernel(x_ref, mod_ref, w_ref, *out_refs):
    mod = mod_ref[0]
    h = _rms_modulate(x_ref[...], mod[0:1], mod[1:2], mod[2:3])
    y = jnp.dot(h.astype(BF16), w_ref[...], preferred_element_type=F32)
    off = 0
    for o_ref in out_refs:
        o_ref[...] = y[:, off:off + o_ref.shape[1]].astype(o_ref.dtype)
        off += o_ref.shape[1]


def in_projection(x, mod, w_in, group_tokens):
    N, D = x.shape
    T = min(PROJ_TOKENS, group_tokens)
    starts = np.cumsum((0,) + IN_SIZES)
    group_cols = lambda i: np.arange(starts[i], starts[i + 1])
    small = [i for i in IN_SMALL if i != MLA_KR]
    n_small = sum(IN_SIZES[i] for i in small)
    lane_pad = lambda w_, lo, hi: jnp.pad(w_, ((0, 0), (lo, hi)))
    w_kr = w_in[:, group_cols(MLA_KR)]
    wp = jnp.concatenate(
        [w_in[:, np.concatenate([group_cols(i) for i in IN_ALIGNED])],
         lane_pad(w_in[:, np.concatenate([group_cols(i) for i in small])], 0, -n_small % LANES),
         lane_pad(w_kr, MLA_NOPE, LANES - MLA_NOPE - MLA_ROPE),
         lane_pad(w_kr[:, ROPE_PARTNER], MLA_NOPE, LANES - MLA_NOPE - MLA_ROPE)], axis=1).astype(BF16)
    widths = [IN_SIZES[i] for i in IN_ALIGNED] + [n_small + (-n_small % LANES), 2 * LANES]
    outs = pl.pallas_call(
        _in_proj_kernel,
        grid=(N // T,),
        in_specs=[pl.BlockSpec((T, D), lambda i: (i, 0)),
                  pl.BlockSpec((1, MOD_ROWS, D), lambda i: (i // (group_tokens // T), 0, 0)),
                  pl.BlockSpec(wp.shape, lambda i: (0, 0))],
        out_specs=[pl.BlockSpec((T, w_), lambda i: (i, 0)) for w_ in widths],
        out_shape=[jax.ShapeDtypeStruct((N, w_), BF16 if i in IN_MXU_ONLY else F32)
                   for i, w_ in zip(IN_ALIGNED + (None, None), widths)],
        compiler_params=pltpu.CompilerParams(vmem_limit_bytes=VMEM_LIMIT_BYTES),
    )(x, mod, wp)
    groups = dict(zip(IN_ALIGNED, outs[:-2]))
    off = 0
    for i in small:
        groups[i] = outs[-2][:, off:off + IN_SIZES[i]]
        off += IN_SIZES[i]
    groups[MLA_KR] = outs[-1]
    return [groups[i] for i in range(len(IN_SIZES))]


def _out_proj_kernel(ya_ref, hl_ref, mo_ref, yc_ref, yd_ref, x_ref, mod_ref, w_ref, xo_ref, h2_ref):
    yb = hl_ref[...] * jax.nn.sigmoid(mo_ref[...])
    y = jnp.concatenate([ya_ref[...], yb, yc_ref[...], yd_ref[...]], axis=-1).astype(BF16)
    mod = mod_ref[0]
    xn = x_ref[...] + mod[0:1] * jnp.dot(y, w_ref[...], preferred_element_type=F32)
    xo_ref[...] = xn
    h2_ref[...] = _rms_modulate(xn, mod[1:2], mod[2:3], mod[3:4])


def out_projection(ya, hl, mo, yc, yd, x, mod, w_out, group_tokens):
    N, D = x.shape
    T = min(PROJ_TOKENS, group_tokens)
    part = pl.BlockSpec((T, GROUP_WIDTH), lambda i: (i, 0))
    tok = pl.BlockSpec((T, D), lambda i: (i, 0))
    return pl.pallas_call(
        _out_proj_kernel,
        grid=(N // T,),
        in_specs=[part, part, part, part, part, tok,
                  pl.BlockSpec((1, MOD_ROWS, D), lambda i: (i // (group_tokens // T), 0, 0)),
                  pl.BlockSpec(w_out.shape, lambda i: (0, 0))],
        out_specs=[tok, tok],
        out_shape=[jax.ShapeDtypeStruct((N, D), F32)] * 2,
        compiler_params=pltpu.CompilerParams(vmem_limit_bytes=VMEM_LIMIT_BYTES),
    )(ya, hl, mo, yc, yd, x, mod, w_out.astype(BF16))


def _adaln_kernel(c_ref, w_ref, b_ref, o_ref):
    c = c_ref[...]
    a = c * jax.nn.sigmoid(c)
    o_ref[...] = jnp.dot(a.astype(BF16), w_ref[...].astype(BF16), preferred_element_type=F32) + b_ref[...]


def adaln_linear(c, w_ada, b_ada):
    R, D = c.shape
    rows = -R % SUBLANES + R
    out = pl.pallas_call(
        _adaln_kernel,
        grid=(w_ada.shape[1] // D,),
        in_specs=[pl.BlockSpec((rows, D), lambda j: (0, 0)),
                  pl.BlockSpec((D, D), lambda j: (0, j)),
                  pl.BlockSpec((1, D), lambda j: (0, j))],
        out_specs=pl.BlockSpec((rows, D), lambda j: (0, j)),
        out_shape=jax.ShapeDtypeStruct((rows, w_ada.shape[1]), F32),
    )(jnp.pad(c, ((0, rows - R), (0, 0))), w_ada, b_ada[None])
    return out[:R]


def _mod_rows(*rows):
    m = jnp.stack([jnp.broadcast_to(r, rows[-1].shape) for r in rows], axis=1)
    return jnp.pad(m, ((0, 0), (0, MOD_ROWS - len(rows)), (0, 0)))


def hybrid_layer(x, xc, c, c_ctx, need_ctx, rope_mla, angs_swa,
                 norm1_g, norm2_g, w_ada, b_ada, w_in, na_rpb, ml_conv, ml_gate_b,
                 mla_q_norm, mla_w_uq, mla_kv_norm, mla_w_ukv, swa_sink, w_out,
                 peer_wq, peer_keys, peer_u, peer_v):
    B, T, D = x.shape
    Tc = xc.shape[1]
    H = GROUP_HEADS
    flat = lambda a: a.reshape(-1, a.shape[-1])
    ada = adaln_linear(jnp.concatenate([c, c_ctx[None]], axis=0), w_ada, b_ada)
    sh1, sc1, g1, sh2, sc2, g2 = jnp.split(ada[:B], 6, axis=-1)
    sh1c, sc1c, g1c, sh2c, sc2c, g2c = jnp.split(ada[B:], 6, axis=-1)
    lat = in_projection(flat(x), _mod_rows(norm1_g, 1.0 + sc1, sh1), w_in, T)
    cx = in_projection(flat(xc), _mod_rows(norm1_g, 1.0 + sc1c, sh1c), w_in, B * Tc)
    (na_q, na_k, na_v, ml_qk, ml_v, ml_o, ml_g,
     mla_cq, mla_ckv, mla_kr, sw_q, sw_k, sw_v) = [a.reshape(B, T, -1) for a in lat]
    (na_qc, na_kc, na_vc, ml_qkc, ml_vc, ml_oc, ml_gc,
     mla_cqc, mla_ckvc, mla_krc, sw_qc, sw_kc, sw_vc) = [a.reshape(B, Tc, -1) for a in cx]
    attn_scale = HEAD_DIM ** -0.5
    mla_scale = (MLA_NOPE + MLA_ROPE) ** -0.5
    kc_a, vc_a = heads(na_kc, H), heads(na_vc, H)
    y_a = neighbourhood_attention(na_q, na_k, na_v, na_kc, na_vc, na_rpb)
    h_lat, h_ctx = mlstm_mixer((ml_qk, ml_v, ml_g), (ml_qkc, ml_vc, ml_gc), ml_conv, ml_gate_b)
    no_rope = (jnp.ones((Tc, LANES), F32), jnp.zeros((Tc, LANES), F32))
    q_m, k_m, v_m = [a.reshape(B, T, -1) for a in
                     mla_qkv(flat(mla_cq), flat(mla_ckv), flat(mla_kr), mla_q_norm, mla_w_uq, mla_kv_norm, mla_w_ukv,
                             *rope_mla)]
    qc_m, kc_m, vc_m = [a.reshape(B, Tc, -1) for a in
                        mla_qkv(flat(mla_cqc), flat(mla_ckvc), flat(mla_krc), mla_q_norm, mla_w_uq, mla_kv_norm,
                                mla_w_ukv, *no_rope)]
    y_c = dense_attention(q_m, jnp.concatenate([kc_m, k_m], axis=1), jnp.concatenate([vc_m, v_m], axis=1),
                          mla_scale, MLA_V)
    kc_d, vc_d = heads(sw_kc, SWA_KV_HEADS), heads(sw_vc, SWA_KV_HEADS)
    y_d = window_attention(rope_2d(heads(sw_q, H), angs_swa), rope_2d(heads(sw_k, SWA_KV_HEADS), angs_swa),
                           heads(sw_v, SWA_KV_HEADS), kc_d, vc_d, swa_sink)
    x2, h2 = out_projection(flat(y_a), h_lat.reshape(B * T, GROUP_WIDTH), flat(ml_o), flat(y_c), flat(y_d),
                            flat(x), _mod_rows(g1, norm2_g, 1.0 + sc2, sh2), w_out, T)
    u_packed, v_packed = pack_expert_table(peer_u), pack_expert_table(peer_v)
    x = peer_ffn(h2, x2, g2, T, peer_wq, peer_keys, u_packed, v_packed).reshape(B, T, D)
    if not need_ctx:
        return x, None
    xc2, h2c = out_projection(flat(ctx_attn(heads(na_qc, H), kc_a, vc_a, attn_scale)),
                              h_ctx.reshape(B * Tc, GROUP_WIDTH), flat(ml_oc),
                              ctx_attn(heads(qc_m, H), heads(kc_m, H), heads(vc_m, H), mla_scale)
                              .reshape(B, Tc, H, LANES)[..., :MLA_V].reshape(B * Tc, GROUP_WIDTH),
                              flat(ctx_attn(heads(sw_qc, H), kc_d, vc_d, attn_scale, swa_sink)),
                              flat(xc), _mod_rows(g1c, norm2_g, 1.0 + sc2c, sh2c), w_out, B * Tc)
    xc = peer_ffn(h2c, xc2, g2c, B * Tc, peer_wq, peer_keys, u_packed, v_packed).reshape(B, Tc, D)
    return x, xc


def _final_rmsnorm_kernel(x_ref, g_ref, o_ref):
    x = x_ref[...]
    o_ref[...] = x * lax.rsqrt(jnp.mean(x * x, axis=-1, keepdims=True) + EPS) * g_ref[...]


def final_rmsnorm(x, g):
    B, T, D = x.shape
    rows = 1024
    xf = x.reshape(B * T, D)
    out = pl.pallas_call(
        _final_rmsnorm_kernel,
        grid=(B * T // rows,),
        in_specs=[pl.BlockSpec((rows, D), lambda i: (i, 0)), pl.BlockSpec((1, D), lambda i: (0, 0))],
        out_specs=pl.BlockSpec((rows, D), lambda i: (i, 0)),
        out_shape=jax.ShapeDtypeStruct((B * T, D), x.dtype),
    )(xf, g.reshape(1, D))
    return out.reshape(B, T, D)


def kernel(x, c, ctx, c_ctx, norm1_g, norm2_g, w_ada, b_ada, w_in, na_rpb, ml_conv, ml_gate_b,
           mla_q_norm, mla_w_uq, mla_kv_norm, mla_w_ukv, swa_sink, w_out,
           peer_wq, peer_keys, peer_u, peer_v, final_norm_g):
    T = x.shape[1]
    rope_mla = mla_rope_tables(T)
    angs_swa = axial_angles(T, HEAD_DIM)
    xc = ctx
    for l in range(DEPTH):
        x, xc = hybrid_layer(x, xc, c, c_ctx, l < DEPTH - 1, rope_mla, angs_swa,
                             norm1_g[l], norm2_g[l], w_ada[l], b_ada[l], w_in[l], na_rpb[l],
                             ml_conv[l], ml_gate_b[l], mla_q_norm[l], mla_w_uq[l], mla_kv_norm[l],
                             mla_w_ukv[l], swa_sink[l], w_out[l], peer_wq[l], peer_keys[l],
                             peer_u[l], peer_v[l])
    return final_rmsnorm(x, final_norm_g)
```

```python
import functools

import jax
import jax.numpy as jnp
from jax import lax
import numpy as np
from jax.experimental import pallas as pl
from jax.experimental.pallas import tpu as pltpu

D_MODEL = 1024
DEPTH = 2

CTX_LEN = 256
GRID_W = 64
N_MIXERS = 4
MIX_WIDTH = D_MODEL
GROUP_WIDTH = MIX_WIDTH // N_MIXERS
GROUP_HEADS = 4
HEAD_DIM = GROUP_WIDTH // GROUP_HEADS
NA_ROWS = 8
NA_COLS = 16
ML_CHUNK = 64
MLA_Q_RANK = 256
MLA_KV_RANK = 128
MLA_NOPE = 64
MLA_ROPE = 32
MLA_V = 64
SWA_KV_HEADS = 2
SWA_WINDOW = 128
ATTN_BLOCK = 128
PEER_HEADS = 8
PEER_NKEYS = 128
PEER_DKEY = 128
PEER_TOPK = 16
ROPE_BASE = 10000.0
EPS = 1e-6
IN_SIZES = (GROUP_WIDTH, GROUP_WIDTH, GROUP_WIDTH,
            2 * GROUP_WIDTH, GROUP_WIDTH, GROUP_WIDTH, 4 * GROUP_HEADS,
            MLA_Q_RANK, MLA_KV_RANK, MLA_ROPE,
            GROUP_WIDTH, SWA_KV_HEADS * HEAD_DIM, SWA_KV_HEADS * HEAD_DIM)
F32 = jnp.float32


def heads(a, h):
    return a.reshape(a.shape[:-1] + (h, a.shape[-1] // h))


def axial_angles(T, rot_dim):
    t = jnp.arange(T)
    row = (t // GRID_W).astype(F32)
    col = (t % GRID_W).astype(F32)
    half = rot_dim // 2
    inv = 1.0 / (ROPE_BASE ** (jnp.arange(0, half, 2, dtype=F32) / half))
    return row[:, None] * inv, col[:, None] * inv


def rope_1d(x, ang):
    cos = jnp.cos(ang)[None, :, None, :]
    sin = jnp.sin(ang)[None, :, None, :]
    x1, x2 = jnp.split(x.astype(F32), 2, axis=-1)
    return jnp.concatenate([x1 * cos - x2 * sin, x1 * sin + x2 * cos], axis=-1)


def rope_2d(x, angs):
    xr, xc = jnp.split(x, 2, axis=-1)
    return jnp.concatenate([rope_1d(xr, angs[0]), rope_1d(xc, angs[1])], axis=-1).astype(x.dtype)


NT_DIMS = (((1,), (1,)), ((), ()))


def _ctx_attn_kernel(q_ref, k_ref, v_ref, sink_ref, o_ref, *, scale, use_sink):
    s = lax.dot_general(q_ref[0, 0].astype(BF16), k_ref[0, 0].astype(BF16), NT_DIMS,
                        preferred_element_type=F32) * scale
    m = jnp.max(s, axis=-1, keepdims=True)
    if use_sink:
        sink = sink_ref[pl.program_id(1)]
        m = jnp.maximum(m, sink)
    p = jnp.exp(s - m)
    l = jnp.sum(p, axis=-1, keepdims=True)
    if use_sink:
        l = l + jnp.exp(sink - m)
    o_ref[0, 0] = jnp.dot(p.astype(BF16), v_ref[0, 0].astype(BF16), preferred_element_type=F32) / l


def ctx_attn(q, k, v, scale, sink=None):
    B, Tc, H, _ = q.shape
    rep = H // k.shape[2]
    hm = lambda a: jnp.swapaxes(a, 1, 2)
    q, k, v = hm(q), hm(jnp.repeat(k, rep, axis=2)), hm(jnp.repeat(v, rep, axis=2))
    blk = lambda a: pl.BlockSpec((1, 1, Tc, a.shape[-1]), lambda b, h: (b, h, 0, 0))
    out = pl.pallas_call(
        functools.partial(_ctx_attn_kernel, scale=scale, use_sink=sink is not None),
        grid=(B, H),
        in_specs=[blk(q), blk(k), blk(v), pl.BlockSpec(memory_space=pltpu.SMEM)],
        out_specs=blk(v),
        out_shape=jax.ShapeDtypeStruct(v.shape, F32),
    )(q, k, v, jnp.zeros((H,), F32) if sink is None else sink.astype(F32))
    return jnp.swapaxes(out, 1, 2).reshape(B, Tc, -1)
NA_SPAN = NA_ROWS * GRID_W


def _head_mask(width):
    rows = lax.broadcasted_iota(jnp.int32, (GROUP_HEADS * width, GROUP_WIDTH), 0) // width
    cols = lax.broadcasted_iota(jnp.int32, (GROUP_HEADS * width, GROUP_WIDTH), 1) // HEAD_DIM
    return (rows == cols).astype(F32)


def _na_kernel(q_ref, k_ref, v_ref, kc_ref, vc_ref, bias_ref, o_ref):
    r = pl.program_id(1)
    rows = pl.num_programs(1)
    rs = jnp.clip(r - NA_ROWS // 2, 0, rows - NA_ROWS)
    start = pl.multiple_of(rs * GRID_W, GRID_W)
    kw = k_ref[0, pl.ds(start, NA_SPAN), :]
    vw = v_ref[0, pl.ds(start, NA_SPAN), :]
    hm = _head_mask(GRID_W)
    q = q_ref[0] * (HEAD_DIM ** -0.5)
    q4 = (jnp.concatenate([q] * GROUP_HEADS, axis=0) * hm).astype(BF16)
    s_loc = lax.dot_general(q4, kw, NT_DIMS, preferred_element_type=F32) + bias_ref[rs - r + NA_ROWS - 1]
    s_ctx = lax.dot_general(q4, kc_ref[0], NT_DIMS, preferred_element_type=F32)
    m = jnp.maximum(jnp.max(s_loc, axis=-1, keepdims=True), jnp.max(s_ctx, axis=-1, keepdims=True))
    p_loc = jnp.exp(s_loc - m)
    p_ctx = jnp.exp(s_ctx - m)
    l = jnp.sum(p_loc, axis=-1, keepdims=True) + jnp.sum(p_ctx, axis=-1, keepdims=True)
    o = (jnp.dot(p_loc.astype(BF16), vw, preferred_element_type=F32)
         + jnp.dot(p_ctx.astype(BF16), vc_ref[0], preferred_element_type=F32)) * (hm / l)
    o_ref[0] = sum(o[h * GRID_W:(h + 1) * GRID_W] for h in range(GROUP_HEADS))


def _na_bias_table(rpb):
    c = np.arange(GRID_W)
    col_start = np.clip(c - NA_COLS // 2, 0, GRID_W - NA_COLS)
    valid = (c[None, :] >= col_start[:, None]) & (c[None, :] < col_start[:, None] + NA_COLS)
    dc = np.clip(c[None, :] - c[:, None] + NA_COLS - 1, 0, 2 * NA_COLS - 2)
    dr = np.arange(NA_ROWS)[:, None] + np.arange(NA_ROWS)[None, :]
    t = rpb.astype(F32)[:, dr][..., dc]
    t = jnp.where(valid[None, None, None], t, -jnp.inf)
    return jnp.transpose(t, (1, 0, 3, 2, 4)).reshape(NA_ROWS, GROUP_HEADS * GRID_W, NA_SPAN)


def neighbourhood_attention(q, k, v, kc, vc, rpb):
    B, T, C = q.shape
    rows = T // GRID_W
    n_ctx = kc.shape[1]
    bias = _na_bias_table(rpb)
    full = lambda n: pl.BlockSpec((1, n, C), lambda b, r: (b, 0, 0))
    return pl.pallas_call(
        _na_kernel,
        grid=(B, rows),
        in_specs=[pl.BlockSpec((1, GRID_W, C), lambda b, r: (b, r, 0)),
                  full(T), full(T), full(n_ctx), full(n_ctx),
                  pl.BlockSpec(bias.shape, lambda b, r: (0, 0, 0))],
        out_specs=pl.BlockSpec((1, GRID_W, C), lambda b, r: (b, r, 0)),
        out_shape=jax.ShapeDtypeStruct((B, T, C), F32),
        compiler_params=pltpu.CompilerParams(vmem_limit_bytes=VMEM_LIMIT_BYTES),
    )(q, k.astype(BF16), v.astype(BF16), kc.astype(BF16), vc.astype(BF16), bias)


def short_conv(a, w):
    T = a.shape[1]
    pad = w.shape[0] // 2
    ap = jnp.pad(a, ((0, 0), (pad, pad), (0, 0)))
    out = ap[:, :T] * w[0]
    for j in range(1, w.shape[0]):
        out = out + ap[:, j:j + T] * w[j]
    return out


ML_CHUNKS_PER_STEP = CTX_LEN // ML_CHUNK


def _bmm(a, b, contract):
    return lax.dot_general(a.astype(BF16), b.astype(BF16), (contract, ((0,), (0,))),
                           preferred_element_type=F32)


def _mlstm_chunk(qt, kt, vt, irow, brow, state, backward):
    L = ML_CHUNK
    C, nrow, m = state
    row = lax.broadcasted_iota(jnp.int32, (1, L, L), 1)
    col = lax.broadcasted_iota(jnp.int32, (1, L, L), 2)
    seen = (row <= col) if backward else (row >= col)
    eye = row == col

    def as_col(r):
        return jnp.sum(jnp.where(eye, r, 0.0), axis=2, keepdims=True)

    blast = brow[:, :, 0:1] if backward else brow[:, :, L - 1:L]
    rrow = brow - irow
    bcol = as_col(brow)
    d_log = jnp.where(seen, bcol - rrow, -jnp.inf)
    inter = bcol + m
    m_t = jnp.maximum(inter, jnp.max(d_log, axis=2, keepdims=True))
    w = jnp.exp(d_log - m_t)
    a = jnp.exp(inter - m_t)
    s = _bmm(qt, kt, ((2,), (2,))) * w
    num = _bmm(s, vt, ((2,), (1,))) + a * _bmm(qt, C, ((2,), (1,)))
    den = jnp.sum(s, axis=2, keepdims=True) + a * jnp.sum(qt * nrow, axis=2, keepdims=True)
    h = num / jnp.maximum(jnp.abs(den), jnp.exp(-m_t))
    g = blast - rrow
    m_new = jnp.maximum(blast + m, jnp.max(g, axis=2, keepdims=True))
    kw = kt * as_col(jnp.exp(g - m_new))
    decay = jnp.exp(blast + m - m_new)
    C = decay * C + _bmm(jnp.swapaxes(kw, 1, 2), vt, ((2,), (1,)))
    nrow = decay * nrow + jnp.sum(kw, axis=1, keepdims=True)
    return h, (C, nrow, m_new)


def _mlstm_kernel(qf_ref, kf_ref, vf_ref, if_ref, bf_ref, qb_ref, kb_ref, vb_ref, ib_ref, bb_ref,
                  hf_ref, hb_ref, c_ref, n_ref, m_ref):
    N, L = qf_ref.shape[0], ML_CHUNK

    @pl.when(pl.program_id(0) == 0)
    def _():
        c_ref[...] = jnp.zeros(c_ref.shape, F32)
        n_ref[...] = jnp.zeros(n_ref.shape, F32)
        m_ref[...] = jnp.zeros(m_ref.shape, F32)

    fwd = (c_ref[:N], n_ref[:N], m_ref[:N])
    bwd = (c_ref[N:], n_ref[N:], m_ref[N:])
    for c in range(ML_CHUNKS_PER_STEP):
        rows = slice(c * L, (c + 1) * L)
        h, fwd = _mlstm_chunk(qf_ref[:, rows, :], kf_ref[:, rows, :], vf_ref[:, rows, :],
                              if_ref[:, 0, c:c + 1, :], bf_ref[:, 0, c:c + 1, :], fwd, False)
        hf_ref[:, rows, :] = h
        cb = ML_CHUNKS_PER_STEP - 1 - c
        rows = slice(cb * L, (cb + 1) * L)
        h, bwd = _mlstm_chunk(qb_ref[:, rows, :], kb_ref[:, rows, :], vb_ref[:, rows, :],
                              ib_ref[:, 0, cb:cb + 1, :], bb_ref[:, 0, cb:cb + 1, :], bwd, True)
        hb_ref[:, rows, :] = h
    for i, ref in enumerate((c_ref, n_ref, m_ref)):
        ref[:N] = fwd[i]
        ref[N:] = bwd[i]


def mlstm_scan(q, k, v, gates_f, gates_b, n_ctx):
    B, T, H, d = q.shape
    CB, L = ML_CHUNKS_PER_STEP, ML_CHUNK
    assert n_ctx == CB * L and T % (CB * L) == 0
    N, steps = B * H, T // (CB * L)
    hm = lambda a: jnp.swapaxes(a, 1, 2).reshape(N, T, d)
    gates = lambda a: jnp.swapaxes(a, 1, 2).reshape(N, steps, CB, L)
    chunked = lambda a: a.reshape(B, T // L, L, H)
    b_f = jnp.cumsum(chunked(gates_f[1]), axis=2).reshape(B, T, H)
    b_b = lax.cumsum(chunked(gates_b[1]), axis=2, reverse=True).reshape(B, T, H)
    back = lambda j: jnp.where(j == 0, 0, steps - j)
    seq_f = pl.BlockSpec((N, CB * L, d), lambda j: (0, j, 0))
    seq_b = pl.BlockSpec((N, CB * L, d), lambda j: (0, back(j), 0))
    gate_f = pl.BlockSpec((N, 1, CB, L), lambda j: (0, j, 0, 0))
    gate_b = pl.BlockSpec((N, 1, CB, L), lambda j: (0, back(j), 0, 0))
    qh, kh, vh = hm(q), hm(k), hm(v)
    hf, hb = pl.pallas_call(
        _mlstm_kernel,
        grid=(steps,),
        in_specs=[seq_f, seq_f, seq_f, gate_f, gate_f, seq_b, seq_b, seq_b, gate_b, gate_b],
        out_specs=[seq_f, seq_b],
        out_shape=[jax.ShapeDtypeStruct((N, T, d), F32)] * 2,
        scratch_shapes=[pltpu.VMEM((2 * N, d, d), F32), pltpu.VMEM((2 * N, 1, d), F32),
                        pltpu.VMEM((2 * N, 1, 1), F32)],
        compiler_params=pltpu.CompilerParams(vmem_limit_bytes=VMEM_LIMIT_BYTES),
    )(qh, kh, vh, gates(gates_f[0]), gates(b_f), qh, kh, vh, gates(gates_b[0]), gates(b_b))
    return jnp.swapaxes((hf + hb).reshape(B, H, T, d), 1, 2)


def mlstm_prep(qk, v, gates, conv_w, gate_b):
    qk = jax.nn.silu(short_conv(qk, conv_w))
    q, k = jnp.split(qk, 2, axis=-1)
    g = (gates + gate_b).astype(F32)
    i_f, f_f, i_b, f_b = jnp.split(g, 4, axis=-1)
    return (heads(q, GROUP_HEADS) * HEAD_DIM ** -0.5, heads(k, GROUP_HEADS), heads(v, GROUP_HEADS),
            (i_f, jax.nn.log_sigmoid(f_f), i_b, jax.nn.log_sigmoid(f_b)))


def mlstm_mixer(lat, ctx, conv_w, gate_b):
    ql, kl, vl, gl = mlstm_prep(lat[0], lat[1], lat[2], conv_w, gate_b)
    qc, kc, vc, gc = mlstm_prep(ctx[0], ctx[1], ctx[2], conv_w, gate_b)
    Tc = qc.shape[1]
    cat = lambda c_, l_: jnp.concatenate([c_, l_], axis=1)
    h = mlstm_scan(cat(qc, ql), cat(kc, kl), cat(vc, vl),
                   (cat(gc[0], gl[0]), cat(gc[1], gl[1])), (cat(gc[2], gl[2]), cat(gc[3], gl[3])), Tc)
    return h[:, Tc:], h[:, :Tc]


MLA_KR = 9
ROPE_PARTNER = np.concatenate([np.arange(q_, q_ + MLA_ROPE // 4) for q_ in
                               (MLA_ROPE // 4, 0, 3 * MLA_ROPE // 4, MLA_ROPE // 2)])


def mla_rope_tables(T):
    ang_r, ang_c = axial_angles(T, MLA_ROPE)
    cos = jnp.concatenate([jnp.cos(ang_r)] * 2 + [jnp.cos(ang_c)] * 2, axis=1)
    sin = jnp.concatenate([-jnp.sin(ang_r), jnp.sin(ang_r), -jnp.sin(ang_c), jnp.sin(ang_c)], axis=1)
    pad = LANES - MLA_NOPE - MLA_ROPE
    return (jnp.concatenate([jnp.ones((T, MLA_NOPE), F32), cos, jnp.zeros((T, pad), F32)], axis=1),
            jnp.concatenate([jnp.zeros((T, MLA_NOPE), F32), sin, jnp.zeros((T, pad), F32)], axis=1))


def _mla_qkv_kernel(cq_ref, ckv_ref, kr_ref, cos_ref, sin_ref, qn_ref, kvn_ref, wq_ref, wkv_ref,
                    q_ref, k_ref, v_ref):
    def up(x, g, w_ref):
        y = x * lax.rsqrt(jnp.mean(x * x, axis=-1, keepdims=True) + EPS) * g
        return jnp.dot(y.astype(BF16), w_ref[...], preferred_element_type=F32)

    W = GROUP_HEADS * LANES
    per_head = lambda a: jnp.concatenate([a] * GROUP_HEADS, axis=1)
    cos, sin = cos_ref[...], sin_ref[...]
    q2 = up(cq_ref[...], qn_ref[...], wq_ref)
    q_ref[...] = (q2[:, :W] * per_head(cos) + q2[:, W:] * per_head(sin)).astype(BF16)
    kv = up(ckv_ref[...], kvn_ref[...], wkv_ref)
    kr = kr_ref[...]
    k_rope = kr[:, :LANES] * cos + kr[:, LANES:] * sin
    k_ref[...] = (kv[:, :W] + per_head(k_rope)).astype(BF16)
    v_ref[...] = kv[:, W:].astype(BF16)


def mla_qkv(cq, ckv, kr2, q_norm, w_uq, kv_norm, w_ukv, cos, sin):
    N = cq.shape[0]
    P = cos.shape[0]
    tm = min(PROJ_TOKENS, P)
    H, dqk = GROUP_HEADS, MLA_NOPE + MLA_ROPE
    blocks = lambda w_, lo, n: jnp.pad(w_.reshape(w_.shape[0], H, -1)[:, :, lo:lo + n],
                                      ((0, 0), (0, 0), (0, LANES - n))).reshape(w_.shape[0], H * LANES)
    wq = w_uq.reshape(w_uq.shape[0], H, dqk)
    wq_partner = jnp.pad(wq[:, :, MLA_NOPE + ROPE_PARTNER], ((0, 0), (0, 0), (MLA_NOPE, LANES - dqk)))
    wq2 = jnp.concatenate([blocks(w_uq, 0, dqk), wq_partner.reshape(-1, H * LANES)], axis=1).astype(BF16)
    wkv2 = jnp.concatenate([blocks(w_ukv, 0, MLA_NOPE), blocks(w_ukv, MLA_NOPE, MLA_V)], axis=1).astype(BF16)
    tok = lambda w_: pl.BlockSpec((tm, w_), lambda i: (i, 0))
    pos = pl.BlockSpec((tm, LANES), lambda i: (i % (P // tm), 0))
    const = lambda a: pl.BlockSpec(a.shape, lambda i: (0, 0))
    args = (cq, ckv, kr2, cos, sin, q_norm[None], kv_norm[None], wq2, wkv2)
    out = jax.ShapeDtypeStruct((N, H * LANES), BF16)
    return pl.pallas_call(
        _mla_qkv_kernel,
        grid=(N // tm,),
        in_specs=[tok(cq.shape[1]), tok(ckv.shape[1]), tok(kr2.shape[1]), pos, pos] + [const(a) for a in args[5:]],
        out_specs=[tok(H * LANES)] * 3,
        out_shape=[out] * 3,
    )(*args)


LOG2_E = 1.4426950408889634
DENSE_Q_TILE = 1024
DENSE_Q_SUB = 256
DENSE_Q_UNROLL = 4
DENSE_K_TILE_MAX = 8320


def _dense_attn_kernel(q_ref, k_ref, v_ref, o_ref, m_ref, l_ref, acc_ref, *, scale, dv):
    h, j = pl.program_id(2), pl.program_id(3)

    @pl.when(j == 0)
    def _():
        m_ref[...] = jnp.full(m_ref.shape, -jnp.inf, F32)
        l_ref[...] = jnp.zeros(l_ref.shape, F32)
        acc_ref[...] = jnp.zeros(acc_ref.shape, F32)

    def rows(i, carry):
        for u in range(DENSE_Q_UNROLL):
            r = pl.ds(pl.multiple_of((i * DENSE_Q_UNROLL + u) * DENSE_Q_SUB, DENSE_Q_SUB), DENSE_Q_SUB)
            s = lax.dot_general(q_ref[0, r, :], k_ref[0], NT_DIMS,
                                preferred_element_type=F32) * (scale * LOG2_E)
            m_prev = m_ref[r, :]
            m_new = jnp.maximum(m_prev, jnp.max(s, axis=-1, keepdims=True))
            alpha = jnp.exp2(m_prev - m_new)
            p = jnp.exp2(s - m_new)
            l_ref[r, :] = alpha * l_ref[r, :] + jnp.sum(p, axis=-1, keepdims=True)
            acc_ref[r, :] = alpha * acc_ref[r, :] + jnp.dot(p.astype(BF16), v_ref[0],
                                                            preferred_element_type=F32)
            m_ref[r, :] = m_new
        return carry

    lax.fori_loop(0, q_ref.shape[1] // (DENSE_Q_SUB * DENSE_Q_UNROLL), rows, 0)

    for hh in range(o_ref.shape[2] // dv):
        @pl.when((j == pl.num_programs(3) - 1) & (h == hh))
        def _():
            o_ref[0, :, hh * dv:(hh + 1) * dv] = (acc_ref[...] / l_ref[...])[:, :dv]


def dense_attention(q, k_all, v_all, scale, dv):
    B, T, C = q.shape
    H, NK = C // LANES, k_all.shape[1]
    tq = min(DENSE_Q_TILE, T)
    tk = max(t for t in range(LANES, DENSE_K_TILE_MAX + 1, LANES) if NK % t == 0)
    return pl.pallas_call(
        functools.partial(_dense_attn_kernel, scale=scale, dv=dv),
        grid=(B, T // tq, H, NK // tk),
        in_specs=[pl.BlockSpec((1, tq, LANES), lambda b, i, h, j: (b, i, h)),
                  pl.BlockSpec((1, tk, LANES), lambda b, i, h, j: (b, j, h)),
                  pl.BlockSpec((1, tk, LANES), lambda b, i, h, j: (b, j, h))],
        out_specs=pl.BlockSpec((1, tq, H * dv), lambda b, i, h, j: (b, i, 0)),
        out_shape=jax.ShapeDtypeStruct((B, T, H * dv), F32),
        scratch_shapes=[pltpu.VMEM((tq, 1), F32), pltpu.VMEM((tq, 1), F32), pltpu.VMEM((tq, LANES), F32)],
        compiler_params=pltpu.CompilerParams(vmem_limit_bytes=VMEM_LIMIT_BYTES),
    )(q, k_all, v_all)


SWA_SPAN = ATTN_BLOCK + 2 * SWA_WINDOW


def _swa_kernel(q_ref, k_ref, v_ref, kc_ref, vc_ref, sink_ref, o_ref):
    n = pl.program_id(1)
    T = k_ref.shape[1]
    start = pl.multiple_of(jnp.clip(n * ATTN_BLOCK - SWA_WINDOW, 0, T - SWA_SPAN), ATTN_BLOCK)
    kw = k_ref[0, pl.ds(start, SWA_SPAN), :]
    vw = v_ref[0, pl.ds(start, SWA_SPAN), :]
    hm = _head_mask(ATTN_BLOCK)
    q = q_ref[0] * (HEAD_DIM ** -0.5)
    q4 = (jnp.concatenate([q] * GROUP_HEADS, axis=0) * hm).astype(BF16)
    rows = GROUP_HEADS * ATTN_BLOCK
    q_pos = n * ATTN_BLOCK + lax.broadcasted_iota(jnp.int32, (rows, SWA_SPAN), 0) % ATTN_BLOCK
    k_pos = start + lax.broadcasted_iota(jnp.int32, (rows, SWA_SPAN), 1)
    s_loc = lax.dot_general(q4, kw, NT_DIMS, preferred_element_type=F32)
    s_loc = jnp.where(jnp.abs(q_pos - k_pos) <= SWA_WINDOW, s_loc, -jnp.inf)
    s_ctx = lax.dot_general(q4, kc_ref[0], NT_DIMS, preferred_element_type=F32)
    sink = sink_ref[...]
    m = jnp.maximum(jnp.maximum(jnp.max(s_loc, axis=-1, keepdims=True),
                                jnp.max(s_ctx, axis=-1, keepdims=True)), sink)
    p_loc = jnp.exp(s_loc - m)
    p_ctx = jnp.exp(s_ctx - m)
    l = jnp.sum(p_loc, axis=-1, keepdims=True) + jnp.sum(p_ctx, axis=-1, keepdims=True) + jnp.exp(sink - m)
    o = (jnp.dot(p_loc.astype(BF16), vw, preferred_element_type=F32)
         + jnp.dot(p_ctx.astype(BF16), vc_ref[0], preferred_element_type=F32)) * (hm / l)
    o_ref[0] = sum(o[h * ATTN_BLOCK:(h + 1) * ATTN_BLOCK] for h in range(GROUP_HEADS))


def window_attention(q, k, v, kc, vc, sink):
    B, T, H, d = q.shape
    G = H // k.shape[2]
    n_ctx = kc.shape[1]
    C = H * d
    rep = lambda a: jnp.repeat(a, G, axis=2).reshape(a.shape[0], a.shape[1], C).astype(BF16)
    sink_rows = jnp.repeat(sink.astype(F32), ATTN_BLOCK).reshape(H * ATTN_BLOCK, 1)
    full = lambda n: pl.BlockSpec((1, n, C), lambda b, i: (b, 0, 0))
    return pl.pallas_call(
        _swa_kernel,
        grid=(B, T // ATTN_BLOCK),
        in_specs=[pl.BlockSpec((1, ATTN_BLOCK, C), lambda b, i: (b, i, 0)),
                  full(T), full(T), full(n_ctx), full(n_ctx),
                  pl.BlockSpec(sink_rows.shape, lambda b, i: (0, 0))],
        out_specs=pl.BlockSpec((1, ATTN_BLOCK, C), lambda b, i: (b, i, 0)),
        out_shape=jax.ShapeDtypeStruct((B, T, C), F32),
        compiler_params=pltpu.CompilerParams(vmem_limit_bytes=VMEM_LIMIT_BYTES),
    )(q.reshape(B, T, C), rep(k), rep(v), rep(kc), rep(vc), sink_rows)


BF16 = jnp.bfloat16
LANES = 128
SUBLANES = 8
ROW_SEGS = D_MODEL // LANES
ROW_WORDS = ROW_SEGS // 2
PEER_PICKS = PEER_HEADS * PEER_TOPK
PEER_TOPK_TOKENS = 1024
PEER_GATHER_TOKENS = 128
PEER_ACT_UNROLL = SUBLANES
VMEM_LIMIT_BYTES = 56 * 1024 * 1024


def _split_bf16(x, parts):
    out = []
    for _ in range(parts):
        p = x.astype(BF16)
        out.append(p)
        x = x - p.astype(F32)
    return out


def _topk_rows(s, k, payload=None):
    n = s.shape[0]
    iota = lax.broadcasted_iota(jnp.int32, s.shape, 0)
    vals, picked = [], []
    for _ in range(k):
        m = jnp.max(s, axis=0, keepdims=True)
        i = jnp.min(jnp.where(s == m, iota, n), axis=0, keepdims=True)
        hit = iota == i
        vals.append(m)
        picked.append(i if payload is None else jnp.max(jnp.where(hit, payload, -1), axis=0, keepdims=True))
        s = jnp.where(hit, -jnp.inf, s)
    return jnp.concatenate(vals, axis=0), jnp.concatenate(picked, axis=0)


def _peer_topk_kernel(x_ref, wq_ref, keys_ref, eidx_ref, gate_ref):
    xb = x_ref[...].astype(BF16)
    q = jnp.dot(xb, wq_ref[...], preferred_element_type=F32)
    nt = (((1,), (1,)), ((), ()))
    sv, si = [], []
    for p in range(2):
        qp = q[:, p * PEER_DKEY:(p + 1) * PEER_DKEY].astype(BF16)
        s = lax.dot_general(keys_ref[0, p], qp, nt, preferred_element_type=F32)
        v_, i_ = _topk_rows(s, PEER_TOPK)
        sv.append(v_)
        si.append(i_)
    cs, ce = [], []
    half = PEER_TOPK // 2
    for a in range(half):
        nb = PEER_TOPK if a == 0 else half
        cs.append(sv[0][a:a + 1] + sv[1][:nb])
        ce.append(si[0][a:a + 1] * PEER_NKEYS + si[1][:nb])
    cs.append(sv[0][half:] + sv[1][0:1])
    ce.append(si[0][half:] * PEER_NKEYS + si[1][0:1])
    cand_s = jnp.concatenate(cs, axis=0)
    cand_e = jnp.concatenate(ce, axis=0)
    fs, eidx = _topk_rows(cand_s, PEER_TOPK, cand_e)
    ex = jnp.exp(fs - fs[0:1])
    eidx_ref[0] = eidx
    gate_ref[0] = ex / jnp.sum(ex, axis=0, keepdims=True)


def peer_topk(h, wq, sub_keys):
    N, D = h.shape
    T = min(PEER_TOPK_TOKENS, N)
    wqb = wq.astype(BF16)
    kb = sub_keys.astype(BF16)
    eidx, gate = pl.pallas_call(
        _peer_topk_kernel,
        grid=(N // T, PEER_HEADS),
        in_specs=[pl.BlockSpec((T, D), lambda i, h_: (i, 0)),
                  pl.BlockSpec((D, 2 * PEER_DKEY), lambda i, h_: (0, h_)),
                  pl.BlockSpec((1, 2, PEER_NKEYS, PEER_DKEY), lambda i, h_: (h_, 0, 0, 0))],
        out_specs=[pl.BlockSpec((1, PEER_TOPK, T), lambda i, h_: (h_, 0, i)),
                   pl.BlockSpec((1, PEER_TOPK, T), lambda i, h_: (h_, 0, i))],
        out_shape=[jax.ShapeDtypeStruct((PEER_HEADS, PEER_TOPK, N), jnp.int32),
                   jax.ShapeDtypeStruct((PEER_HEADS, PEER_TOPK, N), F32)],
        compiler_params=pltpu.CompilerParams(vmem_limit_bytes=VMEM_LIMIT_BYTES),
    )(h, wqb, kb)
    return eidx.reshape(PEER_PICKS, N), gate.reshape(PEER_PICKS, N)


def pack_expert_table(tab):
    E, D = tab.shape
    return pl.pallas_call(
        _pack_table_kernel,
        grid=(E // PACK_ROWS,),
        in_specs=[pl.BlockSpec((PACK_ROWS, D), lambda i: (i, 0))],
        out_specs=pl.BlockSpec((PACK_ROWS * ROW_WORDS, LANES), lambda i: (i, 0)),
        out_shape=jax.ShapeDtypeStruct((E * ROW_WORDS, LANES), jnp.uint32),
    )(tab)


PACK_ROWS = 512


def _pack_table_kernel(t_ref, o_ref):
    bits = lambda a: lax.bitcast_convert_type(a.astype(BF16).astype(F32), jnp.uint32)
    for s_ in range(ROW_WORDS):
        lo = bits(t_ref[:, (2 * s_) * LANES:(2 * s_ + 1) * LANES])
        hi = bits(t_ref[:, (2 * s_ + 1) * LANES:(2 * s_ + 2) * LANES])
        o_ref[pl.ds(s_, PACK_ROWS, stride=ROW_WORDS), :] = (hi & jnp.uint32(0xFFFF0000)) | (lo >> 16)


def _stage_rows(idx_ref, tab_ref, stage_ref, t):
    for k in range(PEER_PICKS):
        off = pl.multiple_of(idx_ref[t, k], ROW_WORDS)
        stage_ref[k * ROW_WORDS:(k + 1) * ROW_WORDS, :] = tab_ref[pl.ds(off, ROW_WORDS), :]
    return pltpu.bitcast(stage_ref[...], BF16)


def _peer_act_kernel(idx_ref, x_ref, gate_ref, tab_ref, seg_mask_ref, group_ref, w_ref,
                     stage_ref, rows_ref):
    T = x_ref.shape[0]
    U = stage_ref.shape[0]

    sub = lax.broadcasted_iota(jnp.int32, (SUBLANES, PEER_PICKS * ROW_SEGS), 0)

    def tokens(g, carry):
        tile = jnp.zeros((SUBLANES, PEER_PICKS * ROW_SEGS), F32)
        for j in range(U):
            t = g * U + j
            sb = _stage_rows(idx_ref, tab_ref, stage_ref.at[j], t)
            xs = jnp.concatenate(_split_bf16(x_ref[t], 2), axis=0)
            r = lax.dot_general(xs, sb, NT_DIMS, preferred_element_type=F32)
            r = jnp.sum(r * seg_mask_ref[...], axis=0, keepdims=True)
            tile = jnp.where(sub == j, r, tile)
        rows_ref[g] = tile
        return carry

    lax.fori_loop(0, T // U, tokens, 0)
    rows = rows_ref[...].reshape(T, PEER_PICKS * ROW_SEGS)
    act = jnp.zeros((T, PEER_PICKS), F32)
    for piece in _split_bf16(rows, 3):
        act = act + jnp.dot(piece, group_ref[...], preferred_element_type=F32)
    w_ref[...] = gate_ref[...] * (0.5 * act * (1.0 + lax.erf(act * (2.0 ** -0.5))))


def _peer_out_kernel(idx_ref, w_ref, x_ref, g_ref, tab_ref, expand_ref, seg_mask_ref, f_ref, stage_ref):
    T = w_ref.shape[0]
    U = stage_ref.shape[0]

    def tokens(g, carry):
        w8 = w_ref[pl.ds(pl.multiple_of(g * U, U), U), :]
        hi, lo = _split_bf16(w8, 2)
        lhs = jnp.concatenate([jnp.broadcast_to(p[j:j + 1], (SUBLANES, PEER_PICKS))
                               for j in range(U) for p in (hi, lo)], axis=0)
        wrep = jnp.dot(lhs, expand_ref[...], preferred_element_type=F32)
        for j in range(U):
            t = g * U + j
            sb = _stage_rows(idx_ref, tab_ref, stage_ref.at[j], t)
            wsel = (wrep[j * 2 * SUBLANES:(j + 1) * 2 * SUBLANES] * seg_mask_ref[...]).astype(BF16)
            o = jnp.dot(wsel, sb, preferred_element_type=F32)
            f_ref[t] = x_ref[t] + g_ref[0] * (o[:SUBLANES] + o[SUBLANES:])
        return carry

    lax.fori_loop(0, T // U, tokens, 0)


def _peer_constants():
    cols = np.arange(PEER_PICKS * ROW_SEGS)
    seg_mask = (cols[None, :] % ROW_SEGS == np.arange(2 * SUBLANES)[:, None] % SUBLANES)
    group = (cols[:, None] // ROW_SEGS == np.arange(PEER_PICKS)[None, :])
    return (jnp.asarray(seg_mask, F32), jnp.asarray(group, BF16), jnp.asarray(group.T, BF16))


def peer_ffn(h, x, gate2, group_tokens, wq, sub_keys, u_packed, v_packed):
    N, D = h.shape
    T = PEER_GATHER_TOKENS
    eidx, gate = peer_topk(h, wq, sub_keys)
    seg_mask, group, expand = _peer_constants()
    rows3 = lambda a: a.reshape(a.shape[0], ROW_SEGS, LANES)
    offs = eidx.T * ROW_WORDS
    idx_spec = pl.BlockSpec((T, PEER_PICKS), lambda i: (i, 0), memory_space=pltpu.SMEM)
    tab_spec = pl.BlockSpec(u_packed.shape, lambda i: (0, 0), pipeline_mode=pl.Buffered(1))
    tok_spec = pl.BlockSpec((T, ROW_SEGS, LANES), lambda i: (i, 0, 0))
    const = lambda shape: pl.BlockSpec(shape, lambda i: (0, 0))
    params = pltpu.CompilerParams(vmem_limit_bytes=VMEM_LIMIT_BYTES)
    w = pl.pallas_call(
        _peer_act_kernel,
        grid=(N // T,),
        in_specs=[idx_spec, tok_spec,
                  pl.BlockSpec((T, PEER_PICKS), lambda i: (i, 0)),
                  tab_spec, const(seg_mask.shape), const(group.shape)],
        out_specs=pl.BlockSpec((T, PEER_PICKS), lambda i: (i, 0)),
        out_shape=jax.ShapeDtypeStruct((N, PEER_PICKS), F32),
        scratch_shapes=[pltpu.VMEM((PEER_ACT_UNROLL, PEER_PICKS * ROW_WORDS, LANES), jnp.uint32),
                        pltpu.VMEM((T // SUBLANES, SUBLANES, PEER_PICKS * ROW_SEGS), F32)],
        compiler_params=params,
    )(offs, rows3(h), gate.T, u_packed, seg_mask, group)
    out = pl.pallas_call(
        _peer_out_kernel,
        grid=(N // T,),
        in_specs=[idx_spec,
                  pl.BlockSpec((T, PEER_PICKS), lambda i: (i, 0)),
                  tok_spec,
                  pl.BlockSpec((1, ROW_SEGS, LANES), lambda i: (i // (group_tokens // T), 0, 0)),
                  tab_spec, const(expand.shape), const(seg_mask.shape)],
        out_specs=tok_spec,
        out_shape=jax.ShapeDtypeStruct((N, ROW_SEGS, LANES), F32),
        scratch_shapes=[pltpu.VMEM((SUBLANES, PEER_PICKS * ROW_WORDS, LANES), jnp.uint32)],
        compiler_params=params,
    )(offs, w, rows3(x), rows3(gate2), v_packed, expand, seg_mask)
    return out.reshape(N, D)


PROJ_TOKENS = 512
MOD_ROWS = SUBLANES
IN_ALIGNED = tuple(i for i, s_ in enumerate(IN_SIZES) if s_ % LANES == 0)
IN_SMALL = tuple(i for i, s_ in enumerate(IN_SIZES) if s_ % LANES)
IN_MXU_ONLY = (1, 2, 4, 12)


def _rms_modulate(x, gain, scale1p, shift):
    r = lax.rsqrt(jnp.mean(x * x, axis=-1, keepdims=True) + EPS)
    return (x * r * gain) * scale1p + shift


def _in_proj_kernel(x_ref, mod_ref, w_ref, *out_refs):
    mod = mod_ref[0]
    h = _rms_modulate(x_ref[...], mod[0:1], mod[1:2], mod[2:3])
    y = jnp.dot(h.astype(BF16), w_ref[...], preferred_element_type=F32)
    off = 0
    for o_ref in out_refs:
        o_ref[...] = y[:, off:off + o_ref.shape[1]].astype(o_ref.dtype)
        off += o_ref.shape[1]


def in_projection(x, mod, w_in, group_tokens):
    N, D = x.shape
    T = min(PROJ_TOKENS, group_tokens)
    starts = np.cumsum((0,) + IN_SIZES)
    group_cols = lambda i: np.arange(starts[i], starts[i + 1])
    small = [i for i in IN_SMALL if i != MLA_KR]
    n_small = sum(IN_SIZES[i] for i in small)
    lane_pad = lambda w_, lo, hi: jnp.pad(w_, ((0, 0), (lo, hi)))
    w_kr = w_in[:, group_cols(MLA_KR)]
    wp = jnp.concatenate(
        [w_in[:, np.concatenate([group_cols(i) for i in IN_ALIGNED])],
         lane_pad(w_in[:, np.concatenate([group_cols(i) for i in small])], 0, -n_small % LANES),
         lane_pad(w_kr, MLA_NOPE, LANES - MLA_NOPE - MLA_ROPE),
         lane_pad(w_kr[:, ROPE_PARTNER], MLA_NOPE, LANES - MLA_NOPE - MLA_ROPE)], axis=1).astype(BF16)
    widths = [IN_SIZES[i] for i in IN_ALIGNED] + [n_small + (-n_small % LANES), 2 * LANES]
    outs = pl.pallas_call(
        _in_proj_kernel,
        grid=(N // T,),
        in_specs=[pl.BlockSpec((T, D), lambda i: (i, 0)),
                  pl.BlockSpec((1, MOD_ROWS, D), lambda i: (i // (group_tokens // T), 0, 0)),
                  pl.BlockSpec(wp.shape, lambda i: (0, 0))],
        out_specs=[pl.BlockSpec((T, w_), lambda i: (i, 0)) for w_ in widths],
        out_shape=[jax.ShapeDtypeStruct((N, w_), BF16 if i in IN_MXU_ONLY else F32)
                   for i, w_ in zip(IN_ALIGNED + (None, None), widths)],
        compiler_params=pltpu.CompilerParams(vmem_limit_bytes=VMEM_LIMIT_BYTES),
    )(x, mod, wp)
    groups = dict(zip(IN_ALIGNED, outs[:-2]))
    off = 0
    for i in small:
        groups[i] = outs[-2][:, off:off + IN_SIZES[i]]
        off += IN_SIZES[i]
    groups[MLA_KR] = outs[-1]
    return [groups[i] for i in range(len(IN_SIZES))]


def _out_proj_kernel(ya_ref, hl_ref, mo_ref, yc_ref, yd_ref, x_ref, mod_ref, w_ref, xo_ref, h2_ref):
    yb = hl_ref[...] * jax.nn.sigmoid(mo_ref[...])
    y = jnp.concatenate([ya_ref[...], yb, yc_ref[...], yd_ref[...]], axis=-1).astype(BF16)
    mod = mod_ref[0]
    xn = x_ref[...] + mod[0:1] * jnp.dot(y, w_ref[...], preferred_element_type=F32)
    xo_ref[...] = xn
    h2_ref[...] = _rms_modulate(xn, mod[1:2], mod[2:3], mod[3:4])


def out_projection(ya, hl, mo, yc, yd, x, mod, w_out, group_tokens):
    N, D = x.shape
    T = min(PROJ_TOKENS, group_tokens)
    part = pl.BlockSpec((T, GROUP_WIDTH), lambda i: (i, 0))
    tok = pl.BlockSpec((T, D), lambda i: (i, 0))
    return pl.pallas_call(
        _out_proj_kernel,
        grid=(N // T,),
        in_specs=[part, part, part, part, part, tok,
                  pl.BlockSpec((1, MOD_ROWS, D), lambda i: (i // (group_tokens // T), 0, 0)),
                  pl.BlockSpec(w_out.shape, lambda i: (0, 0))],
        out_specs=[tok, tok],
        out_shape=[jax.ShapeDtypeStruct((N, D), F32)] * 2,
        compiler_params=pltpu.CompilerParams(vmem_limit_bytes=VMEM_LIMIT_BYTES),
    )(ya, hl, mo, yc, yd, x, mod, w_out.astype(BF16))


def _adaln_kernel(c_ref, w_ref, b_ref, o_ref):
    c = c_ref[...]
    a = c * jax.nn.sigmoid(c)
    o_ref[...] = jnp.dot(a.astype(BF16), w_ref[...].astype(BF16), preferred_element_type=F32) + b_ref[...]


def adaln_linear(c, w_ada, b_ada):
    R, D = c.shape
    rows = -R % SUBLANES + R
    out = pl.pallas_call(
        _adaln_kernel,
        grid=(w_ada.shape[1] // D,),
        in_specs=[pl.BlockSpec((rows, D), lambda j: (0, 0)),
                  pl.BlockSpec((D, D), lambda j: (0, j)),
                  pl.BlockSpec((1, D), lambda j: (0, j))],
        out_specs=pl.BlockSpec((rows, D), lambda j: (0, j)),
        out_shape=jax.ShapeDtypeStruct((rows, w_ada.shape[1]), F32),
    )(jnp.pad(c, ((0, rows - R), (0, 0))), w_ada, b_ada[None])
    return out[:R]


def _mod_rows(*rows):
    m = jnp.stack([jnp.broadcast_to(r, rows[-1].shape) for r in rows], axis=1)
    return jnp.pad(m, ((0, 0), (0, MOD_ROWS - len(rows)), (0, 0)))


def hybrid_layer(x, xc, c, c_ctx, need_ctx, rope_mla, angs_swa,
                 norm1_g, norm2_g, w_ada, b_ada, w_in, na_rpb, ml_conv, ml_gate_b,
                 mla_q_norm, mla_w_uq, mla_kv_norm, mla_w_ukv, swa_sink, w_out,
                 peer_wq, peer_keys, peer_u, peer_v):
    B, T, D = x.shape
    Tc = xc.shape[1]
    H = GROUP_HEADS
    flat = lambda a: a.reshape(-1, a.shape[-1])
    ada = adaln_linear(jnp.concatenate([c, c_ctx[None]], axis=0), w_ada, b_ada)
    sh1, sc1, g1, sh2, sc2, g2 = jnp.split(ada[:B], 6, axis=-1)
    sh1c, sc1c, g1c, sh2c, sc2c, g2c = jnp.split(ada[B:], 6, axis=-1)
    lat = in_projection(flat(x), _mod_rows(norm1_g, 1.0 + sc1, sh1), w_in, T)
    cx = in_projection(flat(xc), _mod_rows(norm1_g, 1.0 + sc1c, sh1c), w_in, B * Tc)
    (na_q, na_k, na_v, ml_qk, ml_v, ml_o, ml_g,
     mla_cq, mla_ckv, mla_kr, sw_q, sw_k, sw_v) = [a.reshape(B, T, -1) for a in lat]
    (na_qc, na_kc, na_vc, ml_qkc, ml_vc, ml_oc, ml_gc,
     mla_cqc, mla_ckvc, mla_krc, sw_qc, sw_kc, sw_vc) = [a.reshape(B, Tc, -1) for a in cx]
    attn_scale = HEAD_DIM ** -0.5
    mla_scale = (MLA_NOPE + MLA_ROPE) ** -0.5
    kc_a, vc_a = heads(na_kc, H), heads(na_vc, H)
    y_a = neighbourhood_attention(na_q, na_k, na_v, na_kc, na_vc, na_rpb)
    h_lat, h_ctx = mlstm_mixer((ml_qk, ml_v, ml_g), (ml_qkc, ml_vc, ml_gc), ml_conv, ml_gate_b)
    no_rope = (jnp.ones((Tc, LANES), F32), jnp.zeros((Tc, LANES), F32))
    q_m, k_m, v_m = [a.reshape(B, T, -1) for a in
                     mla_qkv(flat(mla_cq), flat(mla_ckv), flat(mla_kr), mla_q_norm, mla_w_uq, mla_kv_norm, mla_w_ukv,
                             *rope_mla)]
    qc_m, kc_m, vc_m = [a.reshape(B, Tc, -1) for a in
                        mla_qkv(flat(mla_cqc), flat(mla_ckvc), flat(mla_krc), mla_q_norm, mla_w_uq, mla_kv_norm,
                                mla_w_ukv, *no_rope)]
    y_c = dense_attention(q_m, jnp.concatenate([kc_m, k_m], axis=1), jnp.concatenate([vc_m, v_m], axis=1),
                          mla_scale, MLA_V)
    kc_d, vc_d = heads(sw_kc, SWA_KV_HEADS), heads(sw_vc, SWA_KV_HEADS)
    y_d = window_attention(rope_2d(heads(sw_q, H), angs_swa), rope_2d(heads(sw_k, SWA_KV_HEADS), angs_swa),
                           heads(sw_v, SWA_KV_HEADS), kc_d, vc_d, swa_sink)
    x2, h2 = out_projection(flat(y_a), h_lat.reshape(B * T, GROUP_WIDTH), flat(ml_o), flat(y_c), flat(y_d),
                            flat(x), _mod_rows(g1, norm2_g, 1.0 + sc2, sh2), w_out, T)
    u_packed, v_packed = pack_expert_table(peer_u), pack_expert_table(peer_v)
    x = peer_ffn(h2, x2, g2, T, peer_wq, peer_keys, u_packed, v_packed).reshape(B, T, D)
    if not need_ctx:
        return x, None
    xc2, h2c = out_projection(flat(ctx_attn(heads(na_qc, H), kc_a, vc_a, attn_scale)),
                              h_ctx.reshape(B * Tc, GROUP_WIDTH), flat(ml_oc),
                              ctx_attn(heads(qc_m, H), heads(kc_m, H), heads(vc_m, H), mla_scale)
                              .reshape(B, Tc, H, LANES)[..., :MLA_V].reshape(B * Tc, GROUP_WIDTH),
                              flat(ctx_attn(heads(sw_qc, H), kc_d, vc_d, attn_scale, swa_sink)),
                              flat(xc), _mod_rows(g1c, norm2_g, 1.0 + sc2c, sh2c), w_out, B * Tc)
    xc = peer_ffn(h2c, xc2, g2c, B * Tc, peer_wq, peer_keys, u_packed, v_packed).reshape(B, Tc, D)
    return x, xc


def _final_rmsnorm_kernel(x_ref, g_ref, o_ref):
    x = x_ref[...]
    o_ref[...] = x * lax.rsqrt(jnp.mean(x * x, axis=-1, keepdims=True) + EPS) * g_ref[...]


def final_rmsnorm(x, g):
    B, T, D = x.shape
    rows = 1024
    xf = x.reshape(B * T, D)
    out = pl.pallas_call(
        _final_rmsnorm_kernel,
        grid=(B * T // rows,),
        in_specs=[pl.BlockSpec((rows, D), lambda i: (i, 0)), pl.BlockSpec((1, D), lambda i: (0, 0))],
        out_specs=pl.BlockSpec((rows, D), lambda i: (i, 0)),
        out_shape=jax.ShapeDtypeStruct((B * T, D), x.dtype),
    )(xf, g.reshape(1, D))
    return out.reshape(B, T, D)


def kernel(x, c, ctx, c_ctx, norm1_g, norm2_g, w_ada, b_ada, w_in, na_rpb, ml_conv, ml_gate_b,
           mla_q_norm, mla_w_uq, mla_kv_norm, mla_w_ukv, swa_sink, w_out,
           peer_wq, peer_keys, peer_u, peer_v, final_norm_g):
    T = x.shape[1]
    rope_mla = mla_rope_tables(T)
    angs_swa = axial_angles(T, HEAD_DIM)
    xc = ctx
    for l in range(DEPTH):
        x, xc = hybrid_layer(x, xc, c, c_ctx, l < DEPTH - 1, rope_mla, angs_swa,
                             norm1_g[l], norm2_g[l], w_ada[l], b_ada[l], w_in[l], na_rpb[l],
                             ml_conv[l], ml_gate_b[l], mla_q_norm[l], mla_w_uq[l], mla_kv_norm[l],
                             mla_w_ukv[l], swa_sink[l], w_out[l], peer_wq[l], peer_keys[l],
                             peer_u[l], peer_v[l])
    return final_rmsnorm(x, final_norm_g)
```

```python
import functools

import jax
import jax.numpy as jnp
from jax import lax
import numpy as np
from jax.experimental import pallas as pl
from jax.experimental.pallas import tpu as pltpu

D_MODEL = 1024
DEPTH = 2

CTX_LEN = 256
GRID_W = 64
N_MIXERS = 4
MIX_WIDTH = D_MODEL
GROUP_WIDTH = MIX_WIDTH // N_MIXERS
GROUP_HEADS = 4
HEAD_DIM = GROUP_WIDTH // GROUP_HEADS
NA_ROWS = 8
NA_COLS = 16
ML_CHUNK = 64
MLA_Q_RANK = 256
MLA_KV_RANK = 128
MLA_NOPE = 64
MLA_ROPE = 32
MLA_V = 64
SWA_KV_HEADS = 2
SWA_WINDOW = 128
ATTN_BLOCK = 128
PEER_HEADS = 8
PEER_NKEYS = 128
PEER_DKEY = 128
PEER_TOPK = 16
ROPE_BASE = 10000.0
EPS = 1e-6
IN_SIZES = (GROUP_WIDTH, GROUP_WIDTH, GROUP_WIDTH,
            2 * GROUP_WIDTH, GROUP_WIDTH, GROUP_WIDTH, 4 * GROUP_HEADS,
            MLA_Q_RANK, MLA_KV_RANK, MLA_ROPE,
            GROUP_WIDTH, SWA_KV_HEADS * HEAD_DIM, SWA_KV_HEADS * HEAD_DIM)
F32 = jnp.float32


def heads(a, h):
    return a.reshape(a.shape[:-1] + (h, a.shape[-1] // h))


def axial_angles(T, rot_dim):
    t = jnp.arange(T)
    row = (t // GRID_W).astype(F32)
    col = (t % GRID_W).astype(F32)
    half = rot_dim // 2
    inv = 1.0 / (ROPE_BASE ** (jnp.arange(0, half, 2, dtype=F32) / half))
    return row[:, None] * inv, col[:, None] * inv


def rope_1d(x, ang):
    cos = jnp.cos(ang)[None, :, None, :]
    sin = jnp.sin(ang)[None, :, None, :]
    x1, x2 = jnp.split(x.astype(F32), 2, axis=-1)
    return jnp.concatenate([x1 * cos - x2 * sin, x1 * sin + x2 * cos], axis=-1)


def rope_2d(x, angs):
    xr, xc = jnp.split(x, 2, axis=-1)
    return jnp.concatenate([rope_1d(xr, angs[0]), rope_1d(xc, angs[1])], axis=-1).astype(x.dtype)


NT_DIMS = (((1,), (1,)), ((), ()))


def _ctx_attn_kernel(q_ref, k_ref, v_ref, sink_ref, o_ref, *, scale, use_sink):
    s = lax.dot_general(q_ref[0, 0].astype(BF16), k_ref[0, 0].astype(BF16), NT_DIMS,
                        preferred_element_type=F32) * scale
    m = jnp.max(s, axis=-1, keepdims=True)
    if use_sink:
        sink = sink_ref[pl.program_id(1)]
        m = jnp.maximum(m, sink)
    p = jnp.exp(s - m)
    l = jnp.sum(p, axis=-1, keepdims=True)
    if use_sink:
        l = l + jnp.exp(sink - m)
    o_ref[0, 0] = jnp.dot(p.astype(BF16), v_ref[0, 0].astype(BF16), preferred_element_type=F32) / l


def ctx_attn(q, k, v, scale, sink=None):
    B, Tc, H, _ = q.shape
    rep = H // k.shape[2]
    hm = lambda a: jnp.swapaxes(a, 1, 2)
    q, k, v = hm(q), hm(jnp.repeat(k, rep, axis=2)), hm(jnp.repeat(v, rep, axis=2))
    blk = lambda a: pl.BlockSpec((1, 1, Tc, a.shape[-1]), lambda b, h: (b, h, 0, 0))
    out = pl.pallas_call(
        functools.partial(_ctx_attn_kernel, scale=scale, use_sink=sink is not None),
        grid=(B, H),
        in_specs=[blk(q), blk(k), blk(v), pl.BlockSpec(memory_space=pltpu.SMEM)],
        out_specs=blk(v),
        out_shape=jax.ShapeDtypeStruct(v.shape, F32),
    )(q, k, v, jnp.zeros((H,), F32) if sink is None else sink.astype(F32))
    return jnp.swapaxes(out, 1, 2).reshape(B, Tc, -1)
NA_SPAN = NA_ROWS * GRID_W


def _head_mask(width):
    rows = lax.broadcasted_iota(jnp.int32, (GROUP_HEADS * width, GROUP_WIDTH), 0) // width
    cols = lax.broadcasted_iota(jnp.int32, (GROUP_HEADS * width, GROUP_WIDTH), 1) // HEAD_DIM
    return (rows == cols).astype(F32)


def _na_kernel(q_ref, k_ref, v_ref, kc_ref, vc_ref, bias_ref, o_ref):
    r = pl.program_id(1)
    rows = pl.num_programs(1)
    rs = jnp.clip(r - NA_ROWS // 2, 0, rows - NA_ROWS)
    start = pl.multiple_of(rs * GRID_W, GRID_W)
    kw = k_ref[0, pl.ds(start, NA_SPAN), :]
    vw = v_ref[0, pl.ds(start, NA_SPAN), :]
    hm = _head_mask(GRID_W)
    q = q_ref[0] * (HEAD_DIM ** -0.5)
    q4 = (jnp.concatenate([q] * GROUP_HEADS, axis=0) * hm).astype(BF16)
    s_loc = lax.dot_general(q4, kw, NT_DIMS, preferred_element_type=F32) + bias_ref[rs - r + NA_ROWS - 1]
    s_ctx = lax.dot_general(q4, kc_ref[0], NT_DIMS, preferred_element_type=F32)
    m = jnp.maximum(jnp.max(s_loc, axis=-1, keepdims=True), jnp.max(s_ctx, axis=-1, keepdims=True))
    p_loc = jnp.exp(s_loc - m)
    p_ctx = jnp.exp(s_ctx - m)
    l = jnp.sum(p_loc, axis=-1, keepdims=True) + jnp.sum(p_ctx, axis=-1, keepdims=True)
    o = (jnp.dot(p_loc.astype(BF16), vw, preferred_element_type=F32)
         + jnp.dot(p_ctx.astype(BF16), vc_ref[0], preferred_element_type=F32)) * (hm / l)
    o_ref[0] = sum(o[h * GRID_W:(h + 1) * GRID_W] for h in range(GROUP_HEADS))


def _na_bias_table(rpb):
    c = np.arange(GRID_W)
    col_start = np.clip(c - NA_COLS // 2, 0, GRID_W - NA_COLS)
    valid = (c[None, :] >= col_start[:, None]) & (c[None, :] < col_start[:, None] + NA_COLS)
    dc = np.clip(c[None, :] - c[:, None] + NA_COLS - 1, 0, 2 * NA_COLS - 2)
    dr = np.arange(NA_ROWS)[:, None] + np.arange(NA_ROWS)[None, :]
    t = rpb.astype(F32)[:, dr][..., dc]
    t = jnp.where(valid[None, None, None], t, -jnp.inf)
    return jnp.transpose(t, (1, 0, 3, 2, 4)).reshape(NA_ROWS, GROUP_HEADS * GRID_W, NA_SPAN)


def neighbourhood_attention(q, k, v, kc, vc, rpb):
    B, T, C = q.shape
    rows = T // GRID_W
    n_ctx = kc.shape[1]
    bias = _na_bias_table(rpb)
    full = lambda n: pl.BlockSpec((1, n, C), lambda b, r: (b, 0, 0))
    return pl.pallas_call(
        _na_kernel,
        grid=(B, rows),
        in_specs=[pl.BlockSpec((1, GRID_W, C), lambda b, r: (b, r, 0)),
                  full(T), full(T), full(n_ctx), full(n_ctx),
                  pl.BlockSpec(bias.shape, lambda b, r: (0, 0, 0))],
        out_specs=pl.BlockSpec((1, GRID_W, C), lambda b, r: (b, r, 0)),
        out_shape=jax.ShapeDtypeStruct((B, T, C), F32),
        compiler_params=pltpu.CompilerParams(vmem_limit_bytes=VMEM_LIMIT_BYTES),
    )(q, k.astype(BF16), v.astype(BF16), kc.astype(BF16), vc.astype(BF16), bias)


def short_conv(a, w):
    T = a.shape[1]
    pad = w.shape[0] // 2
    ap = jnp.pad(a, ((0, 0), (pad, pad), (0, 0)))
    out = ap[:, :T] * w[0]
    for j in range(1, w.shape[0]):
        out = out + ap[:, j:j + T] * w[j]
    return out


ML_CHUNKS_PER_STEP = CTX_LEN // ML_CHUNK


def _bmm(a, b, contract):
    return lax.dot_general(a.astype(BF16), b.astype(BF16), (contract, ((0,), (0,))),
                           preferred_element_type=F32)


def _mlstm_chunk(qt, kt, vt, irow, brow, state, backward):
    L = ML_CHUNK
    C, nrow, m = state
    row = lax.broadcasted_iota(jnp.int32, (1, L, L), 1)
    col = lax.broadcasted_iota(jnp.int32, (1, L, L), 2)
    seen = (row <= col) if backward else (row >= col)
    eye = row == col

    def as_col(r):
        return jnp.sum(jnp.where(eye, r, 0.0), axis=2, keepdims=True)

    blast = brow[:, :, 0:1] if backward else brow[:, :, L - 1:L]
    rrow = brow - irow
    bcol = as_col(brow)
    d_log = jnp.where(seen, bcol - rrow, -jnp.inf)
    inter = bcol + m
    m_t = jnp.maximum(inter, jnp.max(d_log, axis=2, keepdims=True))
    w = jnp.exp(d_log - m_t)
    a = jnp.exp(inter - m_t)
    s = _bmm(qt, kt, ((2,), (2,))) * w
    num = _bmm(s, vt, ((2,), (1,))) + a * _bmm(qt, C, ((2,), (1,)))
    den = jnp.sum(s, axis=2, keepdims=True) + a * jnp.sum(qt * nrow, axis=2, keepdims=True)
    h = num / jnp.maximum(jnp.abs(den), jnp.exp(-m_t))
    g = blast - rrow
    m_new = jnp.maximum(blast + m, jnp.max(g, axis=2, keepdims=True))
    kw = kt * as_col(jnp.exp(g - m_new))
    decay = jnp.exp(blast + m - m_new)
    C = decay * C + _bmm(jnp.swapaxes(kw, 1, 2), vt, ((2,), (1,)))
    nrow = decay * nrow + jnp.sum(kw, axis=1, keepdims=True)
    return h, (C, nrow, m_new)


def _mlstm_kernel(qf_ref, kf_ref, vf_ref, if_ref, bf_ref, qb_ref, kb_ref, vb_ref, ib_ref, bb_ref,
                  hf_ref, hb_ref, c_ref, n_ref, m_ref):
    N, L = qf_ref.shape[0], ML_CHUNK

    @pl.when(pl.program_id(0) == 0)
    def _():
        c_ref[...] = jnp.zeros(c_ref.shape, F32)
        n_ref[...] = jnp.zeros(n_ref.shape, F32)
        m_ref[...] = jnp.zeros(m_ref.shape, F32)

    fwd = (c_ref[:N], n_ref[:N], m_ref[:N])
    bwd = (c_ref[N:], n_ref[N:], m_ref[N:])
    for c in range(ML_CHUNKS_PER_STEP):
        rows = slice(c * L, (c + 1) * L)
        h, fwd = _mlstm_chunk(qf_ref[:, rows, :], kf_ref[:, rows, :], vf_ref[:, rows, :],
                              if_ref[:, 0, c:c + 1, :], bf_ref[:, 0, c:c + 1, :], fwd, False)
        hf_ref[:, rows, :] = h
        cb = ML_CHUNKS_PER_STEP - 1 - c
        rows = slice(cb * L, (cb + 1) * L)
        h, bwd = _mlstm_chunk(qb_ref[:, rows, :], kb_ref[:, rows, :], vb_ref[:, rows, :],
                              ib_ref[:, 0, cb:cb + 1, :], bb_ref[:, 0, cb:cb + 1, :], bwd, True)
        hb_ref[:, rows, :] = h
    for i, ref in enumerate((c_ref, n_ref, m_ref)):
        ref[:N] = fwd[i]
        ref[N:] = bwd[i]


def mlstm_scan(q, k, v, gates_f, gates_b, n_ctx):
    B, T, H, d = q.shape
    CB, L = ML_CHUNKS_PER_STEP, ML_CHUNK
    assert n_ctx == CB * L and T % (CB * L) == 0
    N, steps = B * H, T // (CB * L)
    hm = lambda a: jnp.swapaxes(a, 1, 2).reshape(N, T, d)
    gates = lambda a: jnp.swapaxes(a, 1, 2).reshape(N, steps, CB, L)
    chunked = lambda a: a.reshape(B, T // L, L, H)
    b_f = jnp.cumsum(chunked(gates_f[1]), axis=2).reshape(B, T, H)
    b_b = lax.cumsum(chunked(gates_b[1]), axis=2, reverse=True).reshape(B, T, H)
    back = lambda j: jnp.where(j == 0, 0, steps - j)
    seq_f = pl.BlockSpec((N, CB * L, d), lambda j: (0, j, 0))
    seq_b = pl.BlockSpec((N, CB * L, d), lambda j: (0, back(j), 0))
    gate_f = pl.BlockSpec((N, 1, CB, L), lambda j: (0, j, 0, 0))
    gate_b = pl.BlockSpec((N, 1, CB, L), lambda j: (0, back(j), 0, 0))
    qh, kh, vh = hm(q), hm(k), hm(v)
    hf, hb = pl.pallas_call(
        _mlstm_kernel,
        grid=(steps,),
        in_specs=[seq_f, seq_f, seq_f, gate_f, gate_f, seq_b, seq_b, seq_b, gate_b, gate_b],
        out_specs=[seq_f, seq_b],
        out_shape=[jax.ShapeDtypeStruct((N, T, d), F32)] * 2,
        scratch_shapes=[pltpu.VMEM((2 * N, d, d), F32), pltpu.VMEM((2 * N, 1, d), F32),
                        pltpu.VMEM((2 * N, 1, 1), F32)],
        compiler_params=pltpu.CompilerParams(vmem_limit_bytes=VMEM_LIMIT_BYTES),
    )(qh, kh, vh, gates(gates_f[0]), gates(b_f), qh, kh, vh, gates(gates_b[0]), gates(b_b))
    return jnp.swapaxes((hf + hb).reshape(B, H, T, d), 1, 2)


def mlstm_prep(qk, v, gates, conv_w, gate_b):
    qk = jax.nn.silu(short_conv(qk, conv_w))
    q, k = jnp.split(qk, 2, axis=-1)
    g = (gates + gate_b).astype(F32)
    i_f, f_f, i_b, f_b = jnp.split(g, 4, axis=-1)
    return (heads(q, GROUP_HEADS) * HEAD_DIM ** -0.5, heads(k, GROUP_HEADS), heads(v, GROUP_HEADS),
            (i_f, jax.nn.log_sigmoid(f_f), i_b, jax.nn.log_sigmoid(f_b)))


def mlstm_mixer(lat, ctx, conv_w, gate_b):
    ql, kl, vl, gl = mlstm_prep(lat[0], lat[1], lat[2], conv_w, gate_b)
    qc, kc, vc, gc = mlstm_prep(ctx[0], ctx[1], ctx[2], conv_w, gate_b)
    Tc = qc.shape[1]
    cat = lambda c_, l_: jnp.concatenate([c_, l_], axis=1)
    h = mlstm_scan(cat(qc, ql), cat(kc, kl), cat(vc, vl),
                   (cat(gc[0], gl[0]), cat(gc[1], gl[1])), (cat(gc[2], gl[2]), cat(gc[3], gl[3])), Tc)
    return h[:, Tc:], h[:, :Tc]


MLA_KR = 9
ROPE_PARTNER = np.concatenate([np.arange(q_, q_ + MLA_ROPE // 4) for q_ in
                               (MLA_ROPE // 4, 0, 3 * MLA_ROPE // 4, MLA_ROPE // 2)])


def mla_rope_tables(T):
    ang_r, ang_c = axial_angles(T, MLA_ROPE)
    cos = jnp.concatenate([jnp.cos(ang_r)] * 2 + [jnp.cos(ang_c)] * 2, axis=1)
    sin = jnp.concatenate([-jnp.sin(ang_r), jnp.sin(ang_r), -jnp.sin(ang_c), jnp.sin(ang_c)], axis=1)
    pad = LANES - MLA_NOPE - MLA_ROPE
    return (jnp.concatenate([jnp.ones((T, MLA_NOPE), F32), cos, jnp.zeros((T, pad), F32)], axis=1),
            jnp.concatenate([jnp.zeros((T, MLA_NOPE), F32), sin, jnp.zeros((T, pad), F32)], axis=1))


def _mla_qkv_kernel(cq_ref, ckv_ref, kr_ref, cos_ref, sin_ref, qn_ref, kvn_ref, wq_ref, wkv_ref,
                    q_ref, k_ref, v_ref):
    def up(x, g, w_ref):
        y = x * lax.rsqrt(jnp.mean(x * x, axis=-1, keepdims=True) + EPS) * g
        return jnp.dot(y.astype(BF16), w_ref[...], preferred_element_type=F32)

    W = GROUP_HEADS * LANES
    per_head = lambda a: jnp.concatenate([a] * GROUP_HEADS, axis=1)
    cos, sin = cos_ref[...], sin_ref[...]
    q2 = up(cq_ref[...], qn_ref[...], wq_ref)
    q_ref[...] = (q2[:, :W] * per_head(cos) + q2[:, W:] * per_head(sin)).astype(BF16)
    kv = up(ckv_ref[...], kvn_ref[...], wkv_ref)
    kr = kr_ref[...]
    k_rope = kr[:, :LANES] * cos + kr[:, LANES:] * sin
    k_ref[...] = (kv[:, :W] + per_head(k_rope)).astype(BF16)
    v_ref[...] = kv[:, W:].astype(BF16)


def mla_qkv(cq, ckv, kr2, q_norm, w_uq, kv_norm, w_ukv, cos, sin):
    N = cq.shape[0]
    P = cos.shape[0]
    tm = min(PROJ_TOKENS, P)
    H, dqk = GROUP_HEADS, MLA_NOPE + MLA_ROPE
    blocks = lambda w_, lo, n: jnp.pad(w_.reshape(w_.shape[0], H, -1)[:, :, lo:lo + n],
                                      ((0, 0), (0, 0), (0, LANES - n))).reshape(w_.shape[0], H * LANES)
    wq = w_uq.reshape(w_uq.shape[0], H, dqk)
    wq_partner = jnp.pad(wq[:, :, MLA_NOPE + ROPE_PARTNER], ((0, 0), (0, 0), (MLA_NOPE, LANES - dqk)))
    wq2 = jnp.concatenate([blocks(w_uq, 0, dqk), wq_partner.reshape(-1, H * LANES)], axis=1).astype(BF16)
    wkv2 = jnp.concatenate([blocks(w_ukv, 0, MLA_NOPE), blocks(w_ukv, MLA_NOPE, MLA_V)], axis=1).astype(BF16)
    tok = lambda w_: pl.BlockSpec((tm, w_), lambda i: (i, 0))
    pos = pl.BlockSpec((tm, LANES), lambda i: (i % (P // tm), 0))
    const = lambda a: pl.BlockSpec(a.shape, lambda i: (0, 0))
    args = (cq, ckv, kr2, cos, sin, q_norm[None], kv_norm[None], wq2, wkv2)
    out = jax.ShapeDtypeStruct((N, H * LANES), BF16)
    return pl.pallas_call(
        _mla_qkv_kernel,
        grid=(N // tm,),
        in_specs=[tok(cq.shape[1]), tok(ckv.shape[1]), tok(kr2.shape[1]), pos, pos] + [const(a) for a in args[5:]],
        out_specs=[tok(H * LANES)] * 3,
        out_shape=[out] * 3,
    )(*args)


LOG2_E = 1.4426950408889634
DENSE_Q_TILE = 1024
DENSE_Q_SUB = 256
DENSE_Q_UNROLL = 4
DENSE_K_TILE_MAX = 8320


def _dense_attn_kernel(q_ref, k_ref, v_ref, o_ref, m_ref, l_ref, acc_ref, *, scale, dv):
    h, j = pl.program_id(2), pl.program_id(3)

    @pl.when(j == 0)
    def _():
        m_ref[...] = jnp.full(m_ref.shape, -jnp.inf, F32)
        l_ref[...] = jnp.zeros(l_ref.shape, F32)
        acc_ref[...] = jnp.zeros(acc_ref.shape, F32)

    def rows(i, carry):
        for u in range(DENSE_Q_UNROLL):
            r = pl.ds(pl.multiple_of((i * DENSE_Q_UNROLL + u) * DENSE_Q_SUB, DENSE_Q_SUB), DENSE_Q_SUB)
            s = lax.dot_general(q_ref[0, r, :], k_ref[0], NT_DIMS,
                                preferred_element_type=F32) * (scale * LOG2_E)
            m_prev = m_ref[r, :]
            m_new = jnp.maximum(m_prev, jnp.max(s, axis=-1, keepdims=True))
            alpha = jnp.exp2(m_prev - m_new)
            p = jnp.exp2(s - m_new)
            l_ref[r, :] = alpha * l_ref[r, :] + jnp.sum(p, axis=-1, keepdims=True)
            acc_ref[r, :] = alpha * acc_ref[r, :] + jnp.dot(p.astype(BF16), v_ref[0],
                                                            preferred_element_type=F32)
            m_ref[r, :] = m_new
        return carry

    lax.fori_loop(0, q_ref.shape[1] // (DENSE_Q_SUB * DENSE_Q_UNROLL), rows, 0)

    for hh in range(o_ref.shape[2] // dv):
        @pl.when((j == pl.num_programs(3) - 1) & (h == hh))
        def _():
            o_ref[0, :, hh * dv:(hh + 1) * dv] = (acc_ref[...] / l_ref[...])[:, :dv]


def dense_attention(q, k_all, v_all, scale, dv):
    B, T, C = q.shape
    H, NK = C // LANES, k_all.shape[1]
    tq = min(DENSE_Q_TILE, T)
    tk = max(t for t in range(LANES, DENSE_K_TILE_MAX + 1, LANES) if NK % t == 0)
    return pl.pallas_call(
        functools.partial(_dense_attn_kernel, scale=scale, dv=dv),
        grid=(B, T // tq, H, NK // tk),
        in_specs=[pl.BlockSpec((1, tq, LANES), lambda b, i, h, j: (b, i, h)),
                  pl.BlockSpec((1, tk, LANES), lambda b, i, h, j: (b, j, h)),
                  pl.BlockSpec((1, tk, LANES), lambda b, i, h, j: (b, j, h))],
        out_specs=pl.BlockSpec((1, tq, H * dv), lambda b, i, h, j: (b, i, 0)),
        out_shape=jax.ShapeDtypeStruct((B, T, H * dv), F32),
        scratch_shapes=[pltpu.VMEM((tq, 1), F32), pltpu.VMEM((tq, 1), F32), pltpu.VMEM((tq, LANES), F32)],
        compiler_params=pltpu.CompilerParams(vmem_limit_bytes=VMEM_LIMIT_BYTES),
    )(q, k_all, v_all)


SWA_SPAN = ATTN_BLOCK + 2 * SWA_WINDOW


def _swa_kernel(q_ref, k_ref, v_ref, kc_ref, vc_ref, sink_ref, o_ref):
    n = pl.program_id(1)
    T = k_ref.shape[1]
    start = pl.multiple_of(jnp.clip(n * ATTN_BLOCK - SWA_WINDOW, 0, T - SWA_SPAN), ATTN_BLOCK)
    kw = k_ref[0, pl.ds(start, SWA_SPAN), :]
    vw = v_ref[0, pl.ds(start, SWA_SPAN), :]
    hm = _head_mask(ATTN_BLOCK)
    q = q_ref[0] * (HEAD_DIM ** -0.5)
    q4 = (jnp.concatenate([q] * GROUP_HEADS, axis=0) * hm).astype(BF16)
    rows = GROUP_HEADS * ATTN_BLOCK
    q_pos = n * ATTN_BLOCK + lax.broadcasted_iota(jnp.int32, (rows, SWA_SPAN), 0) % ATTN_BLOCK
    k_pos = start + lax.broadcasted_iota(jnp.int32, (rows, SWA_SPAN), 1)
    s_loc = lax.dot_general(q4, kw, NT_DIMS, preferred_element_type=F32)
    s_loc = jnp.where(jnp.abs(q_pos - k_pos) <= SWA_WINDOW, s_loc, -jnp.inf)
    s_ctx = lax.dot_general(q4, kc_ref[0], NT_DIMS, preferred_element_type=F32)
    sink = sink_ref[...]
    m = jnp.maximum(jnp.maximum(jnp.max(s_loc, axis=-1, keepdims=True),
                                jnp.max(s_ctx, axis=-1, keepdims=True)), sink)
    p_loc = jnp.exp(s_loc - m)
    p_ctx = jnp.exp(s_ctx - m)
    l = jnp.sum(p_loc, axis=-1, keepdims=True) + jnp.sum(p_ctx, axis=-1, keepdims=True) + jnp.exp(sink - m)
    o = (jnp.dot(p_loc.astype(BF16), vw, preferred_element_type=F32)
         + jnp.dot(p_ctx.astype(BF16), vc_ref[0], preferred_element_type=F32)) * (hm / l)
    o_ref[0] = sum(o[h * ATTN_BLOCK:(h + 1) * ATTN_BLOCK] for h in range(GROUP_HEADS))


def window_attention(q, k, v, kc, vc, sink):
    B, T, H, d = q.shape
    G = H // k.shape[2]
    n_ctx = kc.shape[1]
    C = H * d
    rep = lambda a: jnp.repeat(a, G, axis=2).reshape(a.shape[0], a.shape[1], C).astype(BF16)
    sink_rows = jnp.repeat(sink.astype(F32), ATTN_BLOCK).reshape(H * ATTN_BLOCK, 1)
    full = lambda n: pl.BlockSpec((1, n, C), lambda b, i: (b, 0, 0))
    return pl.pallas_call(
        _swa_kernel,
        grid=(B, T // ATTN_BLOCK),
        in_specs=[pl.BlockSpec((1, ATTN_BLOCK, C), lambda b, i: (b, i, 0)),
                  full(T), full(T), full(n_ctx), full(n_ctx),
                  pl.BlockSpec(sink_rows.shape, lambda b, i: (0, 0))],
        out_specs=pl.BlockSpec((1, ATTN_BLOCK, C), lambda b, i: (b, i, 0)),
        out_shape=jax.ShapeDtypeStruct((B, T, C), F32),
        compiler_params=pltpu.CompilerParams(vmem_limit_bytes=VMEM_LIMIT_BYTES),
    )(q.reshape(B, T, C), rep(k), rep(v), rep(kc), rep(vc), sink_rows)


BF16 = jnp.bfloat16
LANES = 128
SUBLANES = 8
ROW_SEGS = D_MODEL // LANES
ROW_WORDS = ROW_SEGS // 2
PEER_PICKS = PEER_HEADS * PEER_TOPK
PEER_TOPK_TOKENS = 1024
PEER_GATHER_TOKENS = 256
PEER_ACT_UNROLL = SUBLANES
VMEM_LIMIT_BYTES = 56 * 1024 * 1024


def _split_bf16(x, parts):
    out = []
    for _ in range(parts):
        p = x.astype(BF16)
        out.append(p)
        x = x - p.astype(F32)
    return out


def _topk_rows(s, k, payload=None):
    n = s.shape[0]
    iota = lax.broadcasted_iota(jnp.int32, s.shape, 0)
    vals, picked = [], []
    for _ in range(k):
        m = jnp.max(s, axis=0, keepdims=True)
        i = jnp.min(jnp.where(s == m, iota, n), axis=0, keepdims=True)
        hit = iota == i
        vals.append(m)
        picked.append(i if payload is None else jnp.max(jnp.where(hit, payload, -1), axis=0, keepdims=True))
        s = jnp.where(hit, -jnp.inf, s)
    return jnp.concatenate(vals, axis=0), jnp.concatenate(picked, axis=0)


def _peer_topk_kernel(x_ref, wq_ref, keys_ref, eidx_ref, gate_ref):
    xb = x_ref[...].astype(BF16)
    q = jnp.dot(xb, wq_ref[...], preferred_element_type=F32)
    nt = (((1,), (1,)), ((), ()))
    sv, si = [], []
    for p in range(2):
        qp = q[:, p * PEER_DKEY:(p + 1) * PEER_DKEY].astype(BF16)
        s = lax.dot_general(keys_ref[0, p], qp, nt, preferred_element_type=F32)
        v_, i_ = _topk_rows(s, PEER_TOPK)
        sv.append(v_)
        si.append(i_)
    cs, ce = [], []
    half = PEER_TOPK // 2
    for a in range(half):
        nb = PEER_TOPK if a == 0 else half
        cs.append(sv[0][a:a + 1] + sv[1][:nb])
        ce.append(si[0][a:a + 1] * PEER_NKEYS + si[1][:nb])
    cs.append(sv[0][half:] + sv[1][0:1])
    ce.append(si[0][half:] * PEER_NKEYS + si[1][0:1])
    cand_s = jnp.concatenate(cs, axis=0)
    cand_e = jnp.concatenate(ce, axis=0)
    fs, eidx = _topk_rows(cand_s, PEER_TOPK, cand_e)
    ex = jnp.exp(fs - fs[0:1])
    eidx_ref[0] = eidx
    gate_ref[0] = ex / jnp.sum(ex, axis=0, keepdims=True)


def peer_topk(h, wq, sub_keys):
    N, D = h.shape
    T = min(PEER_TOPK_TOKENS, N)
    wqb = wq.astype(BF16)
    kb = sub_keys.astype(BF16)
    eidx, gate = pl.pallas_call(
        _peer_topk_kernel,
        grid=(N // T, PEER_HEADS),
        in_specs=[pl.BlockSpec((T, D), lambda i, h_: (i, 0)),
                  pl.BlockSpec((D, 2 * PEER_DKEY), lambda i, h_: (0, h_)),
                  pl.BlockSpec((1, 2, PEER_NKEYS, PEER_DKEY), lambda i, h_: (h_, 0, 0, 0))],
        out_specs=[pl.BlockSpec((1, PEER_TOPK, T), lambda i, h_: (h_, 0, i)),
                   pl.BlockSpec((1, PEER_TOPK, T), lambda i, h_: (h_, 0, i))],
        out_shape=[jax.ShapeDtypeStruct((PEER_HEADS, PEER_TOPK, N), jnp.int32),
                   jax.ShapeDtypeStruct((PEER_HEADS, PEER_TOPK, N), F32)],
        compiler_params=pltpu.CompilerParams(vmem_limit_bytes=VMEM_LIMIT_BYTES),
    )(h, wqb, kb)
    return eidx.reshape(PEER_PICKS, N), gate.reshape(PEER_PICKS, N)


def pack_expert_table(tab):
    E, D = tab.shape
    return pl.pallas_call(
        _pack_table_kernel,
        grid=(E // PACK_ROWS,),
        in_specs=[pl.BlockSpec((PACK_ROWS, D), lambda i: (i, 0))],
        out_specs=pl.BlockSpec((PACK_ROWS * ROW_WORDS, LANES), lambda i: (i, 0)),
        out_shape=jax.ShapeDtypeStruct((E * ROW_WORDS, LANES), jnp.uint32),
    )(tab)


PACK_ROWS = 512


def _pack_table_kernel(t_ref, o_ref):
    bits = lambda a: lax.bitcast_convert_type(a.astype(BF16).astype(F32), jnp.uint32)
    for s_ in range(ROW_WORDS):
        lo = bits(t_ref[:, (2 * s_) * LANES:(2 * s_ + 1) * LANES])
        hi = bits(t_ref[:, (2 * s_ + 1) * LANES:(2 * s_ + 2) * LANES])
        o_ref[pl.ds(s_, PACK_ROWS, stride=ROW_WORDS), :] = (hi & jnp.uint32(0xFFFF0000)) | (lo >> 16)


def _stage_rows(idx_ref, tab_ref, stage_ref, t):
    for k in range(PEER_PICKS):
        off = pl.multiple_of(idx_ref[t, k], ROW_WORDS)
        stage_ref[k * ROW_WORDS:(k + 1) * ROW_WORDS, :] = tab_ref[pl.ds(off, ROW_WORDS), :]
    return pltpu.bitcast(stage_ref[...], BF16)


def _peer_act_kernel(idx_ref, x_ref, gate_ref, tab_ref, seg_mask_ref, group_ref, w_ref,
                     stage_ref, rows_ref):
    T = x_ref.shape[0]
    U = stage_ref.shape[0]

    sub = lax.broadcasted_iota(jnp.int32, (SUBLANES, PEER_PICKS * ROW_SEGS), 0)

    def tokens(g, carry):
        tile = jnp.zeros((SUBLANES, PEER_PICKS * ROW_SEGS), F32)
        for j in range(U):
            t = g * U + j
            sb = _stage_rows(idx_ref, tab_ref, stage_ref.at[j], t)
            xs = jnp.concatenate(_split_bf16(x_ref[t], 2), axis=0)
            r = lax.dot_general(xs, sb, NT_DIMS, preferred_element_type=F32)
            r = jnp.sum(r * seg_mask_ref[...], axis=0, keepdims=True)
            tile = jnp.where(sub == j, r, tile)
        rows_ref[g] = tile
        return carry

    lax.fori_loop(0, T // U, tokens, 0)
    rows = rows_ref[...].reshape(T, PEER_PICKS * ROW_SEGS)
    act = jnp.zeros((T, PEER_PICKS), F32)
    for piece in _split_bf16(rows, 3):
        act = act + jnp.dot(piece, group_ref[...], preferred_element_type=F32)
    w_ref[...] = gate_ref[...] * (0.5 * act * (1.0 + lax.erf(act * (2.0 ** -0.5))))


def _peer_out_kernel(idx_ref, w_ref, x_ref, g_ref, tab_ref, expand_ref, seg_mask_ref, f_ref, stage_ref):
    T = w_ref.shape[0]
    U = stage_ref.shape[0]

    def tokens(g, carry):
        w8 = w_ref[pl.ds(pl.multiple_of(g * U, U), U), :]
        hi, lo = _split_bf16(w8, 2)
        lhs = jnp.concatenate([jnp.broadcast_to(p[j:j + 1], (SUBLANES, PEER_PICKS))
                               for j in range(U) for p in (hi, lo)], axis=0)
        wrep = jnp.dot(lhs, expand_ref[...], preferred_element_type=F32)
        for j in range(U):
            t = g * U + j
            sb = _stage_rows(idx_ref, tab_ref, stage_ref.at[j], t)
            wsel = (wrep[j * 2 * SUBLANES:(j + 1) * 2 * SUBLANES] * seg_mask_ref[...]).astype(BF16)
            o = jnp.dot(wsel, sb, preferred_element_type=F32)
            f_ref[t] = x_ref[t] + g_ref[0] * (o[:SUBLANES] + o[SUBLANES:])
        return carry

    lax.fori_loop(0, T // U, tokens, 0)


def _peer_constants():
    cols = np.arange(PEER_PICKS * ROW_SEGS)
    seg_mask = (cols[None, :] % ROW_SEGS == np.arange(2 * SUBLANES)[:, None] % SUBLANES)
    group = (cols[:, None] // ROW_SEGS == np.arange(PEER_PICKS)[None, :])
    return (jnp.asarray(seg_mask, F32), jnp.asarray(group, BF16), jnp.asarray(group.T, BF16))


def peer_ffn(h, x, gate2, group_tokens, wq, sub_keys, u_packed, v_packed):
    N, D = h.shape
    T = PEER_GATHER_TOKENS
    eidx, gate = peer_topk(h, wq, sub_keys)
    seg_mask, group, expand = _peer_constants()
    rows3 = lambda a: a.reshape(a.shape[0], ROW_SEGS, LANES)
    offs = eidx.T * ROW_WORDS
    idx_spec = pl.BlockSpec((T, PEER_PICKS), lambda i: (i, 0), memory_space=pltpu.SMEM)
    tab_spec = pl.BlockSpec(u_packed.shape, lambda i: (0, 0), pipeline_mode=pl.Buffered(1))
    tok_spec = pl.BlockSpec((T, ROW_SEGS, LANES), lambda i: (i, 0, 0))
    const = lambda shape: pl.BlockSpec(shape, lambda i: (0, 0))
    params = pltpu.CompilerParams(vmem_limit_bytes=VMEM_LIMIT_BYTES)
    w = pl.pallas_call(
        _peer_act_kernel,
        grid=(N // T,),
        in_specs=[idx_spec, tok_spec,
                  pl.BlockSpec((T, PEER_PICKS), lambda i: (i, 0)),
                  tab_spec, const(seg_mask.shape), const(group.shape)],
        out_specs=pl.BlockSpec((T, PEER_PICKS), lambda i: (i, 0)),
        out_shape=jax.ShapeDtypeStruct((N, PEER_PICKS), F32),
        scratch_shapes=[pltpu.VMEM((PEER_ACT_UNROLL, PEER_PICKS * ROW_WORDS, LANES), jnp.uint32),
                        pltpu.VMEM((T // SUBLANES, SUBLANES, PEER_PICKS * ROW_SEGS), F32)],
        compiler_params=params,
    )(offs, rows3(h), gate.T, u_packed, seg_mask, group)
    out = pl.pallas_call(
        _peer_out_kernel,
        grid=(N // T,),
        in_specs=[idx_spec,
                  pl.BlockSpec((T, PEER_PICKS), lambda i: (i, 0)),
                  tok_spec,
                  pl.BlockSpec((1, ROW_SEGS, LANES), lambda i: (i // (group_tokens // T), 0, 0)),
                  tab_spec, const(expand.shape), const(seg_mask.shape)],
        out_specs=tok_spec,
        out_shape=jax.ShapeDtypeStruct((N, ROW_SEGS, LANES), F32),
        scratch_shapes=[pltpu.VMEM((SUBLANES, PEER_PICKS * ROW_WORDS, LANES), jnp.uint32)],
        compiler_params=params,
    )(offs, w, rows3(x), rows3(gate2), v_packed, expand, seg_mask)
    return out.reshape(N, D)


PROJ_TOKENS = 512
MOD_ROWS = SUBLANES
IN_ALIGNED = tuple(i for i, s_ in enumerate(IN_SIZES) if s_ % LANES == 0)
IN_SMALL = tuple(i for i, s_ in enumerate(IN_SIZES) if s_ % LANES)
IN_MXU_ONLY = (1, 2, 4, 12)


def _rms_modulate(x, gain, scale1p, shift):
    r = lax.rsqrt(jnp.mean(x * x, axis=-1, keepdims=True) + EPS)
    return (x * r * gain) * scale1p + shift


def _in_proj_kernel(x_ref, mod_ref, w_ref, *out_refs):
    mod = mod_ref[0]
    h = _rms_modulate(x_ref[...], mod[0:1], mod[1:2], mod[2:3])
    y = jnp.dot(h.astype(BF16), w_ref[...], preferred_element_type=F32)
    off = 0
    for o_ref in out_refs:
        o_ref[...] = y[:, off:off + o_ref.shape[1]].astype(o_ref.dtype)
        off += o_ref.shape[1]


def in_projection(x, mod, w_in, group_tokens):
    N, D = x.shape
    T = min(PROJ_TOKENS, group_tokens)
    starts = np.cumsum((0,) + IN_SIZES)
    group_cols = lambda i: np.arange(starts[i], starts[i + 1])
    small = [i for i in IN_SMALL if i != MLA_KR]
    n_small = sum(IN_SIZES[i] for i in small)
    lane_pad = lambda w_, lo, hi: jnp.pad(w_, ((0, 0), (lo, hi)))
    w_kr = w_in[:, group_cols(MLA_KR)]
    wp = jnp.concatenate(
        [w_in[:, np.concatenate([group_cols(i) for i in IN_ALIGNED])],
         lane_pad(w_in[:, np.concatenate([group_cols(i) for i in small])], 0, -n_small % LANES),
         lane_pad(w_kr, MLA_NOPE, LANES - MLA_NOPE - MLA_ROPE),
         lane_pad(w_kr[:, ROPE_PARTNER], MLA_NOPE, LANES - MLA_NOPE - MLA_ROPE)], axis=1).astype(BF16)
    widths = [IN_SIZES[i] for i in IN_ALIGNED] + [n_small + (-n_small % LANES), 2 * LANES]
    outs = pl.pallas_call(
        _in_proj_kernel,
        grid=(N // T,),
        in_specs=[pl.BlockSpec((T, D), lambda i: (i, 0)),
                  pl.BlockSpec((1, MOD_ROWS, D), lambda i: (i // (group_tokens // T), 0, 0)),
                  pl.BlockSpec(wp.shape, lambda i: (0, 0))],
        out_specs=[pl.BlockSpec((T, w_), lambda i: (i, 0)) for w_ in widths],
        out_shape=[jax.ShapeDtypeStruct((N, w_), BF16 if i in IN_MXU_ONLY else F32)
                   for i, w_ in zip(IN_ALIGNED + (None, None), widths)],
        compiler_params=pltpu.CompilerParams(vmem_limit_bytes=VMEM_LIMIT_BYTES),
    )(x, mod, wp)
    groups = dict(zip(IN_ALIGNED, outs[:-2]))
    off = 0
    for i in small:
        groups[i] = outs[-2][:, off:off + IN_SIZES[i]]
        off += IN_SIZES[i]
    groups[MLA_KR] = outs[-1]
    return [groups[i] for i in range(len(IN_SIZES))]


def _out_proj_kernel(ya_ref, hl_ref, mo_ref, yc_ref, yd_ref, x_ref, mod_ref, w_ref, xo_ref, h2_ref):
    yb = hl_ref[...] * jax.nn.sigmoid(mo_ref[...])
    y = jnp.concatenate([ya_ref[...], yb, yc_ref[...], yd_ref[...]], axis=-1).astype(BF16)
    mod = mod_ref[0]
    xn = x_ref[...] + mod[0:1] * jnp.dot(y, w_ref[...], preferred_element_type=F32)
    xo_ref[...] = xn
    h2_ref[...] = _rms_modulate(xn, mod[1:2], mod[2:3], mod[3:4])


def out_projection(ya, hl, mo, yc, yd, x, mod, w_out, group_tokens):
    N, D = x.shape
    T = min(PROJ_TOKENS, group_tokens)
    part = pl.BlockSpec((T, GROUP_WIDTH), lambda i: (i, 0))
    tok = pl.BlockSpec((T, D), lambda i: (i, 0))
    return pl.pallas_call(
        _out_proj_kernel,
        grid=(N // T,),
        in_specs=[part, part, part, part, part, tok,
                  pl.BlockSpec((1, MOD_ROWS, D), lambda i: (i // (group_tokens // T), 0, 0)),
                  pl.BlockSpec(w_out.shape, lambda i: (0, 0))],
        out_specs=[tok, tok],
        out_shape=[jax.ShapeDtypeStruct((N, D), F32)] * 2,
        compiler_params=pltpu.CompilerParams(vmem_limit_bytes=VMEM_LIMIT_BYTES),
    )(ya, hl, mo, yc, yd, x, mod, w_out.astype(BF16))


def _adaln_kernel(c_ref, w_ref, b_ref, o_ref):
    c = c_ref[...]
    a = c * jax.nn.sigmoid(c)
    o_ref[...] = jnp.dot(a.astype(BF16), w_ref[...].astype(BF16), preferred_element_type=F32) + b_ref[...]


def adaln_linear(c, w_ada, b_ada):
    R, D = c.shape
    rows = -R % SUBLANES + R
    out = pl.pallas_call(
        _adaln_kernel,
        grid=(w_ada.shape[1] // D,),
        in_specs=[pl.BlockSpec((rows, D), lambda j: (0, 0)),
                  pl.BlockSpec((D, D), lambda j: (0, j)),
                  pl.BlockSpec((1, D), lambda j: (0, j))],
        out_specs=pl.BlockSpec((rows, D), lambda j: (0, j)),
        out_shape=jax.ShapeDtypeStruct((rows, w_ada.shape[1]), F32),
    )(jnp.pad(c, ((0, rows - R), (0, 0))), w_ada, b_ada[None])
    return out[:R]


def _mod_rows(*rows):
    m = jnp.stack([jnp.broadcast_to(r, rows[-1].shape) for r in rows], axis=1)
    return jnp.pad(m, ((0, 0), (0, MOD_ROWS - len(rows)), (0, 0)))


def hybrid_layer(x, xc, c, c_ctx, need_ctx, rope_mla, angs_swa,
                 norm1_g, norm2_g, w_ada, b_ada, w_in, na_rpb, ml_conv, ml_gate_b,
                 mla_q_norm, mla_w_uq, mla_kv_norm, mla_w_ukv, swa_sink, w_out,
                 peer_wq, peer_keys, peer_u, peer_v):
    B, T, D = x.shape
    Tc = xc.shape[1]
    H = GROUP_HEADS
    flat = lambda a: a.reshape(-1, a.shape[-1])
    ada = adaln_linear(jnp.concatenate([c, c_ctx[None]], axis=0), w_ada, b_ada)
    sh1, sc1, g1, sh2, sc2, g2 = jnp.split(ada[:B], 6, axis=-1)
    sh1c, sc1c, g1c, sh2c, sc2c, g2c = jnp.split(ada[B:], 6, axis=-1)
    lat = in_projection(flat(x), _mod_rows(norm1_g, 1.0 + sc1, sh1), w_in, T)
    cx = in_projection(flat(xc), _mod_rows(norm1_g, 1.0 + sc1c, sh1c), w_in, B * Tc)
    (na_q, na_k, na_v, ml_qk, ml_v, ml_o, ml_g,
     mla_cq, mla_ckv, mla_kr, sw_q, sw_k, sw_v) = [a.reshape(B, T, -1) for a in lat]
    (na_qc, na_kc, na_vc, ml_qkc, ml_vc, ml_oc, ml_gc,
     mla_cqc, mla_ckvc, mla_krc, sw_qc, sw_kc, sw_vc) = [a.reshape(B, Tc, -1) for a in cx]
    attn_scale = HEAD_DIM ** -0.5
    mla_scale = (MLA_NOPE + MLA_ROPE) ** -0.5
    kc_a, vc_a = heads(na_kc, H), heads(na_vc, H)
    y_a = neighbourhood_attention(na_q, na_k, na_v, na_kc, na_vc, na_rpb)
    h_lat, h_ctx = mlstm_mixer((ml_qk, ml_v, ml_g), (ml_qkc, ml_vc, ml_gc), ml_conv, ml_gate_b)
    no_rope = (jnp.ones((Tc, LANES), F32), jnp.zeros((Tc, LANES), F32))
    q_m, k_m, v_m = [a.reshape(B, T, -1) for a in
                     mla_qkv(flat(mla_cq), flat(mla_ckv), flat(mla_kr), mla_q_norm, mla_w_uq, mla_kv_norm, mla_w_ukv,
                             *rope_mla)]
    qc_m, kc_m, vc_m = [a.reshape(B, Tc, -1) for a in
                        mla_qkv(flat(mla_cqc), flat(mla_ckvc), flat(mla_krc), mla_q_norm, mla_w_uq, mla_kv_norm,
                                mla_w_ukv, *no_rope)]
    y_c = dense_attention(q_m, jnp.concatenate([kc_m, k_m], axis=1), jnp.concatenate([vc_m, v_m], axis=1),
                          mla_scale, MLA_V)
    kc_d, vc_d = heads(sw_kc, SWA_KV_HEADS), heads(sw_vc, SWA_KV_HEADS)
    y_d = window_attention(rope_2d(heads(sw_q, H), angs_swa), rope_2d(heads(sw_k, SWA_KV_HEADS), angs_swa),
                           heads(sw_v, SWA_KV_HEADS), kc_d, vc_d, swa_sink)
    x2, h2 = out_projection(flat(y_a), h_lat.reshape(B * T, GROUP_WIDTH), flat(ml_o), flat(y_c), flat(y_d),
                            flat(x), _mod_rows(g1, norm2_g, 1.0 + sc2, sh2), w_out, T)
    u_packed, v_packed = pack_expert_table(peer_u), pack_expert_table(peer_v)
    x = peer_ffn(h2, x2, g2, T, peer_wq, peer_keys, u_packed, v_packed).reshape(B, T, D)
    if not need_ctx:
        return x, None
    xc2, h2c = out_projection(flat(ctx_attn(heads(na_qc, H), kc_a, vc_a, attn_scale)),
                              h_ctx.reshape(B * Tc, GROUP_WIDTH), flat(ml_oc),
                              ctx_attn(heads(qc_m, H), heads(kc_m, H), heads(vc_m, H), mla_scale)
                              .reshape(B, Tc, H, LANES)[..., :MLA_V].reshape(B * Tc, GROUP_WIDTH),
                              flat(ctx_attn(heads(sw_qc, H), kc_d, vc_d, attn_scale, swa_sink)),
                              flat(xc), _mod_rows(g1c, norm2_g, 1.0 + sc2c, sh2c), w_out, B * Tc)
    xc = peer_ffn(h2c, xc2, g2c, B * Tc, peer_wq, peer_keys, u_packed, v_packed).reshape(B, Tc, D)
    return x, xc


def _final_rmsnorm_kernel(x_ref, g_ref, o_ref):
    x = x_ref[...]
    o_ref[...] = x * lax.rsqrt(jnp.mean(x * x, axis=-1, keepdims=True) + EPS) * g_ref[...]


def final_rmsnorm(x, g):
    B, T, D = x.shape
    rows = 1024
    xf = x.reshape(B * T, D)
    out = pl.pallas_call(
        _final_rmsnorm_kernel,
        grid=(B * T // rows,),
        in_specs=[pl.BlockSpec((rows, D), lambda i: (i, 0)), pl.BlockSpec((1, D), lambda i: (0, 0))],
        out_specs=pl.BlockSpec((rows, D), lambda i: (i, 0)),
        out_shape=jax.ShapeDtypeStruct((B * T, D), x.dtype),
    )(xf, g.reshape(1, D))
    return out.reshape(B, T, D)


def kernel(x, c, ctx, c_ctx, norm1_g, norm2_g, w_ada, b_ada, w_in, na_rpb, ml_conv, ml_gate_b,
           mla_q_norm, mla_w_uq, mla_kv_norm, mla_w_ukv, swa_sink, w_out,
           peer_wq, peer_keys, peer_u, peer_v, final_norm_g):
    T = x.shape[1]
    rope_mla = mla_rope_tables(T)
    angs_swa = axial_angles(T, HEAD_DIM)
    xc = ctx
    for l in range(DEPTH):
        x, xc = hybrid_layer(x, xc, c, c_ctx, l < DEPTH - 1, rope_mla, angs_swa,
                             norm1_g[l], norm2_g[l], w_ada[l], b_ada[l], w_in[l], na_rpb[l],
                             ml_conv[l], ml_gate_b[l], mla_q_norm[l], mla_w_uq[l], mla_kv_norm[l],
                             mla_w_ukv[l], swa_sink[l], w_out[l], peer_wq[l], peer_keys[l],
                             peer_u[l], peer_v[l])
    return final_rmsnorm(x, final_norm_g)
```

```python
import functools

import jax
import jax.numpy as jnp
from jax import lax
import numpy as np
from jax.experimental import pallas as pl
from jax.experimental.pallas import tpu as pltpu

D_MODEL = 1024
DEPTH = 2

CTX_LEN = 256
GRID_W = 64
N_MIXERS = 4
MIX_WIDTH = D_MODEL
GROUP_WIDTH = MIX_WIDTH // N_MIXERS
GROUP_HEADS = 4
HEAD_DIM = GROUP_WIDTH // GROUP_HEADS
NA_ROWS = 8
NA_COLS = 16
ML_CHUNK = 64
MLA_Q_RANK = 256
MLA_KV_RANK = 128
MLA_NOPE = 64
MLA_ROPE = 32
MLA_V = 64
SWA_KV_HEADS = 2
SWA_WINDOW = 128
ATTN_BLOCK = 128
PEER_HEADS = 8
PEER_NKEYS = 128
PEER_DKEY = 128
PEER_TOPK = 16
ROPE_BASE = 10000.0
EPS = 1e-6
IN_SIZES = (GROUP_WIDTH, GROUP_WIDTH, GROUP_WIDTH,
            2 * GROUP_WIDTH, GROUP_WIDTH, GROUP_WIDTH, 4 * GROUP_HEADS,
            MLA_Q_RANK, MLA_KV_RANK, MLA_ROPE,
            GROUP_WIDTH, SWA_KV_HEADS * HEAD_DIM, SWA_KV_HEADS * HEAD_DIM)
F32 = jnp.float32


def heads(a, h):
    return a.reshape(a.shape[:-1] + (h, a.shape[-1] // h))


def axial_angles(T, rot_dim):
    t = jnp.arange(T)
    row = (t // GRID_W).astype(F32)
    col = (t % GRID_W).astype(F32)
    half = rot_dim // 2
    inv = 1.0 / (ROPE_BASE ** (jnp.arange(0, half, 2, dtype=F32) / half))
    return row[:, None] * inv, col[:, None] * inv


def rope_1d(x, ang):
    cos = jnp.cos(ang)[None, :, None, :]
    sin = jnp.sin(ang)[None, :, None, :]
    x1, x2 = jnp.split(x.astype(F32), 2, axis=-1)
    return jnp.concatenate([x1 * cos - x2 * sin, x1 * sin + x2 * cos], axis=-1)


def rope_2d(x, angs):
    xr, xc = jnp.split(x, 2, axis=-1)
    return jnp.concatenate([rope_1d(xr, angs[0]), rope_1d(xc, angs[1])], axis=-1).astype(x.dtype)


NT_DIMS = (((1,), (1,)), ((), ()))


def _ctx_attn_kernel(q_ref, k_ref, v_ref, sink_ref, o_ref, *, scale, use_sink):
    s = lax.dot_general(q_ref[0, 0].astype(BF16), k_ref[0, 0].astype(BF16), NT_DIMS,
                        preferred_element_type=F32) * scale
    m = jnp.max(s, axis=-1, keepdims=True)
    if use_sink:
        sink = sink_ref[pl.program_id(1)]
        m = jnp.maximum(m, sink)
    p = jnp.exp(s - m)
    l = jnp.sum(p, axis=-1, keepdims=True)
    if use_sink:
        l = l + jnp.exp(sink - m)
    o_ref[0, 0] = jnp.dot(p.astype(BF16), v_ref[0, 0].astype(BF16), preferred_element_type=F32) / l


def ctx_attn(q, k, v, scale, sink=None):
    B, Tc, H, _ = q.shape
    rep = H // k.shape[2]
    hm = lambda a: jnp.swapaxes(a, 1, 2)
    q, k, v = hm(q), hm(jnp.repeat(k, rep, axis=2)), hm(jnp.repeat(v, rep, axis=2))
    blk = lambda a: pl.BlockSpec((1, 1, Tc, a.shape[-1]), lambda b, h: (b, h, 0, 0))
    out = pl.pallas_call(
        functools.partial(_ctx_attn_kernel, scale=scale, use_sink=sink is not None),
        grid=(B, H),
        in_specs=[blk(q), blk(k), blk(v), pl.BlockSpec(memory_space=pltpu.SMEM)],
        out_specs=blk(v),
        out_shape=jax.ShapeDtypeStruct(v.shape, F32),
    )(q, k, v, jnp.zeros((H,), F32) if sink is None else sink.astype(F32))
    return jnp.swapaxes(out, 1, 2).reshape(B, Tc, -1)
NA_SPAN = NA_ROWS * GRID_W


def _head_mask(width):
    rows = lax.broadcasted_iota(jnp.int32, (GROUP_HEADS * width, GROUP_WIDTH), 0) // width
    cols = lax.broadcasted_iota(jnp.int32, (GROUP_HEADS * width, GROUP_WIDTH), 1) // HEAD_DIM
    return (rows == cols).astype(F32)


def _na_kernel(q_ref, k_ref, v_ref, kc_ref, vc_ref, bias_ref, o_ref):
    r = pl.program_id(1)
    rows = pl.num_programs(1)
    rs = jnp.clip(r - NA_ROWS // 2, 0, rows - NA_ROWS)
    start = pl.multiple_of(rs * GRID_W, GRID_W)
    kw = k_ref[0, pl.ds(start, NA_SPAN), :]
    vw = v_ref[0, pl.ds(start, NA_SPAN), :]
    hm = _head_mask(GRID_W)
    q = q_ref[0] * (HEAD_DIM ** -0.5)
    q4 = (jnp.concatenate([q] * GROUP_HEADS, axis=0) * hm).astype(BF16)
    s_loc = lax.dot_general(q4, kw, NT_DIMS, preferred_element_type=F32) + bias_ref[rs - r + NA_ROWS - 1]
    s_ctx = lax.dot_general(q4, kc_ref[0], NT_DIMS, preferred_element_type=F32)
    m = jnp.maximum(jnp.max(s_loc, axis=-1, keepdims=True), jnp.max(s_ctx, axis=-1, keepdims=True))
    p_loc = jnp.exp(s_loc - m)
    p_ctx = jnp.exp(s_ctx - m)
    l = jnp.sum(p_loc, axis=-1, keepdims=True) + jnp.sum(p_ctx, axis=-1, keepdims=True)
    o = (jnp.dot(p_loc.astype(BF16), vw, preferred_element_type=F32)
         + jnp.dot(p_ctx.astype(BF16), vc_ref[0], preferred_element_type=F32)) * (hm / l)
    o_ref[0] = sum(o[h * GRID_W:(h + 1) * GRID_W] for h in range(GROUP_HEADS))


def _na_bias_table(rpb):
    c = np.arange(GRID_W)
    col_start = np.clip(c - NA_COLS // 2, 0, GRID_W - NA_COLS)
    valid = (c[None, :] >= col_start[:, None]) & (c[None, :] < col_start[:, None] + NA_COLS)
    dc = np.clip(c[None, :] - c[:, None] + NA_COLS - 1, 0, 2 * NA_COLS - 2)
    dr = np.arange(NA_ROWS)[:, None] + np.arange(NA_ROWS)[None, :]
    t = rpb.astype(F32)[:, dr][..., dc]
    t = jnp.where(valid[None, None, None], t, -jnp.inf)
    return jnp.transpose(t, (1, 0, 3, 2, 4)).reshape(NA_ROWS, GROUP_HEADS * GRID_W, NA_SPAN)


def neighbourhood_attention(q, k, v, kc, vc, rpb):
    B, T, C = q.shape
    rows = T // GRID_W
    n_ctx = kc.shape[1]
    bias = _na_bias_table(rpb)
    full = lambda n: pl.BlockSpec((1, n, C), lambda b, r: (b, 0, 0))
    return pl.pallas_call(
        _na_kernel,
        grid=(B, rows),
        in_specs=[pl.BlockSpec((1, GRID_W, C), lambda b, r: (b, r, 0)),
                  full(T), full(T), full(n_ctx), full(n_ctx),
                  pl.BlockSpec(bias.shape, lambda b, r: (0, 0, 0))],
        out_specs=pl.BlockSpec((1, GRID_W, C), lambda b, r: (b, r, 0)),
        out_shape=jax.ShapeDtypeStruct((B, T, C), F32),
        compiler_params=pltpu.CompilerParams(vmem_limit_bytes=VMEM_LIMIT_BYTES),
    )(q, k.astype(BF16), v.astype(BF16), kc.astype(BF16), vc.astype(BF16), bias)


def short_conv(a, w):
    T = a.shape[1]
    pad = w.shape[0] // 2
    ap = jnp.pad(a, ((0, 0), (pad, pad), (0, 0)))
    out = ap[:, :T] * w[0]
    for j in range(1, w.shape[0]):
        out = out + ap[:, j:j + T] * w[j]
    return out


ML_CHUNKS_PER_STEP = CTX_LEN // ML_CHUNK


def _bmm(a, b, contract):
    return lax.dot_general(a.astype(BF16), b.astype(BF16), (contract, ((0,), (0,))),
                           preferred_element_type=F32)


def _mlstm_chunk(qt, kt, vt, irow, brow, state, backward):
    L = ML_CHUNK
    C, nrow, m = state
    row = lax.broadcasted_iota(jnp.int32, (1, L, L), 1)
    col = lax.broadcasted_iota(jnp.int32, (1, L, L), 2)
    seen = (row <= col) if backward else (row >= col)
    eye = row == col

    def as_col(r):
        return jnp.sum(jnp.where(eye, r, 0.0), axis=2, keepdims=True)

    blast = brow[:, :, 0:1] if backward else brow[:, :, L - 1:L]
    rrow = brow - irow
    bcol = as_col(brow)
    d_log = jnp.where(seen, bcol - rrow, -jnp.inf)
    inter = bcol + m
    m_t = jnp.maximum(inter, jnp.max(d_log, axis=2, keepdims=True))
    w = jnp.exp(d_log - m_t)
    a = jnp.exp(inter - m_t)
    s = _bmm(qt, kt, ((2,), (2,))) * w
    num = _bmm(s, vt, ((2,), (1,))) + a * _bmm(qt, C, ((2,), (1,)))
    den = jnp.sum(s, axis=2, keepdims=True) + a * jnp.sum(qt * nrow, axis=2, keepdims=True)
    h = num / jnp.maximum(jnp.abs(den), jnp.exp(-m_t))
    g = blast - rrow
    m_new = jnp.maximum(blast + m, jnp.max(g, axis=2, keepdims=True))
    kw = kt * as_col(jnp.exp(g - m_new))
    decay = jnp.exp(blast + m - m_new)
    C = decay * C + _bmm(jnp.swapaxes(kw, 1, 2), vt, ((2,), (1,)))
    nrow = decay * nrow + jnp.sum(kw, axis=1, keepdims=True)
    return h, (C, nrow, m_new)


def _mlstm_kernel(qf_ref, kf_ref, vf_ref, if_ref, bf_ref, qb_ref, kb_ref, vb_ref, ib_ref, bb_ref,
                  hf_ref, hb_ref, c_ref, n_ref, m_ref):
    N, L = qf_ref.shape[0], ML_CHUNK

    @pl.when(pl.program_id(0) == 0)
    def _():
        c_ref[...] = jnp.zeros(c_ref.shape, F32)
        n_ref[...] = jnp.zeros(n_ref.shape, F32)
        m_ref[...] = jnp.zeros(m_ref.shape, F32)

    fwd = (c_ref[:N], n_ref[:N], m_ref[:N])
    bwd = (c_ref[N:], n_ref[N:], m_ref[N:])
    for c in range(ML_CHUNKS_PER_STEP):
        rows = slice(c * L, (c + 1) * L)
        h, fwd = _mlstm_chunk(qf_ref[:, rows, :], kf_ref[:, rows, :], vf_ref[:, rows, :],
                              if_ref[:, 0, c:c + 1, :], bf_ref[:, 0, c:c + 1, :], fwd, False)
        hf_ref[:, rows, :] = h
        cb = ML_CHUNKS_PER_STEP - 1 - c
        rows = slice(cb * L, (cb + 1) * L)
        h, bwd = _mlstm_chunk(qb_ref[:, rows, :], kb_ref[:, rows, :], vb_ref[:, rows, :],
                              ib_ref[:, 0, cb:cb + 1, :], bb_ref[:, 0, cb:cb + 1, :], bwd, True)
        hb_ref[:, rows, :] = h
    for i, ref in enumerate((c_ref, n_ref, m_ref)):
        ref[:N] = fwd[i]
        ref[N:] = bwd[i]


def mlstm_scan(q, k, v, gates_f, gates_b, n_ctx):
    B, T, H, d = q.shape
    CB, L = ML_CHUNKS_PER_STEP, ML_CHUNK
    assert n_ctx == CB * L and T % (CB * L) == 0
    N, steps = B * H, T // (CB * L)
    hm = lambda a: jnp.swapaxes(a, 1, 2).reshape(N, T, d)
    gates = lambda a: jnp.swapaxes(a, 1, 2).reshape(N, steps, CB, L)
    chunked = lambda a: a.reshape(B, T // L, L, H)
    b_f = jnp.cumsum(chunked(gates_f[1]), axis=2).reshape(B, T, H)
    b_b = lax.cumsum(chunked(gates_b[1]), axis=2, reverse=True).reshape(B, T, H)
    back = lambda j: jnp.where(j == 0, 0, steps - j)
    seq_f = pl.BlockSpec((N, CB * L, d), lambda j: (0, j, 0))
    seq_b = pl.BlockSpec((N, CB * L, d), lambda j: (0, back(j), 0))
    gate_f = pl.BlockSpec((N, 1, CB, L), lambda j: (0, j, 0, 0))
    gate_b = pl.BlockSpec((N, 1, CB, L), lambda j: (0, back(j), 0, 0))
    qh, kh, vh = hm(q), hm(k), hm(v)
    hf, hb = pl.pallas_call(
        _mlstm_kernel,
        grid=(steps,),
        in_specs=[seq_f, seq_f, seq_f, gate_f, gate_f, seq_b, seq_b, seq_b, gate_b, gate_b],
        out_specs=[seq_f, seq_b],
        out_shape=[jax.ShapeDtypeStruct((N, T, d), F32)] * 2,
        scratch_shapes=[pltpu.VMEM((2 * N, d, d), F32), pltpu.VMEM((2 * N, 1, d), F32),
                        pltpu.VMEM((2 * N, 1, 1), F32)],
        compiler_params=pltpu.CompilerParams(vmem_limit_bytes=VMEM_LIMIT_BYTES),
    )(qh, kh, vh, gates(gates_f[0]), gates(b_f), qh, kh, vh, gates(gates_b[0]), gates(b_b))
    return jnp.swapaxes((hf + hb).reshape(B, H, T, d), 1, 2)


def mlstm_prep(qk, v, gates, conv_w, gate_b):
    qk = jax.nn.silu(short_conv(qk, conv_w))
    q, k = jnp.split(qk, 2, axis=-1)
    g = (gates + gate_b).astype(F32)
    i_f, f_f, i_b, f_b = jnp.split(g, 4, axis=-1)
    return (heads(q, GROUP_HEADS) * HEAD_DIM ** -0.5, heads(k, GROUP_HEADS), heads(v, GROUP_HEADS),
            (i_f, jax.nn.log_sigmoid(f_f), i_b, jax.nn.log_sigmoid(f_b)))


def mlstm_mixer(lat, ctx, conv_w, gate_b):
    ql, kl, vl, gl = mlstm_prep(lat[0], lat[1], lat[2], conv_w, gate_b)
    qc, kc, vc, gc = mlstm_prep(ctx[0], ctx[1], ctx[2], conv_w, gate_b)
    Tc = qc.shape[1]
    cat = lambda c_, l_: jnp.concatenate([c_, l_], axis=1)
    h = mlstm_scan(cat(qc, ql), cat(kc, kl), cat(vc, vl),
                   (cat(gc[0], gl[0]), cat(gc[1], gl[1])), (cat(gc[2], gl[2]), cat(gc[3], gl[3])), Tc)
    return h[:, Tc:], h[:, :Tc]


MLA_KR = 9
ROPE_PARTNER = np.concatenate([np.arange(q_, q_ + MLA_ROPE // 4) for q_ in
                               (MLA_ROPE // 4, 0, 3 * MLA_ROPE // 4, MLA_ROPE // 2)])


def mla_rope_tables(T):
    ang_r, ang_c = axial_angles(T, MLA_ROPE)
    cos = jnp.concatenate([jnp.cos(ang_r)] * 2 + [jnp.cos(ang_c)] * 2, axis=1)
    sin = jnp.concatenate([-jnp.sin(ang_r), jnp.sin(ang_r), -jnp.sin(ang_c), jnp.sin(ang_c)], axis=1)
    pad = LANES - MLA_NOPE - MLA_ROPE
    return (jnp.concatenate([jnp.ones((T, MLA_NOPE), F32), cos, jnp.zeros((T, pad), F32)], axis=1),
            jnp.concatenate([jnp.zeros((T, MLA_NOPE), F32), sin, jnp.zeros((T, pad), F32)], axis=1))


def _mla_qkv_kernel(cq_ref, ckv_ref, kr_ref, cos_ref, sin_ref, qn_ref, kvn_ref, wq_ref, wkv_ref,
                    q_ref, k_ref, v_ref):
    def up(x, g, w_ref):
        y = x * lax.rsqrt(jnp.mean(x * x, axis=-1, keepdims=True) + EPS) * g
        return jnp.dot(y.astype(BF16), w_ref[...], preferred_element_type=F32)

    W = GROUP_HEADS * LANES
    per_head = lambda a: jnp.concatenate([a] * GROUP_HEADS, axis=1)
    cos, sin = cos_ref[...], sin_ref[...]
    q2 = up(cq_ref[...], qn_ref[...], wq_ref)
    q_ref[...] = (q2[:, :W] * per_head(cos) + q2[:, W:] * per_head(sin)).astype(BF16)
    kv = up(ckv_ref[...], kvn_ref[...], wkv_ref)
    kr = kr_ref[...]
    k_rope = kr[:, :LANES] * cos + kr[:, LANES:] * sin
    k_ref[...] = (kv[:, :W] + per_head(k_rope)).astype(BF16)
    v_ref[...] = kv[:, W:].astype(BF16)


def mla_qkv(cq, ckv, kr2, q_norm, w_uq, kv_norm, w_ukv, cos, sin):
    N = cq.shape[0]
    P = cos.shape[0]
    tm = min(PROJ_TOKENS, P)
    H, dqk = GROUP_HEADS, MLA_NOPE + MLA_ROPE
    blocks = lambda w_, lo, n: jnp.pad(w_.reshape(w_.shape[0], H, -1)[:, :, lo:lo + n],
                                      ((0, 0), (0, 0), (0, LANES - n))).reshape(w_.shape[0], H * LANES)
    wq = w_uq.reshape(w_uq.shape[0], H, dqk)
    wq_partner = jnp.pad(wq[:, :, MLA_NOPE + ROPE_PARTNER], ((0, 0), (0, 0), (MLA_NOPE, LANES - dqk)))
    wq2 = jnp.concatenate([blocks(w_uq, 0, dqk), wq_partner.reshape(-1, H * LANES)], axis=1).astype(BF16)
    wkv2 = jnp.concatenate([blocks(w_ukv, 0, MLA_NOPE), blocks(w_ukv, MLA_NOPE, MLA_V)], axis=1).astype(BF16)
    tok = lambda w_: pl.BlockSpec((tm, w_), lambda i: (i, 0))
    pos = pl.BlockSpec((tm, LANES), lambda i: (i % (P // tm), 0))
    const = lambda a: pl.BlockSpec(a.shape, lambda i: (0, 0))
    args = (cq, ckv, kr2, cos, sin, q_norm[None], kv_norm[None], wq2, wkv2)
    out = jax.ShapeDtypeStruct((N, H * LANES), BF16)
    return pl.pallas_call(
        _mla_qkv_kernel,
        grid=(N // tm,),
        in_specs=[tok(cq.shape[1]), tok(ckv.shape[1]), tok(kr2.shape[1]), pos, pos] + [const(a) for a in args[5:]],
        out_specs=[tok(H * LANES)] * 3,
        out_shape=[out] * 3,
    )(*args)


LOG2_E = 1.4426950408889634
DENSE_Q_TILE = 1024
DENSE_Q_SUB = 256
DENSE_Q_UNROLL = 4
DENSE_K_TILE_MAX = 16640


def _dense_attn_kernel(q_ref, k_ref, v_ref, o_ref, m_ref, l_ref, acc_ref, *, scale, dv):
    h, j = pl.program_id(2), pl.program_id(3)

    @pl.when(j == 0)
    def _():
        m_ref[...] = jnp.full(m_ref.shape, -jnp.inf, F32)
        l_ref[...] = jnp.zeros(l_ref.shape, F32)
        acc_ref[...] = jnp.zeros(acc_ref.shape, F32)

    def rows(i, carry):
        for u in range(DENSE_Q_UNROLL):
            r = pl.ds(pl.multiple_of((i * DENSE_Q_UNROLL + u) * DENSE_Q_SUB, DENSE_Q_SUB), DENSE_Q_SUB)
            s = lax.dot_general(q_ref[0, r, :], k_ref[0], NT_DIMS,
                                preferred_element_type=F32) * (scale * LOG2_E)
            m_prev = m_ref[r, :]
            m_new = jnp.maximum(m_prev, jnp.max(s, axis=-1, keepdims=True))
            alpha = jnp.exp2(m_prev - m_new)
            p = jnp.exp2(s - m_new)
            l_ref[r, :] = alpha * l_ref[r, :] + jnp.sum(p, axis=-1, keepdims=True)
            acc_ref[r, :] = alpha * acc_ref[r, :] + jnp.dot(p.astype(BF16), v_ref[0],
                                                            preferred_element_type=F32)
            m_ref[r, :] = m_new
        return carry

    lax.fori_loop(0, q_ref.shape[1] // (DENSE_Q_SUB * DENSE_Q_UNROLL), rows, 0)

    for hh in range(o_ref.shape[2] // dv):
        @pl.when((j == pl.num_programs(3) - 1) & (h == hh))
        def _():
            o_ref[0, :, hh * dv:(hh + 1) * dv] = (acc_ref[...] / l_ref[...])[:, :dv]


def dense_attention(q, k_all, v_all, scale, dv):
    B, T, C = q.shape
    H, NK = C // LANES, k_all.shape[1]
    tq = min(DENSE_Q_TILE, T)
    tk = max(t for t in range(LANES, DENSE_K_TILE_MAX + 1, LANES) if NK % t == 0)
    return pl.pallas_call(
        functools.partial(_dense_attn_kernel, scale=scale, dv=dv),
        grid=(B, T // tq, H, NK // tk),
        in_specs=[pl.BlockSpec((1, tq, LANES), lambda b, i, h, j: (b, i, h)),
                  pl.BlockSpec((1, tk, LANES), lambda b, i, h, j: (b, j, h)),
                  pl.BlockSpec((1, tk, LANES), lambda b, i, h, j: (b, j, h))],
        out_specs=pl.BlockSpec((1, tq, H * dv), lambda b, i, h, j: (b, i, 0)),
        out_shape=jax.ShapeDtypeStruct((B, T, H * dv), F32),
        scratch_shapes=[pltpu.VMEM((tq, 1), F32), pltpu.VMEM((tq, 1), F32), pltpu.VMEM((tq, LANES), F32)],
        compiler_params=pltpu.CompilerParams(vmem_limit_bytes=VMEM_LIMIT_BYTES),
    )(q, k_all, v_all)


SWA_SPAN = ATTN_BLOCK + 2 * SWA_WINDOW


def _swa_kernel(q_ref, k_ref, v_ref, kc_ref, vc_ref, sink_ref, o_ref):
    n = pl.program_id(1)
    T = k_ref.shape[1]
    start = pl.multiple_of(jnp.clip(n * ATTN_BLOCK - SWA_WINDOW, 0, T - SWA_SPAN), ATTN_BLOCK)
    kw = k_ref[0, pl.ds(start, SWA_SPAN), :]
    vw = v_ref[0, pl.ds(start, SWA_SPAN), :]
    hm = _head_mask(ATTN_BLOCK)
    q = q_ref[0] * (HEAD_DIM ** -0.5)
    q4 = (jnp.concatenate([q] * GROUP_HEADS, axis=0) * hm).astype(BF16)
    rows = GROUP_HEADS * ATTN_BLOCK
    q_pos = n * ATTN_BLOCK + lax.broadcasted_iota(jnp.int32, (rows, SWA_SPAN), 0) % ATTN_BLOCK
    k_pos = start + lax.broadcasted_iota(jnp.int32, (rows, SWA_SPAN), 1)
    s_loc = lax.dot_general(q4, kw, NT_DIMS, preferred_element_type=F32)
    s_loc = jnp.where(jnp.abs(q_pos - k_pos) <= SWA_WINDOW, s_loc, -jnp.inf)
    s_ctx = lax.dot_general(q4, kc_ref[0], NT_DIMS, preferred_element_type=F32)
    sink = sink_ref[...]
    m = jnp.maximum(jnp.maximum(jnp.max(s_loc, axis=-1, keepdims=True),
                                jnp.max(s_ctx, axis=-1, keepdims=True)), sink)
    p_loc = jnp.exp(s_loc - m)
    p_ctx = jnp.exp(s_ctx - m)
    l = jnp.sum(p_loc, axis=-1, keepdims=True) + jnp.sum(p_ctx, axis=-1, keepdims=True) + jnp.exp(sink - m)
    o = (jnp.dot(p_loc.astype(BF16), vw, preferred_element_type=F32)
         + jnp.dot(p_ctx.astype(BF16), vc_ref[0], preferred_element_type=F32)) * (hm / l)
    o_ref[0] = sum(o[h * ATTN_BLOCK:(h + 1) * ATTN_BLOCK] for h in range(GROUP_HEADS))


def window_attention(q, k, v, kc, vc, sink):
    B, T, H, d = q.shape
    G = H // k.shape[2]
    n_ctx = kc.shape[1]
    C = H * d
    rep = lambda a: jnp.repeat(a, G, axis=2).reshape(a.shape[0], a.shape[1], C).astype(BF16)
    sink_rows = jnp.repeat(sink.astype(F32), ATTN_BLOCK).reshape(H * ATTN_BLOCK, 1)
    full = lambda n: pl.BlockSpec((1, n, C), lambda b, i: (b, 0, 0))
    return pl.pallas_call(
        _swa_kernel,
        grid=(B, T // ATTN_BLOCK),
        in_specs=[pl.BlockSpec((1, ATTN_BLOCK, C), lambda b, i: (b, i, 0)),
                  full(T), full(T), full(n_ctx), full(n_ctx),
                  pl.BlockSpec(sink_rows.shape, lambda b, i: (0, 0))],
        out_specs=pl.BlockSpec((1, ATTN_BLOCK, C), lambda b, i: (b, i, 0)),
        out_shape=jax.ShapeDtypeStruct((B, T, C), F32),
        compiler_params=pltpu.CompilerParams(vmem_limit_bytes=VMEM_LIMIT_BYTES),
    )(q.reshape(B, T, C), rep(k), rep(v), rep(kc), rep(vc), sink_rows)


BF16 = jnp.bfloat16
LANES = 128
SUBLANES = 8
ROW_SEGS = D_MODEL // LANES
ROW_WORDS = ROW_SEGS // 2
PEER_PICKS = PEER_HEADS * PEER_TOPK
PEER_TOPK_TOKENS = 1024
PEER_GATHER_TOKENS = 256
PEER_ACT_UNROLL = SUBLANES
VMEM_LIMIT_BYTES = 56 * 1024 * 1024


def _split_bf16(x, parts):
    out = []
    for _ in range(parts):
        p = x.astype(BF16)
        out.append(p)
        x = x - p.astype(F32)
    return out


def _topk_rows(s, k, payload=None):
    n = s.shape[0]
    iota = lax.broadcasted_iota(jnp.int32, s.shape, 0)
    vals, picked = [], []
    for _ in range(k):
        m = jnp.max(s, axis=0, keepdims=True)
        i = jnp.min(jnp.where(s == m, iota, n), axis=0, keepdims=True)
        hit = iota == i
        vals.append(m)
        picked.append(i if payload is None else jnp.max(jnp.where(hit, payload, -1), axis=0, keepdims=True))
        s = jnp.where(hit, -jnp.inf, s)
    return jnp.concatenate(vals, axis=0), jnp.concatenate(picked, axis=0)


def _peer_topk_kernel(x_ref, wq_ref, keys_ref, eidx_ref, gate_ref):
    xb = x_ref[...].astype(BF16)
    q = jnp.dot(xb, wq_ref[...], preferred_element_type=F32)
    nt = (((1,), (1,)), ((), ()))
    sv, si = [], []
    for p in range(2):
        qp = q[:, p * PEER_DKEY:(p + 1) * PEER_DKEY].astype(BF16)
        s = lax.dot_general(keys_ref[0, p], qp, nt, preferred_element_type=F32)
        v_, i_ = _topk_rows(s, PEER_TOPK)
        sv.append(v_)
        si.append(i_)
    cs, ce = [], []
    half = PEER_TOPK // 2
    for a in range(half):
        nb = PEER_TOPK if a == 0 else half
        cs.append(sv[0][a:a + 1] + sv[1][:nb])
        ce.append(si[0][a:a + 1] * PEER_NKEYS + si[1][:nb])
    cs.append(sv[0][half:] + sv[1][0:1])
    ce.append(si[0][half:] * PEER_NKEYS + si[1][0:1])
    cand_s = jnp.concatenate(cs, axis=0)
    cand_e = jnp.concatenate(ce, axis=0)
    fs, eidx = _topk_rows(cand_s, PEER_TOPK, cand_e)
    ex = jnp.exp(fs - fs[0:1])
    eidx_ref[0] = eidx
    gate_ref[0] = ex / jnp.sum(ex, axis=0, keepdims=True)


def peer_topk(h, wq, sub_keys):
    N, D = h.shape
    T = min(PEER_TOPK_TOKENS, N)
    wqb = wq.astype(BF16)
    kb = sub_keys.astype(BF16)
    eidx, gate = pl.pallas_call(
        _peer_topk_kernel,
        grid=(N // T, PEER_HEADS),
        in_specs=[pl.BlockSpec((T, D), lambda i, h_: (i, 0)),
                  pl.BlockSpec((D, 2 * PEER_DKEY), lambda i, h_: (0, h_)),
                  pl.BlockSpec((1, 2, PEER_NKEYS, PEER_DKEY), lambda i, h_: (h_, 0, 0, 0))],
        out_specs=[pl.BlockSpec((1, PEER_TOPK, T), lambda i, h_: (h_, 0, i)),
                   pl.BlockSpec((1, PEER_TOPK, T), lambda i, h_: (h_, 0, i))],
        out_shape=[jax.ShapeDtypeStruct((PEER_HEADS, PEER_TOPK, N), jnp.int32),
                   jax.ShapeDtypeStruct((PEER_HEADS, PEER_TOPK, N), F32)],
        compiler_params=pltpu.CompilerParams(vmem_limit_bytes=VMEM_LIMIT_BYTES),
    )(h, wqb, kb)
    return eidx.reshape(PEER_PICKS, N), gate.reshape(PEER_PICKS, N)


def pack_expert_table(tab):
    E, D = tab.shape
    return pl.pallas_call(
        _pack_table_kernel,
        grid=(E // PACK_ROWS,),
        in_specs=[pl.BlockSpec((PACK_ROWS, D), lambda i: (i, 0))],
        out_specs=pl.BlockSpec((PACK_ROWS * ROW_WORDS, LANES), lambda i: (i, 0)),
        out_shape=jax.ShapeDtypeStruct((E * ROW_WORDS, LANES), jnp.uint32),
    )(tab)


PACK_ROWS = 512


def _pack_table_kernel(t_ref, o_ref):
    bits = lambda a: lax.bitcast_convert_type(a.astype(BF16).astype(F32), jnp.uint32)
    for s_ in range(ROW_WORDS):
        lo = bits(t_ref[:, (2 * s_) * LANES:(2 * s_ + 1) * LANES])
        hi = bits(t_ref[:, (2 * s_ + 1) * LANES:(2 * s_ + 2) * LANES])
        o_ref[pl.ds(s_, PACK_ROWS, stride=ROW_WORDS), :] = (hi & jnp.uint32(0xFFFF0000)) | (lo >> 16)


def _stage_rows(idx_ref, tab_ref, stage_ref, t):
    for k in range(PEER_PICKS):
        off = pl.multiple_of(idx_ref[t, k], ROW_WORDS)
        stage_ref[k * ROW_WORDS:(k + 1) * ROW_WORDS, :] = tab_ref[pl.ds(off, ROW_WORDS), :]
    return pltpu.bitcast(stage_ref[...], BF16)


def _peer_act_kernel(idx_ref, x_ref, gate_ref, tab_ref, seg_mask_ref, group_ref, w_ref,
                     stage_ref, rows_ref):
    T = x_ref.shape[0]
    U = stage_ref.shape[0]

    sub = lax.broadcasted_iota(jnp.int32, (SUBLANES, PEER_PICKS * ROW_SEGS), 0)

    def tokens(g, carry):
        tile = jnp.zeros((SUBLANES, PEER_PICKS * ROW_SEGS), F32)
        for j in range(U):
            t = g * U + j
            sb = _stage_rows(idx_ref, tab_ref, stage_ref.at[j], t)
            xs = jnp.concatenate(_split_bf16(x_ref[t], 2), axis=0)
            r = lax.dot_general(xs, sb, NT_DIMS, preferred_element_type=F32)
            r = jnp.sum(r * seg_mask_ref[...], axis=0, keepdims=True)
            tile = jnp.where(sub == j, r, tile)
        rows_ref[g] = tile
        return carry

    lax.fori_loop(0, T // U, tokens, 0)
    rows = rows_ref[...].reshape(T, PEER_PICKS * ROW_SEGS)
    act = jnp.zeros((T, PEER_PICKS), F32)
    for piece in _split_bf16(rows, 3):
        act = act + jnp.dot(piece, group_ref[...], preferred_element_type=F32)
    w_ref[...] = gate_ref[...] * (0.5 * act * (1.0 + lax.erf(act * (2.0 ** -0.5))))


def _peer_out_kernel(idx_ref, w_ref, x_ref, g_ref, tab_ref, expand_ref, seg_mask_ref, f_ref, stage_ref):
    T = w_ref.shape[0]
    U = stage_ref.shape[0]

    def tokens(g, carry):
        w8 = w_ref[pl.ds(pl.multiple_of(g * U, U), U), :]
        hi, lo = _split_bf16(w8, 2)
        lhs = jnp.concatenate([jnp.broadcast_to(p[j:j + 1], (SUBLANES, PEER_PICKS))
                               for j in range(U) for p in (hi, lo)], axis=0)
        wrep = jnp.dot(lhs, expand_ref[...], preferred_element_type=F32)
        for j in range(U):
            t = g * U + j
            sb = _stage_rows(idx_ref, tab_ref, stage_ref.at[j], t)
            wsel = (wrep[j * 2 * SUBLANES:(j + 1) * 2 * SUBLANES] * seg_mask_ref[...]).astype(BF16)
            o = jnp.dot(wsel, sb, preferred_element_type=F32)
            f_ref[t] = x_ref[t] + g_ref[0] * (o[:SUBLANES] + o[SUBLANES:])
        return carry

    lax.fori_loop(0, T // U, tokens, 0)


def _peer_constants():
    cols = np.arange(PEER_PICKS * ROW_SEGS)
    seg_mask = (cols[None, :] % ROW_SEGS == np.arange(2 * SUBLANES)[:, None] % SUBLANES)
    group = (cols[:, None] // ROW_SEGS == np.arange(PEER_PICKS)[None, :])
    return (jnp.asarray(seg_mask, F32), jnp.asarray(group, BF16), jnp.asarray(group.T, BF16))


def peer_ffn(h, x, gate2, group_tokens, wq, sub_keys, u_packed, v_packed):
    N, D = h.shape
    T = PEER_GATHER_TOKENS
    eidx, gate = peer_topk(h, wq, sub_keys)
    seg_mask, group, expand = _peer_constants()
    rows3 = lambda a: a.reshape(a.shape[0], ROW_SEGS, LANES)
    offs = eidx.T * ROW_WORDS
    idx_spec = pl.BlockSpec((T, PEER_PICKS), lambda i: (i, 0), memory_space=pltpu.SMEM)
    tab_spec = pl.BlockSpec(u_packed.shape, lambda i: (0, 0), pipeline_mode=pl.Buffered(1))
    tok_spec = pl.BlockSpec((T, ROW_SEGS, LANES), lambda i: (i, 0, 0))
    const = lambda shape: pl.BlockSpec(shape, lambda i: (0, 0))
    params = pltpu.CompilerParams(vmem_limit_bytes=VMEM_LIMIT_BYTES)
    w = pl.pallas_call(
        _peer_act_kernel,
        grid=(N // T,),
        in_specs=[idx_spec, tok_spec,
                  pl.BlockSpec((T, PEER_PICKS), lambda i: (i, 0)),
                  tab_spec, const(seg_mask.shape), const(group.shape)],
        out_specs=pl.BlockSpec((T, PEER_PICKS), lambda i: (i, 0)),
        out_shape=jax.ShapeDtypeStruct((N, PEER_PICKS), F32),
        scratch_shapes=[pltpu.VMEM((PEER_ACT_UNROLL, PEER_PICKS * ROW_WORDS, LANES), jnp.uint32),
                        pltpu.VMEM((T // SUBLANES, SUBLANES, PEER_PICKS * ROW_SEGS), F32)],
        compiler_params=params,
    )(offs, rows3(h), gate.T, u_packed, seg_mask, group)
    out = pl.pallas_call(
        _peer_out_kernel,
        grid=(N // T,),
        in_specs=[idx_spec,
                  pl.BlockSpec((T, PEER_PICKS), lambda i: (i, 0)),
                  tok_spec,
                  pl.BlockSpec((1, ROW_SEGS, LANES), lambda i: (i // (group_tokens // T), 0, 0)),
                  tab_spec, const(expand.shape), const(seg_mask.shape)],
        out_specs=tok_spec,
        out_shape=jax.ShapeDtypeStruct((N, ROW_SEGS, LANES), F32),
        scratch_shapes=[pltpu.VMEM((SUBLANES, PEER_PICKS * ROW_WORDS, LANES), jnp.uint32)],
        compiler_params=params,
    )(offs, w, rows3(x), rows3(gate2), v_packed, expand, seg_mask)
    return out.reshape(N, D)


PROJ_TOKENS = 512
MOD_ROWS = SUBLANES
IN_ALIGNED = tuple(i for i, s_ in enumerate(IN_SIZES) if s_ % LANES == 0)
IN_SMALL = tuple(i for i, s_ in enumerate(IN_SIZES) if s_ % LANES)
IN_MXU_ONLY = (1, 2, 4, 12)


def _rms_modulate(x, gain, scale1p, shift):
    r = lax.rsqrt(jnp.mean(x * x, axis=-1, keepdims=True) + EPS)
    return (x * r * gain) * scale1p + shift


def _in_proj_kernel(x_ref, mod_ref, w_ref, *out_refs):
    mod = mod_ref[0]
    h = _rms_modulate(x_ref[...], mod[0:1], mod[1:2], mod[2:3])
    y = jnp.dot(h.astype(BF16), w_ref[...], preferred_element_type=F32)
    off = 0
    for o_ref in out_refs:
        o_ref[...] = y[:, off:off + o_ref.shape[1]].astype(o_ref.dtype)
        off += o_ref.shape[1]


def in_projection(x, mod, w_in, group_tokens):
    N, D = x.shape
    T = min(PROJ_TOKENS, group_tokens)
    starts = np.cumsum((0,) + IN_SIZES)
    group_cols = lambda i: np.arange(starts[i], starts[i + 1])
    small = [i for i in IN_SMALL if i != MLA_KR]
    n_small = sum(IN_SIZES[i] for i in small)
    lane_pad = lambda w_, lo, hi: jnp.pad(w_, ((0, 0), (lo, hi)))
    w_kr = w_in[:, group_cols(MLA_KR)]
    wp = jnp.concatenate(
        [w_in[:, np.concatenate([group_cols(i) for i in IN_ALIGNED])],
         lane_pad(w_in[:, np.concatenate([group_cols(i) for i in small])], 0, -n_small % LANES),
         lane_pad(w_kr, MLA_NOPE, LANES - MLA_NOPE - MLA_ROPE),
         lane_pad(w_kr[:, ROPE_PARTNER], MLA_NOPE, LANES - MLA_NOPE - MLA_ROPE)], axis=1).astype(BF16)
    widths = [IN_SIZES[i] for i in IN_ALIGNED] + [n_small + (-n_small % LANES), 2 * LANES]
    outs = pl.pallas_call(
        _in_proj_kernel,
        grid=(N // T,),
        in_specs=[pl.BlockSpec((T, D), lambda i: (i, 0)),
                  pl.BlockSpec((1, MOD_ROWS, D), lambda i: (i // (group_tokens // T), 0, 0)),
                  pl.BlockSpec(wp.shape, lambda i: (0, 0))],
        out_specs=[pl.BlockSpec((T, w_), lambda i: (i, 0)) for w_ in widths],
        out_shape=[jax.ShapeDtypeStruct((N, w_), BF16 if i in IN_MXU_ONLY else F32)
                   for i, w_ in zip(IN_ALIGNED + (None, None), widths)],
        compiler_params=pltpu.CompilerParams(vmem_limit_bytes=VMEM_LIMIT_BYTES),
    )(x, mod, wp)
    groups = dict(zip(IN_ALIGNED, outs[:-2]))
    off = 0
    for i in small:
        groups[i] = outs[-2][:, off:off + IN_SIZES[i]]
        off += IN_SIZES[i]
    groups[MLA_KR] = outs[-1]
    return [groups[i] for i in range(len(IN_SIZES))]


def _out_proj_kernel(ya_ref, hl_ref, mo_ref, yc_ref, yd_ref, x_ref, mod_ref, w_ref, xo_ref, h2_ref):
    yb = hl_ref[...] * jax.nn.sigmoid(mo_ref[...])
    y = jnp.concatenate([ya_ref[...], yb, yc_ref[...], yd_ref[...]], axis=-1).astype(BF16)
    mod = mod_ref[0]
    xn = x_ref[...] + mod[0:1] * jnp.dot(y, w_ref[...], preferred_element_type=F32)
    xo_ref[...] = xn
    h2_ref[...] = _rms_modulate(xn, mod[1:2], mod[2:3], mod[3:4])


def out_projection(ya, hl, mo, yc, yd, x, mod, w_out, group_tokens):
    N, D = x.shape
    T = min(PROJ_TOKENS, group_tokens)
    part = pl.BlockSpec((T, GROUP_WIDTH), lambda i: (i, 0))
    tok = pl.BlockSpec((T, D), lambda i: (i, 0))
    return pl.pallas_call(
        _out_proj_kernel,
        grid=(N // T,),
        in_specs=[part, part, part, part, part, tok,
                  pl.BlockSpec((1, MOD_ROWS, D), lambda i: (i // (group_tokens // T), 0, 0)),
                  pl.BlockSpec(w_out.shape, lambda i: (0, 0))],
        out_specs=[tok, tok],
        out_shape=[jax.ShapeDtypeStruct((N, D), F32)] * 2,
        compiler_params=pltpu.CompilerParams(vmem_limit_bytes=VMEM_LIMIT_BYTES),
    )(ya, hl, mo, yc, yd, x, mod, w_out.astype(BF16))


def _adaln_kernel(c_ref, w_ref, b_ref, o_ref):
    c = c_ref[...]
    a = c * jax.nn.sigmoid(c)
    o_ref[...] = jnp.dot(a.astype(BF16), w_ref[...].astype(BF16), preferred_element_type=F32) + b_ref[...]


def adaln_linear(c, w_ada, b_ada):
    R, D = c.shape
    rows = -R % SUBLANES + R
    out = pl.pallas_call(
        _adaln_kernel,
        grid=(w_ada.shape[1] // D,),
        in_specs=[pl.BlockSpec((rows, D), lambda j: (0, 0)),
                  pl.BlockSpec((D, D), lambda j: (0, j)),
                  pl.BlockSpec((1, D), lambda j: (0, j))],
        out_specs=pl.BlockSpec((rows, D), lambda j: (0, j)),
        out_shape=jax.ShapeDtypeStruct((rows, w_ada.shape[1]), F32),
    )(jnp.pad(c, ((0, rows - R), (0, 0))), w_ada, b_ada[None])
    return out[:R]


def _mod_rows(*rows):
    m = jnp.stack([jnp.broadcast_to(r, rows[-1].shape) for r in rows], axis=1)
    return jnp.pad(m, ((0, 0), (0, MOD_ROWS - len(rows)), (0, 0)))


def hybrid_layer(x, xc, c, c_ctx, need_ctx, rope_mla, angs_swa,
                 norm1_g, norm2_g, w_ada, b_ada, w_in, na_rpb, ml_conv, ml_gate_b,
                 mla_q_norm, mla_w_uq, mla_kv_norm, mla_w_ukv, swa_sink, w_out,
                 peer_wq, peer_keys, peer_u, peer_v):
    B, T, D = x.shape
    Tc = xc.shape[1]
    H = GROUP_HEADS
    flat = lambda a: a.reshape(-1, a.shape[-1])
    ada = adaln_linear(jnp.concatenate([c, c_ctx[None]], axis=0), w_ada, b_ada)
    sh1, sc1, g1, sh2, sc2, g2 = jnp.split(ada[:B], 6, axis=-1)
    sh1c, sc1c, g1c, sh2c, sc2c, g2c = jnp.split(ada[B:], 6, axis=-1)
    lat = in_projection(flat(x), _mod_rows(norm1_g, 1.0 + sc1, sh1), w_in, T)
    cx = in_projection(flat(xc), _mod_rows(norm1_g, 1.0 + sc1c, sh1c), w_in, B * Tc)
    (na_q, na_k, na_v, ml_qk, ml_v, ml_o, ml_g,
     mla_cq, mla_ckv, mla_kr, sw_q, sw_k, sw_v) = [a.reshape(B, T, -1) for a in lat]
    (na_qc, na_kc, na_vc, ml_qkc, ml_vc, ml_oc, ml_gc,
     mla_cqc, mla_ckvc, mla_krc, sw_qc, sw_kc, sw_vc) = [a.reshape(B, Tc, -1) for a in cx]
    attn_scale = HEAD_DIM ** -0.5
    mla_scale = (MLA_NOPE + MLA_ROPE) ** -0.5
    kc_a, vc_a = heads(na_kc, H), heads(na_vc, H)
    y_a = neighbourhood_attention(na_q, na_k, na_v, na_kc, na_vc, na_rpb)
    h_lat, h_ctx = mlstm_mixer((ml_qk, ml_v, ml_g), (ml_qkc, ml_vc, ml_gc), ml_conv, ml_gate_b)
    no_rope = (jnp.ones((Tc, LANES), F32), jnp.zeros((Tc, LANES), F32))
    q_m, k_m, v_m = [a.reshape(B, T, -1) for a in
                     mla_qkv(flat(mla_cq), flat(mla_ckv), flat(mla_kr), mla_q_norm, mla_w_uq, mla_kv_norm, mla_w_ukv,
                             *rope_mla)]
    qc_m, kc_m, vc_m = [a.reshape(B, Tc, -1) for a in
                        mla_qkv(flat(mla_cqc), flat(mla_ckvc), flat(mla_krc), mla_q_norm, mla_w_uq, mla_kv_norm,
                                mla_w_ukv, *no_rope)]
    y_c = dense_attention(q_m, jnp.concatenate([kc_m, k_m], axis=1), jnp.concatenate([vc_m, v_m], axis=1),
                          mla_scale, MLA_V)
    kc_d, vc_d = heads(sw_kc, SWA_KV_HEADS), heads(sw_vc, SWA_KV_HEADS)
    y_d = window_attention(rope_2d(heads(sw_q, H), angs_swa), rope_2d(heads(sw_k, SWA_KV_HEADS), angs_swa),
                           heads(sw_v, SWA_KV_HEADS), kc_d, vc_d, swa_sink)
    x2, h2 = out_projection(flat(y_a), h_lat.reshape(B * T, GROUP_WIDTH), flat(ml_o), flat(y_c), flat(y_d),
                            flat(x), _mod_rows(g1, norm2_g, 1.0 + sc2, sh2), w_out, T)
    u_packed, v_packed = pack_expert_table(peer_u), pack_expert_table(peer_v)
    x = peer_ffn(h2, x2, g2, T, peer_wq, peer_keys, u_packed, v_packed).reshape(B, T, D)
    if not need_ctx:
        return x, None
    xc2, h2c = out_projection(flat(ctx_attn(heads(na_qc, H), kc_a, vc_a, attn_scale)),
                              h_ctx.reshape(B * Tc, GROUP_WIDTH), flat(ml_oc),
                              ctx_attn(heads(qc_m, H), heads(kc_m, H), heads(vc_m, H), mla_scale)
                              .reshape(B, Tc, H, LANES)[..., :MLA_V].reshape(B * Tc, GROUP_WIDTH),
                              flat(ctx_attn(heads(sw_qc, H), kc_d, vc_d, attn_scale, swa_sink)),
                              flat(xc), _mod_rows(g1c, norm2_g, 1.0 + sc2c, sh2c), w_out, B * Tc)
    xc = peer_ffn(h2c, xc2, g2c, B * Tc, peer_wq, peer_keys, u_packed, v_packed).reshape(B, Tc, D)
    return x, xc


def _final_rmsnorm_kernel(x_ref, g_ref, o_ref):
    x = x_ref[...]
    o_ref[...] = x * lax.rsqrt(jnp.mean(x * x, axis=-1, keepdims=True) + EPS) * g_ref[...]


def final_rmsnorm(x, g):
    B, T, D = x.shape
    rows = 1024
    xf = x.reshape(B * T, D)
    out = pl.pallas_call(
        _final_rmsnorm_kernel,
        grid=(B * T // rows,),
        in_specs=[pl.BlockSpec((rows, D), lambda i: (i, 0)), pl.BlockSpec((1, D), lambda i: (0, 0))],
        out_specs=pl.BlockSpec((rows, D), lambda i: (i, 0)),
        out_shape=jax.ShapeDtypeStruct((B * T, D), x.dtype),
    )(xf, g.reshape(1, D))
    return out.reshape(B, T, D)


def kernel(x, c, ctx, c_ctx, norm1_g, norm2_g, w_ada, b_ada, w_in, na_rpb, ml_conv, ml_gate_b,
           mla_q_norm, mla_w_uq, mla_kv_norm, mla_w_ukv, swa_sink, w_out,
           peer_wq, peer_keys, peer_u, peer_v, final_norm_g):
    T = x.shape[1]
    rope_mla = mla_rope_tables(T)
    angs_swa = axial_angles(T, HEAD_DIM)
    xc = ctx
    for l in range(DEPTH):
        x, xc = hybrid_layer(x, xc, c, c_ctx, l < DEPTH - 1, rope_mla, angs_swa,
                             norm1_g[l], norm2_g[l], w_ada[l], b_ada[l], w_in[l], na_rpb[l],
                             ml_conv[l], ml_gate_b[l], mla_q_norm[l], mla_w_uq[l], mla_kv_norm[l],
                             mla_w_ukv[l], swa_sink[l], w_out[l], peer_wq[l], peer_keys[l],
                             peer_u[l], peer_v[l])
    return final_rmsnorm(x, final_norm_g)
```

```python
import functools

import jax
import jax.numpy as jnp
from jax import lax
import numpy as np
from jax.experimental import pallas as pl
from jax.experimental.pallas import tpu as pltpu

D_MODEL = 1024
DEPTH = 2

CTX_LEN = 256
GRID_W = 64
N_MIXERS = 4
MIX_WIDTH = D_MODEL
GROUP_WIDTH = MIX_WIDTH // N_MIXERS
GROUP_HEADS = 4
HEAD_DIM = GROUP_WIDTH // GROUP_HEADS
NA_ROWS = 8
NA_COLS = 16
ML_CHUNK = 64
MLA_Q_RANK = 256
MLA_KV_RANK = 128
MLA_NOPE = 64
MLA_ROPE = 32
MLA_V = 64
SWA_KV_HEADS = 2
SWA_WINDOW = 128
ATTN_BLOCK = 128
PEER_HEADS = 8
PEER_NKEYS = 128
PEER_DKEY = 128
PEER_TOPK = 16
ROPE_BASE = 10000.0
EPS = 1e-6
IN_SIZES = (GROUP_WIDTH, GROUP_WIDTH, GROUP_WIDTH,
            2 * GROUP_WIDTH, GROUP_WIDTH, GROUP_WIDTH, 4 * GROUP_HEADS,
            MLA_Q_RANK, MLA_KV_RANK, MLA_ROPE,
            GROUP_WIDTH, SWA_KV_HEADS * HEAD_DIM, SWA_KV_HEADS * HEAD_DIM)
F32 = jnp.float32


def heads(a, h):
    return a.reshape(a.shape[:-1] + (h, a.shape[-1] // h))


def axial_angles(T, rot_dim):
    t = jnp.arange(T)
    row = (t // GRID_W).astype(F32)
    col = (t % GRID_W).astype(F32)
    half = rot_dim // 2
    inv = 1.0 / (ROPE_BASE ** (jnp.arange(0, half, 2, dtype=F32) / half))
    return row[:, None] * inv, col[:, None] * inv


def rope_1d(x, ang):
    cos = jnp.cos(ang)[None, :, None, :]
    sin = jnp.sin(ang)[None, :, None, :]
    x1, x2 = jnp.split(x.astype(F32), 2, axis=-1)
    return jnp.concatenate([x1 * cos - x2 * sin, x1 * sin + x2 * cos], axis=-1)


def rope_2d(x, angs):
    xr, xc = jnp.split(x, 2, axis=-1)
    return jnp.concatenate([rope_1d(xr, angs[0]), rope_1d(xc, angs[1])], axis=-1).astype(x.dtype)


NT_DIMS = (((1,), (1,)), ((), ()))


def _ctx_attn_kernel(q_ref, k_ref, v_ref, sink_ref, o_ref, *, scale, use_sink):
    s = lax.dot_general(q_ref[0, 0].astype(BF16), k_ref[0, 0].astype(BF16), NT_DIMS,
                        preferred_element_type=F32) * scale
    m = jnp.max(s, axis=-1, keepdims=True)
    if use_sink:
        sink = sink_ref[pl.program_id(1)]
        m = jnp.maximum(m, sink)
    p = jnp.exp(s - m)
    l = jnp.sum(p, axis=-1, keepdims=True)
    if use_sink:
        l = l + jnp.exp(sink - m)
    o_ref[0, 0] = jnp.dot(p.astype(BF16), v_ref[0, 0].astype(BF16), preferred_element_type=F32) / l


def ctx_attn(q, k, v, scale, sink=None):
    B, Tc, H, _ = q.shape
    rep = H // k.shape[2]
    hm = lambda a: jnp.swapaxes(a, 1, 2)
    q, k, v = hm(q), hm(jnp.repeat(k, rep, axis=2)), hm(jnp.repeat(v, rep, axis=2))
    blk = lambda a: pl.BlockSpec((1, 1, Tc, a.shape[-1]), lambda b, h: (b, h, 0, 0))
    out = pl.pallas_call(
        functools.partial(_ctx_attn_kernel, scale=scale, use_sink=sink is not None),
        grid=(B, H),
        in_specs=[blk(q), blk(k), blk(v), pl.BlockSpec(memory_space=pltpu.SMEM)],
        out_specs=blk(v),
        out_shape=jax.ShapeDtypeStruct(v.shape, F32),
    )(q, k, v, jnp.zeros((H,), F32) if sink is None else sink.astype(F32))
    return jnp.swapaxes(out, 1, 2).reshape(B, Tc, -1)
NA_SPAN = NA_ROWS * GRID_W


def _head_mask(width):
    rows = lax.broadcasted_iota(jnp.int32, (GROUP_HEADS * width, GROUP_WIDTH), 0) // width
    cols = lax.broadcasted_iota(jnp.int32, (GROUP_HEADS * width, GROUP_WIDTH), 1) // HEAD_DIM
    return (rows == cols).astype(F32)


def _na_kernel(q_ref, k_ref, v_ref, kc_ref, vc_ref, bias_ref, o_ref):
    r = pl.program_id(1)
    rows = pl.num_programs(1)
    rs = jnp.clip(r - NA_ROWS // 2, 0, rows - NA_ROWS)
    start = pl.multiple_of(rs * GRID_W, GRID_W)
    kw = k_ref[0, pl.ds(start, NA_SPAN), :]
    vw = v_ref[0, pl.ds(start, NA_SPAN), :]
    hm = _head_mask(GRID_W)
    q = q_ref[0] * (HEAD_DIM ** -0.5)
    q4 = (jnp.concatenate([q] * GROUP_HEADS, axis=0) * hm).astype(BF16)
    s_loc = lax.dot_general(q4, kw, NT_DIMS, preferred_element_type=F32) + bias_ref[rs - r + NA_ROWS - 1]
    s_ctx = lax.dot_general(q4, kc_ref[0], NT_DIMS, preferred_element_type=F32)
    m = jnp.maximum(jnp.max(s_loc, axis=-1, keepdims=True), jnp.max(s_ctx, axis=-1, keepdims=True))
    p_loc = jnp.exp(s_loc - m)
    p_ctx = jnp.exp(s_ctx - m)
    l = jnp.sum(p_loc, axis=-1, keepdims=True) + jnp.sum(p_ctx, axis=-1, keepdims=True)
    o = (jnp.dot(p_loc.astype(BF16), vw, preferred_element_type=F32)
         + jnp.dot(p_ctx.astype(BF16), vc_ref[0], preferred_element_type=F32)) * (hm / l)
    o_ref[0] = sum(o[h * GRID_W:(h + 1) * GRID_W] for h in range(GROUP_HEADS))


def _na_bias_table(rpb):
    c = np.arange(GRID_W)
    col_start = np.clip(c - NA_COLS // 2, 0, GRID_W - NA_COLS)
    valid = (c[None, :] >= col_start[:, None]) & (c[None, :] < col_start[:, None] + NA_COLS)
    dc = np.clip(c[None, :] - c[:, None] + NA_COLS - 1, 0, 2 * NA_COLS - 2)
    dr = np.arange(NA_ROWS)[:, None] + np.arange(NA_ROWS)[None, :]
    t = rpb.astype(F32)[:, dr][..., dc]
    t = jnp.where(valid[None, None, None], t, -jnp.inf)
    return jnp.transpose(t, (1, 0, 3, 2, 4)).reshape(NA_ROWS, GROUP_HEADS * GRID_W, NA_SPAN)


def neighbourhood_attention(q, k, v, kc, vc, rpb):
    B, T, C = q.shape
    rows = T // GRID_W
    n_ctx = kc.shape[1]
    bias = _na_bias_table(rpb)
    full = lambda n: pl.BlockSpec((1, n, C), lambda b, r: (b, 0, 0))
    return pl.pallas_call(
        _na_kernel,
        grid=(B, rows),
        in_specs=[pl.BlockSpec((1, GRID_W, C), lambda b, r: (b, r, 0)),
                  full(T), full(T), full(n_ctx), full(n_ctx),
                  pl.BlockSpec(bias.shape, lambda b, r: (0, 0, 0))],
        out_specs=pl.BlockSpec((1, GRID_W, C), lambda b, r: (b, r, 0)),
        out_shape=jax.ShapeDtypeStruct((B, T, C), F32),
        compiler_params=pltpu.CompilerParams(vmem_limit_bytes=VMEM_LIMIT_BYTES),
    )(q, k.astype(BF16), v.astype(BF16), kc.astype(BF16), vc.astype(BF16), bias)


def short_conv(a, w):
    T = a.shape[1]
    pad = w.shape[0] // 2
    ap = jnp.pad(a, ((0, 0), (pad, pad), (0, 0)))
    out = ap[:, :T] * w[0]
    for j in range(1, w.shape[0]):
        out = out + ap[:, j:j + T] * w[j]
    return out


ML_CHUNKS_PER_STEP = CTX_LEN // ML_CHUNK


def _bmm(a, b, contract):
    return lax.dot_general(a.astype(BF16), b.astype(BF16), (contract, ((0,), (0,))),
                           preferred_element_type=F32)


def _mlstm_chunk(qt, kt, vt, irow, brow, state, backward):
    L = ML_CHUNK
    C, nrow, m = state
    row = lax.broadcasted_iota(jnp.int32, (1, L, L), 1)
    col = lax.broadcasted_iota(jnp.int32, (1, L, L), 2)
    seen = (row <= col) if backward else (row >= col)
    eye = row == col

    def as_col(r):
        return jnp.sum(jnp.where(eye, r, 0.0), axis=2, keepdims=True)

    blast = brow[:, :, 0:1] if backward else brow[:, :, L - 1:L]
    rrow = brow - irow
    bcol = as_col(brow)
    d_log = jnp.where(seen, bcol - rrow, -jnp.inf)
    inter = bcol + m
    m_t = jnp.maximum(inter, jnp.max(d_log, axis=2, keepdims=True))
    w = jnp.exp(d_log - m_t)
    a = jnp.exp(inter - m_t)
    s = _bmm(qt, kt, ((2,), (2,))) * w
    num = _bmm(s, vt, ((2,), (1,))) + a * _bmm(qt, C, ((2,), (1,)))
    den = jnp.sum(s, axis=2, keepdims=True) + a * jnp.sum(qt * nrow, axis=2, keepdims=True)
    h = num / jnp.maximum(jnp.abs(den), jnp.exp(-m_t))
    g = blast - rrow
    m_new = jnp.maximum(blast + m, jnp.max(g, axis=2, keepdims=True))
    kw = kt * as_col(jnp.exp(g - m_new))
    decay = jnp.exp(blast + m - m_new)
    C = decay * C + _bmm(jnp.swapaxes(kw, 1, 2), vt, ((2,), (1,)))
    nrow = decay * nrow + jnp.sum(kw, axis=1, keepdims=True)
    return h, (C, nrow, m_new)


def _mlstm_kernel(qf_ref, kf_ref, vf_ref, if_ref, bf_ref, qb_ref, kb_ref, vb_ref, ib_ref, bb_ref,
                  hf_ref, hb_ref, c_ref, n_ref, m_ref):
    N, L = qf_ref.shape[0], ML_CHUNK

    @pl.when(pl.program_id(0) == 0)
    def _():
        c_ref[...] = jnp.zeros(c_ref.shape, F32)
        n_ref[...] = jnp.zeros(n_ref.shape, F32)
        m_ref[...] = jnp.zeros(m_ref.shape, F32)

    fwd = (c_ref[:N], n_ref[:N], m_ref[:N])
    bwd = (c_ref[N:], n_ref[N:], m_ref[N:])
    for c in range(ML_CHUNKS_PER_STEP):
        rows = slice(c * L, (c + 1) * L)
        h, fwd = _mlstm_chunk(qf_ref[:, rows, :], kf_ref[:, rows, :], vf_ref[:, rows, :],
                              if_ref[:, 0, c:c + 1, :], bf_ref[:, 0, c:c + 1, :], fwd, False)
        hf_ref[:, rows, :] = h
        cb = ML_CHUNKS_PER_STEP - 1 - c
        rows = slice(cb * L, (cb + 1) * L)
        h, bwd = _mlstm_chunk(qb_ref[:, rows, :], kb_ref[:, rows, :], vb_ref[:, rows, :],
                              ib_ref[:, 0, cb:cb + 1, :], bb_ref[:, 0, cb:cb + 1, :], bwd, True)
        hb_ref[:, rows, :] = h
    for i, ref in enumerate((c_ref, n_ref, m_ref)):
        ref[:N] = fwd[i]
        ref[N:] = bwd[i]


def mlstm_scan(q, k, v, gates_f, gates_b, n_ctx):
    B, T, H, d = q.shape
    CB, L = ML_CHUNKS_PER_STEP, ML_CHUNK
    assert n_ctx == CB * L and T % (CB * L) == 0
    N, steps = B * H, T // (CB * L)
    hm = lambda a: jnp.swapaxes(a, 1, 2).reshape(N, T, d)
    gates = lambda a: jnp.swapaxes(a, 1, 2).reshape(N, steps, CB, L)
    chunked = lambda a: a.reshape(B, T // L, L, H)
    b_f = jnp.cumsum(chunked(gates_f[1]), axis=2).reshape(B, T, H)
    b_b = lax.cumsum(chunked(gates_b[1]), axis=2, reverse=True).reshape(B, T, H)
    back = lambda j: jnp.where(j == 0, 0, steps - j)
    seq_f = pl.BlockSpec((N, CB * L, d), lambda j: (0, j, 0))
    seq_b = pl.BlockSpec((N, CB * L, d), lambda j: (0, back(j), 0))
    gate_f = pl.BlockSpec((N, 1, CB, L), lambda j: (0, j, 0, 0))
    gate_b = pl.BlockSpec((N, 1, CB, L), lambda j: (0, back(j), 0, 0))
    qh, kh, vh = hm(q), hm(k), hm(v)
    hf, hb = pl.pallas_call(
        _mlstm_kernel,
        grid=(steps,),
        in_specs=[seq_f, seq_f, seq_f, gate_f, gate_f, seq_b, seq_b, seq_b, gate_b, gate_b],
        out_specs=[seq_f, seq_b],
        out_shape=[jax.ShapeDtypeStruct((N, T, d), F32)] * 2,
        scratch_shapes=[pltpu.VMEM((2 * N, d, d), F32), pltpu.VMEM((2 * N, 1, d), F32),
                        pltpu.VMEM((2 * N, 1, 1), F32)],
        compiler_params=pltpu.CompilerParams(vmem_limit_bytes=VMEM_LIMIT_BYTES),
    )(qh, kh, vh, gates(gates_f[0]), gates(b_f), qh, kh, vh, gates(gates_b[0]), gates(b_b))
    return jnp.swapaxes((hf + hb).reshape(B, H, T, d), 1, 2)


def mlstm_prep(qk, v, gates, conv_w, gate_b):
    qk = jax.nn.silu(short_conv(qk, conv_w))
    q, k = jnp.split(qk, 2, axis=-1)
    g = (gates + gate_b).astype(F32)
    i_f, f_f, i_b, f_b = jnp.split(g, 4, axis=-1)
    return (heads(q, GROUP_HEADS) * HEAD_DIM ** -0.5, heads(k, GROUP_HEADS), heads(v, GROUP_HEADS),
            (i_f, jax.nn.log_sigmoid(f_f), i_b, jax.nn.log_sigmoid(f_b)))


def mlstm_mixer(lat, ctx, conv_w, gate_b):
    ql, kl, vl, gl = mlstm_prep(lat[0], lat[1], lat[2], conv_w, gate_b)
    qc, kc, vc, gc = mlstm_prep(ctx[0], ctx[1], ctx[2], conv_w, gate_b)
    Tc = qc.shape[1]
    cat = lambda c_, l_: jnp.concatenate([c_, l_], axis=1)
    h = mlstm_scan(cat(qc, ql), cat(kc, kl), cat(vc, vl),
                   (cat(gc[0], gl[0]), cat(gc[1], gl[1])), (cat(gc[2], gl[2]), cat(gc[3], gl[3])), Tc)
    return h[:, Tc:], h[:, :Tc]


MLA_KR = 9
ROPE_PARTNER = np.concatenate([np.arange(q_, q_ + MLA_ROPE // 4) for q_ in
                               (MLA_ROPE // 4, 0, 3 * MLA_ROPE // 4, MLA_ROPE // 2)])


def mla_rope_tables(T):
    ang_r, ang_c = axial_angles(T, MLA_ROPE)
    cos = jnp.concatenate([jnp.cos(ang_r)] * 2 + [jnp.cos(ang_c)] * 2, axis=1)
    sin = jnp.concatenate([-jnp.sin(ang_r), jnp.sin(ang_r), -jnp.sin(ang_c), jnp.sin(ang_c)], axis=1)
    pad = LANES - MLA_NOPE - MLA_ROPE
    return (jnp.concatenate([jnp.ones((T, MLA_NOPE), F32), cos, jnp.zeros((T, pad), F32)], axis=1),
            jnp.concatenate([jnp.zeros((T, MLA_NOPE), F32), sin, jnp.zeros((T, pad), F32)], axis=1))


def _mla_qkv_kernel(cq_ref, ckv_ref, kr_ref, cos_ref, sin_ref, qn_ref, kvn_ref, wq_ref, wkv_ref,
                    q_ref, k_ref, v_ref):
    def up(x, g, w_ref):
        y = x * lax.rsqrt(jnp.mean(x * x, axis=-1, keepdims=True) + EPS) * g
        return jnp.dot(y.astype(BF16), w_ref[...], preferred_element_type=F32)

    W = GROUP_HEADS * LANES
    per_head = lambda a: jnp.concatenate([a] * GROUP_HEADS, axis=1)
    cos, sin = cos_ref[...], sin_ref[...]
    q2 = up(cq_ref[...], qn_ref[...], wq_ref)
    q_ref[...] = (q2[:, :W] * per_head(cos) + q2[:, W:] * per_head(sin)).astype(BF16)
    kv = up(ckv_ref[...], kvn_ref[...], wkv_ref)
    kr = kr_ref[...]
    k_rope = kr[:, :LANES] * cos + kr[:, LANES:] * sin
    k_ref[...] = (kv[:, :W] + per_head(k_rope)).astype(BF16)
    v_ref[...] = kv[:, W:].astype(BF16)


def mla_qkv(cq, ckv, kr2, q_norm, w_uq, kv_norm, w_ukv, cos, sin):
    N = cq.shape[0]
    P = cos.shape[0]
    tm = min(PROJ_TOKENS, P)
    H, dqk = GROUP_HEADS, MLA_NOPE + MLA_ROPE
    blocks = lambda w_, lo, n: jnp.pad(w_.reshape(w_.shape[0], H, -1)[:, :, lo:lo + n],
                                      ((0, 0), (0, 0), (0, LANES - n))).reshape(w_.shape[0], H * LANES)
    wq = w_uq.reshape(w_uq.shape[0], H, dqk)
    wq_partner = jnp.pad(wq[:, :, MLA_NOPE + ROPE_PARTNER], ((0, 0), (0, 0), (MLA_NOPE, LANES - dqk)))
    wq2 = jnp.concatenate([blocks(w_uq, 0, dqk), wq_partner.reshape(-1, H * LANES)], axis=1).astype(BF16)
    wkv2 = jnp.concatenate([blocks(w_ukv, 0, MLA_NOPE), blocks(w_ukv, MLA_NOPE, MLA_V)], axis=1).astype(BF16)
    tok = lambda w_: pl.BlockSpec((tm, w_), lambda i: (i, 0))
    pos = pl.BlockSpec((tm, LANES), lambda i: (i % (P // tm), 0))
    const = lambda a: pl.BlockSpec(a.shape, lambda i: (0, 0))
    args = (cq, ckv, kr2, cos, sin, q_norm[None], kv_norm[None], wq2, wkv2)
    out = jax.ShapeDtypeStruct((N, H * LANES), BF16)
    return pl.pallas_call(
        _mla_qkv_kernel,
        grid=(N // tm,),
        in_specs=[tok(cq.shape[1]), tok(ckv.shape[1]), tok(kr2.shape[1]), pos, pos] + [const(a) for a in args[5:]],
        out_specs=[tok(H * LANES)] * 3,
        out_shape=[out] * 3,
    )(*args)


LOG2_E = 1.4426950408889634
DENSE_Q_TILE = 1024
DENSE_Q_SUB = 256
DENSE_Q_UNROLL = 4
DENSE_K_TILE_MAX = 16640


def _dense_attn_kernel(q_ref, k_ref, v_ref, o_ref, m_ref, l_ref, acc_ref, *, scale, dv):
    h, j = pl.program_id(2), pl.program_id(3)

    @pl.when(j == 0)
    def _():
        m_ref[...] = jnp.full(m_ref.shape, -jnp.inf, F32)
        l_ref[...] = jnp.zeros(l_ref.shape, F32)
        acc_ref[...] = jnp.zeros(acc_ref.shape, F32)

    def rows(i, carry):
        for u in range(DENSE_Q_UNROLL):
            r = pl.ds(pl.multiple_of((i * DENSE_Q_UNROLL + u) * DENSE_Q_SUB, DENSE_Q_SUB), DENSE_Q_SUB)
            s = lax.dot_general(q_ref[0, r, :], k_ref[0], NT_DIMS,
                                preferred_element_type=F32) * (scale * LOG2_E)
            m_prev = m_ref[r, :]
            m_new = jnp.maximum(m_prev, jnp.max(s, axis=-1, keepdims=True))
            alpha = jnp.exp2(m_prev - m_new)
            p = jnp.exp2(s - m_new)
            l_ref[r, :] = alpha * l_ref[r, :] + jnp.sum(p, axis=-1, keepdims=True)
            acc_ref[r, :] = alpha * acc_ref[r, :] + jnp.dot(p.astype(BF16), v_ref[0],
                                                            preferred_element_type=F32)
            m_ref[r, :] = m_new
        return carry

    lax.fori_loop(0, q_ref.shape[1] // (DENSE_Q_SUB * DENSE_Q_UNROLL), rows, 0)

    for hh in range(o_ref.shape[2] // dv):
        @pl.when((j == pl.num_programs(3) - 1) & (h == hh))
        def _():
            o_ref[0, :, hh * dv:(hh + 1) * dv] = (acc_ref[...] / l_ref[...])[:, :dv]


def dense_attention(q, k_all, v_all, scale, dv):
    B, T, C = q.shape
    H, NK = C // LANES, k_all.shape[1]
    tq = min(DENSE_Q_TILE, T)
    tk = max(t for t in range(LANES, DENSE_K_TILE_MAX + 1, LANES) if NK % t == 0)
    return pl.pallas_call(
        functools.partial(_dense_attn_kernel, scale=scale, dv=dv),
        grid=(B, T // tq, H, NK // tk),
        in_specs=[pl.BlockSpec((1, tq, LANES), lambda b, i, h, j: (b, i, h)),
                  pl.BlockSpec((1, tk, LANES), lambda b, i, h, j: (b, j, h)),
                  pl.BlockSpec((1, tk, LANES), lambda b, i, h, j: (b, j, h))],
        out_specs=pl.BlockSpec((1, tq, H * dv), lambda b, i, h, j: (b, i, 0)),
        out_shape=jax.ShapeDtypeStruct((B, T, H * dv), F32),
        scratch_shapes=[pltpu.VMEM((tq, 1), F32), pltpu.VMEM((tq, 1), F32), pltpu.VMEM((tq, LANES), F32)],
        compiler_params=pltpu.CompilerParams(vmem_limit_bytes=VMEM_LIMIT_BYTES),
    )(q, k_all, v_all)


SWA_SPAN = ATTN_BLOCK + 2 * SWA_WINDOW


def _swa_kernel(q_ref, k_ref, v_ref, kc_ref, vc_ref, sink_ref, o_ref):
    n = pl.program_id(1)
    T = k_ref.shape[1]
    start = pl.multiple_of(jnp.clip(n * ATTN_BLOCK - SWA_WINDOW, 0, T - SWA_SPAN), ATTN_BLOCK)
    kw = k_ref[0, pl.ds(start, SWA_SPAN), :]
    vw = v_ref[0, pl.ds(start, SWA_SPAN), :]
    hm = _head_mask(ATTN_BLOCK)
    q = q_ref[0] * (HEAD_DIM ** -0.5)
    q4 = (jnp.concatenate([q] * GROUP_HEADS, axis=0) * hm).astype(BF16)
    rows = GROUP_HEADS * ATTN_BLOCK
    q_pos = n * ATTN_BLOCK + lax.broadcasted_iota(jnp.int32, (rows, SWA_SPAN), 0) % ATTN_BLOCK
    k_pos = start + lax.broadcasted_iota(jnp.int32, (rows, SWA_SPAN), 1)
    s_loc = lax.dot_general(q4, kw, NT_DIMS, preferred_element_type=F32)
    s_loc = jnp.where(jnp.abs(q_pos - k_pos) <= SWA_WINDOW, s_loc, -jnp.inf)
    s_ctx = lax.dot_general(q4, kc_ref[0], NT_DIMS, preferred_element_type=F32)
    sink = sink_ref[...]
    m = jnp.maximum(jnp.maximum(jnp.max(s_loc, axis=-1, keepdims=True),
                                jnp.max(s_ctx, axis=-1, keepdims=True)), sink)
    p_loc = jnp.exp(s_loc - m)
    p_ctx = jnp.exp(s_ctx - m)
    l = jnp.sum(p_loc, axis=-1, keepdims=True) + jnp.sum(p_ctx, axis=-1, keepdims=True) + jnp.exp(sink - m)
    o = (jnp.dot(p_loc.astype(BF16), vw, preferred_element_type=F32)
         + jnp.dot(p_ctx.astype(BF16), vc_ref[0], preferred_element_type=F32)) * (hm / l)
    o_ref[0] = sum(o[h * ATTN_BLOCK:(h + 1) * ATTN_BLOCK] for h in range(GROUP_HEADS))


def window_attention(q, k, v, kc, vc, sink):
    B, T, H, d = q.shape
    G = H // k.shape[2]
    n_ctx = kc.shape[1]
    C = H * d
    rep = lambda a: jnp.repeat(a, G, axis=2).reshape(a.shape[0], a.shape[1], C).astype(BF16)
    sink_rows = jnp.repeat(sink.astype(F32), ATTN_BLOCK).reshape(H * ATTN_BLOCK, 1)
    full = lambda n: pl.BlockSpec((1, n, C), lambda b, i: (b, 0, 0))
    return pl.pallas_call(
        _swa_kernel,
        grid=(B, T // ATTN_BLOCK),
        in_specs=[pl.BlockSpec((1, ATTN_BLOCK, C), lambda b, i: (b, i, 0)),
                  full(T), full(T), full(n_ctx), full(n_ctx),
                  pl.BlockSpec(sink_rows.shape, lambda b, i: (0, 0))],
        out_specs=pl.BlockSpec((1, ATTN_BLOCK, C), lambda b, i: (b, i, 0)),
        out_shape=jax.ShapeDtypeStruct((B, T, C), F32),
        compiler_params=pltpu.CompilerParams(vmem_limit_bytes=VMEM_LIMIT_BYTES),
    )(q.reshape(B, T, C), rep(k), rep(v), rep(kc), rep(vc), sink_rows)


BF16 = jnp.bfloat16
LANES = 128
SUBLANES = 8
ROW_SEGS = D_MODEL // LANES
ROW_WORDS = ROW_SEGS // 2
PEER_PICKS = PEER_HEADS * PEER_TOPK
PEER_TOPK_TOKENS = 1024
PEER_GATHER_TOKENS = 512
PEER_ACT_UNROLL = SUBLANES
VMEM_LIMIT_BYTES = 56 * 1024 * 1024


def _split_bf16(x, parts):
    out = []
    for _ in range(parts):
        p = x.astype(BF16)
        out.append(p)
        x = x - p.astype(F32)
    return out


def _topk_rows(s, k, payload=None):
    n = s.shape[0]
    iota = lax.broadcasted_iota(jnp.int32, s.shape, 0)
    vals, picked = [], []
    for _ in range(k):
        m = jnp.max(s, axis=0, keepdims=True)
        i = jnp.min(jnp.where(s == m, iota, n), axis=0, keepdims=True)
        hit = iota == i
        vals.append(m)
        picked.append(i if payload is None else jnp.max(jnp.where(hit, payload, -1), axis=0, keepdims=True))
        s = jnp.where(hit, -jnp.inf, s)
    return jnp.concatenate(vals, axis=0), jnp.concatenate(picked, axis=0)


def _peer_topk_kernel(x_ref, wq_ref, keys_ref, eidx_ref, gate_ref):
    xb = x_ref[...].astype(BF16)
    q = jnp.dot(xb, wq_ref[...], preferred_element_type=F32)
    nt = (((1,), (1,)), ((), ()))
    sv, si = [], []
    for p in range(2):
        qp = q[:, p * PEER_DKEY:(p + 1) * PEER_DKEY].astype(BF16)
        s = lax.dot_general(keys_ref[0, p], qp, nt, preferred_element_type=F32)
        v_, i_ = _topk_rows(s, PEER_TOPK)
        sv.append(v_)
        si.append(i_)
    cs, ce = [], []
    half = PEER_TOPK // 2
    for a in range(half):
        nb = PEER_TOPK if a == 0 else half
        cs.append(sv[0][a:a + 1] + sv[1][:nb])
        ce.append(si[0][a:a + 1] * PEER_NKEYS + si[1][:nb])
    cs.append(sv[0][half:] + sv[1][0:1])
    ce.append(si[0][half:] * PEER_NKEYS + si[1][0:1])
    cand_s = jnp.concatenate(cs, axis=0)
    cand_e = jnp.concatenate(ce, axis=0)
    fs, eidx = _topk_rows(cand_s, PEER_TOPK, cand_e)
    ex = jnp.exp(fs - fs[0:1])
    eidx_ref[0] = eidx
    gate_ref[0] = ex / jnp.sum(ex, axis=0, keepdims=True)


def peer_topk(h, wq, sub_keys):
    N, D = h.shape
    T = min(PEER_TOPK_TOKENS, N)
    wqb = wq.astype(BF16)
    kb = sub_keys.astype(BF16)
    eidx, gate = pl.pallas_call(
        _peer_topk_kernel,
        grid=(N // T, PEER_HEADS),
        in_specs=[pl.BlockSpec((T, D), lambda i, h_: (i, 0)),
                  pl.BlockSpec((D, 2 * PEER_DKEY), lambda i, h_: (0, h_)),
                  pl.BlockSpec((1, 2, PEER_NKEYS, PEER_DKEY), lambda i, h_: (h_, 0, 0, 0))],
        out_specs=[pl.BlockSpec((1, PEER_TOPK, T), lambda i, h_: (h_, 0, i)),
                   pl.BlockSpec((1, PEER_TOPK, T), lambda i, h_: (h_, 0, i))],
        out_shape=[jax.ShapeDtypeStruct((PEER_HEADS, PEER_TOPK, N), jnp.int32),
                   jax.ShapeDtypeStruct((PEER_HEADS, PEER_TOPK, N), F32)],
        compiler_params=pltpu.CompilerParams(vmem_limit_bytes=VMEM_LIMIT_BYTES),
    )(h, wqb, kb)
    return eidx.reshape(PEER_PICKS, N), gate.reshape(PEER_PICKS, N)


def pack_expert_table(tab):
    E, D = tab.shape
    return pl.pallas_call(
        _pack_table_kernel,
        grid=(E // PACK_ROWS,),
        in_specs=[pl.BlockSpec((PACK_ROWS, D), lambda i: (i, 0))],
        out_specs=pl.BlockSpec((PACK_ROWS * ROW_WORDS, LANES), lambda i: (i, 0)),
        out_shape=jax.ShapeDtypeStruct((E * ROW_WORDS, LANES), jnp.uint32),
    )(tab)


PACK_ROWS = 512


def _pack_table_kernel(t_ref, o_ref):
    bits = lambda a: lax.bitcast_convert_type(a.astype(BF16).astype(F32), jnp.uint32)
    for s_ in range(ROW_WORDS):
        lo = bits(t_ref[:, (2 * s_) * LANES:(2 * s_ + 1) * LANES])
        hi = bits(t_ref[:, (2 * s_ + 1) * LANES:(2 * s_ + 2) * LANES])
        o_ref[pl.ds(s_, PACK_ROWS, stride=ROW_WORDS), :] = (hi & jnp.uint32(0xFFFF0000)) | (lo >> 16)


def _stage_rows(idx_ref, tab_ref, stage_ref, t):
    for k in range(PEER_PICKS):
        off = pl.multiple_of(idx_ref[t, k], ROW_WORDS)
        stage_ref[k * ROW_WORDS:(k + 1) * ROW_WORDS, :] = tab_ref[pl.ds(off, ROW_WORDS), :]
    return pltpu.bitcast(stage_ref[...], BF16)


def _peer_act_kernel(idx_ref, x_ref, gate_ref, tab_ref, seg_mask_ref, group_ref, w_ref,
                     stage_ref, rows_ref):
    T = x_ref.shape[0]
    U = stage_ref.shape[0]

    sub = lax.broadcasted_iota(jnp.int32, (SUBLANES, PEER_PICKS * ROW_SEGS), 0)

    def tokens(g, carry):
        tile = jnp.zeros((SUBLANES, PEER_PICKS * ROW_SEGS), F32)
        for j in range(U):
            t = g * U + j
            sb = _stage_rows(idx_ref, tab_ref, stage_ref.at[j], t)
            xs = jnp.concatenate(_split_bf16(x_ref[t], 2), axis=0)
            r = lax.dot_general(xs, sb, NT_DIMS, preferred_element_type=F32)
            r = jnp.sum(r * seg_mask_ref[...], axis=0, keepdims=True)
            tile = jnp.where(sub == j, r, tile)
        rows_ref[g] = tile
        return carry

    lax.fori_loop(0, T // U, tokens, 0)
    rows = rows_ref[...].reshape(T, PEER_PICKS * ROW_SEGS)
    act = jnp.zeros((T, PEER_PICKS), F32)
    for piece in _split_bf16(rows, 3):
        act = act + jnp.dot(piece, group_ref[...], preferred_element_type=F32)
    w_ref[...] = gate_ref[...] * (0.5 * act * (1.0 + lax.erf(act * (2.0 ** -0.5))))


def _peer_out_kernel(idx_ref, w_ref, x_ref, g_ref, tab_ref, expand_ref, seg_mask_ref, f_ref, stage_ref):
    T = w_ref.shape[0]
    U = stage_ref.shape[0]

    def tokens(g, carry):
        w8 = w_ref[pl.ds(pl.multiple_of(g * U, U), U), :]
        hi, lo = _split_bf16(w8, 2)
        lhs = jnp.concatenate([jnp.broadcast_to(p[j:j + 1], (SUBLANES, PEER_PICKS))
                               for j in range(U) for p in (hi, lo)], axis=0)
        wrep = jnp.dot(lhs, expand_ref[...], preferred_element_type=F32)
        for j in range(U):
            t = g * U + j
            sb = _stage_rows(idx_ref, tab_ref, stage_ref.at[j], t)
            wsel = (wrep[j * 2 * SUBLANES:(j + 1) * 2 * SUBLANES] * seg_mask_ref[...]).astype(BF16)
            o = jnp.dot(wsel, sb, preferred_element_type=F32)
            f_ref[t] = x_ref[t] + g_ref[0] * (o[:SUBLANES] + o[SUBLANES:])
        return carry

    lax.fori_loop(0, T // U, tokens, 0)


def _peer_constants():
    cols = np.arange(PEER_PICKS * ROW_SEGS)
    seg_mask = (cols[None, :] % ROW_SEGS == np.arange(2 * SUBLANES)[:, None] % SUBLANES)
    group = (cols[:, None] // ROW_SEGS == np.arange(PEER_PICKS)[None, :])
    return (jnp.asarray(seg_mask, F32), jnp.asarray(group, BF16), jnp.asarray(group.T, BF16))


def peer_ffn(h, x, gate2, group_tokens, wq, sub_keys, u_packed, v_packed):
    N, D = h.shape
    T = PEER_GATHER_TOKENS
    eidx, gate = peer_topk(h, wq, sub_keys)
    seg_mask, group, expand = _peer_constants()
    rows3 = lambda a: a.reshape(a.shape[0], ROW_SEGS, LANES)
    offs = eidx.T * ROW_WORDS
    idx_spec = pl.BlockSpec((T, PEER_PICKS), lambda i: (i, 0), memory_space=pltpu.SMEM)
    tab_spec = pl.BlockSpec(u_packed.shape, lambda i: (0, 0), pipeline_mode=pl.Buffered(1))
    tok_spec = pl.BlockSpec((T, ROW_SEGS, LANES), lambda i: (i, 0, 0))
    const = lambda shape: pl.BlockSpec(shape, lambda i: (0, 0))
    params = pltpu.CompilerParams(vmem_limit_bytes=VMEM_LIMIT_BYTES)
    w = pl.pallas_call(
        _peer_act_kernel,
        grid=(N // T,),
        in_specs=[idx_spec, tok_spec,
                  pl.BlockSpec((T, PEER_PICKS), lambda i: (i, 0)),
                  tab_spec, const(seg_mask.shape), const(group.shape)],
        out_specs=pl.BlockSpec((T, PEER_PICKS), lambda i: (i, 0)),
        out_shape=jax.ShapeDtypeStruct((N, PEER_PICKS), F32),
        scratch_shapes=[pltpu.VMEM((PEER_ACT_UNROLL, PEER_PICKS * ROW_WORDS, LANES), jnp.uint32),
                        pltpu.VMEM((T // SUBLANES, SUBLANES, PEER_PICKS * ROW_SEGS), F32)],
        compiler_params=params,
    )(offs, rows3(h), gate.T, u_packed, seg_mask, group)
    out = pl.pallas_call(
        _peer_out_kernel,
        grid=(N // T,),
        in_specs=[idx_spec,
                  pl.BlockSpec((T, PEER_PICKS), lambda i: (i, 0)),
                  tok_spec,
                  pl.BlockSpec((1, ROW_SEGS, LANES), lambda i: (i // (group_tokens // T), 0, 0)),
                  tab_spec, const(expand.shape), const(seg_mask.shape)],
        out_specs=tok_spec,
        out_shape=jax.ShapeDtypeStruct((N, ROW_SEGS, LANES), F32),
        scratch_shapes=[pltpu.VMEM((SUBLANES, PEER_PICKS * ROW_WORDS, LANES), jnp.uint32)],
        compiler_params=params,
    )(offs, w, rows3(x), rows3(gate2), v_packed, expand, seg_mask)
    return out.reshape(N, D)


PROJ_TOKENS = 512
MOD_ROWS = SUBLANES
IN_ALIGNED = tuple(i for i, s_ in enumerate(IN_SIZES) if s_ % LANES == 0)
IN_SMALL = tuple(i for i, s_ in enumerate(IN_SIZES) if s_ % LANES)
IN_MXU_ONLY = (1, 2, 4, 12)


def _rms_modulate(x, gain, scale1p, shift):
    r = lax.rsqrt(jnp.mean(x * x, axis=-1, keepdims=True) + EPS)
    return (x * r * gain) * scale1p + shift


def _in_proj_kernel(x_ref, mod_ref, w_ref, *out_refs):
    mod = mod_ref[0]
    h = _rms_modulate(x_ref[...], mod[0:1], mod[1:2], mod[2:3])
    y = jnp.dot(h.astype(BF16), w_ref[...], preferred_element_type=F32)
    off = 0
    for o_ref in out_refs:
        o_ref[...] = y[:, off:off + o_ref.shape[1]].astype(o_ref.dtype)
        off += o_ref.shape[1]


def in_projection(x, mod, w_in, group_tokens):
    N, D = x.shape
    T = min(PROJ_TOKENS, group_tokens)
    starts = np.cumsum((0,) + IN_SIZES)
    group_cols = lambda i: np.arange(starts[i], starts[i + 1])
    small = [i for i in IN_SMALL if i != MLA_KR]
    n_small = sum(IN_SIZES[i] for i in small)
    lane_pad = lambda w_, lo, hi: jnp.pad(w_, ((0, 0), (lo, hi)))
    w_kr = w_in[:, group_cols(MLA_KR)]
    wp = jnp.concatenate(
        [w_in[:, np.concatenate([group_cols(i) for i in IN_ALIGNED])],
         lane_pad(w_in[:, np.concatenate([group_cols(i) for i in small])], 0, -n_small % LANES),
         lane_pad(w_kr, MLA_NOPE, LANES - MLA_NOPE - MLA_ROPE),
         lane_pad(w_kr[:, ROPE_PARTNER], MLA_NOPE, LANES - MLA_NOPE - MLA_ROPE)], axis=1).astype(BF16)
    widths = [IN_SIZES[i] for i in IN_ALIGNED] + [n_small + (-n_small % LANES), 2 * LANES]
    outs = pl.pallas_call(
        _in_proj_kernel,
        grid=(N // T,),
        in_specs=[pl.BlockSpec((T, D), lambda i: (i, 0)),
                  pl.BlockSpec((1, MOD_ROWS, D), lambda i: (i // (group_tokens // T), 0, 0)),
                  pl.BlockSpec(wp.shape, lambda i: (0, 0))],
        out_specs=[pl.BlockSpec((T, w_), lambda i: (i, 0)) for w_ in widths],
        out_shape=[jax.ShapeDtypeStruct((N, w_), BF16 if i in IN_MXU_ONLY else F32)
                   for i, w_ in zip(IN_ALIGNED + (None, None), widths)],
        compiler_params=pltpu.CompilerParams(vmem_limit_bytes=VMEM_LIMIT_BYTES),
    )(x, mod, wp)
    groups = dict(zip(IN_ALIGNED, outs[:-2]))
    off = 0
    for i in small:
        groups[i] = outs[-2][:, off:off + IN_SIZES[i]]
        off += IN_SIZES[i]
    groups[MLA_KR] = outs[-1]
    return [groups[i] for i in range(len(IN_SIZES))]


def _out_proj_kernel(ya_ref, hl_ref, mo_ref, yc_ref, yd_ref, x_ref, mod_ref, w_ref, xo_ref, h2_ref):
    yb = hl_ref[...] * jax.nn.sigmoid(mo_ref[...])
    y = jnp.concatenate([ya_ref[...], yb, yc_ref[...], yd_ref[...]], axis=-1).astype(BF16)
    mod = mod_ref[0]
    xn = x_ref[...] + mod[0:1] * jnp.dot(y, w_ref[...], preferred_element_type=F32)
    xo_ref[...] = xn
    h2_ref[...] = _rms_modulate(xn, mod[1:2], mod[2:3], mod[3:4])


def out_projection(ya, hl, mo, yc, yd, x, mod, w_out, group_tokens):
    N, D = x.shape
    T = min(PROJ_TOKENS, group_tokens)
    part = pl.BlockSpec((T, GROUP_WIDTH), lambda i: (i, 0))
    tok = pl.BlockSpec((T, D), lambda i: (i, 0))
    return pl.pallas_call(
        _out_proj_kernel,
        grid=(N // T,),
        in_specs=[part, part, part, part, part, tok,
                  pl.BlockSpec((1, MOD_ROWS, D), lambda i: (i // (group_tokens // T), 0, 0)),
                  pl.BlockSpec(w_out.shape, lambda i: (0, 0))],
        out_specs=[tok, tok],
        out_shape=[jax.ShapeDtypeStruct((N, D), F32)] * 2,
        compiler_params=pltpu.CompilerParams(vmem_limit_bytes=VMEM_LIMIT_BYTES),
    )(ya, hl, mo, yc, yd, x, mod, w_out.astype(BF16))


def _adaln_kernel(c_ref, w_ref, b_ref, o_ref):
    c = c_ref[...]
    a = c * jax.nn.sigmoid(c)
    o_ref[...] = jnp.dot(a.astype(BF16), w_ref[...].astype(BF16), preferred_element_type=F32) + b_ref[...]


def adaln_linear(c, w_ada, b_ada):
    R, D = c.shape
    rows = -R % SUBLANES + R
    out = pl.pallas_call(
        _adaln_kernel,
        grid=(w_ada.shape[1] // D,),
        in_specs=[pl.BlockSpec((rows, D), lambda j: (0, 0)),
                  pl.BlockSpec((D, D), lambda j: (0, j)),
                  pl.BlockSpec((1, D), lambda j: (0, j))],
        out_specs=pl.BlockSpec((rows, D), lambda j: (0, j)),
        out_shape=jax.ShapeDtypeStruct((rows, w_ada.shape[1]), F32),
    )(jnp.pad(c, ((0, rows - R), (0, 0))), w_ada, b_ada[None])
    return out[:R]


def _mod_rows(*rows):
    m = jnp.stack([jnp.broadcast_to(r, rows[-1].shape) for r in rows], axis=1)
    return jnp.pad(m, ((0, 0), (0, MOD_ROWS - len(rows)), (0, 0)))


def hybrid_layer(x, xc, c, c_ctx, need_ctx, rope_mla, angs_swa,
                 norm1_g, norm2_g, w_ada, b_ada, w_in, na_rpb, ml_conv, ml_gate_b,
                 mla_q_norm, mla_w_uq, mla_kv_norm, mla_w_ukv, swa_sink, w_out,
                 peer_wq, peer_keys, peer_u, peer_v):
    B, T, D = x.shape
    Tc = xc.shape[1]
    H = GROUP_HEADS
    flat = lambda a: a.reshape(-1, a.shape[-1])
    ada = adaln_linear(jnp.concatenate([c, c_ctx[None]], axis=0), w_ada, b_ada)
    sh1, sc1, g1, sh2, sc2, g2 = jnp.split(ada[:B], 6, axis=-1)
    sh1c, sc1c, g1c, sh2c, sc2c, g2c = jnp.split(ada[B:], 6, axis=-1)
    lat = in_projection(flat(x), _mod_rows(norm1_g, 1.0 + sc1, sh1), w_in, T)
    cx = in_projection(flat(xc), _mod_rows(norm1_g, 1.0 + sc1c, sh1c), w_in, B * Tc)
    (na_q, na_k, na_v, ml_qk, ml_v, ml_o, ml_g,
     mla_cq, mla_ckv, mla_kr, sw_q, sw_k, sw_v) = [a.reshape(B, T, -1) for a in lat]
    (na_qc, na_kc, na_vc, ml_qkc, ml_vc, ml_oc, ml_gc,
     mla_cqc, mla_ckvc, mla_krc, sw_qc, sw_kc, sw_vc) = [a.reshape(B, Tc, -1) for a in cx]
    attn_scale = HEAD_DIM ** -0.5
    mla_scale = (MLA_NOPE + MLA_ROPE) ** -0.5
    kc_a, vc_a = heads(na_kc, H), heads(na_vc, H)
    y_a = neighbourhood_attention(na_q, na_k, na_v, na_kc, na_vc, na_rpb)
    h_lat, h_ctx = mlstm_mixer((ml_qk, ml_v, ml_g), (ml_qkc, ml_vc, ml_gc), ml_conv, ml_gate_b)
    no_rope = (jnp.ones((Tc, LANES), F32), jnp.zeros((Tc, LANES), F32))
    q_m, k_m, v_m = [a.reshape(B, T, -1) for a in
                     mla_qkv(flat(mla_cq), flat(mla_ckv), flat(mla_kr), mla_q_norm, mla_w_uq, mla_kv_norm, mla_w_ukv,
                             *rope_mla)]
    qc_m, kc_m, vc_m = [a.reshape(B, Tc, -1) for a in
                        mla_qkv(flat(mla_cqc), flat(mla_ckvc), flat(mla_krc), mla_q_norm, mla_w_uq, mla_kv_norm,
                                mla_w_ukv, *no_rope)]
    y_c = dense_attention(q_m, jnp.concatenate([kc_m, k_m], axis=1), jnp.concatenate([vc_m, v_m], axis=1),
                          mla_scale, MLA_V)
    kc_d, vc_d = heads(sw_kc, SWA_KV_HEADS), heads(sw_vc, SWA_KV_HEADS)
    y_d = window_attention(rope_2d(heads(sw_q, H), angs_swa), rope_2d(heads(sw_k, SWA_KV_HEADS), angs_swa),
                           heads(sw_v, SWA_KV_HEADS), kc_d, vc_d, swa_sink)
    x2, h2 = out_projection(flat(y_a), h_lat.reshape(B * T, GROUP_WIDTH), flat(ml_o), flat(y_c), flat(y_d),
                            flat(x), _mod_rows(g1, norm2_g, 1.0 + sc2, sh2), w_out, T)
    u_packed, v_packed = pack_expert_table(peer_u), pack_expert_table(peer_v)
    x = peer_ffn(h2, x2, g2, T, peer_wq, peer_keys, u_packed, v_packed).reshape(B, T, D)
    if not need_ctx:
        return x, None
    xc2, h2c = out_projection(flat(ctx_attn(heads(na_qc, H), kc_a, vc_a, attn_scale)),
                              h_ctx.reshape(B * Tc, GROUP_WIDTH), flat(ml_oc),
                              ctx_attn(heads(qc_m, H), heads(kc_m, H), heads(vc_m, H), mla_scale)
                              .reshape(B, Tc, H, LANES)[..., :MLA_V].reshape(B * Tc, GROUP_WIDTH),
                              flat(ctx_attn(heads(sw_qc, H), kc_d, vc_d, attn_scale, swa_sink)),
                              flat(xc), _mod_rows(g1c, norm2_g, 1.0 + sc2c, sh2c), w_out, B * Tc)
    xc = peer_ffn(h2c, xc2, g2c, B * Tc, peer_wq, peer_keys, u_packed, v_packed).reshape(B, Tc, D)
    return x, xc


def _final_rmsnorm_kernel(x_ref, g_ref, o_ref):
    x = x_ref[...]
    o_ref[...] = x * lax.rsqrt(jnp.mean(x * x, axis=-1, keepdims=True) + EPS) * g_ref[...]


def final_rmsnorm(x, g):
    B, T, D = x.shape
    rows = 1024
    xf = x.reshape(B * T, D)
    out = pl.pallas_call(
        _final_rmsnorm_kernel,
        grid=(B * T // rows,),
        in_specs=[pl.BlockSpec((rows, D), lambda i: (i, 0)), pl.BlockSpec((1, D), lambda i: (0, 0))],
        out_specs=pl.BlockSpec((rows, D), lambda i: (i, 0)),
        out_shape=jax.ShapeDtypeStruct((B * T, D), x.dtype),
    )(xf, g.reshape(1, D))
    return out.reshape(B, T, D)


def kernel(x, c, ctx, c_ctx, norm1_g, norm2_g, w_ada, b_ada, w_in, na_rpb, ml_conv, ml_gate_b,
           mla_q_norm, mla_w_uq, mla_kv_norm, mla_w_ukv, swa_sink, w_out,
           peer_wq, peer_keys, peer_u, peer_v, final_norm_g):
    T = x.shape[1]
    rope_mla = mla_rope_tables(T)
    angs_swa = axial_angles(T, HEAD_DIM)
    xc = ctx
    for l in range(DEPTH):
        x, xc = hybrid_layer(x, xc, c, c_ctx, l < DEPTH - 1, rope_mla, angs_swa,
                             norm1_g[l], norm2_g[l], w_ada[l], b_ada[l], w_in[l], na_rpb[l],
                             ml_conv[l], ml_gate_b[l], mla_q_norm[l], mla_w_uq[l], mla_kv_norm[l],
                             mla_w_ukv[l], swa_sink[l], w_out[l], peer_wq[l], peer_keys[l],
                             peer_u[l], peer_v[l])
    return final_rmsnorm(x, final_norm_g)
```
